```python
import jax
import jax.numpy as jnp
from jax import lax
import numpy as np

D_MODEL = 1024
BATCH = 8
SEQ = 8192
DEPTH = 2

N_META = 16
D_MIX = D_MODEL
CONV_WIDTH = D_MIX // 4
HG_WIDTH = D_MIX // 2
HG_HEAD_DIM = 128
HG_HEADS = HG_WIDTH // HG_HEAD_DIM
POOL_WIDTH = D_MIX - CONV_WIDTH - HG_WIDTH
POOL_WINDOWS = (2, 4, 8, 16)
POOL_GROUPS = len(POOL_WINDOWS)
POOL_GROUP_DIM = POOL_WIDTH // POOL_GROUPS
SHORT_CONV_K = 3
FFN_CONV_K = 3
D_FF = 2816
CHUNK = 64
D_IN = 3 * CONV_WIDTH + 4 * HG_WIDTH + POOL_WIDTH
ALPHA = (2 * DEPTH) ** 0.25
BETA = (8 * DEPTH) ** -0.25
LN_EPS = 1e-5
RMS_EPS = 1e-6
F_FLOOR = 1e-30
SPLIT_SIZES = (CONV_WIDTH,) * 3 + (HG_WIDTH,) * 4 + (POOL_WIDTH,)
SPLIT_IDX = tuple(int(s) for s in np.cumsum(SPLIT_SIZES)[:-1])

kernel_name = "hymba_conv_hgrn2_pool_deepnorm"


def causal_dwconv(x, w, b=None):
    K = w.shape[-1]
    L = x.shape[1]
    xp = jnp.pad(x, ((0, 0), (K - 1, 0), (0, 0)))
    y = xp[:, 0:L, :] * w[:, 0]
    for k in range(1, K):
        y = y + xp[:, k:k + L, :] * w[:, k]
    if b is not None:
        y = y + b
    return y


def layer_norm(x, g, b):
    xf = x.astype(jnp.float32)
    mu = jnp.mean(xf, axis=-1, keepdims=True)
    var = jnp.mean(jnp.square(xf - mu), axis=-1, keepdims=True)
    return ((xf - mu) * lax.rsqrt(var + LN_EPS) * g + b).astype(x.dtype)


def short_conv_mixer(bg, cg, v, w_conv):
    return bg * causal_dwconv(cg * v, w_conv)


def multiscale_pool_mixer(v, w_pool, pool_scale):
    B, L, _ = v.shape
    vf = v.astype(jnp.float32)
    c = jnp.pad(jnp.cumsum(vf, axis=1), ((0, 0), (1, 0), (0, 0)))
    t = jnp.arange(L)
    outs = []
    for gi, win in enumerate(POOL_WINDOWS):
        lo, hi = gi * POOL_GROUP_DIM, (gi + 1) * POOL_GROUP_DIM
        cg = c[..., lo:hi]
        prev = jnp.pad(cg, ((0, 0), (win, 0), (0, 0)))[:, 1:L + 1]
        count = jnp.minimum(t + 1, win).astype(jnp.float32)[:, None]
        outs.append((cg[:, 1:] - prev) / count - vf[..., lo:hi])
    d = jnp.stack(outs, axis=2)
    y = jnp.einsum('blgc,gcd->blgd', d, w_pool).reshape(B, L, POOL_WIDTH)
    return (y * pool_scale).astype(v.dtype)


def hgrn2_mixer(q, fz, i, gz, lb, g_norm):
    B, L, _ = q.shape
    f32 = jnp.float32
    fz = fz.astype(f32)
    lb = lb.astype(f32)
    sig = jax.nn.sigmoid(fz)
    f = lb + (1.0 - lb) * sig
    log_f = jnp.log(jnp.maximum(f, F_FLOOR))
    k = (1.0 - lb) * (1.0 - sig)
    pad = CHUNK - N_META

    def to_chunks(a):
        a = jnp.pad(a.astype(f32), ((0, 0), (pad, 0), (0, 0)))
        n = a.shape[1] // CHUNK
        return a.reshape(B, n, CHUNK, HG_HEADS, HG_HEAD_DIM).transpose(1, 0, 3, 2, 4)

    qc = to_chunks(q.astype(f32) * (HG_HEAD_DIM ** -0.5))
    kc, ic, gc = to_chunks(k), to_chunks(i), to_chunks(log_f)
    mask = jnp.tril(jnp.ones((CHUNK, CHUNK), dtype=bool))[:, :, None]

    def step(S, xs):
        qb, kb, ib, gb = xs
        G = jnp.cumsum(gb, axis=2)
        o_inter = jnp.einsum('bhtd,bhde->bhte', qb * jnp.exp(G), S)
        diff = G[:, :, :, None, :] - G[:, :, None, :, :]
        decay = jnp.where(mask, jnp.exp(jnp.where(mask, diff, 0.0)), 0.0)
        A = jnp.einsum('bhtd,bhsd,bhtsd->bhts', qb, kb, decay)
        o_intra = jnp.einsum('bhts,bhse->bhte', A, ib)
        G_last = G[:, :, -1:, :]
        S_new = jnp.exp(G_last[:, :, 0, :, None]) * S + jnp.einsum(
            'bhsd,bhse->bhde', kb * jnp.exp(G_last - G), ib)
        return S_new, o_inter + o_intra

    S0 = jnp.zeros((B, HG_HEADS, HG_HEAD_DIM, HG_HEAD_DIM), f32)
    _, o = lax.scan(step, S0, (qc, kc, ic, gc))
    o = o.transpose(1, 0, 3, 2, 4).reshape(B, -1, HG_HEADS, HG_HEAD_DIM)[:, pad:]
    o = o * lax.rsqrt(jnp.mean(jnp.square(o), axis=-1, keepdims=True) + RMS_EPS) * g_norm
    o = o * jax.nn.silu(gz.astype(f32).reshape(B, L, HG_HEADS, HG_HEAD_DIM))
    return o.reshape(B, L, HG_WIDTH).astype(q.dtype)


def hybrid_layer(x, lb, w_in, w_conv, w_pool, pool_scale, hg_norm_g, w_o, ln1_g, ln1_b,
                 w_up, w_ffn_conv, b_ffn_conv, w_down, ln2_g, ln2_b):
    h = x @ w_in
    cb, cc, cv, hq, hf, hi, hgate, pv = jnp.split(h, SPLIT_IDX, axis=-1)
    y_conv = short_conv_mixer(cb, cc, cv, w_conv)
    y_hg = hgrn2_mixer(hq, hf, hi, hgate, lb, hg_norm_g)
    y_pool = multiscale_pool_mixer(pv, w_pool, pool_scale)
    mix = jnp.concatenate([y_conv, y_hg, y_pool], axis=-1) @ w_o
    x = layer_norm(ALPHA * x + mix, ln1_g, ln1_b)
    u = causal_dwconv(x @ w_up, w_ffn_conv, b_ffn_conv)
    gate, val = jnp.split(u, 2, axis=-1)
    ffn = (jax.nn.silu(gate) * val) @ w_down
    return layer_norm(ALPHA * x + ffn, ln2_g, ln2_b)


def _fwd_setup_inputs(seed: int = 0) -> dict:
    key = jax.random.key(seed)
    ks = jax.random.split(key, 17)
    f32 = jnp.float32

    def nrm(k, shape):
        return jax.random.normal(k, shape, f32)

    col_scale = jnp.concatenate([
        jnp.ones((2 * CONV_WIDTH,), f32), jnp.full((CONV_WIDTH,), BETA, f32),
        jnp.ones((2 * HG_WIDTH,), f32), jnp.full((HG_WIDTH,), BETA, f32),
        jnp.ones((HG_WIDTH,), f32), jnp.full((POOL_WIDTH,), BETA, f32)])
    return {
        'x': nrm(ks[0], (BATCH, SEQ, D_MODEL)),
        'meta_tokens': nrm(ks[1], (N_META, D_MODEL)),
        'hg_lower_bounds': 0.1 * nrm(ks[2], (DEPTH, HG_WIDTH)),
        'w_in': nrm(ks[3], (DEPTH, D_MODEL, D_IN)) * (D_MODEL ** -0.5) * col_scale,
        'w_conv': nrm(ks[4], (DEPTH, CONV_WIDTH, SHORT_CONV_K)) * (SHORT_CONV_K ** -0.5),
        'w_pool': nrm(ks[5], (DEPTH, POOL_GROUPS, POOL_GROUP_DIM, POOL_GROUP_DIM)) * (POOL_GROUP_DIM ** -0.5),
        'pool_scale': 1.0 + 0.02 * nrm(ks[6], (DEPTH, POOL_WIDTH)),
        'hg_norm_g': 1.0 + 0.02 * nrm(ks[7], (DEPTH, HG_HEAD_DIM)),
        'w_o': nrm(ks[8], (DEPTH, D_MIX, D_MODEL)) * (D_MIX ** -0.5) * BETA,
        'ln1_g': 1.0 + 0.02 * nrm(ks[9], (DEPTH, D_MODEL)),
        'ln1_b': 0.02 * nrm(ks[10], (DEPTH, D_MODEL)),
        'w_up': nrm(ks[11], (DEPTH, D_MODEL, 2 * D_FF)) * (D_MODEL ** -0.5),
        'w_ffn_conv': nrm(ks[12], (DEPTH, 2 * D_FF, FFN_CONV_K)) * (FFN_CONV_K ** -0.5),
        'b_ffn_conv': 0.02 * nrm(ks[13], (DEPTH, 2 * D_FF)),
        'w_down': nrm(ks[14], (DEPTH, D_FF, D_MODEL)) * (D_FF ** -0.5) * BETA,
        'ln2_g': 1.0 + 0.02 * nrm(ks[15], (DEPTH, D_MODEL)),
        'ln2_b': 0.02 * nrm(ks[16], (DEPTH, D_MODEL)),
    }


def _fwd_reference(x, meta_tokens, hg_lower_bounds, w_in, w_conv, w_pool, pool_scale, hg_norm_g,
              w_o, ln1_g, ln1_b, w_up, w_ffn_conv, b_ffn_conv, w_down, ln2_g, ln2_b):
    B = x.shape[0]
    meta = jnp.broadcast_to(meta_tokens[None].astype(x.dtype), (B, N_META, D_MODEL))
    h = jnp.concatenate([meta, x], axis=1)
    p = jax.nn.softmax(hg_lower_bounds.astype(jnp.float32), axis=0)
    lbs = jnp.cumsum(p, axis=0) - p[0]
    for l in range(DEPTH):
        h = hybrid_layer(h, lbs[l], w_in[l], w_conv[l], w_pool[l], pool_scale[l], hg_norm_g[l],
                         w_o[l], ln1_g[l], ln1_b[l], w_up[l], w_ffn_conv[l], b_ffn_conv[l],
                         w_down[l], ln2_g[l], ln2_b[l])
    return h[:, N_META:]


import jax as _jax
import jax.numpy as _jnp

TWIN_FORMAT = 'train_step'
FWD_PARAMS = ['x', 'meta_tokens', 'hg_lower_bounds', 'w_in', 'w_conv', 'w_pool', 'pool_scale', 'hg_norm_g', 'w_o', 'ln1_g', 'ln1_b', 'w_up', 'w_ffn_conv', 'b_ffn_conv', 'w_down', 'ln2_g', 'ln2_b']
TWIN_WEIGHTS = ['meta_tokens', 'hg_lower_bounds', 'w_in', 'w_conv', 'w_pool', 'pool_scale', 'hg_norm_g', 'w_o', 'ln1_g', 'ln1_b', 'w_up', 'w_ffn_conv', 'b_ffn_conv', 'w_down', 'ln2_g', 'ln2_b']
TWIN_DIFF_INPUT = 'x'
TWIN_INPUTS = ['x', 'meta_tokens', 'hg_lower_bounds', 'w_in', 'w_conv', 'w_pool', 'pool_scale', 'hg_norm_g', 'w_o', 'ln1_g', 'ln1_b', 'w_up', 'w_ffn_conv', 'b_ffn_conv', 'w_down', 'ln2_g', 'ln2_b', 'loss_target', 'm_meta_tokens', 'm_hg_lower_bounds', 'm_w_in', 'm_w_conv', 'm_w_pool', 'm_pool_scale', 'm_hg_norm_g', 'm_w_o', 'm_ln1_g', 'm_ln1_b', 'm_w_up', 'm_w_ffn_conv', 'm_b_ffn_conv', 'm_w_down', 'm_ln2_g', 'm_ln2_b', 'v_meta_tokens', 'v_hg_lower_bounds', 'v_w_in', 'v_w_conv', 'v_w_pool', 'v_pool_scale', 'v_hg_norm_g', 'v_w_o', 'v_ln1_g', 'v_ln1_b', 'v_w_up', 'v_w_ffn_conv', 'v_b_ffn_conv', 'v_w_down', 'v_ln2_g', 'v_ln2_b']
TWIN_OUTPUTS = ['loss', 'grad_x', 'grad_meta_tokens', 'grad_hg_lower_bounds', 'grad_w_in', 'grad_w_conv', 'grad_w_pool', 'grad_pool_scale', 'grad_hg_norm_g', 'grad_w_o', 'grad_ln1_g', 'grad_ln1_b', 'grad_w_up', 'grad_w_ffn_conv', 'grad_b_ffn_conv', 'grad_w_down', 'grad_ln2_g', 'grad_ln2_b', 'delta_meta_tokens', 'delta_hg_lower_bounds', 'delta_w_in', 'delta_w_conv', 'delta_w_pool', 'delta_pool_scale', 'delta_hg_norm_g', 'delta_w_o', 'delta_ln1_g', 'delta_ln1_b', 'delta_w_up', 'delta_w_ffn_conv', 'delta_b_ffn_conv', 'delta_w_down', 'delta_ln2_g', 'delta_ln2_b', 'new_m_meta_tokens', 'new_m_hg_lower_bounds', 'new_m_w_in', 'new_m_w_conv', 'new_m_w_pool', 'new_m_pool_scale', 'new_m_hg_norm_g', 'new_m_w_o', 'new_m_ln1_g', 'new_m_ln1_b', 'new_m_w_up', 'new_m_w_ffn_conv', 'new_m_b_ffn_conv', 'new_m_w_down', 'new_m_ln2_g', 'new_m_ln2_b', 'new_v_meta_tokens', 'new_v_hg_lower_bounds', 'new_v_w_in', 'new_v_w_conv', 'new_v_w_pool', 'new_v_pool_scale', 'new_v_hg_norm_g', 'new_v_w_o', 'new_v_ln1_g', 'new_v_ln1_b', 'new_v_w_up', 'new_v_w_ffn_conv', 'new_v_b_ffn_conv', 'new_v_w_down', 'new_v_ln2_g', 'new_v_ln2_b']
TWIN_LEAF_KINDS = {'loss': 'loss', 'grad_x': 'grad_x', 'grad_meta_tokens': 'grad_w', 'grad_hg_lower_bounds': 'grad_w', 'grad_w_in': 'grad_w', 'grad_w_conv': 'grad_w', 'grad_w_pool': 'grad_w', 'grad_pool_scale': 'grad_w', 'grad_hg_norm_g': 'grad_w', 'grad_w_o': 'grad_w', 'grad_ln1_g': 'grad_w', 'grad_ln1_b': 'grad_w', 'grad_w_up': 'grad_w', 'grad_w_ffn_conv': 'grad_w', 'grad_b_ffn_conv': 'grad_w', 'grad_w_down': 'grad_w', 'grad_ln2_g': 'grad_w', 'grad_ln2_b': 'grad_w', 'delta_meta_tokens': 'delta_w', 'delta_hg_lower_bounds': 'delta_w', 'delta_w_in': 'delta_w', 'delta_w_conv': 'delta_w', 'delta_w_pool': 'delta_w', 'delta_pool_scale': 'delta_w', 'delta_hg_norm_g': 'delta_w', 'delta_w_o': 'delta_w', 'delta_ln1_g': 'delta_w', 'delta_ln1_b': 'delta_w', 'delta_w_up': 'delta_w', 'delta_w_ffn_conv': 'delta_w', 'delta_b_ffn_conv': 'delta_w', 'delta_w_down': 'delta_w', 'delta_ln2_g': 'delta_w', 'delta_ln2_b': 'delta_w', 'new_m_meta_tokens': 'new_m', 'new_m_hg_lower_bounds': 'new_m', 'new_m_w_in': 'new_m', 'new_m_w_conv': 'new_m', 'new_m_w_pool': 'new_m', 'new_m_pool_scale': 'new_m', 'new_m_hg_norm_g': 'new_m', 'new_m_w_o': 'new_m', 'new_m_ln1_g': 'new_m', 'new_m_ln1_b': 'new_m', 'new_m_w_up': 'new_m', 'new_m_w_ffn_conv': 'new_m', 'new_m_b_ffn_conv': 'new_m', 'new_m_w_down': 'new_m', 'new_m_ln2_g': 'new_m', 'new_m_ln2_b': 'new_m', 'new_v_meta_tokens': 'new_v', 'new_v_hg_lower_bounds': 'new_v', 'new_v_w_in': 'new_v', 'new_v_w_conv': 'new_v', 'new_v_w_pool': 'new_v', 'new_v_pool_scale': 'new_v', 'new_v_hg_norm_g': 'new_v', 'new_v_w_o': 'new_v', 'new_v_ln1_g': 'new_v', 'new_v_ln1_b': 'new_v', 'new_v_w_up': 'new_v', 'new_v_w_ffn_conv': 'new_v', 'new_v_b_ffn_conv': 'new_v', 'new_v_w_down': 'new_v', 'new_v_ln2_g': 'new_v', 'new_v_ln2_b': 'new_v'}


def _forward(args):
    return _fwd_reference(*[args[k] for k in FWD_PARAMS])


def _output_shape():
    def fwd():
        inp = _fwd_setup_inputs(0)
        return _fwd_reference(*[inp[k] for k in FWD_PARAMS])
    out = _jax.eval_shape(fwd)
    return out.shape, out.dtype

N_MICROBATCH = 1
ADAM_LR = 0.001
ADAM_B1 = 0.9
ADAM_B2 = 0.999
ADAM_EPS = 1e-08
ADAM_WD = 0.01
ADAM_STEP = 10
PER_EXAMPLE_BATCH_AXIS = {'x': 0, 'loss_target': 0}
SHARED_INPUTS = []
_WEIGHT_DTYPES = {'meta_tokens': _jnp.float32, 'hg_lower_bounds': _jnp.float32, 'w_in': _jnp.float32, 'w_conv': _jnp.float32, 'w_pool': _jnp.float32, 'pool_scale': _jnp.float32, 'hg_norm_g': _jnp.float32, 'w_o': _jnp.float32, 'ln1_g': _jnp.float32, 'ln1_b': _jnp.float32, 'w_up': _jnp.float32, 'w_ffn_conv': _jnp.float32, 'b_ffn_conv': _jnp.float32, 'w_down': _jnp.float32, 'ln2_g': _jnp.float32, 'ln2_b': _jnp.float32}
MOMENT_SCALE = {'meta_tokens': 2.277688e-03, 'hg_lower_bounds': 3.819721e-02, 'w_in': 7.578543e-02, 'w_conv': 4.697969e-02, 'w_pool': 4.078989e-02, 'pool_scale': 4.020581e-02, 'hg_norm_g': 1.088620e-01, 'w_o': 9.919394e-02, 'ln1_g': 2.059692e+00, 'ln1_b': 9.802581e-01, 'w_up': 3.420131e-02, 'w_ffn_conv': 3.410617e-02, 'b_ffn_conv': 4.081405e-02, 'w_down': 1.113161e-01, 'ln2_g': 4.529864e+01, 'ln2_b': 1.586235e+00}


def _to_microbatches(a, axis):
    t = _jnp.moveaxis(a, axis, 0)
    t = t.reshape((N_MICROBATCH, t.shape[0] // N_MICROBATCH) + t.shape[1:])
    return _jnp.moveaxis(t, 1, axis + 1)


def setup_inputs(seed: int = 0) -> dict:
    inp = _fwd_setup_inputs(seed)
    key = _jax.random.fold_in(_jax.random.key(seed), 7919)
    shape, _ = _output_shape()
    out = dict(inp)
    out["loss_target"] = _jax.random.normal(_jax.random.fold_in(key, 0), shape, _jnp.float32)
    for i, name in enumerate(TWIN_WEIGHTS):
        w = inp[name].astype(_jnp.float32)
        if MOMENT_SCALE is None:
            s = _jnp.sqrt(_jnp.mean(_jnp.square(w)) + 1e-30)
        else:
            s = MOMENT_SCALE[name]
        km, kv = _jax.random.split(_jax.random.fold_in(key, i + 1))
        out[name] = w
        out["m_" + name] = s * _jax.random.normal(km, w.shape, _jnp.float32)
        out["v_" + name] = (s * s) * _jax.random.uniform(kv, w.shape, _jnp.float32, 0.5, 1.5)
    if N_MICROBATCH > 1:
        for name, axis in PER_EXAMPLE_BATCH_AXIS.items():
            out[name] = _to_microbatches(out[name], axis)
    return {'x': out['x'], 'meta_tokens': out['meta_tokens'], 'hg_lower_bounds': out['hg_lower_bounds'], 'w_in': out['w_in'], 'w_conv': out['w_conv'], 'w_pool': out['w_pool'], 'pool_scale': out['pool_scale'], 'hg_norm_g': out['hg_norm_g'], 'w_o': out['w_o'], 'ln1_g': out['ln1_g'], 'ln1_b': out['ln1_b'], 'w_up': out['w_up'], 'w_ffn_conv': out['w_ffn_conv'], 'b_ffn_conv': out['b_ffn_conv'], 'w_down': out['w_down'], 'ln2_g': out['ln2_g'], 'ln2_b': out['ln2_b'], 'loss_target': out['loss_target'], 'm_meta_tokens': out['m_meta_tokens'], 'm_hg_lower_bounds': out['m_hg_lower_bounds'], 'm_w_in': out['m_w_in'], 'm_w_conv': out['m_w_conv'], 'm_w_pool': out['m_w_pool'], 'm_pool_scale': out['m_pool_scale'], 'm_hg_norm_g': out['m_hg_norm_g'], 'm_w_o': out['m_w_o'], 'm_ln1_g': out['m_ln1_g'], 'm_ln1_b': out['m_ln1_b'], 'm_w_up': out['m_w_up'], 'm_w_ffn_conv': out['m_w_ffn_conv'], 'm_b_ffn_conv': out['m_b_ffn_conv'], 'm_w_down': out['m_w_down'], 'm_ln2_g': out['m_ln2_g'], 'm_ln2_b': out['m_ln2_b'], 'v_meta_tokens': out['v_meta_tokens'], 'v_hg_lower_bounds': out['v_hg_lower_bounds'], 'v_w_in': out['v_w_in'], 'v_w_conv': out['v_w_conv'], 'v_w_pool': out['v_w_pool'], 'v_pool_scale': out['v_pool_scale'], 'v_hg_norm_g': out['v_hg_norm_g'], 'v_w_o': out['v_w_o'], 'v_ln1_g': out['v_ln1_g'], 'v_ln1_b': out['v_ln1_b'], 'v_w_up': out['v_w_up'], 'v_w_ffn_conv': out['v_w_ffn_conv'], 'v_b_ffn_conv': out['v_b_ffn_conv'], 'v_w_down': out['v_w_down'], 'v_ln2_g': out['v_ln2_g'], 'v_ln2_b': out['v_ln2_b']}


def _loss(weights, diff, rest, loss_target):
    with _jax.named_scope("forward"):
        args = {**rest, TWIN_DIFF_INPUT: diff, **{k: w.astype(_WEIGHT_DTYPES[k]) for k, w in weights.items()}}
        y = _forward(args)
    with _jax.named_scope("loss_head"):
        err = _jnp.square(y.astype(_jnp.float32) - loss_target)
        return 0.5 * _jnp.sum(_jnp.mean(err, axis=-1)) if err.ndim else 0.5 * err


def _adamw(w, g, m, v):
    m = ADAM_B1 * m + (1.0 - ADAM_B1) * g
    v = ADAM_B2 * v + (1.0 - ADAM_B2) * _jnp.square(g)
    m_hat = m / (1.0 - ADAM_B1 ** ADAM_STEP)
    v_hat = v / (1.0 - ADAM_B2 ** ADAM_STEP)
    delta = -ADAM_LR * (m_hat / (_jnp.sqrt(v_hat) + ADAM_EPS) + ADAM_WD * w)
    return delta, m, v


def reference(x, meta_tokens, hg_lower_bounds, w_in, w_conv, w_pool, pool_scale, hg_norm_g, w_o, ln1_g, ln1_b, w_up, w_ffn_conv, b_ffn_conv, w_down, ln2_g, ln2_b, loss_target, m_meta_tokens, m_hg_lower_bounds, m_w_in, m_w_conv, m_w_pool, m_pool_scale, m_hg_norm_g, m_w_o, m_ln1_g, m_ln1_b, m_w_up, m_w_ffn_conv, m_b_ffn_conv, m_w_down, m_ln2_g, m_ln2_b, v_meta_tokens, v_hg_lower_bounds, v_w_in, v_w_conv, v_w_pool, v_pool_scale, v_hg_norm_g, v_w_o, v_ln1_g, v_ln1_b, v_w_up, v_w_ffn_conv, v_b_ffn_conv, v_w_down, v_ln2_g, v_ln2_b):
    given = dict(x=x, meta_tokens=meta_tokens, hg_lower_bounds=hg_lower_bounds, w_in=w_in, w_conv=w_conv, w_pool=w_pool, pool_scale=pool_scale, hg_norm_g=hg_norm_g, w_o=w_o, ln1_g=ln1_g, ln1_b=ln1_b, w_up=w_up, w_ffn_conv=w_ffn_conv, b_ffn_conv=b_ffn_conv, w_down=w_down, ln2_g=ln2_g, ln2_b=ln2_b, loss_target=loss_target, m_meta_tokens=m_meta_tokens, m_hg_lower_bounds=m_hg_lower_bounds, m_w_in=m_w_in, m_w_conv=m_w_conv, m_w_pool=m_w_pool, m_pool_scale=m_pool_scale, m_hg_norm_g=m_hg_norm_g, m_w_o=m_w_o, m_ln1_g=m_ln1_g, m_ln1_b=m_ln1_b, m_w_up=m_w_up, m_w_ffn_conv=m_w_ffn_conv, m_b_ffn_conv=m_b_ffn_conv, m_w_down=m_w_down, m_ln2_g=m_ln2_g, m_ln2_b=m_ln2_b, v_meta_tokens=v_meta_tokens, v_hg_lower_bounds=v_hg_lower_bounds, v_w_in=v_w_in, v_w_conv=v_w_conv, v_w_pool=v_w_pool, v_pool_scale=v_pool_scale, v_hg_norm_g=v_hg_norm_g, v_w_o=v_w_o, v_ln1_g=v_ln1_g, v_ln1_b=v_ln1_b, v_w_up=v_w_up, v_w_ffn_conv=v_w_ffn_conv, v_b_ffn_conv=v_b_ffn_conv, v_w_down=v_w_down, v_ln2_g=v_ln2_g, v_ln2_b=v_ln2_b)
    weights = {n: given[n] for n in TWIN_WEIGHTS}
    shared = {n: given[n] for n in SHARED_INPUTS}
    per_example = {n: given[n] for n in ['x']}
    grad_fn = _jax.value_and_grad(_loss, argnums=(0, 1))

    def one_microbatch(ex, loss_target):
        ex = dict(ex)
        diff = ex.pop(TWIN_DIFF_INPUT)
        return grad_fn(weights, diff, {**shared, **ex}, loss_target)

    if N_MICROBATCH == 1:
        loss, (grad_w, grad_x) = one_microbatch(per_example, given["loss_target"])
    else:
        def body(carry, xs):
            loss_sum, grad_sum = carry
            l_k, (gw_k, gx_k) = one_microbatch(xs[0], xs[1])
            with _jax.named_scope("update"):
                return (loss_sum + l_k, _jax.tree.map(_jnp.add, grad_sum, gw_k)), gx_k

        init = (_jnp.zeros((), _jnp.float32), _jax.tree.map(_jnp.zeros_like, weights))
        (loss, grad_w), grad_x = _jax.lax.scan(body, init, (per_example, given["loss_target"]))
    with _jax.named_scope("update"):
        delta_w, new_m, new_v = {}, {}, {}
        for n in TWIN_WEIGHTS:
            delta_w[n], new_m[n], new_v[n] = _adamw(weights[n], grad_w[n], given["m_" + n], given["v_" + n])
    return (loss, grad_x, *[grad_w[n] for n in TWIN_WEIGHTS], *[delta_w[n] for n in TWIN_WEIGHTS],
            *[new_m[n] for n in TWIN_WEIGHTS], *[new_v[n] for n in TWIN_WEIGHTS])
```

```python
import functools

import numpy as np

import jax
import jax.numpy as jnp
from jax import lax
from jax.experimental import pallas as pl
from jax.experimental.pallas import tpu as pltpu

F32 = jnp.float32
BF = jnp.bfloat16
MX = jnp.bfloat16

D = 1024
CW = 256
HW = 512
HD = 128
NH = 4
PW = 256
DIN = 3072
N_META = 16
CH = 64
ROW0 = 256
PADR = ROW0 - N_META
HALO = 16
FH = 8
LEVELS = (32, 16, 8, 4, 2, 1)
DEPTH = 2
ALPHA = (2 * DEPTH) ** 0.25
LN_EPS = 1e-5
RMS_EPS = 1e-6
F_FLOOR = 1e-30
SCALE = HD ** -0.5
ADAM_LR, ADAM_B1, ADAM_B2, ADAM_EPS, ADAM_WD, ADAM_STEP = 0.001, 0.9, 0.999, 1e-08, 0.01, 10

VMEM_V7X = 64 * 2 ** 20
LANES = 128
SUBLANES = 8

NN = (((1,), (0,)), ((), ()))
NT = (((1,), (1,)), ((), ()))
TN = (((0,), (0,)), ((), ()))
MESH = pl.DeviceIdType.MESH
ANY = pl.BlockSpec(memory_space=pl.ANY)


def _dot(a, b, dn):
    return lax.dot_general(a.astype(MX), b.astype(MX), dn, preferred_element_type=F32)


def _cp(sem, est_bytes):
    lim = int(min(VMEM_V7X - 6 * 2 ** 20, max(32 * 2 ** 20, est_bytes)))
    return pltpu.CompilerParams(dimension_semantics=sem, vmem_limit_bytes=lim)


def _nbytes(shape, dtype):
    return int(np.prod(shape)) * jnp.dtype(dtype).itemsize


def _row_tile(rows, row_bytes, budget):
    best = SUBLANES
    for t in range(SUBLANES, rows + 1, SUBLANES):
        if rows % t == 0 and t * row_bytes <= budget:
            best = t
    return best


def _mm(a, w, *, nt=False, res=None, tm, out_dtype=F32, name):
    R, K = a.shape
    N = w.shape[0] if nt else w.shape[1]
    dn = NT if nt else NN

    def body(*refs):
        if res is None:
            a_ref, w_ref, o_ref = refs
        else:
            a_ref, w_ref, r_ref, o_ref = refs
        acc = _dot(a_ref[...], w_ref[...], dn)
        if res is not None:
            acc = acc + ALPHA * r_ref[...]
        o_ref[...] = acc.astype(out_dtype)

    in_specs = [pl.BlockSpec((tm, K), lambda i: (i, 0)), pl.BlockSpec(w.shape, lambda i: (0, 0))]
    args = [a, w]
    est = 2 * _nbytes((tm, K), a.dtype) + 2 * _nbytes(w.shape, w.dtype) + 3 * _nbytes((tm, N), F32)
    if res is not None:
        in_specs.append(pl.BlockSpec((tm, N), lambda i: (i, 0)))
        args.append(res)
        est += 2 * _nbytes((tm, N), F32)
    return pl.pallas_call(
        body, name=name, grid=(R // tm,), in_specs=in_specs,
        out_specs=pl.BlockSpec((tm, N), lambda i: (i, 0)),
        out_shape=jax.ShapeDtypeStruct((R, N), out_dtype),
        compiler_params=_cp(("parallel",), est + 4 * 2 ** 20),
    )(*args)


def _wgrad(a, b, *, layer, prev, slabs_on_cols, tm, tn, name):
    R, Ka = a.shape
    Nb = b.shape[1]
    if slabs_on_cols:
        out_shape = (4, DEPTH, Ka, Nb // 4)
        assert tn == Nb // 4
        out_spec = pl.BlockSpec((None, None, Ka, tn), lambda j, i: (j, layer, 0, 0))
    else:
        out_shape = (4, DEPTH, Ka // 4, Nb)
        out_spec = pl.BlockSpec((4, None, Ka // 4, tn), lambda j, i: (0, layer, 0, j))

    def body(*refs):
        a_ref, b_ref, o_ref = refs[0], refs[1], refs[-1]

        @pl.when(pl.program_id(1) == 0)
        def _():
            o_ref[...] = jnp.zeros(o_ref.shape, F32)

        acc = _dot(a_ref[...], b_ref[...], TN)
        o_ref[...] += acc.reshape(o_ref.shape)

    in_specs = [pl.BlockSpec((tm, Ka), lambda j, i: (i, 0)), pl.BlockSpec((tm, tn), lambda j, i: (i, j))]
    args = [a, b]
    aliases = {}
    if prev is not None:
        in_specs.append(ANY)
        args.append(prev)
        aliases = {2: 0}
    est = 2 * _nbytes((tm, Ka), a.dtype) + 2 * _nbytes((tm, tn), b.dtype) + 4 * _nbytes((Ka, tn), F32) \
        + _nbytes((tm, Ka), F32)
    return pl.pallas_call(
        body, name=name, grid=(Nb // tn, R // tm), in_specs=in_specs, out_specs=out_spec,
        out_shape=jax.ShapeDtypeStruct(out_shape, F32), input_output_aliases=aliases,
        compiler_params=_cp(("parallel", "arbitrary"), est + 4 * 2 ** 20),
    )(*args)


def _mm_ln(a, w, xres, g, b, *, tm, name):
    R, K = a.shape

    def body(a_ref, w_ref, x_ref, g_ref, b_ref, xo_ref, xh_ref, r_ref):
        z = ALPHA * x_ref[...] + _dot(a_ref[...], w_ref[...], NN)
        mu = jnp.mean(z, axis=-1, keepdims=True)
        zc = z - mu
        var = jnp.mean(zc * zc, axis=-1, keepdims=True)
        r = lax.rsqrt(var + LN_EPS)
        xh = zc * r
        xh_ref[...] = xh
        r_ref[...] = r
        xo_ref[...] = xh * g_ref[...] + b_ref[...]

    row = lambda i: (i, 0)
    fix = lambda i: (0, 0)
    est = 2 * _nbytes((tm, K), a.dtype) + 2 * _nbytes(w.shape, w.dtype) + 10 * _nbytes((tm, D), F32)
    return pl.pallas_call(
        body, name=name, grid=(R // tm,),
        in_specs=[pl.BlockSpec((tm, K), row), pl.BlockSpec(w.shape, fix), pl.BlockSpec((tm, D), row),
                  pl.BlockSpec((1, D), fix), pl.BlockSpec((1, D), fix)],
        out_specs=[pl.BlockSpec((tm, D), row), pl.BlockSpec((tm, D), row), pl.BlockSpec((tm, 1), row)],
        out_shape=[jax.ShapeDtypeStruct((R, D), F32), jax.ShapeDtypeStruct((R, D), F32),
                   jax.ShapeDtypeStruct((R, 1), F32)],
        compiler_params=_cp(("parallel",), est + 4 * 2 ** 20),
    )(a, w, xres, g.reshape(1, D), b.reshape(1, D))


def _ln_bwd(dx, xh, r, g, *, tm, name):
    R = dx.shape[0]

    def body(dx_ref, xh_ref, r_ref, g_ref, dz_ref, gb_ref):
        @pl.when(pl.program_id(0) == 0)
        def _():
            gb_ref[...] = jnp.zeros(gb_ref.shape, F32)

        dxv = dx_ref[...]
        xhv = xh_ref[...]
        dyh = dxv * g_ref[...]
        m1 = jnp.mean(dyh, axis=-1, keepdims=True)
        m2 = jnp.mean(dyh * xhv, axis=-1, keepdims=True)
        dz_ref[...] = r_ref[...] * (dyh - m1 - xhv * m2)
        gb_ref[0:1, :] += jnp.sum(dxv * xhv, axis=0, keepdims=True)
        gb_ref[1:2, :] += jnp.sum(dxv, axis=0, keepdims=True)

    row = lambda i: (i, 0)
    fix = lambda i: (0, 0)
    return pl.pallas_call(
        body, name=name, grid=(R // tm,),
        in_specs=[pl.BlockSpec((tm, D), row), pl.BlockSpec((tm, D), row), pl.BlockSpec((tm, 1), row),
                  pl.BlockSpec((1, D), fix)],
        out_specs=[pl.BlockSpec((tm, D), row), pl.BlockSpec((SUBLANES, D), fix)],
        out_shape=[jax.ShapeDtypeStruct((R, D), F32), jax.ShapeDtypeStruct((SUBLANES, D), F32)],
        compiler_params=_cp(("arbitrary",), 12 * _nbytes((tm, D), F32)),
    )(dx, xh, r, g.reshape(1, D))


def _loss(x2, tgt, *, name):
    R = x2.shape[0]
    tb = ROW0

    def body(x_ref, t_ref, dy_ref, acc_ref):
        i = pl.program_id(0)

        @pl.when(i == 0)
        def _():
            acc_ref[...] = jnp.zeros(acc_ref.shape, F32)
            dy_ref[...] = jnp.zeros(dy_ref.shape, F32)

        @pl.when(i > 0)
        def _():
            err = x_ref[...] - t_ref[...]
            dy_ref[...] = err / D
            acc_ref[0:1, :] += jnp.sum(err * err, axis=0, keepdims=True)

    return pl.pallas_call(
        body, name=name, grid=(R // tb,),
        in_specs=[pl.BlockSpec((tb, D), lambda i: (i, 0)),
                  pl.BlockSpec((tb, D), lambda i: (jnp.maximum(i - 1, 0), 0))],
        out_specs=[pl.BlockSpec((tb, D), lambda i: (i, 0)), pl.BlockSpec((SUBLANES, D), lambda i: (0, 0))],
        out_shape=[jax.ShapeDtypeStruct((R, D), F32), jax.ShapeDtypeStruct((SUBLANES, D), F32)],
        compiler_params=_cp(("arbitrary",), 8 * _nbytes((tb, D), F32)),
    )(x2, tgt)


def _row_ids(start, n):
    return start + lax.broadcasted_iota(jnp.int32, (n, 1), 0)


def _ffn_fwd(up, taps, bias, *, tb, name):
    R, F2 = up.shape
    F = F2 // 2
    nh = tb // FH

    def body(u_ref, up_ref, w_ref, b_ref, a_ref):
        i = pl.program_id(0)
        row = _row_ids(i * tb - FH, tb + FH)
        ue = jnp.concatenate([up_ref[...], u_ref[...]], axis=0)
        ue = jnp.where(row >= PADR, ue, 0.0)
        n = tb + FH
        u = (w_ref[0:1, :] * pltpu.roll(ue, 2, 0) + w_ref[1:2, :] * pltpu.roll(ue, 1, 0)
             + w_ref[2:3, :] * ue + b_ref[...])[FH:n]
        gate = u[:, :F]
        val = u[:, F:]
        a_ref[...] = (gate * jax.nn.sigmoid(gate) * val).astype(a_ref.dtype)

    est = 2 * _nbytes((tb, F2), F32) * 4
    return pl.pallas_call(
        body, name=name, grid=(R // tb,),
        in_specs=[pl.BlockSpec((tb, F2), lambda i: (i, 0)),
                  pl.BlockSpec((FH, F2), lambda i: (jnp.maximum(i * nh - 1, 0), 0)),
                  pl.BlockSpec((SUBLANES, F2), lambda i: (0, 0)), pl.BlockSpec((1, F2), lambda i: (0, 0))],
        out_specs=pl.BlockSpec((tb, F), lambda i: (i, 0)),
        out_shape=jax.ShapeDtypeStruct((R, F), MX),
        compiler_params=_cp(("parallel",), est),
    )(up, up, taps, bias)


def _ffn_bwd(da, up, taps, bias, *, tb, name):
    R, F2 = up.shape
    F = F2 // 2
    nh = tb // FH
    last = R // FH - 1

    def body(da_ref, dan_ref, u_ref, up_ref, un_ref, w_ref, b_ref, du_ref, acc_ref):
        i = pl.program_id(0)

        @pl.when(i == 0)
        def _():
            acc_ref[...] = jnp.zeros(acc_ref.shape, F32)

        n = tb + 2 * FH
        row = _row_ids(i * tb - FH, n)
        ue = jnp.concatenate([up_ref[...], u_ref[...], un_ref[...]], axis=0)
        ue = jnp.where((row >= PADR) & (row < R), ue, 0.0)
        w0, w1, w2 = w_ref[0:1, :], w_ref[1:2, :], w_ref[2:3, :]
        x2 = pltpu.roll(ue, 2, 0)
        x1 = pltpu.roll(ue, 1, 0)
        u = (w0 * x2 + w1 * x1 + w2 * ue + b_ref[...])[FH:n]
        m = tb + FH
        rown = row[FH:n]
        dae = jnp.concatenate([da_ref[...], dan_ref[...]], axis=0)
        dae = jnp.where(rown < R, dae, 0.0)
        gate = u[:, :F]
        val = u[:, F:]
        sg = jax.nn.sigmoid(gate)
        dgate = dae * val * (sg * (1.0 + gate * (1.0 - sg)))
        dval = dae * (gate * sg)
        du = jnp.concatenate([dgate, dval], axis=1)
        dx = w2 * du + w1 * pltpu.roll(du, m - 1, 0) + w0 * pltpu.roll(du, m - 2, 0)
        dx = jnp.where(rown >= PADR, dx, 0.0)[0:tb]
        du_ref[...] = dx.astype(du_ref.dtype)
        duc = du[0:tb]
        acc_ref[0:1, :] += jnp.sum(duc * x2[FH:FH + tb], axis=0, keepdims=True)
        acc_ref[1:2, :] += jnp.sum(duc * x1[FH:FH + tb], axis=0, keepdims=True)
        acc_ref[2:3, :] += jnp.sum(duc * ue[FH:FH + tb], axis=0, keepdims=True)
        acc_ref[3:4, :] += jnp.sum(duc, axis=0, keepdims=True)

    prev = lambda i: (jnp.maximum(i * nh - 1, 0), 0)
    nxt = lambda i: (jnp.minimum(i * nh + nh, last), 0)
    cur = lambda i: (i, 0)
    fix = lambda i: (0, 0)
    est = 12 * _nbytes((tb + 2 * FH, F2), F32)
    return pl.pallas_call(
        body, name=name, grid=(R // tb,),
        in_specs=[pl.BlockSpec((tb, F), cur), pl.BlockSpec((FH, F), nxt),
                  pl.BlockSpec((tb, F2), cur), pl.BlockSpec((FH, F2), prev), pl.BlockSpec((FH, F2), nxt),
                  pl.BlockSpec((SUBLANES, F2), fix), pl.BlockSpec((1, F2), fix)],
        out_specs=[pl.BlockSpec((tb, F2), cur), pl.BlockSpec((SUBLANES, F2), fix)],
        out_shape=[jax.ShapeDtypeStruct((R, F2), MX), jax.ShapeDtypeStruct((SUBLANES, F2), F32)],
        compiler_params=_cp(("arbitrary",), est),
    )(da, da, up, up, up, taps, bias)


def _hg_consts():
    t = np.arange(CH)[:, None]
    j = np.arange(CH)[None, :]
    low = (j <= t).astype(np.float32)
    blocks = [low, (j > t).astype(np.float32)]
    for m in LEVELS:
        ref = (t // (2 * m)) * 2 * m + m - 1
        blocks.append(low - (j <= ref).astype(np.float32))
    mat = np.concatenate(blocks, axis=0)
    suf = (j >= t).astype(np.float32)
    return jnp.asarray(mat, BF), jnp.asarray(suf, BF)


def _split_dot(mat, x):
    hi = x.astype(BF)
    lo = (x - hi.astype(F32)).astype(BF)
    return (lax.dot_general(mat, hi, NN, preferred_element_type=F32)
            + lax.dot_general(mat, lo, NN, preferred_element_type=F32))


def _lane_select(a2, a4, a8, a16):
    lane = lax.broadcasted_iota(jnp.int32, (1, PW), 1)
    return jnp.where(lane < 64, a2, jnp.where(lane < 128, a4, jnp.where(lane < 192, a8, a16)))


def _pool_count(row):
    win = _lane_select(2.0, 4.0, 8.0, 16.0)
    t1 = jnp.maximum((row - PADR + 1).astype(F32), 1.0)
    return jnp.minimum(t1, win)


def _gates(fz, lb, valid):
    sig = jax.nn.sigmoid(fz)
    f = lb + (1.0 - lb) * sig
    lf = jnp.where(valid, jnp.log(jnp.maximum(f, F_FLOOR)), 0.0)
    kk = jnp.where(valid, (1.0 - lb) * (1.0 - sig), 0.0)
    return sig, f, lf, kk


def _level_masks():
    tt = lax.broadcasted_iota(jnp.int32, (CH, CH), 0)
    ss = lax.broadcasted_iota(jnp.int32, (CH, CH), 1)
    xr = tt ^ ss
    low = tt > ss
    return tt, ss, [(xr >= m) & (xr < 2 * m) & low for m in LEVELS]


def _intra(q, kk, ex, sl, tt, ss, masks):
    a = jnp.where(tt == ss, jnp.sum(q * kk, axis=-1, keepdims=True), 0.0)
    parts = []
    for l in range(len(LEVELS)):
        e = jnp.exp(-jnp.abs(ex[(2 + l) * CH:(3 + l) * CH, sl]))
        ql = (q * e).astype(MX)
        kl = (kk * e).astype(MX)
        a = a + jnp.where(masks[l], _dot(ql, kl, NT), 0.0)
        parts.append((e, ql, kl))
    return a, parts


def _mixer_fwd(h, taps, wbd, ps, gn, lb, mat, *, name):
    R = h.shape[0]
    nb = R // CH
    q4 = CH // HALO

    def body(h_ref, hp_ref, w_ref, wbd_ref, ps_ref, gn_ref, lb_ref, mat_ref, y_ref, o_ref, sp_ref, st_ref):
        i = pl.program_id(0)

        @pl.when(i == 0)
        def _():
            st_ref[...] = jnp.zeros(st_ref.shape, F32)

        row = _row_ids(i * CH, CH)
        valid = row >= PADR
        rowe = _row_ids(i * CH - HALO, CH + HALO)
        valide = rowe >= PADR
        se = jnp.concatenate([hp_ref[:, 256:512] * hp_ref[:, 512:768], h_ref[:, 256:512] * h_ref[:, 512:768]], axis=0)
        se = jnp.where(valide, se, 0.0)
        conv = (w_ref[0:1, :] * pltpu.roll(se, 2, 0) + w_ref[1:2, :] * pltpu.roll(se, 1, 0)
                + w_ref[2:3, :] * se)[HALO:]
        y_ref[:, 0:CW] = (h_ref[:, 0:256] * conv).astype(y_ref.dtype)
        ve = jnp.where(valide, jnp.concatenate([hp_ref[:, 2816:3072], h_ref[:, 2816:3072]], axis=0), 0.0)
        s2 = ve + pltpu.roll(ve, 1, 0)
        s4 = s2 + pltpu.roll(s2, 2, 0)
        s8 = s4 + pltpu.roll(s4, 4, 0)
        s16 = s8 + pltpu.roll(s8, 8, 0)
        dp = (_lane_select(s2, s4, s8, s16) / _pool_count(rowe) - ve)[HALO:]
        y_ref[:, CW + HW:D] = (_dot(dp, wbd_ref[...], NN) * ps_ref[...]).astype(y_ref.dtype)
        lbv = lb_ref[...]
        _, _, lf, kk = _gates(h_ref[:, 1280:1792], lbv, valid)
        q = h_ref[:, 768:1280] * SCALE
        ii = h_ref[:, 1792:2304]
        gz = h_ref[:, 2304:2816]
        ex = _split_dot(mat_ref[...], lf)
        eg = jnp.exp(ex[0:CH])
        q0 = q * eg
        kr = kk * jnp.exp(ex[CH:2 * CH])
        tt, ss, masks = _level_masks()
        for hd in range(NH):
            sl = slice(HD * hd, HD * (hd + 1))
            st = st_ref[hd]
            sp_ref[hd] = st
            a, _ = _intra(q[:, sl], kk[:, sl], ex, sl, tt, ss, masks)
            o = _dot(q0[:, sl], st, NT) + _dot(a, ii[:, sl], NN)
            st_ref[hd] = st * eg[CH - 1:CH, sl] + _dot(ii[:, sl], kr[:, sl], TN)
            o_ref[:, sl] = o
            r = lax.rsqrt(jnp.mean(o * o, axis=-1, keepdims=True) + RMS_EPS)
            g_ = gz[:, sl]
            y_ref[:, CW + HD * hd:CW + HD * (hd + 1)] = (
                o * r * gn_ref[...] * (g_ * jax.nn.sigmoid(g_))).astype(y_ref.dtype)

    fix = lambda i: (0, 0)
    return pl.pallas_call(
        body, name=name, grid=(nb,),
        in_specs=[pl.BlockSpec((CH, DIN), lambda i: (i, 0)),
                  pl.BlockSpec((HALO, DIN), lambda i: (jnp.maximum(i * q4 - 1, 0), 0)),
                  pl.BlockSpec((SUBLANES, CW), fix), pl.BlockSpec((PW, PW), fix), pl.BlockSpec((1, PW), fix),
                  pl.BlockSpec((1, HD), fix), pl.BlockSpec((1, HW), fix), pl.BlockSpec(mat.shape, fix)],
        out_specs=[pl.BlockSpec((CH, D), lambda i: (i, 0)), pl.BlockSpec((CH, HW), lambda i: (i, 0)),
                   pl.BlockSpec((None, NH, HD, HD), lambda i: (i, 0, 0, 0))],
        out_shape=[jax.ShapeDtypeStruct((R, D), MX), jax.ShapeDtypeStruct((R, HW), F32),
                   jax.ShapeDtypeStruct((nb, NH, HD, HD), F32)],
        scratch_shapes=[pltpu.VMEM((NH, HD, HD), F32)],
        compiler_params=_cp(("arbitrary",), 32 * 2 ** 20),
    )(h, h, taps, wbd, ps, gn, lb, mat)


def _mixer_bwd(h, dy, o, sp, taps, wbd, ps, gn, lb, mat, suf, *, name):
    R = h.shape[0]
    nb = R // CH
    q4 = CH // HALO
    lasth = R // HALO - 1

    def body(h_ref, hp_ref, hn_ref, dy_ref, dyn_ref, o_ref, sp_ref, w_ref, wbd_ref, ps_ref, gn_ref, lb_ref,
             mat_ref, suf_ref, dh_ref, dw_ref, dsc_ref, dgn_ref, dlb_ref, dwbd_ref, dst_ref):
        i = pl.program_id(0)
        b = nb - 1 - i

        @pl.when(i == 0)
        def _():
            dst_ref[...] = jnp.zeros(dst_ref.shape, F32)
            dw_ref[...] = jnp.zeros(dw_ref.shape, F32)
            dsc_ref[...] = jnp.zeros(dsc_ref.shape, F32)
            dgn_ref[...] = jnp.zeros(dgn_ref.shape, F32)
            dlb_ref[...] = jnp.zeros(dlb_ref.shape, F32)
            dwbd_ref[...] = jnp.zeros(dwbd_ref.shape, F32)

        ne = CH + 2 * HALO
        nc = CH + HALO
        row = _row_ids(b * CH, CH)
        valid = row >= PADR
        rowe = _row_ids(b * CH - HALO, ne)
        valide = (rowe >= PADR) & (rowe < R)
        rown = rowe[HALO:]
        validn = rown < R

        def cat3(lo, hi):
            return jnp.concatenate([hp_ref[:, lo:hi], h_ref[:, lo:hi], hn_ref[:, lo:hi]], axis=0)

        def catd(lo, hi):
            return jnp.where(validn, jnp.concatenate([dy_ref[:, lo:hi], dyn_ref[:, lo:hi]], axis=0), 0.0)

        w0, w1, w2 = w_ref[0:1, :], w_ref[1:2, :], w_ref[2:3, :]
        cce = cat3(256, 512)
        cve = cat3(512, 768)
        se = jnp.where(valide, cce * cve, 0.0)
        sm2 = pltpu.roll(se, 2, 0)
        sm1 = pltpu.roll(se, 1, 0)
        conv = (w0 * sm2 + w1 * sm1 + w2 * se)[HALO:HALO + CH]
        cbn = jnp.concatenate([h_ref[:, 0:256], hn_ref[:, 0:256]], axis=0)
        dconv = catd(0, CW) * cbn
        ds = w2 * dconv + w1 * pltpu.roll(dconv, nc - 1, 0) + w0 * pltpu.roll(dconv, nc - 2, 0)
        ds = jnp.where(valid, ds[0:CH], 0.0)
        dcv = dconv[0:CH]
        dw_ref[0:1, :] += jnp.sum(dcv * sm2[HALO:HALO + CH], axis=0, keepdims=True)
        dw_ref[1:2, :] += jnp.sum(dcv * sm1[HALO:HALO + CH], axis=0, keepdims=True)
        dw_ref[2:3, :] += jnp.sum(dcv * se[HALO:HALO + CH], axis=0, keepdims=True)
        dh_ref[:, 0:256] = jnp.where(valid, dy_ref[:, 0:CW] * conv, 0.0).astype(dh_ref.dtype)
        dh_ref[:, 256:512] = (ds * h_ref[:, 512:768]).astype(dh_ref.dtype)
        dh_ref[:, 512:768] = (ds * h_ref[:, 256:512]).astype(dh_ref.dtype)

        ve = jnp.where(valide, cat3(2816, 3072), 0.0)
        s2 = ve + pltpu.roll(ve, 1, 0)
        s4 = s2 + pltpu.roll(s2, 2, 0)
        s8 = s4 + pltpu.roll(s4, 4, 0)
        s16 = s8 + pltpu.roll(s8, 8, 0)
        cnte = _pool_count(rowe)
        dp = (_lane_select(s2, s4, s8, s16) / cnte - ve)[HALO:HALO + CH]
        dyp = catd(CW + HW, D)
        pre = _dot(dp, wbd_ref[...], NN)
        dsc_ref[0:1, :] += jnp.sum(dyp[0:CH] * pre, axis=0, keepdims=True)
        dyps = dyp * ps_ref[...]
        dd = _dot(dyps, wbd_ref[...], NT)
        dwbd_ref[...] += _dot(dp, dyps[0:CH], TN)
        e = dd / cnte[HALO:]
        t2 = e + pltpu.roll(e, nc - 1, 0)
        t4 = t2 + pltpu.roll(t2, nc - 2, 0)
        t8 = t4 + pltpu.roll(t4, nc - 4, 0)
        t16 = t8 + pltpu.roll(t8, nc - 8, 0)
        dv = (_lane_select(t2, t4, t8, t16) - dd)[0:CH]
        dh_ref[:, 2816:3072] = jnp.where(valid, dv, 0.0).astype(dh_ref.dtype)

        lbv = lb_ref[...]
        sig, f, lf, kk = _gates(h_ref[:, 1280:1792], lbv, valid)
        q = h_ref[:, 768:1280] * SCALE
        ii = h_ref[:, 1792:2304]
        gz = h_ref[:, 2304:2816]
        ex = _split_dot(mat_ref[...], lf)
        eg = jnp.exp(ex[0:CH])
        egr = jnp.exp(ex[CH:2 * CH])
        q0 = q * eg
        kr = kk * egr
        tt, ss, masks = _level_masks()
        gnv = gn_ref[...]
        dqs, dks, dis, dgzs, tails, dbs = [], [], [], [], [], []
        for hd in range(NH):
            sl = slice(HD * hd, HD * (hd + 1))
            ov = o_ref[:, sl]
            r = lax.rsqrt(jnp.mean(ov * ov, axis=-1, keepdims=True) + RMS_EPS)
            oh = ov * r
            g_ = gz[:, sl]
            sg = jax.nn.sigmoid(g_)
            dyv = dy_ref[:, CW + HD * hd:CW + HD * (hd + 1)]
            don = dyv * (g_ * sg)
            dgzs.append(dyv * (oh * gnv) * (sg * (1.0 + g_ * (1.0 - sg))))
            dgn_ref[0:1, :] += jnp.sum(don * oh, axis=0, keepdims=True)
            doh = don * gnv
            do = r * (doh - oh * jnp.mean(doh * oh, axis=-1, keepdims=True))
            st = sp_ref[hd]
            dst = dst_ref[hd]
            qh, kh, ih = q[:, sl], kk[:, sl], ii[:, sl]
            a, parts = _intra(qh, kh, ex, sl, tt, ss, masks)
            da = jnp.where(tt >= ss, _dot(do, ih, NT), 0.0)
            dis.append(_dot(a, do, TN) + _dot(kr[:, sl], dst, NT))
            q0h = q0[:, sl].astype(MX)
            krh = kr[:, sl].astype(MX)
            dq0 = _dot(do, st, NN)
            dkr = _dot(ih, dst, NN)
            dq = dq0 * eg[:, sl]
            dk = dkr * egr[:, sl]
            kdk = krh.astype(F32) * dkr
            db = q0h.astype(F32) * dq0 - kdk
            tails.append(jnp.sum(kdk, axis=0, keepdims=True)
                         + eg[CH - 1:CH, sl] * jnp.sum(dst * st, axis=0, keepdims=True))
            dga = jnp.sum(jnp.where(tt == ss, da, 0.0), axis=-1, keepdims=True)
            dq = dq + dga * kh
            dk = dk + dga * qh
            for l in range(len(LEVELS)):
                e_l, ql, kl = parts[l]
                dpl = jnp.where(masks[l], da, 0.0).astype(MX)
                dql = _dot(dpl, kl, NN)
                dkl = _dot(dpl, ql, TN)
                dq = dq + dql * e_l
                dk = dk + dkl * e_l
                db = db + (ql.astype(F32) * dql - kl.astype(F32) * dkl)
            dst_ref[hd] = dst * eg[CH - 1:CH, sl] + _dot(do, q0h, TN)
            dqs.append(dq)
            dks.append(dk)
            dbs.append(db)
        dq = jnp.concatenate(dqs, axis=1)
        dk = jnp.concatenate(dks, axis=1)
        db = jnp.concatenate(dbs, axis=1)
        dlf = _split_dot(suf_ref[...], db) + jnp.concatenate(tails, axis=1)
        t = jnp.where(valid, dlf * jnp.where(f > F_FLOOR, 1.0 / f, 0.0) - dk, 0.0)
        dlb_ref[0:1, :] += jnp.sum(t * (1.0 - sig), axis=0, keepdims=True)
        dh_ref[:, 768:1280] = jnp.where(valid, dq * SCALE, 0.0).astype(dh_ref.dtype)
        dh_ref[:, 1280:1792] = (t * (1.0 - lbv) * (sig * (1.0 - sig))).astype(dh_ref.dtype)
        dh_ref[:, 1792:2304] = jnp.where(valid, jnp.concatenate(dis, axis=1), 0.0).astype(dh_ref.dtype)
        dh_ref[:, 2304:2816] = jnp.where(valid, jnp.concatenate(dgzs, axis=1), 0.0).astype(dh_ref.dtype)

    cur = lambda i: (nb - 1 - i, 0)
    prev = lambda i: (jnp.maximum((nb - 1 - i) * q4 - 1, 0), 0)
    nxt = lambda i: (jnp.minimum((nb - 1 - i) * q4 + q4, lasth), 0)
    fix = lambda i: (0, 0)
    return pl.pallas_call(
        body, name=name, grid=(nb,),
        in_specs=[pl.BlockSpec((CH, DIN), cur), pl.BlockSpec((HALO, DIN), prev), pl.BlockSpec((HALO, DIN), nxt),
                  pl.BlockSpec((CH, D), cur), pl.BlockSpec((HALO, D), nxt), pl.BlockSpec((CH, HW), cur),
                  pl.BlockSpec((None, NH, HD, HD), lambda i: (nb - 1 - i, 0, 0, 0)),
                  pl.BlockSpec((SUBLANES, CW), fix), pl.BlockSpec((PW, PW), fix), pl.BlockSpec((1, PW), fix),
                  pl.BlockSpec((1, HD), fix), pl.BlockSpec((1, HW), fix), pl.BlockSpec(mat.shape, fix),
                  pl.BlockSpec(suf.shape, fix)],
        out_specs=[pl.BlockSpec((CH, DIN), cur), pl.BlockSpec((SUBLANES, CW), fix), pl.BlockSpec((SUBLANES, PW), fix),
                   pl.BlockSpec((SUBLANES, HD), fix), pl.BlockSpec((SUBLANES, HW), fix), pl.BlockSpec((PW, PW), fix)],
        out_shape=[jax.ShapeDtypeStruct((R, DIN), MX), jax.ShapeDtypeStruct((SUBLANES, CW), F32),
                   jax.ShapeDtypeStruct((SUBLANES, PW), F32), jax.ShapeDtypeStruct((SUBLANES, HD), F32),
                   jax.ShapeDtypeStruct((SUBLANES, HW), F32), jax.ShapeDtypeStruct((PW, PW), F32)],
        scratch_shapes=[pltpu.VMEM((NH, HD, HD), F32)],
        compiler_params=_cp(("arbitrary",), 40 * 2 ** 20),
    )(h, h, h, dy, dy, o, sp, taps, wbd, ps, gn, lb, mat, suf)


def _sum_slots(recv, *, name):
    S, L, rows, cols = recv.shape
    tr = _row_tile(rows, S * cols * 4, 6 * 2 ** 20)

    def body(r_ref, o_ref):
        acc = r_ref[0]
        for s in range(1, S):
            acc = acc + r_ref[s]
        o_ref[...] = acc

    return pl.pallas_call(
        body, name=name, grid=(L, rows // tr),
        in_specs=[pl.BlockSpec((S, None, tr, cols), lambda l, i: (0, l, i, 0))],
        out_specs=pl.BlockSpec((None, tr, cols), lambda l, i: (l, i, 0)),
        out_shape=jax.ShapeDtypeStruct((L, rows, cols), F32),
        compiler_params=_cp(("parallel", "parallel"), 4 * S * tr * cols * 4),
    )(recv)


def _adamw(w, m, v, ga, gb, *, name):
    L, rows, cols = w.shape
    tr = _row_tile(rows, cols * 4, 2 ** 20)
    two = gb is not None

    def body(*refs):
        if two:
            w_ref, m_ref, v_ref, a_ref, b_ref, g_ref, d_ref, m2_ref, v2_ref = refs
            g = a_ref[...] + b_ref[...]
        else:
            w_ref, m_ref, v_ref, a_ref, g_ref, d_ref, m2_ref, v2_ref = refs
            g = a_ref[...]
        m2 = ADAM_B1 * m_ref[...] + (1.0 - ADAM_B1) * g
        v2 = ADAM_B2 * v_ref[...] + (1.0 - ADAM_B2) * (g * g)
        m_hat = m2 / (1.0 - ADAM_B1 ** ADAM_STEP)
        v_hat = v2 / (1.0 - ADAM_B2 ** ADAM_STEP)
        g_ref[...] = g
        d_ref[...] = -ADAM_LR * (m_hat / (jnp.sqrt(v_hat) + ADAM_EPS) + ADAM_WD * w_ref[...])
        m2_ref[...] = m2
        v2_ref[...] = v2

    spec = pl.BlockSpec((None, tr, cols), lambda l, i: (l, i, 0))
    args = [w, m, v, ga] + ([gb] if two else [])
    sd = jax.ShapeDtypeStruct((L, rows, cols), F32)
    return pl.pallas_call(
        body, name=name, grid=(L, rows // tr), in_specs=[spec] * len(args), out_specs=[spec] * 4,
        out_shape=[sd] * 4, compiler_params=_cp(("parallel", "parallel"), 24 * tr * cols * 4),
    )(*args)


def _exchange(arrays, *, flips, n_slots, slot_of, scatter, self_copy, name):
    n = len(arrays)
    nf = len(flips)
    out_shapes = [jax.ShapeDtypeStruct(a.shape if scatter else (n_slots,) + a.shape, a.dtype) for a in arrays]

    def body(*refs):
        ins, outs = refs[:n], refs[n:2 * n]
        send_sems, recv_sems, loc_sems = refs[2 * n:]
        x, y, c = lax.axis_index("x"), lax.axis_index("y"), lax.axis_index("c")
        me = slot_of(x, y, c)
        peers = [(1 - x if fx else x, 1 - y if fy else y, 1 - c if fc else c) for fx, fy, fc in flips]
        local, remote = [], []
        for a in range(n):
            if self_copy:
                lc = pltpu.make_async_copy(ins[a].at[me] if scatter else ins[a], outs[a].at[me], loc_sems.at[a])
                lc.start()
                local.append(lc)
            for k, p in enumerate(peers):
                src = ins[a].at[slot_of(*p)] if scatter else ins[a]
                cp = pltpu.make_async_remote_copy(
                    src_ref=src, dst_ref=outs[a].at[me], send_sem=send_sems.at[a, k], recv_sem=recv_sems.at[a, k],
                    device_id=p, device_id_type=MESH)
                cp.start()
                remote.append(cp)
        for a in range(n):
            for k, p in enumerate(peers):
                src = ins[a].at[slot_of(*p)] if scatter else ins[a]
                pltpu.make_async_remote_copy(
                    src_ref=src, dst_ref=outs[a].at[slot_of(*p)], send_sem=send_sems.at[a, k],
                    recv_sem=recv_sems.at[a, k], device_id=p, device_id_type=MESH).wait_recv()
        for cp in remote:
            cp.wait_send()
        for lc in local:
            lc.wait()

    return pl.pallas_call(
        body, name=name, in_specs=[ANY] * n, out_specs=[ANY] * n, out_shape=out_shapes,
        scratch_shapes=[pltpu.SemaphoreType.DMA((n, nf)), pltpu.SemaphoreType.DMA((n, nf)),
                        pltpu.SemaphoreType.DMA((n,))],
        compiler_params=pltpu.CompilerParams(has_side_effects=True),
    )(*arrays)


CHIP_FLIPS = [(1, 0, 0), (0, 1, 0), (1, 1, 0)]
ALL_FLIPS = [(fx, fy, fc) for fx in (0, 1) for fy in (0, 1) for fc in (0, 1) if fx or fy or fc]


def _chip_slot(x, y, c):
    return 2 * x + y


def _dev_slot(x, y, c):
    return 4 * x + 2 * y + c


def _zero_slot(x, y, c):
    return 0


def _pack(arrs):
    flat = jnp.concatenate([a.reshape(-1).astype(F32) for a in arrs])
    tile = SUBLANES * LANES
    pad = (-flat.shape[0]) % tile
    return jnp.pad(flat, (0, pad)).reshape(-1, LANES)


def _unpack(buf, shapes):
    flat = buf.reshape(-1)
    out, off = [], 0
    for s in shapes:
        n = int(np.prod(s))
        out.append(flat[off:off + n].reshape(s))
        off += n
    return out


def _lower_bounds(hg_lower_bounds):
    p = jax.nn.softmax(hg_lower_bounds.astype(F32), axis=0)
    return jnp.cumsum(p, axis=0) - p[0]


def kernel(x, meta_tokens, hg_lower_bounds, w_in, w_conv, w_pool, pool_scale, hg_norm_g, w_o, ln1_g, ln1_b, w_up, w_ffn_conv, b_ffn_conv, w_down, ln2_g, ln2_b, loss_target, m_meta_tokens, m_hg_lower_bounds, m_w_in, m_w_conv, m_w_pool, m_pool_scale, m_hg_norm_g, m_w_o, m_ln1_g, m_ln1_b, m_w_up, m_w_ffn_conv, m_b_ffn_conv, m_w_down, m_ln2_g, m_ln2_b, v_meta_tokens, v_hg_lower_bounds, v_w_in, v_w_conv, v_w_pool, v_pool_scale, v_hg_norm_g, v_w_o, v_ln1_g, v_ln1_b, v_w_up, v_w_ffn_conv, v_b_ffn_conv, v_w_down, v_ln2_g, v_ln2_b):
    S = x.shape[1]
    R = S + ROW0
    Fq = w_down.shape[1]
    F = 4 * Fq
    F2 = 2 * F
    F2q = w_up.shape[2]
    assert x.shape == (1, S, D) and R % 384 == 0 and S % ROW0 == 0
    chip = 2 * lax.axis_index("x") + lax.axis_index("y")
    tm = 384
    tb_ffn = 128

    small_shapes = [(N_META, D // 4), (DEPTH, CW // 4, 3), (DEPTH, F2q, 3)]
    g_in, g_o, g_up, g_down, g_small = _exchange(
        [w_in.astype(MX), w_o.astype(MX), w_up.astype(MX), w_down.astype(MX),
         _pack([meta_tokens, w_conv, w_ffn_conv])],
        flips=CHIP_FLIPS, n_slots=4, slot_of=_chip_slot, scatter=False, self_copy=True, name="gather_weights")
    Win = g_in.transpose(1, 2, 0, 3).reshape(DEPTH, D, DIN)
    Wo = g_o.transpose(1, 0, 2, 3).reshape(DEPTH, D, D)
    Wup = g_up.transpose(1, 2, 0, 3).reshape(DEPTH, D, F2)
    Wdown = g_down.transpose(1, 0, 2, 3).reshape(DEPTH, F, D)
    sm = [_unpack(g_small[k], small_shapes) for k in range(4)]
    meta_full = jnp.concatenate([sm[k][0] for k in range(4)], axis=1)
    wconv_full = jnp.concatenate([sm[k][1] for k in range(4)], axis=1)
    wffn_full = jnp.concatenate([sm[k][2] for k in range(4)], axis=1)
    taps_c = jnp.pad(wconv_full.transpose(0, 2, 1), ((0, 0), (0, SUBLANES - 3), (0, 0)))
    taps_f = jnp.pad(wffn_full.transpose(0, 2, 1), ((0, 0), (0, SUBLANES - 3), (0, 0)))
    wbd = jnp.stack([jax.scipy.linalg.block_diag(*[w_pool[l, g] for g in range(4)]) for l in range(DEPTH)]).astype(MX)
    lbs, lbs_vjp = jax.vjp(_lower_bounds, hg_lower_bounds)
    mat, suf = _hg_consts()

    X = jnp.concatenate([jnp.zeros((PADR, D), F32), meta_full, x[0]], axis=0)
    saved = []
    for l in range(DEPTH):
        h = _mm(X, Win[l], tm=tm, name=f"mm_in_{l}")
        y, o, sp = _mixer_fwd(h, taps_c[l], wbd[l], pool_scale[l].reshape(1, PW), hg_norm_g[l].reshape(1, HD),
                              lbs[l].reshape(1, HW), mat, name=f"mixer_fwd_{l}")
        x1, xh1, r1 = _mm_ln(y, Wo[l], X, ln1_g[l], ln1_b[l], tm=tm, name=f"mm_o_ln_{l}")
        up = _mm(x1, Wup[l], tm=tm // 2, name=f"mm_up_{l}")
        a = _ffn_fwd(up, taps_f[l], b_ffn_conv[l].reshape(1, F2), tb=tb_ffn, name=f"ffn_fwd_{l}")
        x2, xh2, r2 = _mm_ln(a, Wdown[l], x1, ln2_g[l], ln2_b[l], tm=tm, name=f"mm_down_ln_{l}")
        saved.append((X, h, y, o, sp, x1, xh1, r1, up, a, xh2, r2))
        X = x2

    dxo, lacc = _loss(X, loss_target[0], name="loss")
    loss = lax.psum(0.5 * jnp.sum(lacc[0]) / D, ("x", "y", "c"))

    gW_in = gW_o = gW_up = gW_down = None
    small_g = [None] * DEPTH
    for l in reversed(range(DEPTH)):
        X, h, y, o, sp, x1, xh1, r1, up, a, xh2, r2 = saved[l]
        dz2, gb2 = _ln_bwd(dxo, xh2, r2, ln2_g[l], tm=tm, name=f"ln2_bwd_{l}")
        da = _mm(dz2, Wdown[l], nt=True, tm=tm, name=f"mm_da_{l}")
        gW_down = _wgrad(a, dz2, layer=l, prev=gW_down, slabs_on_cols=False, tm=tm, tn=512, name=f"wgrad_down_{l}")
        dup, facc = _ffn_bwd(da, up, taps_f[l], b_ffn_conv[l].reshape(1, F2), tb=tb_ffn, name=f"ffn_bwd_{l}")
        dx1 = _mm(dup, Wup[l], nt=True, res=dz2, tm=tm, name=f"mm_dx1_{l}")
        gW_up = _wgrad(x1, dup, layer=l, prev=gW_up, slabs_on_cols=True, tm=tm, tn=F2q, name=f"wgrad_up_{l}")
        dz1, gb1 = _ln_bwd(dx1, xh1, r1, ln1_g[l], tm=tm, name=f"ln1_bwd_{l}")
        dym = _mm(dz1, Wo[l], nt=True, tm=tm, name=f"mm_dym_{l}")
        gW_o = _wgrad(y, dz1, layer=l, prev=gW_o, slabs_on_cols=False, tm=tm, tn=512, name=f"wgrad_o_{l}")
        dh, dwc, dsc, dgn, dlb, dwbd = _mixer_bwd(
            h, dym, o, sp, taps_c[l], wbd[l], pool_scale[l].reshape(1, PW), hg_norm_g[l].reshape(1, HD),
            lbs[l].reshape(1, HW), mat, suf, name=f"mixer_bwd_{l}")
        dX = _mm(dh, Win[l], nt=True, res=dz1, tm=tm, name=f"mm_dx_{l}")
        gW_in = _wgrad(X, dh, layer=l, prev=gW_in, slabs_on_cols=True, tm=tm, tn=DIN // 4, name=f"wgrad_in_{l}")
        small_g[l] = dict(
            lbs=dlb[0], w_conv=dwc[0:3].T, w_pool=jnp.stack([dwbd[64 * g:64 * g + 64, 64 * g:64 * g + 64] for g in range(4)]),
            pool_scale=dsc[0], hg_norm_g=dgn[0], ln1_g=gb1[0], ln1_b=gb1[1], w_ffn_conv=facc[0:3].T,
            b_ffn_conv=facc[3], ln2_g=gb2[0], ln2_b=gb2[1])
        dxo = dX
    grad_x = dxo[ROW0:][None]

    r_in, r_o, r_up, r_down = _exchange(
        [gW_in, gW_o, gW_up, gW_down], flips=CHIP_FLIPS, n_slots=4, slot_of=_chip_slot, scatter=True,
        self_copy=True, name="scatter_grads")
    part = [_sum_slots(r, name=f"sum_grads_{k}") for k, r in enumerate((r_in, r_o, r_up, r_down))]
    sib = _exchange(part, flips=[(0, 0, 1)], n_slots=1, slot_of=_zero_slot, scatter=False, self_copy=False,
                    name="swap_cores")
    big = {}
    for k, (nm, w, m, v) in enumerate((("w_in", w_in, m_w_in, v_w_in), ("w_o", w_o, m_w_o, v_w_o),
                                       ("w_up", w_up, m_w_up, v_w_up), ("w_down", w_down, m_w_down, v_w_down))):
        big[nm] = _adamw(w, m, v, part[k], sib[k][0], name=f"adamw_{nm}")

    sg_names = ["lbs", "w_conv", "w_pool", "pool_scale", "hg_norm_g", "ln1_g", "ln1_b", "w_ffn_conv",
                "b_ffn_conv", "ln2_g", "ln2_b"]
    sg_list = [dxo[PADR:ROW0]] + [jnp.stack([small_g[l][nm] for l in range(DEPTH)]) for nm in sg_names]
    sg_shapes = [a.shape for a in sg_list]
    packed = _pack(sg_list)
    (gathered,) = _exchange([packed], flips=ALL_FLIPS, n_slots=8, slot_of=_dev_slot, scatter=False, self_copy=True,
                            name="gather_small_grads")
    total = _sum_slots(gathered[:, None], name="sum_small_grads")[0]
    tot = dict(zip(["meta_tokens"] + sg_names, _unpack(total, sg_shapes)))
    (g_hg,) = lbs_vjp(tot["lbs"])
    small_grads = dict(
        meta_tokens=lax.dynamic_slice_in_dim(tot["meta_tokens"], chip * (D // 4), D // 4, axis=1),
        hg_lower_bounds=g_hg,
        w_conv=lax.dynamic_slice_in_dim(tot["w_conv"], chip * (CW // 4), CW // 4, axis=1),
        w_pool=tot["w_pool"], pool_scale=tot["pool_scale"], hg_norm_g=tot["hg_norm_g"],
        ln1_g=tot["ln1_g"], ln1_b=tot["ln1_b"],
        w_ffn_conv=lax.dynamic_slice_in_dim(tot["w_ffn_conv"], chip * F2q, F2q, axis=1),
        b_ffn_conv=tot["b_ffn_conv"], ln2_g=tot["ln2_g"], ln2_b=tot["ln2_b"])
    small_w = dict(meta_tokens=(meta_tokens, m_meta_tokens, v_meta_tokens),
                   hg_lower_bounds=(hg_lower_bounds, m_hg_lower_bounds, v_hg_lower_bounds),
                   w_conv=(w_conv, m_w_conv, v_w_conv), w_pool=(w_pool, m_w_pool, v_w_pool),
                   pool_scale=(pool_scale, m_pool_scale, v_pool_scale), hg_norm_g=(hg_norm_g, m_hg_norm_g, v_hg_norm_g),
                   ln1_g=(ln1_g, m_ln1_g, v_ln1_g), ln1_b=(ln1_b, m_ln1_b, v_ln1_b),
                   w_ffn_conv=(w_ffn_conv, m_w_ffn_conv, v_w_ffn_conv), b_ffn_conv=(b_ffn_conv, m_b_ffn_conv, v_b_ffn_conv),
                   ln2_g=(ln2_g, m_ln2_g, v_ln2_g), ln2_b=(ln2_b, m_ln2_b, v_ln2_b))
    names_s = list(small_w)
    shapes_s = [small_w[nm][0].shape for nm in names_s]
    pk = [_pack([small_w[nm][j] for nm in names_s])[None] for j in range(3)]
    pg = _pack([small_grads[nm] for nm in names_s])[None]
    outs_s = _adamw(pk[0], pk[1], pk[2], pg, None, name="adamw_small")
    small = {nm: [] for nm in names_s}
    for j in range(4):
        for nm, val in zip(names_s, _unpack(outs_s[j][0], shapes_s)):
            small[nm].append(val)

    order = ["meta_tokens", "hg_lower_bounds", "w_in", "w_conv", "w_pool", "pool_scale", "hg_norm_g", "w_o",
             "ln1_g", "ln1_b", "w_up", "w_ffn_conv", "b_ffn_conv", "w_down", "ln2_g", "ln2_b"]
    res = {nm: (big[nm] if nm in big else small[nm]) for nm in order}
    outs = [loss, grad_x]
    for j in range(4):
        outs += [res[nm][j] for nm in order]
    return tuple(outs)
```

```python
import functools

import numpy as np

import jax
import jax.numpy as jnp
from jax import lax
from jax.experimental import pallas as pl
from jax.experimental.pallas import tpu as pltpu

F32 = jnp.float32
BF = jnp.bfloat16
MX = jnp.bfloat16

D = 1024
CW = 256
HW = 512
HD = 128
NH = 4
PW = 256
DIN = 3072
N_META = 16
CH = 64
ROW0 = 256
PADR = ROW0 - N_META
HALO = 16
FH = 8
LEVELS = (32, 16, 8, 4, 2, 1)
DEPTH = 2
ALPHA = (2 * DEPTH) ** 0.25
LN_EPS = 1e-5
RMS_EPS = 1e-6
F_FLOOR = 1e-30
SCALE = HD ** -0.5
ADAM_LR, ADAM_B1, ADAM_B2, ADAM_EPS, ADAM_WD, ADAM_STEP = 0.001, 0.9, 0.999, 1e-08, 0.01, 10

VMEM_V7X = 64 * 2 ** 20
LANES = 128
SUBLANES = 8

NN = (((1,), (0,)), ((), ()))
NT = (((1,), (1,)), ((), ()))
TN = (((0,), (0,)), ((), ()))
MESH = pl.DeviceIdType.MESH
ANY = pl.BlockSpec(memory_space=pl.ANY)


def _dot(a, b, dn):
    return lax.dot_general(a.astype(MX), b.astype(MX), dn, preferred_element_type=F32)


def _cp(sem, est_bytes):
    lim = int(min(VMEM_V7X - 6 * 2 ** 20, max(32 * 2 ** 20, est_bytes)))
    return pltpu.CompilerParams(dimension_semantics=sem, vmem_limit_bytes=lim)


def _nbytes(shape, dtype):
    return int(np.prod(shape)) * jnp.dtype(dtype).itemsize


def _row_tile(rows, row_bytes, budget):
    best = SUBLANES
    for t in range(SUBLANES, rows + 1, SUBLANES):
        if rows % t == 0 and t * row_bytes <= budget:
            best = t
    return best


def _mm(a, w, *, nt=False, res=None, tm, out_dtype=F32, name):
    R, K = a.shape
    N = w.shape[0] if nt else w.shape[1]
    dn = NT if nt else NN

    def body(*refs):
        if res is None:
            a_ref, w_ref, o_ref = refs
        else:
            a_ref, w_ref, r_ref, o_ref = refs
        acc = _dot(a_ref[...], w_ref[...], dn)
        if res is not None:
            acc = acc + ALPHA * r_ref[...]
        o_ref[...] = acc.astype(out_dtype)

    in_specs = [pl.BlockSpec((tm, K), lambda i: (i, 0)), pl.BlockSpec(w.shape, lambda i: (0, 0))]
    args = [a, w]
    est = 2 * _nbytes((tm, K), a.dtype) + 2 * _nbytes(w.shape, w.dtype) + 3 * _nbytes((tm, N), F32)
    if res is not None:
        in_specs.append(pl.BlockSpec((tm, N), lambda i: (i, 0)))
        args.append(res)
        est += 2 * _nbytes((tm, N), F32)
    return pl.pallas_call(
        body, name=name, grid=(R // tm,), in_specs=in_specs,
        out_specs=pl.BlockSpec((tm, N), lambda i: (i, 0)),
        out_shape=jax.ShapeDtypeStruct((R, N), out_dtype),
        compiler_params=_cp(("parallel",), est + 4 * 2 ** 20),
    )(*args)


def _wgrad(a, b, *, slabs_on_cols, tm, tn, name):
    R, Ka = a.shape
    Nb = b.shape[1]
    if slabs_on_cols:
        out_shape = (4, Ka, Nb // 4)
        assert tn == Nb // 4
        out_spec = pl.BlockSpec((None, Ka, tn), lambda j, i: (j, 0, 0))
    else:
        out_shape = (4, Ka // 4, Nb)
        out_spec = pl.BlockSpec((4, Ka // 4, tn), lambda j, i: (0, 0, j))

    def body(a_ref, b_ref, o_ref):
        @pl.when(pl.program_id(1) == 0)
        def _():
            o_ref[...] = jnp.zeros(o_ref.shape, F32)

        acc = _dot(a_ref[...], b_ref[...], TN)
        o_ref[...] += acc.reshape(o_ref.shape)

    in_specs = [pl.BlockSpec((tm, Ka), lambda j, i: (i, 0)), pl.BlockSpec((tm, tn), lambda j, i: (i, j))]
    est = 2 * _nbytes((tm, Ka), a.dtype) + 2 * _nbytes((tm, tn), b.dtype) + 4 * _nbytes((Ka, tn), F32) \
        + _nbytes((tm, Ka), F32)
    return pl.pallas_call(
        body, name=name, grid=(Nb // tn, R // tm), in_specs=in_specs, out_specs=out_spec,
        out_shape=jax.ShapeDtypeStruct(out_shape, F32),
        compiler_params=_cp(("parallel", "arbitrary"), est + 4 * 2 ** 20),
    )(a, b)


def _mm_ln(a, w, xres, g, b, *, tm, name):
    R, K = a.shape

    def body(a_ref, w_ref, x_ref, g_ref, b_ref, xo_ref, xh_ref, r_ref):
        z = ALPHA * x_ref[...] + _dot(a_ref[...], w_ref[...], NN)
        mu = jnp.mean(z, axis=-1, keepdims=True)
        zc = z - mu
        var = jnp.mean(zc * zc, axis=-1, keepdims=True)
        r = lax.rsqrt(var + LN_EPS)
        xh = zc * r
        xh_ref[...] = xh
        r_ref[...] = r
        xo_ref[...] = xh * g_ref[...] + b_ref[...]

    row = lambda i: (i, 0)
    fix = lambda i: (0, 0)
    est = 2 * _nbytes((tm, K), a.dtype) + 2 * _nbytes(w.shape, w.dtype) + 10 * _nbytes((tm, D), F32)
    return pl.pallas_call(
        body, name=name, grid=(R // tm,),
        in_specs=[pl.BlockSpec((tm, K), row), pl.BlockSpec(w.shape, fix), pl.BlockSpec((tm, D), row),
                  pl.BlockSpec((1, D), fix), pl.BlockSpec((1, D), fix)],
        out_specs=[pl.BlockSpec((tm, D), row), pl.BlockSpec((tm, D), row), pl.BlockSpec((tm, 1), row)],
        out_shape=[jax.ShapeDtypeStruct((R, D), F32), jax.ShapeDtypeStruct((R, D), F32),
                   jax.ShapeDtypeStruct((R, 1), F32)],
        compiler_params=_cp(("parallel",), est + 4 * 2 ** 20),
    )(a, w, xres, g.reshape(1, D), b.reshape(1, D))


def _ln_bwd(dx, xh, r, g, *, tm, name):
    R = dx.shape[0]

    def body(dx_ref, xh_ref, r_ref, g_ref, dz_ref, gb_ref):
        @pl.when(pl.program_id(0) == 0)
        def _():
            gb_ref[...] = jnp.zeros(gb_ref.shape, F32)

        dxv = dx_ref[...]
        xhv = xh_ref[...]
        dyh = dxv * g_ref[...]
        m1 = jnp.mean(dyh, axis=-1, keepdims=True)
        m2 = jnp.mean(dyh * xhv, axis=-1, keepdims=True)
        dz_ref[...] = r_ref[...] * (dyh - m1 - xhv * m2)
        gb_ref[0:1, :] += jnp.sum(dxv * xhv, axis=0, keepdims=True)
        gb_ref[1:2, :] += jnp.sum(dxv, axis=0, keepdims=True)

    row = lambda i: (i, 0)
    fix = lambda i: (0, 0)
    return pl.pallas_call(
        body, name=name, grid=(R // tm,),
        in_specs=[pl.BlockSpec((tm, D), row), pl.BlockSpec((tm, D), row), pl.BlockSpec((tm, 1), row),
                  pl.BlockSpec((1, D), fix)],
        out_specs=[pl.BlockSpec((tm, D), row), pl.BlockSpec((SUBLANES, D), fix)],
        out_shape=[jax.ShapeDtypeStruct((R, D), F32), jax.ShapeDtypeStruct((SUBLANES, D), F32)],
        compiler_params=_cp(("arbitrary",), 12 * _nbytes((tm, D), F32)),
    )(dx, xh, r, g.reshape(1, D))


def _loss(x2, tgt, *, name):
    R = x2.shape[0]
    tb = ROW0

    def body(x_ref, t_ref, dy_ref, acc_ref):
        i = pl.program_id(0)

        @pl.when(i == 0)
        def _():
            acc_ref[...] = jnp.zeros(acc_ref.shape, F32)
            dy_ref[...] = jnp.zeros(dy_ref.shape, F32)

        @pl.when(i > 0)
        def _():
            err = x_ref[...] - t_ref[...]
            dy_ref[...] = err / D
            acc_ref[0:1, :] += jnp.sum(err * err, axis=0, keepdims=True)

    return pl.pallas_call(
        body, name=name, grid=(R // tb,),
        in_specs=[pl.BlockSpec((tb, D), lambda i: (i, 0)),
                  pl.BlockSpec((tb, D), lambda i: (jnp.maximum(i - 1, 0), 0))],
        out_specs=[pl.BlockSpec((tb, D), lambda i: (i, 0)), pl.BlockSpec((SUBLANES, D), lambda i: (0, 0))],
        out_shape=[jax.ShapeDtypeStruct((R, D), F32), jax.ShapeDtypeStruct((SUBLANES, D), F32)],
        compiler_params=_cp(("arbitrary",), 8 * _nbytes((tb, D), F32)),
    )(x2, tgt)


def _row_ids(start, n):
    return start + lax.broadcasted_iota(jnp.int32, (n, 1), 0)


def _ffn_fwd(up, taps, bias, *, tb, name):
    R, F2 = up.shape
    F = F2 // 2
    nh = tb // FH

    def body(u_ref, up_ref, w_ref, b_ref, a_ref):
        i = pl.program_id(0)
        row = _row_ids(i * tb - FH, tb + FH)
        ue = jnp.concatenate([up_ref[...], u_ref[...]], axis=0)
        ue = jnp.where(row >= PADR, ue, 0.0)
        n = tb + FH
        u = (w_ref[0:1, :] * pltpu.roll(ue, 2, 0) + w_ref[1:2, :] * pltpu.roll(ue, 1, 0)
             + w_ref[2:3, :] * ue + b_ref[...])[FH:n]
        gate = u[:, :F]
        val = u[:, F:]
        a_ref[...] = (gate * jax.nn.sigmoid(gate) * val).astype(a_ref.dtype)

    est = 2 * _nbytes((tb, F2), F32) * 4
    return pl.pallas_call(
        body, name=name, grid=(R // tb,),
        in_specs=[pl.BlockSpec((tb, F2), lambda i: (i, 0)),
                  pl.BlockSpec((FH, F2), lambda i: (jnp.maximum(i * nh - 1, 0), 0)),
                  pl.BlockSpec((SUBLANES, F2), lambda i: (0, 0)), pl.BlockSpec((1, F2), lambda i: (0, 0))],
        out_specs=pl.BlockSpec((tb, F), lambda i: (i, 0)),
        out_shape=jax.ShapeDtypeStruct((R, F), MX),
        compiler_params=_cp(("parallel",), est),
    )(up, up, taps, bias)


def _ffn_bwd(da, up, taps, bias, *, tb, name):
    R, F2 = up.shape
    F = F2 // 2
    nh = tb // FH
    last = R // FH - 1

    def body(da_ref, dan_ref, u_ref, up_ref, un_ref, w_ref, b_ref, du_ref, acc_ref):
        i = pl.program_id(0)

        @pl.when(i == 0)
        def _():
            acc_ref[...] = jnp.zeros(acc_ref.shape, F32)

        n = tb + 2 * FH
        row = _row_ids(i * tb - FH, n)
        ue = jnp.concatenate([up_ref[...], u_ref[...], un_ref[...]], axis=0)
        ue = jnp.where((row >= PADR) & (row < R), ue, 0.0)
        w0, w1, w2 = w_ref[0:1, :], w_ref[1:2, :], w_ref[2:3, :]
        x2 = pltpu.roll(ue, 2, 0)
        x1 = pltpu.roll(ue, 1, 0)
        u = (w0 * x2 + w1 * x1 + w2 * ue + b_ref[...])[FH:n]
        m = tb + FH
        rown = row[FH:n]
        dae = jnp.concatenate([da_ref[...], dan_ref[...]], axis=0)
        dae = jnp.where(rown < R, dae, 0.0)
        gate = u[:, :F]
        val = u[:, F:]
        sg = jax.nn.sigmoid(gate)
        dgate = dae * val * (sg * (1.0 + gate * (1.0 - sg)))
        dval = dae * (gate * sg)
        du = jnp.concatenate([dgate, dval], axis=1)
        dx = w2 * du + w1 * pltpu.roll(du, m - 1, 0) + w0 * pltpu.roll(du, m - 2, 0)
        dx = jnp.where(rown >= PADR, dx, 0.0)[0:tb]
        du_ref[...] = dx.astype(du_ref.dtype)
        duc = du[0:tb]
        acc_ref[0:1, :] += jnp.sum(duc * x2[FH:FH + tb], axis=0, keepdims=True)
        acc_ref[1:2, :] += jnp.sum(duc * x1[FH:FH + tb], axis=0, keepdims=True)
        acc_ref[2:3, :] += jnp.sum(duc * ue[FH:FH + tb], axis=0, keepdims=True)
        acc_ref[3:4, :] += jnp.sum(duc, axis=0, keepdims=True)

    prev = lambda i: (jnp.maximum(i * nh - 1, 0), 0)
    nxt = lambda i: (jnp.minimum(i * nh + nh, last), 0)
    cur = lambda i: (i, 0)
    fix = lambda i: (0, 0)
    est = 12 * _nbytes((tb + 2 * FH, F2), F32)
    return pl.pallas_call(
        body, name=name, grid=(R // tb,),
        in_specs=[pl.BlockSpec((tb, F), cur), pl.BlockSpec((FH, F), nxt),
                  pl.BlockSpec((tb, F2), cur), pl.BlockSpec((FH, F2), prev), pl.BlockSpec((FH, F2), nxt),
                  pl.BlockSpec((SUBLANES, F2), fix), pl.BlockSpec((1, F2), fix)],
        out_specs=[pl.BlockSpec((tb, F2), cur), pl.BlockSpec((SUBLANES, F2), fix)],
        out_shape=[jax.ShapeDtypeStruct((R, F2), MX), jax.ShapeDtypeStruct((SUBLANES, F2), F32)],
        compiler_params=_cp(("arbitrary",), est),
    )(da, da, up, up, up, taps, bias)


def _hg_consts():
    t = np.arange(CH)[:, None]
    j = np.arange(CH)[None, :]
    low = (j <= t).astype(np.float32)
    blocks = [low, (j > t).astype(np.float32)]
    for m in LEVELS:
        ref = (t // (2 * m)) * 2 * m + m - 1
        blocks.append(low - (j <= ref).astype(np.float32))
    mat = np.concatenate(blocks, axis=0)
    suf = (j >= t).astype(np.float32)
    return jnp.asarray(mat, BF), jnp.asarray(suf, BF)


def _split_dot(mat, x):
    hi = x.astype(BF)
    lo = (x - hi.astype(F32)).astype(BF)
    return (lax.dot_general(mat, hi, NN, preferred_element_type=F32)
            + lax.dot_general(mat, lo, NN, preferred_element_type=F32))


def _lane_select(a2, a4, a8, a16):
    lane = lax.broadcasted_iota(jnp.int32, (1, PW), 1)
    return jnp.where(lane < 64, a2, jnp.where(lane < 128, a4, jnp.where(lane < 192, a8, a16)))


def _pool_count(row):
    win = _lane_select(2.0, 4.0, 8.0, 16.0)
    t1 = jnp.maximum((row - PADR + 1).astype(F32), 1.0)
    return jnp.minimum(t1, win)


def _gates(fz, lb, valid):
    sig = jax.nn.sigmoid(fz)
    f = lb + (1.0 - lb) * sig
    lf = jnp.where(valid, jnp.log(jnp.maximum(f, F_FLOOR)), 0.0)
    kk = jnp.where(valid, (1.0 - lb) * (1.0 - sig), 0.0)
    return sig, f, lf, kk


def _level_masks():
    tt = lax.broadcasted_iota(jnp.int32, (CH, CH), 0)
    ss = lax.broadcasted_iota(jnp.int32, (CH, CH), 1)
    xr = tt ^ ss
    low = tt > ss
    return tt, ss, [(xr >= m) & (xr < 2 * m) & low for m in LEVELS]


def _intra(q, kk, ex, sl, tt, ss, masks):
    a = jnp.where(tt == ss, jnp.sum(q * kk, axis=-1, keepdims=True), 0.0)
    parts = []
    for l in range(len(LEVELS)):
        e = jnp.exp(-jnp.abs(ex[(2 + l) * CH:(3 + l) * CH, sl]))
        ql = (q * e).astype(MX)
        kl = (kk * e).astype(MX)
        a = a + jnp.where(masks[l], _dot(ql, kl, NT), 0.0)
        parts.append((e, ql, kl))
    return a, parts


def _mixer_fwd(h, taps, wbd, ps, gn, lb, mat, *, name):
    R = h.shape[0]
    nb = R // CH
    q4 = CH // HALO

    def body(h_ref, hp_ref, w_ref, wbd_ref, ps_ref, gn_ref, lb_ref, mat_ref, y_ref, o_ref, sp_ref, st_ref):
        i = pl.program_id(0)

        @pl.when(i == 0)
        def _():
            st_ref[...] = jnp.zeros(st_ref.shape, F32)

        row = _row_ids(i * CH, CH)
        valid = row >= PADR
        rowe = _row_ids(i * CH - HALO, CH + HALO)
        valide = rowe >= PADR
        se = jnp.concatenate([hp_ref[:, 256:512] * hp_ref[:, 512:768], h_ref[:, 256:512] * h_ref[:, 512:768]], axis=0)
        se = jnp.where(valide, se, 0.0)
        conv = (w_ref[0:1, :] * pltpu.roll(se, 2, 0) + w_ref[1:2, :] * pltpu.roll(se, 1, 0)
                + w_ref[2:3, :] * se)[HALO:]
        y_ref[:, 0:CW] = (h_ref[:, 0:256] * conv).astype(y_ref.dtype)
        ve = jnp.where(valide, jnp.concatenate([hp_ref[:, 2816:3072], h_ref[:, 2816:3072]], axis=0), 0.0)
        s2 = ve + pltpu.roll(ve, 1, 0)
        s4 = s2 + pltpu.roll(s2, 2, 0)
        s8 = s4 + pltpu.roll(s4, 4, 0)
        s16 = s8 + pltpu.roll(s8, 8, 0)
        dp = (_lane_select(s2, s4, s8, s16) / _pool_count(rowe) - ve)[HALO:]
        y_ref[:, CW + HW:D] = (_dot(dp, wbd_ref[...], NN) * ps_ref[...]).astype(y_ref.dtype)
        lbv = lb_ref[...]
        _, _, lf, kk = _gates(h_ref[:, 1280:1792], lbv, valid)
        q = h_ref[:, 768:1280] * SCALE
        ii = h_ref[:, 1792:2304]
        gz = h_ref[:, 2304:2816]
        ex = _split_dot(mat_ref[...], lf)
        eg = jnp.exp(ex[0:CH])
        q0 = q * eg
        kr = kk * jnp.exp(ex[CH:2 * CH])
        tt, ss, masks = _level_masks()
        for hd in range(NH):
            sl = slice(HD * hd, HD * (hd + 1))
            st = st_ref[hd]
            sp_ref[hd] = st
            a, _ = _intra(q[:, sl], kk[:, sl], ex, sl, tt, ss, masks)
            o = _dot(q0[:, sl], st, NT) + _dot(a, ii[:, sl], NN)
            st_ref[hd] = st * eg[CH - 1:CH, sl] + _dot(ii[:, sl], kr[:, sl], TN)
            o_ref[:, sl] = o
            r = lax.rsqrt(jnp.mean(o * o, axis=-1, keepdims=True) + RMS_EPS)
            g_ = gz[:, sl]
            y_ref[:, CW + HD * hd:CW + HD * (hd + 1)] = (
                o * r * gn_ref[...] * (g_ * jax.nn.sigmoid(g_))).astype(y_ref.dtype)

    fix = lambda i: (0, 0)
    return pl.pallas_call(
        body, name=name, grid=(nb,),
        in_specs=[pl.BlockSpec((CH, DIN), lambda i: (i, 0)),
                  pl.BlockSpec((HALO, DIN), lambda i: (jnp.maximum(i * q4 - 1, 0), 0)),
                  pl.BlockSpec((SUBLANES, CW), fix), pl.BlockSpec((PW, PW), fix), pl.BlockSpec((1, PW), fix),
                  pl.BlockSpec((1, HD), fix), pl.BlockSpec((1, HW), fix), pl.BlockSpec(mat.shape, fix)],
        out_specs=[pl.BlockSpec((CH, D), lambda i: (i, 0)), pl.BlockSpec((CH, HW), lambda i: (i, 0)),
                   pl.BlockSpec((None, NH, HD, HD), lambda i: (i, 0, 0, 0))],
        out_shape=[jax.ShapeDtypeStruct((R, D), MX), jax.ShapeDtypeStruct((R, HW), F32),
                   jax.ShapeDtypeStruct((nb, NH, HD, HD), F32)],
        scratch_shapes=[pltpu.VMEM((NH, HD, HD), F32)],
        compiler_params=_cp(("arbitrary",), 32 * 2 ** 20),
    )(h, h, taps, wbd, ps, gn, lb, mat)


def _mixer_bwd(h, dy, o, sp, taps, wbd, ps, gn, lb, mat, suf, *, name):
    R = h.shape[0]
    nb = R // CH
    q4 = CH // HALO
    lasth = R // HALO - 1

    def body(h_ref, hp_ref, hn_ref, dy_ref, dyn_ref, o_ref, sp_ref, w_ref, wbd_ref, ps_ref, gn_ref, lb_ref,
             mat_ref, suf_ref, dh_ref, dw_ref, dsc_ref, dgn_ref, dlb_ref, dwbd_ref, dst_ref):
        i = pl.program_id(0)
        b = nb - 1 - i

        @pl.when(i == 0)
        def _():
            dst_ref[...] = jnp.zeros(dst_ref.shape, F32)
            dw_ref[...] = jnp.zeros(dw_ref.shape, F32)
            dsc_ref[...] = jnp.zeros(dsc_ref.shape, F32)
            dgn_ref[...] = jnp.zeros(dgn_ref.shape, F32)
            dlb_ref[...] = jnp.zeros(dlb_ref.shape, F32)
            dwbd_ref[...] = jnp.zeros(dwbd_ref.shape, F32)

        ne = CH + 2 * HALO
        nc = CH + HALO
        row = _row_ids(b * CH, CH)
        valid = row >= PADR
        rowe = _row_ids(b * CH - HALO, ne)
        valide = (rowe >= PADR) & (rowe < R)
        rown = rowe[HALO:]
        validn = rown < R

        def cat3(lo, hi):
            return jnp.concatenate([hp_ref[:, lo:hi], h_ref[:, lo:hi], hn_ref[:, lo:hi]], axis=0)

        def catd(lo, hi):
            return jnp.where(validn, jnp.concatenate([dy_ref[:, lo:hi], dyn_ref[:, lo:hi]], axis=0), 0.0)

        w0, w1, w2 = w_ref[0:1, :], w_ref[1:2, :], w_ref[2:3, :]
        cce = cat3(256, 512)
        cve = cat3(512, 768)
        se = jnp.where(valide, cce * cve, 0.0)
        sm2 = pltpu.roll(se, 2, 0)
        sm1 = pltpu.roll(se, 1, 0)
        conv = (w0 * sm2 + w1 * sm1 + w2 * se)[HALO:HALO + CH]
        cbn = jnp.concatenate([h_ref[:, 0:256], hn_ref[:, 0:256]], axis=0)
        dconv = catd(0, CW) * cbn
        ds = w2 * dconv + w1 * pltpu.roll(dconv, nc - 1, 0) + w0 * pltpu.roll(dconv, nc - 2, 0)
        ds = jnp.where(valid, ds[0:CH], 0.0)
        dcv = dconv[0:CH]
        dw_ref[0:1, :] += jnp.sum(dcv * sm2[HALO:HALO + CH], axis=0, keepdims=True)
        dw_ref[1:2, :] += jnp.sum(dcv * sm1[HALO:HALO + CH], axis=0, keepdims=True)
        dw_ref[2:3, :] += jnp.sum(dcv * se[HALO:HALO + CH], axis=0, keepdims=True)
        dh_ref[:, 0:256] = jnp.where(valid, dy_ref[:, 0:CW] * conv, 0.0).astype(dh_ref.dtype)
        dh_ref[:, 256:512] = (ds * h_ref[:, 512:768]).astype(dh_ref.dtype)
        dh_ref[:, 512:768] = (ds * h_ref[:, 256:512]).astype(dh_ref.dtype)

        ve = jnp.where(valide, cat3(2816, 3072), 0.0)
        s2 = ve + pltpu.roll(ve, 1, 0)
        s4 = s2 + pltpu.roll(s2, 2, 0)
        s8 = s4 + pltpu.roll(s4, 4, 0)
        s16 = s8 + pltpu.roll(s8, 8, 0)
        cnte = _pool_count(rowe)
        dp = (_lane_select(s2, s4, s8, s16) / cnte - ve)[HALO:HALO + CH]
        dyp = catd(CW + HW, D)
        pre = _dot(dp, wbd_ref[...], NN)
        dsc_ref[0:1, :] += jnp.sum(dyp[0:CH] * pre, axis=0, keepdims=True)
        dyps = dyp * ps_ref[...]
        dd = _dot(dyps, wbd_ref[...], NT)
        dwbd_ref[...] += _dot(dp, dyps[0:CH], TN)
        e = dd / cnte[HALO:]
        t2 = e + pltpu.roll(e, nc - 1, 0)
        t4 = t2 + pltpu.roll(t2, nc - 2, 0)
        t8 = t4 + pltpu.roll(t4, nc - 4, 0)
        t16 = t8 + pltpu.roll(t8, nc - 8, 0)
        dv = (_lane_select(t2, t4, t8, t16) - dd)[0:CH]
        dh_ref[:, 2816:3072] = jnp.where(valid, dv, 0.0).astype(dh_ref.dtype)

        lbv = lb_ref[...]
        sig, f, lf, kk = _gates(h_ref[:, 1280:1792], lbv, valid)
        q = h_ref[:, 768:1280] * SCALE
        ii = h_ref[:, 1792:2304]
        gz = h_ref[:, 2304:2816]
        ex = _split_dot(mat_ref[...], lf)
        eg = jnp.exp(ex[0:CH])
        egr = jnp.exp(ex[CH:2 * CH])
        q0 = q * eg
        kr = kk * egr
        tt, ss, masks = _level_masks()
        gnv = gn_ref[...]
        dqs, dks, dis, dgzs, tails, dbs = [], [], [], [], [], []
        for hd in range(NH):
            sl = slice(HD * hd, HD * (hd + 1))
            ov = o_ref[:, sl]
            r = lax.rsqrt(jnp.mean(ov * ov, axis=-1, keepdims=True) + RMS_EPS)
            oh = ov * r
            g_ = gz[:, sl]
            sg = jax.nn.sigmoid(g_)
            dyv = dy_ref[:, CW + HD * hd:CW + HD * (hd + 1)]
            don = dyv * (g_ * sg)
            dgzs.append(dyv * (oh * gnv) * (sg * (1.0 + g_ * (1.0 - sg))))
            dgn_ref[0:1, :] += jnp.sum(don * oh, axis=0, keepdims=True)
            doh = don * gnv
            do = r * (doh - oh * jnp.mean(doh * oh, axis=-1, keepdims=True))
            st = sp_ref[hd]
            dst = dst_ref[hd]
            qh, kh, ih = q[:, sl], kk[:, sl], ii[:, sl]
            a, parts = _intra(qh, kh, ex, sl, tt, ss, masks)
            da = jnp.where(tt >= ss, _dot(do, ih, NT), 0.0)
            dis.append(_dot(a, do, TN) + _dot(kr[:, sl], dst, NT))
            q0h = q0[:, sl].astype(MX)
            krh = kr[:, sl].astype(MX)
            dq0 = _dot(do, st, NN)
            dkr = _dot(ih, dst, NN)
            dq = dq0 * eg[:, sl]
            dk = dkr * egr[:, sl]
            kdk = krh.astype(F32) * dkr
            db = q0h.astype(F32) * dq0 - kdk
            tails.append(jnp.sum(kdk, axis=0, keepdims=True)
                         + eg[CH - 1:CH, sl] * jnp.sum(dst * st, axis=0, keepdims=True))
            dga = jnp.sum(jnp.where(tt == ss, da, 0.0), axis=-1, keepdims=True)
            dq = dq + dga * kh
            dk = dk + dga * qh
            for l in range(len(LEVELS)):
                e_l, ql, kl = parts[l]
                dpl = jnp.where(masks[l], da, 0.0).astype(MX)
                dql = _dot(dpl, kl, NN)
                dkl = _dot(dpl, ql, TN)
                dq = dq + dql * e_l
                dk = dk + dkl * e_l
                db = db + (ql.astype(F32) * dql - kl.astype(F32) * dkl)
            dst_ref[hd] = dst * eg[CH - 1:CH, sl] + _dot(do, q0h, TN)
            dqs.append(dq)
            dks.append(dk)
            dbs.append(db)
        dq = jnp.concatenate(dqs, axis=1)
        dk = jnp.concatenate(dks, axis=1)
        db = jnp.concatenate(dbs, axis=1)
        dlf = _split_dot(suf_ref[...], db) + jnp.concatenate(tails, axis=1)
        t = jnp.where(valid, dlf * jnp.where(f > F_FLOOR, 1.0 / f, 0.0) - dk, 0.0)
        dlb_ref[0:1, :] += jnp.sum(t * (1.0 - sig), axis=0, keepdims=True)
        dh_ref[:, 768:1280] = jnp.where(valid, dq * SCALE, 0.0).astype(dh_ref.dtype)
        dh_ref[:, 1280:1792] = (t * (1.0 - lbv) * (sig * (1.0 - sig))).astype(dh_ref.dtype)
        dh_ref[:, 1792:2304] = jnp.where(valid, jnp.concatenate(dis, axis=1), 0.0).astype(dh_ref.dtype)
        dh_ref[:, 2304:2816] = jnp.where(valid, jnp.concatenate(dgzs, axis=1), 0.0).astype(dh_ref.dtype)

    cur = lambda i: (nb - 1 - i, 0)
    prev = lambda i: (jnp.maximum((nb - 1 - i) * q4 - 1, 0), 0)
    nxt = lambda i: (jnp.minimum((nb - 1 - i) * q4 + q4, lasth), 0)
    fix = lambda i: (0, 0)
    return pl.pallas_call(
        body, name=name, grid=(nb,),
        in_specs=[pl.BlockSpec((CH, DIN), cur), pl.BlockSpec((HALO, DIN), prev), pl.BlockSpec((HALO, DIN), nxt),
                  pl.BlockSpec((CH, D), cur), pl.BlockSpec((HALO, D), nxt), pl.BlockSpec((CH, HW), cur),
                  pl.BlockSpec((None, NH, HD, HD), lambda i: (nb - 1 - i, 0, 0, 0)),
                  pl.BlockSpec((SUBLANES, CW), fix), pl.BlockSpec((PW, PW), fix), pl.BlockSpec((1, PW), fix),
                  pl.BlockSpec((1, HD), fix), pl.BlockSpec((1, HW), fix), pl.BlockSpec(mat.shape, fix),
                  pl.BlockSpec(suf.shape, fix)],
        out_specs=[pl.BlockSpec((CH, DIN), cur), pl.BlockSpec((SUBLANES, CW), fix), pl.BlockSpec((SUBLANES, PW), fix),
                   pl.BlockSpec((SUBLANES, HD), fix), pl.BlockSpec((SUBLANES, HW), fix), pl.BlockSpec((PW, PW), fix)],
        out_shape=[jax.ShapeDtypeStruct((R, DIN), MX), jax.ShapeDtypeStruct((SUBLANES, CW), F32),
                   jax.ShapeDtypeStruct((SUBLANES, PW), F32), jax.ShapeDtypeStruct((SUBLANES, HD), F32),
                   jax.ShapeDtypeStruct((SUBLANES, HW), F32), jax.ShapeDtypeStruct((PW, PW), F32)],
        scratch_shapes=[pltpu.VMEM((NH, HD, HD), F32)],
        compiler_params=_cp(("arbitrary",), 40 * 2 ** 20),
    )(h, h, h, dy, dy, o, sp, taps, wbd, ps, gn, lb, mat, suf)


def _sum_slots(recv, *, name):
    S, L, rows, cols = recv.shape
    tr = _row_tile(rows, S * cols * 4, 6 * 2 ** 20)

    def body(r_ref, o_ref):
        acc = r_ref[0]
        for s in range(1, S):
            acc = acc + r_ref[s]
        o_ref[...] = acc

    return pl.pallas_call(
        body, name=name, grid=(L, rows // tr),
        in_specs=[pl.BlockSpec((S, None, tr, cols), lambda l, i: (0, l, i, 0))],
        out_specs=pl.BlockSpec((None, tr, cols), lambda l, i: (l, i, 0)),
        out_shape=jax.ShapeDtypeStruct((L, rows, cols), F32),
        compiler_params=_cp(("parallel", "parallel"), 4 * S * tr * cols * 4),
    )(recv)


def _sum_own(recv, own, chip, *, name):
    S, rows, cols = recv.shape
    tr = _row_tile(rows, S * cols * 4, 6 * 2 ** 20)

    def body(me_ref, r_ref, o_ref, out_ref):
        me = me_ref[0]
        acc = None
        for s in range(S):
            t = jnp.where(me == s, o_ref[...], r_ref[s])
            acc = t if acc is None else acc + t
        out_ref[...] = acc

    grid_spec = pltpu.PrefetchScalarGridSpec(
        num_scalar_prefetch=1, grid=(rows // tr,),
        in_specs=[pl.BlockSpec((S, tr, cols), lambda i, me: (0, i, 0)),
                  pl.BlockSpec((None, tr, cols), lambda i, me: (me[0], i, 0))],
        out_specs=pl.BlockSpec((tr, cols), lambda i, me: (i, 0)))
    return pl.pallas_call(
        body, name=name, grid_spec=grid_spec, out_shape=jax.ShapeDtypeStruct((rows, cols), F32),
        compiler_params=_cp(("parallel",), 5 * S * tr * cols * 4),
    )(chip.reshape(1).astype(jnp.int32), recv, own)


def _adamw(w, m, v, ga, gb, *, layer, prev, name):
    L, rows, cols = w.shape
    tr = _row_tile(rows, cols * 4, 2 ** 20)
    two = gb is not None
    nin = 5 if two else 4

    def body(*refs):
        w_ref, m_ref, v_ref, a_ref = refs[:4]
        g_ref, d_ref, m2_ref, v2_ref = refs[-4:]
        g = a_ref[...] + refs[4][...] if two else a_ref[...]
        m2 = ADAM_B1 * m_ref[...] + (1.0 - ADAM_B1) * g
        v2 = ADAM_B2 * v_ref[...] + (1.0 - ADAM_B2) * (g * g)
        m_hat = m2 / (1.0 - ADAM_B1 ** ADAM_STEP)
        v_hat = v2 / (1.0 - ADAM_B2 ** ADAM_STEP)
        g_ref[...] = g
        d_ref[...] = -ADAM_LR * (m_hat / (jnp.sqrt(v_hat) + ADAM_EPS) + ADAM_WD * w_ref[...])
        m2_ref[...] = m2
        v2_ref[...] = v2

    spec = pl.BlockSpec((None, tr, cols), lambda i: (layer, i, 0))
    gspec = pl.BlockSpec((tr, cols), lambda i: (i, 0))
    args = [w, m, v, ga] + ([gb] if two else [])
    in_specs = [spec] * 3 + [gspec] * (nin - 3)
    aliases = {}
    if prev is not None:
        args += list(prev)
        in_specs += [ANY] * 4
        aliases = {nin + j: j for j in range(4)}
    sd = jax.ShapeDtypeStruct((L, rows, cols), F32)
    return pl.pallas_call(
        body, name=name, grid=(rows // tr,), in_specs=in_specs, out_specs=[spec] * 4,
        out_shape=[sd] * 4, input_output_aliases=aliases,
        compiler_params=_cp(("parallel",), 24 * tr * cols * 4),
    )(*args)


def _exchange(arrays, *, flips, n_slots, slot_of, scatter, self_copy, name):
    n = len(arrays)
    nf = len(flips)
    out_shapes = [jax.ShapeDtypeStruct(a.shape if scatter else (n_slots,) + a.shape, a.dtype) for a in arrays]

    def body(*refs):
        ins, outs = refs[:n], refs[n:2 * n]
        send_sems, recv_sems, loc_sems = refs[2 * n:]
        x, y, c = lax.axis_index("x"), lax.axis_index("y"), lax.axis_index("c")
        me = slot_of(x, y, c)
        peers = [(1 - x if fx else x, 1 - y if fy else y, 1 - c if fc else c) for fx, fy, fc in flips]
        local, remote = [], []
        for a in range(n):
            if self_copy:
                lc = pltpu.make_async_copy(ins[a].at[me] if scatter else ins[a], outs[a].at[me], loc_sems.at[a])
                lc.start()
                local.append(lc)
            for k, p in enumerate(peers):
                src = ins[a].at[slot_of(*p)] if scatter else ins[a]
                cp = pltpu.make_async_remote_copy(
                    src_ref=src, dst_ref=outs[a].at[me], send_sem=send_sems.at[a, k], recv_sem=recv_sems.at[a, k],
                    device_id=p, device_id_type=MESH)
                cp.start()
                remote.append(cp)
        for a in range(n):
            for k, p in enumerate(peers):
                src = ins[a].at[slot_of(*p)] if scatter else ins[a]
                pltpu.make_async_remote_copy(
                    src_ref=src, dst_ref=outs[a].at[slot_of(*p)], send_sem=send_sems.at[a, k],
                    recv_sem=recv_sems.at[a, k], device_id=p, device_id_type=MESH).wait_recv()
        for cp in remote:
            cp.wait_send()
        for lc in local:
            lc.wait()

    return pl.pallas_call(
        body, name=name, in_specs=[ANY] * n, out_specs=[ANY] * n, out_shape=out_shapes,
        scratch_shapes=[pltpu.SemaphoreType.DMA((n, nf)), pltpu.SemaphoreType.DMA((n, nf)),
                        pltpu.SemaphoreType.DMA((n,))],
        compiler_params=pltpu.CompilerParams(has_side_effects=True),
    )(*arrays)


CHIP_FLIPS = [(1, 0, 0), (0, 1, 0), (1, 1, 0)]
ALL_FLIPS = [(fx, fy, fc) for fx in (0, 1) for fy in (0, 1) for fc in (0, 1) if fx or fy or fc]


def _chip_slot(x, y, c):
    return 2 * x + y


def _dev_slot(x, y, c):
    return 4 * x + 2 * y + c


def _zero_slot(x, y, c):
    return 0


HBM_SPEC = pl.BlockSpec(memory_space=pltpu.HBM)
SEM_SPEC = pl.BlockSpec(memory_space=pltpu.SEMAPHORE)
N_PEER_CHIPS = 3


def _peer_chips(x, y, c):
    return [(1 - x, y, c), (x, 1 - y, c), (1 - x, 1 - y, c)]


def _send_start(srcs, *, scatter, name):
    n = len(srcs)
    nc = n * N_PEER_CHIPS
    srcs = [pltpu.with_memory_space_constraint(s, pltpu.HBM) for s in srcs]
    land_shapes = [s.shape if scatter else (4,) + s.shape for s in srcs]
    lands = [pltpu.with_memory_space_constraint(lax.empty(sh, s.dtype), pltpu.HBM) for sh, s in zip(land_shapes, srcs)]

    def body(*refs):
        ins, lnd = refs[:n], refs[n:2 * n]
        send_sems, recv_sems = refs[2 * n:2 * n + nc], refs[2 * n + nc:2 * n + 2 * nc]
        token = refs[-1]
        x, y, c = lax.axis_index("x"), lax.axis_index("y"), lax.axis_index("c")
        me = 2 * x + y
        for a in range(n):
            for k, p in enumerate(_peer_chips(x, y, c)):
                src = ins[a].at[2 * p[0] + p[1]] if scatter else ins[a]
                j = a * N_PEER_CHIPS + k
                pltpu.make_async_remote_copy(
                    src_ref=src, dst_ref=lnd[a].at[me], send_sem=send_sems[j], recv_sem=recv_sems[j],
                    device_id=p, device_id_type=MESH).start()
        token[...] = jnp.zeros(token.shape, token.dtype)

    sem = pltpu.SemaphoreType.DMA(())
    outs = pl.pallas_call(
        body, name=name,
        out_shape=(*[sem] * (2 * nc), *[pltpu.HBM(s.shape, s.dtype) for s in srcs],
                   *[pltpu.HBM(sh, s.dtype) for sh, s in zip(land_shapes, srcs)],
                   jax.ShapeDtypeStruct((SUBLANES, LANES), F32)),
        in_specs=[HBM_SPEC] * (2 * n),
        out_specs=(*[SEM_SPEC] * (2 * nc), *[HBM_SPEC] * (2 * n), pl.BlockSpec(memory_space=pltpu.VMEM)),
        input_output_aliases={i: 2 * nc + i for i in range(2 * n)},
        compiler_params=pltpu.CompilerParams(has_side_effects=pltpu.SideEffectType.DATAFLOW_SIDE_EFFECTING),
    )(*srcs, *lands)
    return dict(sems=list(outs[:2 * nc]), srcs=list(outs[2 * nc:2 * nc + n]),
                lands=list(outs[2 * nc + n:2 * nc + 2 * n]), token=outs[-1])


def _send_wait(h, *, scatter, after, name):
    n = len(h["srcs"])
    nc = n * N_PEER_CHIPS

    def body(*refs):
        ins, lnd = refs[:n], refs[n:2 * n]
        send_sems, recv_sems = refs[2 * n:2 * n + nc], refs[2 * n + nc:2 * n + 2 * nc]
        x, y, c = lax.axis_index("x"), lax.axis_index("y"), lax.axis_index("c")
        for a in range(n):
            for k, p in enumerate(_peer_chips(x, y, c)):
                slot = 2 * p[0] + p[1]
                j = a * N_PEER_CHIPS + k
                cp = pltpu.make_async_remote_copy(
                    src_ref=ins[a].at[slot] if scatter else ins[a], dst_ref=lnd[a].at[slot],
                    send_sem=send_sems[j], recv_sem=recv_sems[j], device_id=p, device_id_type=MESH)
                cp.wait_send()
                cp.wait_recv()

    thru = h["srcs"] + h["lands"]
    outs = pl.pallas_call(
        body, name=name, out_shape=tuple(pltpu.HBM(t.shape, t.dtype) for t in thru),
        in_specs=[HBM_SPEC] * (2 * n) + [SEM_SPEC] * (2 * nc) + [ANY], out_specs=tuple([HBM_SPEC] * (2 * n)),
        input_output_aliases={i: i for i in range(2 * n)},
        compiler_params=pltpu.CompilerParams(has_side_effects=pltpu.SideEffectType.DATAFLOW_SIDE_EFFECTING),
    )(*thru, *h["sems"], after)
    return list(outs[:n]), list(outs[n:])


def _tie(x, dep):
    return lax.optimization_barrier((x, dep))[0]


def _assemble(land, own, chip, axis):
    return jnp.concatenate([jnp.where(chip == k, own, land[k]) for k in range(4)], axis=axis)


def _pack(arrs):
    flat = jnp.concatenate([a.reshape(-1).astype(F32) for a in arrs])
    tile = SUBLANES * LANES
    pad = (-flat.shape[0]) % tile
    return jnp.pad(flat, (0, pad)).reshape(-1, LANES)


def _unpack(buf, shapes):
    flat = buf.reshape(-1)
    out, off = [], 0
    for s in shapes:
        n = int(np.prod(s))
        out.append(flat[off:off + n].reshape(s))
        off += n
    return out


def _lower_bounds(hg_lower_bounds):
    p = jax.nn.softmax(hg_lower_bounds.astype(F32), axis=0)
    return jnp.cumsum(p, axis=0) - p[0]


def kernel(x, meta_tokens, hg_lower_bounds, w_in, w_conv, w_pool, pool_scale, hg_norm_g, w_o, ln1_g, ln1_b, w_up, w_ffn_conv, b_ffn_conv, w_down, ln2_g, ln2_b, loss_target, m_meta_tokens, m_hg_lower_bounds, m_w_in, m_w_conv, m_w_pool, m_pool_scale, m_hg_norm_g, m_w_o, m_ln1_g, m_ln1_b, m_w_up, m_w_ffn_conv, m_b_ffn_conv, m_w_down, m_ln2_g, m_ln2_b, v_meta_tokens, v_hg_lower_bounds, v_w_in, v_w_conv, v_w_pool, v_pool_scale, v_hg_norm_g, v_w_o, v_ln1_g, v_ln1_b, v_w_up, v_w_ffn_conv, v_b_ffn_conv, v_w_down, v_ln2_g, v_ln2_b):
    S = x.shape[1]
    R = S + ROW0
    Fq = w_down.shape[1]
    F = 4 * Fq
    F2 = 2 * F
    F2q = w_up.shape[2]
    assert x.shape == (1, S, D) and R % 384 == 0 and S % ROW0 == 0
    chip = 2 * lax.axis_index("x") + lax.axis_index("y")
    tm = 384
    tb_ffn = 128

    small_shapes = [(N_META, D // 4), (DEPTH, CW // 4, 3), (DEPTH, F2q, 3)]
    wb_in, wb_o, wb_up, wb_down = (w.astype(MX) for w in (w_in, w_o, w_up, w_down))
    h_a = _send_start([wb_in[0], _pack([meta_tokens, w_conv, w_ffn_conv])], scatter=False, name="gather_a_start")
    (own_in, own_small), (l_in, l_small) = _send_wait(h_a, scatter=False, after=h_a["token"], name="gather_a_wait")
    h_b = _send_start([_tie(wb_o[0], l_in), wb_up[0], wb_down[0]], scatter=False, name="gather_b_start")
    Win, Wo, Wup, Wdown = {}, {}, {}, {}
    Win[0] = _assemble(l_in, own_in, chip, 1)
    sm = [_unpack(jnp.where(chip == k, own_small, l_small[k]), small_shapes) for k in range(4)]
    meta_full = jnp.concatenate([sm[k][0] for k in range(4)], axis=1)
    wconv_full = jnp.concatenate([sm[k][1] for k in range(4)], axis=1)
    wffn_full = jnp.concatenate([sm[k][2] for k in range(4)], axis=1)
    taps_c = jnp.pad(wconv_full.transpose(0, 2, 1), ((0, 0), (0, SUBLANES - 3), (0, 0)))
    taps_f = jnp.pad(wffn_full.transpose(0, 2, 1), ((0, 0), (0, SUBLANES - 3), (0, 0)))
    wbd = jnp.stack([jax.scipy.linalg.block_diag(*[w_pool[l, g] for g in range(4)]) for l in range(DEPTH)]).astype(MX)
    lbs, lbs_vjp = jax.vjp(_lower_bounds, hg_lower_bounds)
    mat, suf = _hg_consts()

    def fwd_mixer(l, X):
        h = _mm(X, Win[l], tm=tm, name=f"mm_in_{l}")
        y, o, sp = _mixer_fwd(h, taps_c[l], wbd[l], pool_scale[l].reshape(1, PW), hg_norm_g[l].reshape(1, HD),
                              lbs[l].reshape(1, HW), mat, name=f"mixer_fwd_{l}")
        return h, y, o, sp

    def fwd_rest(l, X, h, y, o, sp):
        x1, xh1, r1 = _mm_ln(y, Wo[l], X, ln1_g[l], ln1_b[l], tm=tm, name=f"mm_o_ln_{l}")
        up = _mm(x1, Wup[l], tm=tm // 2, name=f"mm_up_{l}")
        a = _ffn_fwd(up, taps_f[l], b_ffn_conv[l].reshape(1, F2), tb=tb_ffn, name=f"ffn_fwd_{l}")
        x2, xh2, r2 = _mm_ln(a, Wdown[l], x1, ln2_g[l], ln2_b[l], tm=tm, name=f"mm_down_ln_{l}")
        return (X, h, y, o, sp, x1, xh1, r1, up, a, xh2, r2), x2

    X = jnp.concatenate([jnp.zeros((PADR, D), F32), meta_full, x[0]], axis=0)
    X = _tie(X, h_b["token"])
    h, y, o, sp = fwd_mixer(0, X)
    (own_o, own_up, own_down), (l_o, l_up, l_down) = _send_wait(h_b, scatter=False, after=y, name="gather_b_wait")
    Wo[0], Wup[0], Wdown[0] = (_assemble(l_o, own_o, chip, 0), _assemble(l_up, own_up, chip, 1),
                               _assemble(l_down, own_down, chip, 0))
    h_c = _send_start([_tie(wb_in[1], l_o), wb_o[1], wb_up[1], wb_down[1]], scatter=False, name="gather_c_start")
    saved0, X1 = fwd_rest(0, X, h, _tie(y, h_c["token"]), o, sp)
    own_c, l_c = _send_wait(h_c, scatter=False, after=X1, name="gather_c_wait")
    Win[1], Wo[1] = _assemble(l_c[0], own_c[0], chip, 1), _assemble(l_c[1], own_c[1], chip, 0)
    Wup[1], Wdown[1] = _assemble(l_c[2], own_c[2], chip, 1), _assemble(l_c[3], own_c[3], chip, 0)
    saved1, X2 = fwd_rest(1, X1, *fwd_mixer(1, X1))
    saved = [saved0, saved1]

    dxo, lacc = _loss(X2, loss_target[0], name="loss")
    loss = lax.psum(0.5 * jnp.sum(lacc[0]) / D, ("x", "y", "c"))

    sc = {}

    def scatter(nm, l, g, then):
        sc[nm, l] = _send_start([g], scatter=True, name=f"scatter_{nm}_{l}_start")
        return _tie(then, sc[nm, l]["token"])

    small_g = [None] * DEPTH
    for l in reversed(range(DEPTH)):
        X, h, y, o, sp, x1, xh1, r1, up, a, xh2, r2 = saved[l]
        dz2, gb2 = _ln_bwd(dxo, xh2, r2, ln2_g[l], tm=tm, name=f"ln2_bwd_{l}")
        da = _mm(dz2, Wdown[l], nt=True, tm=tm, name=f"mm_da_{l}")
        da = scatter("w_down", l, _wgrad(a, dz2, slabs_on_cols=False, tm=tm, tn=512, name=f"wgrad_down_{l}"), da)
        dup, facc = _ffn_bwd(da, up, taps_f[l], b_ffn_conv[l].reshape(1, F2), tb=tb_ffn, name=f"ffn_bwd_{l}")
        dx1 = _mm(dup, Wup[l], nt=True, res=dz2, tm=tm, name=f"mm_dx1_{l}")
        dx1 = scatter("w_up", l, _wgrad(x1, dup, slabs_on_cols=True, tm=tm, tn=F2q, name=f"wgrad_up_{l}"), dx1)
        dz1, gb1 = _ln_bwd(dx1, xh1, r1, ln1_g[l], tm=tm, name=f"ln1_bwd_{l}")
        dym = _mm(dz1, Wo[l], nt=True, tm=tm, name=f"mm_dym_{l}")
        dym = scatter("w_o", l, _wgrad(y, dz1, slabs_on_cols=False, tm=tm, tn=512, name=f"wgrad_o_{l}"), dym)
        dh, dwc, dsc, dgn, dlb, dwbd = _mixer_bwd(
            h, dym, o, sp, taps_c[l], wbd[l], pool_scale[l].reshape(1, PW), hg_norm_g[l].reshape(1, HD),
            lbs[l].reshape(1, HW), mat, suf, name=f"mixer_bwd_{l}")
        dX = _mm(dh, Win[l], nt=True, res=dz1, tm=tm, name=f"mm_dx_{l}")
        dX = scatter("w_in", l, _wgrad(X, dh, slabs_on_cols=True, tm=tm, tn=DIN // 4, name=f"wgrad_in_{l}"), dX)
        small_g[l] = dict(
            lbs=dlb[0], w_conv=dwc[0:3].T, w_pool=jnp.stack([dwbd[64 * g:64 * g + 64, 64 * g:64 * g + 64] for g in range(4)]),
            pool_scale=dsc[0], hg_norm_g=dgn[0], ln1_g=gb1[0], ln1_b=gb1[1], w_ffn_conv=facc[0:3].T,
            b_ffn_conv=facc[3], ln2_g=gb2[0], ln2_b=gb2[1])
        dxo = dX
    grad_x = dxo[ROW0:][None]

    big_w = dict(w_in=(w_in, m_w_in, v_w_in), w_o=(w_o, m_w_o, v_w_o), w_up=(w_up, m_w_up, v_w_up),
                 w_down=(w_down, m_w_down, v_w_down))
    big = {nm: None for nm in big_w}
    for l in reversed(range(DEPTH)):
        part = []
        for nm in big_w:
            (own,), (recv,) = _send_wait(sc[nm, l], scatter=True, after=dxo, name=f"scatter_{nm}_{l}_wait")
            part.append(_sum_own(recv, own, chip, name=f"sum_{nm}_{l}"))
        sib = _exchange(part, flips=[(0, 0, 1)], n_slots=1, slot_of=_zero_slot, scatter=False, self_copy=False,
                        name=f"swap_cores_{l}")
        for k, nm in enumerate(big_w):
            w, m, v = big_w[nm]
            big[nm] = _adamw(w, m, v, part[k], sib[k][0], layer=l, prev=big[nm], name=f"adamw_{nm}_{l}")

    sg_names = ["lbs", "w_conv", "w_pool", "pool_scale", "hg_norm_g", "ln1_g", "ln1_b", "w_ffn_conv",
                "b_ffn_conv", "ln2_g", "ln2_b"]
    sg_list = [dxo[PADR:ROW0]] + [jnp.stack([small_g[l][nm] for l in range(DEPTH)]) for nm in sg_names]
    sg_shapes = [a.shape for a in sg_list]
    packed = _pack(sg_list)
    (gathered,) = _exchange([packed], flips=ALL_FLIPS, n_slots=8, slot_of=_dev_slot, scatter=False, self_copy=True,
                            name="gather_small_grads")
    total = _sum_slots(gathered[:, None], name="sum_small_grads")[0]
    tot = dict(zip(["meta_tokens"] + sg_names, _unpack(total, sg_shapes)))
    (g_hg,) = lbs_vjp(tot["lbs"])
    small_grads = dict(
        meta_tokens=lax.dynamic_slice_in_dim(tot["meta_tokens"], chip * (D // 4), D // 4, axis=1),
        hg_lower_bounds=g_hg,
        w_conv=lax.dynamic_slice_in_dim(tot["w_conv"], chip * (CW // 4), CW // 4, axis=1),
        w_pool=tot["w_pool"], pool_scale=tot["pool_scale"], hg_norm_g=tot["hg_norm_g"],
        ln1_g=tot["ln1_g"], ln1_b=tot["ln1_b"],
        w_ffn_conv=lax.dynamic_slice_in_dim(tot["w_ffn_conv"], chip * F2q, F2q, axis=1),
        b_ffn_conv=tot["b_ffn_conv"], ln2_g=tot["ln2_g"], ln2_b=tot["ln2_b"])
    small_w = dict(meta_tokens=(meta_tokens, m_meta_tokens, v_meta_tokens),
                   hg_lower_bounds=(hg_lower_bounds, m_hg_lower_bounds, v_hg_lower_bounds),
                   w_conv=(w_conv, m_w_conv, v_w_conv), w_pool=(w_pool, m_w_pool, v_w_pool),
                   pool_scale=(pool_scale, m_pool_scale, v_pool_scale), hg_norm_g=(hg_norm_g, m_hg_norm_g, v_hg_norm_g),
                   ln1_g=(ln1_g, m_ln1_g, v_ln1_g), ln1_b=(ln1_b, m_ln1_b, v_ln1_b),
                   w_ffn_conv=(w_ffn_conv, m_w_ffn_conv, v_w_ffn_conv), b_ffn_conv=(b_ffn_conv, m_b_ffn_conv, v_b_ffn_conv),
                   ln2_g=(ln2_g, m_ln2_g, v_ln2_g), ln2_b=(ln2_b, m_ln2_b, v_ln2_b))
    names_s = list(small_w)
    shapes_s = [small_w[nm][0].shape for nm in names_s]
    pk = [_pack([small_w[nm][j] for nm in names_s])[None] for j in range(3)]
    pg = _pack([small_grads[nm] for nm in names_s])[None]
    outs_s = _adamw(pk[0], pk[1], pk[2], pg[0], None, layer=0, prev=None, name="adamw_small")
    small = {nm: [] for nm in names_s}
    for j in range(4):
        for nm, val in zip(names_s, _unpack(outs_s[j][0], shapes_s)):
            small[nm].append(val)

    order = ["meta_tokens", "hg_lower_bounds", "w_in", "w_conv", "w_pool", "pool_scale", "hg_norm_g", "w_o",
             "ln1_g", "ln1_b", "w_up", "w_ffn_conv", "b_ffn_conv", "w_down", "ln2_g", "ln2_b"]
    res = {nm: (big[nm] if nm in big else small[nm]) for nm in order}
    outs = [loss, grad_x]
    for j in range(4):
        outs += [res[nm][j] for nm in order]
    return tuple(outs)
```

```python
import functools

import numpy as np

import jax
import jax.numpy as jnp
from jax import lax
from jax.experimental import pallas as pl
from jax.experimental.pallas import tpu as pltpu

F32 = jnp.float32
BF = jnp.bfloat16
MX = jnp.bfloat16

D = 1024
CW = 256
HW = 512
HD = 128
NH = 4
PW = 256
DIN = 3072
N_META = 16
CH = 64
ROW0 = 256
PADR = ROW0 - N_META
HALO = 16
FH = 8
LEVELS = (32, 16, 8, 4, 2, 1)
DEPTH = 2
ALPHA = (2 * DEPTH) ** 0.25
LN_EPS = 1e-5
RMS_EPS = 1e-6
F_FLOOR = 1e-30
SCALE = HD ** -0.5
ADAM_LR, ADAM_B1, ADAM_B2, ADAM_EPS, ADAM_WD, ADAM_STEP = 0.001, 0.9, 0.999, 1e-08, 0.01, 10

VMEM_V7X = 64 * 2 ** 20
LANES = 128
SUBLANES = 8

NN = (((1,), (0,)), ((), ()))
NT = (((1,), (1,)), ((), ()))
TN = (((0,), (0,)), ((), ()))
MESH = pl.DeviceIdType.MESH
ANY = pl.BlockSpec(memory_space=pl.ANY)


def _dot(a, b, dn):
    return lax.dot_general(a.astype(MX), b.astype(MX), dn, preferred_element_type=F32)


def _cp(sem, est_bytes):
    lim = int(min(VMEM_V7X - 6 * 2 ** 20, max(32 * 2 ** 20, est_bytes)))
    return pltpu.CompilerParams(dimension_semantics=sem, vmem_limit_bytes=lim)


def _nbytes(shape, dtype):
    return int(np.prod(shape)) * jnp.dtype(dtype).itemsize


def _row_tile(rows, row_bytes, budget):
    best = SUBLANES
    for t in range(SUBLANES, rows + 1, SUBLANES):
        if rows % t == 0 and t * row_bytes <= budget:
            best = t
    return best


def _after(dep, body, in_specs, args):
    if dep is None:
        return body, list(in_specs), list(args)

    def body_after(dep_ref, *refs):
        body(*refs)

    return body_after, [ANY] + list(in_specs), [dep] + list(args)


def _mm(a, w, *, nt=False, res=None, tm, out_dtype=F32, dep=None, name):
    R, K = a.shape
    N = w.shape[0] if nt else w.shape[1]
    dn = NT if nt else NN

    def body(*refs):
        if res is None:
            a_ref, w_ref, o_ref = refs
        else:
            a_ref, w_ref, r_ref, o_ref = refs
        acc = _dot(a_ref[...], w_ref[...], dn)
        if res is not None:
            acc = acc + ALPHA * r_ref[...]
        o_ref[...] = acc.astype(out_dtype)

    in_specs = [pl.BlockSpec((tm, K), lambda i: (i, 0)), pl.BlockSpec(w.shape, lambda i: (0, 0))]
    args = [a, w]
    est = 2 * _nbytes((tm, K), a.dtype) + 2 * _nbytes(w.shape, w.dtype) + 3 * _nbytes((tm, N), F32)
    if res is not None:
        in_specs.append(pl.BlockSpec((tm, N), lambda i: (i, 0)))
        args.append(res)
        est += 2 * _nbytes((tm, N), F32)
    body, in_specs, args = _after(dep, body, in_specs, args)
    return pl.pallas_call(
        body, name=name, grid=(R // tm,), in_specs=in_specs,
        out_specs=pl.BlockSpec((tm, N), lambda i: (i, 0)),
        out_shape=jax.ShapeDtypeStruct((R, N), out_dtype),
        compiler_params=_cp(("parallel",), est + 4 * 2 ** 20),
    )(*args)


def _wgrad(a, b, *, slabs_on_cols, tm, tn, name):
    R, Ka = a.shape
    Nb = b.shape[1]
    if slabs_on_cols:
        out_shape = (4, Ka, Nb // 4)
        assert tn == Nb // 4
        out_spec = pl.BlockSpec((None, Ka, tn), lambda j, i: (j, 0, 0))
    else:
        out_shape = (4, Ka // 4, Nb)
        out_spec = pl.BlockSpec((4, Ka // 4, tn), lambda j, i: (0, 0, j))

    def body(a_ref, b_ref, o_ref):
        @pl.when(pl.program_id(1) == 0)
        def _():
            o_ref[...] = jnp.zeros(o_ref.shape, F32)

        acc = _dot(a_ref[...], b_ref[...], TN)
        o_ref[...] += acc.reshape(o_ref.shape)

    in_specs = [pl.BlockSpec((tm, Ka), lambda j, i: (i, 0)), pl.BlockSpec((tm, tn), lambda j, i: (i, j))]
    est = 2 * _nbytes((tm, Ka), a.dtype) + 2 * _nbytes((tm, tn), b.dtype) + 4 * _nbytes((Ka, tn), F32) \
        + _nbytes((tm, Ka), F32)
    return pl.pallas_call(
        body, name=name, grid=(Nb // tn, R // tm), in_specs=in_specs, out_specs=out_spec,
        out_shape=jax.ShapeDtypeStruct(out_shape, F32),
        compiler_params=_cp(("parallel", "arbitrary"), est + 4 * 2 ** 20),
    )(a, b)


def _mm_ln(a, w, xres, g, b, *, tm, dep=None, name):
    R, K = a.shape

    def body(a_ref, w_ref, x_ref, g_ref, b_ref, xo_ref, xh_ref, r_ref):
        z = ALPHA * x_ref[...] + _dot(a_ref[...], w_ref[...], NN)
        mu = jnp.mean(z, axis=-1, keepdims=True)
        zc = z - mu
        var = jnp.mean(zc * zc, axis=-1, keepdims=True)
        r = lax.rsqrt(var + LN_EPS)
        xh = zc * r
        xh_ref[...] = xh
        r_ref[...] = r
        xo_ref[...] = xh * g_ref[...] + b_ref[...]

    row = lambda i: (i, 0)
    fix = lambda i: (0, 0)
    est = 2 * _nbytes((tm, K), a.dtype) + 2 * _nbytes(w.shape, w.dtype) + 10 * _nbytes((tm, D), F32)
    body, in_specs, args = _after(
        dep, body, [pl.BlockSpec((tm, K), row), pl.BlockSpec(w.shape, fix), pl.BlockSpec((tm, D), row),
                    pl.BlockSpec((1, D), fix), pl.BlockSpec((1, D), fix)],
        [a, w, xres, g.reshape(1, D), b.reshape(1, D)])
    return pl.pallas_call(
        body, name=name, grid=(R // tm,), in_specs=in_specs,
        out_specs=[pl.BlockSpec((tm, D), row), pl.BlockSpec((tm, D), row), pl.BlockSpec((tm, 1), row)],
        out_shape=[jax.ShapeDtypeStruct((R, D), F32), jax.ShapeDtypeStruct((R, D), F32),
                   jax.ShapeDtypeStruct((R, 1), F32)],
        compiler_params=_cp(("parallel",), est + 4 * 2 ** 20),
    )(*args)


def _ln_bwd(dx, xh, r, g, *, tm, dep=None, name):
    R = dx.shape[0]

    def body(dx_ref, xh_ref, r_ref, g_ref, dz_ref, gb_ref):
        @pl.when(pl.program_id(0) == 0)
        def _():
            gb_ref[...] = jnp.zeros(gb_ref.shape, F32)

        dxv = dx_ref[...]
        xhv = xh_ref[...]
        dyh = dxv * g_ref[...]
        m1 = jnp.mean(dyh, axis=-1, keepdims=True)
        m2 = jnp.mean(dyh * xhv, axis=-1, keepdims=True)
        dz_ref[...] = r_ref[...] * (dyh - m1 - xhv * m2)
        gb_ref[0:1, :] += jnp.sum(dxv * xhv, axis=0, keepdims=True)
        gb_ref[1:2, :] += jnp.sum(dxv, axis=0, keepdims=True)

    row = lambda i: (i, 0)
    fix = lambda i: (0, 0)
    body, in_specs, args = _after(
        dep, body, [pl.BlockSpec((tm, D), row), pl.BlockSpec((tm, D), row), pl.BlockSpec((tm, 1), row),
                    pl.BlockSpec((1, D), fix)], [dx, xh, r, g.reshape(1, D)])
    return pl.pallas_call(
        body, name=name, grid=(R // tm,), in_specs=in_specs,
        out_specs=[pl.BlockSpec((tm, D), row), pl.BlockSpec((SUBLANES, D), fix)],
        out_shape=[jax.ShapeDtypeStruct((R, D), F32), jax.ShapeDtypeStruct((SUBLANES, D), F32)],
        compiler_params=_cp(("arbitrary",), 12 * _nbytes((tm, D), F32)),
    )(*args)


def _loss(x2, tgt, *, name):
    R = x2.shape[0]
    tb = ROW0

    def body(x_ref, t_ref, dy_ref, acc_ref):
        i = pl.program_id(0)

        @pl.when(i == 0)
        def _():
            acc_ref[...] = jnp.zeros(acc_ref.shape, F32)
            dy_ref[...] = jnp.zeros(dy_ref.shape, F32)

        @pl.when(i > 0)
        def _():
            err = x_ref[...] - t_ref[...]
            dy_ref[...] = err / D
            acc_ref[0:1, :] += jnp.sum(err * err, axis=0, keepdims=True)

    return pl.pallas_call(
        body, name=name, grid=(R // tb,),
        in_specs=[pl.BlockSpec((tb, D), lambda i: (i, 0)),
                  pl.BlockSpec((tb, D), lambda i: (jnp.maximum(i - 1, 0), 0))],
        out_specs=[pl.BlockSpec((tb, D), lambda i: (i, 0)), pl.BlockSpec((SUBLANES, D), lambda i: (0, 0))],
        out_shape=[jax.ShapeDtypeStruct((R, D), F32), jax.ShapeDtypeStruct((SUBLANES, D), F32)],
        compiler_params=_cp(("arbitrary",), 8 * _nbytes((tb, D), F32)),
    )(x2, tgt)


def _row_ids(start, n):
    return start + lax.broadcasted_iota(jnp.int32, (n, 1), 0)


def _ffn_fwd(up, taps, bias, *, tb, name):
    R, F2 = up.shape
    F = F2 // 2
    nh = tb // FH

    def body(u_ref, up_ref, w_ref, b_ref, a_ref):
        i = pl.program_id(0)
        row = _row_ids(i * tb - FH, tb + FH)
        ue = jnp.concatenate([up_ref[...], u_ref[...]], axis=0)
        ue = jnp.where(row >= PADR, ue, 0.0)
        n = tb + FH
        u = (w_ref[0:1, :] * pltpu.roll(ue, 2, 0) + w_ref[1:2, :] * pltpu.roll(ue, 1, 0)
             + w_ref[2:3, :] * ue + b_ref[...])[FH:n]
        gate = u[:, :F]
        val = u[:, F:]
        a_ref[...] = (gate * jax.nn.sigmoid(gate) * val).astype(a_ref.dtype)

    est = 2 * _nbytes((tb, F2), F32) * 4
    return pl.pallas_call(
        body, name=name, grid=(R // tb,),
        in_specs=[pl.BlockSpec((tb, F2), lambda i: (i, 0)),
                  pl.BlockSpec((FH, F2), lambda i: (jnp.maximum(i * nh - 1, 0), 0)),
                  pl.BlockSpec((SUBLANES, F2), lambda i: (0, 0)), pl.BlockSpec((1, F2), lambda i: (0, 0))],
        out_specs=pl.BlockSpec((tb, F), lambda i: (i, 0)),
        out_shape=jax.ShapeDtypeStruct((R, F), MX),
        compiler_params=_cp(("parallel",), est),
    )(up, up, taps, bias)


def _ffn_bwd(da, up, taps, bias, *, tb, dep=None, name):
    R, F2 = up.shape
    F = F2 // 2
    nh = tb // FH
    last = R // FH - 1

    def body(da_ref, dan_ref, u_ref, up_ref, un_ref, w_ref, b_ref, du_ref, acc_ref):
        i = pl.program_id(0)

        @pl.when(i == 0)
        def _():
            acc_ref[...] = jnp.zeros(acc_ref.shape, F32)

        n = tb + 2 * FH
        row = _row_ids(i * tb - FH, n)
        ue = jnp.concatenate([up_ref[...], u_ref[...], un_ref[...]], axis=0)
        ue = jnp.where((row >= PADR) & (row < R), ue, 0.0)
        w0, w1, w2 = w_ref[0:1, :], w_ref[1:2, :], w_ref[2:3, :]
        x2 = pltpu.roll(ue, 2, 0)
        x1 = pltpu.roll(ue, 1, 0)
        u = (w0 * x2 + w1 * x1 + w2 * ue + b_ref[...])[FH:n]
        m = tb + FH
        rown = row[FH:n]
        dae = jnp.concatenate([da_ref[...], dan_ref[...]], axis=0)
        dae = jnp.where(rown < R, dae, 0.0)
        gate = u[:, :F]
        val = u[:, F:]
        sg = jax.nn.sigmoid(gate)
        dgate = dae * val * (sg * (1.0 + gate * (1.0 - sg)))
        dval = dae * (gate * sg)
        du = jnp.concatenate([dgate, dval], axis=1)
        dx = w2 * du + w1 * pltpu.roll(du, m - 1, 0) + w0 * pltpu.roll(du, m - 2, 0)
        dx = jnp.where(rown >= PADR, dx, 0.0)[0:tb]
        du_ref[...] = dx.astype(du_ref.dtype)
        duc = du[0:tb]
        acc_ref[0:1, :] += jnp.sum(duc * x2[FH:FH + tb], axis=0, keepdims=True)
        acc_ref[1:2, :] += jnp.sum(duc * x1[FH:FH + tb], axis=0, keepdims=True)
        acc_ref[2:3, :] += jnp.sum(duc * ue[FH:FH + tb], axis=0, keepdims=True)
        acc_ref[3:4, :] += jnp.sum(duc, axis=0, keepdims=True)

    prev = lambda i: (jnp.maximum(i * nh - 1, 0), 0)
    nxt = lambda i: (jnp.minimum(i * nh + nh, last), 0)
    cur = lambda i: (i, 0)
    fix = lambda i: (0, 0)
    est = 12 * _nbytes((tb + 2 * FH, F2), F32)
    body, in_specs, args = _after(
        dep, body, [pl.BlockSpec((tb, F), cur), pl.BlockSpec((FH, F), nxt),
                    pl.BlockSpec((tb, F2), cur), pl.BlockSpec((FH, F2), prev), pl.BlockSpec((FH, F2), nxt),
                    pl.BlockSpec((SUBLANES, F2), fix), pl.BlockSpec((1, F2), fix)], [da, da, up, up, up, taps, bias])
    return pl.pallas_call(
        body, name=name, grid=(R // tb,), in_specs=in_specs,
        out_specs=[pl.BlockSpec((tb, F2), cur), pl.BlockSpec((SUBLANES, F2), fix)],
        out_shape=[jax.ShapeDtypeStruct((R, F2), MX), jax.ShapeDtypeStruct((SUBLANES, F2), F32)],
        compiler_params=_cp(("arbitrary",), est),
    )(*args)


def _hg_consts():
    t = np.arange(CH)[:, None]
    j = np.arange(CH)[None, :]
    low = (j <= t).astype(np.float32)
    blocks = [low, (j > t).astype(np.float32)]
    for m in LEVELS:
        ref = (t // (2 * m)) * 2 * m + m - 1
        blocks.append(low - (j <= ref).astype(np.float32))
    mat = np.concatenate(blocks, axis=0)
    suf = (j >= t).astype(np.float32)
    return jnp.asarray(mat, BF), jnp.asarray(suf, BF)


def _split_dot(mat, x):
    hi = x.astype(BF)
    lo = (x - hi.astype(F32)).astype(BF)
    return (lax.dot_general(mat, hi, NN, preferred_element_type=F32)
            + lax.dot_general(mat, lo, NN, preferred_element_type=F32))


def _lane_select(a2, a4, a8, a16):
    lane = lax.broadcasted_iota(jnp.int32, (1, PW), 1)
    return jnp.where(lane < 64, a2, jnp.where(lane < 128, a4, jnp.where(lane < 192, a8, a16)))


def _pool_count(row):
    win = _lane_select(2.0, 4.0, 8.0, 16.0)
    t1 = jnp.maximum((row - PADR + 1).astype(F32), 1.0)
    return jnp.minimum(t1, win)


def _gates(fz, lb, valid):
    sig = jax.nn.sigmoid(fz)
    f = lb + (1.0 - lb) * sig
    lf = jnp.where(valid, jnp.log(jnp.maximum(f, F_FLOOR)), 0.0)
    kk = jnp.where(valid, (1.0 - lb) * (1.0 - sig), 0.0)
    return sig, f, lf, kk


def _level_masks():
    tt = lax.broadcasted_iota(jnp.int32, (CH, CH), 0)
    ss = lax.broadcasted_iota(jnp.int32, (CH, CH), 1)
    xr = tt ^ ss
    low = tt > ss
    return tt, ss, [(xr >= m) & (xr < 2 * m) & low for m in LEVELS]


def _intra(q, kk, ex, sl, tt, ss, masks):
    a = jnp.where(tt == ss, jnp.sum(q * kk, axis=-1, keepdims=True), 0.0)
    parts = []
    for l in range(len(LEVELS)):
        e = jnp.exp(-jnp.abs(ex[(2 + l) * CH:(3 + l) * CH, sl]))
        ql = (q * e).astype(MX)
        kl = (kk * e).astype(MX)
        a = a + jnp.where(masks[l], _dot(ql, kl, NT), 0.0)
        parts.append((e, ql, kl))
    return a, parts


def _mixer_fwd(h, taps, wbd, ps, gn, lb, mat, *, name):
    R = h.shape[0]
    nb = R // CH
    q4 = CH // HALO

    def body(h_ref, hp_ref, w_ref, wbd_ref, ps_ref, gn_ref, lb_ref, mat_ref, y_ref, o_ref, sp_ref, st_ref):
        i = pl.program_id(0)

        @pl.when(i == 0)
        def _():
            st_ref[...] = jnp.zeros(st_ref.shape, F32)

        row = _row_ids(i * CH, CH)
        valid = row >= PADR
        rowe = _row_ids(i * CH - HALO, CH + HALO)
        valide = rowe >= PADR
        se = jnp.concatenate([hp_ref[:, 256:512] * hp_ref[:, 512:768], h_ref[:, 256:512] * h_ref[:, 512:768]], axis=0)
        se = jnp.where(valide, se, 0.0)
        conv = (w_ref[0:1, :] * pltpu.roll(se, 2, 0) + w_ref[1:2, :] * pltpu.roll(se, 1, 0)
                + w_ref[2:3, :] * se)[HALO:]
        y_ref[:, 0:CW] = (h_ref[:, 0:256] * conv).astype(y_ref.dtype)
        ve = jnp.where(valide, jnp.concatenate([hp_ref[:, 2816:3072], h_ref[:, 2816:3072]], axis=0), 0.0)
        s2 = ve + pltpu.roll(ve, 1, 0)
        s4 = s2 + pltpu.roll(s2, 2, 0)
        s8 = s4 + pltpu.roll(s4, 4, 0)
        s16 = s8 + pltpu.roll(s8, 8, 0)
        dp = (_lane_select(s2, s4, s8, s16) / _pool_count(rowe) - ve)[HALO:]
        y_ref[:, CW + HW:D] = (_dot(dp, wbd_ref[...], NN) * ps_ref[...]).astype(y_ref.dtype)
        lbv = lb_ref[...]
        _, _, lf, kk = _gates(h_ref[:, 1280:1792], lbv, valid)
        q = h_ref[:, 768:1280] * SCALE
        ii = h_ref[:, 1792:2304]
        gz = h_ref[:, 2304:2816]
        ex = _split_dot(mat_ref[...], lf)
        eg = jnp.exp(ex[0:CH])
        q0 = q * eg
        kr = kk * jnp.exp(ex[CH:2 * CH])
        tt, ss, masks = _level_masks()
        for hd in range(NH):
            sl = slice(HD * hd, HD * (hd + 1))
            st = st_ref[hd]
            sp_ref[hd] = st
            a, _ = _intra(q[:, sl], kk[:, sl], ex, sl, tt, ss, masks)
            o = _dot(q0[:, sl], st, NT) + _dot(a, ii[:, sl], NN)
            st_ref[hd] = st * eg[CH - 1:CH, sl] + _dot(ii[:, sl], kr[:, sl], TN)
            o_ref[:, sl] = o
            r = lax.rsqrt(jnp.mean(o * o, axis=-1, keepdims=True) + RMS_EPS)
            g_ = gz[:, sl]
            y_ref[:, CW + HD * hd:CW + HD * (hd + 1)] = (
                o * r * gn_ref[...] * (g_ * jax.nn.sigmoid(g_))).astype(y_ref.dtype)

    fix = lambda i: (0, 0)
    return pl.pallas_call(
        body, name=name, grid=(nb,),
        in_specs=[pl.BlockSpec((CH, DIN), lambda i: (i, 0)),
                  pl.BlockSpec((HALO, DIN), lambda i: (jnp.maximum(i * q4 - 1, 0), 0)),
                  pl.BlockSpec((SUBLANES, CW), fix), pl.BlockSpec((PW, PW), fix), pl.BlockSpec((1, PW), fix),
                  pl.BlockSpec((1, HD), fix), pl.BlockSpec((1, HW), fix), pl.BlockSpec(mat.shape, fix)],
        out_specs=[pl.BlockSpec((CH, D), lambda i: (i, 0)), pl.BlockSpec((CH, HW), lambda i: (i, 0)),
                   pl.BlockSpec((None, NH, HD, HD), lambda i: (i, 0, 0, 0))],
        out_shape=[jax.ShapeDtypeStruct((R, D), MX), jax.ShapeDtypeStruct((R, HW), F32),
                   jax.ShapeDtypeStruct((nb, NH, HD, HD), F32)],
        scratch_shapes=[pltpu.VMEM((NH, HD, HD), F32)],
        compiler_params=_cp(("arbitrary",), 32 * 2 ** 20),
    )(h, h, taps, wbd, ps, gn, lb, mat)


def _mixer_bwd(h, dy, o, sp, taps, wbd, ps, gn, lb, mat, suf, *, dep=None, name):
    R = h.shape[0]
    nb = R // CH
    q4 = CH // HALO
    lasth = R // HALO - 1

    def body(h_ref, hp_ref, hn_ref, dy_ref, dyn_ref, o_ref, sp_ref, w_ref, wbd_ref, ps_ref, gn_ref, lb_ref,
             mat_ref, suf_ref, dh_ref, dw_ref, dsc_ref, dgn_ref, dlb_ref, dwbd_ref, dst_ref):
        i = pl.program_id(0)
        b = nb - 1 - i

        @pl.when(i == 0)
        def _():
            dst_ref[...] = jnp.zeros(dst_ref.shape, F32)
            dw_ref[...] = jnp.zeros(dw_ref.shape, F32)
            dsc_ref[...] = jnp.zeros(dsc_ref.shape, F32)
            dgn_ref[...] = jnp.zeros(dgn_ref.shape, F32)
            dlb_ref[...] = jnp.zeros(dlb_ref.shape, F32)
            dwbd_ref[...] = jnp.zeros(dwbd_ref.shape, F32)

        ne = CH + 2 * HALO
        nc = CH + HALO
        row = _row_ids(b * CH, CH)
        valid = row >= PADR
        rowe = _row_ids(b * CH - HALO, ne)
        valide = (rowe >= PADR) & (rowe < R)
        rown = rowe[HALO:]
        validn = rown < R

        def cat3(lo, hi):
            return jnp.concatenate([hp_ref[:, lo:hi], h_ref[:, lo:hi], hn_ref[:, lo:hi]], axis=0)

        def catd(lo, hi):
            return jnp.where(validn, jnp.concatenate([dy_ref[:, lo:hi], dyn_ref[:, lo:hi]], axis=0), 0.0)

        w0, w1, w2 = w_ref[0:1, :], w_ref[1:2, :], w_ref[2:3, :]
        cce = cat3(256, 512)
        cve = cat3(512, 768)
        se = jnp.where(valide, cce * cve, 0.0)
        sm2 = pltpu.roll(se, 2, 0)
        sm1 = pltpu.roll(se, 1, 0)
        conv = (w0 * sm2 + w1 * sm1 + w2 * se)[HALO:HALO + CH]
        cbn = jnp.concatenate([h_ref[:, 0:256], hn_ref[:, 0:256]], axis=0)
        dconv = catd(0, CW) * cbn
        ds = w2 * dconv + w1 * pltpu.roll(dconv, nc - 1, 0) + w0 * pltpu.roll(dconv, nc - 2, 0)
        ds = jnp.where(valid, ds[0:CH], 0.0)
        dcv = dconv[0:CH]
        dw_ref[0:1, :] += jnp.sum(dcv * sm2[HALO:HALO + CH], axis=0, keepdims=True)
        dw_ref[1:2, :] += jnp.sum(dcv * sm1[HALO:HALO + CH], axis=0, keepdims=True)
        dw_ref[2:3, :] += jnp.sum(dcv * se[HALO:HALO + CH], axis=0, keepdims=True)
        dh_ref[:, 0:256] = jnp.where(valid, dy_ref[:, 0:CW] * conv, 0.0).astype(dh_ref.dtype)
        dh_ref[:, 256:512] = (ds * h_ref[:, 512:768]).astype(dh_ref.dtype)
        dh_ref[:, 512:768] = (ds * h_ref[:, 256:512]).astype(dh_ref.dtype)

        ve = jnp.where(valide, cat3(2816, 3072), 0.0)
        s2 = ve + pltpu.roll(ve, 1, 0)
        s4 = s2 + pltpu.roll(s2, 2, 0)
        s8 = s4 + pltpu.roll(s4, 4, 0)
        s16 = s8 + pltpu.roll(s8, 8, 0)
        cnte = _pool_count(rowe)
        dp = (_lane_select(s2, s4, s8, s16) / cnte - ve)[HALO:HALO + CH]
        dyp = catd(CW + HW, D)
        pre = _dot(dp, wbd_ref[...], NN)
        dsc_ref[0:1, :] += jnp.sum(dyp[0:CH] * pre, axis=0, keepdims=True)
        dyps = dyp * ps_ref[...]
        dd = _dot(dyps, wbd_ref[...], NT)
        dwbd_ref[...] += _dot(dp, dyps[0:CH], TN)
        e = dd / cnte[HALO:]
        t2 = e + pltpu.roll(e, nc - 1, 0)
        t4 = t2 + pltpu.roll(t2, nc - 2, 0)
        t8 = t4 + pltpu.roll(t4, nc - 4, 0)
        t16 = t8 + pltpu.roll(t8, nc - 8, 0)
        dv = (_lane_select(t2, t4, t8, t16) - dd)[0:CH]
        dh_ref[:, 2816:3072] = jnp.where(valid, dv, 0.0).astype(dh_ref.dtype)

        lbv = lb_ref[...]
        sig, f, lf, kk = _gates(h_ref[:, 1280:1792], lbv, valid)
        q = h_ref[:, 768:1280] * SCALE
        ii = h_ref[:, 1792:2304]
        gz = h_ref[:, 2304:2816]
        ex = _split_dot(mat_ref[...], lf)
        eg = jnp.exp(ex[0:CH])
        egr = jnp.exp(ex[CH:2 * CH])
        q0 = q * eg
        kr = kk * egr
        tt, ss, masks = _level_masks()
        gnv = gn_ref[...]
        dqs, dks, dis, dgzs, tails, dbs = [], [], [], [], [], []
        for hd in range(NH):
            sl = slice(HD * hd, HD * (hd + 1))
            ov = o_ref[:, sl]
            r = lax.rsqrt(jnp.mean(ov * ov, axis=-1, keepdims=True) + RMS_EPS)
            oh = ov * r
            g_ = gz[:, sl]
            sg = jax.nn.sigmoid(g_)
            dyv = dy_ref[:, CW + HD * hd:CW + HD * (hd + 1)]
            don = dyv * (g_ * sg)
            dgzs.append(dyv * (oh * gnv) * (sg * (1.0 + g_ * (1.0 - sg))))
            dgn_ref[0:1, :] += jnp.sum(don * oh, axis=0, keepdims=True)
            doh = don * gnv
            do = r * (doh - oh * jnp.mean(doh * oh, axis=-1, keepdims=True))
            st = sp_ref[hd]
            dst = dst_ref[hd]
            qh, kh, ih = q[:, sl], kk[:, sl], ii[:, sl]
            a, parts = _intra(qh, kh, ex, sl, tt, ss, masks)
            da = jnp.where(tt >= ss, _dot(do, ih, NT), 0.0)
            dis.append(_dot(a, do, TN) + _dot(kr[:, sl], dst, NT))
            q0h = q0[:, sl].astype(MX)
            krh = kr[:, sl].astype(MX)
            dq0 = _dot(do, st, NN)
            dkr = _dot(ih, dst, NN)
            dq = dq0 * eg[:, sl]
            dk = dkr * egr[:, sl]
            kdk = krh.astype(F32) * dkr
            db = q0h.astype(F32) * dq0 - kdk
            tails.append(jnp.sum(kdk, axis=0, keepdims=True)
                         + eg[CH - 1:CH, sl] * jnp.sum(dst * st, axis=0, keepdims=True))
            dga = jnp.sum(jnp.where(tt == ss, da, 0.0), axis=-1, keepdims=True)
            dq = dq + dga * kh
            dk = dk + dga * qh
            for l in range(len(LEVELS)):
                e_l, ql, kl = parts[l]
                dpl = jnp.where(masks[l], da, 0.0).astype(MX)
                dql = _dot(dpl, kl, NN)
                dkl = _dot(dpl, ql, TN)
                dq = dq + dql * e_l
                dk = dk + dkl * e_l
                db = db + (ql.astype(F32) * dql - kl.astype(F32) * dkl)
            dst_ref[hd] = dst * eg[CH - 1:CH, sl] + _dot(do, q0h, TN)
            dqs.append(dq)
            dks.append(dk)
            dbs.append(db)
        dq = jnp.concatenate(dqs, axis=1)
        dk = jnp.concatenate(dks, axis=1)
        db = jnp.concatenate(dbs, axis=1)
        dlf = _split_dot(suf_ref[...], db) + jnp.concatenate(tails, axis=1)
        t = jnp.where(valid, dlf * jnp.where(f > F_FLOOR, 1.0 / f, 0.0) - dk, 0.0)
        dlb_ref[0:1, :] += jnp.sum(t * (1.0 - sig), axis=0, keepdims=True)
        dh_ref[:, 768:1280] = jnp.where(valid, dq * SCALE, 0.0).astype(dh_ref.dtype)
        dh_ref[:, 1280:1792] = (t * (1.0 - lbv) * (sig * (1.0 - sig))).astype(dh_ref.dtype)
        dh_ref[:, 1792:2304] = jnp.where(valid, jnp.concatenate(dis, axis=1), 0.0).astype(dh_ref.dtype)
        dh_ref[:, 2304:2816] = jnp.where(valid, jnp.concatenate(dgzs, axis=1), 0.0).astype(dh_ref.dtype)

    cur = lambda i: (nb - 1 - i, 0)
    prev = lambda i: (jnp.maximum((nb - 1 - i) * q4 - 1, 0), 0)
    nxt = lambda i: (jnp.minimum((nb - 1 - i) * q4 + q4, lasth), 0)
    fix = lambda i: (0, 0)
    body, in_specs, args = _after(
        dep, body,
        [pl.BlockSpec((CH, DIN), cur), pl.BlockSpec((HALO, DIN), prev), pl.BlockSpec((HALO, DIN), nxt),
         pl.BlockSpec((CH, D), cur), pl.BlockSpec((HALO, D), nxt), pl.BlockSpec((CH, HW), cur),
         pl.BlockSpec((None, NH, HD, HD), lambda i: (nb - 1 - i, 0, 0, 0)),
         pl.BlockSpec((SUBLANES, CW), fix), pl.BlockSpec((PW, PW), fix), pl.BlockSpec((1, PW), fix),
         pl.BlockSpec((1, HD), fix), pl.BlockSpec((1, HW), fix), pl.BlockSpec(mat.shape, fix),
         pl.BlockSpec(suf.shape, fix)],
        [h, h, h, dy, dy, o, sp, taps, wbd, ps, gn, lb, mat, suf])
    return pl.pallas_call(
        body, name=name, grid=(nb,), in_specs=in_specs,
        out_specs=[pl.BlockSpec((CH, DIN), cur), pl.BlockSpec((SUBLANES, CW), fix), pl.BlockSpec((SUBLANES, PW), fix),
                   pl.BlockSpec((SUBLANES, HD), fix), pl.BlockSpec((SUBLANES, HW), fix), pl.BlockSpec((PW, PW), fix)],
        out_shape=[jax.ShapeDtypeStruct((R, DIN), MX), jax.ShapeDtypeStruct((SUBLANES, CW), F32),
                   jax.ShapeDtypeStruct((SUBLANES, PW), F32), jax.ShapeDtypeStruct((SUBLANES, HD), F32),
                   jax.ShapeDtypeStruct((SUBLANES, HW), F32), jax.ShapeDtypeStruct((PW, PW), F32)],
        scratch_shapes=[pltpu.VMEM((NH, HD, HD), F32)],
        compiler_params=_cp(("arbitrary",), 40 * 2 ** 20),
    )(*args)


def _sum_slots(recv, *, name):
    S, L, rows, cols = recv.shape
    tr = _row_tile(rows, S * cols * 4, 6 * 2 ** 20)

    def body(r_ref, o_ref):
        acc = r_ref[0]
        for s in range(1, S):
            acc = acc + r_ref[s]
        o_ref[...] = acc

    return pl.pallas_call(
        body, name=name, grid=(L, rows // tr),
        in_specs=[pl.BlockSpec((S, None, tr, cols), lambda l, i: (0, l, i, 0))],
        out_specs=pl.BlockSpec((None, tr, cols), lambda l, i: (l, i, 0)),
        out_shape=jax.ShapeDtypeStruct((L, rows, cols), F32),
        compiler_params=_cp(("parallel", "parallel"), 4 * S * tr * cols * 4),
    )(recv)


def _sum_own(recv, own, chip, *, name):
    S, rows, cols = recv.shape
    tr = _row_tile(rows, S * cols * 4, 6 * 2 ** 20)

    def body(me_ref, r_ref, o_ref, out_ref):
        me = me_ref[0]
        acc = None
        for s in range(S):
            t = jnp.where(me == s, o_ref[...], r_ref[s])
            acc = t if acc is None else acc + t
        out_ref[...] = acc

    grid_spec = pltpu.PrefetchScalarGridSpec(
        num_scalar_prefetch=1, grid=(rows // tr,),
        in_specs=[pl.BlockSpec((S, tr, cols), lambda i, me: (0, i, 0)),
                  pl.BlockSpec((None, tr, cols), lambda i, me: (me[0], i, 0))],
        out_specs=pl.BlockSpec((tr, cols), lambda i, me: (i, 0)))
    return pl.pallas_call(
        body, name=name, grid_spec=grid_spec, out_shape=jax.ShapeDtypeStruct((rows, cols), F32),
        compiler_params=_cp(("parallel",), 5 * S * tr * cols * 4),
    )(chip.reshape(1).astype(jnp.int32), recv, own)


def _adamw(w, m, v, ga, gb, *, layer, prev, name):
    L, rows, cols = w.shape
    tr = _row_tile(rows, cols * 4, 2 ** 20)
    two = gb is not None
    nin = 5 if two else 4

    def body(*refs):
        w_ref, m_ref, v_ref, a_ref = refs[:4]
        g_ref, d_ref, m2_ref, v2_ref = refs[-4:]
        g = a_ref[...] + refs[4][...] if two else a_ref[...]
        m2 = ADAM_B1 * m_ref[...] + (1.0 - ADAM_B1) * g
        v2 = ADAM_B2 * v_ref[...] + (1.0 - ADAM_B2) * (g * g)
        m_hat = m2 / (1.0 - ADAM_B1 ** ADAM_STEP)
        v_hat = v2 / (1.0 - ADAM_B2 ** ADAM_STEP)
        g_ref[...] = g
        d_ref[...] = -ADAM_LR * (m_hat / (jnp.sqrt(v_hat) + ADAM_EPS) + ADAM_WD * w_ref[...])
        m2_ref[...] = m2
        v2_ref[...] = v2

    spec = pl.BlockSpec((None, tr, cols), lambda i: (layer, i, 0))
    gspec = pl.BlockSpec((tr, cols), lambda i: (i, 0))
    args = [w, m, v, ga] + ([gb] if two else [])
    in_specs = [spec] * 3 + [gspec] * (nin - 3)
    aliases = {}
    if prev is not None:
        args += list(prev)
        in_specs += [ANY] * 4
        aliases = {nin + j: j for j in range(4)}
    sd = jax.ShapeDtypeStruct((L, rows, cols), F32)
    return pl.pallas_call(
        body, name=name, grid=(rows // tr,), in_specs=in_specs, out_specs=[spec] * 4,
        out_shape=[sd] * 4, input_output_aliases=aliases,
        compiler_params=_cp(("parallel",), 24 * tr * cols * 4),
    )(*args)


def _exchange(arrays, *, flips, n_slots, slot_of, scatter, self_copy, name):
    n = len(arrays)
    nf = len(flips)
    out_shapes = [jax.ShapeDtypeStruct(a.shape if scatter else (n_slots,) + a.shape, a.dtype) for a in arrays]

    def body(*refs):
        ins, outs = refs[:n], refs[n:2 * n]
        send_sems, recv_sems, loc_sems = refs[2 * n:]
        x, y, c = lax.axis_index("x"), lax.axis_index("y"), lax.axis_index("c")
        me = slot_of(x, y, c)
        peers = [(1 - x if fx else x, 1 - y if fy else y, 1 - c if fc else c) for fx, fy, fc in flips]
        local, remote = [], []
        for a in range(n):
            if self_copy:
                lc = pltpu.make_async_copy(ins[a].at[me] if scatter else ins[a], outs[a].at[me], loc_sems.at[a])
                lc.start()
                local.append(lc)
            for k, p in enumerate(peers):
                src = ins[a].at[slot_of(*p)] if scatter else ins[a]
                cp = pltpu.make_async_remote_copy(
                    src_ref=src, dst_ref=outs[a].at[me], send_sem=send_sems.at[a, k], recv_sem=recv_sems.at[a, k],
                    device_id=p, device_id_type=MESH)
                cp.start()
                remote.append(cp)
        for a in range(n):
            for k, p in enumerate(peers):
                src = ins[a].at[slot_of(*p)] if scatter else ins[a]
                pltpu.make_async_remote_copy(
                    src_ref=src, dst_ref=outs[a].at[slot_of(*p)], send_sem=send_sems.at[a, k],
                    recv_sem=recv_sems.at[a, k], device_id=p, device_id_type=MESH).wait_recv()
        for cp in remote:
            cp.wait_send()
        for lc in local:
            lc.wait()

    return pl.pallas_call(
        body, name=name, in_specs=[ANY] * n, out_specs=[ANY] * n, out_shape=out_shapes,
        scratch_shapes=[pltpu.SemaphoreType.DMA((n, nf)), pltpu.SemaphoreType.DMA((n, nf)),
                        pltpu.SemaphoreType.DMA((n,))],
        compiler_params=pltpu.CompilerParams(has_side_effects=True),
    )(*arrays)


CHIP_FLIPS = [(1, 0, 0), (0, 1, 0), (1, 1, 0)]
ALL_FLIPS = [(fx, fy, fc) for fx in (0, 1) for fy in (0, 1) for fc in (0, 1) if fx or fy or fc]


def _chip_slot(x, y, c):
    return 2 * x + y


def _dev_slot(x, y, c):
    return 4 * x + 2 * y + c


def _zero_slot(x, y, c):
    return 0


HBM_SPEC = pl.BlockSpec(memory_space=pltpu.HBM)
SEM_SPEC = pl.BlockSpec(memory_space=pltpu.SEMAPHORE)
N_PEER_CHIPS = 3


def _peer_chips(x, y, c):
    return [(1 - x, y, c), (x, 1 - y, c), (1 - x, 1 - y, c)]


def _send_start(srcs, *, scatter, dep=None, name):
    n = len(srcs)
    nc = n * N_PEER_CHIPS
    srcs = [pltpu.with_memory_space_constraint(s, pltpu.HBM) for s in srcs]
    land_shapes = [s.shape if scatter else (4,) + s.shape for s in srcs]
    lands = [pltpu.with_memory_space_constraint(lax.empty(sh, s.dtype), pltpu.HBM) for sh, s in zip(land_shapes, srcs)]

    deps = [] if dep is None else [dep]
    nd = len(deps)

    def body(*refs):
        ins, lnd = refs[:n], refs[n:2 * n]
        send_sems, recv_sems = refs[2 * n + nd:2 * n + nd + nc], refs[2 * n + nd + nc:2 * n + nd + 2 * nc]
        token = refs[-1]
        x, y, c = lax.axis_index("x"), lax.axis_index("y"), lax.axis_index("c")
        me = 2 * x + y
        for a in range(n):
            for k, p in enumerate(_peer_chips(x, y, c)):
                src = ins[a].at[2 * p[0] + p[1]] if scatter else ins[a]
                j = a * N_PEER_CHIPS + k
                pltpu.make_async_remote_copy(
                    src_ref=src, dst_ref=lnd[a].at[me], send_sem=send_sems[j], recv_sem=recv_sems[j],
                    device_id=p, device_id_type=MESH).start()
        token[...] = jnp.zeros(token.shape, token.dtype)

    sem = pltpu.SemaphoreType.DMA(())
    outs = pl.pallas_call(
        body, name=name,
        out_shape=(*[sem] * (2 * nc), *[pltpu.HBM(s.shape, s.dtype) for s in srcs],
                   *[pltpu.HBM(sh, s.dtype) for sh, s in zip(land_shapes, srcs)],
                   jax.ShapeDtypeStruct((SUBLANES, LANES), F32)),
        in_specs=[HBM_SPEC] * (2 * n) + [ANY] * nd,
        out_specs=(*[SEM_SPEC] * (2 * nc), *[HBM_SPEC] * (2 * n), pl.BlockSpec(memory_space=pltpu.VMEM)),
        input_output_aliases={i: 2 * nc + i for i in range(2 * n)},
        compiler_params=pltpu.CompilerParams(has_side_effects=pltpu.SideEffectType.DATAFLOW_SIDE_EFFECTING),
    )(*srcs, *lands, *deps)
    return dict(sems=list(outs[:2 * nc]), srcs=list(outs[2 * nc:2 * nc + n]),
                lands=list(outs[2 * nc + n:2 * nc + 2 * n]), token=outs[-1])


def _send_wait(h, *, scatter, after, name):
    n = len(h["srcs"])
    nc = n * N_PEER_CHIPS

    def body(*refs):
        ins, lnd = refs[:n], refs[n:2 * n]
        send_sems, recv_sems = refs[2 * n:2 * n + nc], refs[2 * n + nc:2 * n + 2 * nc]
        x, y, c = lax.axis_index("x"), lax.axis_index("y"), lax.axis_index("c")
        for a in range(n):
            for k, p in enumerate(_peer_chips(x, y, c)):
                slot = 2 * p[0] + p[1]
                j = a * N_PEER_CHIPS + k
                cp = pltpu.make_async_remote_copy(
                    src_ref=ins[a].at[slot] if scatter else ins[a], dst_ref=lnd[a].at[slot],
                    send_sem=send_sems[j], recv_sem=recv_sems[j], device_id=p, device_id_type=MESH)
                cp.wait_send()
                cp.wait_recv()

    thru = h["srcs"] + h["lands"]
    outs = pl.pallas_call(
        body, name=name, out_shape=tuple(pltpu.HBM(t.shape, t.dtype) for t in thru),
        in_specs=[HBM_SPEC] * (2 * n) + [SEM_SPEC] * (2 * nc) + [ANY], out_specs=tuple([HBM_SPEC] * (2 * n)),
        input_output_aliases={i: i for i in range(2 * n)},
        compiler_params=pltpu.CompilerParams(has_side_effects=pltpu.SideEffectType.DATAFLOW_SIDE_EFFECTING),
    )(*thru, *h["sems"], after)
    return list(outs[:n]), list(outs[n:])


def _assemble(land, own, chip, axis):
    return jnp.concatenate([jnp.where(chip == k, own, land[k]) for k in range(4)], axis=axis)


def _pack(arrs):
    flat = jnp.concatenate([a.reshape(-1).astype(F32) for a in arrs])
    tile = SUBLANES * LANES
    pad = (-flat.shape[0]) % tile
    return jnp.pad(flat, (0, pad)).reshape(-1, LANES)


def _unpack(buf, shapes):
    flat = buf.reshape(-1)
    out, off = [], 0
    for s in shapes:
        n = int(np.prod(s))
        out.append(flat[off:off + n].reshape(s))
        off += n
    return out


def _lower_bounds(hg_lower_bounds):
    p = jax.nn.softmax(hg_lower_bounds.astype(F32), axis=0)
    return jnp.cumsum(p, axis=0) - p[0]


def kernel(x, meta_tokens, hg_lower_bounds, w_in, w_conv, w_pool, pool_scale, hg_norm_g, w_o, ln1_g, ln1_b, w_up, w_ffn_conv, b_ffn_conv, w_down, ln2_g, ln2_b, loss_target, m_meta_tokens, m_hg_lower_bounds, m_w_in, m_w_conv, m_w_pool, m_pool_scale, m_hg_norm_g, m_w_o, m_ln1_g, m_ln1_b, m_w_up, m_w_ffn_conv, m_b_ffn_conv, m_w_down, m_ln2_g, m_ln2_b, v_meta_tokens, v_hg_lower_bounds, v_w_in, v_w_conv, v_w_pool, v_pool_scale, v_hg_norm_g, v_w_o, v_ln1_g, v_ln1_b, v_w_up, v_w_ffn_conv, v_b_ffn_conv, v_w_down, v_ln2_g, v_ln2_b):
    S = x.shape[1]
    R = S + ROW0
    Fq = w_down.shape[1]
    F = 4 * Fq
    F2 = 2 * F
    F2q = w_up.shape[2]
    assert x.shape == (1, S, D) and R % 384 == 0 and S % ROW0 == 0
    chip = 2 * lax.axis_index("x") + lax.axis_index("y")
    tm = 384
    tb_ffn = 128

    small_shapes = [(N_META, D // 4), (DEPTH, CW // 4, 3), (DEPTH, F2q, 3)]
    wb_in, wb_o, wb_up, wb_down = (w.astype(MX) for w in (w_in, w_o, w_up, w_down))
    h_a = _send_start([wb_in[0], _pack([meta_tokens, w_conv, w_ffn_conv])], scatter=False, name="gather_a_start")
    (own_in, own_small), (l_in, l_small) = _send_wait(h_a, scatter=False, after=h_a["token"], name="gather_a_wait")
    h_b = _send_start([wb_o[0], wb_up[0], wb_down[0]], scatter=False, dep=l_in, name="gather_b_start")
    Win, Wo, Wup, Wdown = {}, {}, {}, {}
    Win[0] = _assemble(l_in, own_in, chip, 1)
    sm = [_unpack(jnp.where(chip == k, own_small, l_small[k]), small_shapes) for k in range(4)]
    meta_full = jnp.concatenate([sm[k][0] for k in range(4)], axis=1)
    wconv_full = jnp.concatenate([sm[k][1] for k in range(4)], axis=1)
    wffn_full = jnp.concatenate([sm[k][2] for k in range(4)], axis=1)
    taps_c = jnp.pad(wconv_full.transpose(0, 2, 1), ((0, 0), (0, SUBLANES - 3), (0, 0)))
    taps_f = jnp.pad(wffn_full.transpose(0, 2, 1), ((0, 0), (0, SUBLANES - 3), (0, 0)))
    wbd = jnp.stack([jax.scipy.linalg.block_diag(*[w_pool[l, g] for g in range(4)]) for l in range(DEPTH)]).astype(MX)
    lbs, lbs_vjp = jax.vjp(_lower_bounds, hg_lower_bounds)
    mat, suf = _hg_consts()

    def fwd_mixer(l, X, dep=None):
        h = _mm(X, Win[l], tm=tm, dep=dep, name=f"mm_in_{l}")
        y, o, sp = _mixer_fwd(h, taps_c[l], wbd[l], pool_scale[l].reshape(1, PW), hg_norm_g[l].reshape(1, HD),
                              lbs[l].reshape(1, HW), mat, name=f"mixer_fwd_{l}")
        return h, y, o, sp

    def fwd_rest(l, X, h, y, o, sp, dep=None):
        x1, xh1, r1 = _mm_ln(y, Wo[l], X, ln1_g[l], ln1_b[l], tm=tm, dep=dep, name=f"mm_o_ln_{l}")
        up = _mm(x1, Wup[l], tm=tm // 2, name=f"mm_up_{l}")
        a = _ffn_fwd(up, taps_f[l], b_ffn_conv[l].reshape(1, F2), tb=tb_ffn, name=f"ffn_fwd_{l}")
        x2, xh2, r2 = _mm_ln(a, Wdown[l], x1, ln2_g[l], ln2_b[l], tm=tm, name=f"mm_down_ln_{l}")
        return (X, h, y, o, sp, x1, xh1, r1, up, a, xh2, r2), x2

    X = jnp.concatenate([jnp.zeros((PADR, D), F32), meta_full, x[0]], axis=0)
    h, y, o, sp = fwd_mixer(0, X, dep=h_b["token"])
    (own_o, own_up, own_down), (l_o, l_up, l_down) = _send_wait(h_b, scatter=False, after=y, name="gather_b_wait")
    Wo[0], Wup[0], Wdown[0] = (_assemble(l_o, own_o, chip, 0), _assemble(l_up, own_up, chip, 1),
                               _assemble(l_down, own_down, chip, 0))
    h_c = _send_start([wb_in[1], wb_o[1], wb_up[1], wb_down[1]], scatter=False, dep=l_o, name="gather_c_start")
    saved0, X1 = fwd_rest(0, X, h, y, o, sp, dep=h_c["token"])
    own_c, l_c = _send_wait(h_c, scatter=False, after=X1, name="gather_c_wait")
    Win[1], Wo[1] = _assemble(l_c[0], own_c[0], chip, 1), _assemble(l_c[1], own_c[1], chip, 0)
    Wup[1], Wdown[1] = _assemble(l_c[2], own_c[2], chip, 1), _assemble(l_c[3], own_c[3], chip, 0)
    saved1, X2 = fwd_rest(1, X1, *fwd_mixer(1, X1))
    saved = [saved0, saved1]

    dxo, lacc = _loss(X2, loss_target[0], name="loss")
    loss = lax.psum(0.5 * jnp.sum(lacc[0]) / D, ("x", "y", "c"))

    sc = {}

    def scatter(nm, l, g):
        sc[nm, l] = _send_start([g], scatter=True, name=f"scatter_{nm}_{l}_start")
        return sc[nm, l]["token"]

    tok = None

    small_g = [None] * DEPTH
    for l in reversed(range(DEPTH)):
        X, h, y, o, sp, x1, xh1, r1, up, a, xh2, r2 = saved[l]
        dz2, gb2 = _ln_bwd(dxo, xh2, r2, ln2_g[l], tm=tm, dep=tok, name=f"ln2_bwd_{l}")
        da = _mm(dz2, Wdown[l], nt=True, tm=tm, name=f"mm_da_{l}")
        tok = scatter("w_down", l, _wgrad(a, dz2, slabs_on_cols=False, tm=tm, tn=512, name=f"wgrad_down_{l}"))
        dup, facc = _ffn_bwd(da, up, taps_f[l], b_ffn_conv[l].reshape(1, F2), tb=tb_ffn, dep=tok, name=f"ffn_bwd_{l}")
        dx1 = _mm(dup, Wup[l], nt=True, res=dz2, tm=tm, name=f"mm_dx1_{l}")
        tok = scatter("w_up", l, _wgrad(x1, dup, slabs_on_cols=True, tm=tm, tn=F2q, name=f"wgrad_up_{l}"))
        dz1, gb1 = _ln_bwd(dx1, xh1, r1, ln1_g[l], tm=tm, dep=tok, name=f"ln1_bwd_{l}")
        dym = _mm(dz1, Wo[l], nt=True, tm=tm, name=f"mm_dym_{l}")
        tok = scatter("w_o", l, _wgrad(y, dz1, slabs_on_cols=False, tm=tm, tn=512, name=f"wgrad_o_{l}"))
        dh, dwc, dsc, dgn, dlb, dwbd = _mixer_bwd(
            h, dym, o, sp, taps_c[l], wbd[l], pool_scale[l].reshape(1, PW), hg_norm_g[l].reshape(1, HD),
            lbs[l].reshape(1, HW), mat, suf, dep=tok, name=f"mixer_bwd_{l}")
        dX = _mm(dh, Win[l], nt=True, res=dz1, tm=tm, name=f"mm_dx_{l}")
        tok = scatter("w_in", l, _wgrad(X, dh, slabs_on_cols=True, tm=tm, tn=DIN // 4, name=f"wgrad_in_{l}"))
        small_g[l] = dict(
            lbs=dlb[0], w_conv=dwc[0:3].T, w_pool=jnp.stack([dwbd[64 * g:64 * g + 64, 64 * g:64 * g + 64] for g in range(4)]),
            pool_scale=dsc[0], hg_norm_g=dgn[0], ln1_g=gb1[0], ln1_b=gb1[1], w_ffn_conv=facc[0:3].T,
            b_ffn_conv=facc[3], ln2_g=gb2[0], ln2_b=gb2[1])
        dxo = dX
    grad_x = dxo[ROW0:][None]

    big_w = dict(w_in=(w_in, m_w_in, v_w_in), w_o=(w_o, m_w_o, v_w_o), w_up=(w_up, m_w_up, v_w_up),
                 w_down=(w_down, m_w_down, v_w_down))
    big = {nm: None for nm in big_w}
    for l in reversed(range(DEPTH)):
        part = []
        for nm in big_w:
            (own,), (recv,) = _send_wait(sc[nm, l], scatter=True, after=dxo, name=f"scatter_{nm}_{l}_wait")
            part.append(_sum_own(recv, own, chip, name=f"sum_{nm}_{l}"))
        sib = _exchange(part, flips=[(0, 0, 1)], n_slots=1, slot_of=_zero_slot, scatter=False, self_copy=False,
                        name=f"swap_cores_{l}")
        for k, nm in enumerate(big_w):
            w, m, v = big_w[nm]
            big[nm] = _adamw(w, m, v, part[k], sib[k][0], layer=l, prev=big[nm], name=f"adamw_{nm}_{l}")

    sg_names = ["lbs", "w_conv", "w_pool", "pool_scale", "hg_norm_g", "ln1_g", "ln1_b", "w_ffn_conv",
                "b_ffn_conv", "ln2_g", "ln2_b"]
    sg_list = [dxo[PADR:ROW0]] + [jnp.stack([small_g[l][nm] for l in range(DEPTH)]) for nm in sg_names]
    sg_shapes = [a.shape for a in sg_list]
    packed = _pack(sg_list)
    (gathered,) = _exchange([packed], flips=ALL_FLIPS, n_slots=8, slot_of=_dev_slot, scatter=False, self_copy=True,
                            name="gather_small_grads")
    total = _sum_slots(gathered[:, None], name="sum_small_grads")[0]
    tot = dict(zip(["meta_tokens"] + sg_names, _unpack(total, sg_shapes)))
    (g_hg,) = lbs_vjp(tot["lbs"])
    small_grads = dict(
        meta_tokens=lax.dynamic_slice_in_dim(tot["meta_tokens"], chip * (D // 4), D // 4, axis=1),
        hg_lower_bounds=g_hg,
        w_conv=lax.dynamic_slice_in_dim(tot["w_conv"], chip * (CW // 4), CW // 4, axis=1),
        w_pool=tot["w_pool"], pool_scale=tot["pool_scale"], hg_norm_g=tot["hg_norm_g"],
        ln1_g=tot["ln1_g"], ln1_b=tot["ln1_b"],
        w_ffn_conv=lax.dynamic_slice_in_dim(tot["w_ffn_conv"], chip * F2q, F2q, axis=1),
        b_ffn_conv=tot["b_ffn_conv"], ln2_g=tot["ln2_g"], ln2_b=tot["ln2_b"])
    small_w = dict(meta_tokens=(meta_tokens, m_meta_tokens, v_meta_tokens),
                   hg_lower_bounds=(hg_lower_bounds, m_hg_lower_bounds, v_hg_lower_bounds),
                   w_conv=(w_conv, m_w_conv, v_w_conv), w_pool=(w_pool, m_w_pool, v_w_pool),
                   pool_scale=(pool_scale, m_pool_scale, v_pool_scale), hg_norm_g=(hg_norm_g, m_hg_norm_g, v_hg_norm_g),
                   ln1_g=(ln1_g, m_ln1_g, v_ln1_g), ln1_b=(ln1_b, m_ln1_b, v_ln1_b),
                   w_ffn_conv=(w_ffn_conv, m_w_ffn_conv, v_w_ffn_conv), b_ffn_conv=(b_ffn_conv, m_b_ffn_conv, v_b_ffn_conv),
                   ln2_g=(ln2_g, m_ln2_g, v_ln2_g), ln2_b=(ln2_b, m_ln2_b, v_ln2_b))
    names_s = list(small_w)
    shapes_s = [small_w[nm][0].shape for nm in names_s]
    pk = [_pack([small_w[nm][j] for nm in names_s])[None] for j in range(3)]
    pg = _pack([small_grads[nm] for nm in names_s])[None]
    outs_s = _adamw(pk[0], pk[1], pk[2], pg[0], None, layer=0, prev=None, name="adamw_small")
    small = {nm: [] for nm in names_s}
    for j in range(4):
        for nm, val in zip(names_s, _unpack(outs_s[j][0], shapes_s)):
            small[nm].append(val)

    order = ["meta_tokens", "hg_lower_bounds", "w_in", "w_conv", "w_pool", "pool_scale", "hg_norm_g", "w_o",
             "ln1_g", "ln1_b", "w_up", "w_ffn_conv", "b_ffn_conv", "w_down", "ln2_g", "ln2_b"]
    res = {nm: (big[nm] if nm in big else small[nm]) for nm in order}
    outs = [loss, grad_x]
    for j in range(4):
        outs += [res[nm][j] for nm in order]
    return tuple(outs)
```

```python
import functools

import numpy as np

import jax
import jax.numpy as jnp
from jax import lax
from jax.experimental import pallas as pl
from jax.experimental.pallas import tpu as pltpu

F32 = jnp.float32
BF = jnp.bfloat16
MX = jnp.bfloat16

D = 1024
CW = 256
HW = 512
HD = 128
NH = 4
PW = 256
DIN = 3072
N_META = 16
CH = 64
ROW0 = 256
PADR = ROW0 - N_META
HALO = 16
FH = 8
LEVELS = (32, 16, 8, 4, 2, 1)
DEPTH = 2
ALPHA = (2 * DEPTH) ** 0.25
LN_EPS = 1e-5
RMS_EPS = 1e-6
F_FLOOR = 1e-30
SCALE = HD ** -0.5
ADAM_LR, ADAM_B1, ADAM_B2, ADAM_EPS, ADAM_WD, ADAM_STEP = 0.001, 0.9, 0.999, 1e-08, 0.01, 10

VMEM_V7X = 64 * 2 ** 20
LANES = 128
SUBLANES = 8

NN = (((1,), (0,)), ((), ()))
NT = (((1,), (1,)), ((), ()))
TN = (((0,), (0,)), ((), ()))
MESH = pl.DeviceIdType.MESH
ANY = pl.BlockSpec(memory_space=pl.ANY)


def _dot(a, b, dn):
    return lax.dot_general(a.astype(MX), b.astype(MX), dn, preferred_element_type=F32)


def _cp(sem, est_bytes):
    lim = int(min(VMEM_V7X - 6 * 2 ** 20, max(32 * 2 ** 20, est_bytes)))
    return pltpu.CompilerParams(dimension_semantics=sem, vmem_limit_bytes=lim)


def _nbytes(shape, dtype):
    return int(np.prod(shape)) * jnp.dtype(dtype).itemsize


def _row_tile(rows, row_bytes, budget):
    best = SUBLANES
    for t in range(SUBLANES, rows + 1, SUBLANES):
        if rows % t == 0 and t * row_bytes <= budget:
            best = t
    return best


def _after(dep, body, in_specs, args):
    if dep is None:
        return body, list(in_specs), list(args)

    def body_after(dep_ref, *refs):
        body(*refs)

    return body_after, [ANY] + list(in_specs), [dep] + list(args)


def _mm(a, w, *, nt=False, res=None, tm, out_dtype=F32, zero_inert=False, dep=None, name):
    R, K = a.shape
    N = w.shape[0] if nt else w.shape[1]
    dn = NT if nt else NN

    def body(*refs):
        if res is None:
            a_ref, w_ref, o_ref = refs
        else:
            a_ref, w_ref, r_ref, o_ref = refs
        acc = _dot(a_ref[...], w_ref[...], dn)
        if res is not None:
            acc = acc + ALPHA * r_ref[...]
        if zero_inert:
            acc = jnp.where(_row_ids(pl.program_id(0) * tm, tm) >= PADR, acc, 0.0)
        o_ref[...] = acc.astype(out_dtype)

    in_specs = [pl.BlockSpec((tm, K), lambda i: (i, 0)), pl.BlockSpec(w.shape, lambda i: (0, 0))]
    args = [a, w]
    est = 2 * _nbytes((tm, K), a.dtype) + 2 * _nbytes(w.shape, w.dtype) + 3 * _nbytes((tm, N), F32)
    if res is not None:
        in_specs.append(pl.BlockSpec((tm, N), lambda i: (i, 0)))
        args.append(res)
        est += 2 * _nbytes((tm, N), F32)
    body, in_specs, args = _after(dep, body, in_specs, args)
    return pl.pallas_call(
        body, name=name, grid=(R // tm,), in_specs=in_specs,
        out_specs=pl.BlockSpec((tm, N), lambda i: (i, 0)),
        out_shape=jax.ShapeDtypeStruct((R, N), out_dtype),
        compiler_params=_cp(("parallel",), est + 4 * 2 ** 20),
    )(*args)


def _wgrad(a, b, *, slabs_on_cols, tm, tn, name):
    R, Ka = a.shape
    Nb = b.shape[1]
    if slabs_on_cols:
        out_shape = (4, Ka, Nb // 4)
        assert tn == Nb // 4
        out_spec = pl.BlockSpec((None, Ka, tn), lambda j, i: (j, 0, 0))
    else:
        out_shape = (4, Ka // 4, Nb)
        out_spec = pl.BlockSpec((4, Ka // 4, tn), lambda j, i: (0, 0, j))

    def body(a_ref, b_ref, o_ref):
        @pl.when(pl.program_id(1) == 0)
        def _():
            o_ref[...] = jnp.zeros(o_ref.shape, F32)

        acc = _dot(a_ref[...], b_ref[...], TN)
        o_ref[...] += acc.reshape(o_ref.shape)

    in_specs = [pl.BlockSpec((tm, Ka), lambda j, i: (i, 0)), pl.BlockSpec((tm, tn), lambda j, i: (i, j))]
    est = 2 * _nbytes((tm, Ka), a.dtype) + 2 * _nbytes((tm, tn), b.dtype) + 4 * _nbytes((Ka, tn), F32) \
        + _nbytes((tm, Ka), F32)
    return pl.pallas_call(
        body, name=name, grid=(Nb // tn, R // tm), in_specs=in_specs, out_specs=out_spec,
        out_shape=jax.ShapeDtypeStruct(out_shape, F32),
        compiler_params=_cp(("parallel", "arbitrary"), est + 4 * 2 ** 20),
    )(a, b)


def _mm_ln(a, w, xres, g, b, *, tm, dep=None, name):
    R, K = a.shape

    def body(a_ref, w_ref, x_ref, g_ref, b_ref, xo_ref, xh_ref, r_ref):
        z = ALPHA * x_ref[...] + _dot(a_ref[...], w_ref[...], NN)
        mu = jnp.mean(z, axis=-1, keepdims=True)
        zc = z - mu
        var = jnp.mean(zc * zc, axis=-1, keepdims=True)
        r = lax.rsqrt(var + LN_EPS)
        xh = zc * r
        xh_ref[...] = xh
        r_ref[...] = r
        xo_ref[...] = xh * g_ref[...] + b_ref[...]

    row = lambda i: (i, 0)
    fix = lambda i: (0, 0)
    est = 2 * _nbytes((tm, K), a.dtype) + 2 * _nbytes(w.shape, w.dtype) + 10 * _nbytes((tm, D), F32)
    body, in_specs, args = _after(
        dep, body, [pl.BlockSpec((tm, K), row), pl.BlockSpec(w.shape, fix), pl.BlockSpec((tm, D), row),
                    pl.BlockSpec((1, D), fix), pl.BlockSpec((1, D), fix)],
        [a, w, xres, g.reshape(1, D), b.reshape(1, D)])
    return pl.pallas_call(
        body, name=name, grid=(R // tm,), in_specs=in_specs,
        out_specs=[pl.BlockSpec((tm, D), row), pl.BlockSpec((tm, D), row), pl.BlockSpec((tm, 1), row)],
        out_shape=[jax.ShapeDtypeStruct((R, D), F32), jax.ShapeDtypeStruct((R, D), F32),
                   jax.ShapeDtypeStruct((R, 1), F32)],
        compiler_params=_cp(("parallel",), est + 4 * 2 ** 20),
    )(*args)


def _ln_bwd(dx, xh, r, g, *, tm, dep=None, name):
    R = dx.shape[0]

    def body(dx_ref, xh_ref, r_ref, g_ref, dz_ref, gb_ref):
        @pl.when(pl.program_id(0) == 0)
        def _():
            gb_ref[...] = jnp.zeros(gb_ref.shape, F32)

        dxv = dx_ref[...]
        xhv = xh_ref[...]
        dyh = dxv * g_ref[...]
        m1 = jnp.mean(dyh, axis=-1, keepdims=True)
        m2 = jnp.mean(dyh * xhv, axis=-1, keepdims=True)
        dz_ref[...] = r_ref[...] * (dyh - m1 - xhv * m2)
        gb_ref[0:1, :] += jnp.sum(dxv * xhv, axis=0, keepdims=True)
        gb_ref[1:2, :] += jnp.sum(dxv, axis=0, keepdims=True)

    row = lambda i: (i, 0)
    fix = lambda i: (0, 0)
    body, in_specs, args = _after(
        dep, body, [pl.BlockSpec((tm, D), row), pl.BlockSpec((tm, D), row), pl.BlockSpec((tm, 1), row),
                    pl.BlockSpec((1, D), fix)], [dx, xh, r, g.reshape(1, D)])
    return pl.pallas_call(
        body, name=name, grid=(R // tm,), in_specs=in_specs,
        out_specs=[pl.BlockSpec((tm, D), row), pl.BlockSpec((SUBLANES, D), fix)],
        out_shape=[jax.ShapeDtypeStruct((R, D), F32), jax.ShapeDtypeStruct((SUBLANES, D), F32)],
        compiler_params=_cp(("arbitrary",), 12 * _nbytes((tm, D), F32)),
    )(*args)


def _loss(x2, tgt, *, name):
    R = x2.shape[0]
    tb = ROW0

    def body(x_ref, t_ref, dy_ref, acc_ref):
        i = pl.program_id(0)

        @pl.when(i == 0)
        def _():
            acc_ref[...] = jnp.zeros(acc_ref.shape, F32)
            dy_ref[...] = jnp.zeros(dy_ref.shape, F32)

        @pl.when(i > 0)
        def _():
            err = x_ref[...] - t_ref[...]
            dy_ref[...] = err / D
            acc_ref[0:1, :] += jnp.sum(err * err, axis=0, keepdims=True)

    return pl.pallas_call(
        body, name=name, grid=(R // tb,),
        in_specs=[pl.BlockSpec((tb, D), lambda i: (i, 0)),
                  pl.BlockSpec((tb, D), lambda i: (jnp.maximum(i - 1, 0), 0))],
        out_specs=[pl.BlockSpec((tb, D), lambda i: (i, 0)), pl.BlockSpec((SUBLANES, D), lambda i: (0, 0))],
        out_shape=[jax.ShapeDtypeStruct((R, D), F32), jax.ShapeDtypeStruct((SUBLANES, D), F32)],
        compiler_params=_cp(("arbitrary",), 8 * _nbytes((tb, D), F32)),
    )(x2, tgt)


def _row_ids(start, n):
    return start + lax.broadcasted_iota(jnp.int32, (n, 1), 0)


def _ffn_fwd(up, taps, bias, *, tb, name):
    R, F2 = up.shape
    F = F2 // 2
    nh = tb // FH

    def body(u_ref, up_ref, w_ref, b_ref, a_ref):
        ue = jnp.concatenate([up_ref[...], u_ref[...]], axis=0)
        u = (w_ref[0:1, :] * pltpu.roll(ue, 2, 0)[FH:] + w_ref[1:2, :] * pltpu.roll(ue, 1, 0)[FH:]
             + w_ref[2:3, :] * u_ref[...] + b_ref[...])
        gate = u[:, :F]
        val = u[:, F:]
        a_ref[...] = (gate * jax.nn.sigmoid(gate) * val).astype(a_ref.dtype)

    est = 2 * _nbytes((tb, F2), F32) * 4
    return pl.pallas_call(
        body, name=name, grid=(R // tb,),
        in_specs=[pl.BlockSpec((tb, F2), lambda i: (i, 0)),
                  pl.BlockSpec((FH, F2), lambda i: (jnp.maximum(i * nh - 1, 0), 0)),
                  pl.BlockSpec((SUBLANES, F2), lambda i: (0, 0)), pl.BlockSpec((1, F2), lambda i: (0, 0))],
        out_specs=pl.BlockSpec((tb, F), lambda i: (i, 0)),
        out_shape=jax.ShapeDtypeStruct((R, F), MX),
        compiler_params=_cp(("parallel",), est),
    )(up, up, taps, bias)


def _ffn_bwd(da, up, taps, bias, *, tb, dep=None, name):
    R, F2 = up.shape
    F = F2 // 2
    nh = tb // FH
    nb = R // tb
    last = R // FH - 1
    m = tb + FH

    def body(da_ref, dan_ref, u_ref, up_ref, un_ref, w_ref, b_ref, du_ref, acc_ref):
        i = pl.program_id(0)

        @pl.when(i == 0)
        def _():
            acc_ref[...] = jnp.zeros(acc_ref.shape, F32)

        inside = (i < nb - 1).astype(F32)
        n = FH + m
        ue = jnp.concatenate([up_ref[...], u_ref[...], un_ref[...] * inside], axis=0)
        w0, w1, w2 = w_ref[0:1, :], w_ref[1:2, :], w_ref[2:3, :]
        u = (w0 * pltpu.roll(ue, 2, 0) + w1 * pltpu.roll(ue, 1, 0) + w2 * ue + b_ref[...])[FH:n]
        dae = jnp.concatenate([da_ref[...], dan_ref[...] * inside], axis=0)
        gate = u[:, :F]
        val = u[:, F:]
        sg = jax.nn.sigmoid(gate)
        gs = gate * sg
        du = jnp.concatenate([dae * val * (sg * (1.0 + gate - gs)), dae * gs], axis=1)
        du0 = du[0:tb]
        du1 = pltpu.roll(du, m - 1, 0)[0:tb]
        du2 = pltpu.roll(du, m - 2, 0)[0:tb]
        du_ref[...] = (w2 * du0 + w1 * du1 + w0 * du2).astype(du_ref.dtype)

        @pl.when(i * tb < PADR)
        def _():
            row = _row_ids(i * tb, tb)
            du_ref[...] = jnp.where(row >= PADR, du_ref[...], jnp.zeros((), du_ref.dtype))

        x = u_ref[...]
        acc_ref[0:1, :] += jnp.sum(du2 * x, axis=0, keepdims=True)
        acc_ref[1:2, :] += jnp.sum(du1 * x, axis=0, keepdims=True)
        acc_ref[2:3, :] += jnp.sum(du0 * x, axis=0, keepdims=True)
        acc_ref[3:4, :] += jnp.sum(du0, axis=0, keepdims=True)

    prev = lambda i: (jnp.maximum(i * nh - 1, 0), 0)
    nxt = lambda i: (jnp.minimum(i * nh + nh, last), 0)
    cur = lambda i: (i, 0)
    fix = lambda i: (0, 0)
    est = 12 * _nbytes((tb + 2 * FH, F2), F32)
    body, in_specs, args = _after(
        dep, body, [pl.BlockSpec((tb, F), cur), pl.BlockSpec((FH, F), nxt),
                    pl.BlockSpec((tb, F2), cur), pl.BlockSpec((FH, F2), prev), pl.BlockSpec((FH, F2), nxt),
                    pl.BlockSpec((SUBLANES, F2), fix), pl.BlockSpec((1, F2), fix)], [da, da, up, up, up, taps, bias])
    return pl.pallas_call(
        body, name=name, grid=(R // tb,), in_specs=in_specs,
        out_specs=[pl.BlockSpec((tb, F2), cur), pl.BlockSpec((SUBLANES, F2), fix)],
        out_shape=[jax.ShapeDtypeStruct((R, F2), MX), jax.ShapeDtypeStruct((SUBLANES, F2), F32)],
        compiler_params=_cp(("arbitrary",), est),
    )(*args)


def _hg_consts():
    t = np.arange(CH)[:, None]
    j = np.arange(CH)[None, :]
    low = (j <= t).astype(np.float32)
    blocks = [low, (j > t).astype(np.float32)]
    for m in LEVELS:
        ref = (t // (2 * m)) * 2 * m + m - 1
        blocks.append(low - (j <= ref).astype(np.float32))
    mat = np.concatenate(blocks, axis=0)
    suf = (j >= t).astype(np.float32)
    return jnp.asarray(mat, BF), jnp.asarray(suf, BF)


def _split_dot(mat, x):
    hi = x.astype(BF)
    lo = (x - hi.astype(F32)).astype(BF)
    return (lax.dot_general(mat, hi, NN, preferred_element_type=F32)
            + lax.dot_general(mat, lo, NN, preferred_element_type=F32))


def _lane_select(a2, a4, a8, a16):
    lane = lax.broadcasted_iota(jnp.int32, (1, PW), 1)
    return jnp.where(lane < 64, a2, jnp.where(lane < 128, a4, jnp.where(lane < 192, a8, a16)))


def _pool_count(row):
    win = _lane_select(2.0, 4.0, 8.0, 16.0)
    t1 = jnp.maximum((row - PADR + 1).astype(F32), 1.0)
    return jnp.minimum(t1, win)


def _gates(fz, lb, valid):
    sig = jax.nn.sigmoid(fz)
    f = lb + (1.0 - lb) * sig
    lf = jnp.where(valid, jnp.log(jnp.maximum(f, F_FLOOR)), 0.0)
    kk = jnp.where(valid, (1.0 - lb) * (1.0 - sig), 0.0)
    return sig, f, lf, kk


def _level_masks():
    tt = lax.broadcasted_iota(jnp.int32, (CH, CH), 0)
    ss = lax.broadcasted_iota(jnp.int32, (CH, CH), 1)
    xr = tt ^ ss
    low = tt > ss
    return tt, ss, [(xr >= m) & (xr < 2 * m) & low for m in LEVELS]


def _intra(q, kk, ex, sl, tt, ss, masks):
    a = jnp.where(tt == ss, jnp.sum(q * kk, axis=-1, keepdims=True), 0.0)
    parts = []
    for l in range(len(LEVELS)):
        e = jnp.exp(-jnp.abs(ex[(2 + l) * CH:(3 + l) * CH, sl]))
        ql = (q * e).astype(MX)
        kl = (kk * e).astype(MX)
        a = a + jnp.where(masks[l], _dot(ql, kl, NT), 0.0)
        parts.append((e, ql, kl))
    return a, parts


def _mixer_fwd(h, taps, wbd, ps, gn, lb, mat, *, name):
    R = h.shape[0]
    nb = R // CH
    q4 = CH // HALO

    def body(h_ref, hp_ref, w_ref, wbd_ref, ps_ref, gn_ref, lb_ref, mat_ref, y_ref, o_ref, sp_ref, st_ref):
        i = pl.program_id(0)

        @pl.when(i == 0)
        def _():
            st_ref[...] = jnp.zeros(st_ref.shape, F32)

        row = _row_ids(i * CH, CH)
        valid = row >= PADR
        rowe = _row_ids(i * CH - HALO, CH + HALO)
        valide = rowe >= PADR
        se = jnp.concatenate([hp_ref[:, 256:512] * hp_ref[:, 512:768], h_ref[:, 256:512] * h_ref[:, 512:768]], axis=0)
        se = jnp.where(valide, se, 0.0)
        conv = (w_ref[0:1, :] * pltpu.roll(se, 2, 0) + w_ref[1:2, :] * pltpu.roll(se, 1, 0)
                + w_ref[2:3, :] * se)[HALO:]
        y_ref[:, 0:CW] = (h_ref[:, 0:256] * conv).astype(y_ref.dtype)
        ve = jnp.where(valide, jnp.concatenate([hp_ref[:, 2816:3072], h_ref[:, 2816:3072]], axis=0), 0.0)
        s2 = ve + pltpu.roll(ve, 1, 0)
        s4 = s2 + pltpu.roll(s2, 2, 0)
        s8 = s4 + pltpu.roll(s4, 4, 0)
        s16 = s8 + pltpu.roll(s8, 8, 0)
        dp = (_lane_select(s2, s4, s8, s16) / _pool_count(rowe) - ve)[HALO:]
        y_ref[:, CW + HW:D] = (_dot(dp, wbd_ref[...], NN) * ps_ref[...]).astype(y_ref.dtype)
        lbv = lb_ref[...]
        _, _, lf, kk = _gates(h_ref[:, 1280:1792], lbv, valid)
        q = h_ref[:, 768:1280] * SCALE
        ii = h_ref[:, 1792:2304]
        gz = h_ref[:, 2304:2816]
        ex = _split_dot(mat_ref[...], lf)
        eg = jnp.exp(ex[0:CH])
        q0 = q * eg
        kr = kk * jnp.exp(ex[CH:2 * CH])
        tt, ss, masks = _level_masks()
        for hd in range(NH):
            sl = slice(HD * hd, HD * (hd + 1))
            st = st_ref[hd]
            sp_ref[hd] = st
            a, _ = _intra(q[:, sl], kk[:, sl], ex, sl, tt, ss, masks)
            o = _dot(q0[:, sl], st, NT) + _dot(a, ii[:, sl], NN)
            st_ref[hd] = st * eg[CH - 1:CH, sl] + _dot(ii[:, sl], kr[:, sl], TN)
            o_ref[:, sl] = o
            r = lax.rsqrt(jnp.mean(o * o, axis=-1, keepdims=True) + RMS_EPS)
            g_ = gz[:, sl]
            y_ref[:, CW + HD * hd:CW + HD * (hd + 1)] = (
                o * r * gn_ref[...] * (g_ * jax.nn.sigmoid(g_))).astype(y_ref.dtype)

    fix = lambda i: (0, 0)
    return pl.pallas_call(
        body, name=name, grid=(nb,),
        in_specs=[pl.BlockSpec((CH, DIN), lambda i: (i, 0)),
                  pl.BlockSpec((HALO, DIN), lambda i: (jnp.maximum(i * q4 - 1, 0), 0)),
                  pl.BlockSpec((SUBLANES, CW), fix), pl.BlockSpec((PW, PW), fix), pl.BlockSpec((1, PW), fix),
                  pl.BlockSpec((1, HD), fix), pl.BlockSpec((1, HW), fix), pl.BlockSpec(mat.shape, fix)],
        out_specs=[pl.BlockSpec((CH, D), lambda i: (i, 0)), pl.BlockSpec((CH, HW), lambda i: (i, 0)),
                   pl.BlockSpec((None, NH, HD, HD), lambda i: (i, 0, 0, 0))],
        out_shape=[jax.ShapeDtypeStruct((R, D), MX), jax.ShapeDtypeStruct((R, HW), F32),
                   jax.ShapeDtypeStruct((nb, NH, HD, HD), F32)],
        scratch_shapes=[pltpu.VMEM((NH, HD, HD), F32)],
        compiler_params=_cp(("arbitrary",), 32 * 2 ** 20),
    )(h, h, taps, wbd, ps, gn, lb, mat)


def _mixer_bwd(h, dy, o, sp, taps, wbd, ps, gn, lb, mat, suf, *, dep=None, name):
    R = h.shape[0]
    nb = R // CH
    q4 = CH // HALO
    lasth = R // HALO - 1

    def body(h_ref, hp_ref, hn_ref, dy_ref, dyn_ref, o_ref, sp_ref, w_ref, wbd_ref, ps_ref, gn_ref, lb_ref,
             mat_ref, suf_ref, dh_ref, dw_ref, dsc_ref, dgn_ref, dlb_ref, dwbd_ref, dst_ref):
        i = pl.program_id(0)
        b = nb - 1 - i

        @pl.when(i == 0)
        def _():
            dst_ref[...] = jnp.zeros(dst_ref.shape, F32)
            dw_ref[...] = jnp.zeros(dw_ref.shape, F32)
            dsc_ref[...] = jnp.zeros(dsc_ref.shape, F32)
            dgn_ref[...] = jnp.zeros(dgn_ref.shape, F32)
            dlb_ref[...] = jnp.zeros(dlb_ref.shape, F32)
            dwbd_ref[...] = jnp.zeros(dwbd_ref.shape, F32)

        ne = CH + 2 * HALO
        nc = CH + HALO
        row = _row_ids(b * CH, CH)
        valid = row >= PADR
        rowe = _row_ids(b * CH - HALO, ne)
        valide = (rowe >= PADR) & (rowe < R)
        rown = rowe[HALO:]
        validn = rown < R

        def cat3(lo, hi):
            return jnp.concatenate([hp_ref[:, lo:hi], h_ref[:, lo:hi], hn_ref[:, lo:hi]], axis=0)

        def catd(lo, hi):
            return jnp.where(validn, jnp.concatenate([dy_ref[:, lo:hi], dyn_ref[:, lo:hi]], axis=0), 0.0)

        w0, w1, w2 = w_ref[0:1, :], w_ref[1:2, :], w_ref[2:3, :]
        cce = cat3(256, 512)
        cve = cat3(512, 768)
        se = jnp.where(valide, cce * cve, 0.0)
        sm2 = pltpu.roll(se, 2, 0)
        sm1 = pltpu.roll(se, 1, 0)
        conv = (w0 * sm2 + w1 * sm1 + w2 * se)[HALO:HALO + CH]
        cbn = jnp.concatenate([h_ref[:, 0:256], hn_ref[:, 0:256]], axis=0)
        dconv = catd(0, CW) * cbn
        ds = w2 * dconv + w1 * pltpu.roll(dconv, nc - 1, 0) + w0 * pltpu.roll(dconv, nc - 2, 0)
        ds = jnp.where(valid, ds[0:CH], 0.0)
        dcv = dconv[0:CH]
        dw_ref[0:1, :] += jnp.sum(dcv * sm2[HALO:HALO + CH], axis=0, keepdims=True)
        dw_ref[1:2, :] += jnp.sum(dcv * sm1[HALO:HALO + CH], axis=0, keepdims=True)
        dw_ref[2:3, :] += jnp.sum(dcv * se[HALO:HALO + CH], axis=0, keepdims=True)
        dh_ref[:, 0:256] = jnp.where(valid, dy_ref[:, 0:CW] * conv, 0.0).astype(dh_ref.dtype)
        dh_ref[:, 256:512] = (ds * h_ref[:, 512:768]).astype(dh_ref.dtype)
        dh_ref[:, 512:768] = (ds * h_ref[:, 256:512]).astype(dh_ref.dtype)

        ve = jnp.where(valide, cat3(2816, 3072), 0.0)
        s2 = ve + pltpu.roll(ve, 1, 0)
        s4 = s2 + pltpu.roll(s2, 2, 0)
        s8 = s4 + pltpu.roll(s4, 4, 0)
        s16 = s8 + pltpu.roll(s8, 8, 0)
        cnte = _pool_count(rowe)
        dp = (_lane_select(s2, s4, s8, s16) / cnte - ve)[HALO:HALO + CH]
        dyp = catd(CW + HW, D)
        pre = _dot(dp, wbd_ref[...], NN)
        dsc_ref[0:1, :] += jnp.sum(dyp[0:CH] * pre, axis=0, keepdims=True)
        dyps = dyp * ps_ref[...]
        dd = _dot(dyps, wbd_ref[...], NT)
        dwbd_ref[...] += _dot(dp, dyps[0:CH], TN)
        e = dd / cnte[HALO:]
        t2 = e + pltpu.roll(e, nc - 1, 0)
        t4 = t2 + pltpu.roll(t2, nc - 2, 0)
        t8 = t4 + pltpu.roll(t4, nc - 4, 0)
        t16 = t8 + pltpu.roll(t8, nc - 8, 0)
        dv = (_lane_select(t2, t4, t8, t16) - dd)[0:CH]
        dh_ref[:, 2816:3072] = jnp.where(valid, dv, 0.0).astype(dh_ref.dtype)

        lbv = lb_ref[...]
        sig, f, lf, kk = _gates(h_ref[:, 1280:1792], lbv, valid)
        q = h_ref[:, 768:1280] * SCALE
        ii = h_ref[:, 1792:2304]
        gz = h_ref[:, 2304:2816]
        ex = _split_dot(mat_ref[...], lf)
        eg = jnp.exp(ex[0:CH])
        egr = jnp.exp(ex[CH:2 * CH])
        q0 = q * eg
        kr = kk * egr
        tt, ss, masks = _level_masks()
        gnv = gn_ref[...]
        dqs, dks, dis, dgzs, tails, dbs = [], [], [], [], [], []
        for hd in range(NH):
            sl = slice(HD * hd, HD * (hd + 1))
            ov = o_ref[:, sl]
            r = lax.rsqrt(jnp.mean(ov * ov, axis=-1, keepdims=True) + RMS_EPS)
            oh = ov * r
            g_ = gz[:, sl]
            sg = jax.nn.sigmoid(g_)
            dyv = dy_ref[:, CW + HD * hd:CW + HD * (hd + 1)]
            don = dyv * (g_ * sg)
            dgzs.append(dyv * (oh * gnv) * (sg * (1.0 + g_ * (1.0 - sg))))
            dgn_ref[0:1, :] += jnp.sum(don * oh, axis=0, keepdims=True)
            doh = don * gnv
            do = r * (doh - oh * jnp.mean(doh * oh, axis=-1, keepdims=True))
            st = sp_ref[hd]
            dst = dst_ref[hd]
            qh, kh, ih = q[:, sl], kk[:, sl], ii[:, sl]
            a, parts = _intra(qh, kh, ex, sl, tt, ss, masks)
            da = jnp.where(tt >= ss, _dot(do, ih, NT), 0.0)
            dis.append(_dot(a, do, TN) + _dot(kr[:, sl], dst, NT))
            q0h = q0[:, sl].astype(MX)
            krh = kr[:, sl].astype(MX)
            dq0 = _dot(do, st, NN)
            dkr = _dot(ih, dst, NN)
            dq = dq0 * eg[:, sl]
            dk = dkr * egr[:, sl]
            kdk = krh.astype(F32) * dkr
            db = q0h.astype(F32) * dq0 - kdk
            tails.append(jnp.sum(kdk, axis=0, keepdims=True)
                         + eg[CH - 1:CH, sl] * jnp.sum(dst * st, axis=0, keepdims=True))
            dga = jnp.sum(jnp.where(tt == ss, da, 0.0), axis=-1, keepdims=True)
            dq = dq + dga * kh
            dk = dk + dga * qh
            for l in range(len(LEVELS)):
                e_l, ql, kl = parts[l]
                dpl = jnp.where(masks[l], da, 0.0).astype(MX)
                dql = _dot(dpl, kl, NN)
                dkl = _dot(dpl, ql, TN)
                dq = dq + dql * e_l
                dk = dk + dkl * e_l
                db = db + (ql.astype(F32) * dql - kl.astype(F32) * dkl)
            dst_ref[hd] = dst * eg[CH - 1:CH, sl] + _dot(do, q0h, TN)
            dqs.append(dq)
            dks.append(dk)
            dbs.append(db)
        dq = jnp.concatenate(dqs, axis=1)
        dk = jnp.concatenate(dks, axis=1)
        db = jnp.concatenate(dbs, axis=1)
        dlf = _split_dot(suf_ref[...], db) + jnp.concatenate(tails, axis=1)
        t = jnp.where(valid, dlf * jnp.where(f > F_FLOOR, 1.0 / f, 0.0) - dk, 0.0)
        dlb_ref[0:1, :] += jnp.sum(t * (1.0 - sig), axis=0, keepdims=True)
        dh_ref[:, 768:1280] = jnp.where(valid, dq * SCALE, 0.0).astype(dh_ref.dtype)
        dh_ref[:, 1280:1792] = (t * (1.0 - lbv) * (sig * (1.0 - sig))).astype(dh_ref.dtype)
        dh_ref[:, 1792:2304] = jnp.where(valid, jnp.concatenate(dis, axis=1), 0.0).astype(dh_ref.dtype)
        dh_ref[:, 2304:2816] = jnp.where(valid, jnp.concatenate(dgzs, axis=1), 0.0).astype(dh_ref.dtype)

    cur = lambda i: (nb - 1 - i, 0)
    prev = lambda i: (jnp.maximum((nb - 1 - i) * q4 - 1, 0), 0)
    nxt = lambda i: (jnp.minimum((nb - 1 - i) * q4 + q4, lasth), 0)
    fix = lambda i: (0, 0)
    body, in_specs, args = _after(
        dep, body,
        [pl.BlockSpec((CH, DIN), cur), pl.BlockSpec((HALO, DIN), prev), pl.BlockSpec((HALO, DIN), nxt),
         pl.BlockSpec((CH, D), cur), pl.BlockSpec((HALO, D), nxt), pl.BlockSpec((CH, HW), cur),
         pl.BlockSpec((None, NH, HD, HD), lambda i: (nb - 1 - i, 0, 0, 0)),
         pl.BlockSpec((SUBLANES, CW), fix), pl.BlockSpec((PW, PW), fix), pl.BlockSpec((1, PW), fix),
         pl.BlockSpec((1, HD), fix), pl.BlockSpec((1, HW), fix), pl.BlockSpec(mat.shape, fix),
         pl.BlockSpec(suf.shape, fix)],
        [h, h, h, dy, dy, o, sp, taps, wbd, ps, gn, lb, mat, suf])
    return pl.pallas_call(
        body, name=name, grid=(nb,), in_specs=in_specs,
        out_specs=[pl.BlockSpec((CH, DIN), cur), pl.BlockSpec((SUBLANES, CW), fix), pl.BlockSpec((SUBLANES, PW), fix),
                   pl.BlockSpec((SUBLANES, HD), fix), pl.BlockSpec((SUBLANES, HW), fix), pl.BlockSpec((PW, PW), fix)],
        out_shape=[jax.ShapeDtypeStruct((R, DIN), MX), jax.ShapeDtypeStruct((SUBLANES, CW), F32),
                   jax.ShapeDtypeStruct((SUBLANES, PW), F32), jax.ShapeDtypeStruct((SUBLANES, HD), F32),
                   jax.ShapeDtypeStruct((SUBLANES, HW), F32), jax.ShapeDtypeStruct((PW, PW), F32)],
        scratch_shapes=[pltpu.VMEM((NH, HD, HD), F32)],
        compiler_params=_cp(("arbitrary",), 40 * 2 ** 20),
    )(*args)


def _sum_slots(recv, *, name):
    S, L, rows, cols = recv.shape
    tr = _row_tile(rows, S * cols * 4, 6 * 2 ** 20)

    def body(r_ref, o_ref):
        acc = r_ref[0]
        for s in range(1, S):
            acc = acc + r_ref[s]
        o_ref[...] = acc

    return pl.pallas_call(
        body, name=name, grid=(L, rows // tr),
        in_specs=[pl.BlockSpec((S, None, tr, cols), lambda l, i: (0, l, i, 0))],
        out_specs=pl.BlockSpec((None, tr, cols), lambda l, i: (l, i, 0)),
        out_shape=jax.ShapeDtypeStruct((L, rows, cols), F32),
        compiler_params=_cp(("parallel", "parallel"), 4 * S * tr * cols * 4),
    )(recv)


def _sum_own(recv, own, chip, *, name):
    S, rows, cols = recv.shape
    tr = _row_tile(rows, S * cols * 4, 6 * 2 ** 20)

    def body(me_ref, r_ref, o_ref, out_ref):
        me = me_ref[0]
        acc = None
        for s in range(S):
            t = jnp.where(me == s, o_ref[...], r_ref[s])
            acc = t if acc is None else acc + t
        out_ref[...] = acc

    grid_spec = pltpu.PrefetchScalarGridSpec(
        num_scalar_prefetch=1, grid=(rows // tr,),
        in_specs=[pl.BlockSpec((S, tr, cols), lambda i, me: (0, i, 0)),
                  pl.BlockSpec((None, tr, cols), lambda i, me: (me[0], i, 0))],
        out_specs=pl.BlockSpec((tr, cols), lambda i, me: (i, 0)))
    return pl.pallas_call(
        body, name=name, grid_spec=grid_spec, out_shape=jax.ShapeDtypeStruct((rows, cols), F32),
        compiler_params=_cp(("parallel",), 5 * S * tr * cols * 4),
    )(chip.reshape(1).astype(jnp.int32), recv, own)


def _adamw(w, m, v, ga, gb, *, layer, prev, name):
    L, rows, cols = w.shape
    tr = _row_tile(rows, cols * 4, 2 ** 20)
    two = gb is not None
    nin = 5 if two else 4

    def body(*refs):
        w_ref, m_ref, v_ref, a_ref = refs[:4]
        g_ref, d_ref, m2_ref, v2_ref = refs[-4:]
        g = a_ref[...] + refs[4][...] if two else a_ref[...]
        m2 = ADAM_B1 * m_ref[...] + (1.0 - ADAM_B1) * g
        v2 = ADAM_B2 * v_ref[...] + (1.0 - ADAM_B2) * (g * g)
        m_hat = m2 / (1.0 - ADAM_B1 ** ADAM_STEP)
        v_hat = v2 / (1.0 - ADAM_B2 ** ADAM_STEP)
        g_ref[...] = g
        d_ref[...] = -ADAM_LR * (m_hat / (jnp.sqrt(v_hat) + ADAM_EPS) + ADAM_WD * w_ref[...])
        m2_ref[...] = m2
        v2_ref[...] = v2

    spec = pl.BlockSpec((None, tr, cols), lambda i: (layer, i, 0))
    gspec = pl.BlockSpec((tr, cols), lambda i: (i, 0))
    args = [w, m, v, ga] + ([gb] if two else [])
    in_specs = [spec] * 3 + [gspec] * (nin - 3)
    aliases = {}
    if prev is not None:
        args += list(prev)
        in_specs += [ANY] * 4
        aliases = {nin + j: j for j in range(4)}
    sd = jax.ShapeDtypeStruct((L, rows, cols), F32)
    return pl.pallas_call(
        body, name=name, grid=(rows // tr,), in_specs=in_specs, out_specs=[spec] * 4,
        out_shape=[sd] * 4, input_output_aliases=aliases,
        compiler_params=_cp(("parallel",), 24 * tr * cols * 4),
    )(*args)


def _exchange(arrays, *, flips, n_slots, slot_of, scatter, self_copy, name):
    n = len(arrays)
    nf = len(flips)
    out_shapes = [jax.ShapeDtypeStruct(a.shape if scatter else (n_slots,) + a.shape, a.dtype) for a in arrays]

    def body(*refs):
        ins, outs = refs[:n], refs[n:2 * n]
        send_sems, recv_sems, loc_sems = refs[2 * n:]
        x, y, c = lax.axis_index("x"), lax.axis_index("y"), lax.axis_index("c")
        me = slot_of(x, y, c)
        peers = [(1 - x if fx else x, 1 - y if fy else y, 1 - c if fc else c) for fx, fy, fc in flips]
        local, remote = [], []
        for a in range(n):
            if self_copy:
                lc = pltpu.make_async_copy(ins[a].at[me] if scatter else ins[a], outs[a].at[me], loc_sems.at[a])
                lc.start()
                local.append(lc)
            for k, p in enumerate(peers):
                src = ins[a].at[slot_of(*p)] if scatter else ins[a]
                cp = pltpu.make_async_remote_copy(
                    src_ref=src, dst_ref=outs[a].at[me], send_sem=send_sems.at[a, k], recv_sem=recv_sems.at[a, k],
                    device_id=p, device_id_type=MESH)
                cp.start()
                remote.append(cp)
        for a in range(n):
            for k, p in enumerate(peers):
                src = ins[a].at[slot_of(*p)] if scatter else ins[a]
                pltpu.make_async_remote_copy(
                    src_ref=src, dst_ref=outs[a].at[slot_of(*p)], send_sem=send_sems.at[a, k],
                    recv_sem=recv_sems.at[a, k], device_id=p, device_id_type=MESH).wait_recv()
        for cp in remote:
            cp.wait_send()
        for lc in local:
            lc.wait()

    return pl.pallas_call(
        body, name=name, in_specs=[ANY] * n, out_specs=[ANY] * n, out_shape=out_shapes,
        scratch_shapes=[pltpu.SemaphoreType.DMA((n, nf)), pltpu.SemaphoreType.DMA((n, nf)),
                        pltpu.SemaphoreType.DMA((n,))],
        compiler_params=pltpu.CompilerParams(has_side_effects=True),
    )(*arrays)


CHIP_FLIPS = [(1, 0, 0), (0, 1, 0), (1, 1, 0)]
ALL_FLIPS = [(fx, fy, fc) for fx in (0, 1) for fy in (0, 1) for fc in (0, 1) if fx or fy or fc]


def _chip_slot(x, y, c):
    return 2 * x + y


def _dev_slot(x, y, c):
    return 4 * x + 2 * y + c


def _zero_slot(x, y, c):
    return 0


HBM_SPEC = pl.BlockSpec(memory_space=pltpu.HBM)
SEM_SPEC = pl.BlockSpec(memory_space=pltpu.SEMAPHORE)
N_PEER_CHIPS = 3


def _peer_chips(x, y, c):
    return [(1 - x, y, c), (x, 1 - y, c), (1 - x, 1 - y, c)]


def _send_start(srcs, *, scatter, dep=None, name):
    n = len(srcs)
    nc = n * N_PEER_CHIPS
    srcs = [pltpu.with_memory_space_constraint(s, pltpu.HBM) for s in srcs]
    land_shapes = [s.shape if scatter else (4,) + s.shape for s in srcs]
    lands = [pltpu.with_memory_space_constraint(lax.empty(sh, s.dtype), pltpu.HBM) for sh, s in zip(land_shapes, srcs)]

    deps = [] if dep is None else [dep]
    nd = len(deps)

    def body(*refs):
        ins, lnd = refs[:n], refs[n:2 * n]
        send_sems, recv_sems = refs[2 * n + nd:2 * n + nd + nc], refs[2 * n + nd + nc:2 * n + nd + 2 * nc]
        token = refs[-1]
        x, y, c = lax.axis_index("x"), lax.axis_index("y"), lax.axis_index("c")
        me = 2 * x + y
        for a in range(n):
            for k, p in enumerate(_peer_chips(x, y, c)):
                src = ins[a].at[2 * p[0] + p[1]] if scatter else ins[a]
                j = a * N_PEER_CHIPS + k
                pltpu.make_async_remote_copy(
                    src_ref=src, dst_ref=lnd[a].at[me], send_sem=send_sems[j], recv_sem=recv_sems[j],
                    device_id=p, device_id_type=MESH).start()
        token[...] = jnp.zeros(token.shape, token.dtype)

    sem = pltpu.SemaphoreType.DMA(())
    outs = pl.pallas_call(
        body, name=name,
        out_shape=(*[sem] * (2 * nc), *[pltpu.HBM(s.shape, s.dtype) for s in srcs],
                   *[pltpu.HBM(sh, s.dtype) for sh, s in zip(land_shapes, srcs)],
                   jax.ShapeDtypeStruct((SUBLANES, LANES), F32)),
        in_specs=[HBM_SPEC] * (2 * n) + [ANY] * nd,
        out_specs=(*[SEM_SPEC] * (2 * nc), *[HBM_SPEC] * (2 * n), pl.BlockSpec(memory_space=pltpu.VMEM)),
        input_output_aliases={i: 2 * nc + i for i in range(2 * n)},
        compiler_params=pltpu.CompilerParams(has_side_effects=pltpu.SideEffectType.DATAFLOW_SIDE_EFFECTING),
    )(*srcs, *lands, *deps)
    return dict(sems=list(outs[:2 * nc]), srcs=list(outs[2 * nc:2 * nc + n]),
                lands=list(outs[2 * nc + n:2 * nc + 2 * n]), token=outs[-1])


def _send_wait(h, *, scatter, after, name):
    n = len(h["srcs"])
    nc = n * N_PEER_CHIPS

    def body(*refs):
        ins, lnd = refs[:n], refs[n:2 * n]
        send_sems, recv_sems = refs[2 * n:2 * n + nc], refs[2 * n + nc:2 * n + 2 * nc]
        x, y, c = lax.axis_index("x"), lax.axis_index("y"), lax.axis_index("c")
        for a in range(n):
            for k, p in enumerate(_peer_chips(x, y, c)):
                slot = 2 * p[0] + p[1]
                j = a * N_PEER_CHIPS + k
                cp = pltpu.make_async_remote_copy(
                    src_ref=ins[a].at[slot] if scatter else ins[a], dst_ref=lnd[a].at[slot],
                    send_sem=send_sems[j], recv_sem=recv_sems[j], device_id=p, device_id_type=MESH)
                cp.wait_send()
                cp.wait_recv()

    thru = h["srcs"] + h["lands"]
    outs = pl.pallas_call(
        body, name=name, out_shape=tuple(pltpu.HBM(t.shape, t.dtype) for t in thru),
        in_specs=[HBM_SPEC] * (2 * n) + [SEM_SPEC] * (2 * nc) + [ANY] * len(after),
        out_specs=tuple([HBM_SPEC] * (2 * n)),
        input_output_aliases={i: i for i in range(2 * n)},
        compiler_params=pltpu.CompilerParams(has_side_effects=pltpu.SideEffectType.DATAFLOW_SIDE_EFFECTING),
    )(*thru, *h["sems"], *after)
    return list(outs[:n]), list(outs[n:])


def _assemble(land, own, chip, axis):
    return jnp.concatenate([jnp.where(chip == k, own, land[k]) for k in range(4)], axis=axis)


def _pack(arrs):
    flat = jnp.concatenate([a.reshape(-1).astype(F32) for a in arrs])
    tile = SUBLANES * LANES
    pad = (-flat.shape[0]) % tile
    return jnp.pad(flat, (0, pad)).reshape(-1, LANES)


def _unpack(buf, shapes):
    flat = buf.reshape(-1)
    out, off = [], 0
    for s in shapes:
        n = int(np.prod(s))
        out.append(flat[off:off + n].reshape(s))
        off += n
    return out


def _lower_bounds(hg_lower_bounds):
    p = jax.nn.softmax(hg_lower_bounds.astype(F32), axis=0)
    return jnp.cumsum(p, axis=0) - p[0]


def kernel(x, meta_tokens, hg_lower_bounds, w_in, w_conv, w_pool, pool_scale, hg_norm_g, w_o, ln1_g, ln1_b, w_up, w_ffn_conv, b_ffn_conv, w_down, ln2_g, ln2_b, loss_target, m_meta_tokens, m_hg_lower_bounds, m_w_in, m_w_conv, m_w_pool, m_pool_scale, m_hg_norm_g, m_w_o, m_ln1_g, m_ln1_b, m_w_up, m_w_ffn_conv, m_b_ffn_conv, m_w_down, m_ln2_g, m_ln2_b, v_meta_tokens, v_hg_lower_bounds, v_w_in, v_w_conv, v_w_pool, v_pool_scale, v_hg_norm_g, v_w_o, v_ln1_g, v_ln1_b, v_w_up, v_w_ffn_conv, v_b_ffn_conv, v_w_down, v_ln2_g, v_ln2_b):
    S = x.shape[1]
    R = S + ROW0
    Fq = w_down.shape[1]
    F = 4 * Fq
    F2 = 2 * F
    F2q = w_up.shape[2]
    assert x.shape == (1, S, D) and R % 384 == 0 and S % ROW0 == 0
    chip = 2 * lax.axis_index("x") + lax.axis_index("y")
    tm = 384
    tm_w = max(t for t in range(SUBLANES, 1057, SUBLANES) if R % t == 0)
    tb_ffn = 128

    small_shapes = [(N_META, D // 4), (DEPTH, CW // 4, 3), (DEPTH, F2q, 3)]
    wb_in, wb_o, wb_up, wb_down = (w.astype(MX) for w in (w_in, w_o, w_up, w_down))
    h_a = _send_start([wb_in[0], _pack([meta_tokens, w_conv, w_ffn_conv])], scatter=False, name="gather_a_start")
    (own_in, own_small), (l_in, l_small) = _send_wait(h_a, scatter=False, after=[h_a["token"]], name="gather_a_wait")
    h_b = _send_start([wb_o[0], wb_up[0], wb_down[0]], scatter=False, dep=l_in, name="gather_b_start")
    Win, Wo, Wup, Wdown = {}, {}, {}, {}
    Win[0] = _assemble(l_in, own_in, chip, 1)
    sm = [_unpack(jnp.where(chip == k, own_small, l_small[k]), small_shapes) for k in range(4)]
    meta_full = jnp.concatenate([sm[k][0] for k in range(4)], axis=1)
    wconv_full = jnp.concatenate([sm[k][1] for k in range(4)], axis=1)
    wffn_full = jnp.concatenate([sm[k][2] for k in range(4)], axis=1)
    taps_c = jnp.pad(wconv_full.transpose(0, 2, 1), ((0, 0), (0, SUBLANES - 3), (0, 0)))
    taps_f = jnp.pad(wffn_full.transpose(0, 2, 1), ((0, 0), (0, SUBLANES - 3), (0, 0)))
    wbd = jnp.stack([jax.scipy.linalg.block_diag(*[w_pool[l, g] for g in range(4)]) for l in range(DEPTH)]).astype(MX)
    lbs, lbs_vjp = jax.vjp(_lower_bounds, hg_lower_bounds)
    mat, suf = _hg_consts()

    def fwd_mixer(l, X, dep=None):
        h = _mm(X, Win[l], tm=tm, dep=dep, name=f"mm_in_{l}")
        y, o, sp = _mixer_fwd(h, taps_c[l], wbd[l], pool_scale[l].reshape(1, PW), hg_norm_g[l].reshape(1, HD),
                              lbs[l].reshape(1, HW), mat, name=f"mixer_fwd_{l}")
        return h, y, o, sp

    def fwd_rest(l, X, h, y, o, sp, dep=None):
        x1, xh1, r1 = _mm_ln(y, Wo[l], X, ln1_g[l], ln1_b[l], tm=tm, dep=dep, name=f"mm_o_ln_{l}")
        up = _mm(x1, Wup[l], tm=tm // 2, zero_inert=True, name=f"mm_up_{l}")
        a = _ffn_fwd(up, taps_f[l], b_ffn_conv[l].reshape(1, F2), tb=tb_ffn, name=f"ffn_fwd_{l}")
        x2, xh2, r2 = _mm_ln(a, Wdown[l], x1, ln2_g[l], ln2_b[l], tm=tm, name=f"mm_down_ln_{l}")
        return (X, h, y, o, sp, x1, xh1, r1, up, a, xh2, r2), x2

    X = jnp.concatenate([jnp.zeros((PADR, D), F32), meta_full, x[0]], axis=0)
    h, y, o, sp = fwd_mixer(0, X, dep=h_b["token"])
    (own_o, own_up, own_down), (l_o, l_up, l_down) = _send_wait(h_b, scatter=False, after=[y], name="gather_b_wait")
    Wo[0], Wup[0], Wdown[0] = (_assemble(l_o, own_o, chip, 0), _assemble(l_up, own_up, chip, 1),
                               _assemble(l_down, own_down, chip, 0))
    h_c = _send_start([wb_in[1], wb_o[1], wb_up[1], wb_down[1]], scatter=False, dep=l_o, name="gather_c_start")
    saved0, X1 = fwd_rest(0, X, h, y, o, sp, dep=h_c["token"])
    own_c, l_c = _send_wait(h_c, scatter=False, after=[X1], name="gather_c_wait")
    Win[1], Wo[1] = _assemble(l_c[0], own_c[0], chip, 1), _assemble(l_c[1], own_c[1], chip, 0)
    Wup[1], Wdown[1] = _assemble(l_c[2], own_c[2], chip, 1), _assemble(l_c[3], own_c[3], chip, 0)
    saved1, X2 = fwd_rest(1, X1, *fwd_mixer(1, X1))
    saved = [saved0, saved1]

    dxo, lacc = _loss(X2, loss_target[0], name="loss")
    loss = lax.psum(0.5 * jnp.sum(lacc[0]) / D, ("x", "y", "c"))

    sc = {}

    def scatter(nm, l, g):
        sc[nm, l] = _send_start([g], scatter=True, name=f"scatter_{nm}_{l}_start")
        return sc[nm, l]["token"]

    tok = None

    small_g = [None] * DEPTH
    for l in reversed(range(DEPTH)):
        X, h, y, o, sp, x1, xh1, r1, up, a, xh2, r2 = saved[l]
        dz2, gb2 = _ln_bwd(dxo, xh2, r2, ln2_g[l], tm=tm, dep=tok, name=f"ln2_bwd_{l}")
        da = _mm(dz2, Wdown[l], nt=True, tm=tm, name=f"mm_da_{l}")
        tok = scatter("w_down", l, _wgrad(a, dz2, slabs_on_cols=False, tm=tm_w, tn=512, name=f"wgrad_down_{l}"))
        dup, facc = _ffn_bwd(da, up, taps_f[l], b_ffn_conv[l].reshape(1, F2), tb=tb_ffn, dep=tok, name=f"ffn_bwd_{l}")
        dx1 = _mm(dup, Wup[l], nt=True, res=dz2, tm=tm, name=f"mm_dx1_{l}")
        tok = scatter("w_up", l, _wgrad(x1, dup, slabs_on_cols=True, tm=tm_w, tn=F2q, name=f"wgrad_up_{l}"))
        dz1, gb1 = _ln_bwd(dx1, xh1, r1, ln1_g[l], tm=tm, dep=tok, name=f"ln1_bwd_{l}")
        dym = _mm(dz1, Wo[l], nt=True, tm=tm, name=f"mm_dym_{l}")
        tok = scatter("w_o", l, _wgrad(y, dz1, slabs_on_cols=False, tm=tm_w, tn=512, name=f"wgrad_o_{l}"))
        dh, dwc, dsc, dgn, dlb, dwbd = _mixer_bwd(
            h, dym, o, sp, taps_c[l], wbd[l], pool_scale[l].reshape(1, PW), hg_norm_g[l].reshape(1, HD),
            lbs[l].reshape(1, HW), mat, suf, dep=tok, name=f"mixer_bwd_{l}")
        tok = scatter("w_in", l, _wgrad(X, dh, slabs_on_cols=True, tm=tm_w, tn=DIN // 4, name=f"wgrad_in_{l}"))
        dX = _mm(dh, Win[l], nt=True, res=dz1, tm=tm, dep=tok, name=f"mm_dx_{l}")
        small_g[l] = dict(
            lbs=dlb[0], w_conv=dwc[0:3].T, w_pool=jnp.stack([dwbd[64 * g:64 * g + 64, 64 * g:64 * g + 64] for g in range(4)]),
            pool_scale=dsc[0], hg_norm_g=dgn[0], ln1_g=gb1[0], ln1_b=gb1[1], w_ffn_conv=facc[0:3].T,
            b_ffn_conv=facc[3], ln2_g=gb2[0], ln2_b=gb2[1])
        dxo = dX
    grad_x = dxo[ROW0:][None]

    sg_names = ["lbs", "w_conv", "w_pool", "pool_scale", "hg_norm_g", "ln1_g", "ln1_b", "w_ffn_conv",
                "b_ffn_conv", "ln2_g", "ln2_b"]
    sg_list = [dxo[PADR:ROW0]] + [jnp.stack([small_g[l][nm] for l in range(DEPTH)]) for nm in sg_names]
    sg_shapes = [a.shape for a in sg_list]
    packed = _pack(sg_list)
    (gathered,) = _exchange([packed], flips=ALL_FLIPS, n_slots=8, slot_of=_dev_slot, scatter=False, self_copy=True,
                            name="gather_small_grads")
    total = _sum_slots(gathered[:, None], name="sum_small_grads")[0]
    tot = dict(zip(["meta_tokens"] + sg_names, _unpack(total, sg_shapes)))
    (g_hg,) = lbs_vjp(tot["lbs"])
    small_grads = dict(
        meta_tokens=lax.dynamic_slice_in_dim(tot["meta_tokens"], chip * (D // 4), D // 4, axis=1),
        hg_lower_bounds=g_hg,
        w_conv=lax.dynamic_slice_in_dim(tot["w_conv"], chip * (CW // 4), CW // 4, axis=1),
        w_pool=tot["w_pool"], pool_scale=tot["pool_scale"], hg_norm_g=tot["hg_norm_g"],
        ln1_g=tot["ln1_g"], ln1_b=tot["ln1_b"],
        w_ffn_conv=lax.dynamic_slice_in_dim(tot["w_ffn_conv"], chip * F2q, F2q, axis=1),
        b_ffn_conv=tot["b_ffn_conv"], ln2_g=tot["ln2_g"], ln2_b=tot["ln2_b"])
    small_w = dict(meta_tokens=(meta_tokens, m_meta_tokens, v_meta_tokens),
                   hg_lower_bounds=(hg_lower_bounds, m_hg_lower_bounds, v_hg_lower_bounds),
                   w_conv=(w_conv, m_w_conv, v_w_conv), w_pool=(w_pool, m_w_pool, v_w_pool),
                   pool_scale=(pool_scale, m_pool_scale, v_pool_scale), hg_norm_g=(hg_norm_g, m_hg_norm_g, v_hg_norm_g),
                   ln1_g=(ln1_g, m_ln1_g, v_ln1_g), ln1_b=(ln1_b, m_ln1_b, v_ln1_b),
                   w_ffn_conv=(w_ffn_conv, m_w_ffn_conv, v_w_ffn_conv), b_ffn_conv=(b_ffn_conv, m_b_ffn_conv, v_b_ffn_conv),
                   ln2_g=(ln2_g, m_ln2_g, v_ln2_g), ln2_b=(ln2_b, m_ln2_b, v_ln2_b))
    names_s = list(small_w)
    shapes_s = [small_w[nm][0].shape for nm in names_s]
    pk = [_pack([small_w[nm][j] for nm in names_s])[None] for j in range(3)]
    pg = _pack([small_grads[nm] for nm in names_s])[None]
    outs_s = _adamw(pk[0], pk[1], pk[2], pg[0], None, layer=0, prev=None, name="adamw_small")
    small = {nm: [] for nm in names_s}
    for j in range(4):
        for nm, val in zip(names_s, _unpack(outs_s[j][0], shapes_s)):
            small[nm].append(val)

    big_w = dict(w_in=(w_in, m_w_in, v_w_in), w_o=(w_o, m_w_o, v_w_o), w_up=(w_up, m_w_up, v_w_up),
                 w_down=(w_down, m_w_down, v_w_down))
    big = {nm: None for nm in big_w}
    for l in reversed(range(DEPTH)):
        part = {}
        for nm in ("w_down", "w_up", "w_o", "w_in"):
            after = [dxo] if (nm, l) != ("w_in", 0) else [outs_s[0]] + [big[k][0] for k in big_w] + list(part.values())
            (own,), (recv,) = _send_wait(sc[nm, l], scatter=True, after=after, name=f"scatter_{nm}_{l}_wait")
            part[nm] = _sum_own(recv, own, chip, name=f"sum_{nm}_{l}")
        sib = _exchange([part[nm] for nm in big_w], flips=[(0, 0, 1)], n_slots=1, slot_of=_zero_slot, scatter=False,
                        self_copy=False, name=f"swap_cores_{l}")
        for k, nm in enumerate(big_w):
            w, m, v = big_w[nm]
            big[nm] = _adamw(w, m, v, part[nm], sib[k][0], layer=l, prev=big[nm], name=f"adamw_{nm}_{l}")

    order = ["meta_tokens", "hg_lower_bounds", "w_in", "w_conv", "w_pool", "pool_scale", "hg_norm_g", "w_o",
             "ln1_g", "ln1_b", "w_up", "w_ffn_conv", "b_ffn_conv", "w_down", "ln2_g", "ln2_b"]
    res = {nm: (big[nm] if nm in big else small[nm]) for nm in order}
    outs = [loss, grad_x]
    for j in range(4):
        outs += [res[nm][j] for nm in order]
    return tuple(outs)
```

```python
import functools

import numpy as np

import jax
import jax.numpy as jnp
from jax import lax
from jax.experimental import pallas as pl
from jax.experimental.pallas import tpu as pltpu

F32 = jnp.float32
BF = jnp.bfloat16
MX = jnp.bfloat16

D = 1024
CW = 256
HW = 512
HD = 128
NH = 4
PW = 256
DIN = 3072
N_META = 16
CH = 64
ROW0 = 256
PADR = ROW0 - N_META
HALO = 16
FH = 8
LEVELS = (32, 16, 8, 4, 2, 1)
DEPTH = 2
ALPHA = (2 * DEPTH) ** 0.25
LN_EPS = 1e-5
RMS_EPS = 1e-6
F_FLOOR = 1e-30
SCALE = HD ** -0.5
ADAM_LR, ADAM_B1, ADAM_B2, ADAM_EPS, ADAM_WD, ADAM_STEP = 0.001, 0.9, 0.999, 1e-08, 0.01, 10

VMEM_V7X = 64 * 2 ** 20
LANES = 128
SUBLANES = 8

NN = (((1,), (0,)), ((), ()))
NT = (((1,), (1,)), ((), ()))
TN = (((0,), (0,)), ((), ()))
MESH = pl.DeviceIdType.MESH
ANY = pl.BlockSpec(memory_space=pl.ANY)


def _dot(a, b, dn):
    return lax.dot_general(a.astype(MX), b.astype(MX), dn, preferred_element_type=F32)


def _cp(sem, est_bytes):
    lim = int(min(VMEM_V7X - 6 * 2 ** 20, max(32 * 2 ** 20, est_bytes)))
    return pltpu.CompilerParams(dimension_semantics=sem, vmem_limit_bytes=lim)


def _nbytes(shape, dtype):
    return int(np.prod(shape)) * jnp.dtype(dtype).itemsize


def _row_tile(rows, row_bytes, budget):
    best = SUBLANES
    for t in range(SUBLANES, rows + 1, SUBLANES):
        if rows % t == 0 and t * row_bytes <= budget:
            best = t
    return best


def _after(dep, body, in_specs, args):
    if dep is None:
        return body, list(in_specs), list(args)

    def body_after(dep_ref, *refs):
        body(*refs)

    return body_after, [ANY] + list(in_specs), [dep] + list(args)


def _mm(a, w, *, nt=False, res=None, tm, out_dtype=F32, zero_inert=False, dep=None, name):
    R, K = a.shape
    N = w.shape[0] if nt else w.shape[1]
    dn = NT if nt else NN

    def body(*refs):
        if res is None:
            a_ref, w_ref, o_ref = refs
        else:
            a_ref, w_ref, r_ref, o_ref = refs
        acc = _dot(a_ref[...], w_ref[...], dn)
        if res is not None:
            acc = acc + ALPHA * r_ref[...]
        if zero_inert:
            acc = jnp.where(_row_ids(pl.program_id(0) * tm, tm) >= PADR, acc, 0.0)
        o_ref[...] = acc.astype(out_dtype)

    in_specs = [pl.BlockSpec((tm, K), lambda i: (i, 0)), pl.BlockSpec(w.shape, lambda i: (0, 0))]
    args = [a, w]
    est = 2 * _nbytes((tm, K), a.dtype) + 2 * _nbytes(w.shape, w.dtype) + 3 * _nbytes((tm, N), F32)
    if res is not None:
        in_specs.append(pl.BlockSpec((tm, N), lambda i: (i, 0)))
        args.append(res)
        est += 2 * _nbytes((tm, N), F32)
    body, in_specs, args = _after(dep, body, in_specs, args)
    return pl.pallas_call(
        body, name=name, grid=(R // tm,), in_specs=in_specs,
        out_specs=pl.BlockSpec((tm, N), lambda i: (i, 0)),
        out_shape=jax.ShapeDtypeStruct((R, N), out_dtype),
        compiler_params=_cp(("parallel",), est + 4 * 2 ** 20),
    )(*args)


def _wgrad(a, b, *, slabs_on_cols, tm, tn, name):
    R, Ka = a.shape
    Nb = b.shape[1]
    if slabs_on_cols:
        out_shape = (4, Ka, Nb // 4)
        assert tn == Nb // 4
        out_spec = pl.BlockSpec((None, Ka, tn), lambda j, i: (j, 0, 0))
    else:
        out_shape = (4, Ka // 4, Nb)
        out_spec = pl.BlockSpec((4, Ka // 4, tn), lambda j, i: (0, 0, j))

    def body(a_ref, b_ref, o_ref):
        @pl.when(pl.program_id(1) == 0)
        def _():
            o_ref[...] = jnp.zeros(o_ref.shape, F32)

        acc = _dot(a_ref[...], b_ref[...], TN)
        o_ref[...] += acc.reshape(o_ref.shape)

    in_specs = [pl.BlockSpec((tm, Ka), lambda j, i: (i, 0)), pl.BlockSpec((tm, tn), lambda j, i: (i, j))]
    est = 2 * _nbytes((tm, Ka), a.dtype) + 2 * _nbytes((tm, tn), b.dtype) + 4 * _nbytes((Ka, tn), F32) \
        + _nbytes((tm, Ka), F32)
    return pl.pallas_call(
        body, name=name, grid=(Nb // tn, R // tm), in_specs=in_specs, out_specs=out_spec,
        out_shape=jax.ShapeDtypeStruct(out_shape, F32),
        compiler_params=_cp(("parallel", "arbitrary"), est + 4 * 2 ** 20),
    )(a, b)


def _mm_ln(a, w, xres, g, b, *, tm, dep=None, name):
    R, K = a.shape

    def body(a_ref, w_ref, x_ref, g_ref, b_ref, xo_ref, xh_ref, r_ref):
        z = ALPHA * x_ref[...] + _dot(a_ref[...], w_ref[...], NN)
        mu = jnp.mean(z, axis=-1, keepdims=True)
        zc = z - mu
        var = jnp.mean(zc * zc, axis=-1, keepdims=True)
        r = lax.rsqrt(var + LN_EPS)
        xh = zc * r
        xh_ref[...] = xh
        r_ref[...] = r
        xo_ref[...] = xh * g_ref[...] + b_ref[...]

    row = lambda i: (i, 0)
    fix = lambda i: (0, 0)
    est = 2 * _nbytes((tm, K), a.dtype) + 2 * _nbytes(w.shape, w.dtype) + 10 * _nbytes((tm, D), F32)
    body, in_specs, args = _after(
        dep, body, [pl.BlockSpec((tm, K), row), pl.BlockSpec(w.shape, fix), pl.BlockSpec((tm, D), row),
                    pl.BlockSpec((1, D), fix), pl.BlockSpec((1, D), fix)],
        [a, w, xres, g.reshape(1, D), b.reshape(1, D)])
    return pl.pallas_call(
        body, name=name, grid=(R // tm,), in_specs=in_specs,
        out_specs=[pl.BlockSpec((tm, D), row), pl.BlockSpec((tm, D), row), pl.BlockSpec((tm, 1), row)],
        out_shape=[jax.ShapeDtypeStruct((R, D), F32), jax.ShapeDtypeStruct((R, D), F32),
                   jax.ShapeDtypeStruct((R, 1), F32)],
        compiler_params=_cp(("parallel",), est + 4 * 2 ** 20),
    )(*args)


def _ln_bwd(dx, xh, r, g, *, tm, dep=None, name):
    R = dx.shape[0]

    def body(dx_ref, xh_ref, r_ref, g_ref, dz_ref, gb_ref):
        @pl.when(pl.program_id(0) == 0)
        def _():
            gb_ref[...] = jnp.zeros(gb_ref.shape, F32)

        dxv = dx_ref[...]
        xhv = xh_ref[...]
        dyh = dxv * g_ref[...]
        m1 = jnp.mean(dyh, axis=-1, keepdims=True)
        m2 = jnp.mean(dyh * xhv, axis=-1, keepdims=True)
        dz_ref[...] = r_ref[...] * (dyh - m1 - xhv * m2)
        gb_ref[0:1, :] += jnp.sum(dxv * xhv, axis=0, keepdims=True)
        gb_ref[1:2, :] += jnp.sum(dxv, axis=0, keepdims=True)

    row = lambda i: (i, 0)
    fix = lambda i: (0, 0)
    body, in_specs, args = _after(
        dep, body, [pl.BlockSpec((tm, D), row), pl.BlockSpec((tm, D), row), pl.BlockSpec((tm, 1), row),
                    pl.BlockSpec((1, D), fix)], [dx, xh, r, g.reshape(1, D)])
    return pl.pallas_call(
        body, name=name, grid=(R // tm,), in_specs=in_specs,
        out_specs=[pl.BlockSpec((tm, D), row), pl.BlockSpec((SUBLANES, D), fix)],
        out_shape=[jax.ShapeDtypeStruct((R, D), F32), jax.ShapeDtypeStruct((SUBLANES, D), F32)],
        compiler_params=_cp(("arbitrary",), 12 * _nbytes((tm, D), F32)),
    )(*args)


def _resident(shape):
    return pl.BlockSpec(shape, lambda i: (0,) * len(shape), pipeline_mode=pl.Buffered(1))


def _ln_bwd_mm(dx, xh, r, g, w, *, tm, dep=None, name):
    R = dx.shape[0]
    N = w.shape[0]

    def body(dx_ref, xh_ref, r_ref, g_ref, w_ref, dz_ref, gb_ref, o_ref):
        @pl.when(pl.program_id(0) == 0)
        def _():
            gb_ref[...] = jnp.zeros(gb_ref.shape, F32)

        dxv = dx_ref[...]
        xhv = xh_ref[...]
        dyh = dxv * g_ref[...]
        m1 = jnp.mean(dyh, axis=-1, keepdims=True)
        m2 = jnp.mean(dyh * xhv, axis=-1, keepdims=True)
        dz = r_ref[...] * (dyh - m1 - xhv * m2)
        dz_ref[...] = dz
        gb_ref[0:1, :] += jnp.sum(dxv * xhv, axis=0, keepdims=True)
        gb_ref[1:2, :] += jnp.sum(dxv, axis=0, keepdims=True)
        o_ref[...] = _dot(dz, w_ref[...], NT)

    row = lambda i: (i, 0)
    fix = lambda i: (0, 0)
    body, in_specs, args = _after(
        dep, body, [pl.BlockSpec((tm, D), row), pl.BlockSpec((tm, D), row), pl.BlockSpec((tm, 1), row),
                    pl.BlockSpec((1, D), fix), _resident(w.shape)], [dx, xh, r, g.reshape(1, D), w])
    est = _nbytes(w.shape, w.dtype) + 14 * _nbytes((tm, D), F32) + 4 * _nbytes((tm, N), F32)
    return pl.pallas_call(
        body, name=name, grid=(R // tm,), in_specs=in_specs,
        out_specs=[pl.BlockSpec((tm, D), row), pl.BlockSpec((SUBLANES, D), fix), pl.BlockSpec((tm, N), row)],
        out_shape=[jax.ShapeDtypeStruct((R, D), F32), jax.ShapeDtypeStruct((SUBLANES, D), F32),
                   jax.ShapeDtypeStruct((R, N), F32)],
        compiler_params=_cp(("arbitrary",), est + 4 * 2 ** 20),
    )(*args)


def _mm_ffn_fwd(x1, w, taps, bias, *, tm, name):
    R, K = x1.shape
    F2 = w.shape[1]
    F = F2 // 2

    def body(x_ref, w_ref, t_ref, b_ref, up_ref, a_ref, carry_ref):
        i = pl.program_id(0)

        @pl.when(i == 0)
        def _():
            carry_ref[...] = jnp.zeros(carry_ref.shape, F32)

        acc = _dot(x_ref[...], w_ref[...], NN)
        acc = jnp.where(_row_ids(i * tm, tm) >= PADR, acc, 0.0)
        up_ref[...] = acc
        ue = jnp.concatenate([carry_ref[...], acc], axis=0)
        carry_ref[...] = acc[tm - FH:tm]
        u = (t_ref[0:1, :] * pltpu.roll(ue, 2, 0)[FH:] + t_ref[1:2, :] * pltpu.roll(ue, 1, 0)[FH:]
             + t_ref[2:3, :] * acc + b_ref[...])
        gate = u[:, :F]
        val = u[:, F:]
        a_ref[...] = (gate * jax.nn.sigmoid(gate) * val).astype(a_ref.dtype)

    row = lambda i: (i, 0)
    est = _nbytes(w.shape, w.dtype) + 2 * _nbytes((tm, K), x1.dtype) + 8 * _nbytes((tm + FH, F2), F32)
    return pl.pallas_call(
        body, name=name, grid=(R // tm,),
        in_specs=[pl.BlockSpec((tm, K), row), _resident(w.shape), _resident((SUBLANES, F2)), _resident((1, F2))],
        out_specs=[pl.BlockSpec((tm, F2), row), pl.BlockSpec((tm, F), row)],
        out_shape=[jax.ShapeDtypeStruct((R, F2), F32), jax.ShapeDtypeStruct((R, F), MX)],
        scratch_shapes=[pltpu.VMEM((FH, F2), F32)],
        compiler_params=_cp(("arbitrary",), est + 4 * 2 ** 20),
    )(x1, w, taps, bias)


def _loss(x2, tgt, *, name):
    R = x2.shape[0]
    tb = ROW0

    def body(x_ref, t_ref, dy_ref, acc_ref):
        i = pl.program_id(0)

        @pl.when(i == 0)
        def _():
            acc_ref[...] = jnp.zeros(acc_ref.shape, F32)
            dy_ref[...] = jnp.zeros(dy_ref.shape, F32)

        @pl.when(i > 0)
        def _():
            err = x_ref[...] - t_ref[...]
            dy_ref[...] = err / D
            acc_ref[0:1, :] += jnp.sum(err * err, axis=0, keepdims=True)

    return pl.pallas_call(
        body, name=name, grid=(R // tb,),
        in_specs=[pl.BlockSpec((tb, D), lambda i: (i, 0)),
                  pl.BlockSpec((tb, D), lambda i: (jnp.maximum(i - 1, 0), 0))],
        out_specs=[pl.BlockSpec((tb, D), lambda i: (i, 0)), pl.BlockSpec((SUBLANES, D), lambda i: (0, 0))],
        out_shape=[jax.ShapeDtypeStruct((R, D), F32), jax.ShapeDtypeStruct((SUBLANES, D), F32)],
        compiler_params=_cp(("arbitrary",), 8 * _nbytes((tb, D), F32)),
    )(x2, tgt)


def _row_ids(start, n):
    return start + lax.broadcasted_iota(jnp.int32, (n, 1), 0)


def _ffn_fwd(up, taps, bias, *, tb, name):
    R, F2 = up.shape
    F = F2 // 2
    nh = tb // FH

    def body(u_ref, up_ref, w_ref, b_ref, a_ref):
        ue = jnp.concatenate([up_ref[...], u_ref[...]], axis=0)
        u = (w_ref[0:1, :] * pltpu.roll(ue, 2, 0)[FH:] + w_ref[1:2, :] * pltpu.roll(ue, 1, 0)[FH:]
             + w_ref[2:3, :] * u_ref[...] + b_ref[...])
        gate = u[:, :F]
        val = u[:, F:]
        a_ref[...] = (gate * jax.nn.sigmoid(gate) * val).astype(a_ref.dtype)

    est = 2 * _nbytes((tb, F2), F32) * 4
    return pl.pallas_call(
        body, name=name, grid=(R // tb,),
        in_specs=[pl.BlockSpec((tb, F2), lambda i: (i, 0)),
                  pl.BlockSpec((FH, F2), lambda i: (jnp.maximum(i * nh - 1, 0), 0)),
                  pl.BlockSpec((SUBLANES, F2), lambda i: (0, 0)), pl.BlockSpec((1, F2), lambda i: (0, 0))],
        out_specs=pl.BlockSpec((tb, F), lambda i: (i, 0)),
        out_shape=jax.ShapeDtypeStruct((R, F), MX),
        compiler_params=_cp(("parallel",), est),
    )(up, up, taps, bias)


def _ffn_bwd(da, up, taps, bias, *, tb, dep=None, name):
    R, F2 = up.shape
    F = F2 // 2
    nh = tb // FH
    nb = R // tb
    last = R // FH - 1
    m = tb + FH

    def body(da_ref, dan_ref, u_ref, up_ref, un_ref, w_ref, b_ref, du_ref, acc_ref):
        i = pl.program_id(0)

        @pl.when(i == 0)
        def _():
            acc_ref[...] = jnp.zeros(acc_ref.shape, F32)

        inside = (i < nb - 1).astype(F32)
        n = FH + m
        ue = jnp.concatenate([up_ref[...], u_ref[...], un_ref[...] * inside], axis=0)
        w0, w1, w2 = w_ref[0:1, :], w_ref[1:2, :], w_ref[2:3, :]
        u = (w0 * pltpu.roll(ue, 2, 0) + w1 * pltpu.roll(ue, 1, 0) + w2 * ue + b_ref[...])[FH:n]
        dae = jnp.concatenate([da_ref[...], dan_ref[...] * inside], axis=0)
        gate = u[:, :F]
        val = u[:, F:]
        sg = jax.nn.sigmoid(gate)
        gs = gate * sg
        du = jnp.concatenate([dae * val * (sg * (1.0 + gate - gs)), dae * gs], axis=1)
        du0 = du[0:tb]
        du1 = pltpu.roll(du, m - 1, 0)[0:tb]
        du2 = pltpu.roll(du, m - 2, 0)[0:tb]
        du_ref[...] = (w2 * du0 + w1 * du1 + w0 * du2).astype(du_ref.dtype)

        @pl.when(i * tb < PADR)
        def _():
            row = _row_ids(i * tb, tb)
            du_ref[...] = jnp.where(row >= PADR, du_ref[...], jnp.zeros((), du_ref.dtype))

        x = u_ref[...]
        acc_ref[0:1, :] += jnp.sum(du2 * x, axis=0, keepdims=True)
        acc_ref[1:2, :] += jnp.sum(du1 * x, axis=0, keepdims=True)
        acc_ref[2:3, :] += jnp.sum(du0 * x, axis=0, keepdims=True)
        acc_ref[3:4, :] += jnp.sum(du0, axis=0, keepdims=True)

    prev = lambda i: (jnp.maximum(i * nh - 1, 0), 0)
    nxt = lambda i: (jnp.minimum(i * nh + nh, last), 0)
    cur = lambda i: (i, 0)
    fix = lambda i: (0, 0)
    est = 12 * _nbytes((tb + 2 * FH, F2), F32)
    body, in_specs, args = _after(
        dep, body, [pl.BlockSpec((tb, F), cur), pl.BlockSpec((FH, F), nxt),
                    pl.BlockSpec((tb, F2), cur), pl.BlockSpec((FH, F2), prev), pl.BlockSpec((FH, F2), nxt),
                    pl.BlockSpec((SUBLANES, F2), fix), pl.BlockSpec((1, F2), fix)], [da, da, up, up, up, taps, bias])
    return pl.pallas_call(
        body, name=name, grid=(R // tb,), in_specs=in_specs,
        out_specs=[pl.BlockSpec((tb, F2), cur), pl.BlockSpec((SUBLANES, F2), fix)],
        out_shape=[jax.ShapeDtypeStruct((R, F2), MX), jax.ShapeDtypeStruct((SUBLANES, F2), F32)],
        compiler_params=_cp(("arbitrary",), est),
    )(*args)


def _hg_consts():
    t = np.arange(CH)[:, None]
    j = np.arange(CH)[None, :]
    low = (j <= t).astype(np.float32)
    blocks = [low, (j > t).astype(np.float32)]
    for m in LEVELS:
        ref = (t // (2 * m)) * 2 * m + m - 1
        blocks.append(low - (j <= ref).astype(np.float32))
    mat = np.concatenate(blocks, axis=0)
    suf = (j >= t).astype(np.float32)
    return jnp.asarray(mat, BF), jnp.asarray(suf, BF)


def _split_dot(mat, x):
    hi = x.astype(BF)
    lo = (x - hi.astype(F32)).astype(BF)
    return (lax.dot_general(mat, hi, NN, preferred_element_type=F32)
            + lax.dot_general(mat, lo, NN, preferred_element_type=F32))


def _lane_select(a2, a4, a8, a16):
    lane = lax.broadcasted_iota(jnp.int32, (1, PW), 1)
    return jnp.where(lane < 64, a2, jnp.where(lane < 128, a4, jnp.where(lane < 192, a8, a16)))


def _pool_count(row):
    win = _lane_select(2.0, 4.0, 8.0, 16.0)
    t1 = jnp.maximum((row - PADR + 1).astype(F32), 1.0)
    return jnp.minimum(t1, win)


def _gates(fz, lb, valid):
    sig = jax.nn.sigmoid(fz)
    f = lb + (1.0 - lb) * sig
    lf = jnp.where(valid, jnp.log(jnp.maximum(f, F_FLOOR)), 0.0)
    kk = jnp.where(valid, (1.0 - lb) * (1.0 - sig), 0.0)
    return sig, f, lf, kk


def _level_masks():
    tt = lax.broadcasted_iota(jnp.int32, (CH, CH), 0)
    ss = lax.broadcasted_iota(jnp.int32, (CH, CH), 1)
    xr = tt ^ ss
    low = tt > ss
    return tt, ss, [(xr >= m) & (xr < 2 * m) & low for m in LEVELS]


def _intra(q, kk, ex, sl, tt, ss, masks):
    a = jnp.where(tt == ss, jnp.sum(q * kk, axis=-1, keepdims=True), 0.0)
    parts = []
    for l in range(len(LEVELS)):
        e = jnp.exp(-jnp.abs(ex[(2 + l) * CH:(3 + l) * CH, sl]))
        ql = (q * e).astype(MX)
        kl = (kk * e).astype(MX)
        a = a + jnp.where(masks[l], _dot(ql, kl, NT), 0.0)
        parts.append((e, ql, kl))
    return a, parts


def _mixer_fwd(h, taps, wbd, ps, gn, lb, mat, *, name):
    R = h.shape[0]
    nb = R // CH
    q4 = CH // HALO

    def body(h_ref, hp_ref, w_ref, wbd_ref, ps_ref, gn_ref, lb_ref, mat_ref, y_ref, o_ref, sp_ref, st_ref):
        i = pl.program_id(0)

        @pl.when(i == 0)
        def _():
            st_ref[...] = jnp.zeros(st_ref.shape, F32)

        row = _row_ids(i * CH, CH)
        valid = row >= PADR
        rowe = _row_ids(i * CH - HALO, CH + HALO)
        valide = rowe >= PADR
        se = jnp.concatenate([hp_ref[:, 256:512] * hp_ref[:, 512:768], h_ref[:, 256:512] * h_ref[:, 512:768]], axis=0)
        se = jnp.where(valide, se, 0.0)
        conv = (w_ref[0:1, :] * pltpu.roll(se, 2, 0) + w_ref[1:2, :] * pltpu.roll(se, 1, 0)
                + w_ref[2:3, :] * se)[HALO:]
        y_ref[:, 0:CW] = (h_ref[:, 0:256] * conv).astype(y_ref.dtype)
        ve = jnp.where(valide, jnp.concatenate([hp_ref[:, 2816:3072], h_ref[:, 2816:3072]], axis=0), 0.0)
        s2 = ve + pltpu.roll(ve, 1, 0)
        s4 = s2 + pltpu.roll(s2, 2, 0)
        s8 = s4 + pltpu.roll(s4, 4, 0)
        s16 = s8 + pltpu.roll(s8, 8, 0)
        dp = (_lane_select(s2, s4, s8, s16) / _pool_count(rowe) - ve)[HALO:]
        y_ref[:, CW + HW:D] = (_dot(dp, wbd_ref[...], NN) * ps_ref[...]).astype(y_ref.dtype)
        lbv = lb_ref[...]
        _, _, lf, kk = _gates(h_ref[:, 1280:1792], lbv, valid)
        q = h_ref[:, 768:1280] * SCALE
        ii = h_ref[:, 1792:2304]
        gz = h_ref[:, 2304:2816]
        ex = _split_dot(mat_ref[...], lf)
        eg = jnp.exp(ex[0:CH])
        q0 = q * eg
        kr = kk * jnp.exp(ex[CH:2 * CH])
        tt, ss, masks = _level_masks()
        for hd in range(NH):
            sl = slice(HD * hd, HD * (hd + 1))
            st = st_ref[hd]
            sp_ref[hd] = st
            a, _ = _intra(q[:, sl], kk[:, sl], ex, sl, tt, ss, masks)
            o = _dot(q0[:, sl], st, NT) + _dot(a, ii[:, sl], NN)
            st_ref[hd] = st * eg[CH - 1:CH, sl] + _dot(ii[:, sl], kr[:, sl], TN)
            o_ref[:, sl] = o
            r = lax.rsqrt(jnp.mean(o * o, axis=-1, keepdims=True) + RMS_EPS)
            g_ = gz[:, sl]
            y_ref[:, CW + HD * hd:CW + HD * (hd + 1)] = (
                o * r * gn_ref[...] * (g_ * jax.nn.sigmoid(g_))).astype(y_ref.dtype)

    fix = lambda i: (0, 0)
    return pl.pallas_call(
        body, name=name, grid=(nb,),
        in_specs=[pl.BlockSpec((CH, DIN), lambda i: (i, 0)),
                  pl.BlockSpec((HALO, DIN), lambda i: (jnp.maximum(i * q4 - 1, 0), 0)),
                  pl.BlockSpec((SUBLANES, CW), fix), pl.BlockSpec((PW, PW), fix), pl.BlockSpec((1, PW), fix),
                  pl.BlockSpec((1, HD), fix), pl.BlockSpec((1, HW), fix), pl.BlockSpec(mat.shape, fix)],
        out_specs=[pl.BlockSpec((CH, D), lambda i: (i, 0)), pl.BlockSpec((CH, HW), lambda i: (i, 0)),
                   pl.BlockSpec((None, NH, HD, HD), lambda i: (i, 0, 0, 0))],
        out_shape=[jax.ShapeDtypeStruct((R, D), MX), jax.ShapeDtypeStruct((R, HW), F32),
                   jax.ShapeDtypeStruct((nb, NH, HD, HD), F32)],
        scratch_shapes=[pltpu.VMEM((NH, HD, HD), F32)],
        compiler_params=_cp(("arbitrary",), 32 * 2 ** 20),
    )(h, h, taps, wbd, ps, gn, lb, mat)


def _mixer_bwd(h, dy, o, sp, taps, wbd, ps, gn, lb, mat, suf, *, dep=None, name):
    R = h.shape[0]
    nb = R // CH
    q4 = CH // HALO
    lasth = R // HALO - 1

    def body(h_ref, hp_ref, hn_ref, dy_ref, dyn_ref, o_ref, sp_ref, w_ref, wbd_ref, ps_ref, gn_ref, lb_ref,
             mat_ref, suf_ref, dh_ref, dw_ref, dsc_ref, dgn_ref, dlb_ref, dwbd_ref, dst_ref):
        i = pl.program_id(0)
        b = nb - 1 - i

        @pl.when(i == 0)
        def _():
            dst_ref[...] = jnp.zeros(dst_ref.shape, F32)
            dw_ref[...] = jnp.zeros(dw_ref.shape, F32)
            dsc_ref[...] = jnp.zeros(dsc_ref.shape, F32)
            dgn_ref[...] = jnp.zeros(dgn_ref.shape, F32)
            dlb_ref[...] = jnp.zeros(dlb_ref.shape, F32)
            dwbd_ref[...] = jnp.zeros(dwbd_ref.shape, F32)

        ne = CH + 2 * HALO
        nc = CH + HALO
        row = _row_ids(b * CH, CH)
        valid = row >= PADR
        rowe = _row_ids(b * CH - HALO, ne)
        valide = (rowe >= PADR) & (rowe < R)
        rown = rowe[HALO:]
        validn = rown < R

        def cat3(lo, hi):
            return jnp.concatenate([hp_ref[:, lo:hi], h_ref[:, lo:hi], hn_ref[:, lo:hi]], axis=0)

        def catd(lo, hi):
            return jnp.where(validn, jnp.concatenate([dy_ref[:, lo:hi], dyn_ref[:, lo:hi]], axis=0), 0.0)

        w0, w1, w2 = w_ref[0:1, :], w_ref[1:2, :], w_ref[2:3, :]
        cce = cat3(256, 512)
        cve = cat3(512, 768)
        se = jnp.where(valide, cce * cve, 0.0)
        sm2 = pltpu.roll(se, 2, 0)
        sm1 = pltpu.roll(se, 1, 0)
        conv = (w0 * sm2 + w1 * sm1 + w2 * se)[HALO:HALO + CH]
        cbn = jnp.concatenate([h_ref[:, 0:256], hn_ref[:, 0:256]], axis=0)
        dconv = catd(0, CW) * cbn
        ds = w2 * dconv + w1 * pltpu.roll(dconv, nc - 1, 0) + w0 * pltpu.roll(dconv, nc - 2, 0)
        ds = jnp.where(valid, ds[0:CH], 0.0)
        dcv = dconv[0:CH]
        dw_ref[0:1, :] += jnp.sum(dcv * sm2[HALO:HALO + CH], axis=0, keepdims=True)
        dw_ref[1:2, :] += jnp.sum(dcv * sm1[HALO:HALO + CH], axis=0, keepdims=True)
        dw_ref[2:3, :] += jnp.sum(dcv * se[HALO:HALO + CH], axis=0, keepdims=True)
        dh_ref[:, 0:256] = jnp.where(valid, dy_ref[:, 0:CW] * conv, 0.0).astype(dh_ref.dtype)
        dh_ref[:, 256:512] = (ds * h_ref[:, 512:768]).astype(dh_ref.dtype)
        dh_ref[:, 512:768] = (ds * h_ref[:, 256:512]).astype(dh_ref.dtype)

        ve = jnp.where(valide, cat3(2816, 3072), 0.0)
        s2 = ve + pltpu.roll(ve, 1, 0)
        s4 = s2 + pltpu.roll(s2, 2, 0)
        s8 = s4 + pltpu.roll(s4, 4, 0)
        s16 = s8 + pltpu.roll(s8, 8, 0)
        cnte = _pool_count(rowe)
        dp = (_lane_select(s2, s4, s8, s16) / cnte - ve)[HALO:HALO + CH]
        dyp = catd(CW + HW, D)
        pre = _dot(dp, wbd_ref[...], NN)
        dsc_ref[0:1, :] += jnp.sum(dyp[0:CH] * pre, axis=0, keepdims=True)
        dyps = dyp * ps_ref[...]
        dd = _dot(dyps, wbd_ref[...], NT)
        dwbd_ref[...] += _dot(dp, dyps[0:CH], TN)
        e = dd / cnte[HALO:]
        t2 = e + pltpu.roll(e, nc - 1, 0)
        t4 = t2 + pltpu.roll(t2, nc - 2, 0)
        t8 = t4 + pltpu.roll(t4, nc - 4, 0)
        t16 = t8 + pltpu.roll(t8, nc - 8, 0)
        dv = (_lane_select(t2, t4, t8, t16) - dd)[0:CH]
        dh_ref[:, 2816:3072] = jnp.where(valid, dv, 0.0).astype(dh_ref.dtype)

        lbv = lb_ref[...]
        sig, f, lf, kk = _gates(h_ref[:, 1280:1792], lbv, valid)
        q = h_ref[:, 768:1280] * SCALE
        ii = h_ref[:, 1792:2304]
        gz = h_ref[:, 2304:2816]
        ex = _split_dot(mat_ref[...], lf)
        eg = jnp.exp(ex[0:CH])
        egr = jnp.exp(ex[CH:2 * CH])
        q0 = q * eg
        kr = kk * egr
        tt, ss, masks = _level_masks()
        gnv = gn_ref[...]
        dqs, dks, dis, dgzs, tails, dbs = [], [], [], [], [], []
        for hd in range(NH):
            sl = slice(HD * hd, HD * (hd + 1))
            ov = o_ref[:, sl]
            r = lax.rsqrt(jnp.mean(ov * ov, axis=-1, keepdims=True) + RMS_EPS)
            oh = ov * r
            g_ = gz[:, sl]
            sg = jax.nn.sigmoid(g_)
            dyv = dy_ref[:, CW + HD * hd:CW + HD * (hd + 1)]
            don = dyv * (g_ * sg)
            dgzs.append(dyv * (oh * gnv) * (sg * (1.0 + g_ * (1.0 - sg))))
            dgn_ref[0:1, :] += jnp.sum(don * oh, axis=0, keepdims=True)
            doh = don * gnv
            do = r * (doh - oh * jnp.mean(doh * oh, axis=-1, keepdims=True))
            st = sp_ref[hd]
            dst = dst_ref[hd]
            qh, kh, ih = q[:, sl], kk[:, sl], ii[:, sl]
            a, parts = _intra(qh, kh, ex, sl, tt, ss, masks)
            da = jnp.where(tt >= ss, _dot(do, ih, NT), 0.0)
            dis.append(_dot(a, do, TN) + _dot(kr[:, sl], dst, NT))
            q0h = q0[:, sl].astype(MX)
            krh = kr[:, sl].astype(MX)
            dq0 = _dot(do, st, NN)
            dkr = _dot(ih, dst, NN)
            dq = dq0 * eg[:, sl]
            dk = dkr * egr[:, sl]
            kdk = krh.astype(F32) * dkr
            db = q0h.astype(F32) * dq0 - kdk
            tails.append(jnp.sum(kdk, axis=0, keepdims=True)
                         + eg[CH - 1:CH, sl] * jnp.sum(dst * st, axis=0, keepdims=True))
            dga = jnp.sum(jnp.where(tt == ss, da, 0.0), axis=-1, keepdims=True)
            dq = dq + dga * kh
            dk = dk + dga * qh
            for l in range(len(LEVELS)):
                e_l, ql, kl = parts[l]
                dpl = jnp.where(masks[l], da, 0.0).astype(MX)
                dql = _dot(dpl, kl, NN)
                dkl = _dot(dpl, ql, TN)
                dq = dq + dql * e_l
                dk = dk + dkl * e_l
                db = db + (ql.astype(F32) * dql - kl.astype(F32) * dkl)
            dst_ref[hd] = dst * eg[CH - 1:CH, sl] + _dot(do, q0h, TN)
            dqs.append(dq)
            dks.append(dk)
            dbs.append(db)
        dq = jnp.concatenate(dqs, axis=1)
        dk = jnp.concatenate(dks, axis=1)
        db = jnp.concatenate(dbs, axis=1)
        dlf = _split_dot(suf_ref[...], db) + jnp.concatenate(tails, axis=1)
        t = jnp.where(valid, dlf * jnp.where(f > F_FLOOR, 1.0 / f, 0.0) - dk, 0.0)
        dlb_ref[0:1, :] += jnp.sum(t * (1.0 - sig), axis=0, keepdims=True)
        dh_ref[:, 768:1280] = jnp.where(valid, dq * SCALE, 0.0).astype(dh_ref.dtype)
        dh_ref[:, 1280:1792] = (t * (1.0 - lbv) * (sig * (1.0 - sig))).astype(dh_ref.dtype)
        dh_ref[:, 1792:2304] = jnp.where(valid, jnp.concatenate(dis, axis=1), 0.0).astype(dh_ref.dtype)
        dh_ref[:, 2304:2816] = jnp.where(valid, jnp.concatenate(dgzs, axis=1), 0.0).astype(dh_ref.dtype)

    cur = lambda i: (nb - 1 - i, 0)
    prev = lambda i: (jnp.maximum((nb - 1 - i) * q4 - 1, 0), 0)
    nxt = lambda i: (jnp.minimum((nb - 1 - i) * q4 + q4, lasth), 0)
    fix = lambda i: (0, 0)
    body, in_specs, args = _after(
        dep, body,
        [pl.BlockSpec((CH, DIN), cur), pl.BlockSpec((HALO, DIN), prev), pl.BlockSpec((HALO, DIN), nxt),
         pl.BlockSpec((CH, D), cur), pl.BlockSpec((HALO, D), nxt), pl.BlockSpec((CH, HW), cur),
         pl.BlockSpec((None, NH, HD, HD), lambda i: (nb - 1 - i, 0, 0, 0)),
         pl.BlockSpec((SUBLANES, CW), fix), pl.BlockSpec((PW, PW), fix), pl.BlockSpec((1, PW), fix),
         pl.BlockSpec((1, HD), fix), pl.BlockSpec((1, HW), fix), pl.BlockSpec(mat.shape, fix),
         pl.BlockSpec(suf.shape, fix)],
        [h, h, h, dy, dy, o, sp, taps, wbd, ps, gn, lb, mat, suf])
    return pl.pallas_call(
        body, name=name, grid=(nb,), in_specs=in_specs,
        out_specs=[pl.BlockSpec((CH, DIN), cur), pl.BlockSpec((SUBLANES, CW), fix), pl.BlockSpec((SUBLANES, PW), fix),
                   pl.BlockSpec((SUBLANES, HD), fix), pl.BlockSpec((SUBLANES, HW), fix), pl.BlockSpec((PW, PW), fix)],
        out_shape=[jax.ShapeDtypeStruct((R, DIN), MX), jax.ShapeDtypeStruct((SUBLANES, CW), F32),
                   jax.ShapeDtypeStruct((SUBLANES, PW), F32), jax.ShapeDtypeStruct((SUBLANES, HD), F32),
                   jax.ShapeDtypeStruct((SUBLANES, HW), F32), jax.ShapeDtypeStruct((PW, PW), F32)],
        scratch_shapes=[pltpu.VMEM((NH, HD, HD), F32)],
        compiler_params=_cp(("arbitrary",), 40 * 2 ** 20),
    )(*args)


def _sum_slots(recv, *, name):
    S, L, rows, cols = recv.shape
    tr = _row_tile(rows, S * cols * 4, 6 * 2 ** 20)

    def body(r_ref, o_ref):
        acc = r_ref[0]
        for s in range(1, S):
            acc = acc + r_ref[s]
        o_ref[...] = acc

    return pl.pallas_call(
        body, name=name, grid=(L, rows // tr),
        in_specs=[pl.BlockSpec((S, None, tr, cols), lambda l, i: (0, l, i, 0))],
        out_specs=pl.BlockSpec((None, tr, cols), lambda l, i: (l, i, 0)),
        out_shape=jax.ShapeDtypeStruct((L, rows, cols), F32),
        compiler_params=_cp(("parallel", "parallel"), 4 * S * tr * cols * 4),
    )(recv)


def _sum_own(recv, own, chip, *, name):
    S, rows, cols = recv.shape
    tr = _row_tile(rows, S * cols * 4, 6 * 2 ** 20)

    def body(me_ref, r_ref, o_ref, out_ref):
        me = me_ref[0]
        acc = None
        for s in range(S):
            t = jnp.where(me == s, o_ref[...], r_ref[s])
            acc = t if acc is None else acc + t
        out_ref[...] = acc

    grid_spec = pltpu.PrefetchScalarGridSpec(
        num_scalar_prefetch=1, grid=(rows // tr,),
        in_specs=[pl.BlockSpec((S, tr, cols), lambda i, me: (0, i, 0)),
                  pl.BlockSpec((None, tr, cols), lambda i, me: (me[0], i, 0))],
        out_specs=pl.BlockSpec((tr, cols), lambda i, me: (i, 0)))
    return pl.pallas_call(
        body, name=name, grid_spec=grid_spec, out_shape=jax.ShapeDtypeStruct((rows, cols), F32),
        compiler_params=_cp(("parallel",), 5 * S * tr * cols * 4),
    )(chip.reshape(1).astype(jnp.int32), recv, own)


def _adamw(w, m, v, ga, gb, *, layer, prev, name):
    L, rows, cols = w.shape
    tr = _row_tile(rows, cols * 4, 2 ** 20)
    two = gb is not None
    nin = 5 if two else 4

    def body(*refs):
        w_ref, m_ref, v_ref, a_ref = refs[:4]
        g_ref, d_ref, m2_ref, v2_ref = refs[-4:]
        g = a_ref[...] + refs[4][...] if two else a_ref[...]
        m2 = ADAM_B1 * m_ref[...] + (1.0 - ADAM_B1) * g
        v2 = ADAM_B2 * v_ref[...] + (1.0 - ADAM_B2) * (g * g)
        m_hat = m2 / (1.0 - ADAM_B1 ** ADAM_STEP)
        v_hat = v2 / (1.0 - ADAM_B2 ** ADAM_STEP)
        g_ref[...] = g
        d_ref[...] = -ADAM_LR * (m_hat / (jnp.sqrt(v_hat) + ADAM_EPS) + ADAM_WD * w_ref[...])
        m2_ref[...] = m2
        v2_ref[...] = v2

    spec = pl.BlockSpec((None, tr, cols), lambda i: (layer, i, 0))
    gspec = pl.BlockSpec((tr, cols), lambda i: (i, 0))
    args = [w, m, v, ga] + ([gb] if two else [])
    in_specs = [spec] * 3 + [gspec] * (nin - 3)
    aliases = {}
    if prev is not None:
        args += list(prev)
        in_specs += [ANY] * 4
        aliases = {nin + j: j for j in range(4)}
    sd = jax.ShapeDtypeStruct((L, rows, cols), F32)
    return pl.pallas_call(
        body, name=name, grid=(rows // tr,), in_specs=in_specs, out_specs=[spec] * 4,
        out_shape=[sd] * 4, input_output_aliases=aliases,
        compiler_params=_cp(("parallel",), 24 * tr * cols * 4),
    )(*args)


def _exchange(arrays, *, flips, n_slots, slot_of, scatter, self_copy, name):
    n = len(arrays)
    nf = len(flips)
    out_shapes = [jax.ShapeDtypeStruct(a.shape if scatter else (n_slots,) + a.shape, a.dtype) for a in arrays]

    def body(*refs):
        ins, outs = refs[:n], refs[n:2 * n]
        send_sems, recv_sems, loc_sems = refs[2 * n:]
        x, y, c = lax.axis_index("x"), lax.axis_index("y"), lax.axis_index("c")
        me = slot_of(x, y, c)
        peers = [(1 - x if fx else x, 1 - y if fy else y, 1 - c if fc else c) for fx, fy, fc in flips]
        local, remote = [], []
        for a in range(n):
            if self_copy:
                lc = pltpu.make_async_copy(ins[a].at[me] if scatter else ins[a], outs[a].at[me], loc_sems.at[a])
                lc.start()
                local.append(lc)
            for k, p in enumerate(peers):
                src = ins[a].at[slot_of(*p)] if scatter else ins[a]
                cp = pltpu.make_async_remote_copy(
                    src_ref=src, dst_ref=outs[a].at[me], send_sem=send_sems.at[a, k], recv_sem=recv_sems.at[a, k],
                    device_id=p, device_id_type=MESH)
                cp.start()
                remote.append(cp)
        for a in range(n):
            for k, p in enumerate(peers):
                src = ins[a].at[slot_of(*p)] if scatter else ins[a]
                pltpu.make_async_remote_copy(
                    src_ref=src, dst_ref=outs[a].at[slot_of(*p)], send_sem=send_sems.at[a, k],
                    recv_sem=recv_sems.at[a, k], device_id=p, device_id_type=MESH).wait_recv()
        for cp in remote:
            cp.wait_send()
        for lc in local:
            lc.wait()

    return pl.pallas_call(
        body, name=name, in_specs=[ANY] * n, out_specs=[ANY] * n, out_shape=out_shapes,
        scratch_shapes=[pltpu.SemaphoreType.DMA((n, nf)), pltpu.SemaphoreType.DMA((n, nf)),
                        pltpu.SemaphoreType.DMA((n,))],
        compiler_params=pltpu.CompilerParams(has_side_effects=True),
    )(*arrays)


CHIP_FLIPS = [(1, 0, 0), (0, 1, 0), (1, 1, 0)]
ALL_FLIPS = [(fx, fy, fc) for fx in (0, 1) for fy in (0, 1) for fc in (0, 1) if fx or fy or fc]


def _chip_slot(x, y, c):
    return 2 * x + y


def _dev_slot(x, y, c):
    return 4 * x + 2 * y + c


def _zero_slot(x, y, c):
    return 0


HBM_SPEC = pl.BlockSpec(memory_space=pltpu.HBM)
SEM_SPEC = pl.BlockSpec(memory_space=pltpu.SEMAPHORE)
N_PEER_CHIPS = 3


def _peer_chips(x, y, c):
    return [(1 - x, y, c), (x, 1 - y, c), (1 - x, 1 - y, c)]


def _send_start(srcs, *, scatter, dep=None, name):
    n = len(srcs)
    nc = n * N_PEER_CHIPS
    srcs = [pltpu.with_memory_space_constraint(s, pltpu.HBM) for s in srcs]
    land_shapes = [s.shape if scatter else (4,) + s.shape for s in srcs]
    lands = [pltpu.with_memory_space_constraint(lax.empty(sh, s.dtype), pltpu.HBM) for sh, s in zip(land_shapes, srcs)]

    deps = [] if dep is None else [dep]
    nd = len(deps)

    def body(*refs):
        ins, lnd = refs[:n], refs[n:2 * n]
        send_sems, recv_sems = refs[2 * n + nd:2 * n + nd + nc], refs[2 * n + nd + nc:2 * n + nd + 2 * nc]
        token = refs[-1]
        x, y, c = lax.axis_index("x"), lax.axis_index("y"), lax.axis_index("c")
        me = 2 * x + y
        for a in range(n):
            for k, p in enumerate(_peer_chips(x, y, c)):
                src = ins[a].at[2 * p[0] + p[1]] if scatter else ins[a]
                j = a * N_PEER_CHIPS + k
                pltpu.make_async_remote_copy(
                    src_ref=src, dst_ref=lnd[a].at[me], send_sem=send_sems[j], recv_sem=recv_sems[j],
                    device_id=p, device_id_type=MESH).start()
        token[...] = jnp.zeros(token.shape, token.dtype)

    sem = pltpu.SemaphoreType.DMA(())
    outs = pl.pallas_call(
        body, name=name,
        out_shape=(*[sem] * (2 * nc), *[pltpu.HBM(s.shape, s.dtype) for s in srcs],
                   *[pltpu.HBM(sh, s.dtype) for sh, s in zip(land_shapes, srcs)],
                   jax.ShapeDtypeStruct((SUBLANES, LANES), F32)),
        in_specs=[HBM_SPEC] * (2 * n) + [ANY] * nd,
        out_specs=(*[SEM_SPEC] * (2 * nc), *[HBM_SPEC] * (2 * n), pl.BlockSpec(memory_space=pltpu.VMEM)),
        input_output_aliases={i: 2 * nc + i for i in range(2 * n)},
        compiler_params=pltpu.CompilerParams(has_side_effects=pltpu.SideEffectType.DATAFLOW_SIDE_EFFECTING),
    )(*srcs, *lands, *deps)
    return dict(sems=list(outs[:2 * nc]), srcs=list(outs[2 * nc:2 * nc + n]),
                lands=list(outs[2 * nc + n:2 * nc + 2 * n]), token=outs[-1])


def _send_wait(h, *, scatter, after, name):
    n = len(h["srcs"])
    nc = n * N_PEER_CHIPS

    def body(*refs):
        ins, lnd = refs[:n], refs[n:2 * n]
        send_sems, recv_sems = refs[2 * n:2 * n + nc], refs[2 * n + nc:2 * n + 2 * nc]
        x, y, c = lax.axis_index("x"), lax.axis_index("y"), lax.axis_index("c")
        for a in range(n):
            for k, p in enumerate(_peer_chips(x, y, c)):
                slot = 2 * p[0] + p[1]
                j = a * N_PEER_CHIPS + k
                cp = pltpu.make_async_remote_copy(
                    src_ref=ins[a].at[slot] if scatter else ins[a], dst_ref=lnd[a].at[slot],
                    send_sem=send_sems[j], recv_sem=recv_sems[j], device_id=p, device_id_type=MESH)
                cp.wait_send()
                cp.wait_recv()

    thru = h["srcs"] + h["lands"]
    outs = pl.pallas_call(
        body, name=name, out_shape=tuple(pltpu.HBM(t.shape, t.dtype) for t in thru),
        in_specs=[HBM_SPEC] * (2 * n) + [SEM_SPEC] * (2 * nc) + [ANY] * len(after),
        out_specs=tuple([HBM_SPEC] * (2 * n)),
        input_output_aliases={i: i for i in range(2 * n)},
        compiler_params=pltpu.CompilerParams(has_side_effects=pltpu.SideEffectType.DATAFLOW_SIDE_EFFECTING),
    )(*thru, *h["sems"], *after)
    return list(outs[:n]), list(outs[n:])


def _assemble(land, own, chip, axis):
    return jnp.concatenate([jnp.where(chip == k, own, land[k]) for k in range(4)], axis=axis)


def _pack(arrs):
    flat = jnp.concatenate([a.reshape(-1).astype(F32) for a in arrs])
    tile = SUBLANES * LANES
    pad = (-flat.shape[0]) % tile
    return jnp.pad(flat, (0, pad)).reshape(-1, LANES)


def _unpack(buf, shapes):
    flat = buf.reshape(-1)
    out, off = [], 0
    for s in shapes:
        n = int(np.prod(s))
        out.append(flat[off:off + n].reshape(s))
        off += n
    return out


def _lower_bounds(hg_lower_bounds):
    p = jax.nn.softmax(hg_lower_bounds.astype(F32), axis=0)
    return jnp.cumsum(p, axis=0) - p[0]


def kernel(x, meta_tokens, hg_lower_bounds, w_in, w_conv, w_pool, pool_scale, hg_norm_g, w_o, ln1_g, ln1_b, w_up, w_ffn_conv, b_ffn_conv, w_down, ln2_g, ln2_b, loss_target, m_meta_tokens, m_hg_lower_bounds, m_w_in, m_w_conv, m_w_pool, m_pool_scale, m_hg_norm_g, m_w_o, m_ln1_g, m_ln1_b, m_w_up, m_w_ffn_conv, m_b_ffn_conv, m_w_down, m_ln2_g, m_ln2_b, v_meta_tokens, v_hg_lower_bounds, v_w_in, v_w_conv, v_w_pool, v_pool_scale, v_hg_norm_g, v_w_o, v_ln1_g, v_ln1_b, v_w_up, v_w_ffn_conv, v_b_ffn_conv, v_w_down, v_ln2_g, v_ln2_b):
    S = x.shape[1]
    R = S + ROW0
    Fq = w_down.shape[1]
    F = 4 * Fq
    F2 = 2 * F
    F2q = w_up.shape[2]
    assert x.shape == (1, S, D) and R % 384 == 0 and S % ROW0 == 0
    chip = 2 * lax.axis_index("x") + lax.axis_index("y")
    tm = 384
    tm_w = max(t for t in range(SUBLANES, 1057, SUBLANES) if R % t == 0)
    tb_ffn = 128

    small_shapes = [(N_META, D // 4), (DEPTH, CW // 4, 3), (DEPTH, F2q, 3)]
    wb_in, wb_o, wb_up, wb_down = (w.astype(MX) for w in (w_in, w_o, w_up, w_down))
    h_a = _send_start([wb_in[0], _pack([meta_tokens, w_conv, w_ffn_conv])], scatter=False, name="gather_a_start")
    (own_in, own_small), (l_in, l_small) = _send_wait(h_a, scatter=False, after=[h_a["token"]], name="gather_a_wait")
    h_b = _send_start([wb_o[0], wb_up[0], wb_down[0]], scatter=False, dep=l_in, name="gather_b_start")
    Win, Wo, Wup, Wdown = {}, {}, {}, {}
    Win[0] = _assemble(l_in, own_in, chip, 1)
    sm = [_unpack(jnp.where(chip == k, own_small, l_small[k]), small_shapes) for k in range(4)]
    meta_full = jnp.concatenate([sm[k][0] for k in range(4)], axis=1)
    wconv_full = jnp.concatenate([sm[k][1] for k in range(4)], axis=1)
    wffn_full = jnp.concatenate([sm[k][2] for k in range(4)], axis=1)
    taps_c = jnp.pad(wconv_full.transpose(0, 2, 1), ((0, 0), (0, SUBLANES - 3), (0, 0)))
    taps_f = jnp.pad(wffn_full.transpose(0, 2, 1), ((0, 0), (0, SUBLANES - 3), (0, 0)))
    wbd = jnp.stack([jax.scipy.linalg.block_diag(*[w_pool[l, g] for g in range(4)]) for l in range(DEPTH)]).astype(MX)
    lbs, lbs_vjp = jax.vjp(_lower_bounds, hg_lower_bounds)
    mat, suf = _hg_consts()

    def fwd_mixer(l, X, dep=None):
        h = _mm(X, Win[l], tm=tm, dep=dep, name=f"mm_in_{l}")
        y, o, sp = _mixer_fwd(h, taps_c[l], wbd[l], pool_scale[l].reshape(1, PW), hg_norm_g[l].reshape(1, HD),
                              lbs[l].reshape(1, HW), mat, name=f"mixer_fwd_{l}")
        return h, y, o, sp

    def fwd_rest(l, X, h, y, o, sp, dep=None):
        x1, xh1, r1 = _mm_ln(y, Wo[l], X, ln1_g[l], ln1_b[l], tm=tm, dep=dep, name=f"mm_o_ln_{l}")
        up, a = _mm_ffn_fwd(x1, Wup[l], taps_f[l], b_ffn_conv[l].reshape(1, F2), tm=tm // 2, name=f"mm_up_ffn_{l}")
        x2, xh2, r2 = _mm_ln(a, Wdown[l], x1, ln2_g[l], ln2_b[l], tm=tm, name=f"mm_down_ln_{l}")
        return (X, h, y, o, sp, x1, xh1, r1, up, a, xh2, r2), x2

    X = jnp.concatenate([jnp.zeros((PADR, D), F32), meta_full, x[0]], axis=0)
    h, y, o, sp = fwd_mixer(0, X, dep=h_b["token"])
    (own_o, own_up, own_down), (l_o, l_up, l_down) = _send_wait(h_b, scatter=False, after=[y], name="gather_b_wait")
    Wo[0], Wup[0], Wdown[0] = (_assemble(l_o, own_o, chip, 0), _assemble(l_up, own_up, chip, 1),
                               _assemble(l_down, own_down, chip, 0))
    h_c = _send_start([wb_in[1], wb_o[1], wb_up[1], wb_down[1]], scatter=False, dep=l_o, name="gather_c_start")
    saved0, X1 = fwd_rest(0, X, h, y, o, sp, dep=h_c["token"])
    own_c, l_c = _send_wait(h_c, scatter=False, after=[X1], name="gather_c_wait")
    Win[1], Wo[1] = _assemble(l_c[0], own_c[0], chip, 1), _assemble(l_c[1], own_c[1], chip, 0)
    Wup[1], Wdown[1] = _assemble(l_c[2], own_c[2], chip, 1), _assemble(l_c[3], own_c[3], chip, 0)
    saved1, X2 = fwd_rest(1, X1, *fwd_mixer(1, X1))
    saved = [saved0, saved1]

    dxo, lacc = _loss(X2, loss_target[0], name="loss")
    loss = lax.psum(0.5 * jnp.sum(lacc[0]) / D, ("x", "y", "c"))

    sc = {}

    def scatter(nm, l, g):
        sc[nm, l] = _send_start([g], scatter=True, name=f"scatter_{nm}_{l}_start")
        return sc[nm, l]["token"]

    tok = None

    small_g = [None] * DEPTH
    for l in reversed(range(DEPTH)):
        X, h, y, o, sp, x1, xh1, r1, up, a, xh2, r2 = saved[l]
        dz2, gb2, da = _ln_bwd_mm(dxo, xh2, r2, ln2_g[l], Wdown[l], tm=tm, dep=tok, name=f"ln2_bwd_da_{l}")
        tok = scatter("w_down", l, _wgrad(a, dz2, slabs_on_cols=False, tm=tm_w, tn=512, name=f"wgrad_down_{l}"))
        dup, facc = _ffn_bwd(da, up, taps_f[l], b_ffn_conv[l].reshape(1, F2), tb=tb_ffn, dep=tok, name=f"ffn_bwd_{l}")
        dx1 = _mm(dup, Wup[l], nt=True, res=dz2, tm=tm, name=f"mm_dx1_{l}")
        tok = scatter("w_up", l, _wgrad(x1, dup, slabs_on_cols=True, tm=tm_w, tn=F2q, name=f"wgrad_up_{l}"))
        dz1, gb1, dym = _ln_bwd_mm(dx1, xh1, r1, ln1_g[l], Wo[l], tm=tm, dep=tok, name=f"ln1_bwd_dym_{l}")
        tok = scatter("w_o", l, _wgrad(y, dz1, slabs_on_cols=False, tm=tm_w, tn=512, name=f"wgrad_o_{l}"))
        dh, dwc, dsc, dgn, dlb, dwbd = _mixer_bwd(
            h, dym, o, sp, taps_c[l], wbd[l], pool_scale[l].reshape(1, PW), hg_norm_g[l].reshape(1, HD),
            lbs[l].reshape(1, HW), mat, suf, dep=tok, name=f"mixer_bwd_{l}")
        tok = scatter("w_in", l, _wgrad(X, dh, slabs_on_cols=True, tm=tm_w, tn=DIN // 4, name=f"wgrad_in_{l}"))
        dX = _mm(dh, Win[l], nt=True, res=dz1, tm=tm, dep=tok, name=f"mm_dx_{l}")
        small_g[l] = dict(
            lbs=dlb[0], w_conv=dwc[0:3].T, w_pool=jnp.stack([dwbd[64 * g:64 * g + 64, 64 * g:64 * g + 64] for g in range(4)]),
            pool_scale=dsc[0], hg_norm_g=dgn[0], ln1_g=gb1[0], ln1_b=gb1[1], w_ffn_conv=facc[0:3].T,
            b_ffn_conv=facc[3], ln2_g=gb2[0], ln2_b=gb2[1])
        dxo = dX
    grad_x = dxo[ROW0:][None]

    sg_names = ["lbs", "w_conv", "w_pool", "pool_scale", "hg_norm_g", "ln1_g", "ln1_b", "w_ffn_conv",
                "b_ffn_conv", "ln2_g", "ln2_b"]
    sg_list = [dxo[PADR:ROW0]] + [jnp.stack([small_g[l][nm] for l in range(DEPTH)]) for nm in sg_names]
    sg_shapes = [a.shape for a in sg_list]
    packed = _pack(sg_list)
    (gathered,) = _exchange([packed], flips=ALL_FLIPS, n_slots=8, slot_of=_dev_slot, scatter=False, self_copy=True,
                            name="gather_small_grads")
    total = _sum_slots(gathered[:, None], name="sum_small_grads")[0]
    tot = dict(zip(["meta_tokens"] + sg_names, _unpack(total, sg_shapes)))
    (g_hg,) = lbs_vjp(tot["lbs"])
    small_grads = dict(
        meta_tokens=lax.dynamic_slice_in_dim(tot["meta_tokens"], chip * (D // 4), D // 4, axis=1),
        hg_lower_bounds=g_hg,
        w_conv=lax.dynamic_slice_in_dim(tot["w_conv"], chip * (CW // 4), CW // 4, axis=1),
        w_pool=tot["w_pool"], pool_scale=tot["pool_scale"], hg_norm_g=tot["hg_norm_g"],
        ln1_g=tot["ln1_g"], ln1_b=tot["ln1_b"],
        w_ffn_conv=lax.dynamic_slice_in_dim(tot["w_ffn_conv"], chip * F2q, F2q, axis=1),
        b_ffn_conv=tot["b_ffn_conv"], ln2_g=tot["ln2_g"], ln2_b=tot["ln2_b"])
    small_w = dict(meta_tokens=(meta_tokens, m_meta_tokens, v_meta_tokens),
                   hg_lower_bounds=(hg_lower_bounds, m_hg_lower_bounds, v_hg_lower_bounds),
                   w_conv=(w_conv, m_w_conv, v_w_conv), w_pool=(w_pool, m_w_pool, v_w_pool),
                   pool_scale=(pool_scale, m_pool_scale, v_pool_scale), hg_norm_g=(hg_norm_g, m_hg_norm_g, v_hg_norm_g),
                   ln1_g=(ln1_g, m_ln1_g, v_ln1_g), ln1_b=(ln1_b, m_ln1_b, v_ln1_b),
                   w_ffn_conv=(w_ffn_conv, m_w_ffn_conv, v_w_ffn_conv), b_ffn_conv=(b_ffn_conv, m_b_ffn_conv, v_b_ffn_conv),
                   ln2_g=(ln2_g, m_ln2_g, v_ln2_g), ln2_b=(ln2_b, m_ln2_b, v_ln2_b))
    names_s = list(small_w)
    shapes_s = [small_w[nm][0].shape for nm in names_s]
    pk = [_pack([small_w[nm][j] for nm in names_s])[None] for j in range(3)]
    pg = _pack([small_grads[nm] for nm in names_s])[None]
    outs_s = _adamw(pk[0], pk[1], pk[2], pg[0], None, layer=0, prev=None, name="adamw_small")
    small = {nm: [] for nm in names_s}
    for j in range(4):
        for nm, val in zip(names_s, _unpack(outs_s[j][0], shapes_s)):
            small[nm].append(val)

    big_w = dict(w_in=(w_in, m_w_in, v_w_in), w_o=(w_o, m_w_o, v_w_o), w_up=(w_up, m_w_up, v_w_up),
                 w_down=(w_down, m_w_down, v_w_down))
    big = {nm: None for nm in big_w}
    for l in reversed(range(DEPTH)):
        part = {}
        for nm in ("w_down", "w_up", "w_o", "w_in"):
            after = [dxo] if (nm, l) != ("w_in", 0) else [outs_s[0]] + [big[k][0] for k in big_w] + list(part.values())
            (own,), (recv,) = _send_wait(sc[nm, l], scatter=True, after=after, name=f"scatter_{nm}_{l}_wait")
            part[nm] = _sum_own(recv, own, chip, name=f"sum_{nm}_{l}")
        sib = _exchange([part[nm] for nm in big_w], flips=[(0, 0, 1)], n_slots=1, slot_of=_zero_slot, scatter=False,
                        self_copy=False, name=f"swap_cores_{l}")
        for k, nm in enumerate(big_w):
            w, m, v = big_w[nm]
            big[nm] = _adamw(w, m, v, part[nm], sib[k][0], layer=l, prev=big[nm], name=f"adamw_{nm}_{l}")

    order = ["meta_tokens", "hg_lower_bounds", "w_in", "w_conv", "w_pool", "pool_scale", "hg_norm_g", "w_o",
             "ln1_g", "ln1_b", "w_up", "w_ffn_conv", "b_ffn_conv", "w_down", "ln2_g", "ln2_b"]
    res = {nm: (big[nm] if nm in big else small[nm]) for nm in order}
    outs = [loss, grad_x]
    for j in range(4):
        outs += [res[nm][j] for nm in order]
    return tuple(outs)
```

```python
import functools

import numpy as np

import jax
import jax.numpy as jnp
from jax import lax
from jax.experimental import pallas as pl
from jax.experimental.pallas import tpu as pltpu

F32 = jnp.float32
BF = jnp.bfloat16
MX = jnp.bfloat16

D = 1024
CW = 256
HW = 512
HD = 128
NH = 4
PW = 256
DIN = 3072
N_META = 16
CH = 64
MB = 4
BM = MB * CH
ROW0 = 256
PADR = ROW0 - N_META
HALO = 16
FH = 8
LEVELS = (32, 16, 8, 4, 2, 1)
DEPTH = 2
ALPHA = (2 * DEPTH) ** 0.25
LN_EPS = 1e-5
RMS_EPS = 1e-6
F_FLOOR = 1e-30
SCALE = HD ** -0.5
ADAM_LR, ADAM_B1, ADAM_B2, ADAM_EPS, ADAM_WD, ADAM_STEP = 0.001, 0.9, 0.999, 1e-08, 0.01, 10

VMEM_V7X = 64 * 2 ** 20
LANES = 128
SUBLANES = 8

NN = (((1,), (0,)), ((), ()))
NT = (((1,), (1,)), ((), ()))
TN = (((0,), (0,)), ((), ()))
MESH = pl.DeviceIdType.MESH
ANY = pl.BlockSpec(memory_space=pl.ANY)


def _dot(a, b, dn):
    return lax.dot_general(a.astype(MX), b.astype(MX), dn, preferred_element_type=F32)


def _cp(sem, est_bytes):
    lim = int(min(VMEM_V7X - 6 * 2 ** 20, max(32 * 2 ** 20, est_bytes)))
    return pltpu.CompilerParams(dimension_semantics=sem, vmem_limit_bytes=lim)


def _nbytes(shape, dtype):
    return int(np.prod(shape)) * jnp.dtype(dtype).itemsize


def _row_tile(rows, row_bytes, budget):
    best = SUBLANES
    for t in range(SUBLANES, rows + 1, SUBLANES):
        if rows % t == 0 and t * row_bytes <= budget:
            best = t
    return best


def _after(dep, body, in_specs, args):
    if dep is None:
        return body, list(in_specs), list(args)

    def body_after(dep_ref, *refs):
        body(*refs)

    return body_after, [ANY] + list(in_specs), [dep] + list(args)


def _mm(a, w, *, nt=False, res=None, tm, out_dtype=F32, zero_inert=False, dep=None, name):
    R, K = a.shape
    N = w.shape[0] if nt else w.shape[1]
    dn = NT if nt else NN

    def body(*refs):
        if res is None:
            a_ref, w_ref, o_ref = refs
        else:
            a_ref, w_ref, r_ref, o_ref = refs
        acc = _dot(a_ref[...], w_ref[...], dn)
        if res is not None:
            acc = acc + ALPHA * r_ref[...]
        if zero_inert:
            acc = jnp.where(_row_ids(pl.program_id(0) * tm, tm) >= PADR, acc, 0.0)
        o_ref[...] = acc.astype(out_dtype)

    in_specs = [pl.BlockSpec((tm, K), lambda i: (i, 0)), pl.BlockSpec(w.shape, lambda i: (0, 0))]
    args = [a, w]
    est = 2 * _nbytes((tm, K), a.dtype) + 2 * _nbytes(w.shape, w.dtype) + 3 * _nbytes((tm, N), F32)
    if res is not None:
        in_specs.append(pl.BlockSpec((tm, N), lambda i: (i, 0)))
        args.append(res)
        est += 2 * _nbytes((tm, N), F32)
    body, in_specs, args = _after(dep, body, in_specs, args)
    return pl.pallas_call(
        body, name=name, grid=(R // tm,), in_specs=in_specs,
        out_specs=pl.BlockSpec((tm, N), lambda i: (i, 0)),
        out_shape=jax.ShapeDtypeStruct((R, N), out_dtype),
        compiler_params=_cp(("parallel",), est + 4 * 2 ** 20),
    )(*args)


def _wgrad(a, b, *, slabs_on_cols, tm, tn, name):
    R, Ka = a.shape
    Nb = b.shape[1]
    if slabs_on_cols:
        out_shape = (4, Ka, Nb // 4)
        assert tn == Nb // 4
        out_spec = pl.BlockSpec((None, Ka, tn), lambda j, i: (j, 0, 0))
    else:
        out_shape = (4, Ka // 4, Nb)
        out_spec = pl.BlockSpec((4, Ka // 4, tn), lambda j, i: (0, 0, j))

    def body(a_ref, b_ref, o_ref):
        @pl.when(pl.program_id(1) == 0)
        def _():
            o_ref[...] = jnp.zeros(o_ref.shape, F32)

        acc = _dot(a_ref[...], b_ref[...], TN)
        o_ref[...] += acc.reshape(o_ref.shape)

    in_specs = [pl.BlockSpec((tm, Ka), lambda j, i: (i, 0)), pl.BlockSpec((tm, tn), lambda j, i: (i, j))]
    est = 2 * _nbytes((tm, Ka), a.dtype) + 2 * _nbytes((tm, tn), b.dtype) + 4 * _nbytes((Ka, tn), F32) \
        + _nbytes((tm, Ka), F32)
    return pl.pallas_call(
        body, name=name, grid=(Nb // tn, R // tm), in_specs=in_specs, out_specs=out_spec,
        out_shape=jax.ShapeDtypeStruct(out_shape, F32),
        compiler_params=_cp(("parallel", "arbitrary"), est + 4 * 2 ** 20),
    )(a, b)


def _mm_ln(a, w, xres, g, b, *, tm, dep=None, name):
    R, K = a.shape

    def body(a_ref, w_ref, x_ref, g_ref, b_ref, xo_ref, xh_ref, r_ref):
        z = ALPHA * x_ref[...] + _dot(a_ref[...], w_ref[...], NN)
        mu = jnp.mean(z, axis=-1, keepdims=True)
        zc = z - mu
        var = jnp.mean(zc * zc, axis=-1, keepdims=True)
        r = lax.rsqrt(var + LN_EPS)
        xh = zc * r
        xh_ref[...] = xh
        r_ref[...] = r
        xo_ref[...] = xh * g_ref[...] + b_ref[...]

    row = lambda i: (i, 0)
    fix = lambda i: (0, 0)
    est = 2 * _nbytes((tm, K), a.dtype) + 2 * _nbytes(w.shape, w.dtype) + 10 * _nbytes((tm, D), F32)
    body, in_specs, args = _after(
        dep, body, [pl.BlockSpec((tm, K), row), pl.BlockSpec(w.shape, fix), pl.BlockSpec((tm, D), row),
                    pl.BlockSpec((1, D), fix), pl.BlockSpec((1, D), fix)],
        [a, w, xres, g.reshape(1, D), b.reshape(1, D)])
    return pl.pallas_call(
        body, name=name, grid=(R // tm,), in_specs=in_specs,
        out_specs=[pl.BlockSpec((tm, D), row), pl.BlockSpec((tm, D), row), pl.BlockSpec((tm, 1), row)],
        out_shape=[jax.ShapeDtypeStruct((R, D), F32), jax.ShapeDtypeStruct((R, D), F32),
                   jax.ShapeDtypeStruct((R, 1), F32)],
        compiler_params=_cp(("parallel",), est + 4 * 2 ** 20),
    )(*args)


def _ln_bwd(dx, xh, r, g, *, tm, dep=None, name):
    R = dx.shape[0]

    def body(dx_ref, xh_ref, r_ref, g_ref, dz_ref, gb_ref):
        @pl.when(pl.program_id(0) == 0)
        def _():
            gb_ref[...] = jnp.zeros(gb_ref.shape, F32)

        dxv = dx_ref[...]
        xhv = xh_ref[...]
        dyh = dxv * g_ref[...]
        m1 = jnp.mean(dyh, axis=-1, keepdims=True)
        m2 = jnp.mean(dyh * xhv, axis=-1, keepdims=True)
        dz_ref[...] = r_ref[...] * (dyh - m1 - xhv * m2)
        gb_ref[0:1, :] += jnp.sum(dxv * xhv, axis=0, keepdims=True)
        gb_ref[1:2, :] += jnp.sum(dxv, axis=0, keepdims=True)

    row = lambda i: (i, 0)
    fix = lambda i: (0, 0)
    body, in_specs, args = _after(
        dep, body, [pl.BlockSpec((tm, D), row), pl.BlockSpec((tm, D), row), pl.BlockSpec((tm, 1), row),
                    pl.BlockSpec((1, D), fix)], [dx, xh, r, g.reshape(1, D)])
    return pl.pallas_call(
        body, name=name, grid=(R // tm,), in_specs=in_specs,
        out_specs=[pl.BlockSpec((tm, D), row), pl.BlockSpec((SUBLANES, D), fix)],
        out_shape=[jax.ShapeDtypeStruct((R, D), F32), jax.ShapeDtypeStruct((SUBLANES, D), F32)],
        compiler_params=_cp(("arbitrary",), 12 * _nbytes((tm, D), F32)),
    )(*args)


def _resident(shape):
    return pl.BlockSpec(shape, lambda i: (0,) * len(shape), pipeline_mode=pl.Buffered(1))


def _ln_bwd_mm(dx, xh, r, g, w, *, tm, dep=None, name):
    R = dx.shape[0]
    N = w.shape[0]

    def body(dx_ref, xh_ref, r_ref, g_ref, w_ref, dz_ref, gb_ref, o_ref):
        @pl.when(pl.program_id(0) == 0)
        def _():
            gb_ref[...] = jnp.zeros(gb_ref.shape, F32)

        dxv = dx_ref[...]
        xhv = xh_ref[...]
        dyh = dxv * g_ref[...]
        m1 = jnp.mean(dyh, axis=-1, keepdims=True)
        m2 = jnp.mean(dyh * xhv, axis=-1, keepdims=True)
        dz = r_ref[...] * (dyh - m1 - xhv * m2)
        dz_ref[...] = dz
        gb_ref[0:1, :] += jnp.sum(dxv * xhv, axis=0, keepdims=True)
        gb_ref[1:2, :] += jnp.sum(dxv, axis=0, keepdims=True)
        o_ref[...] = _dot(dz, w_ref[...], NT)

    row = lambda i: (i, 0)
    fix = lambda i: (0, 0)
    body, in_specs, args = _after(
        dep, body, [pl.BlockSpec((tm, D), row), pl.BlockSpec((tm, D), row), pl.BlockSpec((tm, 1), row),
                    pl.BlockSpec((1, D), fix), _resident(w.shape)], [dx, xh, r, g.reshape(1, D), w])
    est = _nbytes(w.shape, w.dtype) + 14 * _nbytes((tm, D), F32) + 4 * _nbytes((tm, N), F32)
    return pl.pallas_call(
        body, name=name, grid=(R // tm,), in_specs=in_specs,
        out_specs=[pl.BlockSpec((tm, D), row), pl.BlockSpec((SUBLANES, D), fix), pl.BlockSpec((tm, N), row)],
        out_shape=[jax.ShapeDtypeStruct((R, D), F32), jax.ShapeDtypeStruct((SUBLANES, D), F32),
                   jax.ShapeDtypeStruct((R, N), F32)],
        compiler_params=_cp(("arbitrary",), est + 4 * 2 ** 20),
    )(*args)


def _mm_ffn_fwd(x1, w, taps, bias, *, tm, name):
    R, K = x1.shape
    F2 = w.shape[1]
    F = F2 // 2

    def body(x_ref, w_ref, t_ref, b_ref, up_ref, u_ref, a_ref, carry_ref):
        i = pl.program_id(0)

        @pl.when(i == 0)
        def _():
            carry_ref[...] = jnp.zeros(carry_ref.shape, F32)

        acc = _dot(x_ref[...], w_ref[...], NN)
        acc = jnp.where(_row_ids(i * tm, tm) >= PADR, acc, 0.0)
        up_ref[...] = acc
        ue = jnp.concatenate([carry_ref[...], acc], axis=0)
        carry_ref[...] = acc[tm - FH:tm]
        u = (t_ref[0:1, :] * pltpu.roll(ue, 2, 0)[FH:] + t_ref[1:2, :] * pltpu.roll(ue, 1, 0)[FH:]
             + t_ref[2:3, :] * acc + b_ref[...])
        u_ref[...] = u
        gate = u[:, :F]
        val = u[:, F:]
        a_ref[...] = (gate * jax.nn.sigmoid(gate) * val).astype(a_ref.dtype)

    row = lambda i: (i, 0)
    est = _nbytes(w.shape, w.dtype) + 2 * _nbytes((tm, K), x1.dtype) + 10 * _nbytes((tm + FH, F2), F32)
    return pl.pallas_call(
        body, name=name, grid=(R // tm,),
        in_specs=[pl.BlockSpec((tm, K), row), _resident(w.shape), _resident((SUBLANES, F2)), _resident((1, F2))],
        out_specs=[pl.BlockSpec((tm, F2), row), pl.BlockSpec((tm, F2), row), pl.BlockSpec((tm, F), row)],
        out_shape=[jax.ShapeDtypeStruct((R, F2), F32), jax.ShapeDtypeStruct((R, F2), F32),
                   jax.ShapeDtypeStruct((R, F), MX)],
        scratch_shapes=[pltpu.VMEM((FH, F2), F32)],
        compiler_params=_cp(("arbitrary",), est + 4 * 2 ** 20),
    )(x1, w, taps, bias)


def _loss(x2, tgt, *, name):
    R = x2.shape[0]
    tb = ROW0

    def body(x_ref, t_ref, dy_ref, acc_ref):
        i = pl.program_id(0)

        @pl.when(i == 0)
        def _():
            acc_ref[...] = jnp.zeros(acc_ref.shape, F32)
            dy_ref[...] = jnp.zeros(dy_ref.shape, F32)

        @pl.when(i > 0)
        def _():
            err = x_ref[...] - t_ref[...]
            dy_ref[...] = err / D
            acc_ref[0:1, :] += jnp.sum(err * err, axis=0, keepdims=True)

    return pl.pallas_call(
        body, name=name, grid=(R // tb,),
        in_specs=[pl.BlockSpec((tb, D), lambda i: (i, 0)),
                  pl.BlockSpec((tb, D), lambda i: (jnp.maximum(i - 1, 0), 0))],
        out_specs=[pl.BlockSpec((tb, D), lambda i: (i, 0)), pl.BlockSpec((SUBLANES, D), lambda i: (0, 0))],
        out_shape=[jax.ShapeDtypeStruct((R, D), F32), jax.ShapeDtypeStruct((SUBLANES, D), F32)],
        compiler_params=_cp(("arbitrary",), 8 * _nbytes((tb, D), F32)),
    )(x2, tgt)


def _row_ids(start, n):
    return start + lax.broadcasted_iota(jnp.int32, (n, 1), 0)


def _ffn_fwd(up, taps, bias, *, tb, name):
    R, F2 = up.shape
    F = F2 // 2
    nh = tb // FH

    def body(u_ref, up_ref, w_ref, b_ref, a_ref):
        ue = jnp.concatenate([up_ref[...], u_ref[...]], axis=0)
        u = (w_ref[0:1, :] * pltpu.roll(ue, 2, 0)[FH:] + w_ref[1:2, :] * pltpu.roll(ue, 1, 0)[FH:]
             + w_ref[2:3, :] * u_ref[...] + b_ref[...])
        gate = u[:, :F]
        val = u[:, F:]
        a_ref[...] = (gate * jax.nn.sigmoid(gate) * val).astype(a_ref.dtype)

    est = 2 * _nbytes((tb, F2), F32) * 4
    return pl.pallas_call(
        body, name=name, grid=(R // tb,),
        in_specs=[pl.BlockSpec((tb, F2), lambda i: (i, 0)),
                  pl.BlockSpec((FH, F2), lambda i: (jnp.maximum(i * nh - 1, 0), 0)),
                  pl.BlockSpec((SUBLANES, F2), lambda i: (0, 0)), pl.BlockSpec((1, F2), lambda i: (0, 0))],
        out_specs=pl.BlockSpec((tb, F), lambda i: (i, 0)),
        out_shape=jax.ShapeDtypeStruct((R, F), MX),
        compiler_params=_cp(("parallel",), est),
    )(up, up, taps, bias)


def _ffn_bwd(da, up, u, taps, *, tb, dep=None, name):
    R, F2 = up.shape
    F = F2 // 2
    nh = tb // FH
    nb = R // tb
    last = R // FH - 1
    m = tb + FH

    def body(da_ref, dan_ref, x_ref, u_ref, un_ref, w_ref, du_ref, acc_ref):
        i = pl.program_id(0)

        @pl.when(i == 0)
        def _():
            acc_ref[...] = jnp.zeros(acc_ref.shape, F32)

        inside = (i < nb - 1).astype(F32)
        w0, w1, w2 = w_ref[0:1, :], w_ref[1:2, :], w_ref[2:3, :]
        u = jnp.concatenate([u_ref[...], un_ref[...]], axis=0)
        dae = jnp.concatenate([da_ref[...], dan_ref[...] * inside], axis=0)
        gate = u[:, :F]
        val = u[:, F:]
        sg = jax.nn.sigmoid(gate)
        gs = gate * sg
        du = jnp.concatenate([dae * val * (sg * (1.0 + gate - gs)), dae * gs], axis=1)
        du0 = du[0:tb]
        du1 = pltpu.roll(du, m - 1, 0)[0:tb]
        du2 = pltpu.roll(du, m - 2, 0)[0:tb]
        du_ref[...] = (w2 * du0 + w1 * du1 + w0 * du2).astype(du_ref.dtype)

        @pl.when(i * tb < PADR)
        def _():
            row = _row_ids(i * tb, tb)
            du_ref[...] = jnp.where(row >= PADR, du_ref[...], jnp.zeros((), du_ref.dtype))

        x = x_ref[...]
        acc_ref[0:1, :] += jnp.sum(du2 * x, axis=0, keepdims=True)
        acc_ref[1:2, :] += jnp.sum(du1 * x, axis=0, keepdims=True)
        acc_ref[2:3, :] += jnp.sum(du0 * x, axis=0, keepdims=True)
        acc_ref[3:4, :] += jnp.sum(du0, axis=0, keepdims=True)

    nxt = lambda i: (jnp.minimum(i * nh + nh, last), 0)
    cur = lambda i: (i, 0)
    fix = lambda i: (0, 0)
    est = 12 * _nbytes((tb + 2 * FH, F2), F32)
    body, in_specs, args = _after(
        dep, body, [pl.BlockSpec((tb, F), cur), pl.BlockSpec((FH, F), nxt),
                    pl.BlockSpec((tb, F2), cur), pl.BlockSpec((tb, F2), cur), pl.BlockSpec((FH, F2), nxt),
                    pl.BlockSpec((SUBLANES, F2), fix)], [da, da, up, u, u, taps])
    return pl.pallas_call(
        body, name=name, grid=(R // tb,), in_specs=in_specs,
        out_specs=[pl.BlockSpec((tb, F2), cur), pl.BlockSpec((SUBLANES, F2), fix)],
        out_shape=[jax.ShapeDtypeStruct((R, F2), MX), jax.ShapeDtypeStruct((SUBLANES, F2), F32)],
        compiler_params=_cp(("arbitrary",), est),
    )(*args)


def _hg_consts():
    t = np.arange(CH)[:, None]
    j = np.arange(CH)[None, :]
    low = (j <= t).astype(np.float32)
    blocks = [low, (j > t).astype(np.float32)]
    for m in LEVELS:
        ref = (t // (2 * m)) * 2 * m + m - 1
        blocks.append(low - (j <= ref).astype(np.float32))
    mat = np.concatenate(blocks, axis=0)
    suf = (j >= t).astype(np.float32)
    return jnp.asarray(mat, BF), jnp.asarray(suf, BF)


def _split_dot(mat, x):
    hi = x.astype(BF)
    lo = (x - hi.astype(F32)).astype(BF)
    return (lax.dot_general(mat, hi, NN, preferred_element_type=F32)
            + lax.dot_general(mat, lo, NN, preferred_element_type=F32))


def _lane_select(a2, a4, a8, a16):
    lane = lax.broadcasted_iota(jnp.int32, (1, PW), 1)
    return jnp.where(lane < 64, a2, jnp.where(lane < 128, a4, jnp.where(lane < 192, a8, a16)))


def _pool_count(row):
    win = _lane_select(2.0, 4.0, 8.0, 16.0)
    t1 = jnp.maximum((row - PADR + 1).astype(F32), 1.0)
    return jnp.minimum(t1, win)


def _gates(fz, lb, valid):
    sig = jax.nn.sigmoid(fz)
    f = lb + (1.0 - lb) * sig
    lf = jnp.where(valid, jnp.log(jnp.maximum(f, F_FLOOR)), 0.0)
    kk = jnp.where(valid, (1.0 - lb) * (1.0 - sig), 0.0)
    return sig, f, lf, kk


def _level_masks():
    tt = lax.broadcasted_iota(jnp.int32, (CH, CH), 0)
    ss = lax.broadcasted_iota(jnp.int32, (CH, CH), 1)
    xr = tt ^ ss
    low = tt > ss
    return tt, ss, [(xr >= m) & (xr < 2 * m) & low for m in LEVELS]


def _intra(q, kk, ex, sl, tt, ss, masks):
    a = jnp.where(tt == ss, jnp.sum(q * kk, axis=-1, keepdims=True), 0.0)
    parts = []
    for l in range(len(LEVELS)):
        e = jnp.exp(-jnp.abs(ex[(2 + l) * CH:(3 + l) * CH, sl]))
        ql = (q * e).astype(MX)
        kl = (kk * e).astype(MX)
        a = a + jnp.where(masks[l], _dot(ql, kl, NT), 0.0)
        parts.append((e, ql, kl))
    return a, parts


def _mixer_fwd(h, taps, wbd, ps, gn, lb, mat, *, name):
    R = h.shape[0]
    nb = R // BM
    q4 = BM // HALO

    def body(h_ref, hp_ref, w_ref, wbd_ref, ps_ref, gn_ref, lb_ref, mat_ref, y_ref, o_ref, sp_ref, st_ref):
        i = pl.program_id(0)

        @pl.when(i == 0)
        def _():
            st_ref[...] = jnp.zeros(st_ref.shape, F32)

        row = _row_ids(i * BM, BM)
        valid_b = row >= PADR
        rowe = _row_ids(i * BM - HALO, BM + HALO)
        valide = rowe >= PADR
        se = jnp.concatenate([hp_ref[:, 256:512] * hp_ref[:, 512:768], h_ref[:, 256:512] * h_ref[:, 512:768]], axis=0)
        se = jnp.where(valide, se, 0.0)
        conv = (w_ref[0:1, :] * pltpu.roll(se, 2, 0) + w_ref[1:2, :] * pltpu.roll(se, 1, 0)
                + w_ref[2:3, :] * se)[HALO:]
        y_ref[:, 0:CW] = (h_ref[:, 0:256] * conv).astype(y_ref.dtype)
        ve = jnp.where(valide, jnp.concatenate([hp_ref[:, 2816:3072], h_ref[:, 2816:3072]], axis=0), 0.0)
        s2 = ve + pltpu.roll(ve, 1, 0)
        s4 = s2 + pltpu.roll(s2, 2, 0)
        s8 = s4 + pltpu.roll(s4, 4, 0)
        s16 = s8 + pltpu.roll(s8, 8, 0)
        dp = (_lane_select(s2, s4, s8, s16) / _pool_count(rowe) - ve)[HALO:]
        y_ref[:, CW + HW:D] = (_dot(dp, wbd_ref[...], NN) * ps_ref[...]).astype(y_ref.dtype)
        lbv = lb_ref[...]
        tt, ss, masks = _level_masks()
        for c in range(MB):
            rs = slice(c * CH, (c + 1) * CH)
            _, _, lf, kk = _gates(h_ref[rs, 1280:1792], lbv, valid_b[rs])
            q = h_ref[rs, 768:1280] * SCALE
            ii = h_ref[rs, 1792:2304]
            gz = h_ref[rs, 2304:2816]
            ex = _split_dot(mat_ref[...], lf)
            eg = jnp.exp(ex[0:CH])
            q0 = q * eg
            kr = kk * jnp.exp(ex[CH:2 * CH])
            for hd in range(NH):
                sl = slice(HD * hd, HD * (hd + 1))
                st = st_ref[hd]
                sp_ref[c, hd] = st
                a, _ = _intra(q[:, sl], kk[:, sl], ex, sl, tt, ss, masks)
                o = _dot(q0[:, sl], st, NT) + _dot(a, ii[:, sl], NN)
                st_ref[hd] = st * eg[CH - 1:CH, sl] + _dot(ii[:, sl], kr[:, sl], TN)
                o_ref[rs, sl] = o
                r = lax.rsqrt(jnp.mean(o * o, axis=-1, keepdims=True) + RMS_EPS)
                g_ = gz[:, sl]
                y_ref[rs, CW + HD * hd:CW + HD * (hd + 1)] = (
                    o * r * gn_ref[...] * (g_ * jax.nn.sigmoid(g_))).astype(y_ref.dtype)

    fix = lambda i: (0, 0)
    return pl.pallas_call(
        body, name=name, grid=(nb,),
        in_specs=[pl.BlockSpec((BM, DIN), lambda i: (i, 0)),
                  pl.BlockSpec((HALO, DIN), lambda i: (jnp.maximum(i * q4 - 1, 0), 0)),
                  pl.BlockSpec((SUBLANES, CW), fix), pl.BlockSpec((PW, PW), fix), pl.BlockSpec((1, PW), fix),
                  pl.BlockSpec((1, HD), fix), pl.BlockSpec((1, HW), fix), pl.BlockSpec(mat.shape, fix)],
        out_specs=[pl.BlockSpec((BM, D), lambda i: (i, 0)), pl.BlockSpec((BM, HW), lambda i: (i, 0)),
                   pl.BlockSpec((MB, NH, HD, HD), lambda i: (i, 0, 0, 0))],
        out_shape=[jax.ShapeDtypeStruct((R, D), MX), jax.ShapeDtypeStruct((R, HW), F32),
                   jax.ShapeDtypeStruct((R // CH, NH, HD, HD), F32)],
        scratch_shapes=[pltpu.VMEM((NH, HD, HD), F32)],
        compiler_params=_cp(("arbitrary",), 32 * 2 ** 20),
    )(h, h, taps, wbd, ps, gn, lb, mat)


def _mixer_bwd(h, dy, o, sp, taps, wbd, ps, gn, lb, mat, suf, *, dep=None, name):
    R = h.shape[0]
    nb = R // CH
    q4 = CH // HALO
    lasth = R // HALO - 1

    def body(h_ref, hp_ref, hn_ref, dy_ref, dyn_ref, o_ref, sp_ref, w_ref, wbd_ref, ps_ref, gn_ref, lb_ref,
             mat_ref, suf_ref, dh_ref, dw_ref, dsc_ref, dgn_ref, dlb_ref, dwbd_ref, dst_ref):
        i = pl.program_id(0)
        b = nb - 1 - i

        @pl.when(i == 0)
        def _():
            dst_ref[...] = jnp.zeros(dst_ref.shape, F32)
            dw_ref[...] = jnp.zeros(dw_ref.shape, F32)
            dsc_ref[...] = jnp.zeros(dsc_ref.shape, F32)
            dgn_ref[...] = jnp.zeros(dgn_ref.shape, F32)
            dlb_ref[...] = jnp.zeros(dlb_ref.shape, F32)
            dwbd_ref[...] = jnp.zeros(dwbd_ref.shape, F32)

        ne = CH + 2 * HALO
        nc = CH + HALO
        row = _row_ids(b * CH, CH)
        valid = row >= PADR
        rowe = _row_ids(b * CH - HALO, ne)
        valide = (rowe >= PADR) & (rowe < R)
        rown = rowe[HALO:]
        validn = rown < R

        def cat3(lo, hi):
            return jnp.concatenate([hp_ref[:, lo:hi], h_ref[:, lo:hi], hn_ref[:, lo:hi]], axis=0)

        def catd(lo, hi):
            return jnp.where(validn, jnp.concatenate([dy_ref[:, lo:hi], dyn_ref[:, lo:hi]], axis=0), 0.0)

        w0, w1, w2 = w_ref[0:1, :], w_ref[1:2, :], w_ref[2:3, :]
        cce = cat3(256, 512)
        cve = cat3(512, 768)
        se = jnp.where(valide, cce * cve, 0.0)
        sm2 = pltpu.roll(se, 2, 0)
        sm1 = pltpu.roll(se, 1, 0)
        conv = (w0 * sm2 + w1 * sm1 + w2 * se)[HALO:HALO + CH]
        cbn = jnp.concatenate([h_ref[:, 0:256], hn_ref[:, 0:256]], axis=0)
        dconv = catd(0, CW) * cbn
        ds = w2 * dconv + w1 * pltpu.roll(dconv, nc - 1, 0) + w0 * pltpu.roll(dconv, nc - 2, 0)
        ds = jnp.where(valid, ds[0:CH], 0.0)
        dcv = dconv[0:CH]
        dw_ref[0:1, :] += jnp.sum(dcv * sm2[HALO:HALO + CH], axis=0, keepdims=True)
        dw_ref[1:2, :] += jnp.sum(dcv * sm1[HALO:HALO + CH], axis=0, keepdims=True)
        dw_ref[2:3, :] += jnp.sum(dcv * se[HALO:HALO + CH], axis=0, keepdims=True)
        dh_ref[:, 0:256] = jnp.where(valid, dy_ref[:, 0:CW] * conv, 0.0).astype(dh_ref.dtype)
        dh_ref[:, 256:512] = (ds * h_ref[:, 512:768]).astype(dh_ref.dtype)
        dh_ref[:, 512:768] = (ds * h_ref[:, 256:512]).astype(dh_ref.dtype)

        ve = jnp.where(valide, cat3(2816, 3072), 0.0)
        s2 = ve + pltpu.roll(ve, 1, 0)
        s4 = s2 + pltpu.roll(s2, 2, 0)
        s8 = s4 + pltpu.roll(s4, 4, 0)
        s16 = s8 + pltpu.roll(s8, 8, 0)
        cnte = _pool_count(rowe)
        dp = (_lane_select(s2, s4, s8, s16) / cnte - ve)[HALO:HALO + CH]
        dyp = catd(CW + HW, D)
        pre = _dot(dp, wbd_ref[...], NN)
        dsc_ref[0:1, :] += jnp.sum(dyp[0:CH] * pre, axis=0, keepdims=True)
        dyps = dyp * ps_ref[...]
        dd = _dot(dyps, wbd_ref[...], NT)
        dwbd_ref[...] += _dot(dp, dyps[0:CH], TN)
        e = dd / cnte[HALO:]
        t2 = e + pltpu.roll(e, nc - 1, 0)
        t4 = t2 + pltpu.roll(t2, nc - 2, 0)
        t8 = t4 + pltpu.roll(t4, nc - 4, 0)
        t16 = t8 + pltpu.roll(t8, nc - 8, 0)
        dv = (_lane_select(t2, t4, t8, t16) - dd)[0:CH]
        dh_ref[:, 2816:3072] = jnp.where(valid, dv, 0.0).astype(dh_ref.dtype)

        lbv = lb_ref[...]
        sig, f, lf, kk = _gates(h_ref[:, 1280:1792], lbv, valid)
        q = h_ref[:, 768:1280] * SCALE
        ii = h_ref[:, 1792:2304]
        gz = h_ref[:, 2304:2816]
        ex = _split_dot(mat_ref[...], lf)
        eg = jnp.exp(ex[0:CH])
        egr = jnp.exp(ex[CH:2 * CH])
        q0 = q * eg
        kr = kk * egr
        tt, ss, masks = _level_masks()
        gnv = gn_ref[...]
        dqs, dks, dis, dgzs, tails, dbs = [], [], [], [], [], []
        for hd in range(NH):
            sl = slice(HD * hd, HD * (hd + 1))
            ov = o_ref[:, sl]
            r = lax.rsqrt(jnp.mean(ov * ov, axis=-1, keepdims=True) + RMS_EPS)
            oh = ov * r
            g_ = gz[:, sl]
            sg = jax.nn.sigmoid(g_)
            dyv = dy_ref[:, CW + HD * hd:CW + HD * (hd + 1)]
            don = dyv * (g_ * sg)
            dgzs.append(dyv * (oh * gnv) * (sg * (1.0 + g_ * (1.0 - sg))))
            dgn_ref[0:1, :] += jnp.sum(don * oh, axis=0, keepdims=True)
            doh = don * gnv
            do = r * (doh - oh * jnp.mean(doh * oh, axis=-1, keepdims=True))
            st = sp_ref[hd]
            dst = dst_ref[hd]
            qh, kh, ih = q[:, sl], kk[:, sl], ii[:, sl]
            a, parts = _intra(qh, kh, ex, sl, tt, ss, masks)
            da = jnp.where(tt >= ss, _dot(do, ih, NT), 0.0)
            dis.append(_dot(a, do, TN) + _dot(kr[:, sl], dst, NT))
            q0h = q0[:, sl].astype(MX)
            krh = kr[:, sl].astype(MX)
            dq0 = _dot(do, st, NN)
            dkr = _dot(ih, dst, NN)
            dq = dq0 * eg[:, sl]
            dk = dkr * egr[:, sl]
            kdk = krh.astype(F32) * dkr
            db = q0h.astype(F32) * dq0 - kdk
            tails.append(jnp.sum(kdk, axis=0, keepdims=True)
                         + eg[CH - 1:CH, sl] * jnp.sum(dst * st, axis=0, keepdims=True))
            dga = jnp.sum(jnp.where(tt == ss, da, 0.0), axis=-1, keepdims=True)
            dq = dq + dga * kh
            dk = dk + dga * qh
            for l in range(len(LEVELS)):
                e_l, ql, kl = parts[l]
                dpl = jnp.where(masks[l], da, 0.0).astype(MX)
                dql = _dot(dpl, kl, NN)
                dkl = _dot(dpl, ql, TN)
                dq = dq + dql * e_l
                dk = dk + dkl * e_l
                db = db + (ql.astype(F32) * dql - kl.astype(F32) * dkl)
            dst_ref[hd] = dst * eg[CH - 1:CH, sl] + _dot(do, q0h, TN)
            dqs.append(dq)
            dks.append(dk)
            dbs.append(db)
        dq = jnp.concatenate(dqs, axis=1)
        dk = jnp.concatenate(dks, axis=1)
        db = jnp.concatenate(dbs, axis=1)
        dlf = _split_dot(suf_ref[...], db) + jnp.concatenate(tails, axis=1)
        t = jnp.where(valid, dlf * jnp.where(f > F_FLOOR, 1.0 / f, 0.0) - dk, 0.0)
        dlb_ref[0:1, :] += jnp.sum(t * (1.0 - sig), axis=0, keepdims=True)
        dh_ref[:, 768:1280] = jnp.where(valid, dq * SCALE, 0.0).astype(dh_ref.dtype)
        dh_ref[:, 1280:1792] = (t * (1.0 - lbv) * (sig * (1.0 - sig))).astype(dh_ref.dtype)
        dh_ref[:, 1792:2304] = jnp.where(valid, jnp.concatenate(dis, axis=1), 0.0).astype(dh_ref.dtype)
        dh_ref[:, 2304:2816] = jnp.where(valid, jnp.concatenate(dgzs, axis=1), 0.0).astype(dh_ref.dtype)

    cur = lambda i: (nb - 1 - i, 0)
    prev = lambda i: (jnp.maximum((nb - 1 - i) * q4 - 1, 0), 0)
    nxt = lambda i: (jnp.minimum((nb - 1 - i) * q4 + q4, lasth), 0)
    fix = lambda i: (0, 0)
    body, in_specs, args = _after(
        dep, body,
        [pl.BlockSpec((CH, DIN), cur), pl.BlockSpec((HALO, DIN), prev), pl.BlockSpec((HALO, DIN), nxt),
         pl.BlockSpec((CH, D), cur), pl.BlockSpec((HALO, D), nxt), pl.BlockSpec((CH, HW), cur),
         pl.BlockSpec((None, NH, HD, HD), lambda i: (nb - 1 - i, 0, 0, 0)),
         pl.BlockSpec((SUBLANES, CW), fix), pl.BlockSpec((PW, PW), fix), pl.BlockSpec((1, PW), fix),
         pl.BlockSpec((1, HD), fix), pl.BlockSpec((1, HW), fix), pl.BlockSpec(mat.shape, fix),
         pl.BlockSpec(suf.shape, fix)],
        [h, h, h, dy, dy, o, sp, taps, wbd, ps, gn, lb, mat, suf])
    return pl.pallas_call(
        body, name=name, grid=(nb,), in_specs=in_specs,
        out_specs=[pl.BlockSpec((CH, DIN), cur), pl.BlockSpec((SUBLANES, CW), fix), pl.BlockSpec((SUBLANES, PW), fix),
                   pl.BlockSpec((SUBLANES, HD), fix), pl.BlockSpec((SUBLANES, HW), fix), pl.BlockSpec((PW, PW), fix)],
        out_shape=[jax.ShapeDtypeStruct((R, DIN), MX), jax.ShapeDtypeStruct((SUBLANES, CW), F32),
                   jax.ShapeDtypeStruct((SUBLANES, PW), F32), jax.ShapeDtypeStruct((SUBLANES, HD), F32),
                   jax.ShapeDtypeStruct((SUBLANES, HW), F32), jax.ShapeDtypeStruct((PW, PW), F32)],
        scratch_shapes=[pltpu.VMEM((NH, HD, HD), F32)],
        compiler_params=_cp(("arbitrary",), 40 * 2 ** 20),
    )(*args)


def _sum_slots(recv, *, name):
    S, L, rows, cols = recv.shape
    tr = _row_tile(rows, S * cols * 4, 6 * 2 ** 20)

    def body(r_ref, o_ref):
        acc = r_ref[0]
        for s in range(1, S):
            acc = acc + r_ref[s]
        o_ref[...] = acc

    return pl.pallas_call(
        body, name=name, grid=(L, rows // tr),
        in_specs=[pl.BlockSpec((S, None, tr, cols), lambda l, i: (0, l, i, 0))],
        out_specs=pl.BlockSpec((None, tr, cols), lambda l, i: (l, i, 0)),
        out_shape=jax.ShapeDtypeStruct((L, rows, cols), F32),
        compiler_params=_cp(("parallel", "parallel"), 4 * S * tr * cols * 4),
    )(recv)


def _sum_own(recv, own, chip, *, name):
    S, rows, cols = recv.shape
    tr = _row_tile(rows, S * cols * 4, 6 * 2 ** 20)

    def body(me_ref, r_ref, o_ref, out_ref):
        me = me_ref[0]
        acc = None
        for s in range(S):
            t = jnp.where(me == s, o_ref[...], r_ref[s])
            acc = t if acc is None else acc + t
        out_ref[...] = acc

    grid_spec = pltpu.PrefetchScalarGridSpec(
        num_scalar_prefetch=1, grid=(rows // tr,),
        in_specs=[pl.BlockSpec((S, tr, cols), lambda i, me: (0, i, 0)),
                  pl.BlockSpec((None, tr, cols), lambda i, me: (me[0], i, 0))],
        out_specs=pl.BlockSpec((tr, cols), lambda i, me: (i, 0)))
    return pl.pallas_call(
        body, name=name, grid_spec=grid_spec, out_shape=jax.ShapeDtypeStruct((rows, cols), F32),
        compiler_params=_cp(("parallel",), 5 * S * tr * cols * 4),
    )(chip.reshape(1).astype(jnp.int32), recv, own)


def _adamw(w, m, v, ga, gb, *, layer, prev, name):
    L, rows, cols = w.shape
    tr = _row_tile(rows, cols * 4, 2 ** 20)
    two = gb is not None
    nin = 5 if two else 4

    def body(*refs):
        w_ref, m_ref, v_ref, a_ref = refs[:4]
        g_ref, d_ref, m2_ref, v2_ref = refs[-4:]
        g = a_ref[...] + refs[4][...] if two else a_ref[...]
        m2 = ADAM_B1 * m_ref[...] + (1.0 - ADAM_B1) * g
        v2 = ADAM_B2 * v_ref[...] + (1.0 - ADAM_B2) * (g * g)
        m_hat = m2 / (1.0 - ADAM_B1 ** ADAM_STEP)
        v_hat = v2 / (1.0 - ADAM_B2 ** ADAM_STEP)
        g_ref[...] = g
        d_ref[...] = -ADAM_LR * (m_hat / (jnp.sqrt(v_hat) + ADAM_EPS) + ADAM_WD * w_ref[...])
        m2_ref[...] = m2
        v2_ref[...] = v2

    spec = pl.BlockSpec((None, tr, cols), lambda i: (layer, i, 0))
    gspec = pl.BlockSpec((tr, cols), lambda i: (i, 0))
    args = [w, m, v, ga] + ([gb] if two else [])
    in_specs = [spec] * 3 + [gspec] * (nin - 3)
    aliases = {}
    if prev is not None:
        args += list(prev)
        in_specs += [ANY] * 4
        aliases = {nin + j: j for j in range(4)}
    sd = jax.ShapeDtypeStruct((L, rows, cols), F32)
    return pl.pallas_call(
        body, name=name, grid=(rows // tr,), in_specs=in_specs, out_specs=[spec] * 4,
        out_shape=[sd] * 4, input_output_aliases=aliases,
        compiler_params=_cp(("parallel",), 24 * tr * cols * 4),
    )(*args)


def _exchange(arrays, *, flips, n_slots, slot_of, scatter, self_copy, name):
    n = len(arrays)
    nf = len(flips)
    out_shapes = [jax.ShapeDtypeStruct(a.shape if scatter else (n_slots,) + a.shape, a.dtype) for a in arrays]

    def body(*refs):
        ins, outs = refs[:n], refs[n:2 * n]
        send_sems, recv_sems, loc_sems = refs[2 * n:]
        x, y, c = lax.axis_index("x"), lax.axis_index("y"), lax.axis_index("c")
        me = slot_of(x, y, c)
        peers = [(1 - x if fx else x, 1 - y if fy else y, 1 - c if fc else c) for fx, fy, fc in flips]
        local, remote = [], []
        for a in range(n):
            if self_copy:
                lc = pltpu.make_async_copy(ins[a].at[me] if scatter else ins[a], outs[a].at[me], loc_sems.at[a])
                lc.start()
                local.append(lc)
            for k, p in enumerate(peers):
                src = ins[a].at[slot_of(*p)] if scatter else ins[a]
                cp = pltpu.make_async_remote_copy(
                    src_ref=src, dst_ref=outs[a].at[me], send_sem=send_sems.at[a, k], recv_sem=recv_sems.at[a, k],
                    device_id=p, device_id_type=MESH)
                cp.start()
                remote.append(cp)
        for a in range(n):
            for k, p in enumerate(peers):
                src = ins[a].at[slot_of(*p)] if scatter else ins[a]
                pltpu.make_async_remote_copy(
                    src_ref=src, dst_ref=outs[a].at[slot_of(*p)], send_sem=send_sems.at[a, k],
                    recv_sem=recv_sems.at[a, k], device_id=p, device_id_type=MESH).wait_recv()
        for cp in remote:
            cp.wait_send()
        for lc in local:
            lc.wait()

    return pl.pallas_call(
        body, name=name, in_specs=[ANY] * n, out_specs=[ANY] * n, out_shape=out_shapes,
        scratch_shapes=[pltpu.SemaphoreType.DMA((n, nf)), pltpu.SemaphoreType.DMA((n, nf)),
                        pltpu.SemaphoreType.DMA((n,))],
        compiler_params=pltpu.CompilerParams(has_side_effects=True),
    )(*arrays)


CHIP_FLIPS = [(1, 0, 0), (0, 1, 0), (1, 1, 0)]
ALL_FLIPS = [(fx, fy, fc) for fx in (0, 1) for fy in (0, 1) for fc in (0, 1) if fx or fy or fc]


def _chip_slot(x, y, c):
    return 2 * x + y


def _dev_slot(x, y, c):
    return 4 * x + 2 * y + c


def _zero_slot(x, y, c):
    return 0


HBM_SPEC = pl.BlockSpec(memory_space=pltpu.HBM)
SEM_SPEC = pl.BlockSpec(memory_space=pltpu.SEMAPHORE)
N_PEER_CHIPS = 3


def _peer_chips(x, y, c):
    return [(1 - x, y, c), (x, 1 - y, c), (1 - x, 1 - y, c)]


def _send_start(srcs, *, scatter, dep=None, name):
    n = len(srcs)
    nc = n * N_PEER_CHIPS
    srcs = [pltpu.with_memory_space_constraint(s, pltpu.HBM) for s in srcs]
    land_shapes = [s.shape if scatter else (4,) + s.shape for s in srcs]
    lands = [pltpu.with_memory_space_constraint(lax.empty(sh, s.dtype), pltpu.HBM) for sh, s in zip(land_shapes, srcs)]

    deps = [] if dep is None else [dep]
    nd = len(deps)

    def body(*refs):
        ins, lnd = refs[:n], refs[n:2 * n]
        send_sems, recv_sems = refs[2 * n + nd:2 * n + nd + nc], refs[2 * n + nd + nc:2 * n + nd + 2 * nc]
        token = refs[-1]
        x, y, c = lax.axis_index("x"), lax.axis_index("y"), lax.axis_index("c")
        me = 2 * x + y
        for a in range(n):
            for k, p in enumerate(_peer_chips(x, y, c)):
                src = ins[a].at[2 * p[0] + p[1]] if scatter else ins[a]
                j = a * N_PEER_CHIPS + k
                pltpu.make_async_remote_copy(
                    src_ref=src, dst_ref=lnd[a].at[me], send_sem=send_sems[j], recv_sem=recv_sems[j],
                    device_id=p, device_id_type=MESH).start()
        token[...] = jnp.zeros(token.shape, token.dtype)

    sem = pltpu.SemaphoreType.DMA(())
    outs = pl.pallas_call(
        body, name=name,
        out_shape=(*[sem] * (2 * nc), *[pltpu.HBM(s.shape, s.dtype) for s in srcs],
                   *[pltpu.HBM(sh, s.dtype) for sh, s in zip(land_shapes, srcs)],
                   jax.ShapeDtypeStruct((SUBLANES, LANES), F32)),
        in_specs=[HBM_SPEC] * (2 * n) + [ANY] * nd,
        out_specs=(*[SEM_SPEC] * (2 * nc), *[HBM_SPEC] * (2 * n), pl.BlockSpec(memory_space=pltpu.VMEM)),
        input_output_aliases={i: 2 * nc + i for i in range(2 * n)},
        compiler_params=pltpu.CompilerParams(has_side_effects=pltpu.SideEffectType.DATAFLOW_SIDE_EFFECTING),
    )(*srcs, *lands, *deps)
    return dict(sems=list(outs[:2 * nc]), srcs=list(outs[2 * nc:2 * nc + n]),
                lands=list(outs[2 * nc + n:2 * nc + 2 * n]), token=outs[-1])


def _send_wait(h, *, scatter, after, name):
    n = len(h["srcs"])
    nc = n * N_PEER_CHIPS

    def body(*refs):
        ins, lnd = refs[:n], refs[n:2 * n]
        send_sems, recv_sems = refs[2 * n:2 * n + nc], refs[2 * n + nc:2 * n + 2 * nc]
        x, y, c = lax.axis_index("x"), lax.axis_index("y"), lax.axis_index("c")
        for a in range(n):
            for k, p in enumerate(_peer_chips(x, y, c)):
                slot = 2 * p[0] + p[1]
                j = a * N_PEER_CHIPS + k
                cp = pltpu.make_async_remote_copy(
                    src_ref=ins[a].at[slot] if scatter else ins[a], dst_ref=lnd[a].at[slot],
                    send_sem=send_sems[j], recv_sem=recv_sems[j], device_id=p, device_id_type=MESH)
                cp.wait_send()
                cp.wait_recv()

    thru = h["srcs"] + h["lands"]
    outs = pl.pallas_call(
        body, name=name, out_shape=tuple(pltpu.HBM(t.shape, t.dtype) for t in thru),
        in_specs=[HBM_SPEC] * (2 * n) + [SEM_SPEC] * (2 * nc) + [ANY] * len(after),
        out_specs=tuple([HBM_SPEC] * (2 * n)),
        input_output_aliases={i: i for i in range(2 * n)},
        compiler_params=pltpu.CompilerParams(has_side_effects=pltpu.SideEffectType.DATAFLOW_SIDE_EFFECTING),
    )(*thru, *h["sems"], *after)
    return list(outs[:n]), list(outs[n:])


def _assemble(land, own, chip, axis):
    return jnp.concatenate([jnp.where(chip == k, own, land[k]) for k in range(4)], axis=axis)


def _pack(arrs):
    flat = jnp.concatenate([a.reshape(-1).astype(F32) for a in arrs])
    tile = SUBLANES * LANES
    pad = (-flat.shape[0]) % tile
    return jnp.pad(flat, (0, pad)).reshape(-1, LANES)


def _unpack(buf, shapes):
    flat = buf.reshape(-1)
    out, off = [], 0
    for s in shapes:
        n = int(np.prod(s))
        out.append(flat[off:off + n].reshape(s))
        off += n
    return out


def _lower_bounds(hg_lower_bounds):
    p = jax.nn.softmax(hg_lower_bounds.astype(F32), axis=0)
    return jnp.cumsum(p, axis=0) - p[0]


def kernel(x, meta_tokens, hg_lower_bounds, w_in, w_conv, w_pool, pool_scale, hg_norm_g, w_o, ln1_g, ln1_b, w_up, w_ffn_conv, b_ffn_conv, w_down, ln2_g, ln2_b, loss_target, m_meta_tokens, m_hg_lower_bounds, m_w_in, m_w_conv, m_w_pool, m_pool_scale, m_hg_norm_g, m_w_o, m_ln1_g, m_ln1_b, m_w_up, m_w_ffn_conv, m_b_ffn_conv, m_w_down, m_ln2_g, m_ln2_b, v_meta_tokens, v_hg_lower_bounds, v_w_in, v_w_conv, v_w_pool, v_pool_scale, v_hg_norm_g, v_w_o, v_ln1_g, v_ln1_b, v_w_up, v_w_ffn_conv, v_b_ffn_conv, v_w_down, v_ln2_g, v_ln2_b):
    S = x.shape[1]
    R = S + ROW0
    Fq = w_down.shape[1]
    F = 4 * Fq
    F2 = 2 * F
    F2q = w_up.shape[2]
    assert x.shape == (1, S, D) and R % 384 == 0 and S % ROW0 == 0
    chip = 2 * lax.axis_index("x") + lax.axis_index("y")
    tm = 384
    tm_w = max(t for t in range(SUBLANES, 2113, SUBLANES) if R % t == 0)
    tb_ffn = 128

    small_shapes = [(N_META, D // 4), (DEPTH, CW // 4, 3), (DEPTH, F2q, 3)]
    wb_in, wb_o, wb_up, wb_down = (w.astype(MX) for w in (w_in, w_o, w_up, w_down))
    h_a = _send_start([wb_in[0], _pack([meta_tokens, w_conv, w_ffn_conv])], scatter=False, name="gather_a_start")
    (own_in, own_small), (l_in, l_small) = _send_wait(h_a, scatter=False, after=[h_a["token"]], name="gather_a_wait")
    h_b = _send_start([wb_o[0], wb_up[0], wb_down[0]], scatter=False, dep=l_in, name="gather_b_start")
    Win, Wo, Wup, Wdown = {}, {}, {}, {}
    Win[0] = _assemble(l_in, own_in, chip, 1)
    sm = [_unpack(jnp.where(chip == k, own_small, l_small[k]), small_shapes) for k in range(4)]
    meta_full = jnp.concatenate([sm[k][0] for k in range(4)], axis=1)
    wconv_full = jnp.concatenate([sm[k][1] for k in range(4)], axis=1)
    wffn_full = jnp.concatenate([sm[k][2] for k in range(4)], axis=1)
    taps_c = jnp.pad(wconv_full.transpose(0, 2, 1), ((0, 0), (0, SUBLANES - 3), (0, 0)))
    taps_f = jnp.pad(wffn_full.transpose(0, 2, 1), ((0, 0), (0, SUBLANES - 3), (0, 0)))
    wbd = jnp.stack([jax.scipy.linalg.block_diag(*[w_pool[l, g] for g in range(4)]) for l in range(DEPTH)]).astype(MX)
    lbs, lbs_vjp = jax.vjp(_lower_bounds, hg_lower_bounds)
    mat, suf = _hg_consts()

    def fwd_mixer(l, X, dep=None):
        h = _mm(X, Win[l], tm=tm, dep=dep, name=f"mm_in_{l}")
        y, o, sp = _mixer_fwd(h, taps_c[l], wbd[l], pool_scale[l].reshape(1, PW), hg_norm_g[l].reshape(1, HD),
                              lbs[l].reshape(1, HW), mat, name=f"mixer_fwd_{l}")
        return h, y, o, sp

    def fwd_rest(l, X, h, y, o, sp, dep=None):
        x1, xh1, r1 = _mm_ln(y, Wo[l], X, ln1_g[l], ln1_b[l], tm=tm, dep=dep, name=f"mm_o_ln_{l}")
        up, u, a = _mm_ffn_fwd(x1, Wup[l], taps_f[l], b_ffn_conv[l].reshape(1, F2), tm=tm // 2,
                               name=f"mm_up_ffn_{l}")
        x2, xh2, r2 = _mm_ln(a, Wdown[l], x1, ln2_g[l], ln2_b[l], tm=tm, name=f"mm_down_ln_{l}")
        return (X, h, y, o, sp, x1, xh1, r1, up, u, a, xh2, r2), x2

    X = jnp.concatenate([jnp.zeros((PADR, D), F32), meta_full, x[0]], axis=0)
    h, y, o, sp = fwd_mixer(0, X, dep=h_b["token"])
    (own_o, own_up, own_down), (l_o, l_up, l_down) = _send_wait(h_b, scatter=False, after=[y], name="gather_b_wait")
    Wo[0], Wup[0], Wdown[0] = (_assemble(l_o, own_o, chip, 0), _assemble(l_up, own_up, chip, 1),
                               _assemble(l_down, own_down, chip, 0))
    h_c = _send_start([wb_in[1], wb_o[1], wb_up[1], wb_down[1]], scatter=False, dep=l_o, name="gather_c_start")
    saved0, X1 = fwd_rest(0, X, h, y, o, sp, dep=h_c["token"])
    own_c, l_c = _send_wait(h_c, scatter=False, after=[X1], name="gather_c_wait")
    Win[1], Wo[1] = _assemble(l_c[0], own_c[0], chip, 1), _assemble(l_c[1], own_c[1], chip, 0)
    Wup[1], Wdown[1] = _assemble(l_c[2], own_c[2], chip, 1), _assemble(l_c[3], own_c[3], chip, 0)
    saved1, X2 = fwd_rest(1, X1, *fwd_mixer(1, X1))
    saved = [saved0, saved1]

    dxo, lacc = _loss(X2, loss_target[0], name="loss")
    loss = lax.psum(0.5 * jnp.sum(lacc[0]) / D, ("x", "y", "c"))

    sc = {}

    def scatter(nm, l, g, dep=None):
        sc[nm, l] = _send_start([g], scatter=True, dep=dep, name=f"scatter_{nm}_{l}_start")
        return sc[nm, l]["token"]

    tok = None

    small_g = [None] * DEPTH
    for l in reversed(range(DEPTH)):
        X, h, y, o, sp, x1, xh1, r1, up, u, a, xh2, r2 = saved[l]
        dz2, gb2, da = _ln_bwd_mm(dxo, xh2, r2, ln2_g[l], Wdown[l], tm=tm, dep=tok, name=f"ln2_bwd_da_{l}")
        tok = scatter("w_down", l, _wgrad(a, dz2, slabs_on_cols=False, tm=tm_w, tn=512, name=f"wgrad_down_{l}"))
        dup, facc = _ffn_bwd(da, up, u, taps_f[l], tb=tb_ffn, dep=tok, name=f"ffn_bwd_{l}")
        dx1 = _mm(dup, Wup[l], nt=True, res=dz2, tm=tm, name=f"mm_dx1_{l}")
        tok = scatter("w_up", l, _wgrad(x1, dup, slabs_on_cols=True, tm=tm_w, tn=F2q, name=f"wgrad_up_{l}"))
        dz1, gb1, dym = _ln_bwd_mm(dx1, xh1, r1, ln1_g[l], Wo[l], tm=tm, dep=tok, name=f"ln1_bwd_dym_{l}")
        tok = scatter("w_o", l, _wgrad(y, dz1, slabs_on_cols=False, tm=tm_w, tn=512, name=f"wgrad_o_{l}"))
        dh, dwc, dsc, dgn, dlb, dwbd = _mixer_bwd(
            h, dym, o, sp, taps_c[l], wbd[l], pool_scale[l].reshape(1, PW), hg_norm_g[l].reshape(1, HD),
            lbs[l].reshape(1, HW), mat, suf, dep=tok, name=f"mixer_bwd_{l}")
        tok = scatter("w_in", l, _wgrad(X, dh, slabs_on_cols=True, tm=tm_w, tn=DIN // 4, name=f"wgrad_in_{l}"))
        dX = _mm(dh, Win[l], nt=True, res=dz1, tm=tm, dep=tok, name=f"mm_dx_{l}")
        small_g[l] = dict(
            lbs=dlb[0], w_conv=dwc[0:3].T, w_pool=jnp.stack([dwbd[64 * g:64 * g + 64, 64 * g:64 * g + 64] for g in range(4)]),
            pool_scale=dsc[0], hg_norm_g=dgn[0], ln1_g=gb1[0], ln1_b=gb1[1], w_ffn_conv=facc[0:3].T,
            b_ffn_conv=facc[3], ln2_g=gb2[0], ln2_b=gb2[1])
        dxo = dX
    grad_x = dxo[ROW0:][None]

    sg_names = ["lbs", "w_conv", "w_pool", "pool_scale", "hg_norm_g", "ln1_g", "ln1_b", "w_ffn_conv",
                "b_ffn_conv", "ln2_g", "ln2_b"]
    sg_list = [dxo[PADR:ROW0]] + [jnp.stack([small_g[l][nm] for l in range(DEPTH)]) for nm in sg_names]
    sg_shapes = [a.shape for a in sg_list]
    packed = _pack(sg_list)
    (gathered,) = _exchange([packed], flips=ALL_FLIPS, n_slots=8, slot_of=_dev_slot, scatter=False, self_copy=True,
                            name="gather_small_grads")
    total = _sum_slots(gathered[:, None], name="sum_small_grads")[0]
    tot = dict(zip(["meta_tokens"] + sg_names, _unpack(total, sg_shapes)))
    (g_hg,) = lbs_vjp(tot["lbs"])
    small_grads = dict(
        meta_tokens=lax.dynamic_slice_in_dim(tot["meta_tokens"], chip * (D // 4), D // 4, axis=1),
        hg_lower_bounds=g_hg,
        w_conv=lax.dynamic_slice_in_dim(tot["w_conv"], chip * (CW // 4), CW // 4, axis=1),
        w_pool=tot["w_pool"], pool_scale=tot["pool_scale"], hg_norm_g=tot["hg_norm_g"],
        ln1_g=tot["ln1_g"], ln1_b=tot["ln1_b"],
        w_ffn_conv=lax.dynamic_slice_in_dim(tot["w_ffn_conv"], chip * F2q, F2q, axis=1),
        b_ffn_conv=tot["b_ffn_conv"], ln2_g=tot["ln2_g"], ln2_b=tot["ln2_b"])
    small_w = dict(meta_tokens=(meta_tokens, m_meta_tokens, v_meta_tokens),
                   hg_lower_bounds=(hg_lower_bounds, m_hg_lower_bounds, v_hg_lower_bounds),
                   w_conv=(w_conv, m_w_conv, v_w_conv), w_pool=(w_pool, m_w_pool, v_w_pool),
                   pool_scale=(pool_scale, m_pool_scale, v_pool_scale), hg_norm_g=(hg_norm_g, m_hg_norm_g, v_hg_norm_g),
                   ln1_g=(ln1_g, m_ln1_g, v_ln1_g), ln1_b=(ln1_b, m_ln1_b, v_ln1_b),
                   w_ffn_conv=(w_ffn_conv, m_w_ffn_conv, v_w_ffn_conv), b_ffn_conv=(b_ffn_conv, m_b_ffn_conv, v_b_ffn_conv),
                   ln2_g=(ln2_g, m_ln2_g, v_ln2_g), ln2_b=(ln2_b, m_ln2_b, v_ln2_b))
    names_s = list(small_w)
    shapes_s = [small_w[nm][0].shape for nm in names_s]
    pk = [_pack([small_w[nm][j] for nm in names_s])[None] for j in range(3)]
    pg = _pack([small_grads[nm] for nm in names_s])[None]
    outs_s = _adamw(pk[0], pk[1], pk[2], pg[0], None, layer=0, prev=None, name="adamw_small")
    small = {nm: [] for nm in names_s}
    for j in range(4):
        for nm, val in zip(names_s, _unpack(outs_s[j][0], shapes_s)):
            small[nm].append(val)

    big_w = dict(w_in=(w_in, m_w_in, v_w_in), w_o=(w_o, m_w_o, v_w_o), w_up=(w_up, m_w_up, v_w_up),
                 w_down=(w_down, m_w_down, v_w_down))
    big = {nm: None for nm in big_w}
    for l in reversed(range(DEPTH)):
        part = {}
        for nm in ("w_down", "w_up", "w_o", "w_in"):
            after = [dxo] if (nm, l) != ("w_in", 0) else [outs_s[0]] + [big[k][0] for k in big_w] + list(part.values())
            (own,), (recv,) = _send_wait(sc[nm, l], scatter=True, after=after, name=f"scatter_{nm}_{l}_wait")
            part[nm] = _sum_own(recv, own, chip, name=f"sum_{nm}_{l}")
        sib = _exchange([part[nm] for nm in big_w], flips=[(0, 0, 1)], n_slots=1, slot_of=_zero_slot, scatter=False,
                        self_copy=False, name=f"swap_cores_{l}")
        for k, nm in enumerate(big_w):
            w, m, v = big_w[nm]
            big[nm] = _adamw(w, m, v, part[nm], sib[k][0], layer=l, prev=big[nm], name=f"adamw_{nm}_{l}")

    order = ["meta_tokens", "hg_lower_bounds", "w_in", "w_conv", "w_pool", "pool_scale", "hg_norm_g", "w_o",
             "ln1_g", "ln1_b", "w_up", "w_ffn_conv", "b_ffn_conv", "w_down", "ln2_g", "ln2_b"]
    res = {nm: (big[nm] if nm in big else small[nm]) for nm in order}
    outs = [loss, grad_x]
    for j in range(4):
        outs += [res[nm][j] for nm in order]
    return tuple(outs)
```

```python
import functools

import numpy as np

import jax
import jax.numpy as jnp
from jax import lax
from jax.experimental import pallas as pl
from jax.experimental.pallas import tpu as pltpu

F32 = jnp.float32
BF = jnp.bfloat16
MX = jnp.bfloat16

D = 1024
CW = 256
HW = 512
HD = 128
NH = 4
PW = 256
DIN = 3072
N_META = 16
CH = 64
MB = 4
BM = MB * CH
ROW0 = 256
PADR = ROW0 - N_META
HALO = 16
FH = 8
LEVELS = (32, 16, 8, 4, 2, 1)
DEPTH = 2
ALPHA = (2 * DEPTH) ** 0.25
LN_EPS = 1e-5
RMS_EPS = 1e-6
F_FLOOR = 1e-30
SCALE = HD ** -0.5
ADAM_LR, ADAM_B1, ADAM_B2, ADAM_EPS, ADAM_WD, ADAM_STEP = 0.001, 0.9, 0.999, 1e-08, 0.01, 10

VMEM_V7X = 64 * 2 ** 20
LANES = 128
SUBLANES = 8

NN = (((1,), (0,)), ((), ()))
NT = (((1,), (1,)), ((), ()))
TN = (((0,), (0,)), ((), ()))
MESH = pl.DeviceIdType.MESH
ANY = pl.BlockSpec(memory_space=pl.ANY)


def _dot(a, b, dn):
    return lax.dot_general(a.astype(MX), b.astype(MX), dn, preferred_element_type=F32)


def _cp(sem, est_bytes):
    lim = int(min(VMEM_V7X - 6 * 2 ** 20, max(32 * 2 ** 20, est_bytes)))
    return pltpu.CompilerParams(dimension_semantics=sem, vmem_limit_bytes=lim)


def _nbytes(shape, dtype):
    return int(np.prod(shape)) * jnp.dtype(dtype).itemsize


def _row_tile(rows, row_bytes, budget):
    best = SUBLANES
    for t in range(SUBLANES, rows + 1, SUBLANES):
        if rows % t == 0 and t * row_bytes <= budget:
            best = t
    return best


def _after(dep, body, in_specs, args):
    if dep is None:
        return body, list(in_specs), list(args)

    def body_after(dep_ref, *refs):
        body(*refs)

    return body_after, [ANY] + list(in_specs), [dep] + list(args)


def _mm(a, w, *, nt=False, res=None, tm, out_dtype=F32, zero_inert=False, dep=None, name):
    R, K = a.shape
    N = w.shape[0] if nt else w.shape[1]
    dn = NT if nt else NN

    def body(*refs):
        if res is None:
            a_ref, w_ref, o_ref = refs
        else:
            a_ref, w_ref, r_ref, o_ref = refs
        acc = _dot(a_ref[...], w_ref[...], dn)
        if res is not None:
            acc = acc + ALPHA * r_ref[...]
        if zero_inert:
            acc = jnp.where(_row_ids(pl.program_id(0) * tm, tm) >= PADR, acc, 0.0)
        o_ref[...] = acc.astype(out_dtype)

    in_specs = [pl.BlockSpec((tm, K), lambda i: (i, 0)), pl.BlockSpec(w.shape, lambda i: (0, 0))]
    args = [a, w]
    est = 2 * _nbytes((tm, K), a.dtype) + 2 * _nbytes(w.shape, w.dtype) + 3 * _nbytes((tm, N), F32)
    if res is not None:
        in_specs.append(pl.BlockSpec((tm, N), lambda i: (i, 0)))
        args.append(res)
        est += 2 * _nbytes((tm, N), F32)
    body, in_specs, args = _after(dep, body, in_specs, args)
    return pl.pallas_call(
        body, name=name, grid=(R // tm,), in_specs=in_specs,
        out_specs=pl.BlockSpec((tm, N), lambda i: (i, 0)),
        out_shape=jax.ShapeDtypeStruct((R, N), out_dtype),
        compiler_params=_cp(("parallel",), est + 4 * 2 ** 20),
    )(*args)


def _wgrad(a, b, *, slabs_on_cols, tm, tn, name):
    R, Ka = a.shape
    Nb = b.shape[1]
    if slabs_on_cols:
        out_shape = (4, Ka, Nb // 4)
        assert tn == Nb // 4
        out_spec = pl.BlockSpec((None, Ka, tn), lambda j, i: (j, 0, 0))
    else:
        out_shape = (4, Ka // 4, Nb)
        out_spec = pl.BlockSpec((4, Ka // 4, tn), lambda j, i: (0, 0, j))

    def body(a_ref, b_ref, o_ref):
        @pl.when(pl.program_id(1) == 0)
        def _():
            o_ref[...] = jnp.zeros(o_ref.shape, F32)

        acc = _dot(a_ref[...], b_ref[...], TN)
        o_ref[...] += acc.reshape(o_ref.shape)

    in_specs = [pl.BlockSpec((tm, Ka), lambda j, i: (i, 0)), pl.BlockSpec((tm, tn), lambda j, i: (i, j))]
    est = 2 * _nbytes((tm, Ka), a.dtype) + 2 * _nbytes((tm, tn), b.dtype) + 4 * _nbytes((Ka, tn), F32) \
        + _nbytes((tm, Ka), F32)
    return pl.pallas_call(
        body, name=name, grid=(Nb // tn, R // tm), in_specs=in_specs, out_specs=out_spec,
        out_shape=jax.ShapeDtypeStruct(out_shape, F32),
        compiler_params=_cp(("parallel", "arbitrary"), est + 4 * 2 ** 20),
    )(a, b)


def _mm_ln(a, w, xres, g, b, *, tm, dep=None, name):
    R, K = a.shape

    def body(a_ref, w_ref, x_ref, g_ref, b_ref, xo_ref, xh_ref, r_ref):
        z = ALPHA * x_ref[...] + _dot(a_ref[...], w_ref[...], NN)
        mu = jnp.mean(z, axis=-1, keepdims=True)
        zc = z - mu
        var = jnp.mean(zc * zc, axis=-1, keepdims=True)
        r = lax.rsqrt(var + LN_EPS)
        xh = zc * r
        xh_ref[...] = xh
        r_ref[...] = r
        xo_ref[...] = xh * g_ref[...] + b_ref[...]

    row = lambda i: (i, 0)
    fix = lambda i: (0, 0)
    est = 2 * _nbytes((tm, K), a.dtype) + 2 * _nbytes(w.shape, w.dtype) + 10 * _nbytes((tm, D), F32)
    body, in_specs, args = _after(
        dep, body, [pl.BlockSpec((tm, K), row), pl.BlockSpec(w.shape, fix), pl.BlockSpec((tm, D), row),
                    pl.BlockSpec((1, D), fix), pl.BlockSpec((1, D), fix)],
        [a, w, xres, g.reshape(1, D), b.reshape(1, D)])
    return pl.pallas_call(
        body, name=name, grid=(R // tm,), in_specs=in_specs,
        out_specs=[pl.BlockSpec((tm, D), row), pl.BlockSpec((tm, D), row), pl.BlockSpec((tm, 1), row)],
        out_shape=[jax.ShapeDtypeStruct((R, D), F32), jax.ShapeDtypeStruct((R, D), F32),
                   jax.ShapeDtypeStruct((R, 1), F32)],
        compiler_params=_cp(("parallel",), est + 4 * 2 ** 20),
    )(*args)


def _ln_bwd(dx, xh, r, g, *, tm, dep=None, name):
    R = dx.shape[0]

    def body(dx_ref, xh_ref, r_ref, g_ref, dz_ref, gb_ref):
        @pl.when(pl.program_id(0) == 0)
        def _():
            gb_ref[...] = jnp.zeros(gb_ref.shape, F32)

        dxv = dx_ref[...]
        xhv = xh_ref[...]
        dyh = dxv * g_ref[...]
        m1 = jnp.mean(dyh, axis=-1, keepdims=True)
        m2 = jnp.mean(dyh * xhv, axis=-1, keepdims=True)
        dz_ref[...] = r_ref[...] * (dyh - m1 - xhv * m2)
        gb_ref[0:1, :] += jnp.sum(dxv * xhv, axis=0, keepdims=True)
        gb_ref[1:2, :] += jnp.sum(dxv, axis=0, keepdims=True)

    row = lambda i: (i, 0)
    fix = lambda i: (0, 0)
    body, in_specs, args = _after(
        dep, body, [pl.BlockSpec((tm, D), row), pl.BlockSpec((tm, D), row), pl.BlockSpec((tm, 1), row),
                    pl.BlockSpec((1, D), fix)], [dx, xh, r, g.reshape(1, D)])
    return pl.pallas_call(
        body, name=name, grid=(R // tm,), in_specs=in_specs,
        out_specs=[pl.BlockSpec((tm, D), row), pl.BlockSpec((SUBLANES, D), fix)],
        out_shape=[jax.ShapeDtypeStruct((R, D), F32), jax.ShapeDtypeStruct((SUBLANES, D), F32)],
        compiler_params=_cp(("arbitrary",), 12 * _nbytes((tm, D), F32)),
    )(*args)


def _resident(shape):
    return pl.BlockSpec(shape, lambda i: (0,) * len(shape), pipeline_mode=pl.Buffered(1))


def _ln_bwd_mm(dx, xh, r, g, w, *, tm, dep=None, name):
    R = dx.shape[0]
    N = w.shape[0]

    def body(dx_ref, xh_ref, r_ref, g_ref, w_ref, dz_ref, gb_ref, o_ref):
        @pl.when(pl.program_id(0) == 0)
        def _():
            gb_ref[...] = jnp.zeros(gb_ref.shape, F32)

        dxv = dx_ref[...]
        xhv = xh_ref[...]
        dyh = dxv * g_ref[...]
        m1 = jnp.mean(dyh, axis=-1, keepdims=True)
        m2 = jnp.mean(dyh * xhv, axis=-1, keepdims=True)
        dz = r_ref[...] * (dyh - m1 - xhv * m2)
        dz_ref[...] = dz
        gb_ref[0:1, :] += jnp.sum(dxv * xhv, axis=0, keepdims=True)
        gb_ref[1:2, :] += jnp.sum(dxv, axis=0, keepdims=True)
        o_ref[...] = _dot(dz, w_ref[...], NT)

    row = lambda i: (i, 0)
    fix = lambda i: (0, 0)
    body, in_specs, args = _after(
        dep, body, [pl.BlockSpec((tm, D), row), pl.BlockSpec((tm, D), row), pl.BlockSpec((tm, 1), row),
                    pl.BlockSpec((1, D), fix), _resident(w.shape)], [dx, xh, r, g.reshape(1, D), w])
    est = _nbytes(w.shape, w.dtype) + 14 * _nbytes((tm, D), F32) + 4 * _nbytes((tm, N), F32)
    return pl.pallas_call(
        body, name=name, grid=(R // tm,), in_specs=in_specs,
        out_specs=[pl.BlockSpec((tm, D), row), pl.BlockSpec((SUBLANES, D), fix), pl.BlockSpec((tm, N), row)],
        out_shape=[jax.ShapeDtypeStruct((R, D), F32), jax.ShapeDtypeStruct((SUBLANES, D), F32),
                   jax.ShapeDtypeStruct((R, N), F32)],
        compiler_params=_cp(("arbitrary",), est + 4 * 2 ** 20),
    )(*args)


def _mm_ffn_fwd(x1, w, taps, bias, *, tm, name):
    R, K = x1.shape
    F2 = w.shape[1]
    F = F2 // 2

    def body(x_ref, w_ref, t_ref, b_ref, up_ref, u_ref, a_ref, carry_ref):
        i = pl.program_id(0)

        @pl.when(i == 0)
        def _():
            carry_ref[...] = jnp.zeros(carry_ref.shape, F32)

        acc = _dot(x_ref[...], w_ref[...], NN)
        acc = jnp.where(_row_ids(i * tm, tm) >= PADR, acc, 0.0)
        up_ref[...] = acc.astype(up_ref.dtype)
        ue = jnp.concatenate([carry_ref[...], acc], axis=0)
        carry_ref[...] = acc[tm - FH:tm]
        u = (t_ref[0:1, :] * pltpu.roll(ue, 2, 0)[FH:] + t_ref[1:2, :] * pltpu.roll(ue, 1, 0)[FH:]
             + t_ref[2:3, :] * acc + b_ref[...])
        u_ref[...] = u
        gate = u[:, :F]
        val = u[:, F:]
        a_ref[...] = (gate * jax.nn.sigmoid(gate) * val).astype(a_ref.dtype)

    row = lambda i: (i, 0)
    est = _nbytes(w.shape, w.dtype) + 2 * _nbytes((tm, K), x1.dtype) + 10 * _nbytes((tm + FH, F2), F32)
    return pl.pallas_call(
        body, name=name, grid=(R // tm,),
        in_specs=[pl.BlockSpec((tm, K), row), _resident(w.shape), _resident((SUBLANES, F2)), _resident((1, F2))],
        out_specs=[pl.BlockSpec((tm, F2), row), pl.BlockSpec((tm, F2), row), pl.BlockSpec((tm, F), row)],
        out_shape=[jax.ShapeDtypeStruct((R, F2), MX), jax.ShapeDtypeStruct((R, F2), F32),
                   jax.ShapeDtypeStruct((R, F), MX)],
        scratch_shapes=[pltpu.VMEM((FH, F2), F32)],
        compiler_params=_cp(("arbitrary",), est + 4 * 2 ** 20),
    )(x1, w, taps, bias)


def _loss(x2, tgt, *, name):
    R = x2.shape[0]
    tb = ROW0

    def body(x_ref, t_ref, dy_ref, acc_ref):
        i = pl.program_id(0)

        @pl.when(i == 0)
        def _():
            acc_ref[...] = jnp.zeros(acc_ref.shape, F32)
            dy_ref[...] = jnp.zeros(dy_ref.shape, F32)

        @pl.when(i > 0)
        def _():
            err = x_ref[...] - t_ref[...]
            dy_ref[...] = err / D
            acc_ref[0:1, :] += jnp.sum(err * err, axis=0, keepdims=True)

    return pl.pallas_call(
        body, name=name, grid=(R // tb,),
        in_specs=[pl.BlockSpec((tb, D), lambda i: (i, 0)),
                  pl.BlockSpec((tb, D), lambda i: (jnp.maximum(i - 1, 0), 0))],
        out_specs=[pl.BlockSpec((tb, D), lambda i: (i, 0)), pl.BlockSpec((SUBLANES, D), lambda i: (0, 0))],
        out_shape=[jax.ShapeDtypeStruct((R, D), F32), jax.ShapeDtypeStruct((SUBLANES, D), F32)],
        compiler_params=_cp(("arbitrary",), 8 * _nbytes((tb, D), F32)),
    )(x2, tgt)


def _row_ids(start, n):
    return start + lax.broadcasted_iota(jnp.int32, (n, 1), 0)


def _ffn_fwd(up, taps, bias, *, tb, name):
    R, F2 = up.shape
    F = F2 // 2
    nh = tb // FH

    def body(u_ref, up_ref, w_ref, b_ref, a_ref):
        ue = jnp.concatenate([up_ref[...], u_ref[...]], axis=0)
        u = (w_ref[0:1, :] * pltpu.roll(ue, 2, 0)[FH:] + w_ref[1:2, :] * pltpu.roll(ue, 1, 0)[FH:]
             + w_ref[2:3, :] * u_ref[...] + b_ref[...])
        gate = u[:, :F]
        val = u[:, F:]
        a_ref[...] = (gate * jax.nn.sigmoid(gate) * val).astype(a_ref.dtype)

    est = 2 * _nbytes((tb, F2), F32) * 4
    return pl.pallas_call(
        body, name=name, grid=(R // tb,),
        in_specs=[pl.BlockSpec((tb, F2), lambda i: (i, 0)),
                  pl.BlockSpec((FH, F2), lambda i: (jnp.maximum(i * nh - 1, 0), 0)),
                  pl.BlockSpec((SUBLANES, F2), lambda i: (0, 0)), pl.BlockSpec((1, F2), lambda i: (0, 0))],
        out_specs=pl.BlockSpec((tb, F), lambda i: (i, 0)),
        out_shape=jax.ShapeDtypeStruct((R, F), MX),
        compiler_params=_cp(("parallel",), est),
    )(up, up, taps, bias)


def _ffn_bwd(da, up, u, taps, *, tb, dep=None, name):
    R, F2 = up.shape
    F = F2 // 2
    nh = tb // FH
    nb = R // tb
    last = R // FH - 1
    m = tb + FH

    def body(da_ref, dan_ref, x_ref, u_ref, un_ref, w_ref, du_ref, acc_ref):
        i = pl.program_id(0)

        @pl.when(i == 0)
        def _():
            acc_ref[...] = jnp.zeros(acc_ref.shape, F32)

        inside = (i < nb - 1).astype(F32)
        w0, w1, w2 = w_ref[0:1, :], w_ref[1:2, :], w_ref[2:3, :]
        u = jnp.concatenate([u_ref[...], un_ref[...]], axis=0)
        dae = jnp.concatenate([da_ref[...], dan_ref[...] * inside], axis=0)
        gate = u[:, :F]
        val = u[:, F:]
        sg = jax.nn.sigmoid(gate)
        gs = gate * sg
        du = jnp.concatenate([dae * val * (sg * (1.0 + gate - gs)), dae * gs], axis=1)
        du0 = du[0:tb]
        du1 = pltpu.roll(du, m - 1, 0)[0:tb]
        du2 = pltpu.roll(du, m - 2, 0)[0:tb]
        du_ref[...] = (w2 * du0 + w1 * du1 + w0 * du2).astype(du_ref.dtype)

        @pl.when(i * tb < PADR)
        def _():
            row = _row_ids(i * tb, tb)
            du_ref[...] = jnp.where(row >= PADR, du_ref[...], jnp.zeros((), du_ref.dtype))

        x = x_ref[...].astype(F32)
        acc_ref[0:1, :] += jnp.sum(du2 * x, axis=0, keepdims=True)
        acc_ref[1:2, :] += jnp.sum(du1 * x, axis=0, keepdims=True)
        acc_ref[2:3, :] += jnp.sum(du0 * x, axis=0, keepdims=True)
        acc_ref[3:4, :] += jnp.sum(du0, axis=0, keepdims=True)

    nxt = lambda i: (jnp.minimum(i * nh + nh, last), 0)
    cur = lambda i: (i, 0)
    fix = lambda i: (0, 0)
    est = 12 * _nbytes((tb + 2 * FH, F2), F32)
    body, in_specs, args = _after(
        dep, body, [pl.BlockSpec((tb, F), cur), pl.BlockSpec((FH, F), nxt),
                    pl.BlockSpec((tb, F2), cur), pl.BlockSpec((tb, F2), cur), pl.BlockSpec((FH, F2), nxt),
                    pl.BlockSpec((SUBLANES, F2), fix)], [da, da, up, u, u, taps])
    return pl.pallas_call(
        body, name=name, grid=(R // tb,), in_specs=in_specs,
        out_specs=[pl.BlockSpec((tb, F2), cur), pl.BlockSpec((SUBLANES, F2), fix)],
        out_shape=[jax.ShapeDtypeStruct((R, F2), MX), jax.ShapeDtypeStruct((SUBLANES, F2), F32)],
        compiler_params=_cp(("arbitrary",), est),
    )(*args)


def _hg_consts():
    t = np.arange(CH)[:, None]
    j = np.arange(CH)[None, :]
    low = (j <= t).astype(np.float32)
    blocks = [low, (j > t).astype(np.float32)]
    for m in LEVELS:
        ref = (t // (2 * m)) * 2 * m + m - 1
        blocks.append(low - (j <= ref).astype(np.float32))
    mat = np.concatenate(blocks, axis=0)
    suf = (j >= t).astype(np.float32)
    return jnp.asarray(mat, BF), jnp.asarray(suf, BF)


def _split_dot(mat, x):
    hi = x.astype(BF)
    lo = (x - hi.astype(F32)).astype(BF)
    return (lax.dot_general(mat, hi, NN, preferred_element_type=F32)
            + lax.dot_general(mat, lo, NN, preferred_element_type=F32))


def _lane_select(a2, a4, a8, a16):
    lane = lax.broadcasted_iota(jnp.int32, (1, PW), 1)
    return jnp.where(lane < 64, a2, jnp.where(lane < 128, a4, jnp.where(lane < 192, a8, a16)))


def _pool_count(row):
    win = _lane_select(2.0, 4.0, 8.0, 16.0)
    t1 = jnp.maximum((row - PADR + 1).astype(F32), 1.0)
    return jnp.minimum(t1, win)


def _gates(fz, lb, valid):
    sig = jax.nn.sigmoid(fz)
    f = lb + (1.0 - lb) * sig
    lf = jnp.where(valid, jnp.log(jnp.maximum(f, F_FLOOR)), 0.0)
    kk = jnp.where(valid, (1.0 - lb) * (1.0 - sig), 0.0)
    return sig, f, lf, kk


def _level_masks():
    tt = lax.broadcasted_iota(jnp.int32, (CH, CH), 0)
    ss = lax.broadcasted_iota(jnp.int32, (CH, CH), 1)
    xr = tt ^ ss
    low = tt > ss
    return tt, ss, [(xr >= m) & (xr < 2 * m) & low for m in LEVELS]


def _intra(q, kk, ex, sl, tt, ss, masks):
    a = jnp.where(tt == ss, jnp.sum(q * kk, axis=-1, keepdims=True), 0.0)
    parts = []
    for l in range(len(LEVELS)):
        e = jnp.exp(-jnp.abs(ex[(2 + l) * CH:(3 + l) * CH, sl]))
        ql = (q * e).astype(MX)
        kl = (kk * e).astype(MX)
        a = a + jnp.where(masks[l], _dot(ql, kl, NT), 0.0)
        parts.append((e, ql, kl))
    return a, parts


def _mixer_fwd(h, taps, wbd, ps, gn, lb, mat, *, name):
    R = h.shape[0]
    nb = R // BM
    q4 = BM // HALO

    def body(h_ref, hp_ref, w_ref, wbd_ref, ps_ref, gn_ref, lb_ref, mat_ref, y_ref, o_ref, sp_ref, st_ref):
        i = pl.program_id(0)

        @pl.when(i == 0)
        def _():
            st_ref[...] = jnp.zeros(st_ref.shape, F32)

        row = _row_ids(i * BM, BM)
        valid_b = row >= PADR
        rowe = _row_ids(i * BM - HALO, BM + HALO)
        valide = rowe >= PADR
        se = jnp.concatenate([hp_ref[:, 256:512] * hp_ref[:, 512:768], h_ref[:, 256:512] * h_ref[:, 512:768]], axis=0)
        se = jnp.where(valide, se, 0.0)
        conv = (w_ref[0:1, :] * pltpu.roll(se, 2, 0) + w_ref[1:2, :] * pltpu.roll(se, 1, 0)
                + w_ref[2:3, :] * se)[HALO:]
        y_ref[:, 0:CW] = (h_ref[:, 0:256] * conv).astype(y_ref.dtype)
        ve = jnp.where(valide, jnp.concatenate([hp_ref[:, 2816:3072], h_ref[:, 2816:3072]], axis=0), 0.0)
        s2 = ve + pltpu.roll(ve, 1, 0)
        s4 = s2 + pltpu.roll(s2, 2, 0)
        s8 = s4 + pltpu.roll(s4, 4, 0)
        s16 = s8 + pltpu.roll(s8, 8, 0)
        dp = (_lane_select(s2, s4, s8, s16) / _pool_count(rowe) - ve)[HALO:]
        y_ref[:, CW + HW:D] = (_dot(dp, wbd_ref[...], NN) * ps_ref[...]).astype(y_ref.dtype)
        lbv = lb_ref[...]
        tt, ss, masks = _level_masks()
        for c in range(MB):
            rs = slice(c * CH, (c + 1) * CH)
            _, _, lf, kk = _gates(h_ref[rs, 1280:1792], lbv, valid_b[rs])
            q = h_ref[rs, 768:1280] * SCALE
            ii = h_ref[rs, 1792:2304]
            gz = h_ref[rs, 2304:2816]
            ex = _split_dot(mat_ref[...], lf)
            eg = jnp.exp(ex[0:CH])
            q0 = q * eg
            kr = kk * jnp.exp(ex[CH:2 * CH])
            for hd in range(NH):
                sl = slice(HD * hd, HD * (hd + 1))
                st = st_ref[hd]
                sp_ref[c, hd] = st
                a, _ = _intra(q[:, sl], kk[:, sl], ex, sl, tt, ss, masks)
                o = _dot(q0[:, sl], st, NT) + _dot(a, ii[:, sl], NN)
                st_ref[hd] = st * eg[CH - 1:CH, sl] + _dot(ii[:, sl], kr[:, sl], TN)
                o_ref[rs, sl] = o
                r = lax.rsqrt(jnp.mean(o * o, axis=-1, keepdims=True) + RMS_EPS)
                g_ = gz[:, sl]
                y_ref[rs, CW + HD * hd:CW + HD * (hd + 1)] = (
                    o * r * gn_ref[...] * (g_ * jax.nn.sigmoid(g_))).astype(y_ref.dtype)

    fix = lambda i: (0, 0)
    return pl.pallas_call(
        body, name=name, grid=(nb,),
        in_specs=[pl.BlockSpec((BM, DIN), lambda i: (i, 0)),
                  pl.BlockSpec((HALO, DIN), lambda i: (jnp.maximum(i * q4 - 1, 0), 0)),
                  pl.BlockSpec((SUBLANES, CW), fix), pl.BlockSpec((PW, PW), fix), pl.BlockSpec((1, PW), fix),
                  pl.BlockSpec((1, HD), fix), pl.BlockSpec((1, HW), fix), pl.BlockSpec(mat.shape, fix)],
        out_specs=[pl.BlockSpec((BM, D), lambda i: (i, 0)), pl.BlockSpec((BM, HW), lambda i: (i, 0)),
                   pl.BlockSpec((MB, NH, HD, HD), lambda i: (i, 0, 0, 0))],
        out_shape=[jax.ShapeDtypeStruct((R, D), MX), jax.ShapeDtypeStruct((R, HW), F32),
                   jax.ShapeDtypeStruct((R // CH, NH, HD, HD), F32)],
        scratch_shapes=[pltpu.VMEM((NH, HD, HD), F32)],
        compiler_params=_cp(("arbitrary",), 32 * 2 ** 20),
    )(h, h, taps, wbd, ps, gn, lb, mat)


def _mixer_bwd(h, dy, o, sp, taps, wbd, ps, gn, lb, mat, suf, *, dep=None, name):
    R = h.shape[0]
    nb = R // BM
    q4 = BM // HALO
    lasth = R // HALO - 1

    def body(h_ref, hp_ref, hn_ref, dy_ref, dyn_ref, o_ref, sp_ref, w_ref, wbd_ref, ps_ref, gn_ref, lb_ref,
             mat_ref, suf_ref, dh_ref, dw_ref, dsc_ref, dgn_ref, dlb_ref, dwbd_ref, dst_ref):
        i = pl.program_id(0)
        b = nb - 1 - i

        @pl.when(i == 0)
        def _():
            dst_ref[...] = jnp.zeros(dst_ref.shape, F32)
            dw_ref[...] = jnp.zeros(dw_ref.shape, F32)
            dsc_ref[...] = jnp.zeros(dsc_ref.shape, F32)
            dgn_ref[...] = jnp.zeros(dgn_ref.shape, F32)
            dlb_ref[...] = jnp.zeros(dlb_ref.shape, F32)
            dwbd_ref[...] = jnp.zeros(dwbd_ref.shape, F32)

        ne = BM + 2 * HALO
        nc = BM + HALO
        row = _row_ids(b * BM, BM)
        valid = row >= PADR
        rowe = _row_ids(b * BM - HALO, ne)
        valide = (rowe >= PADR) & (rowe < R)
        rown = rowe[HALO:]
        validn = rown < R

        def cat3(lo, hi):
            return jnp.concatenate([hp_ref[:, lo:hi], h_ref[:, lo:hi], hn_ref[:, lo:hi]], axis=0)

        def catd(lo, hi):
            return jnp.where(validn, jnp.concatenate([dy_ref[:, lo:hi], dyn_ref[:, lo:hi]], axis=0), 0.0)

        w0, w1, w2 = w_ref[0:1, :], w_ref[1:2, :], w_ref[2:3, :]
        cce = cat3(256, 512)
        cve = cat3(512, 768)
        se = jnp.where(valide, cce * cve, 0.0)
        sm2 = pltpu.roll(se, 2, 0)
        sm1 = pltpu.roll(se, 1, 0)
        conv = (w0 * sm2 + w1 * sm1 + w2 * se)[HALO:HALO + BM]
        cbn = jnp.concatenate([h_ref[:, 0:256], hn_ref[:, 0:256]], axis=0)
        dconv = catd(0, CW) * cbn
        ds = w2 * dconv + w1 * pltpu.roll(dconv, nc - 1, 0) + w0 * pltpu.roll(dconv, nc - 2, 0)
        ds = jnp.where(valid, ds[0:BM], 0.0)
        dcv = dconv[0:BM]
        dw_ref[0:1, :] += jnp.sum(dcv * sm2[HALO:HALO + BM], axis=0, keepdims=True)
        dw_ref[1:2, :] += jnp.sum(dcv * sm1[HALO:HALO + BM], axis=0, keepdims=True)
        dw_ref[2:3, :] += jnp.sum(dcv * se[HALO:HALO + BM], axis=0, keepdims=True)
        dh_ref[:, 0:256] = jnp.where(valid, dy_ref[:, 0:CW] * conv, 0.0).astype(dh_ref.dtype)
        dh_ref[:, 256:512] = (ds * h_ref[:, 512:768]).astype(dh_ref.dtype)
        dh_ref[:, 512:768] = (ds * h_ref[:, 256:512]).astype(dh_ref.dtype)

        ve = jnp.where(valide, cat3(2816, 3072), 0.0)
        s2 = ve + pltpu.roll(ve, 1, 0)
        s4 = s2 + pltpu.roll(s2, 2, 0)
        s8 = s4 + pltpu.roll(s4, 4, 0)
        s16 = s8 + pltpu.roll(s8, 8, 0)
        cnte = _pool_count(rowe)
        dp = (_lane_select(s2, s4, s8, s16) / cnte - ve)[HALO:HALO + BM]
        dyp = catd(CW + HW, D)
        pre = _dot(dp, wbd_ref[...], NN)
        dsc_ref[0:1, :] += jnp.sum(dyp[0:BM] * pre, axis=0, keepdims=True)
        dyps = dyp * ps_ref[...]
        dd = _dot(dyps, wbd_ref[...], NT)
        dwbd_ref[...] += _dot(dp, dyps[0:BM], TN)
        e = dd / cnte[HALO:]
        t2 = e + pltpu.roll(e, nc - 1, 0)
        t4 = t2 + pltpu.roll(t2, nc - 2, 0)
        t8 = t4 + pltpu.roll(t4, nc - 4, 0)
        t16 = t8 + pltpu.roll(t8, nc - 8, 0)
        dv = (_lane_select(t2, t4, t8, t16) - dd)[0:BM]
        dh_ref[:, 2816:3072] = jnp.where(valid, dv, 0.0).astype(dh_ref.dtype)

        lbv = lb_ref[...]
        tt, ss, masks = _level_masks()
        gnv = gn_ref[...]
        for c in reversed(range(MB)):
            rs = slice(c * CH, (c + 1) * CH)
            vc = valid[rs]
            sig, f, lf, kk = _gates(h_ref[rs, 1280:1792], lbv, vc)
            q = h_ref[rs, 768:1280] * SCALE
            ii = h_ref[rs, 1792:2304]
            gz = h_ref[rs, 2304:2816]
            ex = _split_dot(mat_ref[...], lf)
            eg = jnp.exp(ex[0:CH])
            egr = jnp.exp(ex[CH:2 * CH])
            q0 = q * eg
            kr = kk * egr
            dqs, dks, dis, dgzs, tails, dbs = [], [], [], [], [], []
            for hd in range(NH):
                sl = slice(HD * hd, HD * (hd + 1))
                ov = o_ref[rs, sl]
                r = lax.rsqrt(jnp.mean(ov * ov, axis=-1, keepdims=True) + RMS_EPS)
                oh = ov * r
                g_ = gz[:, sl]
                sg = jax.nn.sigmoid(g_)
                dyv = dy_ref[rs, CW + HD * hd:CW + HD * (hd + 1)]
                don = dyv * (g_ * sg)
                dgzs.append(dyv * (oh * gnv) * (sg * (1.0 + g_ * (1.0 - sg))))
                dgn_ref[0:1, :] += jnp.sum(don * oh, axis=0, keepdims=True)
                doh = don * gnv
                do = r * (doh - oh * jnp.mean(doh * oh, axis=-1, keepdims=True))
                st = sp_ref[c, hd]
                dst = dst_ref[hd]
                qh, kh, ih = q[:, sl], kk[:, sl], ii[:, sl]
                a, parts = _intra(qh, kh, ex, sl, tt, ss, masks)
                da = jnp.where(tt >= ss, _dot(do, ih, NT), 0.0)
                dis.append(_dot(a, do, TN) + _dot(kr[:, sl], dst, NT))
                q0h = q0[:, sl].astype(MX)
                krh = kr[:, sl].astype(MX)
                dq0 = _dot(do, st, NN)
                dkr = _dot(ih, dst, NN)
                dq = dq0 * eg[:, sl]
                dk = dkr * egr[:, sl]
                kdk = krh.astype(F32) * dkr
                db = q0h.astype(F32) * dq0 - kdk
                tails.append(jnp.sum(kdk, axis=0, keepdims=True)
                             + eg[CH - 1:CH, sl] * jnp.sum(dst * st, axis=0, keepdims=True))
                dga = jnp.sum(jnp.where(tt == ss, da, 0.0), axis=-1, keepdims=True)
                dq = dq + dga * kh
                dk = dk + dga * qh
                for l in range(len(LEVELS)):
                    e_l, ql, kl = parts[l]
                    dpl = jnp.where(masks[l], da, 0.0).astype(MX)
                    dql = _dot(dpl, kl, NN)
                    dkl = _dot(dpl, ql, TN)
                    dq = dq + dql * e_l
                    dk = dk + dkl * e_l
                    db = db + (ql.astype(F32) * dql - kl.astype(F32) * dkl)
                dst_ref[hd] = dst * eg[CH - 1:CH, sl] + _dot(do, q0h, TN)
                dqs.append(dq)
                dks.append(dk)
                dbs.append(db)
            dq = jnp.concatenate(dqs, axis=1)
            dk = jnp.concatenate(dks, axis=1)
            db = jnp.concatenate(dbs, axis=1)
            dlf = _split_dot(suf_ref[...], db) + jnp.concatenate(tails, axis=1)
            t = jnp.where(vc, dlf * jnp.where(f > F_FLOOR, 1.0 / f, 0.0) - dk, 0.0)
            dlb_ref[0:1, :] += jnp.sum(t * (1.0 - sig), axis=0, keepdims=True)
            dh_ref[rs, 768:1280] = jnp.where(vc, dq * SCALE, 0.0).astype(dh_ref.dtype)
            dh_ref[rs, 1280:1792] = (t * (1.0 - lbv) * (sig * (1.0 - sig))).astype(dh_ref.dtype)
            dh_ref[rs, 1792:2304] = jnp.where(vc, jnp.concatenate(dis, axis=1), 0.0).astype(dh_ref.dtype)
            dh_ref[rs, 2304:2816] = jnp.where(vc, jnp.concatenate(dgzs, axis=1), 0.0).astype(dh_ref.dtype)

    cur = lambda i: (nb - 1 - i, 0)
    prev = lambda i: (jnp.maximum((nb - 1 - i) * q4 - 1, 0), 0)
    nxt = lambda i: (jnp.minimum((nb - 1 - i) * q4 + q4, lasth), 0)
    fix = lambda i: (0, 0)
    body, in_specs, args = _after(
        dep, body,
        [pl.BlockSpec((BM, DIN), cur), pl.BlockSpec((HALO, DIN), prev), pl.BlockSpec((HALO, DIN), nxt),
         pl.BlockSpec((BM, D), cur), pl.BlockSpec((HALO, D), nxt), pl.BlockSpec((BM, HW), cur),
         pl.BlockSpec((MB, NH, HD, HD), lambda i: (nb - 1 - i, 0, 0, 0)),
         pl.BlockSpec((SUBLANES, CW), fix), pl.BlockSpec((PW, PW), fix), pl.BlockSpec((1, PW), fix),
         pl.BlockSpec((1, HD), fix), pl.BlockSpec((1, HW), fix), pl.BlockSpec(mat.shape, fix),
         pl.BlockSpec(suf.shape, fix)],
        [h, h, h, dy, dy, o, sp, taps, wbd, ps, gn, lb, mat, suf])
    return pl.pallas_call(
        body, name=name, grid=(nb,), in_specs=in_specs,
        out_specs=[pl.BlockSpec((BM, DIN), cur), pl.BlockSpec((SUBLANES, CW), fix), pl.BlockSpec((SUBLANES, PW), fix),
                   pl.BlockSpec((SUBLANES, HD), fix), pl.BlockSpec((SUBLANES, HW), fix), pl.BlockSpec((PW, PW), fix)],
        out_shape=[jax.ShapeDtypeStruct((R, DIN), MX), jax.ShapeDtypeStruct((SUBLANES, CW), F32),
                   jax.ShapeDtypeStruct((SUBLANES, PW), F32), jax.ShapeDtypeStruct((SUBLANES, HD), F32),
                   jax.ShapeDtypeStruct((SUBLANES, HW), F32), jax.ShapeDtypeStruct((PW, PW), F32)],
        scratch_shapes=[pltpu.VMEM((NH, HD, HD), F32)],
        compiler_params=_cp(("arbitrary",), 40 * 2 ** 20),
    )(*args)


def _sum_slots(recv, *, name):
    S, L, rows, cols = recv.shape
    tr = _row_tile(rows, S * cols * 4, 6 * 2 ** 20)

    def body(r_ref, o_ref):
        acc = r_ref[0]
        for s in range(1, S):
            acc = acc + r_ref[s]
        o_ref[...] = acc

    return pl.pallas_call(
        body, name=name, grid=(L, rows // tr),
        in_specs=[pl.BlockSpec((S, None, tr, cols), lambda l, i: (0, l, i, 0))],
        out_specs=pl.BlockSpec((None, tr, cols), lambda l, i: (l, i, 0)),
        out_shape=jax.ShapeDtypeStruct((L, rows, cols), F32),
        compiler_params=_cp(("parallel", "parallel"), 4 * S * tr * cols * 4),
    )(recv)


def _sum_own(recv, own, chip, *, name):
    S, rows, cols = recv.shape
    tr = _row_tile(rows, S * cols * 4, 6 * 2 ** 20)

    def body(me_ref, r_ref, o_ref, out_ref):
        me = me_ref[0]
        acc = None
        for s in range(S):
            t = jnp.where(me == s, o_ref[...], r_ref[s])
            acc = t if acc is None else acc + t
        out_ref[...] = acc

    grid_spec = pltpu.PrefetchScalarGridSpec(
        num_scalar_prefetch=1, grid=(rows // tr,),
        in_specs=[pl.BlockSpec((S, tr, cols), lambda i, me: (0, i, 0)),
                  pl.BlockSpec((None, tr, cols), lambda i, me: (me[0], i, 0))],
        out_specs=pl.BlockSpec((tr, cols), lambda i, me: (i, 0)))
    return pl.pallas_call(
        body, name=name, grid_spec=grid_spec, out_shape=jax.ShapeDtypeStruct((rows, cols), F32),
        compiler_params=_cp(("parallel",), 5 * S * tr * cols * 4),
    )(chip.reshape(1).astype(jnp.int32), recv, own)


def _adamw(w, m, v, ga, gb, *, layer, prev, name):
    L, rows, cols = w.shape
    tr = _row_tile(rows, cols * 4, 2 ** 20)
    two = gb is not None
    nin = 5 if two else 4

    def body(*refs):
        w_ref, m_ref, v_ref, a_ref = refs[:4]
        g_ref, d_ref, m2_ref, v2_ref = refs[-4:]
        g = a_ref[...] + refs[4][...] if two else a_ref[...]
        m2 = ADAM_B1 * m_ref[...] + (1.0 - ADAM_B1) * g
        v2 = ADAM_B2 * v_ref[...] + (1.0 - ADAM_B2) * (g * g)
        m_hat = m2 / (1.0 - ADAM_B1 ** ADAM_STEP)
        v_hat = v2 / (1.0 - ADAM_B2 ** ADAM_STEP)
        g_ref[...] = g
        d_ref[...] = -ADAM_LR * (m_hat / (jnp.sqrt(v_hat) + ADAM_EPS) + ADAM_WD * w_ref[...])
        m2_ref[...] = m2
        v2_ref[...] = v2

    spec = pl.BlockSpec((None, tr, cols), lambda i: (layer, i, 0))
    gspec = pl.BlockSpec((tr, cols), lambda i: (i, 0))
    args = [w, m, v, ga] + ([gb] if two else [])
    in_specs = [spec] * 3 + [gspec] * (nin - 3)
    aliases = {}
    if prev is not None:
        args += list(prev)
        in_specs += [ANY] * 4
        aliases = {nin + j: j for j in range(4)}
    sd = jax.ShapeDtypeStruct((L, rows, cols), F32)
    return pl.pallas_call(
        body, name=name, grid=(rows // tr,), in_specs=in_specs, out_specs=[spec] * 4,
        out_shape=[sd] * 4, input_output_aliases=aliases,
        compiler_params=_cp(("parallel",), 24 * tr * cols * 4),
    )(*args)


def _exchange(arrays, *, flips, n_slots, slot_of, scatter, self_copy, name):
    n = len(arrays)
    nf = len(flips)
    out_shapes = [jax.ShapeDtypeStruct(a.shape if scatter else (n_slots,) + a.shape, a.dtype) for a in arrays]

    def body(*refs):
        ins, outs = refs[:n], refs[n:2 * n]
        send_sems, recv_sems, loc_sems = refs[2 * n:]
        x, y, c = lax.axis_index("x"), lax.axis_index("y"), lax.axis_index("c")
        me = slot_of(x, y, c)
        peers = [(1 - x if fx else x, 1 - y if fy else y, 1 - c if fc else c) for fx, fy, fc in flips]
        local, remote = [], []
        for a in range(n):
            if self_copy:
                lc = pltpu.make_async_copy(ins[a].at[me] if scatter else ins[a], outs[a].at[me], loc_sems.at[a])
                lc.start()
                local.append(lc)
            for k, p in enumerate(peers):
                src = ins[a].at[slot_of(*p)] if scatter else ins[a]
                cp = pltpu.make_async_remote_copy(
                    src_ref=src, dst_ref=outs[a].at[me], send_sem=send_sems.at[a, k], recv_sem=recv_sems.at[a, k],
                    device_id=p, device_id_type=MESH)
                cp.start()
                remote.append(cp)
        for a in range(n):
            for k, p in enumerate(peers):
                src = ins[a].at[slot_of(*p)] if scatter else ins[a]
                pltpu.make_async_remote_copy(
                    src_ref=src, dst_ref=outs[a].at[slot_of(*p)], send_sem=send_sems.at[a, k],
                    recv_sem=recv_sems.at[a, k], device_id=p, device_id_type=MESH).wait_recv()
        for cp in remote:
            cp.wait_send()
        for lc in local:
            lc.wait()

    return pl.pallas_call(
        body, name=name, in_specs=[ANY] * n, out_specs=[ANY] * n, out_shape=out_shapes,
        scratch_shapes=[pltpu.SemaphoreType.DMA((n, nf)), pltpu.SemaphoreType.DMA((n, nf)),
                        pltpu.SemaphoreType.DMA((n,))],
        compiler_params=pltpu.CompilerParams(has_side_effects=True),
    )(*arrays)


CHIP_FLIPS = [(1, 0, 0), (0, 1, 0), (1, 1, 0)]
ALL_FLIPS = [(fx, fy, fc) for fx in (0, 1) for fy in (0, 1) for fc in (0, 1) if fx or fy or fc]


def _chip_slot(x, y, c):
    return 2 * x + y


def _dev_slot(x, y, c):
    return 4 * x + 2 * y + c


def _zero_slot(x, y, c):
    return 0


HBM_SPEC = pl.BlockSpec(memory_space=pltpu.HBM)
SEM_SPEC = pl.BlockSpec(memory_space=pltpu.SEMAPHORE)
N_PEER_CHIPS = 3


def _peer_chips(x, y, c):
    return [(1 - x, y, c), (x, 1 - y, c), (1 - x, 1 - y, c)]


def _send_start(srcs, *, scatter, dep=None, name):
    n = len(srcs)
    nc = n * N_PEER_CHIPS
    srcs = [pltpu.with_memory_space_constraint(s, pltpu.HBM) for s in srcs]
    land_shapes = [s.shape if scatter else (4,) + s.shape for s in srcs]
    lands = [pltpu.with_memory_space_constraint(lax.empty(sh, s.dtype), pltpu.HBM) for sh, s in zip(land_shapes, srcs)]

    deps = [] if dep is None else [dep]
    nd = len(deps)

    def body(*refs):
        ins, lnd = refs[:n], refs[n:2 * n]
        send_sems, recv_sems = refs[2 * n + nd:2 * n + nd + nc], refs[2 * n + nd + nc:2 * n + nd + 2 * nc]
        token = refs[-1]
        x, y, c = lax.axis_index("x"), lax.axis_index("y"), lax.axis_index("c")
        me = 2 * x + y
        for a in range(n):
            for k, p in enumerate(_peer_chips(x, y, c)):
                src = ins[a].at[2 * p[0] + p[1]] if scatter else ins[a]
                j = a * N_PEER_CHIPS + k
                pltpu.make_async_remote_copy(
                    src_ref=src, dst_ref=lnd[a].at[me], send_sem=send_sems[j], recv_sem=recv_sems[j],
                    device_id=p, device_id_type=MESH).start()
        token[...] = jnp.zeros(token.shape, token.dtype)

    sem = pltpu.SemaphoreType.DMA(())
    outs = pl.pallas_call(
        body, name=name,
        out_shape=(*[sem] * (2 * nc), *[pltpu.HBM(s.shape, s.dtype) for s in srcs],
                   *[pltpu.HBM(sh, s.dtype) for sh, s in zip(land_shapes, srcs)],
                   jax.ShapeDtypeStruct((SUBLANES, LANES), F32)),
        in_specs=[HBM_SPEC] * (2 * n) + [ANY] * nd,
        out_specs=(*[SEM_SPEC] * (2 * nc), *[HBM_SPEC] * (2 * n), pl.BlockSpec(memory_space=pltpu.VMEM)),
        input_output_aliases={i: 2 * nc + i for i in range(2 * n)},
        compiler_params=pltpu.CompilerParams(has_side_effects=pltpu.SideEffectType.DATAFLOW_SIDE_EFFECTING),
    )(*srcs, *lands, *deps)
    return dict(sems=list(outs[:2 * nc]), srcs=list(outs[2 * nc:2 * nc + n]),
                lands=list(outs[2 * nc + n:2 * nc + 2 * n]), token=outs[-1])


def _send_wait(h, *, scatter, after, name):
    n = len(h["srcs"])
    nc = n * N_PEER_CHIPS

    def body(*refs):
        ins, lnd = refs[:n], refs[n:2 * n]
        send_sems, recv_sems = refs[2 * n:2 * n + nc], refs[2 * n + nc:2 * n + 2 * nc]
        x, y, c = lax.axis_index("x"), lax.axis_index("y"), lax.axis_index("c")
        for a in range(n):
            for k, p in enumerate(_peer_chips(x, y, c)):
                slot = 2 * p[0] + p[1]
                j = a * N_PEER_CHIPS + k
                cp = pltpu.make_async_remote_copy(
                    src_ref=ins[a].at[slot] if scatter else ins[a], dst_ref=lnd[a].at[slot],
                    send_sem=send_sems[j], recv_sem=recv_sems[j], device_id=p, device_id_type=MESH)
                cp.wait_send()
                cp.wait_recv()

    thru = h["srcs"] + h["lands"]
    outs = pl.pallas_call(
        body, name=name, out_shape=tuple(pltpu.HBM(t.shape, t.dtype) for t in thru),
        in_specs=[HBM_SPEC] * (2 * n) + [SEM_SPEC] * (2 * nc) + [ANY] * len(after),
        out_specs=tuple([HBM_SPEC] * (2 * n)),
        input_output_aliases={i: i for i in range(2 * n)},
        compiler_params=pltpu.CompilerParams(has_side_effects=pltpu.SideEffectType.DATAFLOW_SIDE_EFFECTING),
    )(*thru, *h["sems"], *after)
    return list(outs[:n]), list(outs[n:])


def _assemble(land, own, chip, axis):
    return jnp.concatenate([jnp.where(chip == k, own, land[k]) for k in range(4)], axis=axis)


def _pack(arrs):
    flat = jnp.concatenate([a.reshape(-1).astype(F32) for a in arrs])
    tile = SUBLANES * LANES
    pad = (-flat.shape[0]) % tile
    return jnp.pad(flat, (0, pad)).reshape(-1, LANES)


def _unpack(buf, shapes):
    flat = buf.reshape(-1)
    out, off = [], 0
    for s in shapes:
        n = int(np.prod(s))
        out.append(flat[off:off + n].reshape(s))
        off += n
    return out


def _lower_bounds(hg_lower_bounds):
    p = jax.nn.softmax(hg_lower_bounds.astype(F32), axis=0)
    return jnp.cumsum(p, axis=0) - p[0]


def kernel(x, meta_tokens, hg_lower_bounds, w_in, w_conv, w_pool, pool_scale, hg_norm_g, w_o, ln1_g, ln1_b, w_up, w_ffn_conv, b_ffn_conv, w_down, ln2_g, ln2_b, loss_target, m_meta_tokens, m_hg_lower_bounds, m_w_in, m_w_conv, m_w_pool, m_pool_scale, m_hg_norm_g, m_w_o, m_ln1_g, m_ln1_b, m_w_up, m_w_ffn_conv, m_b_ffn_conv, m_w_down, m_ln2_g, m_ln2_b, v_meta_tokens, v_hg_lower_bounds, v_w_in, v_w_conv, v_w_pool, v_pool_scale, v_hg_norm_g, v_w_o, v_ln1_g, v_ln1_b, v_w_up, v_w_ffn_conv, v_b_ffn_conv, v_w_down, v_ln2_g, v_ln2_b):
    S = x.shape[1]
    R = S + ROW0
    Fq = w_down.shape[1]
    F = 4 * Fq
    F2 = 2 * F
    F2q = w_up.shape[2]
    assert x.shape == (1, S, D) and R % 384 == 0 and S % ROW0 == 0
    chip = 2 * lax.axis_index("x") + lax.axis_index("y")
    tm = 384
    tm_w = max(t for t in range(SUBLANES, 2113, SUBLANES) if R % t == 0)
    tb_ffn = 128

    small_shapes = [(N_META, D // 4), (DEPTH, CW // 4, 3), (DEPTH, F2q, 3)]
    wb_in, wb_o, wb_up, wb_down = (w.astype(MX) for w in (w_in, w_o, w_up, w_down))
    h_a = _send_start([wb_in[0], _pack([meta_tokens, w_conv, w_ffn_conv])], scatter=False, name="gather_a_start")
    (own_in, own_small), (l_in, l_small) = _send_wait(h_a, scatter=False, after=[h_a["token"]], name="gather_a_wait")
    h_b = _send_start([wb_o[0], wb_up[0], wb_down[0]], scatter=False, dep=l_in, name="gather_b_start")
    Win, Wo, Wup, Wdown = {}, {}, {}, {}
    Win[0] = _assemble(l_in, own_in, chip, 1)
    sm = [_unpack(jnp.where(chip == k, own_small, l_small[k]), small_shapes) for k in range(4)]
    meta_full = jnp.concatenate([sm[k][0] for k in range(4)], axis=1)
    wconv_full = jnp.concatenate([sm[k][1] for k in range(4)], axis=1)
    wffn_full = jnp.concatenate([sm[k][2] for k in range(4)], axis=1)
    taps_c = jnp.pad(wconv_full.transpose(0, 2, 1), ((0, 0), (0, SUBLANES - 3), (0, 0)))
    taps_f = jnp.pad(wffn_full.transpose(0, 2, 1), ((0, 0), (0, SUBLANES - 3), (0, 0)))
    wbd = jnp.stack([jax.scipy.linalg.block_diag(*[w_pool[l, g] for g in range(4)]) for l in range(DEPTH)]).astype(MX)
    lbs, lbs_vjp = jax.vjp(_lower_bounds, hg_lower_bounds)
    mat, suf = _hg_consts()

    def fwd_mixer(l, X, dep=None):
        h = _mm(X, Win[l], tm=tm, dep=dep, name=f"mm_in_{l}")
        y, o, sp = _mixer_fwd(h, taps_c[l], wbd[l], pool_scale[l].reshape(1, PW), hg_norm_g[l].reshape(1, HD),
                              lbs[l].reshape(1, HW), mat, name=f"mixer_fwd_{l}")
        return h, y, o, sp

    def fwd_rest(l, X, h, y, o, sp, dep=None):
        x1, xh1, r1 = _mm_ln(y, Wo[l], X, ln1_g[l], ln1_b[l], tm=tm, dep=dep, name=f"mm_o_ln_{l}")
        up, u, a = _mm_ffn_fwd(x1, Wup[l], taps_f[l], b_ffn_conv[l].reshape(1, F2), tm=tm // 2,
                               name=f"mm_up_ffn_{l}")
        x2, xh2, r2 = _mm_ln(a, Wdown[l], x1, ln2_g[l], ln2_b[l], tm=tm, name=f"mm_down_ln_{l}")
        return (X, h, y, o, sp, x1, xh1, r1, up, u, a, xh2, r2), x2

    X = jnp.concatenate([jnp.zeros((PADR, D), F32), meta_full, x[0]], axis=0)
    h, y, o, sp = fwd_mixer(0, X, dep=h_b["token"])
    (own_o, own_up, own_down), (l_o, l_up, l_down) = _send_wait(h_b, scatter=False, after=[y], name="gather_b_wait")
    Wo[0], Wup[0], Wdown[0] = (_assemble(l_o, own_o, chip, 0), _assemble(l_up, own_up, chip, 1),
                               _assemble(l_down, own_down, chip, 0))
    h_c = _send_start([wb_in[1], wb_o[1], wb_up[1], wb_down[1]], scatter=False, dep=l_o, name="gather_c_start")
    saved0, X1 = fwd_rest(0, X, h, y, o, sp, dep=h_c["token"])
    own_c, l_c = _send_wait(h_c, scatter=False, after=[X1], name="gather_c_wait")
    Win[1], Wo[1] = _assemble(l_c[0], own_c[0], chip, 1), _assemble(l_c[1], own_c[1], chip, 0)
    Wup[1], Wdown[1] = _assemble(l_c[2], own_c[2], chip, 1), _assemble(l_c[3], own_c[3], chip, 0)
    saved1, X2 = fwd_rest(1, X1, *fwd_mixer(1, X1))
    saved = [saved0, saved1]

    dxo, lacc = _loss(X2, loss_target[0], name="loss")
    loss = lax.psum(0.5 * jnp.sum(lacc[0]) / D, ("x", "y", "c"))

    sc = {}

    def scatter(nm, l, g, dep=None):
        sc[nm, l] = _send_start([g], scatter=True, dep=dep, name=f"scatter_{nm}_{l}_start")
        return sc[nm, l]["token"]

    tok = None

    small_g = [None] * DEPTH
    for l in reversed(range(DEPTH)):
        X, h, y, o, sp, x1, xh1, r1, up, u, a, xh2, r2 = saved[l]
        dz2, gb2, da = _ln_bwd_mm(dxo, xh2, r2, ln2_g[l], Wdown[l], tm=tm, dep=tok, name=f"ln2_bwd_da_{l}")
        tok = scatter("w_down", l, _wgrad(a, dz2, slabs_on_cols=False, tm=tm_w, tn=512, name=f"wgrad_down_{l}"))
        dup, facc = _ffn_bwd(da, up, u, taps_f[l], tb=tb_ffn, dep=tok, name=f"ffn_bwd_{l}")
        dx1 = _mm(dup, Wup[l], nt=True, res=dz2, tm=tm, name=f"mm_dx1_{l}")
        tok = scatter("w_up", l, _wgrad(x1, dup, slabs_on_cols=True, tm=tm_w, tn=F2q, name=f"wgrad_up_{l}"))
        dz1, gb1, dym = _ln_bwd_mm(dx1, xh1, r1, ln1_g[l], Wo[l], tm=tm, dep=tok, name=f"ln1_bwd_dym_{l}")
        tok = scatter("w_o", l, _wgrad(y, dz1, slabs_on_cols=False, tm=tm_w, tn=512, name=f"wgrad_o_{l}"))
        dh, dwc, dsc, dgn, dlb, dwbd = _mixer_bwd(
            h, dym, o, sp, taps_c[l], wbd[l], pool_scale[l].reshape(1, PW), hg_norm_g[l].reshape(1, HD),
            lbs[l].reshape(1, HW), mat, suf, dep=tok, name=f"mixer_bwd_{l}")
        tok = scatter("w_in", l, _wgrad(X, dh, slabs_on_cols=True, tm=tm_w, tn=DIN // 4, name=f"wgrad_in_{l}"))
        dX = _mm(dh, Win[l], nt=True, res=dz1, tm=tm, dep=tok, name=f"mm_dx_{l}")
        small_g[l] = dict(
            lbs=dlb[0], w_conv=dwc[0:3].T, w_pool=jnp.stack([dwbd[64 * g:64 * g + 64, 64 * g:64 * g + 64] for g in range(4)]),
            pool_scale=dsc[0], hg_norm_g=dgn[0], ln1_g=gb1[0], ln1_b=gb1[1], w_ffn_conv=facc[0:3].T,
            b_ffn_conv=facc[3], ln2_g=gb2[0], ln2_b=gb2[1])
        dxo = dX
    grad_x = dxo[ROW0:][None]

    sg_names = ["lbs", "w_conv", "w_pool", "pool_scale", "hg_norm_g", "ln1_g", "ln1_b", "w_ffn_conv",
                "b_ffn_conv", "ln2_g", "ln2_b"]
    sg_list = [dxo[PADR:ROW0]] + [jnp.stack([small_g[l][nm] for l in range(DEPTH)]) for nm in sg_names]
    sg_shapes = [a.shape for a in sg_list]
    packed = _pack(sg_list)
    (gathered,) = _exchange([packed], flips=ALL_FLIPS, n_slots=8, slot_of=_dev_slot, scatter=False, self_copy=True,
                            name="gather_small_grads")
    total = _sum_slots(gathered[:, None], name="sum_small_grads")[0]
    tot = dict(zip(["meta_tokens"] + sg_names, _unpack(total, sg_shapes)))
    (g_hg,) = lbs_vjp(tot["lbs"])
    small_grads = dict(
        meta_tokens=lax.dynamic_slice_in_dim(tot["meta_tokens"], chip * (D // 4), D // 4, axis=1),
        hg_lower_bounds=g_hg,
        w_conv=lax.dynamic_slice_in_dim(tot["w_conv"], chip * (CW // 4), CW // 4, axis=1),
        w_pool=tot["w_pool"], pool_scale=tot["pool_scale"], hg_norm_g=tot["hg_norm_g"],
        ln1_g=tot["ln1_g"], ln1_b=tot["ln1_b"],
        w_ffn_conv=lax.dynamic_slice_in_dim(tot["w_ffn_conv"], chip * F2q, F2q, axis=1),
        b_ffn_conv=tot["b_ffn_conv"], ln2_g=tot["ln2_g"], ln2_b=tot["ln2_b"])
    small_w = dict(meta_tokens=(meta_tokens, m_meta_tokens, v_meta_tokens),
                   hg_lower_bounds=(hg_lower_bounds, m_hg_lower_bounds, v_hg_lower_bounds),
                   w_conv=(w_conv, m_w_conv, v_w_conv), w_pool=(w_pool, m_w_pool, v_w_pool),
                   pool_scale=(pool_scale, m_pool_scale, v_pool_scale), hg_norm_g=(hg_norm_g, m_hg_norm_g, v_hg_norm_g),
                   ln1_g=(ln1_g, m_ln1_g, v_ln1_g), ln1_b=(ln1_b, m_ln1_b, v_ln1_b),
                   w_ffn_conv=(w_ffn_conv, m_w_ffn_conv, v_w_ffn_conv), b_ffn_conv=(b_ffn_conv, m_b_ffn_conv, v_b_ffn_conv),
                   ln2_g=(ln2_g, m_ln2_g, v_ln2_g), ln2_b=(ln2_b, m_ln2_b, v_ln2_b))
    names_s = list(small_w)
    shapes_s = [small_w[nm][0].shape for nm in names_s]
    pk = [_pack([small_w[nm][j] for nm in names_s])[None] for j in range(3)]
    pg = _pack([small_grads[nm] for nm in names_s])[None]
    outs_s = _adamw(pk[0], pk[1], pk[2], pg[0], None, layer=0, prev=None, name="adamw_small")
    small = {nm: [] for nm in names_s}
    for j in range(4):
        for nm, val in zip(names_s, _unpack(outs_s[j][0], shapes_s)):
            small[nm].append(val)

    big_w = dict(w_in=(w_in, m_w_in, v_w_in), w_o=(w_o, m_w_o, v_w_o), w_up=(w_up, m_w_up, v_w_up),
                 w_down=(w_down, m_w_down, v_w_down))
    big = {nm: None for nm in big_w}
    for l in reversed(range(DEPTH)):
        part = {}
        for nm in ("w_down", "w_up", "w_o", "w_in"):
            after = [dxo] if (nm, l) != ("w_in", 0) else [outs_s[0]] + [big[k][0] for k in big_w] + list(part.values())
            (own,), (recv,) = _send_wait(sc[nm, l], scatter=True, after=after, name=f"scatter_{nm}_{l}_wait")
            part[nm] = _sum_own(recv, own, chip, name=f"sum_{nm}_{l}")
        sib = _exchange([part[nm] for nm in big_w], flips=[(0, 0, 1)], n_slots=1, slot_of=_zero_slot, scatter=False,
                        self_copy=False, name=f"swap_cores_{l}")
        for k, nm in enumerate(big_w):
            w, m, v = big_w[nm]
            big[nm] = _adamw(w, m, v, part[nm], sib[k][0], layer=l, prev=big[nm], name=f"adamw_{nm}_{l}")

    order = ["meta_tokens", "hg_lower_bounds", "w_in", "w_conv", "w_pool", "pool_scale", "hg_norm_g", "w_o",
             "ln1_g", "ln1_b", "w_up", "w_ffn_conv", "b_ffn_conv", "w_down", "ln2_g", "ln2_b"]
    res = {nm: (big[nm] if nm in big else small[nm]) for nm in order}
    outs = [loss, grad_x]
    for j in range(4):
        outs += [res[nm][j] for nm in order]
    return tuple(outs)
```

```python
import functools

import numpy as np

import jax
import jax.numpy as jnp
from jax import lax
from jax.experimental import pallas as pl
from jax.experimental.pallas import tpu as pltpu

F32 = jnp.float32
BF = jnp.bfloat16
MX = jnp.bfloat16

D = 1024
CW = 256
HW = 512
HD = 128
NH = 4
PW = 256
DIN = 3072
N_META = 16
CH = 64
MB = 4
BM = MB * CH
ROW0 = 256
PADR = ROW0 - N_META
HALO = 16
FH = 8
LEVELS = (32, 16, 8, 4, 2, 1)
DEPTH = 2
ALPHA = (2 * DEPTH) ** 0.25
LN_EPS = 1e-5
RMS_EPS = 1e-6
F_FLOOR = 1e-30
SCALE = HD ** -0.5
ADAM_LR, ADAM_B1, ADAM_B2, ADAM_EPS, ADAM_WD, ADAM_STEP = 0.001, 0.9, 0.999, 1e-08, 0.01, 10

VMEM_V7X = 64 * 2 ** 20
LANES = 128
SUBLANES = 8

NN = (((1,), (0,)), ((), ()))
NT = (((1,), (1,)), ((), ()))
TN = (((0,), (0,)), ((), ()))
MESH = pl.DeviceIdType.MESH
ANY = pl.BlockSpec(memory_space=pl.ANY)


def _dot(a, b, dn):
    return lax.dot_general(a.astype(MX), b.astype(MX), dn, preferred_element_type=F32)


def _cp(sem, est_bytes):
    lim = int(min(VMEM_V7X - 6 * 2 ** 20, max(32 * 2 ** 20, est_bytes)))
    return pltpu.CompilerParams(dimension_semantics=sem, vmem_limit_bytes=lim)


def _nbytes(shape, dtype):
    return int(np.prod(shape)) * jnp.dtype(dtype).itemsize


def _row_tile(rows, row_bytes, budget):
    best = SUBLANES
    for t in range(SUBLANES, rows + 1, SUBLANES):
        if rows % t == 0 and t * row_bytes <= budget:
            best = t
    return best


def _after(dep, body, in_specs, args):
    if dep is None:
        return body, list(in_specs), list(args)

    def body_after(dep_ref, *refs):
        body(*refs)

    return body_after, [ANY] + list(in_specs), [dep] + list(args)


def _mm(a, w, *, nt=False, res=None, tm, out_dtype=F32, zero_inert=False, dep=None, name):
    R, K = a.shape
    N = w.shape[0] if nt else w.shape[1]
    dn = NT if nt else NN

    def body(*refs):
        if res is None:
            a_ref, w_ref, o_ref = refs
        else:
            a_ref, w_ref, r_ref, o_ref = refs
        acc = _dot(a_ref[...], w_ref[...], dn)
        if res is not None:
            acc = acc + ALPHA * r_ref[...]
        if zero_inert:
            acc = jnp.where(_row_ids(pl.program_id(0) * tm, tm) >= PADR, acc, 0.0)
        o_ref[...] = acc.astype(out_dtype)

    in_specs = [pl.BlockSpec((tm, K), lambda i: (i, 0)), pl.BlockSpec(w.shape, lambda i: (0, 0))]
    args = [a, w]
    est = 2 * _nbytes((tm, K), a.dtype) + 2 * _nbytes(w.shape, w.dtype) + 3 * _nbytes((tm, N), F32)
    if res is not None:
        in_specs.append(pl.BlockSpec((tm, N), lambda i: (i, 0)))
        args.append(res)
        est += 2 * _nbytes((tm, N), F32)
    body, in_specs, args = _after(dep, body, in_specs, args)
    return pl.pallas_call(
        body, name=name, grid=(R // tm,), in_specs=in_specs,
        out_specs=pl.BlockSpec((tm, N), lambda i: (i, 0)),
        out_shape=jax.ShapeDtypeStruct((R, N), out_dtype),
        compiler_params=_cp(("parallel",), est + 4 * 2 ** 20),
    )(*args)


def _wgrad(a, b, *, slabs_on_cols, tm, tn, name):
    R, Ka = a.shape
    Nb = b.shape[1]
    if slabs_on_cols:
        out_shape = (4, Ka, Nb // 4)
        assert tn == Nb // 4
        out_spec = pl.BlockSpec((None, Ka, tn), lambda j, i: (j, 0, 0))
    else:
        out_shape = (4, Ka // 4, Nb)
        out_spec = pl.BlockSpec((4, Ka // 4, tn), lambda j, i: (0, 0, j))

    def body(a_ref, b_ref, o_ref):
        @pl.when(pl.program_id(1) == 0)
        def _():
            o_ref[...] = jnp.zeros(o_ref.shape, F32)

        acc = _dot(a_ref[...], b_ref[...], TN)
        o_ref[...] += acc.reshape(o_ref.shape)

    in_specs = [pl.BlockSpec((tm, Ka), lambda j, i: (i, 0)), pl.BlockSpec((tm, tn), lambda j, i: (i, j))]
    est = 2 * _nbytes((tm, Ka), a.dtype) + 2 * _nbytes((tm, tn), b.dtype) + 4 * _nbytes((Ka, tn), F32) \
        + _nbytes((tm, Ka), F32)
    return pl.pallas_call(
        body, name=name, grid=(Nb // tn, R // tm), in_specs=in_specs, out_specs=out_spec,
        out_shape=jax.ShapeDtypeStruct(out_shape, F32),
        compiler_params=_cp(("parallel", "arbitrary"), est + 4 * 2 ** 20),
    )(a, b)


def _mm_ln(a, w, xres, g, b, *, tm, dep=None, name):
    R, K = a.shape

    def body(a_ref, w_ref, x_ref, g_ref, b_ref, xo_ref, xh_ref, r_ref):
        z = ALPHA * x_ref[...] + _dot(a_ref[...], w_ref[...], NN)
        mu = jnp.mean(z, axis=-1, keepdims=True)
        zc = z - mu
        var = jnp.mean(zc * zc, axis=-1, keepdims=True)
        r = lax.rsqrt(var + LN_EPS)
        xh = zc * r
        xh_ref[...] = xh
        r_ref[...] = r
        xo_ref[...] = xh * g_ref[...] + b_ref[...]

    row = lambda i: (i, 0)
    fix = lambda i: (0, 0)
    est = 2 * _nbytes((tm, K), a.dtype) + 2 * _nbytes(w.shape, w.dtype) + 10 * _nbytes((tm, D), F32)
    body, in_specs, args = _after(
        dep, body, [pl.BlockSpec((tm, K), row), pl.BlockSpec(w.shape, fix), pl.BlockSpec((tm, D), row),
                    pl.BlockSpec((1, D), fix), pl.BlockSpec((1, D), fix)],
        [a, w, xres, g.reshape(1, D), b.reshape(1, D)])
    return pl.pallas_call(
        body, name=name, grid=(R // tm,), in_specs=in_specs,
        out_specs=[pl.BlockSpec((tm, D), row), pl.BlockSpec((tm, D), row), pl.BlockSpec((tm, 1), row)],
        out_shape=[jax.ShapeDtypeStruct((R, D), F32), jax.ShapeDtypeStruct((R, D), F32),
                   jax.ShapeDtypeStruct((R, 1), F32)],
        compiler_params=_cp(("parallel",), est + 4 * 2 ** 20),
    )(*args)


def _ln_bwd(dx, xh, r, g, *, tm, dep=None, name):
    R = dx.shape[0]

    def body(dx_ref, xh_ref, r_ref, g_ref, dz_ref, gb_ref):
        @pl.when(pl.program_id(0) == 0)
        def _():
            gb_ref[...] = jnp.zeros(gb_ref.shape, F32)

        dxv = dx_ref[...]
        xhv = xh_ref[...]
        dyh = dxv * g_ref[...]
        m1 = jnp.mean(dyh, axis=-1, keepdims=True)
        m2 = jnp.mean(dyh * xhv, axis=-1, keepdims=True)
        dz_ref[...] = r_ref[...] * (dyh - m1 - xhv * m2)
        gb_ref[0:1, :] += jnp.sum(dxv * xhv, axis=0, keepdims=True)
        gb_ref[1:2, :] += jnp.sum(dxv, axis=0, keepdims=True)

    row = lambda i: (i, 0)
    fix = lambda i: (0, 0)
    body, in_specs, args = _after(
        dep, body, [pl.BlockSpec((tm, D), row), pl.BlockSpec((tm, D), row), pl.BlockSpec((tm, 1), row),
                    pl.BlockSpec((1, D), fix)], [dx, xh, r, g.reshape(1, D)])
    return pl.pallas_call(
        body, name=name, grid=(R // tm,), in_specs=in_specs,
        out_specs=[pl.BlockSpec((tm, D), row), pl.BlockSpec((SUBLANES, D), fix)],
        out_shape=[jax.ShapeDtypeStruct((R, D), F32), jax.ShapeDtypeStruct((SUBLANES, D), F32)],
        compiler_params=_cp(("arbitrary",), 12 * _nbytes((tm, D), F32)),
    )(*args)


def _resident(shape):
    return pl.BlockSpec(shape, lambda i: (0,) * len(shape), pipeline_mode=pl.Buffered(1))


def _ln_bwd_mm(dx, xh, r, g, w, *, tm, dep=None, name):
    R = dx.shape[0]
    N = w.shape[0]

    def body(dx_ref, xh_ref, r_ref, g_ref, w_ref, dz_ref, gb_ref, o_ref):
        @pl.when(pl.program_id(0) == 0)
        def _():
            gb_ref[...] = jnp.zeros(gb_ref.shape, F32)

        dxv = dx_ref[...]
        xhv = xh_ref[...]
        dyh = dxv * g_ref[...]
        m1 = jnp.mean(dyh, axis=-1, keepdims=True)
        m2 = jnp.mean(dyh * xhv, axis=-1, keepdims=True)
        dz = r_ref[...] * (dyh - m1 - xhv * m2)
        dz_ref[...] = dz
        gb_ref[0:1, :] += jnp.sum(dxv * xhv, axis=0, keepdims=True)
        gb_ref[1:2, :] += jnp.sum(dxv, axis=0, keepdims=True)
        o_ref[...] = _dot(dz, w_ref[...], NT)

    row = lambda i: (i, 0)
    fix = lambda i: (0, 0)
    body, in_specs, args = _after(
        dep, body, [pl.BlockSpec((tm, D), row), pl.BlockSpec((tm, D), row), pl.BlockSpec((tm, 1), row),
                    pl.BlockSpec((1, D), fix), _resident(w.shape)], [dx, xh, r, g.reshape(1, D), w])
    est = _nbytes(w.shape, w.dtype) + 14 * _nbytes((tm, D), F32) + 4 * _nbytes((tm, N), F32)
    return pl.pallas_call(
        body, name=name, grid=(R // tm,), in_specs=in_specs,
        out_specs=[pl.BlockSpec((tm, D), row), pl.BlockSpec((SUBLANES, D), fix), pl.BlockSpec((tm, N), row)],
        out_shape=[jax.ShapeDtypeStruct((R, D), F32), jax.ShapeDtypeStruct((SUBLANES, D), F32),
                   jax.ShapeDtypeStruct((R, N), F32)],
        compiler_params=_cp(("arbitrary",), est + 4 * 2 ** 20),
    )(*args)


def _mm_ffn_fwd(x1, w, taps, bias, *, tm, name):
    R, K = x1.shape
    F2 = w.shape[1]
    F = F2 // 2

    def body(x_ref, w_ref, t_ref, b_ref, up_ref, u_ref, a_ref, carry_ref):
        i = pl.program_id(0)

        @pl.when(i == 0)
        def _():
            carry_ref[...] = jnp.zeros(carry_ref.shape, F32)

        acc = _dot(x_ref[...], w_ref[...], NN)
        acc = jnp.where(_row_ids(i * tm, tm) >= PADR, acc, 0.0)
        up_ref[...] = acc.astype(up_ref.dtype)
        ue = jnp.concatenate([carry_ref[...], acc], axis=0)
        carry_ref[...] = acc[tm - FH:tm]
        u = (t_ref[0:1, :] * pltpu.roll(ue, 2, 0)[FH:] + t_ref[1:2, :] * pltpu.roll(ue, 1, 0)[FH:]
             + t_ref[2:3, :] * acc + b_ref[...])
        u_ref[...] = u
        gate = u[:, :F]
        val = u[:, F:]
        a_ref[...] = (gate * jax.nn.sigmoid(gate) * val).astype(a_ref.dtype)

    row = lambda i: (i, 0)
    est = _nbytes(w.shape, w.dtype) + 2 * _nbytes((tm, K), x1.dtype) + 10 * _nbytes((tm + FH, F2), F32)
    return pl.pallas_call(
        body, name=name, grid=(R // tm,),
        in_specs=[pl.BlockSpec((tm, K), row), _resident(w.shape), _resident((SUBLANES, F2)), _resident((1, F2))],
        out_specs=[pl.BlockSpec((tm, F2), row), pl.BlockSpec((tm, F2), row), pl.BlockSpec((tm, F), row)],
        out_shape=[jax.ShapeDtypeStruct((R, F2), MX), jax.ShapeDtypeStruct((R, F2), F32),
                   jax.ShapeDtypeStruct((R, F), MX)],
        scratch_shapes=[pltpu.VMEM((FH, F2), F32)],
        compiler_params=_cp(("arbitrary",), est + 4 * 2 ** 20),
    )(x1, w, taps, bias)


def _loss(x2, tgt, *, name):
    R = x2.shape[0]
    tb = ROW0

    def body(x_ref, t_ref, dy_ref, acc_ref):
        i = pl.program_id(0)

        @pl.when(i == 0)
        def _():
            acc_ref[...] = jnp.zeros(acc_ref.shape, F32)
            dy_ref[...] = jnp.zeros(dy_ref.shape, F32)

        @pl.when(i > 0)
        def _():
            err = x_ref[...] - t_ref[...]
            dy_ref[...] = err / D
            acc_ref[0:1, :] += jnp.sum(err * err, axis=0, keepdims=True)

    return pl.pallas_call(
        body, name=name, grid=(R // tb,),
        in_specs=[pl.BlockSpec((tb, D), lambda i: (i, 0)),
                  pl.BlockSpec((tb, D), lambda i: (jnp.maximum(i - 1, 0), 0))],
        out_specs=[pl.BlockSpec((tb, D), lambda i: (i, 0)), pl.BlockSpec((SUBLANES, D), lambda i: (0, 0))],
        out_shape=[jax.ShapeDtypeStruct((R, D), F32), jax.ShapeDtypeStruct((SUBLANES, D), F32)],
        compiler_params=_cp(("arbitrary",), 8 * _nbytes((tb, D), F32)),
    )(x2, tgt)


def _row_ids(start, n):
    return start + lax.broadcasted_iota(jnp.int32, (n, 1), 0)


def _ffn_fwd(up, taps, bias, *, tb, name):
    R, F2 = up.shape
    F = F2 // 2
    nh = tb // FH

    def body(u_ref, up_ref, w_ref, b_ref, a_ref):
        ue = jnp.concatenate([up_ref[...], u_ref[...]], axis=0)
        u = (w_ref[0:1, :] * pltpu.roll(ue, 2, 0)[FH:] + w_ref[1:2, :] * pltpu.roll(ue, 1, 0)[FH:]
             + w_ref[2:3, :] * u_ref[...] + b_ref[...])
        gate = u[:, :F]
        val = u[:, F:]
        a_ref[...] = (gate * jax.nn.sigmoid(gate) * val).astype(a_ref.dtype)

    est = 2 * _nbytes((tb, F2), F32) * 4
    return pl.pallas_call(
        body, name=name, grid=(R // tb,),
        in_specs=[pl.BlockSpec((tb, F2), lambda i: (i, 0)),
                  pl.BlockSpec((FH, F2), lambda i: (jnp.maximum(i * nh - 1, 0), 0)),
                  pl.BlockSpec((SUBLANES, F2), lambda i: (0, 0)), pl.BlockSpec((1, F2), lambda i: (0, 0))],
        out_specs=pl.BlockSpec((tb, F), lambda i: (i, 0)),
        out_shape=jax.ShapeDtypeStruct((R, F), MX),
        compiler_params=_cp(("parallel",), est),
    )(up, up, taps, bias)


def _ffn_bwd(da, up, u, taps, *, tb, dep=None, name):
    R, F2 = up.shape
    F = F2 // 2
    nh = tb // FH
    nb = R // tb
    last = R // FH - 1
    m = tb + FH

    def body(da_ref, dan_ref, x_ref, u_ref, un_ref, w_ref, du_ref, acc_ref):
        i = pl.program_id(0)

        @pl.when(i == 0)
        def _():
            acc_ref[...] = jnp.zeros(acc_ref.shape, F32)

        inside = (i < nb - 1).astype(F32)
        w0, w1, w2 = w_ref[0:1, :], w_ref[1:2, :], w_ref[2:3, :]
        u = jnp.concatenate([u_ref[...], un_ref[...]], axis=0)
        dae = jnp.concatenate([da_ref[...], dan_ref[...] * inside], axis=0)
        gate = u[:, :F]
        val = u[:, F:]
        sg = jax.nn.sigmoid(gate)
        gs = gate * sg
        du = jnp.concatenate([dae * val * (sg * (1.0 + gate - gs)), dae * gs], axis=1)
        du0 = du[0:tb]
        du1 = pltpu.roll(du, m - 1, 0)[0:tb]
        du2 = pltpu.roll(du, m - 2, 0)[0:tb]
        du_ref[...] = (w2 * du0 + w1 * du1 + w0 * du2).astype(du_ref.dtype)

        @pl.when(i * tb < PADR)
        def _():
            row = _row_ids(i * tb, tb)
            du_ref[...] = jnp.where(row >= PADR, du_ref[...], jnp.zeros((), du_ref.dtype))

        x = x_ref[...].astype(F32)
        acc_ref[0:1, :] += jnp.sum(du2 * x, axis=0, keepdims=True)
        acc_ref[1:2, :] += jnp.sum(du1 * x, axis=0, keepdims=True)
        acc_ref[2:3, :] += jnp.sum(du0 * x, axis=0, keepdims=True)
        acc_ref[3:4, :] += jnp.sum(du0, axis=0, keepdims=True)

    nxt = lambda i: (jnp.minimum(i * nh + nh, last), 0)
    cur = lambda i: (i, 0)
    fix = lambda i: (0, 0)
    est = 12 * _nbytes((tb + 2 * FH, F2), F32)
    body, in_specs, args = _after(
        dep, body, [pl.BlockSpec((tb, F), cur), pl.BlockSpec((FH, F), nxt),
                    pl.BlockSpec((tb, F2), cur), pl.BlockSpec((tb, F2), cur), pl.BlockSpec((FH, F2), nxt),
                    pl.BlockSpec((SUBLANES, F2), fix)], [da, da, up, u, u, taps])
    return pl.pallas_call(
        body, name=name, grid=(R // tb,), in_specs=in_specs,
        out_specs=[pl.BlockSpec((tb, F2), cur), pl.BlockSpec((SUBLANES, F2), fix)],
        out_shape=[jax.ShapeDtypeStruct((R, F2), MX), jax.ShapeDtypeStruct((SUBLANES, F2), F32)],
        compiler_params=_cp(("arbitrary",), est),
    )(*args)


def _hg_consts():
    t = np.arange(CH)[:, None]
    j = np.arange(CH)[None, :]
    low = (j <= t).astype(np.float32)
    blocks = [low, (j > t).astype(np.float32)]
    for m in LEVELS:
        ref = (t // (2 * m)) * 2 * m + m - 1
        blocks.append(low - (j <= ref).astype(np.float32))
    mat = np.concatenate(blocks, axis=0)
    suf = (j >= t).astype(np.float32)
    return jnp.asarray(mat, BF), jnp.asarray(suf, BF)


def _split_dot(mat, x):
    hi = x.astype(BF)
    lo = (x - hi.astype(F32)).astype(BF)
    return (lax.dot_general(mat, hi, NN, preferred_element_type=F32)
            + lax.dot_general(mat, lo, NN, preferred_element_type=F32))


def _lane_select(a2, a4, a8, a16):
    lane = lax.broadcasted_iota(jnp.int32, (1, PW), 1)
    return jnp.where(lane < 64, a2, jnp.where(lane < 128, a4, jnp.where(lane < 192, a8, a16)))


def _pool_count(row):
    win = _lane_select(2.0, 4.0, 8.0, 16.0)
    t1 = jnp.maximum((row - PADR + 1).astype(F32), 1.0)
    return jnp.minimum(t1, win)


def _gates(fz, lb, valid):
    sig = jax.nn.sigmoid(fz)
    f = lb + (1.0 - lb) * sig
    lf = jnp.where(valid, jnp.log(jnp.maximum(f, F_FLOOR)), 0.0)
    kk = jnp.where(valid, (1.0 - lb) * (1.0 - sig), 0.0)
    return sig, f, lf, kk


def _level_masks():
    tt = lax.broadcasted_iota(jnp.int32, (CH, CH), 0)
    ss = lax.broadcasted_iota(jnp.int32, (CH, CH), 1)
    xr = tt ^ ss
    low = tt > ss
    return tt, ss, [(xr >= m) & (xr < 2 * m) & low for m in LEVELS]


def _intra(q, kk, ex, sl, tt, ss, masks):
    a = jnp.where(tt == ss, jnp.sum(q * kk, axis=-1, keepdims=True), 0.0)
    parts = []
    for l in range(len(LEVELS)):
        e = jnp.exp(-jnp.abs(ex[(2 + l) * CH:(3 + l) * CH, sl]))
        ql = (q * e).astype(MX)
        kl = (kk * e).astype(MX)
        a = a + jnp.where(masks[l], _dot(ql, kl, NT), 0.0)
        parts.append((e, ql, kl))
    return a, parts


def _mixer_fwd(h, taps, wbd, ps, gn, lb, mat, *, name):
    R = h.shape[0]
    nb = R // BM
    q4 = BM // HALO

    def body(h_ref, hp_ref, w_ref, wbd_ref, ps_ref, gn_ref, lb_ref, mat_ref, y_ref, o_ref, sp_ref, st_ref):
        i = pl.program_id(0)

        @pl.when(i == 0)
        def _():
            st_ref[...] = jnp.zeros(st_ref.shape, F32)

        row = _row_ids(i * BM, BM)
        valid_b = row >= PADR
        rowe = _row_ids(i * BM - HALO, BM + HALO)
        valide = rowe >= PADR
        se = jnp.concatenate([hp_ref[:, 256:512] * hp_ref[:, 512:768], h_ref[:, 256:512] * h_ref[:, 512:768]], axis=0)
        se = jnp.where(valide, se, 0.0)
        conv = (w_ref[0:1, :] * pltpu.roll(se, 2, 0) + w_ref[1:2, :] * pltpu.roll(se, 1, 0)
                + w_ref[2:3, :] * se)[HALO:]
        y_ref[:, 0:CW] = (h_ref[:, 0:256] * conv).astype(y_ref.dtype)
        ve = jnp.where(valide, jnp.concatenate([hp_ref[:, 2816:3072], h_ref[:, 2816:3072]], axis=0), 0.0)
        s2 = ve + pltpu.roll(ve, 1, 0)
        s4 = s2 + pltpu.roll(s2, 2, 0)
        s8 = s4 + pltpu.roll(s4, 4, 0)
        s16 = s8 + pltpu.roll(s8, 8, 0)
        dp = (_lane_select(s2, s4, s8, s16) / _pool_count(rowe) - ve)[HALO:]
        y_ref[:, CW + HW:D] = (_dot(dp, wbd_ref[...], NN) * ps_ref[...]).astype(y_ref.dtype)
        lbv = lb_ref[...]
        tt, ss, masks = _level_masks()
        for c in range(MB):
            rs = slice(c * CH, (c + 1) * CH)
            _, _, lf, kk = _gates(h_ref[rs, 1280:1792], lbv, valid_b[rs])
            q = h_ref[rs, 768:1280] * SCALE
            ii = h_ref[rs, 1792:2304]
            gz = h_ref[rs, 2304:2816]
            ex = _split_dot(mat_ref[...], lf)
            eg = jnp.exp(ex[0:CH])
            q0 = q * eg
            kr = kk * jnp.exp(ex[CH:2 * CH])
            for hd in range(NH):
                sl = slice(HD * hd, HD * (hd + 1))
                st = st_ref[hd]
                sp_ref[c, hd] = st
                a, _ = _intra(q[:, sl], kk[:, sl], ex, sl, tt, ss, masks)
                o = _dot(q0[:, sl], st, NT) + _dot(a, ii[:, sl], NN)
                st_ref[hd] = st * eg[CH - 1:CH, sl] + _dot(ii[:, sl], kr[:, sl], TN)
                o_ref[rs, sl] = o
                r = lax.rsqrt(jnp.mean(o * o, axis=-1, keepdims=True) + RMS_EPS)
                g_ = gz[:, sl]
                y_ref[rs, CW + HD * hd:CW + HD * (hd + 1)] = (
                    o * r * gn_ref[...] * (g_ * jax.nn.sigmoid(g_))).astype(y_ref.dtype)

    fix = lambda i: (0, 0)
    return pl.pallas_call(
        body, name=name, grid=(nb,),
        in_specs=[pl.BlockSpec((BM, DIN), lambda i: (i, 0)),
                  pl.BlockSpec((HALO, DIN), lambda i: (jnp.maximum(i * q4 - 1, 0), 0)),
                  pl.BlockSpec((SUBLANES, CW), fix), pl.BlockSpec((PW, PW), fix), pl.BlockSpec((1, PW), fix),
                  pl.BlockSpec((1, HD), fix), pl.BlockSpec((1, HW), fix), pl.BlockSpec(mat.shape, fix)],
        out_specs=[pl.BlockSpec((BM, D), lambda i: (i, 0)), pl.BlockSpec((BM, HW), lambda i: (i, 0)),
                   pl.BlockSpec((MB, NH, HD, HD), lambda i: (i, 0, 0, 0))],
        out_shape=[jax.ShapeDtypeStruct((R, D), MX), jax.ShapeDtypeStruct((R, HW), F32),
                   jax.ShapeDtypeStruct((R // CH, NH, HD, HD), F32)],
        scratch_shapes=[pltpu.VMEM((NH, HD, HD), F32)],
        compiler_params=_cp(("arbitrary",), 32 * 2 ** 20),
    )(h, h, taps, wbd, ps, gn, lb, mat)


def _mixer_bwd(h, dy, o, sp, taps, wbd, ps, gn, lb, mat, suf, *, dep=None, name):
    R = h.shape[0]
    nb = R // BM
    q4 = BM // HALO
    lasth = R // HALO - 1

    def body(h_ref, hp_ref, hn_ref, dy_ref, dyn_ref, o_ref, sp_ref, w_ref, wbd_ref, ps_ref, gn_ref, lb_ref,
             mat_ref, suf_ref, dh_ref, dw_ref, dsc_ref, dgn_ref, dlb_ref, dwbd_ref, dst_ref):
        i = pl.program_id(0)
        b = nb - 1 - i

        @pl.when(i == 0)
        def _():
            dst_ref[...] = jnp.zeros(dst_ref.shape, F32)
            dw_ref[...] = jnp.zeros(dw_ref.shape, F32)
            dsc_ref[...] = jnp.zeros(dsc_ref.shape, F32)
            dgn_ref[...] = jnp.zeros(dgn_ref.shape, F32)
            dlb_ref[...] = jnp.zeros(dlb_ref.shape, F32)
            dwbd_ref[...] = jnp.zeros(dwbd_ref.shape, F32)

        ne = BM + 2 * HALO
        nc = BM + HALO
        row = _row_ids(b * BM, BM)
        valid = row >= PADR
        rowe = _row_ids(b * BM - HALO, ne)
        valide = (rowe >= PADR) & (rowe < R)
        rown = rowe[HALO:]
        validn = rown < R

        def cat3(lo, hi):
            return jnp.concatenate([hp_ref[:, lo:hi], h_ref[:, lo:hi], hn_ref[:, lo:hi]], axis=0)

        def catd(lo, hi):
            return jnp.where(validn, jnp.concatenate([dy_ref[:, lo:hi], dyn_ref[:, lo:hi]], axis=0), 0.0)

        w0, w1, w2 = w_ref[0:1, :], w_ref[1:2, :], w_ref[2:3, :]
        cce = cat3(256, 512)
        cve = cat3(512, 768)
        se = jnp.where(valide, cce * cve, 0.0)
        sm2 = pltpu.roll(se, 2, 0)
        sm1 = pltpu.roll(se, 1, 0)
        conv = (w0 * sm2 + w1 * sm1 + w2 * se)[HALO:HALO + BM]
        cbn = jnp.concatenate([h_ref[:, 0:256], hn_ref[:, 0:256]], axis=0)
        dconv = catd(0, CW) * cbn
        ds = w2 * dconv + w1 * pltpu.roll(dconv, nc - 1, 0) + w0 * pltpu.roll(dconv, nc - 2, 0)
        ds = jnp.where(valid, ds[0:BM], 0.0)
        dcv = dconv[0:BM]
        dw_ref[0:1, :] += jnp.sum(dcv * sm2[HALO:HALO + BM], axis=0, keepdims=True)
        dw_ref[1:2, :] += jnp.sum(dcv * sm1[HALO:HALO + BM], axis=0, keepdims=True)
        dw_ref[2:3, :] += jnp.sum(dcv * se[HALO:HALO + BM], axis=0, keepdims=True)
        dh_ref[:, 0:256] = jnp.where(valid, dy_ref[:, 0:CW] * conv, 0.0).astype(dh_ref.dtype)
        dh_ref[:, 256:512] = (ds * h_ref[:, 512:768]).astype(dh_ref.dtype)
        dh_ref[:, 512:768] = (ds * h_ref[:, 256:512]).astype(dh_ref.dtype)

        ve = jnp.where(valide, cat3(2816, 3072), 0.0)
        s2 = ve + pltpu.roll(ve, 1, 0)
        s4 = s2 + pltpu.roll(s2, 2, 0)
        s8 = s4 + pltpu.roll(s4, 4, 0)
        s16 = s8 + pltpu.roll(s8, 8, 0)
        cnte = _pool_count(rowe)
        dp = (_lane_select(s2, s4, s8, s16) / cnte - ve)[HALO:HALO + BM]
        dyp = catd(CW + HW, D)
        pre = _dot(dp, wbd_ref[...], NN)
        dsc_ref[0:1, :] += jnp.sum(dyp[0:BM] * pre, axis=0, keepdims=True)
        dyps = dyp * ps_ref[...]
        dd = _dot(dyps, wbd_ref[...], NT)
        dwbd_ref[...] += _dot(dp, dyps[0:BM], TN)
        e = dd / cnte[HALO:]
        t2 = e + pltpu.roll(e, nc - 1, 0)
        t4 = t2 + pltpu.roll(t2, nc - 2, 0)
        t8 = t4 + pltpu.roll(t4, nc - 4, 0)
        t16 = t8 + pltpu.roll(t8, nc - 8, 0)
        dv = (_lane_select(t2, t4, t8, t16) - dd)[0:BM]
        dh_ref[:, 2816:3072] = jnp.where(valid, dv, 0.0).astype(dh_ref.dtype)

        lbv = lb_ref[...]
        tt, ss, masks = _level_masks()
        gnv = gn_ref[...]
        for c in reversed(range(MB)):
            rs = slice(c * CH, (c + 1) * CH)
            vc = valid[rs]
            sig, f, lf, kk = _gates(h_ref[rs, 1280:1792], lbv, vc)
            q = h_ref[rs, 768:1280] * SCALE
            ii = h_ref[rs, 1792:2304]
            gz = h_ref[rs, 2304:2816]
            ex = _split_dot(mat_ref[...], lf)
            eg = jnp.exp(ex[0:CH])
            egr = jnp.exp(ex[CH:2 * CH])
            q0 = q * eg
            kr = kk * egr
            dqs, dks, dis, dgzs, tails, dbs = [], [], [], [], [], []
            for hd in range(NH):
                sl = slice(HD * hd, HD * (hd + 1))
                ov = o_ref[rs, sl]
                r = lax.rsqrt(jnp.mean(ov * ov, axis=-1, keepdims=True) + RMS_EPS)
                oh = ov * r
                g_ = gz[:, sl]
                sg = jax.nn.sigmoid(g_)
                dyv = dy_ref[rs, CW + HD * hd:CW + HD * (hd + 1)]
                don = dyv * (g_ * sg)
                dgzs.append(dyv * (oh * gnv) * (sg * (1.0 + g_ * (1.0 - sg))))
                dgn_ref[0:1, :] += jnp.sum(don * oh, axis=0, keepdims=True)
                doh = don * gnv
                do = r * (doh - oh * jnp.mean(doh * oh, axis=-1, keepdims=True))
                st = sp_ref[c, hd]
                dst = dst_ref[hd]
                qh, kh, ih = q[:, sl], kk[:, sl], ii[:, sl]
                a, parts = _intra(qh, kh, ex, sl, tt, ss, masks)
                da = jnp.where(tt >= ss, _dot(do, ih, NT), 0.0)
                dis.append(_dot(a, do, TN) + _dot(kr[:, sl], dst, NT))
                q0h = q0[:, sl].astype(MX)
                krh = kr[:, sl].astype(MX)
                dq0 = _dot(do, st, NN)
                dkr = _dot(ih, dst, NN)
                dq = dq0 * eg[:, sl]
                dk = dkr * egr[:, sl]
                kdk = krh.astype(F32) * dkr
                db = q0h.astype(F32) * dq0 - kdk
                tails.append(jnp.sum(kdk, axis=0, keepdims=True)
                             + eg[CH - 1:CH, sl] * jnp.sum(dst * st, axis=0, keepdims=True))
                dga = jnp.sum(jnp.where(tt == ss, da, 0.0), axis=-1, keepdims=True)
                dq = dq + dga * kh
                dk = dk + dga * qh
                for l in range(len(LEVELS)):
                    e_l, ql, kl = parts[l]
                    dpl = jnp.where(masks[l], da, 0.0).astype(MX)
                    dql = _dot(dpl, kl, NN)
                    dkl = _dot(dpl, ql, TN)
                    dq = dq + dql * e_l
                    dk = dk + dkl * e_l
                    db = db + (ql.astype(F32) * dql - kl.astype(F32) * dkl)
                dst_ref[hd] = dst * eg[CH - 1:CH, sl] + _dot(do, q0h, TN)
                dqs.append(dq)
                dks.append(dk)
                dbs.append(db)
            dq = jnp.concatenate(dqs, axis=1)
            dk = jnp.concatenate(dks, axis=1)
            db = jnp.concatenate(dbs, axis=1)
            dlf = _split_dot(suf_ref[...], db) + jnp.concatenate(tails, axis=1)
            t = jnp.where(vc, dlf * jnp.where(f > F_FLOOR, 1.0 / f, 0.0) - dk, 0.0)
            dlb_ref[0:1, :] += jnp.sum(t * (1.0 - sig), axis=0, keepdims=True)
            dh_ref[rs, 768:1280] = jnp.where(vc, dq * SCALE, 0.0).astype(dh_ref.dtype)
            dh_ref[rs, 1280:1792] = (t * (1.0 - lbv) * (sig * (1.0 - sig))).astype(dh_ref.dtype)
            dh_ref[rs, 1792:2304] = jnp.where(vc, jnp.concatenate(dis, axis=1), 0.0).astype(dh_ref.dtype)
            dh_ref[rs, 2304:2816] = jnp.where(vc, jnp.concatenate(dgzs, axis=1), 0.0).astype(dh_ref.dtype)

    cur = lambda i: (nb - 1 - i, 0)
    prev = lambda i: (jnp.maximum((nb - 1 - i) * q4 - 1, 0), 0)
    nxt = lambda i: (jnp.minimum((nb - 1 - i) * q4 + q4, lasth), 0)
    fix = lambda i: (0, 0)
    body, in_specs, args = _after(
        dep, body,
        [pl.BlockSpec((BM, DIN), cur), pl.BlockSpec((HALO, DIN), prev), pl.BlockSpec((HALO, DIN), nxt),
         pl.BlockSpec((BM, D), cur), pl.BlockSpec((HALO, D), nxt), pl.BlockSpec((BM, HW), cur),
         pl.BlockSpec((MB, NH, HD, HD), lambda i: (nb - 1 - i, 0, 0, 0)),
         pl.BlockSpec((SUBLANES, CW), fix), pl.BlockSpec((PW, PW), fix), pl.BlockSpec((1, PW), fix),
         pl.BlockSpec((1, HD), fix), pl.BlockSpec((1, HW), fix), pl.BlockSpec(mat.shape, fix),
         pl.BlockSpec(suf.shape, fix)],
        [h, h, h, dy, dy, o, sp, taps, wbd, ps, gn, lb, mat, suf])
    return pl.pallas_call(
        body, name=name, grid=(nb,), in_specs=in_specs,
        out_specs=[pl.BlockSpec((BM, DIN), cur), pl.BlockSpec((SUBLANES, CW), fix), pl.BlockSpec((SUBLANES, PW), fix),
                   pl.BlockSpec((SUBLANES, HD), fix), pl.BlockSpec((SUBLANES, HW), fix), pl.BlockSpec((PW, PW), fix)],
        out_shape=[jax.ShapeDtypeStruct((R, DIN), MX), jax.ShapeDtypeStruct((SUBLANES, CW), F32),
                   jax.ShapeDtypeStruct((SUBLANES, PW), F32), jax.ShapeDtypeStruct((SUBLANES, HD), F32),
                   jax.ShapeDtypeStruct((SUBLANES, HW), F32), jax.ShapeDtypeStruct((PW, PW), F32)],
        scratch_shapes=[pltpu.VMEM((NH, HD, HD), F32)],
        compiler_params=_cp(("arbitrary",), 40 * 2 ** 20),
    )(*args)


def _sum_slots(recv, *, name):
    S, L, rows, cols = recv.shape
    tr = _row_tile(rows, S * cols * 4, 6 * 2 ** 20)

    def body(r_ref, o_ref):
        acc = r_ref[0]
        for s in range(1, S):
            acc = acc + r_ref[s]
        o_ref[...] = acc

    return pl.pallas_call(
        body, name=name, grid=(L, rows // tr),
        in_specs=[pl.BlockSpec((S, None, tr, cols), lambda l, i: (0, l, i, 0))],
        out_specs=pl.BlockSpec((None, tr, cols), lambda l, i: (l, i, 0)),
        out_shape=jax.ShapeDtypeStruct((L, rows, cols), F32),
        compiler_params=_cp(("parallel", "parallel"), 4 * S * tr * cols * 4),
    )(recv)


def _sum_own(recv, own, chip, *, name):
    S, rows, cols = recv.shape
    tr = _row_tile(rows, S * cols * 4, 6 * 2 ** 20)

    def body(me_ref, r_ref, o_ref, out_ref):
        me = me_ref[0]
        acc = None
        for s in range(S):
            t = jnp.where(me == s, o_ref[...], r_ref[s])
            acc = t if acc is None else acc + t
        out_ref[...] = acc

    grid_spec = pltpu.PrefetchScalarGridSpec(
        num_scalar_prefetch=1, grid=(rows // tr,),
        in_specs=[pl.BlockSpec((S, tr, cols), lambda i, me: (0, i, 0)),
                  pl.BlockSpec((None, tr, cols), lambda i, me: (me[0], i, 0))],
        out_specs=pl.BlockSpec((tr, cols), lambda i, me: (i, 0)))
    return pl.pallas_call(
        body, name=name, grid_spec=grid_spec, out_shape=jax.ShapeDtypeStruct((rows, cols), F32),
        compiler_params=_cp(("parallel",), 5 * S * tr * cols * 4),
    )(chip.reshape(1).astype(jnp.int32), recv, own)


def _adamw(w, m, v, ga, gb, *, layer, prev, name):
    L, rows, cols = w.shape
    tr = _row_tile(rows, cols * 4, 2 ** 20)
    two = gb is not None
    nin = 5 if two else 4

    def body(*refs):
        w_ref, m_ref, v_ref, a_ref = refs[:4]
        g_ref, d_ref, m2_ref, v2_ref = refs[-4:]
        g = a_ref[...] + refs[4][...] if two else a_ref[...]
        m2 = ADAM_B1 * m_ref[...] + (1.0 - ADAM_B1) * g
        v2 = ADAM_B2 * v_ref[...] + (1.0 - ADAM_B2) * (g * g)
        m_hat = m2 / (1.0 - ADAM_B1 ** ADAM_STEP)
        v_hat = v2 / (1.0 - ADAM_B2 ** ADAM_STEP)
        g_ref[...] = g
        d_ref[...] = -ADAM_LR * (m_hat / (jnp.sqrt(v_hat) + ADAM_EPS) + ADAM_WD * w_ref[...])
        m2_ref[...] = m2
        v2_ref[...] = v2

    spec = pl.BlockSpec((None, tr, cols), lambda i: (layer, i, 0))
    gspec = pl.BlockSpec((tr, cols), lambda i: (i, 0))
    args = [w, m, v, ga] + ([gb] if two else [])
    in_specs = [spec] * 3 + [gspec] * (nin - 3)
    aliases = {}
    if prev is not None:
        args += list(prev)
        in_specs += [ANY] * 4
        aliases = {nin + j: j for j in range(4)}
    sd = jax.ShapeDtypeStruct((L, rows, cols), F32)
    return pl.pallas_call(
        body, name=name, grid=(rows // tr,), in_specs=in_specs, out_specs=[spec] * 4,
        out_shape=[sd] * 4, input_output_aliases=aliases,
        compiler_params=_cp(("parallel",), 24 * tr * cols * 4),
    )(*args)


def _exchange(arrays, *, flips, n_slots, slot_of, scatter, self_copy, name):
    n = len(arrays)
    nf = len(flips)
    out_shapes = [jax.ShapeDtypeStruct(a.shape if scatter else (n_slots,) + a.shape, a.dtype) for a in arrays]

    def body(*refs):
        ins, outs = refs[:n], refs[n:2 * n]
        send_sems, recv_sems, loc_sems = refs[2 * n:]
        x, y, c = lax.axis_index("x"), lax.axis_index("y"), lax.axis_index("c")
        me = slot_of(x, y, c)
        peers = [(1 - x if fx else x, 1 - y if fy else y, 1 - c if fc else c) for fx, fy, fc in flips]
        local, remote = [], []
        for a in range(n):
            if self_copy:
                lc = pltpu.make_async_copy(ins[a].at[me] if scatter else ins[a], outs[a].at[me], loc_sems.at[a])
                lc.start()
                local.append(lc)
            for k, p in enumerate(peers):
                src = ins[a].at[slot_of(*p)] if scatter else ins[a]
                cp = pltpu.make_async_remote_copy(
                    src_ref=src, dst_ref=outs[a].at[me], send_sem=send_sems.at[a, k], recv_sem=recv_sems.at[a, k],
                    device_id=p, device_id_type=MESH)
                cp.start()
                remote.append(cp)
        for a in range(n):
            for k, p in enumerate(peers):
                src = ins[a].at[slot_of(*p)] if scatter else ins[a]
                pltpu.make_async_remote_copy(
                    src_ref=src, dst_ref=outs[a].at[slot_of(*p)], send_sem=send_sems.at[a, k],
                    recv_sem=recv_sems.at[a, k], device_id=p, device_id_type=MESH).wait_recv()
        for cp in remote:
            cp.wait_send()
        for lc in local:
            lc.wait()

    return pl.pallas_call(
        body, name=name, in_specs=[ANY] * n, out_specs=[ANY] * n, out_shape=out_shapes,
        scratch_shapes=[pltpu.SemaphoreType.DMA((n, nf)), pltpu.SemaphoreType.DMA((n, nf)),
                        pltpu.SemaphoreType.DMA((n,))],
        compiler_params=pltpu.CompilerParams(has_side_effects=True),
    )(*arrays)


CHIP_FLIPS = [(1, 0, 0), (0, 1, 0), (1, 1, 0)]
ALL_FLIPS = [(fx, fy, fc) for fx in (0, 1) for fy in (0, 1) for fc in (0, 1) if fx or fy or fc]


def _chip_slot(x, y, c):
    return 2 * x + y


def _dev_slot(x, y, c):
    return 4 * x + 2 * y + c


def _zero_slot(x, y, c):
    return 0


HBM_SPEC = pl.BlockSpec(memory_space=pltpu.HBM)
SEM_SPEC = pl.BlockSpec(memory_space=pltpu.SEMAPHORE)
N_PEER_CHIPS = 3


def _peer_chips(x, y, c):
    return [(1 - x, y, c), (x, 1 - y, c), (1 - x, 1 - y, c)]


def _send_start(srcs, *, scatter, dep=None, name):
    n = len(srcs)
    nc = n * N_PEER_CHIPS
    srcs = [pltpu.with_memory_space_constraint(s, pltpu.HBM) for s in srcs]
    land_shapes = [s.shape if scatter else (4,) + s.shape for s in srcs]
    lands = [pltpu.with_memory_space_constraint(lax.empty(sh, s.dtype), pltpu.HBM) for sh, s in zip(land_shapes, srcs)]

    deps = [] if dep is None else [dep]
    nd = len(deps)

    def body(*refs):
        ins, lnd = refs[:n], refs[n:2 * n]
        send_sems, recv_sems = refs[2 * n + nd:2 * n + nd + nc], refs[2 * n + nd + nc:2 * n + nd + 2 * nc]
        token = refs[-1]
        x, y, c = lax.axis_index("x"), lax.axis_index("y"), lax.axis_index("c")
        me = 2 * x + y
        for a in range(n):
            for k, p in enumerate(_peer_chips(x, y, c)):
                src = ins[a].at[2 * p[0] + p[1]] if scatter else ins[a]
                j = a * N_PEER_CHIPS + k
                pltpu.make_async_remote_copy(
                    src_ref=src, dst_ref=lnd[a].at[me], send_sem=send_sems[j], recv_sem=recv_sems[j],
                    device_id=p, device_id_type=MESH).start()
        token[...] = jnp.zeros(token.shape, token.dtype)

    sem = pltpu.SemaphoreType.DMA(())
    outs = pl.pallas_call(
        body, name=name,
        out_shape=(*[sem] * (2 * nc), *[pltpu.HBM(s.shape, s.dtype) for s in srcs],
                   *[pltpu.HBM(sh, s.dtype) for sh, s in zip(land_shapes, srcs)],
                   jax.ShapeDtypeStruct((SUBLANES, LANES), F32)),
        in_specs=[HBM_SPEC] * (2 * n) + [ANY] * nd,
        out_specs=(*[SEM_SPEC] * (2 * nc), *[HBM_SPEC] * (2 * n), pl.BlockSpec(memory_space=pltpu.VMEM)),
        input_output_aliases={i: 2 * nc + i for i in range(2 * n)},
        compiler_params=pltpu.CompilerParams(has_side_effects=pltpu.SideEffectType.DATAFLOW_SIDE_EFFECTING),
    )(*srcs, *lands, *deps)
    return dict(sems=list(outs[:2 * nc]), srcs=list(outs[2 * nc:2 * nc + n]),
                lands=list(outs[2 * nc + n:2 * nc + 2 * n]), token=outs[-1])


def _send_wait(h, *, scatter, after, name):
    n = len(h["srcs"])
    nc = n * N_PEER_CHIPS

    def body(*refs):
        ins, lnd = refs[:n], refs[n:2 * n]
        send_sems, recv_sems = refs[2 * n:2 * n + nc], refs[2 * n + nc:2 * n + 2 * nc]
        x, y, c = lax.axis_index("x"), lax.axis_index("y"), lax.axis_index("c")
        for a in range(n):
            for k, p in enumerate(_peer_chips(x, y, c)):
                slot = 2 * p[0] + p[1]
                j = a * N_PEER_CHIPS + k
                cp = pltpu.make_async_remote_copy(
                    src_ref=ins[a].at[slot] if scatter else ins[a], dst_ref=lnd[a].at[slot],
                    send_sem=send_sems[j], recv_sem=recv_sems[j], device_id=p, device_id_type=MESH)
                cp.wait_send()
                cp.wait_recv()

    thru = h["srcs"] + h["lands"]
    outs = pl.pallas_call(
        body, name=name, out_shape=tuple(pltpu.HBM(t.shape, t.dtype) for t in thru),
        in_specs=[HBM_SPEC] * (2 * n) + [SEM_SPEC] * (2 * nc) + [ANY] * len(after),
        out_specs=tuple([HBM_SPEC] * (2 * n)),
        input_output_aliases={i: i for i in range(2 * n)},
        compiler_params=pltpu.CompilerParams(has_side_effects=pltpu.SideEffectType.DATAFLOW_SIDE_EFFECTING),
    )(*thru, *h["sems"], *after)
    return list(outs[:n]), list(outs[n:])


def _assemble(land, own, chip, axis):
    return jnp.concatenate([jnp.where(chip == k, own, land[k]) for k in range(4)], axis=axis)


def _pack(arrs):
    flat = jnp.concatenate([a.reshape(-1).astype(F32) for a in arrs])
    tile = SUBLANES * LANES
    pad = (-flat.shape[0]) % tile
    return jnp.pad(flat, (0, pad)).reshape(-1, LANES)


def _unpack(buf, shapes):
    flat = buf.reshape(-1)
    out, off = [], 0
    for s in shapes:
        n = int(np.prod(s))
        out.append(flat[off:off + n].reshape(s))
        off += n
    return out


def _lower_bounds(hg_lower_bounds):
    p = jax.nn.softmax(hg_lower_bounds.astype(F32), axis=0)
    return jnp.cumsum(p, axis=0) - p[0]


def kernel(x, meta_tokens, hg_lower_bounds, w_in, w_conv, w_pool, pool_scale, hg_norm_g, w_o, ln1_g, ln1_b, w_up, w_ffn_conv, b_ffn_conv, w_down, ln2_g, ln2_b, loss_target, m_meta_tokens, m_hg_lower_bounds, m_w_in, m_w_conv, m_w_pool, m_pool_scale, m_hg_norm_g, m_w_o, m_ln1_g, m_ln1_b, m_w_up, m_w_ffn_conv, m_b_ffn_conv, m_w_down, m_ln2_g, m_ln2_b, v_meta_tokens, v_hg_lower_bounds, v_w_in, v_w_conv, v_w_pool, v_pool_scale, v_hg_norm_g, v_w_o, v_ln1_g, v_ln1_b, v_w_up, v_w_ffn_conv, v_b_ffn_conv, v_w_down, v_ln2_g, v_ln2_b):
    S = x.shape[1]
    R = S + ROW0
    Fq = w_down.shape[1]
    F = 4 * Fq
    F2 = 2 * F
    F2q = w_up.shape[2]
    assert x.shape == (1, S, D) and R % 384 == 0 and S % ROW0 == 0
    chip = 2 * lax.axis_index("x") + lax.axis_index("y")
    tm = 384
    tm_w = max(t for t in range(SUBLANES, 2113, SUBLANES) if R % t == 0)
    tb_ffn = 128

    small_shapes = [(N_META, D // 4), (DEPTH, CW // 4, 3), (DEPTH, F2q, 3)]
    wb_in, wb_o, wb_up, wb_down = (w.astype(MX) for w in (w_in, w_o, w_up, w_down))
    h_a = _send_start([wb_in[0], _pack([meta_tokens, w_conv, w_ffn_conv])], scatter=False, name="gather_a_start")
    (own_in, own_small), (l_in, l_small) = _send_wait(h_a, scatter=False, after=[h_a["token"]], name="gather_a_wait")
    h_b = _send_start([wb_o[0], wb_up[0], wb_down[0]], scatter=False, dep=l_in, name="gather_b_start")
    Win, Wo, Wup, Wdown = {}, {}, {}, {}
    Win[0] = _assemble(l_in, own_in, chip, 1)
    sm = [_unpack(jnp.where(chip == k, own_small, l_small[k]), small_shapes) for k in range(4)]
    meta_full = jnp.concatenate([sm[k][0] for k in range(4)], axis=1)
    wconv_full = jnp.concatenate([sm[k][1] for k in range(4)], axis=1)
    wffn_full = jnp.concatenate([sm[k][2] for k in range(4)], axis=1)
    taps_c = jnp.pad(wconv_full.transpose(0, 2, 1), ((0, 0), (0, SUBLANES - 3), (0, 0)))
    taps_f = jnp.pad(wffn_full.transpose(0, 2, 1), ((0, 0), (0, SUBLANES - 3), (0, 0)))
    wbd = jnp.stack([jax.scipy.linalg.block_diag(*[w_pool[l, g] for g in range(4)]) for l in range(DEPTH)]).astype(MX)
    lbs, lbs_vjp = jax.vjp(_lower_bounds, hg_lower_bounds)
    mat, suf = _hg_consts()

    def fwd_mixer(l, X, dep=None):
        h = _mm(X, Win[l], tm=tm, dep=dep, name=f"mm_in_{l}")
        y, o, sp = _mixer_fwd(h, taps_c[l], wbd[l], pool_scale[l].reshape(1, PW), hg_norm_g[l].reshape(1, HD),
                              lbs[l].reshape(1, HW), mat, name=f"mixer_fwd_{l}")
        return h, y, o, sp

    def fwd_rest(l, X, h, y, o, sp, dep=None):
        x1, xh1, r1 = _mm_ln(y, Wo[l], X, ln1_g[l], ln1_b[l], tm=tm, dep=dep, name=f"mm_o_ln_{l}")
        up, u, a = _mm_ffn_fwd(x1, Wup[l], taps_f[l], b_ffn_conv[l].reshape(1, F2), tm=tm // 2,
                               name=f"mm_up_ffn_{l}")
        x2, xh2, r2 = _mm_ln(a, Wdown[l], x1, ln2_g[l], ln2_b[l], tm=tm, name=f"mm_down_ln_{l}")
        return (X, h, y, o, sp, x1, xh1, r1, up, u, a, xh2, r2), x2

    X = jnp.concatenate([jnp.zeros((PADR, D), F32), meta_full, x[0]], axis=0)
    h, y, o, sp = fwd_mixer(0, X, dep=h_b["token"])
    (own_o, own_up, own_down), (l_o, l_up, l_down) = _send_wait(h_b, scatter=False, after=[y], name="gather_b_wait")
    Wo[0], Wup[0], Wdown[0] = (_assemble(l_o, own_o, chip, 0), _assemble(l_up, own_up, chip, 1),
                               _assemble(l_down, own_down, chip, 0))
    h_c = _send_start([wb_in[1], wb_o[1], wb_up[1], wb_down[1]], scatter=False, dep=l_o, name="gather_c_start")
    saved0, X1 = fwd_rest(0, X, h, y, o, sp, dep=h_c["token"])
    own_c, l_c = _send_wait(h_c, scatter=False, after=[X1], name="gather_c_wait")
    Win[1], Wo[1] = _assemble(l_c[0], own_c[0], chip, 1), _assemble(l_c[1], own_c[1], chip, 0)
    Wup[1], Wdown[1] = _assemble(l_c[2], own_c[2], chip, 1), _assemble(l_c[3], own_c[3], chip, 0)
    saved1, X2 = fwd_rest(1, X1, *fwd_mixer(1, X1))
    saved = [saved0, saved1]

    dxo, lacc = _loss(X2, loss_target[0], name="loss")
    loss = lax.psum(0.5 * jnp.sum(lacc[0]) / D, ("x", "y", "c"))

    sc = {}

    def scatter(nm, l, g, dep=None):
        sc[nm, l] = _send_start([g], scatter=True, dep=dep, name=f"scatter_{nm}_{l}_start")
        return sc[nm, l]["token"]

    tok = None

    small_g = [None] * DEPTH
    for l in reversed(range(DEPTH)):
        X, h, y, o, sp, x1, xh1, r1, up, u, a, xh2, r2 = saved[l]
        dz2, gb2, da = _ln_bwd_mm(dxo, xh2, r2, ln2_g[l], Wdown[l], tm=tm, dep=tok, name=f"ln2_bwd_da_{l}")
        tok = scatter("w_down", l, _wgrad(a, dz2, slabs_on_cols=False, tm=tm_w, tn=512, name=f"wgrad_down_{l}"))
        dup, facc = _ffn_bwd(da, up, u, taps_f[l], tb=tb_ffn, dep=tok, name=f"ffn_bwd_{l}")
        dx1 = _mm(dup, Wup[l], nt=True, res=dz2, tm=tm, name=f"mm_dx1_{l}")
        tok = scatter("w_up", l, _wgrad(x1, dup, slabs_on_cols=True, tm=tm_w, tn=F2q, name=f"wgrad_up_{l}"))
        dz1, gb1, dym = _ln_bwd_mm(dx1, xh1, r1, ln1_g[l], Wo[l], tm=tm, dep=tok, name=f"ln1_bwd_dym_{l}")
        tok = scatter("w_o", l, _wgrad(y, dz1, slabs_on_cols=False, tm=tm_w, tn=512, name=f"wgrad_o_{l}"))
        dh, dwc, dsc, dgn, dlb, dwbd = _mixer_bwd(
            h, dym, o, sp, taps_c[l], wbd[l], pool_scale[l].reshape(1, PW), hg_norm_g[l].reshape(1, HD),
            lbs[l].reshape(1, HW), mat, suf, dep=tok, name=f"mixer_bwd_{l}")
        tok = scatter("w_in", l, _wgrad(X, dh, slabs_on_cols=True, tm=tm_w, tn=DIN // 4, name=f"wgrad_in_{l}"))
        dX = _mm(dh, Win[l], nt=True, res=dz1, tm=tm, dep=tok, name=f"mm_dx_{l}")
        small_g[l] = dict(
            lbs=dlb[0], w_conv=dwc[0:3].T, w_pool=jnp.stack([dwbd[64 * g:64 * g + 64, 64 * g:64 * g + 64] for g in range(4)]),
            pool_scale=dsc[0], hg_norm_g=dgn[0], ln1_g=gb1[0], ln1_b=gb1[1], w_ffn_conv=facc[0:3].T,
            b_ffn_conv=facc[3], ln2_g=gb2[0], ln2_b=gb2[1])
        dxo = dX
    grad_x = dxo[ROW0:][None]

    sg_names = ["lbs", "w_conv", "w_pool", "pool_scale", "hg_norm_g", "ln1_g", "ln1_b", "w_ffn_conv",
                "b_ffn_conv", "ln2_g", "ln2_b"]
    sg_list = [dxo[PADR:ROW0]] + [jnp.stack([small_g[l][nm] for l in range(DEPTH)]) for nm in sg_names]
    sg_shapes = [a.shape for a in sg_list]
    packed = _pack(sg_list)
    (gathered,) = _exchange([packed], flips=ALL_FLIPS, n_slots=8, slot_of=_dev_slot, scatter=False, self_copy=True,
                            name="gather_small_grads")
    total = _sum_slots(gathered[:, None], name="sum_small_grads")[0]
    tot = dict(zip(["meta_tokens"] + sg_names, _unpack(total, sg_shapes)))
    (g_hg,) = lbs_vjp(tot["lbs"])
    small_grads = dict(
        meta_tokens=lax.dynamic_slice_in_dim(tot["meta_tokens"], chip * (D // 4), D // 4, axis=1),
        hg_lower_bounds=g_hg,
        w_conv=lax.dynamic_slice_in_dim(tot["w_conv"], chip * (CW // 4), CW // 4, axis=1),
        w_pool=tot["w_pool"], pool_scale=tot["pool_scale"], hg_norm_g=tot["hg_norm_g"],
        ln1_g=tot["ln1_g"], ln1_b=tot["ln1_b"],
        w_ffn_conv=lax.dynamic_slice_in_dim(tot["w_ffn_conv"], chip * F2q, F2q, axis=1),
        b_ffn_conv=tot["b_ffn_conv"], ln2_g=tot["ln2_g"], ln2_b=tot["ln2_b"])
    small_w = dict(meta_tokens=(meta_tokens, m_meta_tokens, v_meta_tokens),
                   hg_lower_bounds=(hg_lower_bounds, m_hg_lower_bounds, v_hg_lower_bounds),
                   w_conv=(w_conv, m_w_conv, v_w_conv), w_pool=(w_pool, m_w_pool, v_w_pool),
                   pool_scale=(pool_scale, m_pool_scale, v_pool_scale), hg_norm_g=(hg_norm_g, m_hg_norm_g, v_hg_norm_g),
                   ln1_g=(ln1_g, m_ln1_g, v_ln1_g), ln1_b=(ln1_b, m_ln1_b, v_ln1_b),
                   w_ffn_conv=(w_ffn_conv, m_w_ffn_conv, v_w_ffn_conv), b_ffn_conv=(b_ffn_conv, m_b_ffn_conv, v_b_ffn_conv),
                   ln2_g=(ln2_g, m_ln2_g, v_ln2_g), ln2_b=(ln2_b, m_ln2_b, v_ln2_b))
    names_s = list(small_w)
    shapes_s = [small_w[nm][0].shape for nm in names_s]
    pk = [_pack([small_w[nm][j] for nm in names_s])[None] for j in range(3)]
    pg = _pack([small_grads[nm] for nm in names_s])[None]
    outs_s = _adamw(pk[0], pk[1], pk[2], pg[0], None, layer=0, prev=None, name="adamw_small")
    small = {nm: [] for nm in names_s}
    for j in range(4):
        for nm, val in zip(names_s, _unpack(outs_s[j][0], shapes_s)):
            small[nm].append(val)

    big_w = dict(w_in=(w_in, m_w_in, v_w_in), w_o=(w_o, m_w_o, v_w_o), w_up=(w_up, m_w_up, v_w_up),
                 w_down=(w_down, m_w_down, v_w_down))
    big = {nm: None for nm in big_w}
    for l in reversed(range(DEPTH)):
        for group in ((("w_down", "w_up", "w_o", "w_in"),) if l else (("w_down", "w_up", "w_o"), ("w_in",))):
            last = group == ("w_in",)
            after = [outs_s[0]] + [big[k][0] for k in big_w if k != "w_in"] if last else [dxo]
            part = []
            for nm in group:
                (own,), (recv,) = _send_wait(sc[nm, l], scatter=True, after=after, name=f"scatter_{nm}_{l}_wait")
                part.append(_sum_own(recv, own, chip, name=f"sum_{nm}_{l}"))
            sib = _exchange(part, flips=[(0, 0, 1)], n_slots=1, slot_of=_zero_slot, scatter=False, self_copy=False,
                            name=f"swap_cores_{l}{'_in' if last else ''}")
            for k, nm in enumerate(group):
                w, m, v = big_w[nm]
                big[nm] = _adamw(w, m, v, part[k], sib[k][0], layer=l, prev=big[nm], name=f"adamw_{nm}_{l}")

    order = ["meta_tokens", "hg_lower_bounds", "w_in", "w_conv", "w_pool", "pool_scale", "hg_norm_g", "w_o",
             "ln1_g", "ln1_b", "w_up", "w_ffn_conv", "b_ffn_conv", "w_down", "ln2_g", "ln2_b"]
    res = {nm: (big[nm] if nm in big else small[nm]) for nm in order}
    outs = [loss, grad_x]
    for j in range(4):
        outs += [res[nm][j] for nm in order]
    return tuple(outs)
```

```python
import numpy as np

import jax
import jax.numpy as jnp
from jax import lax
from jax.experimental import pallas as pl
from jax.experimental.pallas import tpu as pltpu

F32 = jnp.float32
BF = jnp.bfloat16
MX = jnp.bfloat16

D = 1024
CW = 256
HW = 512
HD = 128
NH = 4
PW = 256
DIN = 3072
N_META = 16
CH = 64
MB = 4
BM = MB * CH
ROW0 = 256
PADR = ROW0 - N_META
HALO = 16
FH = 8
LEVELS = (32, 16, 8, 4, 2, 1)
DEPTH = 2
ALPHA = (2 * DEPTH) ** 0.25
LN_EPS = 1e-5
RMS_EPS = 1e-6
F_FLOOR = 1e-30
SCALE = HD ** -0.5
ADAM_LR, ADAM_B1, ADAM_B2, ADAM_EPS, ADAM_WD, ADAM_STEP = 0.001, 0.9, 0.999, 1e-08, 0.01, 10

VMEM_V7X = 64 * 2 ** 20
LANES = 128
SUBLANES = 8

NN = (((1,), (0,)), ((), ()))
NT = (((1,), (1,)), ((), ()))
TN = (((0,), (0,)), ((), ()))
MESH = pl.DeviceIdType.MESH
ANY = pl.BlockSpec(memory_space=pl.ANY)


def _dot(a, b, dn):
    return lax.dot_general(a.astype(MX), b.astype(MX), dn, preferred_element_type=F32)


def _cp(sem, est_bytes):
    lim = int(min(VMEM_V7X - 6 * 2 ** 20, max(32 * 2 ** 20, est_bytes)))
    return pltpu.CompilerParams(dimension_semantics=sem, vmem_limit_bytes=lim)


def _nbytes(shape, dtype):
    return int(np.prod(shape)) * jnp.dtype(dtype).itemsize


def _row_tile(rows, row_bytes, budget):
    best = SUBLANES
    for t in range(SUBLANES, rows + 1, SUBLANES):
        if rows % t == 0 and t * row_bytes <= budget:
            best = t
    return best


def _after(dep, body, in_specs, args):
    if dep is None:
        return body, list(in_specs), list(args)

    def body_after(dep_ref, *refs):
        body(*refs)

    return body_after, [ANY] + list(in_specs), [dep] + list(args)


def _mm(a, w, *, nt=False, res=None, tm, out_dtype=F32, zero_inert=False, dep=None, name):
    R, K = a.shape
    N = w.shape[0] if nt else w.shape[1]
    dn = NT if nt else NN

    def body(*refs):
        if res is None:
            a_ref, w_ref, o_ref = refs
        else:
            a_ref, w_ref, r_ref, o_ref = refs
        acc = _dot(a_ref[...], w_ref[...], dn)
        if res is not None:
            acc = acc + ALPHA * r_ref[...]
        if zero_inert:
            acc = jnp.where(_row_ids(pl.program_id(0) * tm, tm) >= PADR, acc, 0.0)
        o_ref[...] = acc.astype(out_dtype)

    in_specs = [pl.BlockSpec((tm, K), lambda i: (i, 0)), pl.BlockSpec(w.shape, lambda i: (0, 0))]
    args = [a, w]
    est = 2 * _nbytes((tm, K), a.dtype) + 2 * _nbytes(w.shape, w.dtype) + 3 * _nbytes((tm, N), F32)
    if res is not None:
        in_specs.append(pl.BlockSpec((tm, N), lambda i: (i, 0)))
        args.append(res)
        est += 2 * _nbytes((tm, N), F32)
    body, in_specs, args = _after(dep, body, in_specs, args)
    return pl.pallas_call(
        body, name=name, grid=(R // tm,), in_specs=in_specs,
        out_specs=pl.BlockSpec((tm, N), lambda i: (i, 0)),
        out_shape=jax.ShapeDtypeStruct((R, N), out_dtype),
        compiler_params=_cp(("parallel",), est + 4 * 2 ** 20),
    )(*args)


def _wgrad(a, b, *, slabs_on_cols, tm, tn, name):
    R, Ka = a.shape
    Nb = b.shape[1]
    if slabs_on_cols:
        out_shape = (4, Ka, Nb // 4)
        assert tn == Nb // 4
        out_spec = pl.BlockSpec((None, Ka, tn), lambda j, i: (j, 0, 0))
    else:
        out_shape = (4, Ka // 4, Nb)
        out_spec = pl.BlockSpec((4, Ka // 4, tn), lambda j, i: (0, 0, j))

    def body(a_ref, b_ref, o_ref):
        @pl.when(pl.program_id(1) == 0)
        def _():
            o_ref[...] = jnp.zeros(o_ref.shape, F32)

        acc = _dot(a_ref[...], b_ref[...], TN)
        o_ref[...] += acc.reshape(o_ref.shape)

    in_specs = [pl.BlockSpec((tm, Ka), lambda j, i: (i, 0)), pl.BlockSpec((tm, tn), lambda j, i: (i, j))]
    est = 2 * _nbytes((tm, Ka), a.dtype) + 2 * _nbytes((tm, tn), b.dtype) + 4 * _nbytes((Ka, tn), F32) \
        + _nbytes((tm, Ka), F32)
    return pl.pallas_call(
        body, name=name, grid=(Nb // tn, R // tm), in_specs=in_specs, out_specs=out_spec,
        out_shape=jax.ShapeDtypeStruct(out_shape, F32),
        compiler_params=_cp(("parallel", "arbitrary"), est + 4 * 2 ** 20),
    )(a, b)


def _mm_ln(a, w, xres, g, b, *, tm, dep=None, name):
    R, K = a.shape

    def body(a_ref, w_ref, x_ref, g_ref, b_ref, xo_ref, xh_ref, r_ref):
        z = ALPHA * x_ref[...] + _dot(a_ref[...], w_ref[...], NN)
        mu = jnp.mean(z, axis=-1, keepdims=True)
        zc = z - mu
        var = jnp.mean(zc * zc, axis=-1, keepdims=True)
        r = lax.rsqrt(var + LN_EPS)
        xh = zc * r
        xh_ref[...] = xh
        r_ref[...] = r
        xo_ref[...] = xh * g_ref[...] + b_ref[...]

    row = lambda i: (i, 0)
    fix = lambda i: (0, 0)
    est = 2 * _nbytes((tm, K), a.dtype) + 2 * _nbytes(w.shape, w.dtype) + 10 * _nbytes((tm, D), F32)
    body, in_specs, args = _after(
        dep, body, [pl.BlockSpec((tm, K), row), pl.BlockSpec(w.shape, fix), pl.BlockSpec((tm, D), row),
                    pl.BlockSpec((1, D), fix), pl.BlockSpec((1, D), fix)],
        [a, w, xres, g.reshape(1, D), b.reshape(1, D)])
    return pl.pallas_call(
        body, name=name, grid=(R // tm,), in_specs=in_specs,
        out_specs=[pl.BlockSpec((tm, D), row), pl.BlockSpec((tm, D), row), pl.BlockSpec((tm, 1), row)],
        out_shape=[jax.ShapeDtypeStruct((R, D), F32), jax.ShapeDtypeStruct((R, D), F32),
                   jax.ShapeDtypeStruct((R, 1), F32)],
        compiler_params=_cp(("parallel",), est + 4 * 2 ** 20),
    )(*args)


def _resident(shape):
    return pl.BlockSpec(shape, lambda i: (0,) * len(shape), pipeline_mode=pl.Buffered(1))


def _ln_bwd_mm(dx, xh, r, g, w, *, tm, tgt=None, dep=None, name):
    R = dx.shape[0]
    N = w.shape[0]
    assert tgt is None or tm == ROW0

    def body(*refs):
        if tgt is None:
            dx_ref, xh_ref, r_ref, g_ref, w_ref, dz_ref, gb_ref, o_ref = refs
        else:
            dx_ref, t_ref, xh_ref, r_ref, g_ref, w_ref, dz_ref, gb_ref, o_ref, sq_ref = refs
        i = pl.program_id(0)

        @pl.when(i == 0)
        def _():
            gb_ref[...] = jnp.zeros(gb_ref.shape, F32)
            if tgt is not None:
                sq_ref[...] = jnp.zeros(sq_ref.shape, F32)

        if tgt is None:
            dxv = dx_ref[...]
        else:
            err = jnp.where(i > 0, dx_ref[...] - t_ref[...], 0.0)
            sq_ref[0:1, :] += jnp.sum(err * err, axis=0, keepdims=True)
            dxv = err / D
        xhv = xh_ref[...]
        dyh = dxv * g_ref[...]
        m1 = jnp.mean(dyh, axis=-1, keepdims=True)
        m2 = jnp.mean(dyh * xhv, axis=-1, keepdims=True)
        dz = r_ref[...] * (dyh - m1 - xhv * m2)
        dz_ref[...] = dz
        gb_ref[0:1, :] += jnp.sum(dxv * xhv, axis=0, keepdims=True)
        gb_ref[1:2, :] += jnp.sum(dxv, axis=0, keepdims=True)
        o_ref[...] = _dot(dz, w_ref[...], NT)

    row = lambda i: (i, 0)
    fix = lambda i: (0, 0)
    in_specs = [pl.BlockSpec((tm, D), row), pl.BlockSpec((tm, D), row), pl.BlockSpec((tm, 1), row),
                pl.BlockSpec((1, D), fix), _resident(w.shape)]
    args = [dx, xh, r, g.reshape(1, D), w]
    out_specs = [pl.BlockSpec((tm, D), row), pl.BlockSpec((SUBLANES, D), fix), pl.BlockSpec((tm, N), row)]
    out_shape = [jax.ShapeDtypeStruct((R, D), F32), jax.ShapeDtypeStruct((SUBLANES, D), F32),
                 jax.ShapeDtypeStruct((R, N), F32)]
    if tgt is not None:
        in_specs.insert(1, pl.BlockSpec((tm, D), lambda i: (jnp.maximum(i - 1, 0), 0)))
        args.insert(1, tgt)
        out_specs.append(pl.BlockSpec((SUBLANES, D), fix))
        out_shape.append(jax.ShapeDtypeStruct((SUBLANES, D), F32))
    body, in_specs, args = _after(dep, body, in_specs, args)
    est = _nbytes(w.shape, w.dtype) + 16 * _nbytes((tm, D), F32) + 4 * _nbytes((tm, N), F32)
    return pl.pallas_call(
        body, name=name, grid=(R // tm,), in_specs=in_specs, out_specs=out_specs, out_shape=out_shape,
        compiler_params=_cp(("arbitrary",), est + 4 * 2 ** 20),
    )(*args)


def _mm_dx_head(dh, w, res, *, dep=None, name):
    R, K = dh.shape
    tm = ROW0

    def body(a_ref, w_ref, r_ref, head_ref, seq_ref):
        i = pl.program_id(0)
        acc = _dot(a_ref[...], w_ref[...], NT) + ALPHA * r_ref[...]

        @pl.when(i == 0)
        def _():
            head_ref[...] = acc

        @pl.when(i > 0)
        def _():
            seq_ref[...] = acc

    row = lambda i: (i, 0)
    body, in_specs, args = _after(
        dep, body, [pl.BlockSpec((tm, K), row), _resident(w.shape), pl.BlockSpec((tm, D), row)], [dh, w, res])
    est = _nbytes(w.shape, w.dtype) + 2 * _nbytes((tm, K), dh.dtype) + 8 * _nbytes((tm, D), F32)
    return pl.pallas_call(
        body, name=name, grid=(R // tm,), in_specs=in_specs,
        out_specs=[pl.BlockSpec((tm, D), lambda i: (0, 0)), pl.BlockSpec((tm, D), lambda i: (jnp.maximum(i - 1, 0), 0))],
        out_shape=[jax.ShapeDtypeStruct((ROW0, D), F32), jax.ShapeDtypeStruct((R - ROW0, D), F32)],
        compiler_params=_cp(("arbitrary",), est + 4 * 2 ** 20),
    )(*args)


def _mm_ffn_fwd(x1, w, taps, bias, *, tm, name):
    R, K = x1.shape
    F2 = w.shape[1]
    F = F2 // 2

    def body(x_ref, w_ref, t_ref, b_ref, up_ref, u_ref, a_ref, carry_ref):
        i = pl.program_id(0)

        @pl.when(i == 0)
        def _():
            carry_ref[...] = jnp.zeros(carry_ref.shape, F32)

        acc = _dot(x_ref[...], w_ref[...], NN)
        acc = jnp.where(_row_ids(i * tm, tm) >= PADR, acc, 0.0)
        up_ref[...] = acc.astype(up_ref.dtype)
        ue = jnp.concatenate([carry_ref[...], acc], axis=0)
        carry_ref[...] = acc[tm - FH:tm]
        u = (t_ref[0:1, :] * pltpu.roll(ue, 2, 0)[FH:] + t_ref[1:2, :] * pltpu.roll(ue, 1, 0)[FH:]
             + t_ref[2:3, :] * acc + b_ref[...])
        u_ref[...] = u
        gate = u[:, :F]
        val = u[:, F:]
        a_ref[...] = (gate * jax.nn.sigmoid(gate) * val).astype(a_ref.dtype)

    row = lambda i: (i, 0)
    est = _nbytes(w.shape, w.dtype) + 2 * _nbytes((tm, K), x1.dtype) + 10 * _nbytes((tm + FH, F2), F32)
    return pl.pallas_call(
        body, name=name, grid=(R // tm,),
        in_specs=[pl.BlockSpec((tm, K), row), _resident(w.shape), _resident((SUBLANES, F2)), _resident((1, F2))],
        out_specs=[pl.BlockSpec((tm, F2), row), pl.BlockSpec((tm, F2), row), pl.BlockSpec((tm, F), row)],
        out_shape=[jax.ShapeDtypeStruct((R, F2), MX), jax.ShapeDtypeStruct((R, F2), F32),
                   jax.ShapeDtypeStruct((R, F), MX)],
        scratch_shapes=[pltpu.VMEM((FH, F2), F32)],
        compiler_params=_cp(("arbitrary",), est + 4 * 2 ** 20),
    )(x1, w, taps, bias)


def _row_ids(start, n):
    return start + lax.broadcasted_iota(jnp.int32, (n, 1), 0)


def _ffn_bwd(da, up, u, taps, *, tb, dep=None, name):
    R, F2 = up.shape
    F = F2 // 2
    nh = tb // FH
    nb = R // tb
    last = R // FH - 1
    m = tb + FH

    def body(da_ref, dan_ref, x_ref, u_ref, un_ref, w_ref, du_ref, acc_ref):
        i = pl.program_id(0)

        @pl.when(i == 0)
        def _():
            acc_ref[...] = jnp.zeros(acc_ref.shape, F32)

        inside = (i < nb - 1).astype(F32)
        w0, w1, w2 = w_ref[0:1, :], w_ref[1:2, :], w_ref[2:3, :]
        u = jnp.concatenate([u_ref[...], un_ref[...]], axis=0)
        dae = jnp.concatenate([da_ref[...], dan_ref[...] * inside], axis=0)
        gate = u[:, :F]
        val = u[:, F:]
        sg = jax.nn.sigmoid(gate)
        gs = gate * sg
        du = jnp.concatenate([dae * val * (sg * (1.0 + gate - gs)), dae * gs], axis=1)
        du0 = du[0:tb]
        du1 = pltpu.roll(du, m - 1, 0)[0:tb]
        du2 = pltpu.roll(du, m - 2, 0)[0:tb]
        du_ref[...] = (w2 * du0 + w1 * du1 + w0 * du2).astype(du_ref.dtype)

        @pl.when(i * tb < PADR)
        def _():
            row = _row_ids(i * tb, tb)
            du_ref[...] = jnp.where(row >= PADR, du_ref[...], jnp.zeros((), du_ref.dtype))

        x = x_ref[...].astype(F32)
        acc_ref[0:1, :] += jnp.sum(du2 * x, axis=0, keepdims=True)
        acc_ref[1:2, :] += jnp.sum(du1 * x, axis=0, keepdims=True)
        acc_ref[2:3, :] += jnp.sum(du0 * x, axis=0, keepdims=True)
        acc_ref[3:4, :] += jnp.sum(du0, axis=0, keepdims=True)

    nxt = lambda i: (jnp.minimum(i * nh + nh, last), 0)
    cur = lambda i: (i, 0)
    fix = lambda i: (0, 0)
    est = 12 * _nbytes((tb + 2 * FH, F2), F32)
    body, in_specs, args = _after(
        dep, body, [pl.BlockSpec((tb, F), cur), pl.BlockSpec((FH, F), nxt),
                    pl.BlockSpec((tb, F2), cur), pl.BlockSpec((tb, F2), cur), pl.BlockSpec((FH, F2), nxt),
                    pl.BlockSpec((SUBLANES, F2), fix)], [da, da, up, u, u, taps])
    return pl.pallas_call(
        body, name=name, grid=(R // tb,), in_specs=in_specs,
        out_specs=[pl.BlockSpec((tb, F2), cur), pl.BlockSpec((SUBLANES, F2), fix)],
        out_shape=[jax.ShapeDtypeStruct((R, F2), MX), jax.ShapeDtypeStruct((SUBLANES, F2), F32)],
        compiler_params=_cp(("arbitrary",), est),
    )(*args)


def _hg_consts():
    t = np.arange(CH)[:, None]
    j = np.arange(CH)[None, :]
    low = (j <= t).astype(np.float32)
    blocks = [low, (j > t).astype(np.float32)]
    for m in LEVELS:
        ref = (t // (2 * m)) * 2 * m + m - 1
        blocks.append(low - (j <= ref).astype(np.float32))
    mat = np.concatenate(blocks, axis=0)
    suf = (j >= t).astype(np.float32)
    return jnp.asarray(mat, BF), jnp.asarray(suf, BF)


def _split_dot(mat, x):
    hi = x.astype(BF)
    lo = (x - hi.astype(F32)).astype(BF)
    return (lax.dot_general(mat, hi, NN, preferred_element_type=F32)
            + lax.dot_general(mat, lo, NN, preferred_element_type=F32))


def _lane_select(a2, a4, a8, a16):
    lane = lax.broadcasted_iota(jnp.int32, (1, PW), 1)
    return jnp.where(lane < 64, a2, jnp.where(lane < 128, a4, jnp.where(lane < 192, a8, a16)))


def _pool_count(row):
    win = _lane_select(2.0, 4.0, 8.0, 16.0)
    t1 = jnp.maximum((row - PADR + 1).astype(F32), 1.0)
    return jnp.minimum(t1, win)


def _gates(fz, lb, valid):
    sig = jax.nn.sigmoid(fz)
    f = lb + (1.0 - lb) * sig
    lf = jnp.where(valid, jnp.log(jnp.maximum(f, F_FLOOR)), 0.0)
    kk = jnp.where(valid, (1.0 - lb) * (1.0 - sig), 0.0)
    return sig, f, lf, kk


def _level_masks():
    tt = lax.broadcasted_iota(jnp.int32, (CH, CH), 0)
    ss = lax.broadcasted_iota(jnp.int32, (CH, CH), 1)
    xr = tt ^ ss
    low = tt > ss
    return tt, ss, [(xr >= m) & (xr < 2 * m) & low for m in LEVELS]


def _intra(q, kk, ex, sl, tt, ss, masks):
    a = jnp.where(tt == ss, jnp.sum(q * kk, axis=-1, keepdims=True), 0.0)
    parts = []
    for l in range(len(LEVELS)):
        e = jnp.exp(-jnp.abs(ex[(2 + l) * CH:(3 + l) * CH, sl]))
        ql = (q * e).astype(MX)
        kl = (kk * e).astype(MX)
        a = a + jnp.where(masks[l], _dot(ql, kl, NT), 0.0)
        parts.append((e, ql, kl))
    return a, parts


def _mixer_fwd(h, taps, wbd, ps, gn, lb, mat, *, name):
    R = h.shape[0]
    nb = R // BM
    q4 = BM // HALO

    def body(h_ref, hp_ref, w_ref, wbd_ref, ps_ref, gn_ref, lb_ref, mat_ref, y_ref, o_ref, sp_ref, st_ref):
        i = pl.program_id(0)

        @pl.when(i == 0)
        def _():
            st_ref[...] = jnp.zeros(st_ref.shape, F32)

        row = _row_ids(i * BM, BM)
        valid_b = row >= PADR
        rowe = _row_ids(i * BM - HALO, BM + HALO)
        valide = rowe >= PADR
        se = jnp.concatenate([hp_ref[:, 256:512] * hp_ref[:, 512:768], h_ref[:, 256:512] * h_ref[:, 512:768]], axis=0)
        se = jnp.where(valide, se, 0.0)
        conv = (w_ref[0:1, :] * pltpu.roll(se, 2, 0) + w_ref[1:2, :] * pltpu.roll(se, 1, 0)
                + w_ref[2:3, :] * se)[HALO:]
        y_ref[:, 0:CW] = (h_ref[:, 0:256] * conv).astype(y_ref.dtype)
        ve = jnp.where(valide, jnp.concatenate([hp_ref[:, 2816:3072], h_ref[:, 2816:3072]], axis=0), 0.0)
        s2 = ve + pltpu.roll(ve, 1, 0)
        s4 = s2 + pltpu.roll(s2, 2, 0)
        s8 = s4 + pltpu.roll(s4, 4, 0)
        s16 = s8 + pltpu.roll(s8, 8, 0)
        dp = (_lane_select(s2, s4, s8, s16) / _pool_count(rowe) - ve)[HALO:]
        y_ref[:, CW + HW:D] = (_dot(dp, wbd_ref[...], NN) * ps_ref[...]).astype(y_ref.dtype)
        lbv = lb_ref[...]
        tt, ss, masks = _level_masks()
        for c in range(MB):
            rs = slice(c * CH, (c + 1) * CH)
            _, _, lf, kk = _gates(h_ref[rs, 1280:1792], lbv, valid_b[rs])
            q = h_ref[rs, 768:1280] * SCALE
            ii = h_ref[rs, 1792:2304]
            gz = h_ref[rs, 2304:2816]
            ex = _split_dot(mat_ref[...], lf)
            eg = jnp.exp(ex[0:CH])
            q0 = q * eg
            kr = kk * jnp.exp(ex[CH:2 * CH])
            for hd in range(NH):
                sl = slice(HD * hd, HD * (hd + 1))
                st = st_ref[hd]
                sp_ref[c, hd] = st
                a, _ = _intra(q[:, sl], kk[:, sl], ex, sl, tt, ss, masks)
                o = _dot(q0[:, sl], st, NT) + _dot(a, ii[:, sl], NN)
                st_ref[hd] = st * eg[CH - 1:CH, sl] + _dot(ii[:, sl], kr[:, sl], TN)
                o_ref[rs, sl] = o
                r = lax.rsqrt(jnp.mean(o * o, axis=-1, keepdims=True) + RMS_EPS)
                g_ = gz[:, sl]
                y_ref[rs, CW + HD * hd:CW + HD * (hd + 1)] = (
                    o * r * gn_ref[...] * (g_ * jax.nn.sigmoid(g_))).astype(y_ref.dtype)

    fix = lambda i: (0, 0)
    return pl.pallas_call(
        body, name=name, grid=(nb,),
        in_specs=[pl.BlockSpec((BM, DIN), lambda i: (i, 0)),
                  pl.BlockSpec((HALO, DIN), lambda i: (jnp.maximum(i * q4 - 1, 0), 0)),
                  pl.BlockSpec((SUBLANES, CW), fix), pl.BlockSpec((PW, PW), fix), pl.BlockSpec((1, PW), fix),
                  pl.BlockSpec((1, HD), fix), pl.BlockSpec((1, HW), fix), pl.BlockSpec(mat.shape, fix)],
        out_specs=[pl.BlockSpec((BM, D), lambda i: (i, 0)), pl.BlockSpec((BM, HW), lambda i: (i, 0)),
                   pl.BlockSpec((MB, NH, HD, HD), lambda i: (i, 0, 0, 0))],
        out_shape=[jax.ShapeDtypeStruct((R, D), MX), jax.ShapeDtypeStruct((R, HW), F32),
                   jax.ShapeDtypeStruct((R // CH, NH, HD, HD), F32)],
        scratch_shapes=[pltpu.VMEM((NH, HD, HD), F32)],
        compiler_params=_cp(("arbitrary",), 32 * 2 ** 20),
    )(h, h, taps, wbd, ps, gn, lb, mat)


def _mixer_bwd(h, dy, o, sp, taps, wbd, ps, gn, lb, mat, suf, *, dep=None, name):
    R = h.shape[0]
    nb = R // BM
    q4 = BM // HALO
    lasth = R // HALO - 1

    def body(h_ref, hp_ref, hn_ref, dy_ref, dyn_ref, o_ref, sp_ref, w_ref, wbd_ref, ps_ref, gn_ref, lb_ref,
             mat_ref, suf_ref, dh_ref, dw_ref, dsc_ref, dgn_ref, dlb_ref, dwbd_ref, dst_ref):
        i = pl.program_id(0)
        b = nb - 1 - i

        @pl.when(i == 0)
        def _():
            dst_ref[...] = jnp.zeros(dst_ref.shape, F32)
            dw_ref[...] = jnp.zeros(dw_ref.shape, F32)
            dsc_ref[...] = jnp.zeros(dsc_ref.shape, F32)
            dgn_ref[...] = jnp.zeros(dgn_ref.shape, F32)
            dlb_ref[...] = jnp.zeros(dlb_ref.shape, F32)
            dwbd_ref[...] = jnp.zeros(dwbd_ref.shape, F32)

        ne = BM + 2 * HALO
        nc = BM + HALO
        row = _row_ids(b * BM, BM)
        valid = row >= PADR
        rowe = _row_ids(b * BM - HALO, ne)
        valide = (rowe >= PADR) & (rowe < R)
        rown = rowe[HALO:]
        validn = rown < R

        def cat3(lo, hi):
            return jnp.concatenate([hp_ref[:, lo:hi], h_ref[:, lo:hi], hn_ref[:, lo:hi]], axis=0)

        def catd(lo, hi):
            return jnp.where(validn, jnp.concatenate([dy_ref[:, lo:hi], dyn_ref[:, lo:hi]], axis=0), 0.0)

        w0, w1, w2 = w_ref[0:1, :], w_ref[1:2, :], w_ref[2:3, :]
        cce = cat3(256, 512)
        cve = cat3(512, 768)
        se = jnp.where(valide, cce * cve, 0.0)
        sm2 = pltpu.roll(se, 2, 0)
        sm1 = pltpu.roll(se, 1, 0)
        conv = (w0 * sm2 + w1 * sm1 + w2 * se)[HALO:HALO + BM]
        cbn = jnp.concatenate([h_ref[:, 0:256], hn_ref[:, 0:256]], axis=0)
        dconv = catd(0, CW) * cbn
        ds = w2 * dconv + w1 * pltpu.roll(dconv, nc - 1, 0) + w0 * pltpu.roll(dconv, nc - 2, 0)
        ds = jnp.where(valid, ds[0:BM], 0.0)
        dcv = dconv[0:BM]
        dw_ref[0:1, :] += jnp.sum(dcv * sm2[HALO:HALO + BM], axis=0, keepdims=True)
        dw_ref[1:2, :] += jnp.sum(dcv * sm1[HALO:HALO + BM], axis=0, keepdims=True)
        dw_ref[2:3, :] += jnp.sum(dcv * se[HALO:HALO + BM], axis=0, keepdims=True)
        dh_ref[:, 0:256] = jnp.where(valid, dy_ref[:, 0:CW] * conv, 0.0).astype(dh_ref.dtype)
        dh_ref[:, 256:512] = (ds * h_ref[:, 512:768]).astype(dh_ref.dtype)
        dh_ref[:, 512:768] = (ds * h_ref[:, 256:512]).astype(dh_ref.dtype)

        ve = jnp.where(valide, cat3(2816, 3072), 0.0)
        s2 = ve + pltpu.roll(ve, 1, 0)
        s4 = s2 + pltpu.roll(s2, 2, 0)
        s8 = s4 + pltpu.roll(s4, 4, 0)
        s16 = s8 + pltpu.roll(s8, 8, 0)
        cnte = _pool_count(rowe)
        dp = (_lane_select(s2, s4, s8, s16) / cnte - ve)[HALO:HALO + BM]
        dyp = catd(CW + HW, D)
        pre = _dot(dp, wbd_ref[...], NN)
        dsc_ref[0:1, :] += jnp.sum(dyp[0:BM] * pre, axis=0, keepdims=True)
        dyps = dyp * ps_ref[...]
        dd = _dot(dyps, wbd_ref[...], NT)
        dwbd_ref[...] += _dot(dp, dyps[0:BM], TN)
        e = dd / cnte[HALO:]
        t2 = e + pltpu.roll(e, nc - 1, 0)
        t4 = t2 + pltpu.roll(t2, nc - 2, 0)
        t8 = t4 + pltpu.roll(t4, nc - 4, 0)
        t16 = t8 + pltpu.roll(t8, nc - 8, 0)
        dv = (_lane_select(t2, t4, t8, t16) - dd)[0:BM]
        dh_ref[:, 2816:3072] = jnp.where(valid, dv, 0.0).astype(dh_ref.dtype)

        lbv = lb_ref[...]
        tt, ss, masks = _level_masks()
        gnv = gn_ref[...]
        for c in reversed(range(MB)):
            rs = slice(c * CH, (c + 1) * CH)
            vc = valid[rs]
            sig, f, lf, kk = _gates(h_ref[rs, 1280:1792], lbv, vc)
            q = h_ref[rs, 768:1280] * SCALE
            ii = h_ref[rs, 1792:2304]
            gz = h_ref[rs, 2304:2816]
            ex = _split_dot(mat_ref[...], lf)
            eg = jnp.exp(ex[0:CH])
            egr = jnp.exp(ex[CH:2 * CH])
            q0 = q * eg
            kr = kk * egr
            dqs, dks, dis, dgzs, tails, dbs = [], [], [], [], [], []
            for hd in range(NH):
                sl = slice(HD * hd, HD * (hd + 1))
                ov = o_ref[rs, sl]
                r = lax.rsqrt(jnp.mean(ov * ov, axis=-1, keepdims=True) + RMS_EPS)
                oh = ov * r
                g_ = gz[:, sl]
                sg = jax.nn.sigmoid(g_)
                dyv = dy_ref[rs, CW + HD * hd:CW + HD * (hd + 1)]
                don = dyv * (g_ * sg)
                dgzs.append(dyv * (oh * gnv) * (sg * (1.0 + g_ * (1.0 - sg))))
                dgn_ref[0:1, :] += jnp.sum(don * oh, axis=0, keepdims=True)
                doh = don * gnv
                do = r * (doh - oh * jnp.mean(doh * oh, axis=-1, keepdims=True))
                st = sp_ref[c, hd]
                dst = dst_ref[hd]
                qh, kh, ih = q[:, sl], kk[:, sl], ii[:, sl]
                a, parts = _intra(qh, kh, ex, sl, tt, ss, masks)
                da = jnp.where(tt >= ss, _dot(do, ih, NT), 0.0)
                dis.append(_dot(a, do, TN) + _dot(kr[:, sl], dst, NT))
                q0h = q0[:, sl].astype(MX)
                krh = kr[:, sl].astype(MX)
                dq0 = _dot(do, st, NN)
                dkr = _dot(ih, dst, NN)
                dq = dq0 * eg[:, sl]
                dk = dkr * egr[:, sl]
                kdk = krh.astype(F32) * dkr
                db = q0h.astype(F32) * dq0 - kdk
                tails.append(jnp.sum(kdk, axis=0, keepdims=True)
                             + eg[CH - 1:CH, sl] * jnp.sum(dst * st, axis=0, keepdims=True))
                dga = jnp.sum(jnp.where(tt == ss, da, 0.0), axis=-1, keepdims=True)
                dq = dq + dga * kh
                dk = dk + dga * qh
                for l in range(len(LEVELS)):
                    e_l, ql, kl = parts[l]
                    dpl = jnp.where(masks[l], da, 0.0).astype(MX)
                    dql = _dot(dpl, kl, NN)
                    dkl = _dot(dpl, ql, TN)
                    dq = dq + dql * e_l
                    dk = dk + dkl * e_l
                    db = db + (ql.astype(F32) * dql - kl.astype(F32) * dkl)
                dst_ref[hd] = dst * eg[CH - 1:CH, sl] + _dot(do, q0h, TN)
                dqs.append(dq)
                dks.append(dk)
                dbs.append(db)
            dq = jnp.concatenate(dqs, axis=1)
            dk = jnp.concatenate(dks, axis=1)
            db = jnp.concatenate(dbs, axis=1)
            dlf = _split_dot(suf_ref[...], db) + jnp.concatenate(tails, axis=1)
            t = jnp.where(vc, dlf * jnp.where(f > F_FLOOR, 1.0 / f, 0.0) - dk, 0.0)
            dlb_ref[0:1, :] += jnp.sum(t * (1.0 - sig), axis=0, keepdims=True)
            dh_ref[rs, 768:1280] = jnp.where(vc, dq * SCALE, 0.0).astype(dh_ref.dtype)
            dh_ref[rs, 1280:1792] = (t * (1.0 - lbv) * (sig * (1.0 - sig))).astype(dh_ref.dtype)
            dh_ref[rs, 1792:2304] = jnp.where(vc, jnp.concatenate(dis, axis=1), 0.0).astype(dh_ref.dtype)
            dh_ref[rs, 2304:2816] = jnp.where(vc, jnp.concatenate(dgzs, axis=1), 0.0).astype(dh_ref.dtype)

    cur = lambda i: (nb - 1 - i, 0)
    prev = lambda i: (jnp.maximum((nb - 1 - i) * q4 - 1, 0), 0)
    nxt = lambda i: (jnp.minimum((nb - 1 - i) * q4 + q4, lasth), 0)
    fix = lambda i: (0, 0)
    body, in_specs, args = _after(
        dep, body,
        [pl.BlockSpec((BM, DIN), cur), pl.BlockSpec((HALO, DIN), prev), pl.BlockSpec((HALO, DIN), nxt),
         pl.BlockSpec((BM, D), cur), pl.BlockSpec((HALO, D), nxt), pl.BlockSpec((BM, HW), cur),
         pl.BlockSpec((MB, NH, HD, HD), lambda i: (nb - 1 - i, 0, 0, 0)),
         pl.BlockSpec((SUBLANES, CW), fix), pl.BlockSpec((PW, PW), fix), pl.BlockSpec((1, PW), fix),
         pl.BlockSpec((1, HD), fix), pl.BlockSpec((1, HW), fix), pl.BlockSpec(mat.shape, fix),
         pl.BlockSpec(suf.shape, fix)],
        [h, h, h, dy, dy, o, sp, taps, wbd, ps, gn, lb, mat, suf])
    return pl.pallas_call(
        body, name=name, grid=(nb,), in_specs=in_specs,
        out_specs=[pl.BlockSpec((BM, DIN), cur), pl.BlockSpec((SUBLANES, CW), fix), pl.BlockSpec((SUBLANES, PW), fix),
                   pl.BlockSpec((SUBLANES, HD), fix), pl.BlockSpec((SUBLANES, HW), fix), pl.BlockSpec((PW, PW), fix)],
        out_shape=[jax.ShapeDtypeStruct((R, DIN), MX), jax.ShapeDtypeStruct((SUBLANES, CW), F32),
                   jax.ShapeDtypeStruct((SUBLANES, PW), F32), jax.ShapeDtypeStruct((SUBLANES, HD), F32),
                   jax.ShapeDtypeStruct((SUBLANES, HW), F32), jax.ShapeDtypeStruct((PW, PW), F32)],
        scratch_shapes=[pltpu.VMEM((NH, HD, HD), F32)],
        compiler_params=_cp(("arbitrary",), 40 * 2 ** 20),
    )(*args)


def _sum_slots(recv, *, name):
    S, L, rows, cols = recv.shape
    tr = _row_tile(rows, S * cols * 4, 6 * 2 ** 20)

    def body(r_ref, o_ref):
        acc = r_ref[0]
        for s in range(1, S):
            acc = acc + r_ref[s]
        o_ref[...] = acc

    return pl.pallas_call(
        body, name=name, grid=(L, rows // tr),
        in_specs=[pl.BlockSpec((S, None, tr, cols), lambda l, i: (0, l, i, 0))],
        out_specs=pl.BlockSpec((None, tr, cols), lambda l, i: (l, i, 0)),
        out_shape=jax.ShapeDtypeStruct((L, rows, cols), F32),
        compiler_params=_cp(("parallel", "parallel"), 4 * S * tr * cols * 4),
    )(recv)


def _sum_own(recv, own, chip, *, name):
    S, rows, cols = recv.shape
    tr = _row_tile(rows, S * cols * 4, 6 * 2 ** 20)

    def body(me_ref, r_ref, o_ref, out_ref):
        me = me_ref[0]
        acc = None
        for s in range(S):
            t = jnp.where(me == s, o_ref[...], r_ref[s])
            acc = t if acc is None else acc + t
        out_ref[...] = acc

    grid_spec = pltpu.PrefetchScalarGridSpec(
        num_scalar_prefetch=1, grid=(rows // tr,),
        in_specs=[pl.BlockSpec((S, tr, cols), lambda i, me: (0, i, 0)),
                  pl.BlockSpec((None, tr, cols), lambda i, me: (me[0], i, 0))],
        out_specs=pl.BlockSpec((tr, cols), lambda i, me: (i, 0)))
    return pl.pallas_call(
        body, name=name, grid_spec=grid_spec, out_shape=jax.ShapeDtypeStruct((rows, cols), F32),
        compiler_params=_cp(("parallel",), 5 * S * tr * cols * 4),
    )(chip.reshape(1).astype(jnp.int32), recv, own)


def _adamw(w, m, v, ga, gb, *, layer, prev, name):
    L, rows, cols = w.shape
    tr = _row_tile(rows, cols * 4, 2 ** 20)
    two = gb is not None
    nin = 5 if two else 4

    def body(*refs):
        w_ref, m_ref, v_ref, a_ref = refs[:4]
        g_ref, d_ref, m2_ref, v2_ref = refs[-4:]
        g = a_ref[...] + refs[4][...] if two else a_ref[...]
        m2 = ADAM_B1 * m_ref[...] + (1.0 - ADAM_B1) * g
        v2 = ADAM_B2 * v_ref[...] + (1.0 - ADAM_B2) * (g * g)
        m_hat = m2 / (1.0 - ADAM_B1 ** ADAM_STEP)
        v_hat = v2 / (1.0 - ADAM_B2 ** ADAM_STEP)
        g_ref[...] = g
        d_ref[...] = -ADAM_LR * (m_hat / (jnp.sqrt(v_hat) + ADAM_EPS) + ADAM_WD * w_ref[...])
        m2_ref[...] = m2
        v2_ref[...] = v2

    spec = pl.BlockSpec((None, tr, cols), lambda i: (layer, i, 0))
    gspec = pl.BlockSpec((tr, cols), lambda i: (i, 0))
    args = [w, m, v, ga] + ([gb] if two else [])
    in_specs = [spec] * 3 + [gspec] * (nin - 3)
    aliases = {}
    if prev is not None:
        args += list(prev)
        in_specs += [ANY] * 4
        aliases = {nin + j: j for j in range(4)}
    sd = jax.ShapeDtypeStruct((L, rows, cols), F32)
    return pl.pallas_call(
        body, name=name, grid=(rows // tr,), in_specs=in_specs, out_specs=[spec] * 4,
        out_shape=[sd] * 4, input_output_aliases=aliases,
        compiler_params=_cp(("parallel",), 24 * tr * cols * 4),
    )(*args)


def _exchange(arrays, *, flips, n_slots, slot_of, scatter, self_copy, name):
    n = len(arrays)
    nf = len(flips)
    out_shapes = [jax.ShapeDtypeStruct(a.shape if scatter else (n_slots,) + a.shape, a.dtype) for a in arrays]

    def body(*refs):
        ins, outs = refs[:n], refs[n:2 * n]
        send_sems, recv_sems, loc_sems = refs[2 * n:]
        x, y, c = lax.axis_index("x"), lax.axis_index("y"), lax.axis_index("c")
        me = slot_of(x, y, c)
        peers = [(1 - x if fx else x, 1 - y if fy else y, 1 - c if fc else c) for fx, fy, fc in flips]
        local, remote = [], []
        for a in range(n):
            if self_copy:
                lc = pltpu.make_async_copy(ins[a].at[me] if scatter else ins[a], outs[a].at[me], loc_sems.at[a])
                lc.start()
                local.append(lc)
            for k, p in enumerate(peers):
                src = ins[a].at[slot_of(*p)] if scatter else ins[a]
                cp = pltpu.make_async_remote_copy(
                    src_ref=src, dst_ref=outs[a].at[me], send_sem=send_sems.at[a, k], recv_sem=recv_sems.at[a, k],
                    device_id=p, device_id_type=MESH)
                cp.start()
                remote.append(cp)
        for a in range(n):
            for k, p in enumerate(peers):
                src = ins[a].at[slot_of(*p)] if scatter else ins[a]
                pltpu.make_async_remote_copy(
                    src_ref=src, dst_ref=outs[a].at[slot_of(*p)], send_sem=send_sems.at[a, k],
                    recv_sem=recv_sems.at[a, k], device_id=p, device_id_type=MESH).wait_recv()
        for cp in remote:
            cp.wait_send()
        for lc in local:
            lc.wait()

    return pl.pallas_call(
        body, name=name, in_specs=[ANY] * n, out_specs=[ANY] * n, out_shape=out_shapes,
        scratch_shapes=[pltpu.SemaphoreType.DMA((n, nf)), pltpu.SemaphoreType.DMA((n, nf)),
                        pltpu.SemaphoreType.DMA((n,))],
        compiler_params=pltpu.CompilerParams(has_side_effects=True),
    )(*arrays)


ALL_FLIPS = [(fx, fy, fc) for fx in (0, 1) for fy in (0, 1) for fc in (0, 1) if fx or fy or fc]


def _dev_slot(x, y, c):
    return 4 * x + 2 * y + c


def _zero_slot(x, y, c):
    return 0


HBM_SPEC = pl.BlockSpec(memory_space=pltpu.HBM)
SEM_SPEC = pl.BlockSpec(memory_space=pltpu.SEMAPHORE)
N_PEER_CHIPS = 3


def _peer_chips(x, y, c):
    return [(1 - x, y, c), (x, 1 - y, c), (1 - x, 1 - y, c)]


def _send_start(srcs, *, scatter, dep=None, name):
    n = len(srcs)
    nc = n * N_PEER_CHIPS
    srcs = [pltpu.with_memory_space_constraint(s, pltpu.HBM) for s in srcs]
    land_shapes = [s.shape if scatter else (4,) + s.shape for s in srcs]
    lands = [pltpu.with_memory_space_constraint(lax.empty(sh, s.dtype), pltpu.HBM) for sh, s in zip(land_shapes, srcs)]

    deps = [] if dep is None else [dep]
    nd = len(deps)

    def body(*refs):
        ins, lnd = refs[:n], refs[n:2 * n]
        send_sems, recv_sems = refs[2 * n + nd:2 * n + nd + nc], refs[2 * n + nd + nc:2 * n + nd + 2 * nc]
        token = refs[-1]
        x, y, c = lax.axis_index("x"), lax.axis_index("y"), lax.axis_index("c")
        me = 2 * x + y
        for a in range(n):
            for k, p in enumerate(_peer_chips(x, y, c)):
                src = ins[a].at[2 * p[0] + p[1]] if scatter else ins[a]
                j = a * N_PEER_CHIPS + k
                pltpu.make_async_remote_copy(
                    src_ref=src, dst_ref=lnd[a].at[me], send_sem=send_sems[j], recv_sem=recv_sems[j],
                    device_id=p, device_id_type=MESH).start()
        token[...] = jnp.zeros(token.shape, token.dtype)

    sem = pltpu.SemaphoreType.DMA(())
    outs = pl.pallas_call(
        body, name=name,
        out_shape=(*[sem] * (2 * nc), *[pltpu.HBM(s.shape, s.dtype) for s in srcs],
                   *[pltpu.HBM(sh, s.dtype) for sh, s in zip(land_shapes, srcs)],
                   jax.ShapeDtypeStruct((SUBLANES, LANES), F32)),
        in_specs=[HBM_SPEC] * (2 * n) + [ANY] * nd,
        out_specs=(*[SEM_SPEC] * (2 * nc), *[HBM_SPEC] * (2 * n), pl.BlockSpec(memory_space=pltpu.VMEM)),
        input_output_aliases={i: 2 * nc + i for i in range(2 * n)},
        compiler_params=pltpu.CompilerParams(has_side_effects=pltpu.SideEffectType.DATAFLOW_SIDE_EFFECTING),
    )(*srcs, *lands, *deps)
    return dict(sems=list(outs[:2 * nc]), srcs=list(outs[2 * nc:2 * nc + n]),
                lands=list(outs[2 * nc + n:2 * nc + 2 * n]), token=outs[-1])


def _send_wait(h, *, scatter, after, name):
    n = len(h["srcs"])
    nc = n * N_PEER_CHIPS

    def body(*refs):
        ins, lnd = refs[:n], refs[n:2 * n]
        send_sems, recv_sems = refs[2 * n:2 * n + nc], refs[2 * n + nc:2 * n + 2 * nc]
        x, y, c = lax.axis_index("x"), lax.axis_index("y"), lax.axis_index("c")
        for a in range(n):
            for k, p in enumerate(_peer_chips(x, y, c)):
                slot = 2 * p[0] + p[1]
                j = a * N_PEER_CHIPS + k
                cp = pltpu.make_async_remote_copy(
                    src_ref=ins[a].at[slot] if scatter else ins[a], dst_ref=lnd[a].at[slot],
                    send_sem=send_sems[j], recv_sem=recv_sems[j], device_id=p, device_id_type=MESH)
                cp.wait_send()
                cp.wait_recv()

    thru = h["srcs"] + h["lands"]
    outs = pl.pallas_call(
        body, name=name, out_shape=tuple(pltpu.HBM(t.shape, t.dtype) for t in thru),
        in_specs=[HBM_SPEC] * (2 * n) + [SEM_SPEC] * (2 * nc) + [ANY] * len(after),
        out_specs=tuple([HBM_SPEC] * (2 * n)),
        input_output_aliases={i: i for i in range(2 * n)},
        compiler_params=pltpu.CompilerParams(has_side_effects=pltpu.SideEffectType.DATAFLOW_SIDE_EFFECTING),
    )(*thru, *h["sems"], *after)
    return list(outs[:n]), list(outs[n:])


def _assemble(land, own, chip, axis):
    return jnp.concatenate([jnp.where(chip == k, own, land[k]) for k in range(4)], axis=axis)


def _pack(arrs):
    flat = jnp.concatenate([a.reshape(-1).astype(F32) for a in arrs])
    tile = SUBLANES * LANES
    pad = (-flat.shape[0]) % tile
    return jnp.pad(flat, (0, pad)).reshape(-1, LANES)


def _unpack(buf, shapes):
    flat = buf.reshape(-1)
    out, off = [], 0
    for s in shapes:
        n = int(np.prod(s))
        out.append(flat[off:off + n].reshape(s))
        off += n
    return out


def _lower_bounds(hg_lower_bounds):
    p = jax.nn.softmax(hg_lower_bounds.astype(F32), axis=0)
    return jnp.cumsum(p, axis=0) - p[0]


def kernel(x, meta_tokens, hg_lower_bounds, w_in, w_conv, w_pool, pool_scale, hg_norm_g, w_o, ln1_g, ln1_b, w_up, w_ffn_conv, b_ffn_conv, w_down, ln2_g, ln2_b, loss_target, m_meta_tokens, m_hg_lower_bounds, m_w_in, m_w_conv, m_w_pool, m_pool_scale, m_hg_norm_g, m_w_o, m_ln1_g, m_ln1_b, m_w_up, m_w_ffn_conv, m_b_ffn_conv, m_w_down, m_ln2_g, m_ln2_b, v_meta_tokens, v_hg_lower_bounds, v_w_in, v_w_conv, v_w_pool, v_pool_scale, v_hg_norm_g, v_w_o, v_ln1_g, v_ln1_b, v_w_up, v_w_ffn_conv, v_b_ffn_conv, v_w_down, v_ln2_g, v_ln2_b):
    S = x.shape[1]
    R = S + ROW0
    Fq = w_down.shape[1]
    F = 4 * Fq
    F2 = 2 * F
    F2q = w_up.shape[2]
    assert x.shape == (1, S, D) and R % 384 == 0 and S % ROW0 == 0
    chip = 2 * lax.axis_index("x") + lax.axis_index("y")
    tm = 384
    tm_w = max(t for t in range(SUBLANES, 2113, SUBLANES) if R % t == 0)
    tb_ffn = 128

    small_shapes = [(N_META, D // 4), (DEPTH, CW // 4, 3), (DEPTH, F2q, 3)]
    wb_in, wb_o, wb_up, wb_down = (w.astype(MX) for w in (w_in, w_o, w_up, w_down))
    h_a = _send_start([wb_in[0], _pack([meta_tokens, w_conv, w_ffn_conv])], scatter=False, name="gather_a_start")
    (own_in, own_small), (l_in, l_small) = _send_wait(h_a, scatter=False, after=[h_a["token"]], name="gather_a_wait")
    h_b = _send_start([wb_o[0], wb_up[0], wb_down[0]], scatter=False, dep=l_in, name="gather_b_start")
    Win, Wo, Wup, Wdown = {}, {}, {}, {}
    Win[0] = _assemble(l_in, own_in, chip, 1)
    sm = [_unpack(jnp.where(chip == k, own_small, l_small[k]), small_shapes) for k in range(4)]
    meta_full = jnp.concatenate([sm[k][0] for k in range(4)], axis=1)
    wconv_full = jnp.concatenate([sm[k][1] for k in range(4)], axis=1)
    wffn_full = jnp.concatenate([sm[k][2] for k in range(4)], axis=1)
    taps_c = jnp.pad(wconv_full.transpose(0, 2, 1), ((0, 0), (0, SUBLANES - 3), (0, 0)))
    taps_f = jnp.pad(wffn_full.transpose(0, 2, 1), ((0, 0), (0, SUBLANES - 3), (0, 0)))
    wbd = jnp.stack([jax.scipy.linalg.block_diag(*[w_pool[l, g] for g in range(4)]) for l in range(DEPTH)]).astype(MX)
    lbs, lbs_vjp = jax.vjp(_lower_bounds, hg_lower_bounds)
    mat, suf = _hg_consts()

    def fwd_mixer(l, X, dep=None):
        h = _mm(X, Win[l], tm=tm, dep=dep, name=f"mm_in_{l}")
        y, o, sp = _mixer_fwd(h, taps_c[l], wbd[l], pool_scale[l].reshape(1, PW), hg_norm_g[l].reshape(1, HD),
                              lbs[l].reshape(1, HW), mat, name=f"mixer_fwd_{l}")
        return h, y, o, sp

    def fwd_rest(l, X, h, y, o, sp, dep=None):
        x1, xh1, r1 = _mm_ln(y, Wo[l], X, ln1_g[l], ln1_b[l], tm=tm, dep=dep, name=f"mm_o_ln_{l}")
        up, u, a = _mm_ffn_fwd(x1, Wup[l], taps_f[l], b_ffn_conv[l].reshape(1, F2), tm=tm // 2,
                               name=f"mm_up_ffn_{l}")
        x2, xh2, r2 = _mm_ln(a, Wdown[l], x1, ln2_g[l], ln2_b[l], tm=tm, name=f"mm_down_ln_{l}")
        return (X, h, y, o, sp, x1, xh1, r1, up, u, a, xh2, r2), x2

    X = jnp.concatenate([jnp.zeros((PADR, D), F32), meta_full, x[0]], axis=0)
    h, y, o, sp = fwd_mixer(0, X, dep=h_b["token"])
    (own_o, own_up, own_down), (l_o, l_up, l_down) = _send_wait(h_b, scatter=False, after=[y], name="gather_b_wait")
    Wo[0], Wup[0], Wdown[0] = (_assemble(l_o, own_o, chip, 0), _assemble(l_up, own_up, chip, 1),
                               _assemble(l_down, own_down, chip, 0))
    h_c = _send_start([wb_in[1], wb_o[1], wb_up[1], wb_down[1]], scatter=False, dep=l_o, name="gather_c_start")
    saved0, X1 = fwd_rest(0, X, h, y, o, sp, dep=h_c["token"])
    own_c, l_c = _send_wait(h_c, scatter=False, after=[X1], name="gather_c_wait")
    Win[1], Wo[1] = _assemble(l_c[0], own_c[0], chip, 1), _assemble(l_c[1], own_c[1], chip, 0)
    Wup[1], Wdown[1] = _assemble(l_c[2], own_c[2], chip, 1), _assemble(l_c[3], own_c[3], chip, 0)
    saved1, X2 = fwd_rest(1, X1, *fwd_mixer(1, X1))
    saved = [saved0, saved1]

    dxo = X2

    sc = {}

    def scatter(nm, l, g, dep=None):
        sc[nm, l] = _send_start([g], scatter=True, dep=dep, name=f"scatter_{nm}_{l}_start")
        return sc[nm, l]["token"]

    tok = None

    small_g = [None] * DEPTH
    for l in reversed(range(DEPTH)):
        X, h, y, o, sp, x1, xh1, r1, up, u, a, xh2, r2 = saved[l]
        if l == DEPTH - 1:
            dz2, gb2, da, sq = _ln_bwd_mm(dxo, xh2, r2, ln2_g[l], Wdown[l], tm=ROW0, tgt=loss_target[0],
                                          name=f"loss_ln2_bwd_da_{l}")
            loss = lax.psum(0.5 * jnp.sum(sq[0]) / D, ("x", "y", "c"))
        else:
            dz2, gb2, da = _ln_bwd_mm(dxo, xh2, r2, ln2_g[l], Wdown[l], tm=tm, dep=tok, name=f"ln2_bwd_da_{l}")
        tok = scatter("w_down", l, _wgrad(a, dz2, slabs_on_cols=False, tm=tm_w, tn=512, name=f"wgrad_down_{l}"))
        dup, facc = _ffn_bwd(da, up, u, taps_f[l], tb=tb_ffn, dep=tok, name=f"ffn_bwd_{l}")
        dx1 = _mm(dup, Wup[l], nt=True, res=dz2, tm=tm, name=f"mm_dx1_{l}")
        tok = scatter("w_up", l, _wgrad(x1, dup, slabs_on_cols=True, tm=tm_w, tn=F2q, name=f"wgrad_up_{l}"))
        dz1, gb1, dym = _ln_bwd_mm(dx1, xh1, r1, ln1_g[l], Wo[l], tm=tm, dep=tok, name=f"ln1_bwd_dym_{l}")
        tok = scatter("w_o", l, _wgrad(y, dz1, slabs_on_cols=False, tm=tm_w, tn=512, name=f"wgrad_o_{l}"))
        dh, dwc, dsc, dgn, dlb, dwbd = _mixer_bwd(
            h, dym, o, sp, taps_c[l], wbd[l], pool_scale[l].reshape(1, PW), hg_norm_g[l].reshape(1, HD),
            lbs[l].reshape(1, HW), mat, suf, dep=tok, name=f"mixer_bwd_{l}")
        tok = scatter("w_in", l, _wgrad(X, dh, slabs_on_cols=True, tm=tm_w, tn=DIN // 4, name=f"wgrad_in_{l}"))
        if l == 0:
            dx_head, dx_seq = _mm_dx_head(dh, Win[l], dz1, dep=tok, name=f"mm_dx_{l}")
        else:
            dxo = _mm(dh, Win[l], nt=True, res=dz1, tm=tm, dep=tok, name=f"mm_dx_{l}")
        small_g[l] = dict(
            lbs=dlb[0], w_conv=dwc[0:3].T, w_pool=jnp.stack([dwbd[64 * g:64 * g + 64, 64 * g:64 * g + 64] for g in range(4)]),
            pool_scale=dsc[0], hg_norm_g=dgn[0], ln1_g=gb1[0], ln1_b=gb1[1], w_ffn_conv=facc[0:3].T,
            b_ffn_conv=facc[3], ln2_g=gb2[0], ln2_b=gb2[1])
    grad_x = dx_seq[None]

    sg_names = ["lbs", "w_conv", "w_pool", "pool_scale", "hg_norm_g", "ln1_g", "ln1_b", "w_ffn_conv",
                "b_ffn_conv", "ln2_g", "ln2_b"]
    sg_list = [dx_head[PADR:ROW0]] + [jnp.stack([small_g[l][nm] for l in range(DEPTH)]) for nm in sg_names]
    sg_shapes = [a.shape for a in sg_list]
    packed = _pack(sg_list)
    (gathered,) = _exchange([packed], flips=ALL_FLIPS, n_slots=8, slot_of=_dev_slot, scatter=False, self_copy=True,
                            name="gather_small_grads")
    total = _sum_slots(gathered[:, None], name="sum_small_grads")[0]
    tot = dict(zip(["meta_tokens"] + sg_names, _unpack(total, sg_shapes)))
    (g_hg,) = lbs_vjp(tot["lbs"])
    small_grads = dict(
        meta_tokens=lax.dynamic_slice_in_dim(tot["meta_tokens"], chip * (D // 4), D // 4, axis=1),
        hg_lower_bounds=g_hg,
        w_conv=lax.dynamic_slice_in_dim(tot["w_conv"], chip * (CW // 4), CW // 4, axis=1),
        w_pool=tot["w_pool"], pool_scale=tot["pool_scale"], hg_norm_g=tot["hg_norm_g"],
        ln1_g=tot["ln1_g"], ln1_b=tot["ln1_b"],
        w_ffn_conv=lax.dynamic_slice_in_dim(tot["w_ffn_conv"], chip * F2q, F2q, axis=1),
        b_ffn_conv=tot["b_ffn_conv"], ln2_g=tot["ln2_g"], ln2_b=tot["ln2_b"])
    small_w = dict(meta_tokens=(meta_tokens, m_meta_tokens, v_meta_tokens),
                   hg_lower_bounds=(hg_lower_bounds, m_hg_lower_bounds, v_hg_lower_bounds),
                   w_conv=(w_conv, m_w_conv, v_w_conv), w_pool=(w_pool, m_w_pool, v_w_pool),
                   pool_scale=(pool_scale, m_pool_scale, v_pool_scale), hg_norm_g=(hg_norm_g, m_hg_norm_g, v_hg_norm_g),
                   ln1_g=(ln1_g, m_ln1_g, v_ln1_g), ln1_b=(ln1_b, m_ln1_b, v_ln1_b),
                   w_ffn_conv=(w_ffn_conv, m_w_ffn_conv, v_w_ffn_conv), b_ffn_conv=(b_ffn_conv, m_b_ffn_conv, v_b_ffn_conv),
                   ln2_g=(ln2_g, m_ln2_g, v_ln2_g), ln2_b=(ln2_b, m_ln2_b, v_ln2_b))
    names_s = list(small_w)
    shapes_s = [small_w[nm][0].shape for nm in names_s]
    pk = [_pack([small_w[nm][j] for nm in names_s])[None] for j in range(3)]
    pg = _pack([small_grads[nm] for nm in names_s])[None]
    outs_s = _adamw(pk[0], pk[1], pk[2], pg[0], None, layer=0, prev=None, name="adamw_small")
    small = {nm: [] for nm in names_s}
    for j in range(4):
        for nm, val in zip(names_s, _unpack(outs_s[j][0], shapes_s)):
            small[nm].append(val)

    big_w = dict(w_in=(w_in, m_w_in, v_w_in), w_o=(w_o, m_w_o, v_w_o), w_up=(w_up, m_w_up, v_w_up),
                 w_down=(w_down, m_w_down, v_w_down))
    big = {nm: None for nm in big_w}
    for l in reversed(range(DEPTH)):
        for group in ((("w_down", "w_up", "w_o", "w_in"),) if l else (("w_down", "w_up", "w_o"), ("w_in",))):
            last = group == ("w_in",)
            after = [outs_s[0]] + [big[k][0] for k in big_w if k != "w_in"] if last else [dx_seq]
            part = []
            for nm in group:
                (own,), (recv,) = _send_wait(sc[nm, l], scatter=True, after=after, name=f"scatter_{nm}_{l}_wait")
                part.append(_sum_own(recv, own, chip, name=f"sum_{nm}_{l}"))
            sib = _exchange(part, flips=[(0, 0, 1)], n_slots=1, slot_of=_zero_slot, scatter=False, self_copy=False,
                            name=f"swap_cores_{l}{'_in' if last else ''}")
            for k, nm in enumerate(group):
                w, m, v = big_w[nm]
                big[nm] = _adamw(w, m, v, part[k], sib[k][0], layer=l, prev=big[nm], name=f"adamw_{nm}_{l}")

    order = ["meta_tokens", "hg_lower_bounds", "w_in", "w_conv", "w_pool", "pool_scale", "hg_norm_g", "w_o",
             "ln1_g", "ln1_b", "w_up", "w_ffn_conv", "b_ffn_conv", "w_down", "ln2_g", "ln2_b"]
    res = {nm: (big[nm] if nm in big else small[nm]) for nm in order}
    outs = [loss, grad_x]
    for j in range(4):
        outs += [res[nm][j] for nm in order]
    return tuple(outs)
```

```python
import numpy as np

import jax
import jax.numpy as jnp
from jax import lax
from jax.experimental import pallas as pl
from jax.experimental.pallas import tpu as pltpu

F32 = jnp.float32
BF = jnp.bfloat16
MX = jnp.bfloat16

D = 1024
CW = 256
HW = 512
HD = 128
NH = 4
PW = 256
DIN = 3072
N_META = 16
CH = 64
MB = 4
BM = MB * CH
ROW0 = 256
PADR = ROW0 - N_META
HALO = 16
FH = 8
LEVELS = (32, 16, 8, 4, 2, 1)
DEPTH = 2
ALPHA = (2 * DEPTH) ** 0.25
LN_EPS = 1e-5
RMS_EPS = 1e-6
F_FLOOR = 1e-30
SCALE = HD ** -0.5
ADAM_LR, ADAM_B1, ADAM_B2, ADAM_EPS, ADAM_WD, ADAM_STEP = 0.001, 0.9, 0.999, 1e-08, 0.01, 10

VMEM_V7X = 64 * 2 ** 20
LANES = 128
SUBLANES = 8

NN = (((1,), (0,)), ((), ()))
NT = (((1,), (1,)), ((), ()))
TN = (((0,), (0,)), ((), ()))
MESH = pl.DeviceIdType.MESH
ANY = pl.BlockSpec(memory_space=pl.ANY)


def _dot(a, b, dn):
    return lax.dot_general(a.astype(MX), b.astype(MX), dn, preferred_element_type=F32)


def _cp(sem, est_bytes):
    lim = int(min(VMEM_V7X - 6 * 2 ** 20, max(32 * 2 ** 20, est_bytes)))
    return pltpu.CompilerParams(dimension_semantics=sem, vmem_limit_bytes=lim)


def _nbytes(shape, dtype):
    return int(np.prod(shape)) * jnp.dtype(dtype).itemsize


def _row_tile(rows, row_bytes, budget):
    best = SUBLANES
    for t in range(SUBLANES, rows + 1, SUBLANES):
        if rows % t == 0 and t * row_bytes <= budget:
            best = t
    return best


def _after(dep, body, in_specs, args):
    if dep is None:
        return body, list(in_specs), list(args)

    def body_after(dep_ref, *refs):
        body(*refs)

    return body_after, [ANY] + list(in_specs), [dep] + list(args)


def _mm(a, w, *, nt=False, res=None, tm, out_dtype=F32, zero_inert=False, dep=None, name):
    R, K = a.shape
    N = w.shape[0] if nt else w.shape[1]
    dn = NT if nt else NN

    def body(*refs):
        if res is None:
            a_ref, w_ref, o_ref = refs
        else:
            a_ref, w_ref, r_ref, o_ref = refs
        acc = _dot(a_ref[...], w_ref[...], dn)
        if res is not None:
            acc = acc + ALPHA * r_ref[...]
        if zero_inert:
            acc = jnp.where(_row_ids(pl.program_id(0) * tm, tm) >= PADR, acc, 0.0)
        o_ref[...] = acc.astype(out_dtype)

    in_specs = [pl.BlockSpec((tm, K), lambda i: (i, 0)), pl.BlockSpec(w.shape, lambda i: (0, 0))]
    args = [a, w]
    est = 2 * _nbytes((tm, K), a.dtype) + 2 * _nbytes(w.shape, w.dtype) + 3 * _nbytes((tm, N), F32)
    if res is not None:
        in_specs.append(pl.BlockSpec((tm, N), lambda i: (i, 0)))
        args.append(res)
        est += 2 * _nbytes((tm, N), F32)
    body, in_specs, args = _after(dep, body, in_specs, args)
    return pl.pallas_call(
        body, name=name, grid=(R // tm,), in_specs=in_specs,
        out_specs=pl.BlockSpec((tm, N), lambda i: (i, 0)),
        out_shape=jax.ShapeDtypeStruct((R, N), out_dtype),
        compiler_params=_cp(("parallel",), est + 4 * 2 ** 20),
    )(*args)


def _wgrad(a, b, *, slabs_on_cols, tm, tn, name):
    R, Ka = a.shape
    Nb = b.shape[1]
    if slabs_on_cols:
        out_shape = (4, Ka, Nb // 4)
        assert tn == Nb // 4
        out_spec = pl.BlockSpec((None, Ka, tn), lambda j, i: (j, 0, 0))
    else:
        out_shape = (4, Ka // 4, Nb)
        out_spec = pl.BlockSpec((4, Ka // 4, tn), lambda j, i: (0, 0, j))

    def body(a_ref, b_ref, o_ref):
        @pl.when(pl.program_id(1) == 0)
        def _():
            o_ref[...] = jnp.zeros(o_ref.shape, F32)

        acc = _dot(a_ref[...], b_ref[...], TN)
        o_ref[...] += acc.reshape(o_ref.shape)

    in_specs = [pl.BlockSpec((tm, Ka), lambda j, i: (i, 0)), pl.BlockSpec((tm, tn), lambda j, i: (i, j))]
    est = 2 * _nbytes((tm, Ka), a.dtype) + 2 * _nbytes((tm, tn), b.dtype) + 4 * _nbytes((Ka, tn), F32) \
        + _nbytes((tm, Ka), F32)
    return pl.pallas_call(
        body, name=name, grid=(Nb // tn, R // tm), in_specs=in_specs, out_specs=out_spec,
        out_shape=jax.ShapeDtypeStruct(out_shape, F32),
        compiler_params=_cp(("parallel", "arbitrary"), est + 4 * 2 ** 20),
    )(a, b)


def _mm_ln(a, w, xres, g, b, *, tm, dep=None, name):
    R, K = a.shape

    def body(a_ref, w_ref, x_ref, g_ref, b_ref, xo_ref, xh_ref, r_ref):
        z = ALPHA * x_ref[...] + _dot(a_ref[...], w_ref[...], NN)
        mu = jnp.mean(z, axis=-1, keepdims=True)
        zc = z - mu
        var = jnp.mean(zc * zc, axis=-1, keepdims=True)
        r = lax.rsqrt(var + LN_EPS)
        xh = zc * r
        xh_ref[...] = xh
        r_ref[...] = r
        xo_ref[...] = xh * g_ref[...] + b_ref[...]

    row = lambda i: (i, 0)
    fix = lambda i: (0, 0)
    est = 2 * _nbytes((tm, K), a.dtype) + 2 * _nbytes(w.shape, w.dtype) + 10 * _nbytes((tm, D), F32)
    body, in_specs, args = _after(
        dep, body, [pl.BlockSpec((tm, K), row), pl.BlockSpec(w.shape, fix), pl.BlockSpec((tm, D), row),
                    pl.BlockSpec((1, D), fix), pl.BlockSpec((1, D), fix)],
        [a, w, xres, g.reshape(1, D), b.reshape(1, D)])
    return pl.pallas_call(
        body, name=name, grid=(R // tm,), in_specs=in_specs,
        out_specs=[pl.BlockSpec((tm, D), row), pl.BlockSpec((tm, D), row), pl.BlockSpec((tm, 1), row)],
        out_shape=[jax.ShapeDtypeStruct((R, D), F32), jax.ShapeDtypeStruct((R, D), F32),
                   jax.ShapeDtypeStruct((R, 1), F32)],
        compiler_params=_cp(("parallel",), est + 4 * 2 ** 20),
    )(*args)


def _resident(shape):
    return pl.BlockSpec(shape, lambda i: (0,) * len(shape), pipeline_mode=pl.Buffered(1))


def _ln_bwd_mm(dx, xh, r, g, w, *, tm, tgt=None, dep=None, name):
    R = dx.shape[0]
    N = w.shape[0]
    assert tgt is None or tm == ROW0

    def body(*refs):
        if tgt is None:
            dx_ref, xh_ref, r_ref, g_ref, w_ref, dz_ref, gb_ref, o_ref = refs
        else:
            dx_ref, t_ref, xh_ref, r_ref, g_ref, w_ref, dz_ref, gb_ref, o_ref, sq_ref = refs
        i = pl.program_id(0)

        @pl.when(i == 0)
        def _():
            gb_ref[...] = jnp.zeros(gb_ref.shape, F32)
            if tgt is not None:
                sq_ref[...] = jnp.zeros(sq_ref.shape, F32)

        if tgt is None:
            dxv = dx_ref[...]
        else:
            err = jnp.where(i > 0, dx_ref[...] - t_ref[...], 0.0)
            sq_ref[0:1, :] += jnp.sum(err * err, axis=0, keepdims=True)
            dxv = err / D
        xhv = xh_ref[...]
        dyh = dxv * g_ref[...]
        m1 = jnp.mean(dyh, axis=-1, keepdims=True)
        m2 = jnp.mean(dyh * xhv, axis=-1, keepdims=True)
        dz = r_ref[...] * (dyh - m1 - xhv * m2)
        dz_ref[...] = dz
        gb_ref[0:1, :] += jnp.sum(dxv * xhv, axis=0, keepdims=True)
        gb_ref[1:2, :] += jnp.sum(dxv, axis=0, keepdims=True)
        o_ref[...] = _dot(dz, w_ref[...], NT)

    row = lambda i: (i, 0)
    fix = lambda i: (0, 0)
    in_specs = [pl.BlockSpec((tm, D), row), pl.BlockSpec((tm, D), row), pl.BlockSpec((tm, 1), row),
                pl.BlockSpec((1, D), fix), _resident(w.shape)]
    args = [dx, xh, r, g.reshape(1, D), w]
    out_specs = [pl.BlockSpec((tm, D), row), pl.BlockSpec((SUBLANES, D), fix), pl.BlockSpec((tm, N), row)]
    out_shape = [jax.ShapeDtypeStruct((R, D), F32), jax.ShapeDtypeStruct((SUBLANES, D), F32),
                 jax.ShapeDtypeStruct((R, N), F32)]
    if tgt is not None:
        in_specs.insert(1, pl.BlockSpec((tm, D), lambda i: (jnp.maximum(i - 1, 0), 0)))
        args.insert(1, tgt)
        out_specs.append(pl.BlockSpec((SUBLANES, D), fix))
        out_shape.append(jax.ShapeDtypeStruct((SUBLANES, D), F32))
    body, in_specs, args = _after(dep, body, in_specs, args)
    est = _nbytes(w.shape, w.dtype) + 16 * _nbytes((tm, D), F32) + 4 * _nbytes((tm, N), F32)
    return pl.pallas_call(
        body, name=name, grid=(R // tm,), in_specs=in_specs, out_specs=out_specs, out_shape=out_shape,
        compiler_params=_cp(("arbitrary",), est + 4 * 2 ** 20),
    )(*args)


def _mm_dx_head(dh, w, res, *, dep=None, name):
    R, K = dh.shape
    tm = ROW0

    def body(a_ref, w_ref, r_ref, head_ref, seq_ref):
        i = pl.program_id(0)
        acc = _dot(a_ref[...], w_ref[...], NT) + ALPHA * r_ref[...]

        @pl.when(i == 0)
        def _():
            head_ref[...] = acc

        @pl.when(i > 0)
        def _():
            seq_ref[...] = acc

    row = lambda i: (i, 0)
    body, in_specs, args = _after(
        dep, body, [pl.BlockSpec((tm, K), row), _resident(w.shape), pl.BlockSpec((tm, D), row)], [dh, w, res])
    est = _nbytes(w.shape, w.dtype) + 2 * _nbytes((tm, K), dh.dtype) + 8 * _nbytes((tm, D), F32)
    return pl.pallas_call(
        body, name=name, grid=(R // tm,), in_specs=in_specs,
        out_specs=[pl.BlockSpec((tm, D), lambda i: (0, 0)), pl.BlockSpec((tm, D), lambda i: (jnp.maximum(i - 1, 0), 0))],
        out_shape=[jax.ShapeDtypeStruct((ROW0, D), F32), jax.ShapeDtypeStruct((R - ROW0, D), F32)],
        compiler_params=_cp(("arbitrary",), est + 4 * 2 ** 20),
    )(*args)


def _mm_ffn_fwd(x1, w, taps, bias, *, tm, name):
    R, K = x1.shape
    F2 = w.shape[1]
    F = F2 // 2

    def body(x_ref, w_ref, t_ref, b_ref, up_ref, u_ref, a_ref, carry_ref):
        i = pl.program_id(0)

        @pl.when(i == 0)
        def _():
            carry_ref[...] = jnp.zeros(carry_ref.shape, F32)

        acc = _dot(x_ref[...], w_ref[...], NN)
        acc = jnp.where(_row_ids(i * tm, tm) >= PADR, acc, 0.0)
        up_ref[...] = acc.astype(up_ref.dtype)
        ue = jnp.concatenate([carry_ref[...], acc], axis=0)
        carry_ref[...] = acc[tm - FH:tm]
        u = (t_ref[0:1, :] * pltpu.roll(ue, 2, 0)[FH:] + t_ref[1:2, :] * pltpu.roll(ue, 1, 0)[FH:]
             + t_ref[2:3, :] * acc + b_ref[...])
        u_ref[...] = u
        gate = u[:, :F]
        val = u[:, F:]
        a_ref[...] = (gate * jax.nn.sigmoid(gate) * val).astype(a_ref.dtype)

    row = lambda i: (i, 0)
    est = _nbytes(w.shape, w.dtype) + 2 * _nbytes((tm, K), x1.dtype) + 10 * _nbytes((tm + FH, F2), F32)
    return pl.pallas_call(
        body, name=name, grid=(R // tm,),
        in_specs=[pl.BlockSpec((tm, K), row), _resident(w.shape), _resident((SUBLANES, F2)), _resident((1, F2))],
        out_specs=[pl.BlockSpec((tm, F2), row), pl.BlockSpec((tm, F2), row), pl.BlockSpec((tm, F), row)],
        out_shape=[jax.ShapeDtypeStruct((R, F2), MX), jax.ShapeDtypeStruct((R, F2), F32),
                   jax.ShapeDtypeStruct((R, F), MX)],
        scratch_shapes=[pltpu.VMEM((FH, F2), F32)],
        compiler_params=_cp(("arbitrary",), est + 4 * 2 ** 20),
    )(x1, w, taps, bias)


def _row_ids(start, n):
    return start + lax.broadcasted_iota(jnp.int32, (n, 1), 0)


def _ffn_bwd(da, up, u, taps, *, tb, dep=None, name):
    R, F2 = up.shape
    F = F2 // 2
    nh = tb // FH
    nb = R // tb
    last = R // FH - 1
    m = tb + FH

    def body(da_ref, dan_ref, x_ref, u_ref, un_ref, w_ref, du_ref, acc_ref):
        i = pl.program_id(0)

        @pl.when(i == 0)
        def _():
            acc_ref[...] = jnp.zeros(acc_ref.shape, F32)

        inside = (i < nb - 1).astype(F32)
        for j in range(F // LANES):
            gsl = slice(j * LANES, (j + 1) * LANES)
            vsl = slice(F + j * LANES, F + (j + 1) * LANES)
            gate = jnp.concatenate([u_ref[:, gsl], un_ref[:, gsl]], axis=0)
            val = jnp.concatenate([u_ref[:, vsl], un_ref[:, vsl]], axis=0)
            dae = jnp.concatenate([da_ref[:, gsl], dan_ref[:, gsl] * inside], axis=0)
            sg = jax.nn.sigmoid(gate)
            gs = gate * sg
            for sl, du in ((gsl, dae * val * (sg * (1.0 + gate - gs))), (vsl, dae * gs)):
                du0 = du[0:tb]
                du1 = pltpu.roll(du, m - 1, 0)[0:tb]
                du2 = pltpu.roll(du, m - 2, 0)[0:tb]
                du_ref[:, sl] = (w_ref[2:3, sl] * du0 + w_ref[1:2, sl] * du1 + w_ref[0:1, sl] * du2).astype(du_ref.dtype)
                x = x_ref[:, sl].astype(F32)
                acc_ref[0:1, sl] += jnp.sum(du2 * x, axis=0, keepdims=True)
                acc_ref[1:2, sl] += jnp.sum(du1 * x, axis=0, keepdims=True)
                acc_ref[2:3, sl] += jnp.sum(du0 * x, axis=0, keepdims=True)
                acc_ref[3:4, sl] += jnp.sum(du0, axis=0, keepdims=True)

        @pl.when(i * tb < PADR)
        def _():
            row = _row_ids(i * tb, tb)
            du_ref[...] = jnp.where(row >= PADR, du_ref[...], jnp.zeros((), du_ref.dtype))

    nxt = lambda i: (jnp.minimum(i * nh + nh, last), 0)
    cur = lambda i: (i, 0)
    fix = lambda i: (0, 0)
    est = 12 * _nbytes((tb + 2 * FH, F2), F32)
    body, in_specs, args = _after(
        dep, body, [pl.BlockSpec((tb, F), cur), pl.BlockSpec((FH, F), nxt),
                    pl.BlockSpec((tb, F2), cur), pl.BlockSpec((tb, F2), cur), pl.BlockSpec((FH, F2), nxt),
                    pl.BlockSpec((SUBLANES, F2), fix)], [da, da, up, u, u, taps])
    return pl.pallas_call(
        body, name=name, grid=(R // tb,), in_specs=in_specs,
        out_specs=[pl.BlockSpec((tb, F2), cur), pl.BlockSpec((SUBLANES, F2), fix)],
        out_shape=[jax.ShapeDtypeStruct((R, F2), MX), jax.ShapeDtypeStruct((SUBLANES, F2), F32)],
        compiler_params=_cp(("arbitrary",), est),
    )(*args)


def _hg_consts():
    t = np.arange(CH)[:, None]
    j = np.arange(CH)[None, :]
    low = (j <= t).astype(np.float32)
    blocks = [low, (j > t).astype(np.float32)]
    for m in LEVELS:
        ref = (t // (2 * m)) * 2 * m + m - 1
        blocks.append(low - (j <= ref).astype(np.float32))
    mat = np.concatenate(blocks, axis=0)
    suf = (j >= t).astype(np.float32)
    return jnp.asarray(mat, BF), jnp.asarray(suf, BF)


def _split_dot(mat, x):
    hi = x.astype(BF)
    lo = (x - hi.astype(F32)).astype(BF)
    return (lax.dot_general(mat, hi, NN, preferred_element_type=F32)
            + lax.dot_general(mat, lo, NN, preferred_element_type=F32))


def _lane_select(a2, a4, a8, a16):
    lane = lax.broadcasted_iota(jnp.int32, (1, PW), 1)
    return jnp.where(lane < 64, a2, jnp.where(lane < 128, a4, jnp.where(lane < 192, a8, a16)))


def _pool_count(row):
    win = _lane_select(2.0, 4.0, 8.0, 16.0)
    t1 = jnp.maximum((row - PADR + 1).astype(F32), 1.0)
    return jnp.minimum(t1, win)


def _gates(fz, lb, valid):
    sig = jax.nn.sigmoid(fz)
    f = lb + (1.0 - lb) * sig
    lf = jnp.where(valid, jnp.log(jnp.maximum(f, F_FLOOR)), 0.0)
    kk = jnp.where(valid, (1.0 - lb) * (1.0 - sig), 0.0)
    return sig, f, lf, kk


def _level_masks():
    tt = lax.broadcasted_iota(jnp.int32, (CH, CH), 0)
    ss = lax.broadcasted_iota(jnp.int32, (CH, CH), 1)
    xr = tt ^ ss
    low = tt > ss
    return tt, ss, [(xr >= m) & (xr < 2 * m) & low for m in LEVELS]


def _intra(q, kk, ex, sl, tt, ss, masks):
    a = jnp.where(tt == ss, jnp.sum(q * kk, axis=-1, keepdims=True), 0.0)
    parts = []
    for l in range(len(LEVELS)):
        e = jnp.exp(-jnp.abs(ex[(2 + l) * CH:(3 + l) * CH, sl]))
        ql = (q * e).astype(MX)
        kl = (kk * e).astype(MX)
        a = a + jnp.where(masks[l], _dot(ql, kl, NT), 0.0)
        parts.append((e, ql, kl))
    return a, parts


def _mixer_fwd(h, taps, wbd, ps, gn, lb, mat, *, name):
    R = h.shape[0]
    nb = R // BM
    q4 = BM // HALO

    def body(h_ref, hp_ref, w_ref, wbd_ref, ps_ref, gn_ref, lb_ref, mat_ref, y_ref, o_ref, sp_ref, st_ref):
        i = pl.program_id(0)

        @pl.when(i == 0)
        def _():
            st_ref[...] = jnp.zeros(st_ref.shape, F32)

        row = _row_ids(i * BM, BM)
        valid_b = row >= PADR
        rowe = _row_ids(i * BM - HALO, BM + HALO)
        valide = rowe >= PADR
        se = jnp.concatenate([hp_ref[:, 256:512] * hp_ref[:, 512:768], h_ref[:, 256:512] * h_ref[:, 512:768]], axis=0)
        se = jnp.where(valide, se, 0.0)
        conv = (w_ref[0:1, :] * pltpu.roll(se, 2, 0) + w_ref[1:2, :] * pltpu.roll(se, 1, 0)
                + w_ref[2:3, :] * se)[HALO:]
        y_ref[:, 0:CW] = (h_ref[:, 0:256] * conv).astype(y_ref.dtype)
        ve = jnp.where(valide, jnp.concatenate([hp_ref[:, 2816:3072], h_ref[:, 2816:3072]], axis=0), 0.0)
        s2 = ve + pltpu.roll(ve, 1, 0)
        s4 = s2 + pltpu.roll(s2, 2, 0)
        s8 = s4 + pltpu.roll(s4, 4, 0)
        s16 = s8 + pltpu.roll(s8, 8, 0)
        dp = (_lane_select(s2, s4, s8, s16) / _pool_count(rowe) - ve)[HALO:]
        y_ref[:, CW + HW:D] = (_dot(dp, wbd_ref[...], NN) * ps_ref[...]).astype(y_ref.dtype)
        lbv = lb_ref[...]
        tt, ss, masks = _level_masks()
        for c in range(MB):
            rs = slice(c * CH, (c + 1) * CH)
            _, _, lf, kk = _gates(h_ref[rs, 1280:1792], lbv, valid_b[rs])
            q = h_ref[rs, 768:1280] * SCALE
            ii = h_ref[rs, 1792:2304]
            gz = h_ref[rs, 2304:2816]
            ex = _split_dot(mat_ref[...], lf)
            eg = jnp.exp(ex[0:CH])
            q0 = q * eg
            kr = kk * jnp.exp(ex[CH:2 * CH])
            for hd in range(NH):
                sl = slice(HD * hd, HD * (hd + 1))
                st = st_ref[hd]
                sp_ref[c, hd] = st
                a, _ = _intra(q[:, sl], kk[:, sl], ex, sl, tt, ss, masks)
                o = _dot(q0[:, sl], st, NT) + _dot(a, ii[:, sl], NN)
                st_ref[hd] = st * eg[CH - 1:CH, sl] + _dot(ii[:, sl], kr[:, sl], TN)
                o_ref[rs, sl] = o
                r = lax.rsqrt(jnp.mean(o * o, axis=-1, keepdims=True) + RMS_EPS)
                g_ = gz[:, sl]
                y_ref[rs, CW + HD * hd:CW + HD * (hd + 1)] = (
                    o * r * gn_ref[...] * (g_ * jax.nn.sigmoid(g_))).astype(y_ref.dtype)

    fix = lambda i: (0, 0)
    return pl.pallas_call(
        body, name=name, grid=(nb,),
        in_specs=[pl.BlockSpec((BM, DIN), lambda i: (i, 0)),
                  pl.BlockSpec((HALO, DIN), lambda i: (jnp.maximum(i * q4 - 1, 0), 0)),
                  pl.BlockSpec((SUBLANES, CW), fix), pl.BlockSpec((PW, PW), fix), pl.BlockSpec((1, PW), fix),
                  pl.BlockSpec((1, HD), fix), pl.BlockSpec((1, HW), fix), pl.BlockSpec(mat.shape, fix)],
        out_specs=[pl.BlockSpec((BM, D), lambda i: (i, 0)), pl.BlockSpec((BM, HW), lambda i: (i, 0)),
                   pl.BlockSpec((MB, NH, HD, HD), lambda i: (i, 0, 0, 0))],
        out_shape=[jax.ShapeDtypeStruct((R, D), MX), jax.ShapeDtypeStruct((R, HW), F32),
                   jax.ShapeDtypeStruct((R // CH, NH, HD, HD), F32)],
        scratch_shapes=[pltpu.VMEM((NH, HD, HD), F32)],
        compiler_params=_cp(("arbitrary",), 32 * 2 ** 20),
    )(h, h, taps, wbd, ps, gn, lb, mat)


def _mixer_bwd(h, dy, o, sp, taps, wbd, ps, gn, lb, mat, suf, *, dep=None, name):
    R = h.shape[0]
    nb = R // BM
    q4 = BM // HALO
    lasth = R // HALO - 1

    def body(h_ref, hp_ref, hn_ref, dy_ref, dyn_ref, o_ref, sp_ref, w_ref, wbd_ref, ps_ref, gn_ref, lb_ref,
             mat_ref, suf_ref, dh_ref, dw_ref, dsc_ref, dgn_ref, dlb_ref, dwbd_ref, dst_ref):
        i = pl.program_id(0)
        b = nb - 1 - i

        @pl.when(i == 0)
        def _():
            dst_ref[...] = jnp.zeros(dst_ref.shape, F32)
            dw_ref[...] = jnp.zeros(dw_ref.shape, F32)
            dsc_ref[...] = jnp.zeros(dsc_ref.shape, F32)
            dgn_ref[...] = jnp.zeros(dgn_ref.shape, F32)
            dlb_ref[...] = jnp.zeros(dlb_ref.shape, F32)
            dwbd_ref[...] = jnp.zeros(dwbd_ref.shape, F32)

        ne = BM + 2 * HALO
        nc = BM + HALO
        row = _row_ids(b * BM, BM)
        valid = row >= PADR
        rowe = _row_ids(b * BM - HALO, ne)
        valide = (rowe >= PADR) & (rowe < R)
        rown = rowe[HALO:]
        validn = rown < R

        def cat3(lo, hi):
            return jnp.concatenate([hp_ref[:, lo:hi], h_ref[:, lo:hi], hn_ref[:, lo:hi]], axis=0)

        def catd(lo, hi):
            return jnp.where(validn, jnp.concatenate([dy_ref[:, lo:hi], dyn_ref[:, lo:hi]], axis=0), 0.0)

        w0, w1, w2 = w_ref[0:1, :], w_ref[1:2, :], w_ref[2:3, :]
        cce = cat3(256, 512)
        cve = cat3(512, 768)
        se = jnp.where(valide, cce * cve, 0.0)
        sm2 = pltpu.roll(se, 2, 0)
        sm1 = pltpu.roll(se, 1, 0)
        conv = (w0 * sm2 + w1 * sm1 + w2 * se)[HALO:HALO + BM]
        cbn = jnp.concatenate([h_ref[:, 0:256], hn_ref[:, 0:256]], axis=0)
        dconv = catd(0, CW) * cbn
        ds = w2 * dconv + w1 * pltpu.roll(dconv, nc - 1, 0) + w0 * pltpu.roll(dconv, nc - 2, 0)
        ds = jnp.where(valid, ds[0:BM], 0.0)
        dcv = dconv[0:BM]
        dw_ref[0:1, :] += jnp.sum(dcv * sm2[HALO:HALO + BM], axis=0, keepdims=True)
        dw_ref[1:2, :] += jnp.sum(dcv * sm1[HALO:HALO + BM], axis=0, keepdims=True)
        dw_ref[2:3, :] += jnp.sum(dcv * se[HALO:HALO + BM], axis=0, keepdims=True)
        dh_ref[:, 0:256] = jnp.where(valid, dy_ref[:, 0:CW] * conv, 0.0).astype(dh_ref.dtype)
        dh_ref[:, 256:512] = (ds * h_ref[:, 512:768]).astype(dh_ref.dtype)
        dh_ref[:, 512:768] = (ds * h_ref[:, 256:512]).astype(dh_ref.dtype)

        ve = jnp.where(valide, cat3(2816, 3072), 0.0)
        s2 = ve + pltpu.roll(ve, 1, 0)
        s4 = s2 + pltpu.roll(s2, 2, 0)
        s8 = s4 + pltpu.roll(s4, 4, 0)
        s16 = s8 + pltpu.roll(s8, 8, 0)
        cnte = _pool_count(rowe)
        dp = (_lane_select(s2, s4, s8, s16) / cnte - ve)[HALO:HALO + BM]
        dyp = catd(CW + HW, D)
        pre = _dot(dp, wbd_ref[...], NN)
        dsc_ref[0:1, :] += jnp.sum(dyp[0:BM] * pre, axis=0, keepdims=True)
        dyps = dyp * ps_ref[...]
        dd = _dot(dyps, wbd_ref[...], NT)
        dwbd_ref[...] += _dot(dp, dyps[0:BM], TN)
        e = dd / cnte[HALO:]
        t2 = e + pltpu.roll(e, nc - 1, 0)
        t4 = t2 + pltpu.roll(t2, nc - 2, 0)
        t8 = t4 + pltpu.roll(t4, nc - 4, 0)
        t16 = t8 + pltpu.roll(t8, nc - 8, 0)
        dv = (_lane_select(t2, t4, t8, t16) - dd)[0:BM]
        dh_ref[:, 2816:3072] = jnp.where(valid, dv, 0.0).astype(dh_ref.dtype)

        lbv = lb_ref[...]
        tt, ss, masks = _level_masks()
        gnv = gn_ref[...]
        for c in reversed(range(MB)):
            rs = slice(c * CH, (c + 1) * CH)
            vc = valid[rs]
            sig, f, lf, kk = _gates(h_ref[rs, 1280:1792], lbv, vc)
            q = h_ref[rs, 768:1280] * SCALE
            ii = h_ref[rs, 1792:2304]
            gz = h_ref[rs, 2304:2816]
            ex = _split_dot(mat_ref[...], lf)
            eg = jnp.exp(ex[0:CH])
            egr = jnp.exp(ex[CH:2 * CH])
            q0 = q * eg
            kr = kk * egr
            dqs, dks, dis, dgzs, tails, dbs = [], [], [], [], [], []
            for hd in range(NH):
                sl = slice(HD * hd, HD * (hd + 1))
                ov = o_ref[rs, sl]
                r = lax.rsqrt(jnp.mean(ov * ov, axis=-1, keepdims=True) + RMS_EPS)
                oh = ov * r
                g_ = gz[:, sl]
                sg = jax.nn.sigmoid(g_)
                dyv = dy_ref[rs, CW + HD * hd:CW + HD * (hd + 1)]
                don = dyv * (g_ * sg)
                dgzs.append(dyv * (oh * gnv) * (sg * (1.0 + g_ * (1.0 - sg))))
                dgn_ref[0:1, :] += jnp.sum(don * oh, axis=0, keepdims=True)
                doh = don * gnv
                do = r * (doh - oh * jnp.mean(doh * oh, axis=-1, keepdims=True))
                st = sp_ref[c, hd]
                dst = dst_ref[hd]
                qh, kh, ih = q[:, sl], kk[:, sl], ii[:, sl]
                a, parts = _intra(qh, kh, ex, sl, tt, ss, masks)
                da = jnp.where(tt >= ss, _dot(do, ih, NT), 0.0)
                dis.append(_dot(a, do, TN) + _dot(kr[:, sl], dst, NT))
                q0h = q0[:, sl].astype(MX)
                krh = kr[:, sl].astype(MX)
                dq0 = _dot(do, st, NN)
                dkr = _dot(ih, dst, NN)
                dq = dq0 * eg[:, sl]
                dk = dkr * egr[:, sl]
                kdk = krh.astype(F32) * dkr
                db = q0h.astype(F32) * dq0 - kdk
                tails.append(jnp.sum(kdk, axis=0, keepdims=True)
                             + eg[CH - 1:CH, sl] * jnp.sum(dst * st, axis=0, keepdims=True))
                dga = jnp.sum(jnp.where(tt == ss, da, 0.0), axis=-1, keepdims=True)
                dq = dq + dga * kh
                dk = dk + dga * qh
                for l in range(len(LEVELS)):
                    e_l, ql, kl = parts[l]
                    dpl = jnp.where(masks[l], da, 0.0).astype(MX)
                    dql = _dot(dpl, kl, NN)
                    dkl = _dot(dpl, ql, TN)
                    dq = dq + dql * e_l
                    dk = dk + dkl * e_l
                    db = db + (ql.astype(F32) * dql - kl.astype(F32) * dkl)
                dst_ref[hd] = dst * eg[CH - 1:CH, sl] + _dot(do, q0h, TN)
                dqs.append(dq)
                dks.append(dk)
                dbs.append(db)
            dq = jnp.concatenate(dqs, axis=1)
            dk = jnp.concatenate(dks, axis=1)
            db = jnp.concatenate(dbs, axis=1)
            dlf = _split_dot(suf_ref[...], db) + jnp.concatenate(tails, axis=1)
            t = jnp.where(vc, dlf * jnp.where(f > F_FLOOR, 1.0 / f, 0.0) - dk, 0.0)
            dlb_ref[0:1, :] += jnp.sum(t * (1.0 - sig), axis=0, keepdims=True)
            dh_ref[rs, 768:1280] = jnp.where(vc, dq * SCALE, 0.0).astype(dh_ref.dtype)
            dh_ref[rs, 1280:1792] = (t * (1.0 - lbv) * (sig * (1.0 - sig))).astype(dh_ref.dtype)
            dh_ref[rs, 1792:2304] = jnp.where(vc, jnp.concatenate(dis, axis=1), 0.0).astype(dh_ref.dtype)
            dh_ref[rs, 2304:2816] = jnp.where(vc, jnp.concatenate(dgzs, axis=1), 0.0).astype(dh_ref.dtype)

    cur = lambda i: (nb - 1 - i, 0)
    prev = lambda i: (jnp.maximum((nb - 1 - i) * q4 - 1, 0), 0)
    nxt = lambda i: (jnp.minimum((nb - 1 - i) * q4 + q4, lasth), 0)
    fix = lambda i: (0, 0)
    body, in_specs, args = _after(
        dep, body,
        [pl.BlockSpec((BM, DIN), cur), pl.BlockSpec((HALO, DIN), prev), pl.BlockSpec((HALO, DIN), nxt),
         pl.BlockSpec((BM, D), cur), pl.BlockSpec((HALO, D), nxt), pl.BlockSpec((BM, HW), cur),
         pl.BlockSpec((MB, NH, HD, HD), lambda i: (nb - 1 - i, 0, 0, 0)),
         pl.BlockSpec((SUBLANES, CW), fix), pl.BlockSpec((PW, PW), fix), pl.BlockSpec((1, PW), fix),
         pl.BlockSpec((1, HD), fix), pl.BlockSpec((1, HW), fix), pl.BlockSpec(mat.shape, fix),
         pl.BlockSpec(suf.shape, fix)],
        [h, h, h, dy, dy, o, sp, taps, wbd, ps, gn, lb, mat, suf])
    return pl.pallas_call(
        body, name=name, grid=(nb,), in_specs=in_specs,
        out_specs=[pl.BlockSpec((BM, DIN), cur), pl.BlockSpec((SUBLANES, CW), fix), pl.BlockSpec((SUBLANES, PW), fix),
                   pl.BlockSpec((SUBLANES, HD), fix), pl.BlockSpec((SUBLANES, HW), fix), pl.BlockSpec((PW, PW), fix)],
        out_shape=[jax.ShapeDtypeStruct((R, DIN), MX), jax.ShapeDtypeStruct((SUBLANES, CW), F32),
                   jax.ShapeDtypeStruct((SUBLANES, PW), F32), jax.ShapeDtypeStruct((SUBLANES, HD), F32),
                   jax.ShapeDtypeStruct((SUBLANES, HW), F32), jax.ShapeDtypeStruct((PW, PW), F32)],
        scratch_shapes=[pltpu.VMEM((NH, HD, HD), F32)],
        compiler_params=_cp(("arbitrary",), 40 * 2 ** 20),
    )(*args)


def _sum_slots(recv, *, name):
    S, L, rows, cols = recv.shape
    tr = _row_tile(rows, S * cols * 4, 6 * 2 ** 20)

    def body(r_ref, o_ref):
        acc = r_ref[0]
        for s in range(1, S):
            acc = acc + r_ref[s]
        o_ref[...] = acc

    return pl.pallas_call(
        body, name=name, grid=(L, rows // tr),
        in_specs=[pl.BlockSpec((S, None, tr, cols), lambda l, i: (0, l, i, 0))],
        out_specs=pl.BlockSpec((None, tr, cols), lambda l, i: (l, i, 0)),
        out_shape=jax.ShapeDtypeStruct((L, rows, cols), F32),
        compiler_params=_cp(("parallel", "parallel"), 4 * S * tr * cols * 4),
    )(recv)


def _sum_own(recv, own, chip, *, name):
    S, rows, cols = recv.shape
    tr = _row_tile(rows, S * cols * 4, 6 * 2 ** 20)

    def body(me_ref, r_ref, o_ref, out_ref):
        me = me_ref[0]
        acc = None
        for s in range(S):
            t = jnp.where(me == s, o_ref[...], r_ref[s])
            acc = t if acc is None else acc + t
        out_ref[...] = acc

    grid_spec = pltpu.PrefetchScalarGridSpec(
        num_scalar_prefetch=1, grid=(rows // tr,),
        in_specs=[pl.BlockSpec((S, tr, cols), lambda i, me: (0, i, 0)),
                  pl.BlockSpec((None, tr, cols), lambda i, me: (me[0], i, 0))],
        out_specs=pl.BlockSpec((tr, cols), lambda i, me: (i, 0)))
    return pl.pallas_call(
        body, name=name, grid_spec=grid_spec, out_shape=jax.ShapeDtypeStruct((rows, cols), F32),
        compiler_params=_cp(("parallel",), 5 * S * tr * cols * 4),
    )(chip.reshape(1).astype(jnp.int32), recv, own)


def _adamw(w, m, v, ga, gb, *, layer, prev, name):
    L, rows, cols = w.shape
    tr = _row_tile(rows, cols * 4, 2 ** 20)
    two = gb is not None
    nin = 5 if two else 4

    def body(*refs):
        w_ref, m_ref, v_ref, a_ref = refs[:4]
        g_ref, d_ref, m2_ref, v2_ref = refs[-4:]
        g = a_ref[...] + refs[4][...] if two else a_ref[...]
        m2 = ADAM_B1 * m_ref[...] + (1.0 - ADAM_B1) * g
        v2 = ADAM_B2 * v_ref[...] + (1.0 - ADAM_B2) * (g * g)
        m_hat = m2 / (1.0 - ADAM_B1 ** ADAM_STEP)
        v_hat = v2 / (1.0 - ADAM_B2 ** ADAM_STEP)
        g_ref[...] = g
        d_ref[...] = -ADAM_LR * (m_hat / (jnp.sqrt(v_hat) + ADAM_EPS) + ADAM_WD * w_ref[...])
        m2_ref[...] = m2
        v2_ref[...] = v2

    spec = pl.BlockSpec((None, tr, cols), lambda i: (layer, i, 0))
    gspec = pl.BlockSpec((tr, cols), lambda i: (i, 0))
    args = [w, m, v, ga] + ([gb] if two else [])
    in_specs = [spec] * 3 + [gspec] * (nin - 3)
    aliases = {}
    if prev is not None:
        args += list(prev)
        in_specs += [ANY] * 4
        aliases = {nin + j: j for j in range(4)}
    sd = jax.ShapeDtypeStruct((L, rows, cols), F32)
    return pl.pallas_call(
        body, name=name, grid=(rows // tr,), in_specs=in_specs, out_specs=[spec] * 4,
        out_shape=[sd] * 4, input_output_aliases=aliases,
        compiler_params=_cp(("parallel",), 24 * tr * cols * 4),
    )(*args)


def _exchange(arrays, *, flips, n_slots, slot_of, scatter, self_copy, name):
    n = len(arrays)
    nf = len(flips)
    out_shapes = [jax.ShapeDtypeStruct(a.shape if scatter else (n_slots,) + a.shape, a.dtype) for a in arrays]

    def body(*refs):
        ins, outs = refs[:n], refs[n:2 * n]
        send_sems, recv_sems, loc_sems = refs[2 * n:]
        x, y, c = lax.axis_index("x"), lax.axis_index("y"), lax.axis_index("c")
        me = slot_of(x, y, c)
        peers = [(1 - x if fx else x, 1 - y if fy else y, 1 - c if fc else c) for fx, fy, fc in flips]
        local, remote = [], []
        for a in range(n):
            if self_copy:
                lc = pltpu.make_async_copy(ins[a].at[me] if scatter else ins[a], outs[a].at[me], loc_sems.at[a])
                lc.start()
                local.append(lc)
            for k, p in enumerate(peers):
                src = ins[a].at[slot_of(*p)] if scatter else ins[a]
                cp = pltpu.make_async_remote_copy(
                    src_ref=src, dst_ref=outs[a].at[me], send_sem=send_sems.at[a, k], recv_sem=recv_sems.at[a, k],
                    device_id=p, device_id_type=MESH)
                cp.start()
                remote.append(cp)
        for a in range(n):
            for k, p in enumerate(peers):
                src = ins[a].at[slot_of(*p)] if scatter else ins[a]
                pltpu.make_async_remote_copy(
                    src_ref=src, dst_ref=outs[a].at[slot_of(*p)], send_sem=send_sems.at[a, k],
                    recv_sem=recv_sems.at[a, k], device_id=p, device_id_type=MESH).wait_recv()
        for cp in remote:
            cp.wait_send()
        for lc in local:
            lc.wait()

    return pl.pallas_call(
        body, name=name, in_specs=[ANY] * n, out_specs=[ANY] * n, out_shape=out_shapes,
        scratch_shapes=[pltpu.SemaphoreType.DMA((n, nf)), pltpu.SemaphoreType.DMA((n, nf)),
                        pltpu.SemaphoreType.DMA((n,))],
        compiler_params=pltpu.CompilerParams(has_side_effects=True),
    )(*arrays)


ALL_FLIPS = [(fx, fy, fc) for fx in (0, 1) for fy in (0, 1) for fc in (0, 1) if fx or fy or fc]


def _dev_slot(x, y, c):
    return 4 * x + 2 * y + c


def _zero_slot(x, y, c):
    return 0


HBM_SPEC = pl.BlockSpec(memory_space=pltpu.HBM)
SEM_SPEC = pl.BlockSpec(memory_space=pltpu.SEMAPHORE)
N_PEER_CHIPS = 3


def _peer_chips(x, y, c):
    return [(1 - x, y, c), (x, 1 - y, c), (1 - x, 1 - y, c)]


def _send_start(srcs, *, scatter, dep=None, name):
    n = len(srcs)
    nc = n * N_PEER_CHIPS
    srcs = [pltpu.with_memory_space_constraint(s, pltpu.HBM) for s in srcs]
    land_shapes = [s.shape if scatter else (4,) + s.shape for s in srcs]
    lands = [pltpu.with_memory_space_constraint(lax.empty(sh, s.dtype), pltpu.HBM) for sh, s in zip(land_shapes, srcs)]

    deps = [] if dep is None else [dep]
    nd = len(deps)

    def body(*refs):
        ins, lnd = refs[:n], refs[n:2 * n]
        send_sems, recv_sems = refs[2 * n + nd:2 * n + nd + nc], refs[2 * n + nd + nc:2 * n + nd + 2 * nc]
        token = refs[-1]
        x, y, c = lax.axis_index("x"), lax.axis_index("y"), lax.axis_index("c")
        me = 2 * x + y
        for a in range(n):
            for k, p in enumerate(_peer_chips(x, y, c)):
                src = ins[a].at[2 * p[0] + p[1]] if scatter else ins[a]
                j = a * N_PEER_CHIPS + k
                pltpu.make_async_remote_copy(
                    src_ref=src, dst_ref=lnd[a].at[me], send_sem=send_sems[j], recv_sem=recv_sems[j],
                    device_id=p, device_id_type=MESH).start()
        token[...] = jnp.zeros(token.shape, token.dtype)

    sem = pltpu.SemaphoreType.DMA(())
    outs = pl.pallas_call(
        body, name=name,
        out_shape=(*[sem] * (2 * nc), *[pltpu.HBM(s.shape, s.dtype) for s in srcs],
                   *[pltpu.HBM(sh, s.dtype) for sh, s in zip(land_shapes, srcs)],
                   jax.ShapeDtypeStruct((SUBLANES, LANES), F32)),
        in_specs=[HBM_SPEC] * (2 * n) + [ANY] * nd,
        out_specs=(*[SEM_SPEC] * (2 * nc), *[HBM_SPEC] * (2 * n), pl.BlockSpec(memory_space=pltpu.VMEM)),
        input_output_aliases={i: 2 * nc + i for i in range(2 * n)},
        compiler_params=pltpu.CompilerParams(has_side_effects=pltpu.SideEffectType.DATAFLOW_SIDE_EFFECTING),
    )(*srcs, *lands, *deps)
    return dict(sems=list(outs[:2 * nc]), srcs=list(outs[2 * nc:2 * nc + n]),
                lands=list(outs[2 * nc + n:2 * nc + 2 * n]), token=outs[-1])


def _send_wait(h, *, scatter, after, name):
    n = len(h["srcs"])
    nc = n * N_PEER_CHIPS

    def body(*refs):
        ins, lnd = refs[:n], refs[n:2 * n]
        send_sems, recv_sems = refs[2 * n:2 * n + nc], refs[2 * n + nc:2 * n + 2 * nc]
        x, y, c = lax.axis_index("x"), lax.axis_index("y"), lax.axis_index("c")
        for a in range(n):
            for k, p in enumerate(_peer_chips(x, y, c)):
                slot = 2 * p[0] + p[1]
                j = a * N_PEER_CHIPS + k
                cp = pltpu.make_async_remote_copy(
                    src_ref=ins[a].at[slot] if scatter else ins[a], dst_ref=lnd[a].at[slot],
                    send_sem=send_sems[j], recv_sem=recv_sems[j], device_id=p, device_id_type=MESH)
                cp.wait_send()
                cp.wait_recv()

    thru = h["srcs"] + h["lands"]
    outs = pl.pallas_call(
        body, name=name, out_shape=tuple(pltpu.HBM(t.shape, t.dtype) for t in thru),
        in_specs=[HBM_SPEC] * (2 * n) + [SEM_SPEC] * (2 * nc) + [ANY] * len(after),
        out_specs=tuple([HBM_SPEC] * (2 * n)),
        input_output_aliases={i: i for i in range(2 * n)},
        compiler_params=pltpu.CompilerParams(has_side_effects=pltpu.SideEffectType.DATAFLOW_SIDE_EFFECTING),
    )(*thru, *h["sems"], *after)
    return list(outs[:n]), list(outs[n:])


def _assemble(land, own, chip, axis):
    return jnp.concatenate([jnp.where(chip == k, own, land[k]) for k in range(4)], axis=axis)


def _pack(arrs):
    flat = jnp.concatenate([a.reshape(-1).astype(F32) for a in arrs])
    tile = SUBLANES * LANES
    pad = (-flat.shape[0]) % tile
    return jnp.pad(flat, (0, pad)).reshape(-1, LANES)


def _unpack(buf, shapes):
    flat = buf.reshape(-1)
    out, off = [], 0
    for s in shapes:
        n = int(np.prod(s))
        out.append(flat[off:off + n].reshape(s))
        off += n
    return out


def _lower_bounds(hg_lower_bounds):
    p = jax.nn.softmax(hg_lower_bounds.astype(F32), axis=0)
    return jnp.cumsum(p, axis=0) - p[0]


def kernel(x, meta_tokens, hg_lower_bounds, w_in, w_conv, w_pool, pool_scale, hg_norm_g, w_o, ln1_g, ln1_b, w_up, w_ffn_conv, b_ffn_conv, w_down, ln2_g, ln2_b, loss_target, m_meta_tokens, m_hg_lower_bounds, m_w_in, m_w_conv, m_w_pool, m_pool_scale, m_hg_norm_g, m_w_o, m_ln1_g, m_ln1_b, m_w_up, m_w_ffn_conv, m_b_ffn_conv, m_w_down, m_ln2_g, m_ln2_b, v_meta_tokens, v_hg_lower_bounds, v_w_in, v_w_conv, v_w_pool, v_pool_scale, v_hg_norm_g, v_w_o, v_ln1_g, v_ln1_b, v_w_up, v_w_ffn_conv, v_b_ffn_conv, v_w_down, v_ln2_g, v_ln2_b):
    S = x.shape[1]
    R = S + ROW0
    Fq = w_down.shape[1]
    F = 4 * Fq
    F2 = 2 * F
    F2q = w_up.shape[2]
    assert x.shape == (1, S, D) and R % 384 == 0 and S % ROW0 == 0
    chip = 2 * lax.axis_index("x") + lax.axis_index("y")
    tm = 384
    tm_w = max(t for t in range(SUBLANES, 2113, SUBLANES) if R % t == 0)
    tb_ffn = 128

    small_shapes = [(N_META, D // 4), (DEPTH, CW // 4, 3), (DEPTH, F2q, 3)]
    wb_in, wb_o, wb_up, wb_down = (w.astype(MX) for w in (w_in, w_o, w_up, w_down))
    h_a = _send_start([wb_in[0], _pack([meta_tokens, w_conv, w_ffn_conv])], scatter=False, name="gather_a_start")
    (own_in, own_small), (l_in, l_small) = _send_wait(h_a, scatter=False, after=[h_a["token"]], name="gather_a_wait")
    h_b = _send_start([wb_o[0], wb_up[0], wb_down[0]], scatter=False, dep=l_in, name="gather_b_start")
    Win, Wo, Wup, Wdown = {}, {}, {}, {}
    Win[0] = _assemble(l_in, own_in, chip, 1)
    sm = [_unpack(jnp.where(chip == k, own_small, l_small[k]), small_shapes) for k in range(4)]
    meta_full = jnp.concatenate([sm[k][0] for k in range(4)], axis=1)
    wconv_full = jnp.concatenate([sm[k][1] for k in range(4)], axis=1)
    wffn_full = jnp.concatenate([sm[k][2] for k in range(4)], axis=1)
    taps_c = jnp.pad(wconv_full.transpose(0, 2, 1), ((0, 0), (0, SUBLANES - 3), (0, 0)))
    taps_f = jnp.pad(wffn_full.transpose(0, 2, 1), ((0, 0), (0, SUBLANES - 3), (0, 0)))
    wbd = jnp.stack([jax.scipy.linalg.block_diag(*[w_pool[l, g] for g in range(4)]) for l in range(DEPTH)]).astype(MX)
    lbs, lbs_vjp = jax.vjp(_lower_bounds, hg_lower_bounds)
    mat, suf = _hg_consts()

    def fwd_mixer(l, X, dep=None):
        h = _mm(X, Win[l], tm=tm, dep=dep, name=f"mm_in_{l}")
        y, o, sp = _mixer_fwd(h, taps_c[l], wbd[l], pool_scale[l].reshape(1, PW), hg_norm_g[l].reshape(1, HD),
                              lbs[l].reshape(1, HW), mat, name=f"mixer_fwd_{l}")
        return h, y, o, sp

    def fwd_rest(l, X, h, y, o, sp, dep=None):
        x1, xh1, r1 = _mm_ln(y, Wo[l], X, ln1_g[l], ln1_b[l], tm=tm, dep=dep, name=f"mm_o_ln_{l}")
        up, u, a = _mm_ffn_fwd(x1, Wup[l], taps_f[l], b_ffn_conv[l].reshape(1, F2), tm=tm // 2,
                               name=f"mm_up_ffn_{l}")
        x2, xh2, r2 = _mm_ln(a, Wdown[l], x1, ln2_g[l], ln2_b[l], tm=tm, name=f"mm_down_ln_{l}")
        return (X, h, y, o, sp, x1, xh1, r1, up, u, a, xh2, r2), x2

    X = jnp.concatenate([jnp.zeros((PADR, D), F32), meta_full, x[0]], axis=0)
    h, y, o, sp = fwd_mixer(0, X, dep=h_b["token"])
    (own_o, own_up, own_down), (l_o, l_up, l_down) = _send_wait(h_b, scatter=False, after=[y], name="gather_b_wait")
    Wo[0], Wup[0], Wdown[0] = (_assemble(l_o, own_o, chip, 0), _assemble(l_up, own_up, chip, 1),
                               _assemble(l_down, own_down, chip, 0))
    h_c = _send_start([wb_in[1], wb_o[1], wb_up[1], wb_down[1]], scatter=False, dep=l_o, name="gather_c_start")
    saved0, X1 = fwd_rest(0, X, h, y, o, sp, dep=h_c["token"])
    own_c, l_c = _send_wait(h_c, scatter=False, after=[X1], name="gather_c_wait")
    Win[1], Wo[1] = _assemble(l_c[0], own_c[0], chip, 1), _assemble(l_c[1], own_c[1], chip, 0)
    Wup[1], Wdown[1] = _assemble(l_c[2], own_c[2], chip, 1), _assemble(l_c[3], own_c[3], chip, 0)
    saved1, X2 = fwd_rest(1, X1, *fwd_mixer(1, X1))
    saved = [saved0, saved1]

    dxo = X2

    sc = {}

    def scatter(nm, l, g, dep=None):
        sc[nm, l] = _send_start([g], scatter=True, dep=dep, name=f"scatter_{nm}_{l}_start")
        return sc[nm, l]["token"]

    tok = None

    small_g = [None] * DEPTH
    for l in reversed(range(DEPTH)):
        X, h, y, o, sp, x1, xh1, r1, up, u, a, xh2, r2 = saved[l]
        if l == DEPTH - 1:
            dz2, gb2, da, sq = _ln_bwd_mm(dxo, xh2, r2, ln2_g[l], Wdown[l], tm=ROW0, tgt=loss_target[0],
                                          name=f"loss_ln2_bwd_da_{l}")
            loss = lax.psum(0.5 * jnp.sum(sq[0]) / D, ("x", "y", "c"))
        else:
            dz2, gb2, da = _ln_bwd_mm(dxo, xh2, r2, ln2_g[l], Wdown[l], tm=tm, dep=tok, name=f"ln2_bwd_da_{l}")
        tok = scatter("w_down", l, _wgrad(a, dz2, slabs_on_cols=False, tm=tm_w, tn=512, name=f"wgrad_down_{l}"))
        dup, facc = _ffn_bwd(da, up, u, taps_f[l], tb=tb_ffn, dep=tok, name=f"ffn_bwd_{l}")
        dx1 = _mm(dup, Wup[l], nt=True, res=dz2, tm=tm, name=f"mm_dx1_{l}")
        tok = scatter("w_up", l, _wgrad(x1, dup, slabs_on_cols=True, tm=tm_w, tn=F2q, name=f"wgrad_up_{l}"))
        dz1, gb1, dym = _ln_bwd_mm(dx1, xh1, r1, ln1_g[l], Wo[l], tm=tm, dep=tok, name=f"ln1_bwd_dym_{l}")
        tok = scatter("w_o", l, _wgrad(y, dz1, slabs_on_cols=False, tm=tm_w, tn=512, name=f"wgrad_o_{l}"))
        dh, dwc, dsc, dgn, dlb, dwbd = _mixer_bwd(
            h, dym, o, sp, taps_c[l], wbd[l], pool_scale[l].reshape(1, PW), hg_norm_g[l].reshape(1, HD),
            lbs[l].reshape(1, HW), mat, suf, dep=tok, name=f"mixer_bwd_{l}")
        tok = scatter("w_in", l, _wgrad(X, dh, slabs_on_cols=True, tm=tm_w, tn=DIN // 4, name=f"wgrad_in_{l}"))
        if l == 0:
            dx_head, dx_seq = _mm_dx_head(dh, Win[l], dz1, dep=tok, name=f"mm_dx_{l}")
        else:
            dxo = _mm(dh, Win[l], nt=True, res=dz1, tm=tm, dep=tok, name=f"mm_dx_{l}")
        small_g[l] = dict(
            lbs=dlb[0], w_conv=dwc[0:3].T, w_pool=jnp.stack([dwbd[64 * g:64 * g + 64, 64 * g:64 * g + 64] for g in range(4)]),
            pool_scale=dsc[0], hg_norm_g=dgn[0], ln1_g=gb1[0], ln1_b=gb1[1], w_ffn_conv=facc[0:3].T,
            b_ffn_conv=facc[3], ln2_g=gb2[0], ln2_b=gb2[1])
    grad_x = dx_seq[None]

    sg_names = ["lbs", "w_conv", "w_pool", "pool_scale", "hg_norm_g", "ln1_g", "ln1_b", "w_ffn_conv",
                "b_ffn_conv", "ln2_g", "ln2_b"]
    sg_list = [dx_head[PADR:ROW0]] + [jnp.stack([small_g[l][nm] for l in range(DEPTH)]) for nm in sg_names]
    sg_shapes = [a.shape for a in sg_list]
    packed = _pack(sg_list)
    (gathered,) = _exchange([packed], flips=ALL_FLIPS, n_slots=8, slot_of=_dev_slot, scatter=False, self_copy=True,
                            name="gather_small_grads")
    total = _sum_slots(gathered[:, None], name="sum_small_grads")[0]
    tot = dict(zip(["meta_tokens"] + sg_names, _unpack(total, sg_shapes)))
    (g_hg,) = lbs_vjp(tot["lbs"])
    small_grads = dict(
        meta_tokens=lax.dynamic_slice_in_dim(tot["meta_tokens"], chip * (D // 4), D // 4, axis=1),
        hg_lower_bounds=g_hg,
        w_conv=lax.dynamic_slice_in_dim(tot["w_conv"], chip * (CW // 4), CW // 4, axis=1),
        w_pool=tot["w_pool"], pool_scale=tot["pool_scale"], hg_norm_g=tot["hg_norm_g"],
        ln1_g=tot["ln1_g"], ln1_b=tot["ln1_b"],
        w_ffn_conv=lax.dynamic_slice_in_dim(tot["w_ffn_conv"], chip * F2q, F2q, axis=1),
        b_ffn_conv=tot["b_ffn_conv"], ln2_g=tot["ln2_g"], ln2_b=tot["ln2_b"])
    small_w = dict(meta_tokens=(meta_tokens, m_meta_tokens, v_meta_tokens),
                   hg_lower_bounds=(hg_lower_bounds, m_hg_lower_bounds, v_hg_lower_bounds),
                   w_conv=(w_conv, m_w_conv, v_w_conv), w_pool=(w_pool, m_w_pool, v_w_pool),
                   pool_scale=(pool_scale, m_pool_scale, v_pool_scale), hg_norm_g=(hg_norm_g, m_hg_norm_g, v_hg_norm_g),
                   ln1_g=(ln1_g, m_ln1_g, v_ln1_g), ln1_b=(ln1_b, m_ln1_b, v_ln1_b),
                   w_ffn_conv=(w_ffn_conv, m_w_ffn_conv, v_w_ffn_conv), b_ffn_conv=(b_ffn_conv, m_b_ffn_conv, v_b_ffn_conv),
                   ln2_g=(ln2_g, m_ln2_g, v_ln2_g), ln2_b=(ln2_b, m_ln2_b, v_ln2_b))
    names_s = list(small_w)
    shapes_s = [small_w[nm][0].shape for nm in names_s]
    pk = [_pack([small_w[nm][j] for nm in names_s])[None] for j in range(3)]
    pg = _pack([small_grads[nm] for nm in names_s])[None]
    outs_s = _adamw(pk[0], pk[1], pk[2], pg[0], None, layer=0, prev=None, name="adamw_small")
    small = {nm: [] for nm in names_s}
    for j in range(4):
        for nm, val in zip(names_s, _unpack(outs_s[j][0], shapes_s)):
            small[nm].append(val)

    big_w = dict(w_in=(w_in, m_w_in, v_w_in), w_o=(w_o, m_w_o, v_w_o), w_up=(w_up, m_w_up, v_w_up),
                 w_down=(w_down, m_w_down, v_w_down))
    big = {nm: None for nm in big_w}
    for l in reversed(range(DEPTH)):
        for group in ((("w_down", "w_up", "w_o", "w_in"),) if l else (("w_down", "w_up", "w_o"), ("w_in",))):
            last = group == ("w_in",)
            after = [outs_s[0]] + [big[k][0] for k in big_w if k != "w_in"] if last else [dx_seq]
            part = []
            for nm in group:
                (own,), (recv,) = _send_wait(sc[nm, l], scatter=True, after=after, name=f"scatter_{nm}_{l}_wait")
                part.append(_sum_own(recv, own, chip, name=f"sum_{nm}_{l}"))
            sib = _exchange(part, flips=[(0, 0, 1)], n_slots=1, slot_of=_zero_slot, scatter=False, self_copy=False,
                            name=f"swap_cores_{l}{'_in' if last else ''}")
            for k, nm in enumerate(group):
                w, m, v = big_w[nm]
                big[nm] = _adamw(w, m, v, part[k], sib[k][0], layer=l, prev=big[nm], name=f"adamw_{nm}_{l}")

    order = ["meta_tokens", "hg_lower_bounds", "w_in", "w_conv", "w_pool", "pool_scale", "hg_norm_g", "w_o",
             "ln1_g", "ln1_b", "w_up", "w_ffn_conv", "b_ffn_conv", "w_down", "ln2_g", "ln2_b"]
    res = {nm: (big[nm] if nm in big else small[nm]) for nm in order}
    outs = [loss, grad_x]
    for j in range(4):
        outs += [res[nm][j] for nm in order]
    return tuple(outs)
```

```python
import numpy as np

import jax
import jax.numpy as jnp
from jax import lax
from jax.experimental import pallas as pl
from jax.experimental.pallas import tpu as pltpu

F32 = jnp.float32
BF = jnp.bfloat16
MX = jnp.bfloat16

D = 1024
CW = 256
HW = 512
HD = 128
NH = 4
PW = 256
DIN = 3072
N_META = 16
CH = 64
MB = 4
BM = MB * CH
ROW0 = 256
PADR = ROW0 - N_META
HALO = 16
FH = 8
LEVELS = (32, 16, 8, 4, 2, 1)
DEPTH = 2
ALPHA = (2 * DEPTH) ** 0.25
LN_EPS = 1e-5
RMS_EPS = 1e-6
F_FLOOR = 1e-30
SCALE = HD ** -0.5
ADAM_LR, ADAM_B1, ADAM_B2, ADAM_EPS, ADAM_WD, ADAM_STEP = 0.001, 0.9, 0.999, 1e-08, 0.01, 10

VMEM_V7X = 64 * 2 ** 20
LANES = 128
SUBLANES = 8

NN = (((1,), (0,)), ((), ()))
NT = (((1,), (1,)), ((), ()))
TN = (((0,), (0,)), ((), ()))
MESH = pl.DeviceIdType.MESH
ANY = pl.BlockSpec(memory_space=pl.ANY)


def _dot(a, b, dn):
    return lax.dot_general(a.astype(MX), b.astype(MX), dn, preferred_element_type=F32)


def _cp(sem, est_bytes):
    lim = int(min(VMEM_V7X - 6 * 2 ** 20, max(32 * 2 ** 20, est_bytes)))
    return pltpu.CompilerParams(dimension_semantics=sem, vmem_limit_bytes=lim)


def _nbytes(shape, dtype):
    return int(np.prod(shape)) * jnp.dtype(dtype).itemsize


def _row_tile(rows, row_bytes, budget):
    best = SUBLANES
    for t in range(SUBLANES, rows + 1, SUBLANES):
        if rows % t == 0 and t * row_bytes <= budget:
            best = t
    return best


def _after(dep, body, in_specs, args):
    if dep is None:
        return body, list(in_specs), list(args)

    def body_after(dep_ref, *refs):
        body(*refs)

    return body_after, [ANY] + list(in_specs), [dep] + list(args)


def _mm(a, w, *, nt=False, res=None, tm, out_dtype=F32, zero_inert=False, dep=None, name):
    R, K = a.shape
    N = w.shape[0] if nt else w.shape[1]
    dn = NT if nt else NN

    def body(*refs):
        if res is None:
            a_ref, w_ref, o_ref = refs
        else:
            a_ref, w_ref, r_ref, o_ref = refs
        acc = _dot(a_ref[...], w_ref[...], dn)
        if res is not None:
            acc = acc + ALPHA * r_ref[...]
        if zero_inert:
            acc = jnp.where(_row_ids(pl.program_id(0) * tm, tm) >= PADR, acc, 0.0)
        o_ref[...] = acc.astype(out_dtype)

    in_specs = [pl.BlockSpec((tm, K), lambda i: (i, 0)), pl.BlockSpec(w.shape, lambda i: (0, 0))]
    args = [a, w]
    est = 2 * _nbytes((tm, K), a.dtype) + 2 * _nbytes(w.shape, w.dtype) + 3 * _nbytes((tm, N), F32)
    if res is not None:
        in_specs.append(pl.BlockSpec((tm, N), lambda i: (i, 0)))
        args.append(res)
        est += 2 * _nbytes((tm, N), F32)
    body, in_specs, args = _after(dep, body, in_specs, args)
    return pl.pallas_call(
        body, name=name, grid=(R // tm,), in_specs=in_specs,
        out_specs=pl.BlockSpec((tm, N), lambda i: (i, 0)),
        out_shape=jax.ShapeDtypeStruct((R, N), out_dtype),
        compiler_params=_cp(("parallel",), est + 4 * 2 ** 20),
    )(*args)


def _wgrad(a, b, *, slabs_on_cols, tm, tn, name):
    R, Ka = a.shape
    Nb = b.shape[1]
    if slabs_on_cols:
        out_shape = (4, Ka, Nb // 4)
        assert tn == Nb // 4
        out_spec = pl.BlockSpec((None, Ka, tn), lambda j, i: (j, 0, 0))
    else:
        out_shape = (4, Ka // 4, Nb)
        out_spec = pl.BlockSpec((4, Ka // 4, tn), lambda j, i: (0, 0, j))

    def body(a_ref, b_ref, o_ref):
        @pl.when(pl.program_id(1) == 0)
        def _():
            o_ref[...] = jnp.zeros(o_ref.shape, F32)

        acc = _dot(a_ref[...], b_ref[...], TN)
        o_ref[...] += acc.reshape(o_ref.shape)

    in_specs = [pl.BlockSpec((tm, Ka), lambda j, i: (i, 0)), pl.BlockSpec((tm, tn), lambda j, i: (i, j))]
    est = 2 * _nbytes((tm, Ka), a.dtype) + 2 * _nbytes((tm, tn), b.dtype) + 4 * _nbytes((Ka, tn), F32) \
        + _nbytes((tm, Ka), F32)
    return pl.pallas_call(
        body, name=name, grid=(Nb // tn, R // tm), in_specs=in_specs, out_specs=out_spec,
        out_shape=jax.ShapeDtypeStruct(out_shape, F32),
        compiler_params=_cp(("parallel", "arbitrary"), est + 4 * 2 ** 20),
    )(a, b)


def _mm_ln(a, w, xres, g, b, *, tm, dep=None, name):
    R, K = a.shape

    def body(a_ref, w_ref, x_ref, g_ref, b_ref, xo_ref, xh_ref, r_ref):
        z = ALPHA * x_ref[...] + _dot(a_ref[...], w_ref[...], NN)
        mu = jnp.mean(z, axis=-1, keepdims=True)
        zc = z - mu
        var = jnp.mean(zc * zc, axis=-1, keepdims=True)
        r = lax.rsqrt(var + LN_EPS)
        xh = zc * r
        xh_ref[...] = xh
        r_ref[...] = r
        xo_ref[...] = xh * g_ref[...] + b_ref[...]

    row = lambda i: (i, 0)
    fix = lambda i: (0, 0)
    est = 2 * _nbytes((tm, K), a.dtype) + 2 * _nbytes(w.shape, w.dtype) + 10 * _nbytes((tm, D), F32)
    body, in_specs, args = _after(
        dep, body, [pl.BlockSpec((tm, K), row), pl.BlockSpec(w.shape, fix), pl.BlockSpec((tm, D), row),
                    pl.BlockSpec((1, D), fix), pl.BlockSpec((1, D), fix)],
        [a, w, xres, g.reshape(1, D), b.reshape(1, D)])
    return pl.pallas_call(
        body, name=name, grid=(R // tm,), in_specs=in_specs,
        out_specs=[pl.BlockSpec((tm, D), row), pl.BlockSpec((tm, D), row), pl.BlockSpec((tm, 1), row)],
        out_shape=[jax.ShapeDtypeStruct((R, D), F32), jax.ShapeDtypeStruct((R, D), F32),
                   jax.ShapeDtypeStruct((R, 1), F32)],
        compiler_params=_cp(("parallel",), est + 4 * 2 ** 20),
    )(*args)


def _resident(shape):
    return pl.BlockSpec(shape, lambda i: (0,) * len(shape), pipeline_mode=pl.Buffered(1))


def _ln_bwd_mm(dx, xh, r, g, w, *, tm, tgt=None, dep=None, name):
    R = dx.shape[0]
    N = w.shape[0]
    assert tgt is None or tm == ROW0

    def body(*refs):
        if tgt is None:
            dx_ref, xh_ref, r_ref, g_ref, w_ref, dz_ref, gb_ref, o_ref = refs
        else:
            dx_ref, t_ref, xh_ref, r_ref, g_ref, w_ref, dz_ref, gb_ref, o_ref, sq_ref = refs
        i = pl.program_id(0)

        @pl.when(i == 0)
        def _():
            gb_ref[...] = jnp.zeros(gb_ref.shape, F32)
            if tgt is not None:
                sq_ref[...] = jnp.zeros(sq_ref.shape, F32)

        if tgt is None:
            dxv = dx_ref[...]
        else:
            err = jnp.where(i > 0, dx_ref[...] - t_ref[...], 0.0)
            sq_ref[0:1, :] += jnp.sum(err * err, axis=0, keepdims=True)
            dxv = err / D
        xhv = xh_ref[...]
        dyh = dxv * g_ref[...]
        m1 = jnp.mean(dyh, axis=-1, keepdims=True)
        m2 = jnp.mean(dyh * xhv, axis=-1, keepdims=True)
        dz = r_ref[...] * (dyh - m1 - xhv * m2)
        dz_ref[...] = dz
        gb_ref[0:1, :] += jnp.sum(dxv * xhv, axis=0, keepdims=True)
        gb_ref[1:2, :] += jnp.sum(dxv, axis=0, keepdims=True)
        o_ref[...] = _dot(dz, w_ref[...], NT)

    row = lambda i: (i, 0)
    fix = lambda i: (0, 0)
    in_specs = [pl.BlockSpec((tm, D), row), pl.BlockSpec((tm, D), row), pl.BlockSpec((tm, 1), row),
                pl.BlockSpec((1, D), fix), _resident(w.shape)]
    args = [dx, xh, r, g.reshape(1, D), w]
    out_specs = [pl.BlockSpec((tm, D), row), pl.BlockSpec((SUBLANES, D), fix), pl.BlockSpec((tm, N), row)]
    out_shape = [jax.ShapeDtypeStruct((R, D), F32), jax.ShapeDtypeStruct((SUBLANES, D), F32),
                 jax.ShapeDtypeStruct((R, N), F32)]
    if tgt is not None:
        in_specs.insert(1, pl.BlockSpec((tm, D), lambda i: (jnp.maximum(i - 1, 0), 0)))
        args.insert(1, tgt)
        out_specs.append(pl.BlockSpec((SUBLANES, D), fix))
        out_shape.append(jax.ShapeDtypeStruct((SUBLANES, D), F32))
    body, in_specs, args = _after(dep, body, in_specs, args)
    est = _nbytes(w.shape, w.dtype) + 16 * _nbytes((tm, D), F32) + 4 * _nbytes((tm, N), F32)
    return pl.pallas_call(
        body, name=name, grid=(R // tm,), in_specs=in_specs, out_specs=out_specs, out_shape=out_shape,
        compiler_params=_cp(("arbitrary",), est + 4 * 2 ** 20),
    )(*args)


def _mm_dx_head(dh, w, res, *, dep=None, name):
    R, K = dh.shape
    tm = ROW0

    def body(a_ref, w_ref, r_ref, head_ref, seq_ref):
        i = pl.program_id(0)
        acc = _dot(a_ref[...], w_ref[...], NT) + ALPHA * r_ref[...]

        @pl.when(i == 0)
        def _():
            head_ref[...] = acc

        @pl.when(i > 0)
        def _():
            seq_ref[...] = acc

    row = lambda i: (i, 0)
    body, in_specs, args = _after(
        dep, body, [pl.BlockSpec((tm, K), row), _resident(w.shape), pl.BlockSpec((tm, D), row)], [dh, w, res])
    est = _nbytes(w.shape, w.dtype) + 2 * _nbytes((tm, K), dh.dtype) + 8 * _nbytes((tm, D), F32)
    return pl.pallas_call(
        body, name=name, grid=(R // tm,), in_specs=in_specs,
        out_specs=[pl.BlockSpec((tm, D), lambda i: (0, 0)), pl.BlockSpec((tm, D), lambda i: (jnp.maximum(i - 1, 0), 0))],
        out_shape=[jax.ShapeDtypeStruct((ROW0, D), F32), jax.ShapeDtypeStruct((R - ROW0, D), F32)],
        compiler_params=_cp(("arbitrary",), est + 4 * 2 ** 20),
    )(*args)


def _mm_ffn_fwd(x1, w, taps, bias, *, tm, name):
    R, K = x1.shape
    F2 = w.shape[1]
    F = F2 // 2

    def body(x_ref, w_ref, t_ref, b_ref, up_ref, u_ref, a_ref, carry_ref):
        i = pl.program_id(0)

        @pl.when(i == 0)
        def _():
            carry_ref[...] = jnp.zeros(carry_ref.shape, F32)

        acc = _dot(x_ref[...], w_ref[...], NN)
        acc = jnp.where(_row_ids(i * tm, tm) >= PADR, acc, 0.0)
        up_ref[...] = acc.astype(up_ref.dtype)
        ue = jnp.concatenate([carry_ref[...], acc], axis=0)
        carry_ref[...] = acc[tm - FH:tm]
        u = (t_ref[0:1, :] * pltpu.roll(ue, 2, 0)[FH:] + t_ref[1:2, :] * pltpu.roll(ue, 1, 0)[FH:]
             + t_ref[2:3, :] * acc + b_ref[...])
        u_ref[...] = u
        gate = u[:, :F]
        val = u[:, F:]
        a_ref[...] = (gate * jax.nn.sigmoid(gate) * val).astype(a_ref.dtype)

    row = lambda i: (i, 0)
    est = _nbytes(w.shape, w.dtype) + 2 * _nbytes((tm, K), x1.dtype) + 10 * _nbytes((tm + FH, F2), F32)
    return pl.pallas_call(
        body, name=name, grid=(R // tm,),
        in_specs=[pl.BlockSpec((tm, K), row), _resident(w.shape), _resident((SUBLANES, F2)), _resident((1, F2))],
        out_specs=[pl.BlockSpec((tm, F2), row), pl.BlockSpec((tm, F2), row), pl.BlockSpec((tm, F), row)],
        out_shape=[jax.ShapeDtypeStruct((R, F2), MX), jax.ShapeDtypeStruct((R, F2), F32),
                   jax.ShapeDtypeStruct((R, F), MX)],
        scratch_shapes=[pltpu.VMEM((FH, F2), F32)],
        compiler_params=_cp(("arbitrary",), est + 4 * 2 ** 20),
    )(x1, w, taps, bias)


def _row_ids(start, n):
    return start + lax.broadcasted_iota(jnp.int32, (n, 1), 0)


def _ffn_bwd(da, up, u, taps, *, tb, dep=None, name):
    R, F2 = up.shape
    F = F2 // 2
    nh = tb // FH
    nb = R // tb
    last = R // FH - 1
    m = tb + FH

    def body(da_ref, dan_ref, x_ref, u_ref, un_ref, w_ref, du_ref, acc_ref):
        i = pl.program_id(0)

        @pl.when(i == 0)
        def _():
            acc_ref[...] = jnp.zeros(acc_ref.shape, F32)

        inside = (i < nb - 1).astype(F32)
        for j in range(F // LANES):
            gsl = slice(j * LANES, (j + 1) * LANES)
            vsl = slice(F + j * LANES, F + (j + 1) * LANES)
            gate = jnp.concatenate([u_ref[:, gsl], un_ref[:, gsl]], axis=0)
            val = jnp.concatenate([u_ref[:, vsl], un_ref[:, vsl]], axis=0)
            dae = jnp.concatenate([da_ref[:, gsl], dan_ref[:, gsl] * inside], axis=0)
            sg = jax.nn.sigmoid(gate)
            gs = gate * sg
            for sl, du in ((gsl, dae * val * (sg * (1.0 + gate - gs))), (vsl, dae * gs)):
                du0 = du[0:tb]
                du1 = pltpu.roll(du, m - 1, 0)[0:tb]
                du2 = pltpu.roll(du, m - 2, 0)[0:tb]
                du_ref[:, sl] = (w_ref[2:3, sl] * du0 + w_ref[1:2, sl] * du1 + w_ref[0:1, sl] * du2).astype(du_ref.dtype)
                x = x_ref[:, sl].astype(F32)
                acc_ref[0:1, sl] += jnp.sum(du2 * x, axis=0, keepdims=True)
                acc_ref[1:2, sl] += jnp.sum(du1 * x, axis=0, keepdims=True)
                acc_ref[2:3, sl] += jnp.sum(du0 * x, axis=0, keepdims=True)
                acc_ref[3:4, sl] += jnp.sum(du0, axis=0, keepdims=True)

        @pl.when(i * tb < PADR)
        def _():
            row = _row_ids(i * tb, tb)
            du_ref[...] = jnp.where(row >= PADR, du_ref[...], jnp.zeros((), du_ref.dtype))

    nxt = lambda i: (jnp.minimum(i * nh + nh, last), 0)
    cur = lambda i: (i, 0)
    fix = lambda i: (0, 0)
    est = 12 * _nbytes((tb + 2 * FH, F2), F32)
    body, in_specs, args = _after(
        dep, body, [pl.BlockSpec((tb, F), cur), pl.BlockSpec((FH, F), nxt),
                    pl.BlockSpec((tb, F2), cur), pl.BlockSpec((tb, F2), cur), pl.BlockSpec((FH, F2), nxt),
                    pl.BlockSpec((SUBLANES, F2), fix)], [da, da, up, u, u, taps])
    return pl.pallas_call(
        body, name=name, grid=(R // tb,), in_specs=in_specs,
        out_specs=[pl.BlockSpec((tb, F2), cur), pl.BlockSpec((SUBLANES, F2), fix)],
        out_shape=[jax.ShapeDtypeStruct((R, F2), MX), jax.ShapeDtypeStruct((SUBLANES, F2), F32)],
        compiler_params=_cp(("arbitrary",), est),
    )(*args)


def _hg_consts():
    t = np.arange(CH)[:, None]
    j = np.arange(CH)[None, :]
    low = (j <= t).astype(np.float32)
    blocks = [low, (j > t).astype(np.float32)]
    for m in LEVELS:
        ref = (t // (2 * m)) * 2 * m + m - 1
        blocks.append(low - (j <= ref).astype(np.float32))
    mat = np.concatenate(blocks, axis=0)
    suf = (j >= t).astype(np.float32)
    return jnp.asarray(mat, BF), jnp.asarray(suf, BF)


def _split_dot(mat, x):
    hi = x.astype(BF)
    lo = (x - hi.astype(F32)).astype(BF)
    return (lax.dot_general(mat, hi, NN, preferred_element_type=F32)
            + lax.dot_general(mat, lo, NN, preferred_element_type=F32))


def _lane_select(a2, a4, a8, a16):
    lane = lax.broadcasted_iota(jnp.int32, (1, PW), 1)
    return jnp.where(lane < 64, a2, jnp.where(lane < 128, a4, jnp.where(lane < 192, a8, a16)))


def _pool_count(row):
    win = _lane_select(2.0, 4.0, 8.0, 16.0)
    t1 = jnp.maximum((row - PADR + 1).astype(F32), 1.0)
    return jnp.minimum(t1, win)


def _gates(fz, lb, valid):
    sig = jax.nn.sigmoid(fz)
    f = lb + (1.0 - lb) * sig
    lf = jnp.where(valid, jnp.log(jnp.maximum(f, F_FLOOR)), 0.0)
    kk = jnp.where(valid, (1.0 - lb) * (1.0 - sig), 0.0)
    return sig, f, lf, kk


def _level_masks():
    tt = lax.broadcasted_iota(jnp.int32, (CH, CH), 0)
    ss = lax.broadcasted_iota(jnp.int32, (CH, CH), 1)
    xr = tt ^ ss
    low = tt > ss
    return tt, ss, [(xr >= m) & (xr < 2 * m) & low for m in LEVELS]


def _intra(q, kk, ex, sl, tt, ss, masks):
    a = jnp.where(tt == ss, jnp.sum(q * kk, axis=-1, keepdims=True), 0.0)
    parts = []
    for l in range(len(LEVELS)):
        e = jnp.exp(-jnp.abs(ex[(2 + l) * CH:(3 + l) * CH, sl]))
        ql = (q * e).astype(MX)
        kl = (kk * e).astype(MX)
        a = a + jnp.where(masks[l], _dot(ql, kl, NT), 0.0)
        parts.append((e, ql, kl))
    return a, parts


def _mixer_fwd(h, taps, wbd, ps, gn, lb, mat, *, name):
    R = h.shape[0]
    nb = R // BM
    q4 = BM // HALO

    def body(h_ref, hp_ref, w_ref, wbd_ref, ps_ref, gn_ref, lb_ref, mat_ref, y_ref, o_ref, sp_ref, st_ref):
        i = pl.program_id(0)

        @pl.when(i == 0)
        def _():
            st_ref[...] = jnp.zeros(st_ref.shape, F32)

        row = _row_ids(i * BM, BM)
        valid_b = row >= PADR
        rowe = _row_ids(i * BM - HALO, BM + HALO)
        valide = rowe >= PADR
        se = jnp.concatenate([hp_ref[:, 256:512] * hp_ref[:, 512:768], h_ref[:, 256:512] * h_ref[:, 512:768]], axis=0)
        se = jnp.where(valide, se, 0.0)
        conv = (w_ref[0:1, :] * pltpu.roll(se, 2, 0) + w_ref[1:2, :] * pltpu.roll(se, 1, 0)
                + w_ref[2:3, :] * se)[HALO:]
        y_ref[:, 0:CW] = (h_ref[:, 0:256] * conv).astype(y_ref.dtype)
        ve = jnp.where(valide, jnp.concatenate([hp_ref[:, 2816:3072], h_ref[:, 2816:3072]], axis=0), 0.0)
        s2 = ve + pltpu.roll(ve, 1, 0)
        s4 = s2 + pltpu.roll(s2, 2, 0)
        s8 = s4 + pltpu.roll(s4, 4, 0)
        s16 = s8 + pltpu.roll(s8, 8, 0)
        dp = (_lane_select(s2, s4, s8, s16) / _pool_count(rowe) - ve)[HALO:]
        y_ref[:, CW + HW:D] = (_dot(dp, wbd_ref[...], NN) * ps_ref[...]).astype(y_ref.dtype)
        lbv = lb_ref[...]
        tt, ss, masks = _level_masks()
        for c in range(MB):
            rs = slice(c * CH, (c + 1) * CH)
            _, _, lf, kk = _gates(h_ref[rs, 1280:1792], lbv, valid_b[rs])
            q = h_ref[rs, 768:1280] * SCALE
            ii = h_ref[rs, 1792:2304]
            gz = h_ref[rs, 2304:2816]
            ex = _split_dot(mat_ref[...], lf)
            eg = jnp.exp(ex[0:CH])
            q0 = q * eg
            kr = kk * jnp.exp(ex[CH:2 * CH])
            for hd in range(NH):
                sl = slice(HD * hd, HD * (hd + 1))
                st = st_ref[hd]
                sp_ref[c, hd] = st
                a, _ = _intra(q[:, sl], kk[:, sl], ex, sl, tt, ss, masks)
                o = _dot(q0[:, sl], st, NT) + _dot(a, ii[:, sl], NN)
                st_ref[hd] = st * eg[CH - 1:CH, sl] + _dot(ii[:, sl], kr[:, sl], TN)
                o_ref[rs, sl] = o
                r = lax.rsqrt(jnp.mean(o * o, axis=-1, keepdims=True) + RMS_EPS)
                g_ = gz[:, sl]
                y_ref[rs, CW + HD * hd:CW + HD * (hd + 1)] = (
                    o * r * gn_ref[...] * (g_ * jax.nn.sigmoid(g_))).astype(y_ref.dtype)

    fix = lambda i: (0, 0)
    return pl.pallas_call(
        body, name=name, grid=(nb,),
        in_specs=[pl.BlockSpec((BM, DIN), lambda i: (i, 0)),
                  pl.BlockSpec((HALO, DIN), lambda i: (jnp.maximum(i * q4 - 1, 0), 0)),
                  pl.BlockSpec((SUBLANES, CW), fix), pl.BlockSpec((PW, PW), fix), pl.BlockSpec((1, PW), fix),
                  pl.BlockSpec((1, HD), fix), pl.BlockSpec((1, HW), fix), pl.BlockSpec(mat.shape, fix)],
        out_specs=[pl.BlockSpec((BM, D), lambda i: (i, 0)), pl.BlockSpec((BM, HW), lambda i: (i, 0)),
                   pl.BlockSpec((MB, NH, HD, HD), lambda i: (i, 0, 0, 0))],
        out_shape=[jax.ShapeDtypeStruct((R, D), MX), jax.ShapeDtypeStruct((R, HW), F32),
                   jax.ShapeDtypeStruct((R // CH, NH, HD, HD), F32)],
        scratch_shapes=[pltpu.VMEM((NH, HD, HD), F32)],
        compiler_params=_cp(("arbitrary",), 32 * 2 ** 20),
    )(h, h, taps, wbd, ps, gn, lb, mat)


def _mixer_bwd(h, dy, o, sp, taps, wbd, ps, gn, lb, mat, suf, *, dep=None, name):
    R = h.shape[0]
    nb = R // BM
    q4 = BM // HALO
    lasth = R // HALO - 1

    def body(h_ref, hp_ref, hn_ref, dy_ref, dyn_ref, o_ref, sp_ref, w_ref, wbd_ref, ps_ref, gn_ref, lb_ref,
             mat_ref, suf_ref, dh_ref, dw_ref, dsc_ref, dgn_ref, dlb_ref, dwbd_ref, dst_ref):
        i = pl.program_id(0)
        b = nb - 1 - i

        @pl.when(i == 0)
        def _():
            dst_ref[...] = jnp.zeros(dst_ref.shape, F32)
            dw_ref[...] = jnp.zeros(dw_ref.shape, F32)
            dsc_ref[...] = jnp.zeros(dsc_ref.shape, F32)
            dgn_ref[...] = jnp.zeros(dgn_ref.shape, F32)
            dlb_ref[...] = jnp.zeros(dlb_ref.shape, F32)
            dwbd_ref[...] = jnp.zeros(dwbd_ref.shape, F32)

        ne = BM + 2 * HALO
        nc = BM + HALO
        row = _row_ids(b * BM, BM)
        valid = row >= PADR
        rowe = _row_ids(b * BM - HALO, ne)
        valide = (rowe >= PADR) & (rowe < R)
        rown = rowe[HALO:]
        validn = rown < R

        def cat3(lo, hi):
            return jnp.concatenate([hp_ref[:, lo:hi], h_ref[:, lo:hi], hn_ref[:, lo:hi]], axis=0)

        def catd(lo, hi):
            return jnp.where(validn, jnp.concatenate([dy_ref[:, lo:hi], dyn_ref[:, lo:hi]], axis=0), 0.0)

        w0, w1, w2 = w_ref[0:1, :], w_ref[1:2, :], w_ref[2:3, :]
        cce = cat3(256, 512)
        cve = cat3(512, 768)
        se = jnp.where(valide, cce * cve, 0.0)
        sm2 = pltpu.roll(se, 2, 0)
        sm1 = pltpu.roll(se, 1, 0)
        conv = (w0 * sm2 + w1 * sm1 + w2 * se)[HALO:HALO + BM]
        cbn = jnp.concatenate([h_ref[:, 0:256], hn_ref[:, 0:256]], axis=0)
        dconv = catd(0, CW) * cbn
        ds = w2 * dconv + w1 * pltpu.roll(dconv, nc - 1, 0) + w0 * pltpu.roll(dconv, nc - 2, 0)
        ds = jnp.where(valid, ds[0:BM], 0.0)
        dcv = dconv[0:BM]
        dw_ref[0:1, :] += jnp.sum(dcv * sm2[HALO:HALO + BM], axis=0, keepdims=True)
        dw_ref[1:2, :] += jnp.sum(dcv * sm1[HALO:HALO + BM], axis=0, keepdims=True)
        dw_ref[2:3, :] += jnp.sum(dcv * se[HALO:HALO + BM], axis=0, keepdims=True)
        dh_ref[:, 0:256] = jnp.where(valid, dy_ref[:, 0:CW] * conv, 0.0).astype(dh_ref.dtype)
        dh_ref[:, 256:512] = (ds * h_ref[:, 512:768]).astype(dh_ref.dtype)
        dh_ref[:, 512:768] = (ds * h_ref[:, 256:512]).astype(dh_ref.dtype)

        ve = jnp.where(valide, cat3(2816, 3072), 0.0)
        s2 = ve + pltpu.roll(ve, 1, 0)
        s4 = s2 + pltpu.roll(s2, 2, 0)
        s8 = s4 + pltpu.roll(s4, 4, 0)
        s16 = s8 + pltpu.roll(s8, 8, 0)
        cnte = _pool_count(rowe)
        dp = (_lane_select(s2, s4, s8, s16) / cnte - ve)[HALO:HALO + BM]
        dyp = catd(CW + HW, D)
        pre = _dot(dp, wbd_ref[...], NN)
        dsc_ref[0:1, :] += jnp.sum(dyp[0:BM] * pre, axis=0, keepdims=True)
        dyps = dyp * ps_ref[...]
        dd = _dot(dyps, wbd_ref[...], NT)
        dwbd_ref[...] += _dot(dp, dyps[0:BM], TN)
        e = dd / cnte[HALO:]
        t2 = e + pltpu.roll(e, nc - 1, 0)
        t4 = t2 + pltpu.roll(t2, nc - 2, 0)
        t8 = t4 + pltpu.roll(t4, nc - 4, 0)
        t16 = t8 + pltpu.roll(t8, nc - 8, 0)
        dv = (_lane_select(t2, t4, t8, t16) - dd)[0:BM]
        dh_ref[:, 2816:3072] = jnp.where(valid, dv, 0.0).astype(dh_ref.dtype)

        lbv = lb_ref[...]
        tt, ss, masks = _level_masks()
        gnv = gn_ref[...]
        for c in reversed(range(MB)):
            rs = slice(c * CH, (c + 1) * CH)
            vc = valid[rs]
            sig, f, lf, kk = _gates(h_ref[rs, 1280:1792], lbv, vc)
            q = h_ref[rs, 768:1280] * SCALE
            ii = h_ref[rs, 1792:2304]
            gz = h_ref[rs, 2304:2816]
            ex = _split_dot(mat_ref[...], lf)
            eg = jnp.exp(ex[0:CH])
            egr = jnp.exp(ex[CH:2 * CH])
            q0 = q * eg
            kr = kk * egr
            dqs, dks, dis, dgzs, tails, dbs = [], [], [], [], [], []
            for hd in range(NH):
                sl = slice(HD * hd, HD * (hd + 1))
                ov = o_ref[rs, sl]
                r = lax.rsqrt(jnp.mean(ov * ov, axis=-1, keepdims=True) + RMS_EPS)
                oh = ov * r
                g_ = gz[:, sl]
                sg = jax.nn.sigmoid(g_)
                dyv = dy_ref[rs, CW + HD * hd:CW + HD * (hd + 1)]
                don = dyv * (g_ * sg)
                dgzs.append(dyv * (oh * gnv) * (sg * (1.0 + g_ * (1.0 - sg))))
                dgn_ref[0:1, :] += jnp.sum(don * oh, axis=0, keepdims=True)
                doh = don * gnv
                do = r * (doh - oh * jnp.mean(doh * oh, axis=-1, keepdims=True))
                st = sp_ref[c, hd]
                dst = dst_ref[hd]
                qh, kh, ih = q[:, sl], kk[:, sl], ii[:, sl]
                a, parts = _intra(qh, kh, ex, sl, tt, ss, masks)
                da = jnp.where(tt >= ss, _dot(do, ih, NT), 0.0)
                dis.append(_dot(a, do, TN) + _dot(kr[:, sl], dst, NT))
                q0h = q0[:, sl].astype(MX)
                krh = kr[:, sl].astype(MX)
                dq0 = _dot(do, st, NN)
                dkr = _dot(ih, dst, NN)
                dq = dq0 * eg[:, sl]
                dk = dkr * egr[:, sl]
                kdk = krh.astype(F32) * dkr
                db = q0h.astype(F32) * dq0 - kdk
                tails.append(jnp.sum(kdk, axis=0, keepdims=True)
                             + eg[CH - 1:CH, sl] * jnp.sum(dst * st, axis=0, keepdims=True))
                dga = jnp.sum(jnp.where(tt == ss, da, 0.0), axis=-1, keepdims=True)
                dq = dq + dga * kh
                dk = dk + dga * qh
                for l in range(len(LEVELS)):
                    e_l, ql, kl = parts[l]
                    dpl = jnp.where(masks[l], da, 0.0).astype(MX)
                    dql = _dot(dpl, kl, NN)
                    dkl = _dot(dpl, ql, TN)
                    dq = dq + dql * e_l
                    dk = dk + dkl * e_l
                    db = db + (ql.astype(F32) * dql - kl.astype(F32) * dkl)
                dst_ref[hd] = dst * eg[CH - 1:CH, sl] + _dot(do, q0h, TN)
                dqs.append(dq)
                dks.append(dk)
                dbs.append(db)
            dq = jnp.concatenate(dqs, axis=1)
            dk = jnp.concatenate(dks, axis=1)
            db = jnp.concatenate(dbs, axis=1)
            dlf = _split_dot(suf_ref[...], db) + jnp.concatenate(tails, axis=1)
            t = jnp.where(vc, dlf * jnp.where(f > F_FLOOR, 1.0 / f, 0.0) - dk, 0.0)
            dlb_ref[0:1, :] += jnp.sum(t * (1.0 - sig), axis=0, keepdims=True)
            dh_ref[rs, 768:1280] = jnp.where(vc, dq * SCALE, 0.0).astype(dh_ref.dtype)
            dh_ref[rs, 1280:1792] = (t * (1.0 - lbv) * (sig * (1.0 - sig))).astype(dh_ref.dtype)
            dh_ref[rs, 1792:2304] = jnp.where(vc, jnp.concatenate(dis, axis=1), 0.0).astype(dh_ref.dtype)
            dh_ref[rs, 2304:2816] = jnp.where(vc, jnp.concatenate(dgzs, axis=1), 0.0).astype(dh_ref.dtype)

    cur = lambda i: (nb - 1 - i, 0)
    prev = lambda i: (jnp.maximum((nb - 1 - i) * q4 - 1, 0), 0)
    nxt = lambda i: (jnp.minimum((nb - 1 - i) * q4 + q4, lasth), 0)
    fix = lambda i: (0, 0)
    body, in_specs, args = _after(
        dep, body,
        [pl.BlockSpec((BM, DIN), cur), pl.BlockSpec((HALO, DIN), prev), pl.BlockSpec((HALO, DIN), nxt),
         pl.BlockSpec((BM, D), cur), pl.BlockSpec((HALO, D), nxt), pl.BlockSpec((BM, HW), cur),
         pl.BlockSpec((MB, NH, HD, HD), lambda i: (nb - 1 - i, 0, 0, 0)),
         pl.BlockSpec((SUBLANES, CW), fix), pl.BlockSpec((PW, PW), fix), pl.BlockSpec((1, PW), fix),
         pl.BlockSpec((1, HD), fix), pl.BlockSpec((1, HW), fix), pl.BlockSpec(mat.shape, fix),
         pl.BlockSpec(suf.shape, fix)],
        [h, h, h, dy, dy, o, sp, taps, wbd, ps, gn, lb, mat, suf])
    return pl.pallas_call(
        body, name=name, grid=(nb,), in_specs=in_specs,
        out_specs=[pl.BlockSpec((BM, DIN), cur), pl.BlockSpec((SUBLANES, CW), fix), pl.BlockSpec((SUBLANES, PW), fix),
                   pl.BlockSpec((SUBLANES, HD), fix), pl.BlockSpec((SUBLANES, HW), fix), pl.BlockSpec((PW, PW), fix)],
        out_shape=[jax.ShapeDtypeStruct((R, DIN), MX), jax.ShapeDtypeStruct((SUBLANES, CW), F32),
                   jax.ShapeDtypeStruct((SUBLANES, PW), F32), jax.ShapeDtypeStruct((SUBLANES, HD), F32),
                   jax.ShapeDtypeStruct((SUBLANES, HW), F32), jax.ShapeDtypeStruct((PW, PW), F32)],
        scratch_shapes=[pltpu.VMEM((NH, HD, HD), F32)],
        compiler_params=_cp(("arbitrary",), 40 * 2 ** 20),
    )(*args)


def _sum_slots(recv, *, name):
    S, L, rows, cols = recv.shape
    tr = _row_tile(rows, S * cols * 4, 6 * 2 ** 20)

    def body(r_ref, o_ref):
        acc = r_ref[0]
        for s in range(1, S):
            acc = acc + r_ref[s]
        o_ref[...] = acc

    return pl.pallas_call(
        body, name=name, grid=(L, rows // tr),
        in_specs=[pl.BlockSpec((S, None, tr, cols), lambda l, i: (0, l, i, 0))],
        out_specs=pl.BlockSpec((None, tr, cols), lambda l, i: (l, i, 0)),
        out_shape=jax.ShapeDtypeStruct((L, rows, cols), F32),
        compiler_params=_cp(("parallel", "parallel"), 4 * S * tr * cols * 4),
    )(recv)


def _sum_own(recv, own, chip, *, name):
    S, rows, cols = recv.shape
    tr = _row_tile(rows, S * cols * 4, 6 * 2 ** 20)

    def body(me_ref, r_ref, o_ref, out_ref):
        me = me_ref[0]
        acc = None
        for s in range(S):
            t = jnp.where(me == s, o_ref[...], r_ref[s])
            acc = t if acc is None else acc + t
        out_ref[...] = acc

    grid_spec = pltpu.PrefetchScalarGridSpec(
        num_scalar_prefetch=1, grid=(rows // tr,),
        in_specs=[pl.BlockSpec((S, tr, cols), lambda i, me: (0, i, 0)),
                  pl.BlockSpec((None, tr, cols), lambda i, me: (me[0], i, 0))],
        out_specs=pl.BlockSpec((tr, cols), lambda i, me: (i, 0)))
    return pl.pallas_call(
        body, name=name, grid_spec=grid_spec, out_shape=jax.ShapeDtypeStruct((rows, cols), F32),
        compiler_params=_cp(("parallel",), 5 * S * tr * cols * 4),
    )(chip.reshape(1).astype(jnp.int32), recv, own)


def _adamw(w, m, v, ga, gb, *, layer, prev, name):
    L, rows, cols = w.shape
    tr = _row_tile(rows, cols * 4, 2 ** 20)
    two = gb is not None
    nin = 5 if two else 4

    def body(*refs):
        w_ref, m_ref, v_ref, a_ref = refs[:4]
        g_ref, d_ref, m2_ref, v2_ref = refs[-4:]
        g = a_ref[...] + refs[4][...] if two else a_ref[...]
        m2 = ADAM_B1 * m_ref[...] + (1.0 - ADAM_B1) * g
        v2 = ADAM_B2 * v_ref[...] + (1.0 - ADAM_B2) * (g * g)
        m_hat = m2 / (1.0 - ADAM_B1 ** ADAM_STEP)
        v_hat = v2 / (1.0 - ADAM_B2 ** ADAM_STEP)
        g_ref[...] = g
        d_ref[...] = -ADAM_LR * (m_hat / (jnp.sqrt(v_hat) + ADAM_EPS) + ADAM_WD * w_ref[...])
        m2_ref[...] = m2
        v2_ref[...] = v2

    spec = pl.BlockSpec((None, tr, cols), lambda i: (layer, i, 0))
    gspec = pl.BlockSpec((tr, cols), lambda i: (i, 0))
    args = [w, m, v, ga] + ([gb] if two else [])
    in_specs = [spec] * 3 + [gspec] * (nin - 3)
    aliases = {}
    if prev is not None:
        args += list(prev)
        in_specs += [ANY] * 4
        aliases = {nin + j: j for j in range(4)}
    sd = jax.ShapeDtypeStruct((L, rows, cols), F32)
    return pl.pallas_call(
        body, name=name, grid=(rows // tr,), in_specs=in_specs, out_specs=[spec] * 4,
        out_shape=[sd] * 4, input_output_aliases=aliases,
        compiler_params=_cp(("parallel",), 24 * tr * cols * 4),
    )(*args)


def _exchange(arrays, *, flips, n_slots, slot_of, scatter, self_copy, name):
    n = len(arrays)
    nf = len(flips)
    out_shapes = [jax.ShapeDtypeStruct(a.shape if scatter else (n_slots,) + a.shape, a.dtype) for a in arrays]

    def body(*refs):
        ins, outs = refs[:n], refs[n:2 * n]
        send_sems, recv_sems, loc_sems = refs[2 * n:]
        x, y, c = lax.axis_index("x"), lax.axis_index("y"), lax.axis_index("c")
        me = slot_of(x, y, c)
        peers = [(1 - x if fx else x, 1 - y if fy else y, 1 - c if fc else c) for fx, fy, fc in flips]
        local, remote = [], []
        for a in range(n):
            if self_copy:
                lc = pltpu.make_async_copy(ins[a].at[me] if scatter else ins[a], outs[a].at[me], loc_sems.at[a])
                lc.start()
                local.append(lc)
            for k, p in enumerate(peers):
                src = ins[a].at[slot_of(*p)] if scatter else ins[a]
                cp = pltpu.make_async_remote_copy(
                    src_ref=src, dst_ref=outs[a].at[me], send_sem=send_sems.at[a, k], recv_sem=recv_sems.at[a, k],
                    device_id=p, device_id_type=MESH)
                cp.start()
                remote.append(cp)
        for a in range(n):
            for k, p in enumerate(peers):
                src = ins[a].at[slot_of(*p)] if scatter else ins[a]
                pltpu.make_async_remote_copy(
                    src_ref=src, dst_ref=outs[a].at[slot_of(*p)], send_sem=send_sems.at[a, k],
                    recv_sem=recv_sems.at[a, k], device_id=p, device_id_type=MESH).wait_recv()
        for cp in remote:
            cp.wait_send()
        for lc in local:
            lc.wait()

    return pl.pallas_call(
        body, name=name, in_specs=[ANY] * n, out_specs=[ANY] * n, out_shape=out_shapes,
        scratch_shapes=[pltpu.SemaphoreType.DMA((n, nf)), pltpu.SemaphoreType.DMA((n, nf)),
                        pltpu.SemaphoreType.DMA((n,))],
        compiler_params=pltpu.CompilerParams(has_side_effects=True),
    )(*arrays)


ALL_FLIPS = [(fx, fy, fc) for fx in (0, 1) for fy in (0, 1) for fc in (0, 1) if fx or fy or fc]


def _dev_slot(x, y, c):
    return 4 * x + 2 * y + c


def _zero_slot(x, y, c):
    return 0


HBM_SPEC = pl.BlockSpec(memory_space=pltpu.HBM)
SEM_SPEC = pl.BlockSpec(memory_space=pltpu.SEMAPHORE)
N_PEER_CHIPS = 3


def _peer_chips(x, y, c):
    return [(1 - x, y, c), (x, 1 - y, c), (1 - x, 1 - y, c)]


def _send_start(srcs, *, scatter, dep=None, name):
    n = len(srcs)
    nc = n * N_PEER_CHIPS
    srcs = [pltpu.with_memory_space_constraint(s, pltpu.HBM) for s in srcs]
    land_shapes = [s.shape if scatter else (4,) + s.shape for s in srcs]
    lands = [pltpu.with_memory_space_constraint(lax.empty(sh, s.dtype), pltpu.HBM) for sh, s in zip(land_shapes, srcs)]

    deps = [] if dep is None else [dep]
    nd = len(deps)

    def body(*refs):
        ins, lnd = refs[:n], refs[n:2 * n]
        send_sems, recv_sems = refs[2 * n + nd:2 * n + nd + nc], refs[2 * n + nd + nc:2 * n + nd + 2 * nc]
        token = refs[-1]
        x, y, c = lax.axis_index("x"), lax.axis_index("y"), lax.axis_index("c")
        me = 2 * x + y
        for a in range(n):
            for k, p in enumerate(_peer_chips(x, y, c)):
                src = ins[a].at[2 * p[0] + p[1]] if scatter else ins[a]
                j = a * N_PEER_CHIPS + k
                pltpu.make_async_remote_copy(
                    src_ref=src, dst_ref=lnd[a].at[me], send_sem=send_sems[j], recv_sem=recv_sems[j],
                    device_id=p, device_id_type=MESH).start()
        token[...] = jnp.zeros(token.shape, token.dtype)

    sem = pltpu.SemaphoreType.DMA(())
    outs = pl.pallas_call(
        body, name=name,
        out_shape=(*[sem] * (2 * nc), *[pltpu.HBM(s.shape, s.dtype) for s in srcs],
                   *[pltpu.HBM(sh, s.dtype) for sh, s in zip(land_shapes, srcs)],
                   jax.ShapeDtypeStruct((SUBLANES, LANES), F32)),
        in_specs=[HBM_SPEC] * (2 * n) + [ANY] * nd,
        out_specs=(*[SEM_SPEC] * (2 * nc), *[HBM_SPEC] * (2 * n), pl.BlockSpec(memory_space=pltpu.VMEM)),
        input_output_aliases={i: 2 * nc + i for i in range(2 * n)},
        compiler_params=pltpu.CompilerParams(has_side_effects=pltpu.SideEffectType.DATAFLOW_SIDE_EFFECTING),
    )(*srcs, *lands, *deps)
    return dict(sems=list(outs[:2 * nc]), srcs=list(outs[2 * nc:2 * nc + n]),
                lands=list(outs[2 * nc + n:2 * nc + 2 * n]), token=outs[-1])


def _send_wait(h, *, scatter, after, name):
    n = len(h["srcs"])
    nc = n * N_PEER_CHIPS

    def body(*refs):
        ins, lnd = refs[:n], refs[n:2 * n]
        send_sems, recv_sems = refs[2 * n:2 * n + nc], refs[2 * n + nc:2 * n + 2 * nc]
        x, y, c = lax.axis_index("x"), lax.axis_index("y"), lax.axis_index("c")
        for a in range(n):
            for k, p in enumerate(_peer_chips(x, y, c)):
                slot = 2 * p[0] + p[1]
                j = a * N_PEER_CHIPS + k
                cp = pltpu.make_async_remote_copy(
                    src_ref=ins[a].at[slot] if scatter else ins[a], dst_ref=lnd[a].at[slot],
                    send_sem=send_sems[j], recv_sem=recv_sems[j], device_id=p, device_id_type=MESH)
                cp.wait_send()
                cp.wait_recv()

    thru = h["srcs"] + h["lands"]
    outs = pl.pallas_call(
        body, name=name, out_shape=tuple(pltpu.HBM(t.shape, t.dtype) for t in thru),
        in_specs=[HBM_SPEC] * (2 * n) + [SEM_SPEC] * (2 * nc) + [ANY] * len(after),
        out_specs=tuple([HBM_SPEC] * (2 * n)),
        input_output_aliases={i: i for i in range(2 * n)},
        compiler_params=pltpu.CompilerParams(has_side_effects=pltpu.SideEffectType.DATAFLOW_SIDE_EFFECTING),
    )(*thru, *h["sems"], *after)
    return list(outs[:n]), list(outs[n:])


def _assemble(land, own, chip, axis):
    return jnp.concatenate([jnp.where(chip == k, own, land[k]) for k in range(4)], axis=axis)


def _pack(arrs):
    flat = jnp.concatenate([a.reshape(-1).astype(F32) for a in arrs])
    tile = SUBLANES * LANES
    pad = (-flat.shape[0]) % tile
    return jnp.pad(flat, (0, pad)).reshape(-1, LANES)


def _unpack(buf, shapes):
    flat = buf.reshape(-1)
    out, off = [], 0
    for s in shapes:
        n = int(np.prod(s))
        out.append(flat[off:off + n].reshape(s))
        off += n
    return out


def _lower_bounds(hg_lower_bounds):
    p = jax.nn.softmax(hg_lower_bounds.astype(F32), axis=0)
    return jnp.cumsum(p, axis=0) - p[0]


def kernel(x, meta_tokens, hg_lower_bounds, w_in, w_conv, w_pool, pool_scale, hg_norm_g, w_o, ln1_g, ln1_b, w_up, w_ffn_conv, b_ffn_conv, w_down, ln2_g, ln2_b, loss_target, m_meta_tokens, m_hg_lower_bounds, m_w_in, m_w_conv, m_w_pool, m_pool_scale, m_hg_norm_g, m_w_o, m_ln1_g, m_ln1_b, m_w_up, m_w_ffn_conv, m_b_ffn_conv, m_w_down, m_ln2_g, m_ln2_b, v_meta_tokens, v_hg_lower_bounds, v_w_in, v_w_conv, v_w_pool, v_pool_scale, v_hg_norm_g, v_w_o, v_ln1_g, v_ln1_b, v_w_up, v_w_ffn_conv, v_b_ffn_conv, v_w_down, v_ln2_g, v_ln2_b):
    S = x.shape[1]
    R = S + ROW0
    Fq = w_down.shape[1]
    F = 4 * Fq
    F2 = 2 * F
    F2q = w_up.shape[2]
    assert x.shape == (1, S, D) and R % 384 == 0 and S % ROW0 == 0
    chip = 2 * lax.axis_index("x") + lax.axis_index("y")
    tm = 384
    tm_w = max(t for t in range(SUBLANES, 2113, SUBLANES) if R % t == 0)
    tb_ffn = 128

    small_shapes = [(N_META, D // 4), (DEPTH, CW // 4, 3), (DEPTH, F2q, 3)]
    wb_in, wb_o, wb_up, wb_down = (w.astype(MX) for w in (w_in, w_o, w_up, w_down))
    h_s = _send_start([_pack([meta_tokens, w_conv, w_ffn_conv])], scatter=False, name="gather_small_start")
    h_a = _send_start([wb_in[0]], scatter=False, dep=h_s["token"], name="gather_a_start")
    (own_small,), (l_small,) = _send_wait(h_s, scatter=False, after=[h_a["token"]], name="gather_small_wait")
    sm = [_unpack(jnp.where(chip == k, own_small, l_small[k]), small_shapes) for k in range(4)]
    meta_full = jnp.concatenate([sm[k][0] for k in range(4)], axis=1)
    X = jnp.concatenate([jnp.zeros((PADR, D), F32), meta_full, x[0]], axis=0)
    (own_in,), (l_in,) = _send_wait(h_a, scatter=False, after=[X], name="gather_a_wait")
    h_b = _send_start([wb_o[0], wb_up[0], wb_down[0]], scatter=False, dep=l_in, name="gather_b_start")
    Win, Wo, Wup, Wdown = {}, {}, {}, {}
    Win[0] = _assemble(l_in, own_in, chip, 1)
    wconv_full = jnp.concatenate([sm[k][1] for k in range(4)], axis=1)
    wffn_full = jnp.concatenate([sm[k][2] for k in range(4)], axis=1)
    taps_c = jnp.pad(wconv_full.transpose(0, 2, 1), ((0, 0), (0, SUBLANES - 3), (0, 0)))
    taps_f = jnp.pad(wffn_full.transpose(0, 2, 1), ((0, 0), (0, SUBLANES - 3), (0, 0)))
    wbd = jnp.stack([jax.scipy.linalg.block_diag(*[w_pool[l, g] for g in range(4)]) for l in range(DEPTH)]).astype(MX)
    lbs, lbs_vjp = jax.vjp(_lower_bounds, hg_lower_bounds)
    mat, suf = _hg_consts()

    def fwd_mixer(l, X, dep=None):
        h = _mm(X, Win[l], tm=tm, dep=dep, name=f"mm_in_{l}")
        y, o, sp = _mixer_fwd(h, taps_c[l], wbd[l], pool_scale[l].reshape(1, PW), hg_norm_g[l].reshape(1, HD),
                              lbs[l].reshape(1, HW), mat, name=f"mixer_fwd_{l}")
        return h, y, o, sp

    def fwd_rest(l, X, h, y, o, sp, dep=None):
        x1, xh1, r1 = _mm_ln(y, Wo[l], X, ln1_g[l], ln1_b[l], tm=tm, dep=dep, name=f"mm_o_ln_{l}")
        up, u, a = _mm_ffn_fwd(x1, Wup[l], taps_f[l], b_ffn_conv[l].reshape(1, F2), tm=tm // 2,
                               name=f"mm_up_ffn_{l}")
        x2, xh2, r2 = _mm_ln(a, Wdown[l], x1, ln2_g[l], ln2_b[l], tm=tm, name=f"mm_down_ln_{l}")
        return (X, h, y, o, sp, x1, xh1, r1, up, u, a, xh2, r2), x2

    h, y, o, sp = fwd_mixer(0, X, dep=h_b["token"])
    (own_o, own_up, own_down), (l_o, l_up, l_down) = _send_wait(h_b, scatter=False, after=[y], name="gather_b_wait")
    Wo[0], Wup[0], Wdown[0] = (_assemble(l_o, own_o, chip, 0), _assemble(l_up, own_up, chip, 1),
                               _assemble(l_down, own_down, chip, 0))
    h_c = _send_start([wb_in[1], wb_o[1], wb_up[1], wb_down[1]], scatter=False, dep=l_o, name="gather_c_start")
    saved0, X1 = fwd_rest(0, X, h, y, o, sp, dep=h_c["token"])
    own_c, l_c = _send_wait(h_c, scatter=False, after=[X1], name="gather_c_wait")
    Win[1], Wo[1] = _assemble(l_c[0], own_c[0], chip, 1), _assemble(l_c[1], own_c[1], chip, 0)
    Wup[1], Wdown[1] = _assemble(l_c[2], own_c[2], chip, 1), _assemble(l_c[3], own_c[3], chip, 0)
    saved1, X2 = fwd_rest(1, X1, *fwd_mixer(1, X1))
    saved = [saved0, saved1]

    dxo = X2

    sc = {}

    def scatter(nm, l, g, dep=None):
        sc[nm, l] = _send_start([g], scatter=True, dep=dep, name=f"scatter_{nm}_{l}_start")
        return sc[nm, l]["token"]

    tok = None

    small_g = [None] * DEPTH
    for l in reversed(range(DEPTH)):
        X, h, y, o, sp, x1, xh1, r1, up, u, a, xh2, r2 = saved[l]
        if l == DEPTH - 1:
            dz2, gb2, da, sq = _ln_bwd_mm(dxo, xh2, r2, ln2_g[l], Wdown[l], tm=ROW0, tgt=loss_target[0],
                                          name=f"loss_ln2_bwd_da_{l}")
            loss = lax.psum(0.5 * jnp.sum(sq[0]) / D, ("x", "y", "c"))
        else:
            dz2, gb2, da = _ln_bwd_mm(dxo, xh2, r2, ln2_g[l], Wdown[l], tm=tm, dep=tok, name=f"ln2_bwd_da_{l}")
        tok = scatter("w_down", l, _wgrad(a, dz2, slabs_on_cols=False, tm=tm_w, tn=512, name=f"wgrad_down_{l}"))
        dup, facc = _ffn_bwd(da, up, u, taps_f[l], tb=tb_ffn, dep=tok, name=f"ffn_bwd_{l}")
        dx1 = _mm(dup, Wup[l], nt=True, res=dz2, tm=tm, name=f"mm_dx1_{l}")
        tok = scatter("w_up", l, _wgrad(x1, dup, slabs_on_cols=True, tm=tm_w, tn=F2q, name=f"wgrad_up_{l}"))
        dz1, gb1, dym = _ln_bwd_mm(dx1, xh1, r1, ln1_g[l], Wo[l], tm=tm, dep=tok, name=f"ln1_bwd_dym_{l}")
        tok = scatter("w_o", l, _wgrad(y, dz1, slabs_on_cols=False, tm=tm_w, tn=512, name=f"wgrad_o_{l}"))
        dh, dwc, dsc, dgn, dlb, dwbd = _mixer_bwd(
            h, dym, o, sp, taps_c[l], wbd[l], pool_scale[l].reshape(1, PW), hg_norm_g[l].reshape(1, HD),
            lbs[l].reshape(1, HW), mat, suf, dep=tok, name=f"mixer_bwd_{l}")
        tok = scatter("w_in", l, _wgrad(X, dh, slabs_on_cols=True, tm=tm_w, tn=DIN // 4, name=f"wgrad_in_{l}"))
        if l == 0:
            dx_head, dx_seq = _mm_dx_head(dh, Win[l], dz1, dep=tok, name=f"mm_dx_{l}")
        else:
            dxo = _mm(dh, Win[l], nt=True, res=dz1, tm=tm, dep=tok, name=f"mm_dx_{l}")
        small_g[l] = dict(
            lbs=dlb[0], w_conv=dwc[0:3].T, w_pool=jnp.stack([dwbd[64 * g:64 * g + 64, 64 * g:64 * g + 64] for g in range(4)]),
            pool_scale=dsc[0], hg_norm_g=dgn[0], ln1_g=gb1[0], ln1_b=gb1[1], w_ffn_conv=facc[0:3].T,
            b_ffn_conv=facc[3], ln2_g=gb2[0], ln2_b=gb2[1])
    grad_x = dx_seq[None]

    sg_names = ["lbs", "w_conv", "w_pool", "pool_scale", "hg_norm_g", "ln1_g", "ln1_b", "w_ffn_conv",
                "b_ffn_conv", "ln2_g", "ln2_b"]
    sg_list = [dx_head[PADR:ROW0]] + [jnp.stack([small_g[l][nm] for l in range(DEPTH)]) for nm in sg_names]
    sg_shapes = [a.shape for a in sg_list]
    packed = _pack(sg_list)
    (gathered,) = _exchange([packed], flips=ALL_FLIPS, n_slots=8, slot_of=_dev_slot, scatter=False, self_copy=True,
                            name="gather_small_grads")
    total = _sum_slots(gathered[:, None], name="sum_small_grads")[0]
    tot = dict(zip(["meta_tokens"] + sg_names, _unpack(total, sg_shapes)))
    (g_hg,) = lbs_vjp(tot["lbs"])
    small_grads = dict(
        meta_tokens=lax.dynamic_slice_in_dim(tot["meta_tokens"], chip * (D // 4), D // 4, axis=1),
        hg_lower_bounds=g_hg,
        w_conv=lax.dynamic_slice_in_dim(tot["w_conv"], chip * (CW // 4), CW // 4, axis=1),
        w_pool=tot["w_pool"], pool_scale=tot["pool_scale"], hg_norm_g=tot["hg_norm_g"],
        ln1_g=tot["ln1_g"], ln1_b=tot["ln1_b"],
        w_ffn_conv=lax.dynamic_slice_in_dim(tot["w_ffn_conv"], chip * F2q, F2q, axis=1),
        b_ffn_conv=tot["b_ffn_conv"], ln2_g=tot["ln2_g"], ln2_b=tot["ln2_b"])
    small_w = dict(meta_tokens=(meta_tokens, m_meta_tokens, v_meta_tokens),
                   hg_lower_bounds=(hg_lower_bounds, m_hg_lower_bounds, v_hg_lower_bounds),
                   w_conv=(w_conv, m_w_conv, v_w_conv), w_pool=(w_pool, m_w_pool, v_w_pool),
                   pool_scale=(pool_scale, m_pool_scale, v_pool_scale), hg_norm_g=(hg_norm_g, m_hg_norm_g, v_hg_norm_g),
                   ln1_g=(ln1_g, m_ln1_g, v_ln1_g), ln1_b=(ln1_b, m_ln1_b, v_ln1_b),
                   w_ffn_conv=(w_ffn_conv, m_w_ffn_conv, v_w_ffn_conv), b_ffn_conv=(b_ffn_conv, m_b_ffn_conv, v_b_ffn_conv),
                   ln2_g=(ln2_g, m_ln2_g, v_ln2_g), ln2_b=(ln2_b, m_ln2_b, v_ln2_b))
    names_s = list(small_w)
    shapes_s = [small_w[nm][0].shape for nm in names_s]
    pk = [_pack([small_w[nm][j] for nm in names_s])[None] for j in range(3)]
    pg = _pack([small_grads[nm] for nm in names_s])[None]
    outs_s = _adamw(pk[0], pk[1], pk[2], pg[0], None, layer=0, prev=None, name="adamw_small")
    small = {nm: [] for nm in names_s}
    for j in range(4):
        for nm, val in zip(names_s, _unpack(outs_s[j][0], shapes_s)):
            small[nm].append(val)

    big_w = dict(w_in=(w_in, m_w_in, v_w_in), w_o=(w_o, m_w_o, v_w_o), w_up=(w_up, m_w_up, v_w_up),
                 w_down=(w_down, m_w_down, v_w_down))
    big = {nm: None for nm in big_w}
    for l in reversed(range(DEPTH)):
        for group in ((("w_down", "w_up", "w_o", "w_in"),) if l else (("w_down", "w_up", "w_o"), ("w_in",))):
            last = group == ("w_in",)
            after = [outs_s[0]] + [big[k][0] for k in big_w if k != "w_in"] if last else [dx_seq]
            part = []
            for nm in group:
                (own,), (recv,) = _send_wait(sc[nm, l], scatter=True, after=after, name=f"scatter_{nm}_{l}_wait")
                part.append(_sum_own(recv, own, chip, name=f"sum_{nm}_{l}"))
            sib = _exchange(part, flips=[(0, 0, 1)], n_slots=1, slot_of=_zero_slot, scatter=False, self_copy=False,
                            name=f"swap_cores_{l}{'_in' if last else ''}")
            for k, nm in enumerate(group):
                w, m, v = big_w[nm]
                big[nm] = _adamw(w, m, v, part[k], sib[k][0], layer=l, prev=big[nm], name=f"adamw_{nm}_{l}")

    order = ["meta_tokens", "hg_lower_bounds", "w_in", "w_conv", "w_pool", "pool_scale", "hg_norm_g", "w_o",
             "ln1_g", "ln1_b", "w_up", "w_ffn_conv", "b_ffn_conv", "w_down", "ln2_g", "ln2_b"]
    res = {nm: (big[nm] if nm in big else small[nm]) for nm in order}
    outs = [loss, grad_x]
    for j in range(4):
        outs += [res[nm][j] for nm in order]
    return tuple(outs)
```

```python
import numpy as np

import jax
import jax.numpy as jnp
from jax import lax
from jax.experimental import pallas as pl
from jax.experimental.pallas import tpu as pltpu

F32 = jnp.float32
BF = jnp.bfloat16
MX = jnp.bfloat16

D = 1024
CW = 256
HW = 512
HD = 128
NH = 4
PW = 256
DIN = 3072
N_META = 16
CH = 64
MB = 4
BM = MB * CH
ROW0 = 256
PADR = ROW0 - N_META
HALO = 16
FH = 8
LEVELS = (32, 16, 8, 4, 2, 1)
DEPTH = 2
ALPHA = (2 * DEPTH) ** 0.25
LN_EPS = 1e-5
RMS_EPS = 1e-6
F_FLOOR = 1e-30
SCALE = HD ** -0.5
ADAM_LR, ADAM_B1, ADAM_B2, ADAM_EPS, ADAM_WD, ADAM_STEP = 0.001, 0.9, 0.999, 1e-08, 0.01, 10

VMEM_V7X = 64 * 2 ** 20
LANES = 128
SUBLANES = 8

NN = (((1,), (0,)), ((), ()))
NT = (((1,), (1,)), ((), ()))
TN = (((0,), (0,)), ((), ()))
MESH = pl.DeviceIdType.MESH
ANY = pl.BlockSpec(memory_space=pl.ANY)


def _dot(a, b, dn):
    return lax.dot_general(a.astype(MX), b.astype(MX), dn, preferred_element_type=F32)


def _cp(sem, est_bytes):
    lim = int(min(VMEM_V7X - 6 * 2 ** 20, max(32 * 2 ** 20, est_bytes)))
    return pltpu.CompilerParams(dimension_semantics=sem, vmem_limit_bytes=lim)


def _nbytes(shape, dtype):
    return int(np.prod(shape)) * jnp.dtype(dtype).itemsize


def _row_tile(rows, row_bytes, budget):
    best = SUBLANES
    for t in range(SUBLANES, rows + 1, SUBLANES):
        if rows % t == 0 and t * row_bytes <= budget:
            best = t
    return best


def _after(dep, body, in_specs, args):
    if dep is None:
        return body, list(in_specs), list(args)

    def body_after(dep_ref, *refs):
        body(*refs)

    return body_after, [ANY] + list(in_specs), [dep] + list(args)


def _mm(a, w, *, nt=False, res=None, tm, out_dtype=F32, zero_inert=False, dep=None, name):
    R, K = a.shape
    N = w.shape[0] if nt else w.shape[1]
    dn = NT if nt else NN

    def body(*refs):
        if res is None:
            a_ref, w_ref, o_ref = refs
        else:
            a_ref, w_ref, r_ref, o_ref = refs
        acc = _dot(a_ref[...], w_ref[...], dn)
        if res is not None:
            acc = acc + ALPHA * r_ref[...]
        if zero_inert:
            acc = jnp.where(_row_ids(pl.program_id(0) * tm, tm) >= PADR, acc, 0.0)
        o_ref[...] = acc.astype(out_dtype)

    in_specs = [pl.BlockSpec((tm, K), lambda i: (i, 0)), pl.BlockSpec(w.shape, lambda i: (0, 0))]
    args = [a, w]
    est = 2 * _nbytes((tm, K), a.dtype) + 2 * _nbytes(w.shape, w.dtype) + 3 * _nbytes((tm, N), F32)
    if res is not None:
        in_specs.append(pl.BlockSpec((tm, N), lambda i: (i, 0)))
        args.append(res)
        est += 2 * _nbytes((tm, N), F32)
    body, in_specs, args = _after(dep, body, in_specs, args)
    return pl.pallas_call(
        body, name=name, grid=(R // tm,), in_specs=in_specs,
        out_specs=pl.BlockSpec((tm, N), lambda i: (i, 0)),
        out_shape=jax.ShapeDtypeStruct((R, N), out_dtype),
        compiler_params=_cp(("parallel",), est + 4 * 2 ** 20),
    )(*args)


def _wgrad(a, b, *, slabs_on_cols, tm, tn, name):
    R, Ka = a.shape
    Nb = b.shape[1]
    if slabs_on_cols:
        out_shape = (4, Ka, Nb // 4)
        assert tn == Nb // 4
        out_spec = pl.BlockSpec((None, Ka, tn), lambda j, i: (j, 0, 0))
    else:
        out_shape = (4, Ka // 4, Nb)
        out_spec = pl.BlockSpec((4, Ka // 4, tn), lambda j, i: (0, 0, j))

    def body(a_ref, b_ref, o_ref):
        @pl.when(pl.program_id(1) == 0)
        def _():
            o_ref[...] = jnp.zeros(o_ref.shape, F32)

        acc = _dot(a_ref[...], b_ref[...], TN)
        o_ref[...] += acc.reshape(o_ref.shape)

    in_specs = [pl.BlockSpec((tm, Ka), lambda j, i: (i, 0)), pl.BlockSpec((tm, tn), lambda j, i: (i, j))]
    est = 2 * _nbytes((tm, Ka), a.dtype) + 2 * _nbytes((tm, tn), b.dtype) + 4 * _nbytes((Ka, tn), F32) \
        + _nbytes((tm, Ka), F32)
    return pl.pallas_call(
        body, name=name, grid=(Nb // tn, R // tm), in_specs=in_specs, out_specs=out_spec,
        out_shape=jax.ShapeDtypeStruct(out_shape, F32),
        compiler_params=_cp(("parallel", "arbitrary"), est + 4 * 2 ** 20),
    )(a, b)


def _mm_ln(a, w, xres, g, b, *, tm, dep=None, name):
    R, K = a.shape

    def body(a_ref, w_ref, x_ref, g_ref, b_ref, xo_ref, xh_ref, r_ref):
        z = ALPHA * x_ref[...] + _dot(a_ref[...], w_ref[...], NN)
        mu = jnp.mean(z, axis=-1, keepdims=True)
        zc = z - mu
        var = jnp.mean(zc * zc, axis=-1, keepdims=True)
        r = lax.rsqrt(var + LN_EPS)
        xh = zc * r
        xh_ref[...] = xh
        r_ref[...] = r
        xo_ref[...] = xh * g_ref[...] + b_ref[...]

    row = lambda i: (i, 0)
    fix = lambda i: (0, 0)
    est = 2 * _nbytes((tm, K), a.dtype) + 2 * _nbytes(w.shape, w.dtype) + 10 * _nbytes((tm, D), F32)
    body, in_specs, args = _after(
        dep, body, [pl.BlockSpec((tm, K), row), pl.BlockSpec(w.shape, fix), pl.BlockSpec((tm, D), row),
                    pl.BlockSpec((1, D), fix), pl.BlockSpec((1, D), fix)],
        [a, w, xres, g.reshape(1, D), b.reshape(1, D)])
    return pl.pallas_call(
        body, name=name, grid=(R // tm,), in_specs=in_specs,
        out_specs=[pl.BlockSpec((tm, D), row), pl.BlockSpec((tm, D), row), pl.BlockSpec((tm, 1), row)],
        out_shape=[jax.ShapeDtypeStruct((R, D), F32), jax.ShapeDtypeStruct((R, D), F32),
                   jax.ShapeDtypeStruct((R, 1), F32)],
        compiler_params=_cp(("parallel",), est + 4 * 2 ** 20),
    )(*args)


def _resident(shape):
    return pl.BlockSpec(shape, lambda i: (0,) * len(shape), pipeline_mode=pl.Buffered(1))


def _ln_bwd_mm(dx, xh, r, g, w, *, tm, tgt=None, dep=None, name):
    R = dx.shape[0]
    N = w.shape[0]
    assert tgt is None or tm == ROW0

    def body(*refs):
        if tgt is None:
            dx_ref, xh_ref, r_ref, g_ref, w_ref, dz_ref, gb_ref, o_ref = refs
        else:
            dx_ref, t_ref, xh_ref, r_ref, g_ref, w_ref, dz_ref, gb_ref, o_ref, sq_ref = refs
        i = pl.program_id(0)

        @pl.when(i == 0)
        def _():
            gb_ref[...] = jnp.zeros(gb_ref.shape, F32)
            if tgt is not None:
                sq_ref[...] = jnp.zeros(sq_ref.shape, F32)

        if tgt is None:
            dxv = dx_ref[...]
        else:
            err = jnp.where(i > 0, dx_ref[...] - t_ref[...], 0.0)
            sq_ref[0:1, :] += jnp.sum(err * err, axis=0, keepdims=True)
            dxv = err / D
        xhv = xh_ref[...]
        dyh = dxv * g_ref[...]
        m1 = jnp.mean(dyh, axis=-1, keepdims=True)
        m2 = jnp.mean(dyh * xhv, axis=-1, keepdims=True)
        dz = r_ref[...] * (dyh - m1 - xhv * m2)
        dz_ref[...] = dz
        gb_ref[0:1, :] += jnp.sum(dxv * xhv, axis=0, keepdims=True)
        gb_ref[1:2, :] += jnp.sum(dxv, axis=0, keepdims=True)
        o_ref[...] = _dot(dz, w_ref[...], NT)

    row = lambda i: (i, 0)
    fix = lambda i: (0, 0)
    in_specs = [pl.BlockSpec((tm, D), row), pl.BlockSpec((tm, D), row), pl.BlockSpec((tm, 1), row),
                pl.BlockSpec((1, D), fix), _resident(w.shape)]
    args = [dx, xh, r, g.reshape(1, D), w]
    out_specs = [pl.BlockSpec((tm, D), row), pl.BlockSpec((SUBLANES, D), fix), pl.BlockSpec((tm, N), row)]
    out_shape = [jax.ShapeDtypeStruct((R, D), F32), jax.ShapeDtypeStruct((SUBLANES, D), F32),
                 jax.ShapeDtypeStruct((R, N), F32)]
    if tgt is not None:
        in_specs.insert(1, pl.BlockSpec((tm, D), lambda i: (jnp.maximum(i - 1, 0), 0)))
        args.insert(1, tgt)
        out_specs.append(pl.BlockSpec((SUBLANES, D), fix))
        out_shape.append(jax.ShapeDtypeStruct((SUBLANES, D), F32))
    body, in_specs, args = _after(dep, body, in_specs, args)
    est = _nbytes(w.shape, w.dtype) + 16 * _nbytes((tm, D), F32) + 4 * _nbytes((tm, N), F32)
    return pl.pallas_call(
        body, name=name, grid=(R // tm,), in_specs=in_specs, out_specs=out_specs, out_shape=out_shape,
        compiler_params=_cp(("arbitrary",), est + 4 * 2 ** 20),
    )(*args)


def _mm_dx_head(dh, w, res, *, dep=None, name):
    R, K = dh.shape
    tm = ROW0

    def body(a_ref, w_ref, r_ref, head_ref, seq_ref):
        i = pl.program_id(0)
        acc = _dot(a_ref[...], w_ref[...], NT) + ALPHA * r_ref[...]

        @pl.when(i == 0)
        def _():
            head_ref[...] = acc

        @pl.when(i > 0)
        def _():
            seq_ref[...] = acc

    row = lambda i: (i, 0)
    body, in_specs, args = _after(
        dep, body, [pl.BlockSpec((tm, K), row), _resident(w.shape), pl.BlockSpec((tm, D), row)], [dh, w, res])
    est = _nbytes(w.shape, w.dtype) + 2 * _nbytes((tm, K), dh.dtype) + 8 * _nbytes((tm, D), F32)
    return pl.pallas_call(
        body, name=name, grid=(R // tm,), in_specs=in_specs,
        out_specs=[pl.BlockSpec((tm, D), lambda i: (0, 0)), pl.BlockSpec((tm, D), lambda i: (jnp.maximum(i - 1, 0), 0))],
        out_shape=[jax.ShapeDtypeStruct((ROW0, D), F32), jax.ShapeDtypeStruct((R - ROW0, D), F32)],
        compiler_params=_cp(("arbitrary",), est + 4 * 2 ** 20),
    )(*args)


def _mm_ffn_fwd(x1, w, taps, bias, *, tm, name):
    R, K = x1.shape
    F2 = w.shape[1]
    F = F2 // 2

    def body(x_ref, w_ref, t_ref, b_ref, up_ref, u_ref, a_ref, carry_ref):
        i = pl.program_id(0)

        @pl.when(i == 0)
        def _():
            carry_ref[...] = jnp.zeros(carry_ref.shape, F32)

        acc = _dot(x_ref[...], w_ref[...], NN)
        acc = jnp.where(_row_ids(i * tm, tm) >= PADR, acc, 0.0)
        up_ref[...] = acc.astype(up_ref.dtype)
        ue = jnp.concatenate([carry_ref[...], acc], axis=0)
        carry_ref[...] = acc[tm - FH:tm]
        u = (t_ref[0:1, :] * pltpu.roll(ue, 2, 0)[FH:] + t_ref[1:2, :] * pltpu.roll(ue, 1, 0)[FH:]
             + t_ref[2:3, :] * acc + b_ref[...])
        u_ref[...] = u
        gate = u[:, :F]
        val = u[:, F:]
        a_ref[...] = (gate * jax.nn.sigmoid(gate) * val).astype(a_ref.dtype)

    row = lambda i: (i, 0)
    est = _nbytes(w.shape, w.dtype) + 2 * _nbytes((tm, K), x1.dtype) + 10 * _nbytes((tm + FH, F2), F32)
    return pl.pallas_call(
        body, name=name, grid=(R // tm,),
        in_specs=[pl.BlockSpec((tm, K), row), _resident(w.shape), _resident((SUBLANES, F2)), _resident((1, F2))],
        out_specs=[pl.BlockSpec((tm, F2), row), pl.BlockSpec((tm, F2), row), pl.BlockSpec((tm, F), row)],
        out_shape=[jax.ShapeDtypeStruct((R, F2), MX), jax.ShapeDtypeStruct((R, F2), F32),
                   jax.ShapeDtypeStruct((R, F), MX)],
        scratch_shapes=[pltpu.VMEM((FH, F2), F32)],
        compiler_params=_cp(("arbitrary",), est + 4 * 2 ** 20),
    )(x1, w, taps, bias)


def _row_ids(start, n):
    return start + lax.broadcasted_iota(jnp.int32, (n, 1), 0)


def _ffn_bwd(da, up, u, taps, *, tb, dep=None, name):
    R, F2 = up.shape
    F = F2 // 2
    nh = tb // FH
    nb = R // tb
    last = R // FH - 1
    m = tb + FH

    def body(da_ref, dan_ref, x_ref, u_ref, un_ref, w_ref, du_ref, acc_ref):
        i = pl.program_id(0)

        @pl.when(i == 0)
        def _():
            acc_ref[...] = jnp.zeros(acc_ref.shape, F32)

        inside = (i < nb - 1).astype(F32)
        for j in range(F // LANES):
            gsl = slice(j * LANES, (j + 1) * LANES)
            vsl = slice(F + j * LANES, F + (j + 1) * LANES)
            gate = jnp.concatenate([u_ref[:, gsl], un_ref[:, gsl]], axis=0)
            val = jnp.concatenate([u_ref[:, vsl], un_ref[:, vsl]], axis=0)
            dae = jnp.concatenate([da_ref[:, gsl], dan_ref[:, gsl] * inside], axis=0)
            sg = jax.nn.sigmoid(gate)
            gs = gate * sg
            for sl, du in ((gsl, dae * val * (sg * (1.0 + gate - gs))), (vsl, dae * gs)):
                du0 = du[0:tb]
                du1 = pltpu.roll(du, m - 1, 0)[0:tb]
                du2 = pltpu.roll(du, m - 2, 0)[0:tb]
                du_ref[:, sl] = (w_ref[2:3, sl] * du0 + w_ref[1:2, sl] * du1 + w_ref[0:1, sl] * du2).astype(du_ref.dtype)
                x = x_ref[:, sl].astype(F32)
                acc_ref[0:1, sl] += jnp.sum(du2 * x, axis=0, keepdims=True)
                acc_ref[1:2, sl] += jnp.sum(du1 * x, axis=0, keepdims=True)
                acc_ref[2:3, sl] += jnp.sum(du0 * x, axis=0, keepdims=True)
                acc_ref[3:4, sl] += jnp.sum(du0, axis=0, keepdims=True)

        @pl.when(i * tb < PADR)
        def _():
            row = _row_ids(i * tb, tb)
            du_ref[...] = jnp.where(row >= PADR, du_ref[...], jnp.zeros((), du_ref.dtype))

    nxt = lambda i: (jnp.minimum(i * nh + nh, last), 0)
    cur = lambda i: (i, 0)
    fix = lambda i: (0, 0)
    est = 12 * _nbytes((tb + 2 * FH, F2), F32)
    body, in_specs, args = _after(
        dep, body, [pl.BlockSpec((tb, F), cur), pl.BlockSpec((FH, F), nxt),
                    pl.BlockSpec((tb, F2), cur), pl.BlockSpec((tb, F2), cur), pl.BlockSpec((FH, F2), nxt),
                    pl.BlockSpec((SUBLANES, F2), fix)], [da, da, up, u, u, taps])
    return pl.pallas_call(
        body, name=name, grid=(R // tb,), in_specs=in_specs,
        out_specs=[pl.BlockSpec((tb, F2), cur), pl.BlockSpec((SUBLANES, F2), fix)],
        out_shape=[jax.ShapeDtypeStruct((R, F2), MX), jax.ShapeDtypeStruct((SUBLANES, F2), F32)],
        compiler_params=_cp(("arbitrary",), est),
    )(*args)


def _hg_consts():
    t = np.arange(CH)[:, None]
    j = np.arange(CH)[None, :]
    low = (j <= t).astype(np.float32)
    blocks = [low, (j > t).astype(np.float32)]
    for m in LEVELS:
        ref = (t // (2 * m)) * 2 * m + m - 1
        blocks.append(low - (j <= ref).astype(np.float32))
    mat = np.concatenate(blocks, axis=0)
    suf = (j >= t).astype(np.float32)
    return jnp.asarray(mat, BF), jnp.asarray(suf, BF)


def _split_dot(mat, x):
    hi = x.astype(BF)
    lo = (x - hi.astype(F32)).astype(BF)
    return (lax.dot_general(mat, hi, NN, preferred_element_type=F32)
            + lax.dot_general(mat, lo, NN, preferred_element_type=F32))


def _lane_select(a2, a4, a8, a16):
    lane = lax.broadcasted_iota(jnp.int32, (1, PW), 1)
    return jnp.where(lane < 64, a2, jnp.where(lane < 128, a4, jnp.where(lane < 192, a8, a16)))


def _pool_count(row):
    win = _lane_select(2.0, 4.0, 8.0, 16.0)
    t1 = jnp.maximum((row - PADR + 1).astype(F32), 1.0)
    return jnp.minimum(t1, win)


def _gates(fz, lb, valid):
    sig = jax.nn.sigmoid(fz)
    f = lb + (1.0 - lb) * sig
    lf = jnp.where(valid, jnp.log(jnp.maximum(f, F_FLOOR)), 0.0)
    kk = jnp.where(valid, (1.0 - lb) * (1.0 - sig), 0.0)
    return sig, f, lf, kk


def _level_masks():
    tt = lax.broadcasted_iota(jnp.int32, (CH, CH), 0)
    ss = lax.broadcasted_iota(jnp.int32, (CH, CH), 1)
    xr = tt ^ ss
    low = tt > ss
    return tt, ss, [(xr >= m) & (xr < 2 * m) & low for m in LEVELS]


def _intra(q, kk, ex, sl, tt, ss, masks):
    a = jnp.where(tt == ss, jnp.sum(q * kk, axis=-1, keepdims=True), 0.0)
    parts = []
    for l in range(len(LEVELS)):
        e = jnp.exp(-jnp.abs(ex[(2 + l) * CH:(3 + l) * CH, sl]))
        ql = (q * e).astype(MX)
        kl = (kk * e).astype(MX)
        a = a + jnp.where(masks[l], _dot(ql, kl, NT), 0.0)
        parts.append((e, ql, kl))
    return a, parts


def _mixer_fwd(h, taps, wbd, ps, gn, lb, mat, *, name):
    R = h.shape[0]
    nb = R // BM
    q4 = BM // HALO

    def body(h_ref, hp_ref, w_ref, wbd_ref, ps_ref, gn_ref, lb_ref, mat_ref, y_ref, o_ref, sp_ref, st_ref):
        i = pl.program_id(0)

        @pl.when(i == 0)
        def _():
            st_ref[...] = jnp.zeros(st_ref.shape, F32)

        row = _row_ids(i * BM, BM)
        valid_b = row >= PADR
        rowe = _row_ids(i * BM - HALO, BM + HALO)
        valide = rowe >= PADR
        se = jnp.concatenate([hp_ref[:, 256:512] * hp_ref[:, 512:768], h_ref[:, 256:512] * h_ref[:, 512:768]], axis=0)
        se = jnp.where(valide, se, 0.0)
        conv = (w_ref[0:1, :] * pltpu.roll(se, 2, 0) + w_ref[1:2, :] * pltpu.roll(se, 1, 0)
                + w_ref[2:3, :] * se)[HALO:]
        y_ref[:, 0:CW] = (h_ref[:, 0:256] * conv).astype(y_ref.dtype)
        ve = jnp.where(valide, jnp.concatenate([hp_ref[:, 2816:3072], h_ref[:, 2816:3072]], axis=0), 0.0)
        s2 = ve + pltpu.roll(ve, 1, 0)
        s4 = s2 + pltpu.roll(s2, 2, 0)
        s8 = s4 + pltpu.roll(s4, 4, 0)
        s16 = s8 + pltpu.roll(s8, 8, 0)
        dp = (_lane_select(s2, s4, s8, s16) / _pool_count(rowe) - ve)[HALO:]
        y_ref[:, CW + HW:D] = (_dot(dp, wbd_ref[...], NN) * ps_ref[...]).astype(y_ref.dtype)
        lbv = lb_ref[...]
        tt, ss, masks = _level_masks()
        for c in range(MB):
            rs = slice(c * CH, (c + 1) * CH)
            _, _, lf, kk = _gates(h_ref[rs, 1280:1792], lbv, valid_b[rs])
            q = h_ref[rs, 768:1280] * SCALE
            ii = h_ref[rs, 1792:2304]
            gz = h_ref[rs, 2304:2816]
            ex = _split_dot(mat_ref[...], lf)
            eg = jnp.exp(ex[0:CH])
            q0 = q * eg
            kr = kk * jnp.exp(ex[CH:2 * CH])
            for hd in range(NH):
                sl = slice(HD * hd, HD * (hd + 1))
                st = st_ref[hd]
                sp_ref[c, hd] = st
                a, _ = _intra(q[:, sl], kk[:, sl], ex, sl, tt, ss, masks)
                o = _dot(q0[:, sl], st, NT) + _dot(a, ii[:, sl], NN)
                st_ref[hd] = st * eg[CH - 1:CH, sl] + _dot(ii[:, sl], kr[:, sl], TN)
                o_ref[rs, sl] = o
                r = lax.rsqrt(jnp.mean(o * o, axis=-1, keepdims=True) + RMS_EPS)
                g_ = gz[:, sl]
                y_ref[rs, CW + HD * hd:CW + HD * (hd + 1)] = (
                    o * r * gn_ref[...] * (g_ * jax.nn.sigmoid(g_))).astype(y_ref.dtype)

    fix = lambda i: (0, 0)
    return pl.pallas_call(
        body, name=name, grid=(nb,),
        in_specs=[pl.BlockSpec((BM, DIN), lambda i: (i, 0)),
                  pl.BlockSpec((HALO, DIN), lambda i: (jnp.maximum(i * q4 - 1, 0), 0)),
                  pl.BlockSpec((SUBLANES, CW), fix), pl.BlockSpec((PW, PW), fix), pl.BlockSpec((1, PW), fix),
                  pl.BlockSpec((1, HD), fix), pl.BlockSpec((1, HW), fix), pl.BlockSpec(mat.shape, fix)],
        out_specs=[pl.BlockSpec((BM, D), lambda i: (i, 0)), pl.BlockSpec((BM, HW), lambda i: (i, 0)),
                   pl.BlockSpec((MB, NH, HD, HD), lambda i: (i, 0, 0, 0))],
        out_shape=[jax.ShapeDtypeStruct((R, D), MX), jax.ShapeDtypeStruct((R, HW), F32),
                   jax.ShapeDtypeStruct((R // CH, NH, HD, HD), F32)],
        scratch_shapes=[pltpu.VMEM((NH, HD, HD), F32)],
        compiler_params=_cp(("arbitrary",), 32 * 2 ** 20),
    )(h, h, taps, wbd, ps, gn, lb, mat)


def _mixer_bwd(h, dy, o, sp, taps, wbd, ps, gn, lb, mat, suf, *, dep=None, name):
    R = h.shape[0]
    nb = R // BM
    q4 = BM // HALO
    lasth = R // HALO - 1

    def body(h_ref, hp_ref, hn_ref, dy_ref, dyn_ref, o_ref, sp_ref, w_ref, wbd_ref, ps_ref, gn_ref, lb_ref,
             mat_ref, suf_ref, dh_ref, dw_ref, dsc_ref, dgn_ref, dlb_ref, dwbd_ref, dst_ref):
        i = pl.program_id(0)
        b = nb - 1 - i

        @pl.when(i == 0)
        def _():
            dst_ref[...] = jnp.zeros(dst_ref.shape, F32)
            dw_ref[...] = jnp.zeros(dw_ref.shape, F32)
            dsc_ref[...] = jnp.zeros(dsc_ref.shape, F32)
            dgn_ref[...] = jnp.zeros(dgn_ref.shape, F32)
            dlb_ref[...] = jnp.zeros(dlb_ref.shape, F32)
            dwbd_ref[...] = jnp.zeros(dwbd_ref.shape, F32)

        ne = BM + 2 * HALO
        nc = BM + HALO
        row = _row_ids(b * BM, BM)
        valid = row >= PADR
        rowe = _row_ids(b * BM - HALO, ne)
        valide = (rowe >= PADR) & (rowe < R)
        rown = rowe[HALO:]
        validn = rown < R

        def cat3(lo, hi):
            return jnp.concatenate([hp_ref[:, lo:hi], h_ref[:, lo:hi], hn_ref[:, lo:hi]], axis=0)

        def catd(lo, hi):
            return jnp.where(validn, jnp.concatenate([dy_ref[:, lo:hi], dyn_ref[:, lo:hi]], axis=0), 0.0)

        w0, w1, w2 = w_ref[0:1, :], w_ref[1:2, :], w_ref[2:3, :]
        cce = cat3(256, 512)
        cve = cat3(512, 768)
        se = jnp.where(valide, cce * cve, 0.0)
        sm2 = pltpu.roll(se, 2, 0)
        sm1 = pltpu.roll(se, 1, 0)
        conv = (w0 * sm2 + w1 * sm1 + w2 * se)[HALO:HALO + BM]
        cbn = jnp.concatenate([h_ref[:, 0:256], hn_ref[:, 0:256]], axis=0)
        dconv = catd(0, CW) * cbn
        ds = w2 * dconv + w1 * pltpu.roll(dconv, nc - 1, 0) + w0 * pltpu.roll(dconv, nc - 2, 0)
        ds = jnp.where(valid, ds[0:BM], 0.0)
        dcv = dconv[0:BM]
        dw_ref[0:1, :] += jnp.sum(dcv * sm2[HALO:HALO + BM], axis=0, keepdims=True)
        dw_ref[1:2, :] += jnp.sum(dcv * sm1[HALO:HALO + BM], axis=0, keepdims=True)
        dw_ref[2:3, :] += jnp.sum(dcv * se[HALO:HALO + BM], axis=0, keepdims=True)
        dh_ref[:, 0:256] = jnp.where(valid, dy_ref[:, 0:CW] * conv, 0.0).astype(dh_ref.dtype)
        dh_ref[:, 256:512] = (ds * h_ref[:, 512:768]).astype(dh_ref.dtype)
        dh_ref[:, 512:768] = (ds * h_ref[:, 256:512]).astype(dh_ref.dtype)

        ve = jnp.where(valide, cat3(2816, 3072), 0.0)
        s2 = ve + pltpu.roll(ve, 1, 0)
        s4 = s2 + pltpu.roll(s2, 2, 0)
        s8 = s4 + pltpu.roll(s4, 4, 0)
        s16 = s8 + pltpu.roll(s8, 8, 0)
        cnte = _pool_count(rowe)
        dp = (_lane_select(s2, s4, s8, s16) / cnte - ve)[HALO:HALO + BM]
        dyp = catd(CW + HW, D)
        pre = _dot(dp, wbd_ref[...], NN)
        dsc_ref[0:1, :] += jnp.sum(dyp[0:BM] * pre, axis=0, keepdims=True)
        dyps = dyp * ps_ref[...]
        dd = _dot(dyps, wbd_ref[...], NT)
        dwbd_ref[...] += _dot(dp, dyps[0:BM], TN)
        e = dd / cnte[HALO:]
        t2 = e + pltpu.roll(e, nc - 1, 0)
        t4 = t2 + pltpu.roll(t2, nc - 2, 0)
        t8 = t4 + pltpu.roll(t4, nc - 4, 0)
        t16 = t8 + pltpu.roll(t8, nc - 8, 0)
        dv = (_lane_select(t2, t4, t8, t16) - dd)[0:BM]
        dh_ref[:, 2816:3072] = jnp.where(valid, dv, 0.0).astype(dh_ref.dtype)

        lbv = lb_ref[...]
        tt, ss, masks = _level_masks()
        gnv = gn_ref[...]
        for c in reversed(range(MB)):
            rs = slice(c * CH, (c + 1) * CH)
            vc = valid[rs]
            sig, f, lf, kk = _gates(h_ref[rs, 1280:1792], lbv, vc)
            q = h_ref[rs, 768:1280] * SCALE
            ii = h_ref[rs, 1792:2304]
            gz = h_ref[rs, 2304:2816]
            ex = _split_dot(mat_ref[...], lf)
            eg = jnp.exp(ex[0:CH])
            egr = jnp.exp(ex[CH:2 * CH])
            q0 = q * eg
            kr = kk * egr
            dqs, dks, dis, dgzs, tails, dbs = [], [], [], [], [], []
            for hd in range(NH):
                sl = slice(HD * hd, HD * (hd + 1))
                ov = o_ref[rs, sl]
                r = lax.rsqrt(jnp.mean(ov * ov, axis=-1, keepdims=True) + RMS_EPS)
                oh = ov * r
                g_ = gz[:, sl]
                sg = jax.nn.sigmoid(g_)
                dyv = dy_ref[rs, CW + HD * hd:CW + HD * (hd + 1)]
                don = dyv * (g_ * sg)
                dgzs.append(dyv * (oh * gnv) * (sg * (1.0 + g_ * (1.0 - sg))))
                dgn_ref[0:1, :] += jnp.sum(don * oh, axis=0, keepdims=True)
                doh = don * gnv
                do = r * (doh - oh * jnp.mean(doh * oh, axis=-1, keepdims=True))
                st = sp_ref[c, hd]
                dst = dst_ref[hd]
                qh, kh, ih = q[:, sl], kk[:, sl], ii[:, sl]
                a, parts = _intra(qh, kh, ex, sl, tt, ss, masks)
                da = jnp.where(tt >= ss, _dot(do, ih, NT), 0.0)
                dis.append(_dot(a, do, TN) + _dot(kr[:, sl], dst, NT))
                q0h = q0[:, sl].astype(MX)
                krh = kr[:, sl].astype(MX)
                dq0 = _dot(do, st, NN)
                dkr = _dot(ih, dst, NN)
                dq = dq0 * eg[:, sl]
                dk = dkr * egr[:, sl]
                kdk = krh.astype(F32) * dkr
                db = q0h.astype(F32) * dq0 - kdk
                tails.append(jnp.sum(kdk, axis=0, keepdims=True)
                             + eg[CH - 1:CH, sl] * jnp.sum(dst * st, axis=0, keepdims=True))
                dga = jnp.sum(jnp.where(tt == ss, da, 0.0), axis=-1, keepdims=True)
                dq = dq + dga * kh
                dk = dk + dga * qh
                for l in range(len(LEVELS)):
                    e_l, ql, kl = parts[l]
                    dpl = jnp.where(masks[l], da, 0.0).astype(MX)
                    dql = _dot(dpl, kl, NN)
                    dkl = _dot(dpl, ql, TN)
                    dq = dq + dql * e_l
                    dk = dk + dkl * e_l
                    db = db + (ql.astype(F32) * dql - kl.astype(F32) * dkl)
                dst_ref[hd] = dst * eg[CH - 1:CH, sl] + _dot(do, q0h, TN)
                dqs.append(dq)
                dks.append(dk)
                dbs.append(db)
            dq = jnp.concatenate(dqs, axis=1)
            dk = jnp.concatenate(dks, axis=1)
            db = jnp.concatenate(dbs, axis=1)
            dlf = _split_dot(suf_ref[...], db) + jnp.concatenate(tails, axis=1)
            t = jnp.where(vc, dlf * jnp.where(f > F_FLOOR, 1.0 / f, 0.0) - dk, 0.0)
            dlb_ref[0:1, :] += jnp.sum(t * (1.0 - sig), axis=0, keepdims=True)
            dh_ref[rs, 768:1280] = jnp.where(vc, dq * SCALE, 0.0).astype(dh_ref.dtype)
            dh_ref[rs, 1280:1792] = (t * (1.0 - lbv) * (sig * (1.0 - sig))).astype(dh_ref.dtype)
            dh_ref[rs, 1792:2304] = jnp.where(vc, jnp.concatenate(dis, axis=1), 0.0).astype(dh_ref.dtype)
            dh_ref[rs, 2304:2816] = jnp.where(vc, jnp.concatenate(dgzs, axis=1), 0.0).astype(dh_ref.dtype)

    cur = lambda i: (nb - 1 - i, 0)
    prev = lambda i: (jnp.maximum((nb - 1 - i) * q4 - 1, 0), 0)
    nxt = lambda i: (jnp.minimum((nb - 1 - i) * q4 + q4, lasth), 0)
    fix = lambda i: (0, 0)
    body, in_specs, args = _after(
        dep, body,
        [pl.BlockSpec((BM, DIN), cur), pl.BlockSpec((HALO, DIN), prev), pl.BlockSpec((HALO, DIN), nxt),
         pl.BlockSpec((BM, D), cur), pl.BlockSpec((HALO, D), nxt), pl.BlockSpec((BM, HW), cur),
         pl.BlockSpec((MB, NH, HD, HD), lambda i: (nb - 1 - i, 0, 0, 0)),
         pl.BlockSpec((SUBLANES, CW), fix), pl.BlockSpec((PW, PW), fix), pl.BlockSpec((1, PW), fix),
         pl.BlockSpec((1, HD), fix), pl.BlockSpec((1, HW), fix), pl.BlockSpec(mat.shape, fix),
         pl.BlockSpec(suf.shape, fix)],
        [h, h, h, dy, dy, o, sp, taps, wbd, ps, gn, lb, mat, suf])
    return pl.pallas_call(
        body, name=name, grid=(nb,), in_specs=in_specs,
        out_specs=[pl.BlockSpec((BM, DIN), cur), pl.BlockSpec((SUBLANES, CW), fix), pl.BlockSpec((SUBLANES, PW), fix),
                   pl.BlockSpec((SUBLANES, HD), fix), pl.BlockSpec((SUBLANES, HW), fix), pl.BlockSpec((PW, PW), fix)],
        out_shape=[jax.ShapeDtypeStruct((R, DIN), MX), jax.ShapeDtypeStruct((SUBLANES, CW), F32),
                   jax.ShapeDtypeStruct((SUBLANES, PW), F32), jax.ShapeDtypeStruct((SUBLANES, HD), F32),
                   jax.ShapeDtypeStruct((SUBLANES, HW), F32), jax.ShapeDtypeStruct((PW, PW), F32)],
        scratch_shapes=[pltpu.VMEM((NH, HD, HD), F32)],
        compiler_params=_cp(("arbitrary",), 40 * 2 ** 20),
    )(*args)


def _sum_slots(recv, *, name):
    S, L, rows, cols = recv.shape
    tr = _row_tile(rows, S * cols * 4, 6 * 2 ** 20)

    def body(r_ref, o_ref):
        acc = r_ref[0]
        for s in range(1, S):
            acc = acc + r_ref[s]
        o_ref[...] = acc

    return pl.pallas_call(
        body, name=name, grid=(L, rows // tr),
        in_specs=[pl.BlockSpec((S, None, tr, cols), lambda l, i: (0, l, i, 0))],
        out_specs=pl.BlockSpec((None, tr, cols), lambda l, i: (l, i, 0)),
        out_shape=jax.ShapeDtypeStruct((L, rows, cols), F32),
        compiler_params=_cp(("parallel", "parallel"), 4 * S * tr * cols * 4),
    )(recv)


def _sum_own(recv, own, chip, *, name):
    S, rows, cols = recv.shape
    tr = _row_tile(rows, S * cols * 4, 6 * 2 ** 20)

    def body(me_ref, r_ref, o_ref, out_ref):
        me = me_ref[0]
        acc = None
        for s in range(S):
            t = jnp.where(me == s, o_ref[...], r_ref[s])
            acc = t if acc is None else acc + t
        out_ref[...] = acc

    grid_spec = pltpu.PrefetchScalarGridSpec(
        num_scalar_prefetch=1, grid=(rows // tr,),
        in_specs=[pl.BlockSpec((S, tr, cols), lambda i, me: (0, i, 0)),
                  pl.BlockSpec((None, tr, cols), lambda i, me: (me[0], i, 0))],
        out_specs=pl.BlockSpec((tr, cols), lambda i, me: (i, 0)))
    return pl.pallas_call(
        body, name=name, grid_spec=grid_spec, out_shape=jax.ShapeDtypeStruct((rows, cols), F32),
        compiler_params=_cp(("parallel",), 5 * S * tr * cols * 4),
    )(chip.reshape(1).astype(jnp.int32), recv, own)


def _adamw(w, m, v, ga, gb, *, layer, prev, name):
    L, rows, cols = w.shape
    tr = _row_tile(rows, cols * 4, 2 ** 20)
    two = gb is not None
    nin = 5 if two else 4

    def body(*refs):
        w_ref, m_ref, v_ref, a_ref = refs[:4]
        g_ref, d_ref, m2_ref, v2_ref = refs[-4:]
        g = a_ref[...] + refs[4][...] if two else a_ref[...]
        m2 = ADAM_B1 * m_ref[...] + (1.0 - ADAM_B1) * g
        v2 = ADAM_B2 * v_ref[...] + (1.0 - ADAM_B2) * (g * g)
        m_hat = m2 / (1.0 - ADAM_B1 ** ADAM_STEP)
        v_hat = v2 / (1.0 - ADAM_B2 ** ADAM_STEP)
        g_ref[...] = g
        d_ref[...] = -ADAM_LR * (m_hat / (jnp.sqrt(v_hat) + ADAM_EPS) + ADAM_WD * w_ref[...])
        m2_ref[...] = m2
        v2_ref[...] = v2

    spec = pl.BlockSpec((None, tr, cols), lambda i: (layer, i, 0))
    gspec = pl.BlockSpec((tr, cols), lambda i: (i, 0))
    args = [w, m, v, ga] + ([gb] if two else [])
    in_specs = [spec] * 3 + [gspec] * (nin - 3)
    aliases = {}
    if prev is not None:
        args += list(prev)
        in_specs += [ANY] * 4
        aliases = {nin + j: j for j in range(4)}
    sd = jax.ShapeDtypeStruct((L, rows, cols), F32)
    return pl.pallas_call(
        body, name=name, grid=(rows // tr,), in_specs=in_specs, out_specs=[spec] * 4,
        out_shape=[sd] * 4, input_output_aliases=aliases,
        compiler_params=_cp(("parallel",), 24 * tr * cols * 4),
    )(*args)


def _exchange(arrays, *, flips, n_slots, slot_of, scatter, self_copy, dep=None, name):
    n = len(arrays)
    nf = len(flips)
    deps = [] if dep is None else [dep]
    nd = len(deps)
    out_shapes = [jax.ShapeDtypeStruct(a.shape if scatter else (n_slots,) + a.shape, a.dtype) for a in arrays]

    def body(*refs):
        ins, outs = refs[:n], refs[n + nd:2 * n + nd]
        send_sems, recv_sems, loc_sems = refs[2 * n + nd:]
        x, y, c = lax.axis_index("x"), lax.axis_index("y"), lax.axis_index("c")
        me = slot_of(x, y, c)
        peers = [(1 - x if fx else x, 1 - y if fy else y, 1 - c if fc else c) for fx, fy, fc in flips]
        local, remote = [], []
        for a in range(n):
            if self_copy:
                lc = pltpu.make_async_copy(ins[a].at[me] if scatter else ins[a], outs[a].at[me], loc_sems.at[a])
                lc.start()
                local.append(lc)
            for k, p in enumerate(peers):
                src = ins[a].at[slot_of(*p)] if scatter else ins[a]
                cp = pltpu.make_async_remote_copy(
                    src_ref=src, dst_ref=outs[a].at[me], send_sem=send_sems.at[a, k], recv_sem=recv_sems.at[a, k],
                    device_id=p, device_id_type=MESH)
                cp.start()
                remote.append(cp)
        for a in range(n):
            for k, p in enumerate(peers):
                src = ins[a].at[slot_of(*p)] if scatter else ins[a]
                pltpu.make_async_remote_copy(
                    src_ref=src, dst_ref=outs[a].at[slot_of(*p)], send_sem=send_sems.at[a, k],
                    recv_sem=recv_sems.at[a, k], device_id=p, device_id_type=MESH).wait_recv()
        for cp in remote:
            cp.wait_send()
        for lc in local:
            lc.wait()

    return pl.pallas_call(
        body, name=name, in_specs=[ANY] * (n + nd), out_specs=[ANY] * n, out_shape=out_shapes,
        scratch_shapes=[pltpu.SemaphoreType.DMA((n, nf)), pltpu.SemaphoreType.DMA((n, nf)),
                        pltpu.SemaphoreType.DMA((n,))],
        compiler_params=pltpu.CompilerParams(has_side_effects=True),
    )(*arrays, *deps)


ALL_FLIPS = [(fx, fy, fc) for fx in (0, 1) for fy in (0, 1) for fc in (0, 1) if fx or fy or fc]


def _dev_slot(x, y, c):
    return 4 * x + 2 * y + c


def _zero_slot(x, y, c):
    return 0


HBM_SPEC = pl.BlockSpec(memory_space=pltpu.HBM)
SEM_SPEC = pl.BlockSpec(memory_space=pltpu.SEMAPHORE)


def _peers(sibling, x, y, c):
    return [(x, y, 1 - c)] if sibling else [(1 - x, y, c), (x, 1 - y, c), (1 - x, 1 - y, c)]


def _slot(sibling, x, y, c):
    return 0 if sibling else 2 * x + y


def _send_start(srcs, *, scatter, sibling=False, dep=None, name):
    n = len(srcs)
    npeer = 1 if sibling else 3
    nc = n * npeer
    srcs = [pltpu.with_memory_space_constraint(s, pltpu.HBM) for s in srcs]
    land_shapes = [s.shape if scatter else (1 if sibling else 4,) + s.shape for s in srcs]
    lands = [pltpu.with_memory_space_constraint(lax.empty(sh, s.dtype), pltpu.HBM) for sh, s in zip(land_shapes, srcs)]

    deps = [] if dep is None else [dep]
    nd = len(deps)

    def body(*refs):
        ins, lnd = refs[:n], refs[n:2 * n]
        send_sems, recv_sems = refs[2 * n + nd:2 * n + nd + nc], refs[2 * n + nd + nc:2 * n + nd + 2 * nc]
        token = refs[-1]
        x, y, c = lax.axis_index("x"), lax.axis_index("y"), lax.axis_index("c")
        me = _slot(sibling, x, y, c)
        for a in range(n):
            for k, p in enumerate(_peers(sibling, x, y, c)):
                src = ins[a].at[_slot(sibling, *p)] if scatter else ins[a]
                j = a * npeer + k
                pltpu.make_async_remote_copy(
                    src_ref=src, dst_ref=lnd[a].at[me], send_sem=send_sems[j], recv_sem=recv_sems[j],
                    device_id=p, device_id_type=MESH).start()
        token[...] = jnp.zeros(token.shape, token.dtype)

    sem = pltpu.SemaphoreType.DMA(())
    outs = pl.pallas_call(
        body, name=name,
        out_shape=(*[sem] * (2 * nc), *[pltpu.HBM(s.shape, s.dtype) for s in srcs],
                   *[pltpu.HBM(sh, s.dtype) for sh, s in zip(land_shapes, srcs)],
                   jax.ShapeDtypeStruct((SUBLANES, LANES), F32)),
        in_specs=[HBM_SPEC] * (2 * n) + [ANY] * nd,
        out_specs=(*[SEM_SPEC] * (2 * nc), *[HBM_SPEC] * (2 * n), pl.BlockSpec(memory_space=pltpu.VMEM)),
        input_output_aliases={i: 2 * nc + i for i in range(2 * n)},
        compiler_params=pltpu.CompilerParams(has_side_effects=pltpu.SideEffectType.DATAFLOW_SIDE_EFFECTING),
    )(*srcs, *lands, *deps)
    return dict(sems=list(outs[:2 * nc]), srcs=list(outs[2 * nc:2 * nc + n]),
                lands=list(outs[2 * nc + n:2 * nc + 2 * n]), token=outs[-1], sibling=sibling)


def _send_wait(h, *, scatter, after, name):
    n = len(h["srcs"])
    sibling = h["sibling"]
    nc = len(h["sems"]) // 2

    def body(*refs):
        ins, lnd = refs[:n], refs[n:2 * n]
        send_sems, recv_sems = refs[2 * n:2 * n + nc], refs[2 * n + nc:2 * n + 2 * nc]
        x, y, c = lax.axis_index("x"), lax.axis_index("y"), lax.axis_index("c")
        for a in range(n):
            for k, p in enumerate(_peers(sibling, x, y, c)):
                slot = _slot(sibling, *p)
                j = a * (nc // n) + k
                cp = pltpu.make_async_remote_copy(
                    src_ref=ins[a].at[slot] if scatter else ins[a], dst_ref=lnd[a].at[slot],
                    send_sem=send_sems[j], recv_sem=recv_sems[j], device_id=p, device_id_type=MESH)
                cp.wait_send()
                cp.wait_recv()

    thru = h["srcs"] + h["lands"]
    outs = pl.pallas_call(
        body, name=name, out_shape=tuple(pltpu.HBM(t.shape, t.dtype) for t in thru),
        in_specs=[HBM_SPEC] * (2 * n) + [SEM_SPEC] * (2 * nc) + [ANY] * len(after),
        out_specs=tuple([HBM_SPEC] * (2 * n)),
        input_output_aliases={i: i for i in range(2 * n)},
        compiler_params=pltpu.CompilerParams(has_side_effects=pltpu.SideEffectType.DATAFLOW_SIDE_EFFECTING),
    )(*thru, *h["sems"], *after)
    return list(outs[:n]), list(outs[n:])


def _assemble(land, own, chip, axis):
    return jnp.concatenate([jnp.where(chip == k, own, land[k]) for k in range(4)], axis=axis)


def _pack(arrs):
    flat = jnp.concatenate([a.reshape(-1).astype(F32) for a in arrs])
    tile = SUBLANES * LANES
    pad = (-flat.shape[0]) % tile
    return jnp.pad(flat, (0, pad)).reshape(-1, LANES)


def _unpack(buf, shapes):
    flat = buf.reshape(-1)
    out, off = [], 0
    for s in shapes:
        n = int(np.prod(s))
        out.append(flat[off:off + n].reshape(s))
        off += n
    return out


def _lower_bounds(hg_lower_bounds):
    p = jax.nn.softmax(hg_lower_bounds.astype(F32), axis=0)
    return jnp.cumsum(p, axis=0) - p[0]


def kernel(x, meta_tokens, hg_lower_bounds, w_in, w_conv, w_pool, pool_scale, hg_norm_g, w_o, ln1_g, ln1_b, w_up, w_ffn_conv, b_ffn_conv, w_down, ln2_g, ln2_b, loss_target, m_meta_tokens, m_hg_lower_bounds, m_w_in, m_w_conv, m_w_pool, m_pool_scale, m_hg_norm_g, m_w_o, m_ln1_g, m_ln1_b, m_w_up, m_w_ffn_conv, m_b_ffn_conv, m_w_down, m_ln2_g, m_ln2_b, v_meta_tokens, v_hg_lower_bounds, v_w_in, v_w_conv, v_w_pool, v_pool_scale, v_hg_norm_g, v_w_o, v_ln1_g, v_ln1_b, v_w_up, v_w_ffn_conv, v_b_ffn_conv, v_w_down, v_ln2_g, v_ln2_b):
    S = x.shape[1]
    R = S + ROW0
    Fq = w_down.shape[1]
    F = 4 * Fq
    F2 = 2 * F
    F2q = w_up.shape[2]
    assert x.shape == (1, S, D) and R % 384 == 0 and S % ROW0 == 0
    chip = 2 * lax.axis_index("x") + lax.axis_index("y")
    tm = 384
    tm_w = max(t for t in range(SUBLANES, 2113, SUBLANES) if R % t == 0)
    tb_ffn = 128

    small_shapes = [(N_META, D // 4), (DEPTH, CW // 4, 3), (DEPTH, F2q, 3)]
    wb_in, wb_o, wb_up, wb_down = (w.astype(MX) for w in (w_in, w_o, w_up, w_down))
    h_s = _send_start([_pack([meta_tokens, w_conv, w_ffn_conv])], scatter=False, name="gather_small_start")
    h_a = _send_start([wb_in[0]], scatter=False, dep=h_s["token"], name="gather_a_start")
    (own_small,), (l_small,) = _send_wait(h_s, scatter=False, after=[h_a["token"]], name="gather_small_wait")
    sm = [_unpack(jnp.where(chip == k, own_small, l_small[k]), small_shapes) for k in range(4)]
    meta_full = jnp.concatenate([sm[k][0] for k in range(4)], axis=1)
    X = jnp.concatenate([jnp.zeros((PADR, D), F32), meta_full, x[0]], axis=0)
    (own_in,), (l_in,) = _send_wait(h_a, scatter=False, after=[X, wb_o, wb_up, wb_down], name="gather_a_wait")
    h_b = _send_start([wb_o[0], wb_up[0], wb_down[0]], scatter=False, dep=l_in, name="gather_b_start")
    Win, Wo, Wup, Wdown = {}, {}, {}, {}
    Win[0] = _assemble(l_in, own_in, chip, 1)
    wconv_full = jnp.concatenate([sm[k][1] for k in range(4)], axis=1)
    wffn_full = jnp.concatenate([sm[k][2] for k in range(4)], axis=1)
    taps_c = jnp.pad(wconv_full.transpose(0, 2, 1), ((0, 0), (0, SUBLANES - 3), (0, 0)))
    taps_f = jnp.pad(wffn_full.transpose(0, 2, 1), ((0, 0), (0, SUBLANES - 3), (0, 0)))
    wbd = jnp.stack([jax.scipy.linalg.block_diag(*[w_pool[l, g] for g in range(4)]) for l in range(DEPTH)]).astype(MX)
    lbs, lbs_vjp = jax.vjp(_lower_bounds, hg_lower_bounds)
    mat, suf = _hg_consts()

    def fwd_mixer(l, X, dep=None):
        h = _mm(X, Win[l], tm=tm, dep=dep, name=f"mm_in_{l}")
        y, o, sp = _mixer_fwd(h, taps_c[l], wbd[l], pool_scale[l].reshape(1, PW), hg_norm_g[l].reshape(1, HD),
                              lbs[l].reshape(1, HW), mat, name=f"mixer_fwd_{l}")
        return h, y, o, sp

    def fwd_rest(l, X, h, y, o, sp, dep=None):
        x1, xh1, r1 = _mm_ln(y, Wo[l], X, ln1_g[l], ln1_b[l], tm=tm, dep=dep, name=f"mm_o_ln_{l}")
        up, u, a = _mm_ffn_fwd(x1, Wup[l], taps_f[l], b_ffn_conv[l].reshape(1, F2), tm=tm // 2,
                               name=f"mm_up_ffn_{l}")
        x2, xh2, r2 = _mm_ln(a, Wdown[l], x1, ln2_g[l], ln2_b[l], tm=tm, name=f"mm_down_ln_{l}")
        return (X, h, y, o, sp, x1, xh1, r1, up, u, a, xh2, r2), x2

    h, y, o, sp = fwd_mixer(0, X, dep=h_b["token"])
    (own_o, own_up, own_down), (l_o, l_up, l_down) = _send_wait(h_b, scatter=False, after=[y], name="gather_b_wait")
    Wo[0], Wup[0], Wdown[0] = (_assemble(l_o, own_o, chip, 0), _assemble(l_up, own_up, chip, 1),
                               _assemble(l_down, own_down, chip, 0))
    h_c = _send_start([wb_in[1], wb_o[1], wb_up[1], wb_down[1]], scatter=False, dep=l_o, name="gather_c_start")
    saved0, X1 = fwd_rest(0, X, h, y, o, sp, dep=h_c["token"])
    own_c, l_c = _send_wait(h_c, scatter=False, after=[X1], name="gather_c_wait")
    Win[1], Wo[1] = _assemble(l_c[0], own_c[0], chip, 1), _assemble(l_c[1], own_c[1], chip, 0)
    Wup[1], Wdown[1] = _assemble(l_c[2], own_c[2], chip, 1), _assemble(l_c[3], own_c[3], chip, 0)
    saved1, X2 = fwd_rest(1, X1, *fwd_mixer(1, X1))
    saved = [saved0, saved1]

    dxo = X2

    sc = {}

    def scatter(nm, l, g, dep=None):
        sc[nm, l] = _send_start([g], scatter=True, dep=dep, name=f"scatter_{nm}_{l}_start")
        return sc[nm, l]["token"]

    tok = None

    small_g = [None] * DEPTH
    for l in reversed(range(DEPTH)):
        X, h, y, o, sp, x1, xh1, r1, up, u, a, xh2, r2 = saved[l]
        if l == DEPTH - 1:
            dz2, gb2, da, sq = _ln_bwd_mm(dxo, xh2, r2, ln2_g[l], Wdown[l], tm=ROW0, tgt=loss_target[0],
                                          name=f"loss_ln2_bwd_da_{l}")
            loss = lax.psum(0.5 * jnp.sum(sq[0]) / D, ("x", "y", "c"))
        else:
            dz2, gb2, da = _ln_bwd_mm(dxo, xh2, r2, ln2_g[l], Wdown[l], tm=tm, dep=tok, name=f"ln2_bwd_da_{l}")
        tok = scatter("w_down", l, _wgrad(a, dz2, slabs_on_cols=False, tm=tm_w, tn=512, name=f"wgrad_down_{l}"))
        dup, facc = _ffn_bwd(da, up, u, taps_f[l], tb=tb_ffn, dep=tok, name=f"ffn_bwd_{l}")
        dx1 = _mm(dup, Wup[l], nt=True, res=dz2, tm=tm, name=f"mm_dx1_{l}")
        tok = scatter("w_up", l, _wgrad(x1, dup, slabs_on_cols=True, tm=tm_w, tn=F2q, name=f"wgrad_up_{l}"))
        dz1, gb1, dym = _ln_bwd_mm(dx1, xh1, r1, ln1_g[l], Wo[l], tm=tm, dep=tok, name=f"ln1_bwd_dym_{l}")
        tok = scatter("w_o", l, _wgrad(y, dz1, slabs_on_cols=False, tm=tm_w, tn=512, name=f"wgrad_o_{l}"))
        dh, dwc, dsc, dgn, dlb, dwbd = _mixer_bwd(
            h, dym, o, sp, taps_c[l], wbd[l], pool_scale[l].reshape(1, PW), hg_norm_g[l].reshape(1, HD),
            lbs[l].reshape(1, HW), mat, suf, dep=tok, name=f"mixer_bwd_{l}")
        tok = scatter("w_in", l, _wgrad(X, dh, slabs_on_cols=True, tm=tm_w, tn=DIN // 4, name=f"wgrad_in_{l}"))
        if l == 0:
            dx_head, dx_seq = _mm_dx_head(dh, Win[l], dz1, dep=tok, name=f"mm_dx_{l}")
        else:
            dxo = _mm(dh, Win[l], nt=True, res=dz1, tm=tm, dep=tok, name=f"mm_dx_{l}")
        small_g[l] = dict(
            lbs=dlb[0], w_conv=dwc[0:3].T, w_pool=jnp.stack([dwbd[64 * g:64 * g + 64, 64 * g:64 * g + 64] for g in range(4)]),
            pool_scale=dsc[0], hg_norm_g=dgn[0], ln1_g=gb1[0], ln1_b=gb1[1], w_ffn_conv=facc[0:3].T,
            b_ffn_conv=facc[3], ln2_g=gb2[0], ln2_b=gb2[1])
    grad_x = dx_seq[None]

    big_w = dict(w_in=(w_in, m_w_in, v_w_in), w_o=(w_o, m_w_o, v_w_o), w_up=(w_up, m_w_up, v_w_up),
                 w_down=(w_down, m_w_down, v_w_down))
    groups = {1: ("w_down", "w_up", "w_o", "w_in"), 0: ("w_down", "w_up", "w_o")}
    swaps, tokc = {}, None
    for l in reversed(range(DEPTH)):
        part = []
        for nm in groups[l]:
            after = [dx_seq] if tokc is None else [dx_seq, tokc]
            (own,), (recv,) = _send_wait(sc[nm, l], scatter=True, after=after, name=f"scatter_{nm}_{l}_wait")
            part.append(_sum_own(recv, own, chip, name=f"sum_{nm}_{l}"))
        swaps[l] = _send_start(part, scatter=False, sibling=True, name=f"swap_cores_{l}_start")
        tokc = swaps[l]["token"]

    sg_names = ["lbs", "w_conv", "w_pool", "pool_scale", "hg_norm_g", "ln1_g", "ln1_b", "w_ffn_conv",
                "b_ffn_conv", "ln2_g", "ln2_b"]
    sg_list = [dx_head[PADR:ROW0]] + [jnp.stack([small_g[l][nm] for l in range(DEPTH)]) for nm in sg_names]
    sg_shapes = [a.shape for a in sg_list]
    packed = _pack(sg_list)
    (gathered,) = _exchange([packed], flips=ALL_FLIPS, n_slots=8, slot_of=_dev_slot, scatter=False, self_copy=True,
                            dep=tokc, name="gather_small_grads")
    total = _sum_slots(gathered[:, None], name="sum_small_grads")[0]
    tot = dict(zip(["meta_tokens"] + sg_names, _unpack(total, sg_shapes)))
    (g_hg,) = lbs_vjp(tot["lbs"])
    small_grads = dict(
        meta_tokens=lax.dynamic_slice_in_dim(tot["meta_tokens"], chip * (D // 4), D // 4, axis=1),
        hg_lower_bounds=g_hg,
        w_conv=lax.dynamic_slice_in_dim(tot["w_conv"], chip * (CW // 4), CW // 4, axis=1),
        w_pool=tot["w_pool"], pool_scale=tot["pool_scale"], hg_norm_g=tot["hg_norm_g"],
        ln1_g=tot["ln1_g"], ln1_b=tot["ln1_b"],
        w_ffn_conv=lax.dynamic_slice_in_dim(tot["w_ffn_conv"], chip * F2q, F2q, axis=1),
        b_ffn_conv=tot["b_ffn_conv"], ln2_g=tot["ln2_g"], ln2_b=tot["ln2_b"])
    small_w = dict(meta_tokens=(meta_tokens, m_meta_tokens, v_meta_tokens),
                   hg_lower_bounds=(hg_lower_bounds, m_hg_lower_bounds, v_hg_lower_bounds),
                   w_conv=(w_conv, m_w_conv, v_w_conv), w_pool=(w_pool, m_w_pool, v_w_pool),
                   pool_scale=(pool_scale, m_pool_scale, v_pool_scale), hg_norm_g=(hg_norm_g, m_hg_norm_g, v_hg_norm_g),
                   ln1_g=(ln1_g, m_ln1_g, v_ln1_g), ln1_b=(ln1_b, m_ln1_b, v_ln1_b),
                   w_ffn_conv=(w_ffn_conv, m_w_ffn_conv, v_w_ffn_conv), b_ffn_conv=(b_ffn_conv, m_b_ffn_conv, v_b_ffn_conv),
                   ln2_g=(ln2_g, m_ln2_g, v_ln2_g), ln2_b=(ln2_b, m_ln2_b, v_ln2_b))
    names_s = list(small_w)
    shapes_s = [small_w[nm][0].shape for nm in names_s]
    pk = [_pack([small_w[nm][j] for nm in names_s])[None] for j in range(3)]
    pg = _pack([small_grads[nm] for nm in names_s])[None]
    outs_s = _adamw(pk[0], pk[1], pk[2], pg[0], None, layer=0, prev=None, name="adamw_small")
    small = {nm: [] for nm in names_s}
    for j in range(4):
        for nm, val in zip(names_s, _unpack(outs_s[j][0], shapes_s)):
            small[nm].append(val)

    big = {nm: None for nm in big_w}
    after = [outs_s[0]]
    for l in reversed(range(DEPTH)):
        mine, theirs = _send_wait(swaps[l], scatter=False, after=after, name=f"swap_cores_{l}_wait")
        for k, nm in enumerate(groups[l]):
            w, m, v = big_w[nm]
            big[nm] = _adamw(w, m, v, mine[k], theirs[k][0], layer=l, prev=big[nm], name=f"adamw_{nm}_{l}")
        after = [big[nm][0] for nm in groups[l]]
    (own,), (recv,) = _send_wait(sc["w_in", 0], scatter=True, after=after, name="scatter_w_in_0_wait")
    part = _sum_own(recv, own, chip, name="sum_w_in_0")
    (sib,) = _exchange([part], flips=[(0, 0, 1)], n_slots=1, slot_of=_zero_slot, scatter=False, self_copy=False,
                       name="swap_cores_0_in")
    big["w_in"] = _adamw(*big_w["w_in"], part, sib[0], layer=0, prev=big["w_in"], name="adamw_w_in_0")

    order = ["meta_tokens", "hg_lower_bounds", "w_in", "w_conv", "w_pool", "pool_scale", "hg_norm_g", "w_o",
             "ln1_g", "ln1_b", "w_up", "w_ffn_conv", "b_ffn_conv", "w_down", "ln2_g", "ln2_b"]
    res = {nm: (big[nm] if nm in big else small[nm]) for nm in order}
    outs = [loss, grad_x]
    for j in range(4):
        outs += [res[nm][j] for nm in order]
    return tuple(outs)
```

```python
import numpy as np

import jax
import jax.numpy as jnp
from jax import lax
from jax.experimental import pallas as pl
from jax.experimental.pallas import tpu as pltpu

F32 = jnp.float32
BF = jnp.bfloat16
MX = jnp.bfloat16

D = 1024
CW = 256
HW = 512
HD = 128
NH = 4
PW = 256
DIN = 3072
N_META = 16
CH = 64
MB = 4
BM = MB * CH
ROW0 = 256
PADR = ROW0 - N_META
HALO = 16
FH = 8
LEVELS = (32, 16, 8, 4, 2, 1)
DEPTH = 2
ALPHA = (2 * DEPTH) ** 0.25
LN_EPS = 1e-5
RMS_EPS = 1e-6
F_FLOOR = 1e-30
SCALE = HD ** -0.5
ADAM_LR, ADAM_B1, ADAM_B2, ADAM_EPS, ADAM_WD, ADAM_STEP = 0.001, 0.9, 0.999, 1e-08, 0.01, 10

VMEM_V7X = 64 * 2 ** 20
LANES = 128
SUBLANES = 8

NN = (((1,), (0,)), ((), ()))
NT = (((1,), (1,)), ((), ()))
TN = (((0,), (0,)), ((), ()))
MESH = pl.DeviceIdType.MESH
ANY = pl.BlockSpec(memory_space=pl.ANY)


def _dot(a, b, dn):
    return lax.dot_general(a.astype(MX), b.astype(MX), dn, preferred_element_type=F32)


def _cp(sem, est_bytes):
    lim = int(min(VMEM_V7X - 6 * 2 ** 20, max(32 * 2 ** 20, est_bytes)))
    return pltpu.CompilerParams(dimension_semantics=sem, vmem_limit_bytes=lim)


def _nbytes(shape, dtype):
    return int(np.prod(shape)) * jnp.dtype(dtype).itemsize


def _row_tile(rows, row_bytes, budget):
    best = SUBLANES
    for t in range(SUBLANES, rows + 1, SUBLANES):
        if rows % t == 0 and t * row_bytes <= budget:
            best = t
    return best


def _after(dep, body, in_specs, args):
    if dep is None:
        return body, list(in_specs), list(args)

    def body_after(dep_ref, *refs):
        body(*refs)

    return body_after, [ANY] + list(in_specs), [dep] + list(args)


def _mm(a, w, *, nt=False, res=None, tm, out_dtype=F32, zero_inert=False, dep=None, name):
    R, K = a.shape
    N = w.shape[0] if nt else w.shape[1]
    dn = NT if nt else NN

    def body(*refs):
        if res is None:
            a_ref, w_ref, o_ref = refs
        else:
            a_ref, w_ref, r_ref, o_ref = refs
        acc = _dot(a_ref[...], w_ref[...], dn)
        if res is not None:
            acc = acc + ALPHA * r_ref[...]
        if zero_inert:
            acc = jnp.where(_row_ids(pl.program_id(0) * tm, tm) >= PADR, acc, 0.0)
        o_ref[...] = acc.astype(out_dtype)

    in_specs = [pl.BlockSpec((tm, K), lambda i: (i, 0)), pl.BlockSpec(w.shape, lambda i: (0, 0))]
    args = [a, w]
    est = 2 * _nbytes((tm, K), a.dtype) + 2 * _nbytes(w.shape, w.dtype) + 3 * _nbytes((tm, N), F32)
    if res is not None:
        in_specs.append(pl.BlockSpec((tm, N), lambda i: (i, 0)))
        args.append(res)
        est += 2 * _nbytes((tm, N), F32)
    body, in_specs, args = _after(dep, body, in_specs, args)
    return pl.pallas_call(
        body, name=name, grid=(R // tm,), in_specs=in_specs,
        out_specs=pl.BlockSpec((tm, N), lambda i: (i, 0)),
        out_shape=jax.ShapeDtypeStruct((R, N), out_dtype),
        compiler_params=_cp(("parallel",), est + 4 * 2 ** 20),
    )(*args)


def _wgrad(a, b, *, slabs_on_cols, tm, tn, name):
    R, Ka = a.shape
    Nb = b.shape[1]
    if slabs_on_cols:
        out_shape = (4, Ka, Nb // 4)
        assert tn == Nb // 4
        out_spec = pl.BlockSpec((None, Ka, tn), lambda j, i: (j, 0, 0))
    else:
        out_shape = (4, Ka // 4, Nb)
        out_spec = pl.BlockSpec((4, Ka // 4, tn), lambda j, i: (0, 0, j))

    def body(a_ref, b_ref, o_ref):
        @pl.when(pl.program_id(1) == 0)
        def _():
            o_ref[...] = jnp.zeros(o_ref.shape, F32)

        acc = _dot(a_ref[...], b_ref[...], TN)
        o_ref[...] += acc.reshape(o_ref.shape)

    in_specs = [pl.BlockSpec((tm, Ka), lambda j, i: (i, 0)), pl.BlockSpec((tm, tn), lambda j, i: (i, j))]
    est = 2 * _nbytes((tm, Ka), a.dtype) + 2 * _nbytes((tm, tn), b.dtype) + 4 * _nbytes((Ka, tn), F32) \
        + _nbytes((tm, Ka), F32)
    return pl.pallas_call(
        body, name=name, grid=(Nb // tn, R // tm), in_specs=in_specs, out_specs=out_spec,
        out_shape=jax.ShapeDtypeStruct(out_shape, F32),
        compiler_params=_cp(("parallel", "arbitrary"), est + 4 * 2 ** 20),
    )(a, b)


def _mm_ln(a, w, xres, g, b, *, tm, dep=None, name):
    R, K = a.shape

    def body(a_ref, w_ref, x_ref, g_ref, b_ref, xo_ref, xh_ref, r_ref):
        z = ALPHA * x_ref[...] + _dot(a_ref[...], w_ref[...], NN)
        mu = jnp.mean(z, axis=-1, keepdims=True)
        zc = z - mu
        var = jnp.mean(zc * zc, axis=-1, keepdims=True)
        r = lax.rsqrt(var + LN_EPS)
        xh = zc * r
        xh_ref[...] = xh
        r_ref[...] = r
        xo_ref[...] = xh * g_ref[...] + b_ref[...]

    row = lambda i: (i, 0)
    fix = lambda i: (0, 0)
    est = 2 * _nbytes((tm, K), a.dtype) + 2 * _nbytes(w.shape, w.dtype) + 10 * _nbytes((tm, D), F32)
    body, in_specs, args = _after(
        dep, body, [pl.BlockSpec((tm, K), row), pl.BlockSpec(w.shape, fix), pl.BlockSpec((tm, D), row),
                    pl.BlockSpec((1, D), fix), pl.BlockSpec((1, D), fix)],
        [a, w, xres, g.reshape(1, D), b.reshape(1, D)])
    return pl.pallas_call(
        body, name=name, grid=(R // tm,), in_specs=in_specs,
        out_specs=[pl.BlockSpec((tm, D), row), pl.BlockSpec((tm, D), row), pl.BlockSpec((tm, 1), row)],
        out_shape=[jax.ShapeDtypeStruct((R, D), F32), jax.ShapeDtypeStruct((R, D), F32),
                   jax.ShapeDtypeStruct((R, 1), F32)],
        compiler_params=_cp(("parallel",), est + 4 * 2 ** 20),
    )(*args)


def _resident(shape):
    return pl.BlockSpec(shape, lambda i: (0,) * len(shape), pipeline_mode=pl.Buffered(1))


def _ln_bwd_mm(dx, xh, r, g, w, *, tm, tgt=None, dep=None, name):
    R = dx.shape[0]
    N = w.shape[0]
    assert tgt is None or tm == ROW0

    def body(*refs):
        if tgt is None:
            dx_ref, xh_ref, r_ref, g_ref, w_ref, dz_ref, gb_ref, o_ref = refs
        else:
            dx_ref, t_ref, xh_ref, r_ref, g_ref, w_ref, dz_ref, gb_ref, o_ref, sq_ref = refs
        i = pl.program_id(0)

        @pl.when(i == 0)
        def _():
            gb_ref[...] = jnp.zeros(gb_ref.shape, F32)
            if tgt is not None:
                sq_ref[...] = jnp.zeros(sq_ref.shape, F32)

        if tgt is None:
            dxv = dx_ref[...]
        else:
            err = jnp.where(i > 0, dx_ref[...] - t_ref[...], 0.0)
            sq_ref[0:1, :] += jnp.sum(err * err, axis=0, keepdims=True)
            dxv = err / D
        xhv = xh_ref[...]
        dyh = dxv * g_ref[...]
        m1 = jnp.mean(dyh, axis=-1, keepdims=True)
        m2 = jnp.mean(dyh * xhv, axis=-1, keepdims=True)
        dz = r_ref[...] * (dyh - m1 - xhv * m2)
        dz_ref[...] = dz
        gb_ref[0:1, :] += jnp.sum(dxv * xhv, axis=0, keepdims=True)
        gb_ref[1:2, :] += jnp.sum(dxv, axis=0, keepdims=True)
        o_ref[...] = _dot(dz, w_ref[...], NT)

    row = lambda i: (i, 0)
    fix = lambda i: (0, 0)
    in_specs = [pl.BlockSpec((tm, D), row), pl.BlockSpec((tm, D), row), pl.BlockSpec((tm, 1), row),
                pl.BlockSpec((1, D), fix), _resident(w.shape)]
    args = [dx, xh, r, g.reshape(1, D), w]
    out_specs = [pl.BlockSpec((tm, D), row), pl.BlockSpec((SUBLANES, D), fix), pl.BlockSpec((tm, N), row)]
    out_shape = [jax.ShapeDtypeStruct((R, D), F32), jax.ShapeDtypeStruct((SUBLANES, D), F32),
                 jax.ShapeDtypeStruct((R, N), F32)]
    if tgt is not None:
        in_specs.insert(1, pl.BlockSpec((tm, D), lambda i: (jnp.maximum(i - 1, 0), 0)))
        args.insert(1, tgt)
        out_specs.append(pl.BlockSpec((SUBLANES, D), fix))
        out_shape.append(jax.ShapeDtypeStruct((SUBLANES, D), F32))
    body, in_specs, args = _after(dep, body, in_specs, args)
    est = _nbytes(w.shape, w.dtype) + 16 * _nbytes((tm, D), F32) + 4 * _nbytes((tm, N), F32)
    return pl.pallas_call(
        body, name=name, grid=(R // tm,), in_specs=in_specs, out_specs=out_specs, out_shape=out_shape,
        compiler_params=_cp(("arbitrary",), est + 4 * 2 ** 20),
    )(*args)


def _mm_dx_head(dh, w, res, *, dep=None, name):
    R, K = dh.shape
    tm = ROW0

    def body(a_ref, w_ref, r_ref, head_ref, seq_ref):
        i = pl.program_id(0)
        acc = _dot(a_ref[...], w_ref[...], NT) + ALPHA * r_ref[...]

        @pl.when(i == 0)
        def _():
            head_ref[...] = acc

        @pl.when(i > 0)
        def _():
            seq_ref[...] = acc

    row = lambda i: (i, 0)
    body, in_specs, args = _after(
        dep, body, [pl.BlockSpec((tm, K), row), _resident(w.shape), pl.BlockSpec((tm, D), row)], [dh, w, res])
    est = _nbytes(w.shape, w.dtype) + 2 * _nbytes((tm, K), dh.dtype) + 8 * _nbytes((tm, D), F32)
    return pl.pallas_call(
        body, name=name, grid=(R // tm,), in_specs=in_specs,
        out_specs=[pl.BlockSpec((tm, D), lambda i: (0, 0)), pl.BlockSpec((tm, D), lambda i: (jnp.maximum(i - 1, 0), 0))],
        out_shape=[jax.ShapeDtypeStruct((ROW0, D), F32), jax.ShapeDtypeStruct((R - ROW0, D), F32)],
        compiler_params=_cp(("arbitrary",), est + 4 * 2 ** 20),
    )(*args)


def _mm_ffn_fwd(x1, w, taps, bias, *, tm, name):
    R, K = x1.shape
    F2 = w.shape[1]
    F = F2 // 2

    def body(x_ref, w_ref, t_ref, b_ref, up_ref, u_ref, a_ref, carry_ref):
        i = pl.program_id(0)

        @pl.when(i == 0)
        def _():
            carry_ref[...] = jnp.zeros(carry_ref.shape, F32)

        acc = _dot(x_ref[...], w_ref[...], NN)
        acc = jnp.where(_row_ids(i * tm, tm) >= PADR, acc, 0.0)
        up_ref[...] = acc.astype(up_ref.dtype)
        ue = jnp.concatenate([carry_ref[...], acc], axis=0)
        carry_ref[...] = acc[tm - FH:tm]
        u = (t_ref[0:1, :] * pltpu.roll(ue, 2, 0)[FH:] + t_ref[1:2, :] * pltpu.roll(ue, 1, 0)[FH:]
             + t_ref[2:3, :] * acc + b_ref[...])
        u_ref[...] = u
        gate = u[:, :F]
        val = u[:, F:]
        a_ref[...] = (gate * jax.nn.sigmoid(gate) * val).astype(a_ref.dtype)

    row = lambda i: (i, 0)
    est = _nbytes(w.shape, w.dtype) + 2 * _nbytes((tm, K), x1.dtype) + 10 * _nbytes((tm + FH, F2), F32)
    return pl.pallas_call(
        body, name=name, grid=(R // tm,),
        in_specs=[pl.BlockSpec((tm, K), row), _resident(w.shape), _resident((SUBLANES, F2)), _resident((1, F2))],
        out_specs=[pl.BlockSpec((tm, F2), row), pl.BlockSpec((tm, F2), row), pl.BlockSpec((tm, F), row)],
        out_shape=[jax.ShapeDtypeStruct((R, F2), MX), jax.ShapeDtypeStruct((R, F2), F32),
                   jax.ShapeDtypeStruct((R, F), MX)],
        scratch_shapes=[pltpu.VMEM((FH, F2), F32)],
        compiler_params=_cp(("arbitrary",), est + 4 * 2 ** 20),
    )(x1, w, taps, bias)


def _row_ids(start, n):
    return start + lax.broadcasted_iota(jnp.int32, (n, 1), 0)


def _ffn_bwd(da, up, u, taps, *, tb, dep=None, name):
    R, F2 = up.shape
    F = F2 // 2
    nh = tb // FH
    nb = R // tb
    last = R // FH - 1
    m = tb + FH

    def body(da_ref, dan_ref, x_ref, u_ref, un_ref, w_ref, du_ref, acc_ref):
        i = pl.program_id(0)

        @pl.when(i == 0)
        def _():
            acc_ref[...] = jnp.zeros(acc_ref.shape, F32)

        inside = (i < nb - 1).astype(F32)
        for j in range(F // LANES):
            gsl = slice(j * LANES, (j + 1) * LANES)
            vsl = slice(F + j * LANES, F + (j + 1) * LANES)
            gate = jnp.concatenate([u_ref[:, gsl], un_ref[:, gsl]], axis=0)
            val = jnp.concatenate([u_ref[:, vsl], un_ref[:, vsl]], axis=0)
            dae = jnp.concatenate([da_ref[:, gsl], dan_ref[:, gsl] * inside], axis=0)
            sg = jax.nn.sigmoid(gate)
            gs = gate * sg
            for sl, du in ((gsl, dae * val * (sg * (1.0 + gate - gs))), (vsl, dae * gs)):
                du0 = du[0:tb]
                du1 = pltpu.roll(du, m - 1, 0)[0:tb]
                du2 = pltpu.roll(du, m - 2, 0)[0:tb]
                du_ref[:, sl] = (w_ref[2:3, sl] * du0 + w_ref[1:2, sl] * du1 + w_ref[0:1, sl] * du2).astype(du_ref.dtype)
                x = x_ref[:, sl].astype(F32)
                acc_ref[0:1, sl] += jnp.sum(du2 * x, axis=0, keepdims=True)
                acc_ref[1:2, sl] += jnp.sum(du1 * x, axis=0, keepdims=True)
                acc_ref[2:3, sl] += jnp.sum(du0 * x, axis=0, keepdims=True)
                acc_ref[3:4, sl] += jnp.sum(du0, axis=0, keepdims=True)

        @pl.when(i * tb < PADR)
        def _():
            row = _row_ids(i * tb, tb)
            du_ref[...] = jnp.where(row >= PADR, du_ref[...], jnp.zeros((), du_ref.dtype))

    nxt = lambda i: (jnp.minimum(i * nh + nh, last), 0)
    cur = lambda i: (i, 0)
    fix = lambda i: (0, 0)
    est = 12 * _nbytes((tb + 2 * FH, F2), F32)
    body, in_specs, args = _after(
        dep, body, [pl.BlockSpec((tb, F), cur), pl.BlockSpec((FH, F), nxt),
                    pl.BlockSpec((tb, F2), cur), pl.BlockSpec((tb, F2), cur), pl.BlockSpec((FH, F2), nxt),
                    pl.BlockSpec((SUBLANES, F2), fix)], [da, da, up, u, u, taps])
    return pl.pallas_call(
        body, name=name, grid=(R // tb,), in_specs=in_specs,
        out_specs=[pl.BlockSpec((tb, F2), cur), pl.BlockSpec((SUBLANES, F2), fix)],
        out_shape=[jax.ShapeDtypeStruct((R, F2), MX), jax.ShapeDtypeStruct((SUBLANES, F2), F32)],
        compiler_params=_cp(("arbitrary",), est),
    )(*args)


def _hg_consts():
    t = np.arange(CH)[:, None]
    j = np.arange(CH)[None, :]
    low = (j <= t).astype(np.float32)
    blocks = [low, (j > t).astype(np.float32)]
    for m in LEVELS:
        ref = (t // (2 * m)) * 2 * m + m - 1
        blocks.append(low - (j <= ref).astype(np.float32))
    mat = np.concatenate(blocks, axis=0)
    suf = (j >= t).astype(np.float32)
    return jnp.asarray(mat, BF), jnp.asarray(suf, BF)


def _split_dot(mat, x):
    hi = x.astype(BF)
    lo = (x - hi.astype(F32)).astype(BF)
    return (lax.dot_general(mat, hi, NN, preferred_element_type=F32)
            + lax.dot_general(mat, lo, NN, preferred_element_type=F32))


def _lane_select(a2, a4, a8, a16):
    lane = lax.broadcasted_iota(jnp.int32, (1, PW), 1)
    return jnp.where(lane < 64, a2, jnp.where(lane < 128, a4, jnp.where(lane < 192, a8, a16)))


def _pool_count(row):
    win = _lane_select(2.0, 4.0, 8.0, 16.0)
    t1 = jnp.maximum((row - PADR + 1).astype(F32), 1.0)
    return jnp.minimum(t1, win)


def _gates(fz, lb, valid):
    sig = jax.nn.sigmoid(fz)
    f = lb + (1.0 - lb) * sig
    lf = jnp.where(valid, jnp.log(jnp.maximum(f, F_FLOOR)), 0.0)
    kk = jnp.where(valid, (1.0 - lb) * (1.0 - sig), 0.0)
    return sig, f, lf, kk


def _level_masks():
    tt = lax.broadcasted_iota(jnp.int32, (CH, CH), 0)
    ss = lax.broadcasted_iota(jnp.int32, (CH, CH), 1)
    xr = tt ^ ss
    low = tt > ss
    return tt, ss, [(xr >= m) & (xr < 2 * m) & low for m in LEVELS]


def _intra(q, kk, ex, sl, tt, ss, masks):
    a = jnp.where(tt == ss, jnp.sum(q * kk, axis=-1, keepdims=True), 0.0)
    parts = []
    for l in range(len(LEVELS)):
        e = jnp.exp(-jnp.abs(ex[(2 + l) * CH:(3 + l) * CH, sl]))
        ql = (q * e).astype(MX)
        kl = (kk * e).astype(MX)
        a = a + jnp.where(masks[l], _dot(ql, kl, NT), 0.0)
        parts.append((e, ql, kl))
    return a, parts


def _mixer_fwd(h, taps, wbd, ps, gn, lb, mat, *, name):
    R = h.shape[0]
    nb = R // BM
    q4 = BM // HALO

    def body(h_ref, hp_ref, w_ref, wbd_ref, ps_ref, gn_ref, lb_ref, mat_ref, y_ref, o_ref, sp_ref, st_ref):
        i = pl.program_id(0)

        @pl.when(i == 0)
        def _():
            st_ref[...] = jnp.zeros(st_ref.shape, F32)

        row = _row_ids(i * BM, BM)
        valid_b = row >= PADR
        rowe = _row_ids(i * BM - HALO, BM + HALO)
        valide = rowe >= PADR
        se = jnp.concatenate([hp_ref[:, 256:512] * hp_ref[:, 512:768], h_ref[:, 256:512] * h_ref[:, 512:768]], axis=0)
        se = jnp.where(valide, se, 0.0)
        conv = (w_ref[0:1, :] * pltpu.roll(se, 2, 0) + w_ref[1:2, :] * pltpu.roll(se, 1, 0)
                + w_ref[2:3, :] * se)[HALO:]
        y_ref[:, 0:CW] = (h_ref[:, 0:256] * conv).astype(y_ref.dtype)
        ve = jnp.where(valide, jnp.concatenate([hp_ref[:, 2816:3072], h_ref[:, 2816:3072]], axis=0), 0.0)
        s2 = ve + pltpu.roll(ve, 1, 0)
        s4 = s2 + pltpu.roll(s2, 2, 0)
        s8 = s4 + pltpu.roll(s4, 4, 0)
        s16 = s8 + pltpu.roll(s8, 8, 0)
        dp = (_lane_select(s2, s4, s8, s16) / _pool_count(rowe) - ve)[HALO:]
        y_ref[:, CW + HW:D] = (_dot(dp, wbd_ref[...], NN) * ps_ref[...]).astype(y_ref.dtype)
        lbv = lb_ref[...]
        tt, ss, masks = _level_masks()
        for c in range(MB):
            rs = slice(c * CH, (c + 1) * CH)
            _, _, lf, kk = _gates(h_ref[rs, 1280:1792], lbv, valid_b[rs])
            q = h_ref[rs, 768:1280] * SCALE
            ii = h_ref[rs, 1792:2304]
            gz = h_ref[rs, 2304:2816]
            ex = _split_dot(mat_ref[...], lf)
            eg = jnp.exp(ex[0:CH])
            q0 = q * eg
            kr = kk * jnp.exp(ex[CH:2 * CH])
            for hd in range(NH):
                sl = slice(HD * hd, HD * (hd + 1))
                st = st_ref[hd]
                sp_ref[c, hd] = st
                a, _ = _intra(q[:, sl], kk[:, sl], ex, sl, tt, ss, masks)
                o = _dot(q0[:, sl], st, NT) + _dot(a, ii[:, sl], NN)
                st_ref[hd] = st * eg[CH - 1:CH, sl] + _dot(ii[:, sl], kr[:, sl], TN)
                o_ref[rs, sl] = o
                r = lax.rsqrt(jnp.mean(o * o, axis=-1, keepdims=True) + RMS_EPS)
                g_ = gz[:, sl]
                y_ref[rs, CW + HD * hd:CW + HD * (hd + 1)] = (
                    o * r * gn_ref[...] * (g_ * jax.nn.sigmoid(g_))).astype(y_ref.dtype)

    fix = lambda i: (0, 0)
    return pl.pallas_call(
        body, name=name, grid=(nb,),
        in_specs=[pl.BlockSpec((BM, DIN), lambda i: (i, 0)),
                  pl.BlockSpec((HALO, DIN), lambda i: (jnp.maximum(i * q4 - 1, 0), 0)),
                  pl.BlockSpec((SUBLANES, CW), fix), pl.BlockSpec((PW, PW), fix), pl.BlockSpec((1, PW), fix),
                  pl.BlockSpec((1, HD), fix), pl.BlockSpec((1, HW), fix), pl.BlockSpec(mat.shape, fix)],
        out_specs=[pl.BlockSpec((BM, D), lambda i: (i, 0)), pl.BlockSpec((BM, HW), lambda i: (i, 0)),
                   pl.BlockSpec((MB, NH, HD, HD), lambda i: (i, 0, 0, 0))],
        out_shape=[jax.ShapeDtypeStruct((R, D), MX), jax.ShapeDtypeStruct((R, HW), F32),
                   jax.ShapeDtypeStruct((R // CH, NH, HD, HD), F32)],
        scratch_shapes=[pltpu.VMEM((NH, HD, HD), F32)],
        compiler_params=_cp(("arbitrary",), 32 * 2 ** 20),
    )(h, h, taps, wbd, ps, gn, lb, mat)


def _mixer_bwd(h, dy, o, sp, taps, wbd, ps, gn, lb, mat, suf, *, dep=None, name):
    R = h.shape[0]
    nb = R // BM
    q4 = BM // HALO
    lasth = R // HALO - 1

    def body(h_ref, hp_ref, hn_ref, dy_ref, dyn_ref, o_ref, sp_ref, w_ref, wbd_ref, ps_ref, gn_ref, lb_ref,
             mat_ref, suf_ref, dh_ref, dw_ref, dsc_ref, dgn_ref, dlb_ref, dwbd_ref, dst_ref):
        i = pl.program_id(0)
        b = nb - 1 - i

        @pl.when(i == 0)
        def _():
            dst_ref[...] = jnp.zeros(dst_ref.shape, F32)
            dw_ref[...] = jnp.zeros(dw_ref.shape, F32)
            dsc_ref[...] = jnp.zeros(dsc_ref.shape, F32)
            dgn_ref[...] = jnp.zeros(dgn_ref.shape, F32)
            dlb_ref[...] = jnp.zeros(dlb_ref.shape, F32)
            dwbd_ref[...] = jnp.zeros(dwbd_ref.shape, F32)

        ne = BM + 2 * HALO
        nc = BM + HALO
        row = _row_ids(b * BM, BM)
        valid = row >= PADR
        rowe = _row_ids(b * BM - HALO, ne)
        valide = (rowe >= PADR) & (rowe < R)
        rown = rowe[HALO:]
        validn = rown < R

        def cat3(lo, hi):
            return jnp.concatenate([hp_ref[:, lo:hi], h_ref[:, lo:hi], hn_ref[:, lo:hi]], axis=0)

        def catd(lo, hi):
            return jnp.where(validn, jnp.concatenate([dy_ref[:, lo:hi], dyn_ref[:, lo:hi]], axis=0), 0.0)

        w0, w1, w2 = w_ref[0:1, :], w_ref[1:2, :], w_ref[2:3, :]
        cce = cat3(256, 512)
        cve = cat3(512, 768)
        se = jnp.where(valide, cce * cve, 0.0)
        sm2 = pltpu.roll(se, 2, 0)
        sm1 = pltpu.roll(se, 1, 0)
        conv = (w0 * sm2 + w1 * sm1 + w2 * se)[HALO:HALO + BM]
        cbn = jnp.concatenate([h_ref[:, 0:256], hn_ref[:, 0:256]], axis=0)
        dconv = catd(0, CW) * cbn
        ds = w2 * dconv + w1 * pltpu.roll(dconv, nc - 1, 0) + w0 * pltpu.roll(dconv, nc - 2, 0)
        ds = jnp.where(valid, ds[0:BM], 0.0)
        dcv = dconv[0:BM]
        dw_ref[0:1, :] += jnp.sum(dcv * sm2[HALO:HALO + BM], axis=0, keepdims=True)
        dw_ref[1:2, :] += jnp.sum(dcv * sm1[HALO:HALO + BM], axis=0, keepdims=True)
        dw_ref[2:3, :] += jnp.sum(dcv * se[HALO:HALO + BM], axis=0, keepdims=True)
        dh_ref[:, 0:256] = jnp.where(valid, dy_ref[:, 0:CW] * conv, 0.0).astype(dh_ref.dtype)
        dh_ref[:, 256:512] = (ds * h_ref[:, 512:768]).astype(dh_ref.dtype)
        dh_ref[:, 512:768] = (ds * h_ref[:, 256:512]).astype(dh_ref.dtype)

        ve = jnp.where(valide, cat3(2816, 3072), 0.0)
        s2 = ve + pltpu.roll(ve, 1, 0)
        s4 = s2 + pltpu.roll(s2, 2, 0)
        s8 = s4 + pltpu.roll(s4, 4, 0)
        s16 = s8 + pltpu.roll(s8, 8, 0)
        cnte = _pool_count(rowe)
        dp = (_lane_select(s2, s4, s8, s16) / cnte - ve)[HALO:HALO + BM]
        dyp = catd(CW + HW, D)
        pre = _dot(dp, wbd_ref[...], NN)
        dsc_ref[0:1, :] += jnp.sum(dyp[0:BM] * pre, axis=0, keepdims=True)
        dyps = dyp * ps_ref[...]
        dd = _dot(dyps, wbd_ref[...], NT)
        dwbd_ref[...] += _dot(dp, dyps[0:BM], TN)
        e = dd / cnte[HALO:]
        t2 = e + pltpu.roll(e, nc - 1, 0)
        t4 = t2 + pltpu.roll(t2, nc - 2, 0)
        t8 = t4 + pltpu.roll(t4, nc - 4, 0)
        t16 = t8 + pltpu.roll(t8, nc - 8, 0)
        dv = (_lane_select(t2, t4, t8, t16) - dd)[0:BM]
        dh_ref[:, 2816:3072] = jnp.where(valid, dv, 0.0).astype(dh_ref.dtype)

        lbv = lb_ref[...]
        tt, ss, masks = _level_masks()
        gnv = gn_ref[...]
        for c in reversed(range(MB)):
            rs = slice(c * CH, (c + 1) * CH)
            vc = valid[rs]
            sig, f, lf, kk = _gates(h_ref[rs, 1280:1792], lbv, vc)
            q = h_ref[rs, 768:1280] * SCALE
            ii = h_ref[rs, 1792:2304]
            gz = h_ref[rs, 2304:2816]
            ex = _split_dot(mat_ref[...], lf)
            eg = jnp.exp(ex[0:CH])
            egr = jnp.exp(ex[CH:2 * CH])
            q0 = q * eg
            kr = kk * egr
            dqs, dks, dis, dgzs, tails, dbs = [], [], [], [], [], []
            for hd in range(NH):
                sl = slice(HD * hd, HD * (hd + 1))
                ov = o_ref[rs, sl]
                r = lax.rsqrt(jnp.mean(ov * ov, axis=-1, keepdims=True) + RMS_EPS)
                oh = ov * r
                g_ = gz[:, sl]
                sg = jax.nn.sigmoid(g_)
                dyv = dy_ref[rs, CW + HD * hd:CW + HD * (hd + 1)]
                don = dyv * (g_ * sg)
                dgzs.append(dyv * (oh * gnv) * (sg * (1.0 + g_ * (1.0 - sg))))
                dgn_ref[0:1, :] += jnp.sum(don * oh, axis=0, keepdims=True)
                doh = don * gnv
                do = r * (doh - oh * jnp.mean(doh * oh, axis=-1, keepdims=True))
                st = sp_ref[c, hd]
                dst = dst_ref[hd]
                qh, kh, ih = q[:, sl], kk[:, sl], ii[:, sl]
                a, parts = _intra(qh, kh, ex, sl, tt, ss, masks)
                da = jnp.where(tt >= ss, _dot(do, ih, NT), 0.0)
                dis.append(_dot(a, do, TN) + _dot(kr[:, sl], dst, NT))
                q0h = q0[:, sl].astype(MX)
                krh = kr[:, sl].astype(MX)
                dq0 = _dot(do, st, NN)
                dkr = _dot(ih, dst, NN)
                dq = dq0 * eg[:, sl]
                dk = dkr * egr[:, sl]
                kdk = krh.astype(F32) * dkr
                db = q0h.astype(F32) * dq0 - kdk
                tails.append(jnp.sum(kdk, axis=0, keepdims=True)
                             + eg[CH - 1:CH, sl] * jnp.sum(dst * st, axis=0, keepdims=True))
                dga = jnp.sum(jnp.where(tt == ss, da, 0.0), axis=-1, keepdims=True)
                dq = dq + dga * kh
                dk = dk + dga * qh
                for l in range(len(LEVELS)):
                    e_l, ql, kl = parts[l]
                    dpl = jnp.where(masks[l], da, 0.0).astype(MX)
                    dql = _dot(dpl, kl, NN)
                    dkl = _dot(dpl, ql, TN)
                    dq = dq + dql * e_l
                    dk = dk + dkl * e_l
                    db = db + (ql.astype(F32) * dql - kl.astype(F32) * dkl)
                dst_ref[hd] = dst * eg[CH - 1:CH, sl] + _dot(do, q0h, TN)
                dqs.append(dq)
                dks.append(dk)
                dbs.append(db)
            dq = jnp.concatenate(dqs, axis=1)
            dk = jnp.concatenate(dks, axis=1)
            db = jnp.concatenate(dbs, axis=1)
            dlf = _split_dot(suf_ref[...], db) + jnp.concatenate(tails, axis=1)
            t = jnp.where(vc, dlf * jnp.where(f > F_FLOOR, 1.0 / f, 0.0) - dk, 0.0)
            dlb_ref[0:1, :] += jnp.sum(t * (1.0 - sig), axis=0, keepdims=True)
            dh_ref[rs, 768:1280] = jnp.where(vc, dq * SCALE, 0.0).astype(dh_ref.dtype)
            dh_ref[rs, 1280:1792] = (t * (1.0 - lbv) * (sig * (1.0 - sig))).astype(dh_ref.dtype)
            dh_ref[rs, 1792:2304] = jnp.where(vc, jnp.concatenate(dis, axis=1), 0.0).astype(dh_ref.dtype)
            dh_ref[rs, 2304:2816] = jnp.where(vc, jnp.concatenate(dgzs, axis=1), 0.0).astype(dh_ref.dtype)

    cur = lambda i: (nb - 1 - i, 0)
    prev = lambda i: (jnp.maximum((nb - 1 - i) * q4 - 1, 0), 0)
    nxt = lambda i: (jnp.minimum((nb - 1 - i) * q4 + q4, lasth), 0)
    fix = lambda i: (0, 0)
    body, in_specs, args = _after(
        dep, body,
        [pl.BlockSpec((BM, DIN), cur), pl.BlockSpec((HALO, DIN), prev), pl.BlockSpec((HALO, DIN), nxt),
         pl.BlockSpec((BM, D), cur), pl.BlockSpec((HALO, D), nxt), pl.BlockSpec((BM, HW), cur),
         pl.BlockSpec((MB, NH, HD, HD), lambda i: (nb - 1 - i, 0, 0, 0)),
         pl.BlockSpec((SUBLANES, CW), fix), pl.BlockSpec((PW, PW), fix), pl.BlockSpec((1, PW), fix),
         pl.BlockSpec((1, HD), fix), pl.BlockSpec((1, HW), fix), pl.BlockSpec(mat.shape, fix),
         pl.BlockSpec(suf.shape, fix)],
        [h, h, h, dy, dy, o, sp, taps, wbd, ps, gn, lb, mat, suf])
    return pl.pallas_call(
        body, name=name, grid=(nb,), in_specs=in_specs,
        out_specs=[pl.BlockSpec((BM, DIN), cur), pl.BlockSpec((SUBLANES, CW), fix), pl.BlockSpec((SUBLANES, PW), fix),
                   pl.BlockSpec((SUBLANES, HD), fix), pl.BlockSpec((SUBLANES, HW), fix), pl.BlockSpec((PW, PW), fix)],
        out_shape=[jax.ShapeDtypeStruct((R, DIN), MX), jax.ShapeDtypeStruct((SUBLANES, CW), F32),
                   jax.ShapeDtypeStruct((SUBLANES, PW), F32), jax.ShapeDtypeStruct((SUBLANES, HD), F32),
                   jax.ShapeDtypeStruct((SUBLANES, HW), F32), jax.ShapeDtypeStruct((PW, PW), F32)],
        scratch_shapes=[pltpu.VMEM((NH, HD, HD), F32)],
        compiler_params=_cp(("arbitrary",), 40 * 2 ** 20),
    )(*args)


def _sum_slots(recv, *, name):
    S, L, rows, cols = recv.shape
    tr = _row_tile(rows, S * cols * 4, 6 * 2 ** 20)

    def body(r_ref, o_ref):
        acc = r_ref[0]
        for s in range(1, S):
            acc = acc + r_ref[s]
        o_ref[...] = acc

    return pl.pallas_call(
        body, name=name, grid=(L, rows // tr),
        in_specs=[pl.BlockSpec((S, None, tr, cols), lambda l, i: (0, l, i, 0))],
        out_specs=pl.BlockSpec((None, tr, cols), lambda l, i: (l, i, 0)),
        out_shape=jax.ShapeDtypeStruct((L, rows, cols), F32),
        compiler_params=_cp(("parallel", "parallel"), 4 * S * tr * cols * 4),
    )(recv)


def _sum_own(recv, own, chip, *, name):
    S, rows, cols = recv.shape
    tr = _row_tile(rows, S * cols * 4, 6 * 2 ** 20)

    def body(me_ref, r_ref, o_ref, out_ref):
        me = me_ref[0]
        acc = None
        for s in range(S):
            t = jnp.where(me == s, o_ref[...], r_ref[s])
            acc = t if acc is None else acc + t
        out_ref[...] = acc

    grid_spec = pltpu.PrefetchScalarGridSpec(
        num_scalar_prefetch=1, grid=(rows // tr,),
        in_specs=[pl.BlockSpec((S, tr, cols), lambda i, me: (0, i, 0)),
                  pl.BlockSpec((None, tr, cols), lambda i, me: (me[0], i, 0))],
        out_specs=pl.BlockSpec((tr, cols), lambda i, me: (i, 0)))
    return pl.pallas_call(
        body, name=name, grid_spec=grid_spec, out_shape=jax.ShapeDtypeStruct((rows, cols), F32),
        compiler_params=_cp(("parallel",), 5 * S * tr * cols * 4),
    )(chip.reshape(1).astype(jnp.int32), recv, own)


def _adamw(w, m, v, ga, gb, *, layer, prev, name):
    L, rows, cols = w.shape
    tr = _row_tile(rows, cols * 4, 2 ** 20)
    two = gb is not None
    nin = 5 if two else 4

    def body(*refs):
        w_ref, m_ref, v_ref, a_ref = refs[:4]
        g_ref, d_ref, m2_ref, v2_ref = refs[-4:]
        g = a_ref[...] + refs[4][...] if two else a_ref[...]
        m2 = ADAM_B1 * m_ref[...] + (1.0 - ADAM_B1) * g
        v2 = ADAM_B2 * v_ref[...] + (1.0 - ADAM_B2) * (g * g)
        m_hat = m2 / (1.0 - ADAM_B1 ** ADAM_STEP)
        v_hat = v2 / (1.0 - ADAM_B2 ** ADAM_STEP)
        g_ref[...] = g
        d_ref[...] = -ADAM_LR * (m_hat / (jnp.sqrt(v_hat) + ADAM_EPS) + ADAM_WD * w_ref[...])
        m2_ref[...] = m2
        v2_ref[...] = v2

    spec = pl.BlockSpec((None, tr, cols), lambda i: (layer, i, 0))
    gspec = pl.BlockSpec((tr, cols), lambda i: (i, 0))
    args = [w, m, v, ga] + ([gb] if two else [])
    in_specs = [spec] * 3 + [gspec] * (nin - 3)
    aliases = {}
    if prev is not None:
        args += list(prev)
        in_specs += [ANY] * 4
        aliases = {nin + j: j for j in range(4)}
    sd = jax.ShapeDtypeStruct((L, rows, cols), F32)
    return pl.pallas_call(
        body, name=name, grid=(rows // tr,), in_specs=in_specs, out_specs=[spec] * 4,
        out_shape=[sd] * 4, input_output_aliases=aliases,
        compiler_params=_cp(("parallel",), 24 * tr * cols * 4),
    )(*args)


def _exchange(arrays, *, flips, n_slots, slot_of, scatter, self_copy, dep=None, name):
    n = len(arrays)
    nf = len(flips)
    deps = [] if dep is None else [dep]
    nd = len(deps)
    out_shapes = [jax.ShapeDtypeStruct(a.shape if scatter else (n_slots,) + a.shape, a.dtype) for a in arrays]

    def body(*refs):
        ins, outs = refs[:n], refs[n + nd:2 * n + nd]
        send_sems, recv_sems, loc_sems = refs[2 * n + nd:]
        x, y, c = lax.axis_index("x"), lax.axis_index("y"), lax.axis_index("c")
        me = slot_of(x, y, c)
        peers = [(1 - x if fx else x, 1 - y if fy else y, 1 - c if fc else c) for fx, fy, fc in flips]
        local, remote = [], []
        for a in range(n):
            if self_copy:
                lc = pltpu.make_async_copy(ins[a].at[me] if scatter else ins[a], outs[a].at[me], loc_sems.at[a])
                lc.start()
                local.append(lc)
            for k, p in enumerate(peers):
                src = ins[a].at[slot_of(*p)] if scatter else ins[a]
                cp = pltpu.make_async_remote_copy(
                    src_ref=src, dst_ref=outs[a].at[me], send_sem=send_sems.at[a, k], recv_sem=recv_sems.at[a, k],
                    device_id=p, device_id_type=MESH)
                cp.start()
                remote.append(cp)
        for a in range(n):
            for k, p in enumerate(peers):
                src = ins[a].at[slot_of(*p)] if scatter else ins[a]
                pltpu.make_async_remote_copy(
                    src_ref=src, dst_ref=outs[a].at[slot_of(*p)], send_sem=send_sems.at[a, k],
                    recv_sem=recv_sems.at[a, k], device_id=p, device_id_type=MESH).wait_recv()
        for cp in remote:
            cp.wait_send()
        for lc in local:
            lc.wait()

    return pl.pallas_call(
        body, name=name, in_specs=[ANY] * (n + nd), out_specs=[ANY] * n, out_shape=out_shapes,
        scratch_shapes=[pltpu.SemaphoreType.DMA((n, nf)), pltpu.SemaphoreType.DMA((n, nf)),
                        pltpu.SemaphoreType.DMA((n,))],
        compiler_params=pltpu.CompilerParams(has_side_effects=True),
    )(*arrays, *deps)


ALL_FLIPS = [(fx, fy, fc) for fx in (0, 1) for fy in (0, 1) for fc in (0, 1) if fx or fy or fc]


def _zero_slot(x, y, c):
    return 0


HBM_SPEC = pl.BlockSpec(memory_space=pltpu.HBM)
SEM_SPEC = pl.BlockSpec(memory_space=pltpu.SEMAPHORE)


N_SLOTS = dict(chips=4, sibling=1, all=8)


def _peers(to, x, y, c):
    if to == "sibling":
        return [(x, y, 1 - c)]
    if to == "chips":
        return [(1 - x, y, c), (x, 1 - y, c), (1 - x, 1 - y, c)]
    return [(1 - x if fx else x, 1 - y if fy else y, 1 - c if fc else c) for fx, fy, fc in ALL_FLIPS]


def _slot(to, x, y, c):
    return {"chips": 2 * x + y, "sibling": 0, "all": 4 * x + 2 * y + c}[to]


def _send_start(srcs, *, scatter, to="chips", dep=None, name):
    n = len(srcs)
    npeer = N_SLOTS[to] - (to != "sibling")
    nc = n * npeer
    srcs = [pltpu.with_memory_space_constraint(s, pltpu.HBM) for s in srcs]
    land_shapes = [s.shape if scatter else (N_SLOTS[to],) + s.shape for s in srcs]
    lands = [pltpu.with_memory_space_constraint(lax.empty(sh, s.dtype), pltpu.HBM) for sh, s in zip(land_shapes, srcs)]

    deps = [] if dep is None else [dep]
    nd = len(deps)

    def body(*refs):
        ins, lnd = refs[:n], refs[n:2 * n]
        send_sems, recv_sems = refs[2 * n + nd:2 * n + nd + nc], refs[2 * n + nd + nc:2 * n + nd + 2 * nc]
        token = refs[-1]
        x, y, c = lax.axis_index("x"), lax.axis_index("y"), lax.axis_index("c")
        me = _slot(to, x, y, c)
        for a in range(n):
            for k, p in enumerate(_peers(to, x, y, c)):
                src = ins[a].at[_slot(to, *p)] if scatter else ins[a]
                j = a * npeer + k
                pltpu.make_async_remote_copy(
                    src_ref=src, dst_ref=lnd[a].at[me], send_sem=send_sems[j], recv_sem=recv_sems[j],
                    device_id=p, device_id_type=MESH).start()
        token[...] = jnp.zeros(token.shape, token.dtype)

    sem = pltpu.SemaphoreType.DMA(())
    outs = pl.pallas_call(
        body, name=name,
        out_shape=(*[sem] * (2 * nc), *[pltpu.HBM(s.shape, s.dtype) for s in srcs],
                   *[pltpu.HBM(sh, s.dtype) for sh, s in zip(land_shapes, srcs)],
                   jax.ShapeDtypeStruct((SUBLANES, LANES), F32)),
        in_specs=[HBM_SPEC] * (2 * n) + [ANY] * nd,
        out_specs=(*[SEM_SPEC] * (2 * nc), *[HBM_SPEC] * (2 * n), pl.BlockSpec(memory_space=pltpu.VMEM)),
        input_output_aliases={i: 2 * nc + i for i in range(2 * n)},
        compiler_params=pltpu.CompilerParams(has_side_effects=pltpu.SideEffectType.DATAFLOW_SIDE_EFFECTING),
    )(*srcs, *lands, *deps)
    return dict(sems=list(outs[:2 * nc]), srcs=list(outs[2 * nc:2 * nc + n]),
                lands=list(outs[2 * nc + n:2 * nc + 2 * n]), token=outs[-1], to=to)


def _send_wait(h, *, scatter, after, name):
    n = len(h["srcs"])
    to = h["to"]
    nc = len(h["sems"]) // 2

    def body(*refs):
        ins, lnd = refs[:n], refs[n:2 * n]
        send_sems, recv_sems = refs[2 * n:2 * n + nc], refs[2 * n + nc:2 * n + 2 * nc]
        x, y, c = lax.axis_index("x"), lax.axis_index("y"), lax.axis_index("c")
        for a in range(n):
            for k, p in enumerate(_peers(to, x, y, c)):
                slot = _slot(to, *p)
                j = a * (nc // n) + k
                cp = pltpu.make_async_remote_copy(
                    src_ref=ins[a].at[slot] if scatter else ins[a], dst_ref=lnd[a].at[slot],
                    send_sem=send_sems[j], recv_sem=recv_sems[j], device_id=p, device_id_type=MESH)
                cp.wait_send()
                cp.wait_recv()

    thru = h["srcs"] + h["lands"]
    outs = pl.pallas_call(
        body, name=name, out_shape=tuple(pltpu.HBM(t.shape, t.dtype) for t in thru),
        in_specs=[HBM_SPEC] * (2 * n) + [SEM_SPEC] * (2 * nc) + [ANY] * len(after),
        out_specs=tuple([HBM_SPEC] * (2 * n)),
        input_output_aliases={i: i for i in range(2 * n)},
        compiler_params=pltpu.CompilerParams(has_side_effects=pltpu.SideEffectType.DATAFLOW_SIDE_EFFECTING),
    )(*thru, *h["sems"], *after)
    return list(outs[:n]), list(outs[n:])


def _assemble(land, own, chip, axis):
    return jnp.concatenate([jnp.where(chip == k, own, land[k]) for k in range(4)], axis=axis)


def _pack(arrs):
    flat = jnp.concatenate([a.reshape(-1).astype(F32) for a in arrs])
    tile = SUBLANES * LANES
    pad = (-flat.shape[0]) % tile
    return jnp.pad(flat, (0, pad)).reshape(-1, LANES)


def _unpack(buf, shapes):
    flat = buf.reshape(-1)
    out, off = [], 0
    for s in shapes:
        n = int(np.prod(s))
        out.append(flat[off:off + n].reshape(s))
        off += n
    return out


def _lower_bounds(hg_lower_bounds):
    p = jax.nn.softmax(hg_lower_bounds.astype(F32), axis=0)
    return jnp.cumsum(p, axis=0) - p[0]


def kernel(x, meta_tokens, hg_lower_bounds, w_in, w_conv, w_pool, pool_scale, hg_norm_g, w_o, ln1_g, ln1_b, w_up, w_ffn_conv, b_ffn_conv, w_down, ln2_g, ln2_b, loss_target, m_meta_tokens, m_hg_lower_bounds, m_w_in, m_w_conv, m_w_pool, m_pool_scale, m_hg_norm_g, m_w_o, m_ln1_g, m_ln1_b, m_w_up, m_w_ffn_conv, m_b_ffn_conv, m_w_down, m_ln2_g, m_ln2_b, v_meta_tokens, v_hg_lower_bounds, v_w_in, v_w_conv, v_w_pool, v_pool_scale, v_hg_norm_g, v_w_o, v_ln1_g, v_ln1_b, v_w_up, v_w_ffn_conv, v_b_ffn_conv, v_w_down, v_ln2_g, v_ln2_b):
    S = x.shape[1]
    R = S + ROW0
    Fq = w_down.shape[1]
    F = 4 * Fq
    F2 = 2 * F
    F2q = w_up.shape[2]
    assert x.shape == (1, S, D) and R % 384 == 0 and S % ROW0 == 0
    chip = 2 * lax.axis_index("x") + lax.axis_index("y")
    tm = 384
    tm_w = max(t for t in range(SUBLANES, 2113, SUBLANES) if R % t == 0)
    tb_ffn = 128

    small_shapes = [(N_META, D // 4), (DEPTH, CW // 4, 3), (DEPTH, F2q, 3)]
    wb_in, wb_o, wb_up, wb_down = (w.astype(MX) for w in (w_in, w_o, w_up, w_down))
    h_s = _send_start([_pack([meta_tokens, w_conv, w_ffn_conv])], scatter=False, name="gather_small_start")
    h_a = _send_start([wb_in[0]], scatter=False, dep=h_s["token"], name="gather_a_start")
    (own_small,), (l_small,) = _send_wait(h_s, scatter=False, after=[h_a["token"]], name="gather_small_wait")
    sm = [_unpack(jnp.where(chip == k, own_small, l_small[k]), small_shapes) for k in range(4)]
    meta_full = jnp.concatenate([sm[k][0] for k in range(4)], axis=1)
    X = jnp.concatenate([jnp.zeros((PADR, D), F32), meta_full, x[0]], axis=0)
    (own_in,), (l_in,) = _send_wait(h_a, scatter=False, after=[X, wb_o, wb_up, wb_down], name="gather_a_wait")
    h_b = _send_start([wb_o[0], wb_up[0], wb_down[0]], scatter=False, dep=l_in, name="gather_b_start")
    Win, Wo, Wup, Wdown = {}, {}, {}, {}
    Win[0] = _assemble(l_in, own_in, chip, 1)
    wconv_full = jnp.concatenate([sm[k][1] for k in range(4)], axis=1)
    wffn_full = jnp.concatenate([sm[k][2] for k in range(4)], axis=1)
    taps_c = jnp.pad(wconv_full.transpose(0, 2, 1), ((0, 0), (0, SUBLANES - 3), (0, 0)))
    taps_f = jnp.pad(wffn_full.transpose(0, 2, 1), ((0, 0), (0, SUBLANES - 3), (0, 0)))
    wbd = jnp.stack([jax.scipy.linalg.block_diag(*[w_pool[l, g] for g in range(4)]) for l in range(DEPTH)]).astype(MX)
    lbs, lbs_vjp = jax.vjp(_lower_bounds, hg_lower_bounds)
    mat, suf = _hg_consts()

    def fwd_mixer(l, X, dep=None):
        h = _mm(X, Win[l], tm=tm, dep=dep, name=f"mm_in_{l}")
        y, o, sp = _mixer_fwd(h, taps_c[l], wbd[l], pool_scale[l].reshape(1, PW), hg_norm_g[l].reshape(1, HD),
                              lbs[l].reshape(1, HW), mat, name=f"mixer_fwd_{l}")
        return h, y, o, sp

    def fwd_rest(l, X, h, y, o, sp, dep=None):
        x1, xh1, r1 = _mm_ln(y, Wo[l], X, ln1_g[l], ln1_b[l], tm=tm, dep=dep, name=f"mm_o_ln_{l}")
        up, u, a = _mm_ffn_fwd(x1, Wup[l], taps_f[l], b_ffn_conv[l].reshape(1, F2), tm=tm // 2,
                               name=f"mm_up_ffn_{l}")
        x2, xh2, r2 = _mm_ln(a, Wdown[l], x1, ln2_g[l], ln2_b[l], tm=tm, name=f"mm_down_ln_{l}")
        return (X, h, y, o, sp, x1, xh1, r1, up, u, a, xh2, r2), x2

    h, y, o, sp = fwd_mixer(0, X, dep=h_b["token"])
    (own_o, own_up, own_down), (l_o, l_up, l_down) = _send_wait(h_b, scatter=False, after=[y], name="gather_b_wait")
    Wo[0], Wup[0], Wdown[0] = (_assemble(l_o, own_o, chip, 0), _assemble(l_up, own_up, chip, 1),
                               _assemble(l_down, own_down, chip, 0))
    h_c = _send_start([wb_in[1], wb_o[1], wb_up[1], wb_down[1]], scatter=False, dep=l_o, name="gather_c_start")
    saved0, X1 = fwd_rest(0, X, h, y, o, sp, dep=h_c["token"])
    own_c, l_c = _send_wait(h_c, scatter=False, after=[X1], name="gather_c_wait")
    Win[1], Wo[1] = _assemble(l_c[0], own_c[0], chip, 1), _assemble(l_c[1], own_c[1], chip, 0)
    Wup[1], Wdown[1] = _assemble(l_c[2], own_c[2], chip, 1), _assemble(l_c[3], own_c[3], chip, 0)
    saved1, X2 = fwd_rest(1, X1, *fwd_mixer(1, X1))
    saved = [saved0, saved1]

    dxo = X2

    sc = {}

    def scatter(nm, l, g, dep=None):
        sc[nm, l] = _send_start([g], scatter=True, dep=dep, name=f"scatter_{nm}_{l}_start")
        return sc[nm, l]["token"]

    tok = None

    small_g = [None] * DEPTH
    for l in reversed(range(DEPTH)):
        X, h, y, o, sp, x1, xh1, r1, up, u, a, xh2, r2 = saved[l]
        if l == DEPTH - 1:
            dz2, gb2, da, sq = _ln_bwd_mm(dxo, xh2, r2, ln2_g[l], Wdown[l], tm=ROW0, tgt=loss_target[0],
                                          name=f"loss_ln2_bwd_da_{l}")
            loss = lax.psum(0.5 * jnp.sum(sq[0]) / D, ("x", "y", "c"))
        else:
            dz2, gb2, da = _ln_bwd_mm(dxo, xh2, r2, ln2_g[l], Wdown[l], tm=tm, dep=tok, name=f"ln2_bwd_da_{l}")
        tok = scatter("w_down", l, _wgrad(a, dz2, slabs_on_cols=False, tm=tm_w, tn=512, name=f"wgrad_down_{l}"))
        dup, facc = _ffn_bwd(da, up, u, taps_f[l], tb=tb_ffn, dep=tok, name=f"ffn_bwd_{l}")
        dx1 = _mm(dup, Wup[l], nt=True, res=dz2, tm=tm, name=f"mm_dx1_{l}")
        tok = scatter("w_up", l, _wgrad(x1, dup, slabs_on_cols=True, tm=tm_w, tn=F2q, name=f"wgrad_up_{l}"))
        dz1, gb1, dym = _ln_bwd_mm(dx1, xh1, r1, ln1_g[l], Wo[l], tm=tm, dep=tok, name=f"ln1_bwd_dym_{l}")
        tok = scatter("w_o", l, _wgrad(y, dz1, slabs_on_cols=False, tm=tm_w, tn=512, name=f"wgrad_o_{l}"))
        dh, dwc, dsc, dgn, dlb, dwbd = _mixer_bwd(
            h, dym, o, sp, taps_c[l], wbd[l], pool_scale[l].reshape(1, PW), hg_norm_g[l].reshape(1, HD),
            lbs[l].reshape(1, HW), mat, suf, dep=tok, name=f"mixer_bwd_{l}")
        tok = scatter("w_in", l, _wgrad(X, dh, slabs_on_cols=True, tm=tm_w, tn=DIN // 4, name=f"wgrad_in_{l}"))
        if l == 0:
            dx_head, dx_seq = _mm_dx_head(dh, Win[l], dz1, dep=tok, name=f"mm_dx_{l}")
        else:
            dxo = _mm(dh, Win[l], nt=True, res=dz1, tm=tm, dep=tok, name=f"mm_dx_{l}")
        small_g[l] = dict(
            lbs=dlb[0], w_conv=dwc[0:3].T, w_pool=jnp.stack([dwbd[64 * g:64 * g + 64, 64 * g:64 * g + 64] for g in range(4)]),
            pool_scale=dsc[0], hg_norm_g=dgn[0], ln1_g=gb1[0], ln1_b=gb1[1], w_ffn_conv=facc[0:3].T,
            b_ffn_conv=facc[3], ln2_g=gb2[0], ln2_b=gb2[1])
    grad_x = dx_seq[None]

    sg_names = ["lbs", "w_conv", "w_pool", "pool_scale", "hg_norm_g", "ln1_g", "ln1_b", "w_ffn_conv",
                "b_ffn_conv", "ln2_g", "ln2_b"]
    sg_list = [dx_head[PADR:ROW0]] + [jnp.stack([small_g[l][nm] for l in range(DEPTH)]) for nm in sg_names]
    sg_shapes = [a.shape for a in sg_list]
    packed = _pack(sg_list)
    h_g = _send_start([packed], scatter=False, to="all", name="gather_small_grads_start")

    big_w = dict(w_in=(w_in, m_w_in, v_w_in), w_o=(w_o, m_w_o, v_w_o), w_up=(w_up, m_w_up, v_w_up),
                 w_down=(w_down, m_w_down, v_w_down))
    groups = {1: ("w_down", "w_up", "w_o", "w_in"), 0: ("w_down", "w_up", "w_o")}
    swaps, tokc = {}, h_g["token"]
    for l in reversed(range(DEPTH)):
        part = []
        for nm in groups[l]:
            after = [dx_seq, tokc]
            (own,), (recv,) = _send_wait(sc[nm, l], scatter=True, after=after, name=f"scatter_{nm}_{l}_wait")
            part.append(_sum_own(recv, own, chip, name=f"sum_{nm}_{l}"))
        swaps[l] = _send_start(part, scatter=False, to="sibling", name=f"swap_cores_{l}_start")
        tokc = swaps[l]["token"]

    (own_small_g,), (l_small_g,) = _send_wait(h_g, scatter=False, after=[tokc], name="gather_small_grads_wait")
    dev = 2 * chip + lax.axis_index("c")
    gathered = jnp.where((jnp.arange(8) == dev)[:, None, None], own_small_g[None], l_small_g)
    total = _sum_slots(gathered[:, None], name="sum_small_grads")[0]
    tot = dict(zip(["meta_tokens"] + sg_names, _unpack(total, sg_shapes)))
    (g_hg,) = lbs_vjp(tot["lbs"])
    small_grads = dict(
        meta_tokens=lax.dynamic_slice_in_dim(tot["meta_tokens"], chip * (D // 4), D // 4, axis=1),
        hg_lower_bounds=g_hg,
        w_conv=lax.dynamic_slice_in_dim(tot["w_conv"], chip * (CW // 4), CW // 4, axis=1),
        w_pool=tot["w_pool"], pool_scale=tot["pool_scale"], hg_norm_g=tot["hg_norm_g"],
        ln1_g=tot["ln1_g"], ln1_b=tot["ln1_b"],
        w_ffn_conv=lax.dynamic_slice_in_dim(tot["w_ffn_conv"], chip * F2q, F2q, axis=1),
        b_ffn_conv=tot["b_ffn_conv"], ln2_g=tot["ln2_g"], ln2_b=tot["ln2_b"])
    small_w = dict(meta_tokens=(meta_tokens, m_meta_tokens, v_meta_tokens),
                   hg_lower_bounds=(hg_lower_bounds, m_hg_lower_bounds, v_hg_lower_bounds),
                   w_conv=(w_conv, m_w_conv, v_w_conv), w_pool=(w_pool, m_w_pool, v_w_pool),
                   pool_scale=(pool_scale, m_pool_scale, v_pool_scale), hg_norm_g=(hg_norm_g, m_hg_norm_g, v_hg_norm_g),
                   ln1_g=(ln1_g, m_ln1_g, v_ln1_g), ln1_b=(ln1_b, m_ln1_b, v_ln1_b),
                   w_ffn_conv=(w_ffn_conv, m_w_ffn_conv, v_w_ffn_conv), b_ffn_conv=(b_ffn_conv, m_b_ffn_conv, v_b_ffn_conv),
                   ln2_g=(ln2_g, m_ln2_g, v_ln2_g), ln2_b=(ln2_b, m_ln2_b, v_ln2_b))
    names_s = list(small_w)
    shapes_s = [small_w[nm][0].shape for nm in names_s]
    pk = [_pack([small_w[nm][j] for nm in names_s])[None] for j in range(3)]
    pg = _pack([small_grads[nm] for nm in names_s])[None]
    outs_s = _adamw(pk[0], pk[1], pk[2], pg[0], None, layer=0, prev=None, name="adamw_small")
    small = {nm: [] for nm in names_s}
    for j in range(4):
        for nm, val in zip(names_s, _unpack(outs_s[j][0], shapes_s)):
            small[nm].append(val)

    big = {nm: None for nm in big_w}
    after = [outs_s[0]]
    for l in reversed(range(DEPTH)):
        mine, theirs = _send_wait(swaps[l], scatter=False, after=after, name=f"swap_cores_{l}_wait")
        for k, nm in enumerate(groups[l]):
            w, m, v = big_w[nm]
            big[nm] = _adamw(w, m, v, mine[k], theirs[k][0], layer=l, prev=big[nm], name=f"adamw_{nm}_{l}")
        after = [big[nm][0] for nm in groups[l]]
    (own,), (recv,) = _send_wait(sc["w_in", 0], scatter=True, after=after, name="scatter_w_in_0_wait")
    part = _sum_own(recv, own, chip, name="sum_w_in_0")
    (sib,) = _exchange([part], flips=[(0, 0, 1)], n_slots=1, slot_of=_zero_slot, scatter=False, self_copy=False,
                       name="swap_cores_0_in")
    big["w_in"] = _adamw(*big_w["w_in"], part, sib[0], layer=0, prev=big["w_in"], name="adamw_w_in_0")

    order = ["meta_tokens", "hg_lower_bounds", "w_in", "w_conv", "w_pool", "pool_scale", "hg_norm_g", "w_o",
             "ln1_g", "ln1_b", "w_up", "w_ffn_conv", "b_ffn_conv", "w_down", "ln2_g", "ln2_b"]
    res = {nm: (big[nm] if nm in big else small[nm]) for nm in order}
    outs = [loss, grad_x]
    for j in range(4):
        outs += [res[nm][j] for nm in order]
    return tuple(outs)
```

```python
import numpy as np

import jax
import jax.numpy as jnp
from jax import lax
from jax.experimental import pallas as pl
from jax.experimental.pallas import tpu as pltpu

F32 = jnp.float32
BF = jnp.bfloat16
MX = jnp.bfloat16

D = 1024
CW = 256
HW = 512
HD = 128
NH = 4
PW = 256
DIN = 3072
N_META = 16
CH = 64
MB = 4
BM = MB * CH
ROW0 = 256
PADR = ROW0 - N_META
HALO = 16
FH = 8
LEVELS = (32, 16, 8, 4, 2, 1)
DEPTH = 2
ALPHA = (2 * DEPTH) ** 0.25
LN_EPS = 1e-5
RMS_EPS = 1e-6
F_FLOOR = 1e-30
SCALE = HD ** -0.5
ADAM_LR, ADAM_B1, ADAM_B2, ADAM_EPS, ADAM_WD, ADAM_STEP = 0.001, 0.9, 0.999, 1e-08, 0.01, 10

VMEM_V7X = 64 * 2 ** 20
LANES = 128
SUBLANES = 8

NN = (((1,), (0,)), ((), ()))
NT = (((1,), (1,)), ((), ()))
TN = (((0,), (0,)), ((), ()))
MESH = pl.DeviceIdType.MESH
ANY = pl.BlockSpec(memory_space=pl.ANY)


def _dot(a, b, dn):
    return lax.dot_general(a.astype(MX), b.astype(MX), dn, preferred_element_type=F32)


def _cp(sem, est_bytes):
    lim = int(min(VMEM_V7X - 6 * 2 ** 20, max(32 * 2 ** 20, est_bytes)))
    return pltpu.CompilerParams(dimension_semantics=sem, vmem_limit_bytes=lim)


def _nbytes(shape, dtype):
    return int(np.prod(shape)) * jnp.dtype(dtype).itemsize


def _row_tile(rows, row_bytes, budget):
    best = SUBLANES
    for t in range(SUBLANES, rows + 1, SUBLANES):
        if rows % t == 0 and t * row_bytes <= budget:
            best = t
    return best


def _after(dep, body, in_specs, args):
    if dep is None:
        return body, list(in_specs), list(args)

    def body_after(dep_ref, *refs):
        body(*refs)

    return body_after, [ANY] + list(in_specs), [dep] + list(args)


def _mm(a, w, *, nt=False, res=None, tm, out_dtype=F32, zero_inert=False, dep=None, name):
    R, K = a.shape
    N = w.shape[0] if nt else w.shape[1]
    dn = NT if nt else NN

    def body(*refs):
        if res is None:
            a_ref, w_ref, o_ref = refs
        else:
            a_ref, w_ref, r_ref, o_ref = refs
        acc = _dot(a_ref[...], w_ref[...], dn)
        if res is not None:
            acc = acc + ALPHA * r_ref[...]
        if zero_inert:
            acc = jnp.where(_row_ids(pl.program_id(0) * tm, tm) >= PADR, acc, 0.0)
        o_ref[...] = acc.astype(out_dtype)

    in_specs = [pl.BlockSpec((tm, K), lambda i: (i, 0)), pl.BlockSpec(w.shape, lambda i: (0, 0))]
    args = [a, w]
    est = 2 * _nbytes((tm, K), a.dtype) + 2 * _nbytes(w.shape, w.dtype) + 3 * _nbytes((tm, N), F32)
    if res is not None:
        in_specs.append(pl.BlockSpec((tm, N), lambda i: (i, 0)))
        args.append(res)
        est += 2 * _nbytes((tm, N), F32)
    body, in_specs, args = _after(dep, body, in_specs, args)
    return pl.pallas_call(
        body, name=name, grid=(R // tm,), in_specs=in_specs,
        out_specs=pl.BlockSpec((tm, N), lambda i: (i, 0)),
        out_shape=jax.ShapeDtypeStruct((R, N), out_dtype),
        compiler_params=_cp(("parallel",), est + 4 * 2 ** 20),
    )(*args)


def _wgrad(a, b, *, slabs_on_cols, tm, tn, name):
    R, Ka = a.shape
    Nb = b.shape[1]
    if slabs_on_cols:
        out_shape = (4, Ka, Nb // 4)
        assert tn == Nb // 4
        out_spec = pl.BlockSpec((None, Ka, tn), lambda j, i: (j, 0, 0))
    else:
        out_shape = (4, Ka // 4, Nb)
        out_spec = pl.BlockSpec((4, Ka // 4, tn), lambda j, i: (0, 0, j))

    def body(a_ref, b_ref, o_ref):
        @pl.when(pl.program_id(1) == 0)
        def _():
            o_ref[...] = jnp.zeros(o_ref.shape, F32)

        acc = _dot(a_ref[...], b_ref[...], TN)
        o_ref[...] += acc.reshape(o_ref.shape)

    in_specs = [pl.BlockSpec((tm, Ka), lambda j, i: (i, 0)), pl.BlockSpec((tm, tn), lambda j, i: (i, j))]
    est = 2 * _nbytes((tm, Ka), a.dtype) + 2 * _nbytes((tm, tn), b.dtype) + 4 * _nbytes((Ka, tn), F32) \
        + _nbytes((tm, Ka), F32)
    return pl.pallas_call(
        body, name=name, grid=(Nb // tn, R // tm), in_specs=in_specs, out_specs=out_spec,
        out_shape=jax.ShapeDtypeStruct(out_shape, F32),
        compiler_params=_cp(("parallel", "arbitrary"), est + 4 * 2 ** 20),
    )(a, b)


def _mm_ln(a, w, xres, g, b, *, tm, dep=None, name):
    R, K = a.shape

    def body(a_ref, w_ref, x_ref, g_ref, b_ref, xo_ref, xh_ref, r_ref):
        z = ALPHA * x_ref[...] + _dot(a_ref[...], w_ref[...], NN)
        mu = jnp.mean(z, axis=-1, keepdims=True)
        zc = z - mu
        var = jnp.mean(zc * zc, axis=-1, keepdims=True)
        r = lax.rsqrt(var + LN_EPS)
        xh = zc * r
        xh_ref[...] = xh
        r_ref[...] = r
        xo_ref[...] = xh * g_ref[...] + b_ref[...]

    row = lambda i: (i, 0)
    fix = lambda i: (0, 0)
    est = 2 * _nbytes((tm, K), a.dtype) + 2 * _nbytes(w.shape, w.dtype) + 10 * _nbytes((tm, D), F32)
    body, in_specs, args = _after(
        dep, body, [pl.BlockSpec((tm, K), row), pl.BlockSpec(w.shape, fix), pl.BlockSpec((tm, D), row),
                    pl.BlockSpec((1, D), fix), pl.BlockSpec((1, D), fix)],
        [a, w, xres, g.reshape(1, D), b.reshape(1, D)])
    return pl.pallas_call(
        body, name=name, grid=(R // tm,), in_specs=in_specs,
        out_specs=[pl.BlockSpec((tm, D), row), pl.BlockSpec((tm, D), row), pl.BlockSpec((tm, 1), row)],
        out_shape=[jax.ShapeDtypeStruct((R, D), F32), jax.ShapeDtypeStruct((R, D), F32),
                   jax.ShapeDtypeStruct((R, 1), F32)],
        compiler_params=_cp(("parallel",), est + 4 * 2 ** 20),
    )(*args)


def _resident(shape):
    return pl.BlockSpec(shape, lambda i: (0,) * len(shape), pipeline_mode=pl.Buffered(1))


def _ln_bwd_mm(dx, xh, r, g, w, *, tm, tgt=None, dep=None, name):
    R = dx.shape[0]
    N = w.shape[0]
    assert tgt is None or tm == ROW0

    def body(*refs):
        if tgt is None:
            dx_ref, xh_ref, r_ref, g_ref, w_ref, dz_ref, gb_ref, o_ref = refs
        else:
            dx_ref, t_ref, xh_ref, r_ref, g_ref, w_ref, dz_ref, gb_ref, o_ref, sq_ref = refs
        i = pl.program_id(0)

        @pl.when(i == 0)
        def _():
            gb_ref[...] = jnp.zeros(gb_ref.shape, F32)
            if tgt is not None:
                sq_ref[...] = jnp.zeros(sq_ref.shape, F32)

        if tgt is None:
            dxv = dx_ref[...]
        else:
            err = jnp.where(i > 0, dx_ref[...] - t_ref[...], 0.0)
            sq_ref[0:1, :] += jnp.sum(err * err, axis=0, keepdims=True)
            dxv = err / D
        xhv = xh_ref[...]
        dyh = dxv * g_ref[...]
        m1 = jnp.mean(dyh, axis=-1, keepdims=True)
        m2 = jnp.mean(dyh * xhv, axis=-1, keepdims=True)
        dz = r_ref[...] * (dyh - m1 - xhv * m2)
        dz_ref[...] = dz
        gb_ref[0:1, :] += jnp.sum(dxv * xhv, axis=0, keepdims=True)
        gb_ref[1:2, :] += jnp.sum(dxv, axis=0, keepdims=True)
        o_ref[...] = _dot(dz, w_ref[...], NT)

    row = lambda i: (i, 0)
    fix = lambda i: (0, 0)
    in_specs = [pl.BlockSpec((tm, D), row), pl.BlockSpec((tm, D), row), pl.BlockSpec((tm, 1), row),
                pl.BlockSpec((1, D), fix), _resident(w.shape)]
    args = [dx, xh, r, g.reshape(1, D), w]
    out_specs = [pl.BlockSpec((tm, D), row), pl.BlockSpec((SUBLANES, D), fix), pl.BlockSpec((tm, N), row)]
    out_shape = [jax.ShapeDtypeStruct((R, D), F32), jax.ShapeDtypeStruct((SUBLANES, D), F32),
                 jax.ShapeDtypeStruct((R, N), F32)]
    if tgt is not None:
        in_specs.insert(1, pl.BlockSpec((tm, D), lambda i: (jnp.maximum(i - 1, 0), 0)))
        args.insert(1, tgt)
        out_specs.append(pl.BlockSpec((SUBLANES, D), fix))
        out_shape.append(jax.ShapeDtypeStruct((SUBLANES, D), F32))
    body, in_specs, args = _after(dep, body, in_specs, args)
    est = _nbytes(w.shape, w.dtype) + 16 * _nbytes((tm, D), F32) + 4 * _nbytes((tm, N), F32)
    return pl.pallas_call(
        body, name=name, grid=(R // tm,), in_specs=in_specs, out_specs=out_specs, out_shape=out_shape,
        compiler_params=_cp(("arbitrary",), est + 4 * 2 ** 20),
    )(*args)


def _mm_dx_head(dh, w, res, *, dep=None, name):
    R, K = dh.shape
    tm = ROW0

    def body(a_ref, w_ref, r_ref, head_ref, seq_ref):
        i = pl.program_id(0)
        acc = _dot(a_ref[...], w_ref[...], NT) + ALPHA * r_ref[...]

        @pl.when(i == 0)
        def _():
            head_ref[...] = acc

        @pl.when(i > 0)
        def _():
            seq_ref[...] = acc

    row = lambda i: (i, 0)
    body, in_specs, args = _after(
        dep, body, [pl.BlockSpec((tm, K), row), _resident(w.shape), pl.BlockSpec((tm, D), row)], [dh, w, res])
    est = _nbytes(w.shape, w.dtype) + 2 * _nbytes((tm, K), dh.dtype) + 8 * _nbytes((tm, D), F32)
    return pl.pallas_call(
        body, name=name, grid=(R // tm,), in_specs=in_specs,
        out_specs=[pl.BlockSpec((tm, D), lambda i: (0, 0)), pl.BlockSpec((tm, D), lambda i: (jnp.maximum(i - 1, 0), 0))],
        out_shape=[jax.ShapeDtypeStruct((ROW0, D), F32), jax.ShapeDtypeStruct((R - ROW0, D), F32)],
        compiler_params=_cp(("arbitrary",), est + 4 * 2 ** 20),
    )(*args)


def _mm_ffn_fwd(x1, w, taps, bias, *, tm, name):
    R, K = x1.shape
    F2 = w.shape[1]
    F = F2 // 2

    def body(x_ref, w_ref, t_ref, b_ref, up_ref, u_ref, a_ref, carry_ref):
        i = pl.program_id(0)

        @pl.when(i == 0)
        def _():
            carry_ref[...] = jnp.zeros(carry_ref.shape, F32)

        acc = _dot(x_ref[...], w_ref[...], NN)
        acc = jnp.where(_row_ids(i * tm, tm) >= PADR, acc, 0.0)
        up_ref[...] = acc.astype(up_ref.dtype)
        ue = jnp.concatenate([carry_ref[...], acc], axis=0)
        carry_ref[...] = acc[tm - FH:tm]
        u = (t_ref[0:1, :] * pltpu.roll(ue, 2, 0)[FH:] + t_ref[1:2, :] * pltpu.roll(ue, 1, 0)[FH:]
             + t_ref[2:3, :] * acc + b_ref[...])
        u_ref[...] = u
        gate = u[:, :F]
        val = u[:, F:]
        a_ref[...] = (gate * jax.nn.sigmoid(gate) * val).astype(a_ref.dtype)

    row = lambda i: (i, 0)
    est = _nbytes(w.shape, w.dtype) + 2 * _nbytes((tm, K), x1.dtype) + 10 * _nbytes((tm + FH, F2), F32)
    return pl.pallas_call(
        body, name=name, grid=(R // tm,),
        in_specs=[pl.BlockSpec((tm, K), row), _resident(w.shape), _resident((SUBLANES, F2)), _resident((1, F2))],
        out_specs=[pl.BlockSpec((tm, F2), row), pl.BlockSpec((tm, F2), row), pl.BlockSpec((tm, F), row)],
        out_shape=[jax.ShapeDtypeStruct((R, F2), MX), jax.ShapeDtypeStruct((R, F2), F32),
                   jax.ShapeDtypeStruct((R, F), MX)],
        scratch_shapes=[pltpu.VMEM((FH, F2), F32)],
        compiler_params=_cp(("arbitrary",), est + 4 * 2 ** 20),
    )(x1, w, taps, bias)


def _row_ids(start, n):
    return start + lax.broadcasted_iota(jnp.int32, (n, 1), 0)


def _ffn_bwd(da, up, u, taps, *, tb, dep=None, name):
    R, F2 = up.shape
    F = F2 // 2
    nh = tb // FH
    nb = R // tb
    last = R // FH - 1
    m = tb + FH

    def body(da_ref, dan_ref, x_ref, u_ref, un_ref, w_ref, du_ref, acc_ref):
        i = pl.program_id(0)

        @pl.when(i == 0)
        def _():
            acc_ref[...] = jnp.zeros(acc_ref.shape, F32)

        inside = (i < nb - 1).astype(F32)
        for j in range(F // LANES):
            gsl = slice(j * LANES, (j + 1) * LANES)
            vsl = slice(F + j * LANES, F + (j + 1) * LANES)
            gate = jnp.concatenate([u_ref[:, gsl], un_ref[:, gsl]], axis=0)
            val = jnp.concatenate([u_ref[:, vsl], un_ref[:, vsl]], axis=0)
            dae = jnp.concatenate([da_ref[:, gsl], dan_ref[:, gsl] * inside], axis=0)
            sg = jax.nn.sigmoid(gate)
            gs = gate * sg
            for sl, du in ((gsl, dae * val * (sg * (1.0 + gate - gs))), (vsl, dae * gs)):
                du0 = du[0:tb]
                du1 = pltpu.roll(du, m - 1, 0)[0:tb]
                du2 = pltpu.roll(du, m - 2, 0)[0:tb]
                du_ref[:, sl] = (w_ref[2:3, sl] * du0 + w_ref[1:2, sl] * du1 + w_ref[0:1, sl] * du2).astype(du_ref.dtype)
                x = x_ref[:, sl].astype(F32)
                acc_ref[0:1, sl] += jnp.sum(du2 * x, axis=0, keepdims=True)
                acc_ref[1:2, sl] += jnp.sum(du1 * x, axis=0, keepdims=True)
                acc_ref[2:3, sl] += jnp.sum(du0 * x, axis=0, keepdims=True)
                acc_ref[3:4, sl] += jnp.sum(du0, axis=0, keepdims=True)

        @pl.when(i * tb < PADR)
        def _():
            row = _row_ids(i * tb, tb)
            du_ref[...] = jnp.where(row >= PADR, du_ref[...], jnp.zeros((), du_ref.dtype))

    nxt = lambda i: (jnp.minimum(i * nh + nh, last), 0)
    cur = lambda i: (i, 0)
    fix = lambda i: (0, 0)
    est = 12 * _nbytes((tb + 2 * FH, F2), F32)
    body, in_specs, args = _after(
        dep, body, [pl.BlockSpec((tb, F), cur), pl.BlockSpec((FH, F), nxt),
                    pl.BlockSpec((tb, F2), cur), pl.BlockSpec((tb, F2), cur), pl.BlockSpec((FH, F2), nxt),
                    pl.BlockSpec((SUBLANES, F2), fix)], [da, da, up, u, u, taps])
    return pl.pallas_call(
        body, name=name, grid=(R // tb,), in_specs=in_specs,
        out_specs=[pl.BlockSpec((tb, F2), cur), pl.BlockSpec((SUBLANES, F2), fix)],
        out_shape=[jax.ShapeDtypeStruct((R, F2), MX), jax.ShapeDtypeStruct((SUBLANES, F2), F32)],
        compiler_params=_cp(("arbitrary",), est),
    )(*args)


def _hg_consts():
    t = np.arange(CH)[:, None]
    j = np.arange(CH)[None, :]
    low = (j <= t).astype(np.float32)
    blocks = [low, (j > t).astype(np.float32)]
    for m in LEVELS:
        ref = (t // (2 * m)) * 2 * m + m - 1
        blocks.append(low - (j <= ref).astype(np.float32))
    mat = np.concatenate(blocks, axis=0)
    suf = (j >= t).astype(np.float32)
    return jnp.asarray(mat, BF), jnp.asarray(suf, BF)


def _split_dot(mat, x):
    hi = x.astype(BF)
    lo = (x - hi.astype(F32)).astype(BF)
    return (lax.dot_general(mat, hi, NN, preferred_element_type=F32)
            + lax.dot_general(mat, lo, NN, preferred_element_type=F32))


def _lane_select(a2, a4, a8, a16):
    lane = lax.broadcasted_iota(jnp.int32, (1, PW), 1)
    return jnp.where(lane < 64, a2, jnp.where(lane < 128, a4, jnp.where(lane < 192, a8, a16)))


def _pool_count(row):
    win = _lane_select(2.0, 4.0, 8.0, 16.0)
    t1 = jnp.maximum((row - PADR + 1).astype(F32), 1.0)
    return jnp.minimum(t1, win)


def _gates(fz, lb, valid):
    sig = jax.nn.sigmoid(fz)
    f = lb + (1.0 - lb) * sig
    lf = jnp.where(valid, jnp.log(jnp.maximum(f, F_FLOOR)), 0.0)
    kk = jnp.where(valid, (1.0 - lb) * (1.0 - sig), 0.0)
    return sig, f, lf, kk


def _level_masks():
    tt = lax.broadcasted_iota(jnp.int32, (CH, CH), 0)
    ss = lax.broadcasted_iota(jnp.int32, (CH, CH), 1)
    xr = tt ^ ss
    low = tt > ss
    return tt, ss, [(xr >= m) & (xr < 2 * m) & low for m in LEVELS]


def _intra(q, kk, ex, sl, tt, ss, masks):
    a = jnp.where(tt == ss, jnp.sum(q * kk, axis=-1, keepdims=True), 0.0)
    parts = []
    for l in range(len(LEVELS)):
        e = jnp.exp(-jnp.abs(ex[(2 + l) * CH:(3 + l) * CH, sl]))
        ql = (q * e).astype(MX)
        kl = (kk * e).astype(MX)
        a = a + jnp.where(masks[l], _dot(ql, kl, NT), 0.0)
        parts.append((e, ql, kl))
    return a, parts


def _mixer_fwd(h, taps, wbd, ps, gn, lb, mat, *, name):
    R = h.shape[0]
    nb = R // BM
    q4 = BM // HALO

    def body(h_ref, hp_ref, w_ref, wbd_ref, ps_ref, gn_ref, lb_ref, mat_ref, y_ref, o_ref, sp_ref, st_ref):
        i = pl.program_id(0)

        @pl.when(i == 0)
        def _():
            st_ref[...] = jnp.zeros(st_ref.shape, F32)

        row = _row_ids(i * BM, BM)
        valid_b = row >= PADR
        rowe = _row_ids(i * BM - HALO, BM + HALO)
        valide = rowe >= PADR
        se = jnp.concatenate([hp_ref[:, 256:512] * hp_ref[:, 512:768], h_ref[:, 256:512] * h_ref[:, 512:768]], axis=0)
        se = jnp.where(valide, se, 0.0)
        conv = (w_ref[0:1, :] * pltpu.roll(se, 2, 0) + w_ref[1:2, :] * pltpu.roll(se, 1, 0)
                + w_ref[2:3, :] * se)[HALO:]
        y_ref[:, 0:CW] = (h_ref[:, 0:256] * conv).astype(y_ref.dtype)
        ve = jnp.where(valide, jnp.concatenate([hp_ref[:, 2816:3072], h_ref[:, 2816:3072]], axis=0), 0.0)
        s2 = ve + pltpu.roll(ve, 1, 0)
        s4 = s2 + pltpu.roll(s2, 2, 0)
        s8 = s4 + pltpu.roll(s4, 4, 0)
        s16 = s8 + pltpu.roll(s8, 8, 0)
        dp = (_lane_select(s2, s4, s8, s16) / _pool_count(rowe) - ve)[HALO:]
        y_ref[:, CW + HW:D] = (_dot(dp, wbd_ref[...], NN) * ps_ref[...]).astype(y_ref.dtype)
        lbv = lb_ref[...]
        tt, ss, masks = _level_masks()
        for c in range(MB):
            rs = slice(c * CH, (c + 1) * CH)
            _, _, lf, kk = _gates(h_ref[rs, 1280:1792], lbv, valid_b[rs])
            q = h_ref[rs, 768:1280] * SCALE
            ii = h_ref[rs, 1792:2304]
            gz = h_ref[rs, 2304:2816]
            ex = _split_dot(mat_ref[...], lf)
            eg = jnp.exp(ex[0:CH])
            q0 = q * eg
            kr = kk * jnp.exp(ex[CH:2 * CH])
            for hd in range(NH):
                sl = slice(HD * hd, HD * (hd + 1))
                st = st_ref[hd]
                sp_ref[c, hd] = st
                a, _ = _intra(q[:, sl], kk[:, sl], ex, sl, tt, ss, masks)
                o = _dot(q0[:, sl], st, NT) + _dot(a, ii[:, sl], NN)
                st_ref[hd] = st * eg[CH - 1:CH, sl] + _dot(ii[:, sl], kr[:, sl], TN)
                o_ref[rs, sl] = o
                r = lax.rsqrt(jnp.mean(o * o, axis=-1, keepdims=True) + RMS_EPS)
                g_ = gz[:, sl]
                y_ref[rs, CW + HD * hd:CW + HD * (hd + 1)] = (
                    o * r * gn_ref[...] * (g_ * jax.nn.sigmoid(g_))).astype(y_ref.dtype)

    fix = lambda i: (0, 0)
    return pl.pallas_call(
        body, name=name, grid=(nb,),
        in_specs=[pl.BlockSpec((BM, DIN), lambda i: (i, 0)),
                  pl.BlockSpec((HALO, DIN), lambda i: (jnp.maximum(i * q4 - 1, 0), 0)),
                  pl.BlockSpec((SUBLANES, CW), fix), pl.BlockSpec((PW, PW), fix), pl.BlockSpec((1, PW), fix),
                  pl.BlockSpec((1, HD), fix), pl.BlockSpec((1, HW), fix), pl.BlockSpec(mat.shape, fix)],
        out_specs=[pl.BlockSpec((BM, D), lambda i: (i, 0)), pl.BlockSpec((BM, HW), lambda i: (i, 0)),
                   pl.BlockSpec((MB, NH, HD, HD), lambda i: (i, 0, 0, 0))],
        out_shape=[jax.ShapeDtypeStruct((R, D), MX), jax.ShapeDtypeStruct((R, HW), F32),
                   jax.ShapeDtypeStruct((R // CH, NH, HD, HD), F32)],
        scratch_shapes=[pltpu.VMEM((NH, HD, HD), F32)],
        compiler_params=_cp(("arbitrary",), 32 * 2 ** 20),
    )(h, h, taps, wbd, ps, gn, lb, mat)


def _mixer_bwd(h, dy, o, sp, taps, wbd, ps, gn, lb, mat, suf, *, dep=None, name):
    R = h.shape[0]
    nb = R // BM
    q4 = BM // HALO
    lasth = R // HALO - 1

    def body(h_ref, hp_ref, hn_ref, dy_ref, dyn_ref, o_ref, sp_ref, w_ref, wbd_ref, ps_ref, gn_ref, lb_ref,
             mat_ref, suf_ref, dh_ref, dw_ref, dsc_ref, dgn_ref, dlb_ref, dwbd_ref, dst_ref):
        i = pl.program_id(0)
        b = nb - 1 - i

        @pl.when(i == 0)
        def _():
            dst_ref[...] = jnp.zeros(dst_ref.shape, F32)
            dw_ref[...] = jnp.zeros(dw_ref.shape, F32)
            dsc_ref[...] = jnp.zeros(dsc_ref.shape, F32)
            dgn_ref[...] = jnp.zeros(dgn_ref.shape, F32)
            dlb_ref[...] = jnp.zeros(dlb_ref.shape, F32)
            dwbd_ref[...] = jnp.zeros(dwbd_ref.shape, F32)

        ne = BM + 2 * HALO
        nc = BM + HALO
        row = _row_ids(b * BM, BM)
        valid = row >= PADR
        rowe = _row_ids(b * BM - HALO, ne)
        valide = (rowe >= PADR) & (rowe < R)
        rown = rowe[HALO:]
        validn = rown < R

        def cat3(lo, hi):
            return jnp.concatenate([hp_ref[:, lo:hi], h_ref[:, lo:hi], hn_ref[:, lo:hi]], axis=0)

        def catd(lo, hi):
            return jnp.where(validn, jnp.concatenate([dy_ref[:, lo:hi], dyn_ref[:, lo:hi]], axis=0), 0.0)

        w0, w1, w2 = w_ref[0:1, :], w_ref[1:2, :], w_ref[2:3, :]
        cce = cat3(256, 512)
        cve = cat3(512, 768)
        se = jnp.where(valide, cce * cve, 0.0)
        sm2 = pltpu.roll(se, 2, 0)
        sm1 = pltpu.roll(se, 1, 0)
        conv = (w0 * sm2 + w1 * sm1 + w2 * se)[HALO:HALO + BM]
        cbn = jnp.concatenate([h_ref[:, 0:256], hn_ref[:, 0:256]], axis=0)
        dconv = catd(0, CW) * cbn
        ds = w2 * dconv + w1 * pltpu.roll(dconv, nc - 1, 0) + w0 * pltpu.roll(dconv, nc - 2, 0)
        ds = jnp.where(valid, ds[0:BM], 0.0)
        dcv = dconv[0:BM]
        dw_ref[0:1, :] += jnp.sum(dcv * sm2[HALO:HALO + BM], axis=0, keepdims=True)
        dw_ref[1:2, :] += jnp.sum(dcv * sm1[HALO:HALO + BM], axis=0, keepdims=True)
        dw_ref[2:3, :] += jnp.sum(dcv * se[HALO:HALO + BM], axis=0, keepdims=True)
        dh_ref[:, 0:256] = jnp.where(valid, dy_ref[:, 0:CW] * conv, 0.0).astype(dh_ref.dtype)
        dh_ref[:, 256:512] = (ds * h_ref[:, 512:768]).astype(dh_ref.dtype)
        dh_ref[:, 512:768] = (ds * h_ref[:, 256:512]).astype(dh_ref.dtype)

        ve = jnp.where(valide, cat3(2816, 3072), 0.0)
        s2 = ve + pltpu.roll(ve, 1, 0)
        s4 = s2 + pltpu.roll(s2, 2, 0)
        s8 = s4 + pltpu.roll(s4, 4, 0)
        s16 = s8 + pltpu.roll(s8, 8, 0)
        cnte = _pool_count(rowe)
        dp = (_lane_select(s2, s4, s8, s16) / cnte - ve)[HALO:HALO + BM]
        dyp = catd(CW + HW, D)
        pre = _dot(dp, wbd_ref[...], NN)
        dsc_ref[0:1, :] += jnp.sum(dyp[0:BM] * pre, axis=0, keepdims=True)
        dyps = dyp * ps_ref[...]
        dd = _dot(dyps, wbd_ref[...], NT)
        dwbd_ref[...] += _dot(dp, dyps[0:BM], TN)
        e = dd / cnte[HALO:]
        t2 = e + pltpu.roll(e, nc - 1, 0)
        t4 = t2 + pltpu.roll(t2, nc - 2, 0)
        t8 = t4 + pltpu.roll(t4, nc - 4, 0)
        t16 = t8 + pltpu.roll(t8, nc - 8, 0)
        dv = (_lane_select(t2, t4, t8, t16) - dd)[0:BM]
        dh_ref[:, 2816:3072] = jnp.where(valid, dv, 0.0).astype(dh_ref.dtype)

        lbv = lb_ref[...]
        tt, ss, masks = _level_masks()
        gnv = gn_ref[...]
        for c in reversed(range(MB)):
            rs = slice(c * CH, (c + 1) * CH)
            vc = valid[rs]
            sig, f, lf, kk = _gates(h_ref[rs, 1280:1792], lbv, vc)
            q = h_ref[rs, 768:1280] * SCALE
            ii = h_ref[rs, 1792:2304]
            gz = h_ref[rs, 2304:2816]
            ex = _split_dot(mat_ref[...], lf)
            eg = jnp.exp(ex[0:CH])
            egr = jnp.exp(ex[CH:2 * CH])
            q0 = q * eg
            kr = kk * egr
            dqs, dks, dis, dgzs, tails, dbs = [], [], [], [], [], []
            for hd in range(NH):
                sl = slice(HD * hd, HD * (hd + 1))
                ov = o_ref[rs, sl]
                r = lax.rsqrt(jnp.mean(ov * ov, axis=-1, keepdims=True) + RMS_EPS)
                oh = ov * r
                g_ = gz[:, sl]
                sg = jax.nn.sigmoid(g_)
                dyv = dy_ref[rs, CW + HD * hd:CW + HD * (hd + 1)]
                don = dyv * (g_ * sg)
                dgzs.append(dyv * (oh * gnv) * (sg * (1.0 + g_ * (1.0 - sg))))
                dgn_ref[0:1, :] += jnp.sum(don * oh, axis=0, keepdims=True)
                doh = don * gnv
                do = r * (doh - oh * jnp.mean(doh * oh, axis=-1, keepdims=True))
                st = sp_ref[c, hd]
                dst = dst_ref[hd]
                qh, kh, ih = q[:, sl], kk[:, sl], ii[:, sl]
                a, parts = _intra(qh, kh, ex, sl, tt, ss, masks)
                da = jnp.where(tt >= ss, _dot(do, ih, NT), 0.0)
                dis.append(_dot(a, do, TN) + _dot(kr[:, sl], dst, NT))
                q0h = q0[:, sl].astype(MX)
                krh = kr[:, sl].astype(MX)
                dq0 = _dot(do, st, NN)
                dkr = _dot(ih, dst, NN)
                dq = dq0 * eg[:, sl]
                dk = dkr * egr[:, sl]
                kdk = krh.astype(F32) * dkr
                db = q0h.astype(F32) * dq0 - kdk
                tails.append(jnp.sum(kdk, axis=0, keepdims=True)
                             + eg[CH - 1:CH, sl] * jnp.sum(dst * st, axis=0, keepdims=True))
                dga = jnp.sum(jnp.where(tt == ss, da, 0.0), axis=-1, keepdims=True)
                dq = dq + dga * kh
                dk = dk + dga * qh
                for l in range(len(LEVELS)):
                    e_l, ql, kl = parts[l]
                    dpl = jnp.where(masks[l], da, 0.0).astype(MX)
                    dql = _dot(dpl, kl, NN)
                    dkl = _dot(dpl, ql, TN)
                    dq = dq + dql * e_l
                    dk = dk + dkl * e_l
                    db = db + (ql.astype(F32) * dql - kl.astype(F32) * dkl)
                dst_ref[hd] = dst * eg[CH - 1:CH, sl] + _dot(do, q0h, TN)
                dqs.append(dq)
                dks.append(dk)
                dbs.append(db)
            dq = jnp.concatenate(dqs, axis=1)
            dk = jnp.concatenate(dks, axis=1)
            db = jnp.concatenate(dbs, axis=1)
            dlf = _split_dot(suf_ref[...], db) + jnp.concatenate(tails, axis=1)
            t = jnp.where(vc, dlf * jnp.where(f > F_FLOOR, 1.0 / f, 0.0) - dk, 0.0)
            dlb_ref[0:1, :] += jnp.sum(t * (1.0 - sig), axis=0, keepdims=True)
            dh_ref[rs, 768:1280] = jnp.where(vc, dq * SCALE, 0.0).astype(dh_ref.dtype)
            dh_ref[rs, 1280:1792] = (t * (1.0 - lbv) * (sig * (1.0 - sig))).astype(dh_ref.dtype)
            dh_ref[rs, 1792:2304] = jnp.where(vc, jnp.concatenate(dis, axis=1), 0.0).astype(dh_ref.dtype)
            dh_ref[rs, 2304:2816] = jnp.where(vc, jnp.concatenate(dgzs, axis=1), 0.0).astype(dh_ref.dtype)

    cur = lambda i: (nb - 1 - i, 0)
    prev = lambda i: (jnp.maximum((nb - 1 - i) * q4 - 1, 0), 0)
    nxt = lambda i: (jnp.minimum((nb - 1 - i) * q4 + q4, lasth), 0)
    fix = lambda i: (0, 0)
    body, in_specs, args = _after(
        dep, body,
        [pl.BlockSpec((BM, DIN), cur), pl.BlockSpec((HALO, DIN), prev), pl.BlockSpec((HALO, DIN), nxt),
         pl.BlockSpec((BM, D), cur), pl.BlockSpec((HALO, D), nxt), pl.BlockSpec((BM, HW), cur),
         pl.BlockSpec((MB, NH, HD, HD), lambda i: (nb - 1 - i, 0, 0, 0)),
         pl.BlockSpec((SUBLANES, CW), fix), pl.BlockSpec((PW, PW), fix), pl.BlockSpec((1, PW), fix),
         pl.BlockSpec((1, HD), fix), pl.BlockSpec((1, HW), fix), pl.BlockSpec(mat.shape, fix),
         pl.BlockSpec(suf.shape, fix)],
        [h, h, h, dy, dy, o, sp, taps, wbd, ps, gn, lb, mat, suf])
    return pl.pallas_call(
        body, name=name, grid=(nb,), in_specs=in_specs,
        out_specs=[pl.BlockSpec((BM, DIN), cur), pl.BlockSpec((SUBLANES, CW), fix), pl.BlockSpec((SUBLANES, PW), fix),
                   pl.BlockSpec((SUBLANES, HD), fix), pl.BlockSpec((SUBLANES, HW), fix), pl.BlockSpec((PW, PW), fix)],
        out_shape=[jax.ShapeDtypeStruct((R, DIN), MX), jax.ShapeDtypeStruct((SUBLANES, CW), F32),
                   jax.ShapeDtypeStruct((SUBLANES, PW), F32), jax.ShapeDtypeStruct((SUBLANES, HD), F32),
                   jax.ShapeDtypeStruct((SUBLANES, HW), F32), jax.ShapeDtypeStruct((PW, PW), F32)],
        scratch_shapes=[pltpu.VMEM((NH, HD, HD), F32)],
        compiler_params=_cp(("arbitrary",), 40 * 2 ** 20),
    )(*args)


def _sum_slots(recv, *, name):
    S, L, rows, cols = recv.shape
    tr = _row_tile(rows, S * cols * 4, 6 * 2 ** 20)

    def body(r_ref, o_ref):
        acc = r_ref[0]
        for s in range(1, S):
            acc = acc + r_ref[s]
        o_ref[...] = acc

    return pl.pallas_call(
        body, name=name, grid=(L, rows // tr),
        in_specs=[pl.BlockSpec((S, None, tr, cols), lambda l, i: (0, l, i, 0))],
        out_specs=pl.BlockSpec((None, tr, cols), lambda l, i: (l, i, 0)),
        out_shape=jax.ShapeDtypeStruct((L, rows, cols), F32),
        compiler_params=_cp(("parallel", "parallel"), 4 * S * tr * cols * 4),
    )(recv)


def _sum_own(recv, own, chip, *, name):
    S, rows, cols = recv.shape
    tr = _row_tile(rows, S * cols * 4, 6 * 2 ** 20)

    def body(me_ref, r_ref, o_ref, out_ref):
        me = me_ref[0]
        acc = None
        for s in range(S):
            t = jnp.where(me == s, o_ref[...], r_ref[s])
            acc = t if acc is None else acc + t
        out_ref[...] = acc

    grid_spec = pltpu.PrefetchScalarGridSpec(
        num_scalar_prefetch=1, grid=(rows // tr,),
        in_specs=[pl.BlockSpec((S, tr, cols), lambda i, me: (0, i, 0)),
                  pl.BlockSpec((None, tr, cols), lambda i, me: (me[0], i, 0))],
        out_specs=pl.BlockSpec((tr, cols), lambda i, me: (i, 0)))
    return pl.pallas_call(
        body, name=name, grid_spec=grid_spec, out_shape=jax.ShapeDtypeStruct((rows, cols), F32),
        compiler_params=_cp(("parallel",), 5 * S * tr * cols * 4),
    )(chip.reshape(1).astype(jnp.int32), recv, own)


def _adamw(w, m, v, ga, gb, *, layer, prev, name):
    L, rows, cols = w.shape
    tr = _row_tile(rows, cols * 4, 2 ** 20)
    two = gb is not None
    nin = 5 if two else 4

    def body(*refs):
        w_ref, m_ref, v_ref, a_ref = refs[:4]
        g_ref, d_ref, m2_ref, v2_ref = refs[-4:]
        g = a_ref[...] + refs[4][...] if two else a_ref[...]
        m2 = ADAM_B1 * m_ref[...] + (1.0 - ADAM_B1) * g
        v2 = ADAM_B2 * v_ref[...] + (1.0 - ADAM_B2) * (g * g)
        m_hat = m2 / (1.0 - ADAM_B1 ** ADAM_STEP)
        v_hat = v2 / (1.0 - ADAM_B2 ** ADAM_STEP)
        g_ref[...] = g
        d_ref[...] = -ADAM_LR * (m_hat / (jnp.sqrt(v_hat) + ADAM_EPS) + ADAM_WD * w_ref[...])
        m2_ref[...] = m2
        v2_ref[...] = v2

    spec = pl.BlockSpec((None, tr, cols), lambda i: (layer, i, 0))
    gspec = pl.BlockSpec((tr, cols), lambda i: (i, 0))
    args = [w, m, v, ga] + ([gb] if two else [])
    in_specs = [spec] * 3 + [gspec] * (nin - 3)
    aliases = {}
    if prev is not None:
        args += list(prev)
        in_specs += [ANY] * 4
        aliases = {nin + j: j for j in range(4)}
    sd = jax.ShapeDtypeStruct((L, rows, cols), F32)
    return pl.pallas_call(
        body, name=name, grid=(rows // tr,), in_specs=in_specs, out_specs=[spec] * 4,
        out_shape=[sd] * 4, input_output_aliases=aliases,
        compiler_params=_cp(("parallel",), 24 * tr * cols * 4),
    )(*args)


def _exchange(arrays, *, flips, n_slots, slot_of, scatter, self_copy, dep=None, name):
    n = len(arrays)
    nf = len(flips)
    deps = [] if dep is None else [dep]
    nd = len(deps)
    out_shapes = [jax.ShapeDtypeStruct(a.shape if scatter else (n_slots,) + a.shape, a.dtype) for a in arrays]

    def body(*refs):
        ins, outs = refs[:n], refs[n + nd:2 * n + nd]
        send_sems, recv_sems, loc_sems = refs[2 * n + nd:]
        x, y, c = lax.axis_index("x"), lax.axis_index("y"), lax.axis_index("c")
        me = slot_of(x, y, c)
        peers = [(1 - x if fx else x, 1 - y if fy else y, 1 - c if fc else c) for fx, fy, fc in flips]
        local, remote = [], []
        for a in range(n):
            if self_copy:
                lc = pltpu.make_async_copy(ins[a].at[me] if scatter else ins[a], outs[a].at[me], loc_sems.at[a])
                lc.start()
                local.append(lc)
            for k, p in enumerate(peers):
                src = ins[a].at[slot_of(*p)] if scatter else ins[a]
                cp = pltpu.make_async_remote_copy(
                    src_ref=src, dst_ref=outs[a].at[me], send_sem=send_sems.at[a, k], recv_sem=recv_sems.at[a, k],
                    device_id=p, device_id_type=MESH)
                cp.start()
                remote.append(cp)
        for a in range(n):
            for k, p in enumerate(peers):
                src = ins[a].at[slot_of(*p)] if scatter else ins[a]
                pltpu.make_async_remote_copy(
                    src_ref=src, dst_ref=outs[a].at[slot_of(*p)], send_sem=send_sems.at[a, k],
                    recv_sem=recv_sems.at[a, k], device_id=p, device_id_type=MESH).wait_recv()
        for cp in remote:
            cp.wait_send()
        for lc in local:
            lc.wait()

    return pl.pallas_call(
        body, name=name, in_specs=[ANY] * (n + nd), out_specs=[ANY] * n, out_shape=out_shapes,
        scratch_shapes=[pltpu.SemaphoreType.DMA((n, nf)), pltpu.SemaphoreType.DMA((n, nf)),
                        pltpu.SemaphoreType.DMA((n,))],
        compiler_params=pltpu.CompilerParams(has_side_effects=True),
    )(*arrays, *deps)


ALL_FLIPS = [(fx, fy, fc) for fx in (0, 1) for fy in (0, 1) for fc in (0, 1) if fx or fy or fc]


def _zero_slot(x, y, c):
    return 0


HBM_SPEC = pl.BlockSpec(memory_space=pltpu.HBM)
SEM_SPEC = pl.BlockSpec(memory_space=pltpu.SEMAPHORE)


N_SLOTS = dict(chips=4, sibling=1, all=8)


def _peers(to, x, y, c):
    if to == "sibling":
        return [(x, y, 1 - c)]
    if to == "chips":
        return [(1 - x, y, c), (x, 1 - y, c), (1 - x, 1 - y, c)]
    return [(1 - x if fx else x, 1 - y if fy else y, 1 - c if fc else c) for fx, fy, fc in ALL_FLIPS]


def _slot(to, x, y, c):
    return {"chips": 2 * x + y, "sibling": 0, "all": 4 * x + 2 * y + c}[to]


def _send_start(srcs, *, scatter, to="chips", dep=None, name):
    n = len(srcs)
    npeer = N_SLOTS[to] - (to != "sibling")
    nc = n * npeer
    srcs = [pltpu.with_memory_space_constraint(s, pltpu.HBM) for s in srcs]
    land_shapes = [s.shape if scatter else (N_SLOTS[to],) + s.shape for s in srcs]
    lands = [pltpu.with_memory_space_constraint(lax.empty(sh, s.dtype), pltpu.HBM) for sh, s in zip(land_shapes, srcs)]

    deps = [] if dep is None else [dep]
    nd = len(deps)

    def body(*refs):
        ins, lnd = refs[:n], refs[n:2 * n]
        send_sems, recv_sems = refs[2 * n + nd:2 * n + nd + nc], refs[2 * n + nd + nc:2 * n + nd + 2 * nc]
        token = refs[-1]
        x, y, c = lax.axis_index("x"), lax.axis_index("y"), lax.axis_index("c")
        me = _slot(to, x, y, c)
        for a in range(n):
            for k, p in enumerate(_peers(to, x, y, c)):
                src = ins[a].at[_slot(to, *p)] if scatter else ins[a]
                j = a * npeer + k
                pltpu.make_async_remote_copy(
                    src_ref=src, dst_ref=lnd[a].at[me], send_sem=send_sems[j], recv_sem=recv_sems[j],
                    device_id=p, device_id_type=MESH).start()
        token[...] = jnp.zeros(token.shape, token.dtype)

    sem = pltpu.SemaphoreType.DMA(())
    outs = pl.pallas_call(
        body, name=name,
        out_shape=(*[sem] * (2 * nc), *[pltpu.HBM(s.shape, s.dtype) for s in srcs],
                   *[pltpu.HBM(sh, s.dtype) for sh, s in zip(land_shapes, srcs)],
                   jax.ShapeDtypeStruct((SUBLANES, LANES), F32)),
        in_specs=[HBM_SPEC] * (2 * n) + [ANY] * nd,
        out_specs=(*[SEM_SPEC] * (2 * nc), *[HBM_SPEC] * (2 * n), pl.BlockSpec(memory_space=pltpu.VMEM)),
        input_output_aliases={i: 2 * nc + i for i in range(2 * n)},
        compiler_params=pltpu.CompilerParams(has_side_effects=pltpu.SideEffectType.DATAFLOW_SIDE_EFFECTING),
    )(*srcs, *lands, *deps)
    return dict(sems=list(outs[:2 * nc]), srcs=list(outs[2 * nc:2 * nc + n]),
                lands=list(outs[2 * nc + n:2 * nc + 2 * n]), token=outs[-1], to=to)


def _send_wait(h, *, scatter, after, name):
    n = len(h["srcs"])
    to = h["to"]
    nc = len(h["sems"]) // 2

    def body(*refs):
        ins, lnd = refs[:n], refs[n:2 * n]
        send_sems, recv_sems = refs[2 * n:2 * n + nc], refs[2 * n + nc:2 * n + 2 * nc]
        x, y, c = lax.axis_index("x"), lax.axis_index("y"), lax.axis_index("c")
        for a in range(n):
            for k, p in enumerate(_peers(to, x, y, c)):
                slot = _slot(to, *p)
                j = a * (nc // n) + k
                cp = pltpu.make_async_remote_copy(
                    src_ref=ins[a].at[slot] if scatter else ins[a], dst_ref=lnd[a].at[slot],
                    send_sem=send_sems[j], recv_sem=recv_sems[j], device_id=p, device_id_type=MESH)
                cp.wait_send()
                cp.wait_recv()

    thru = h["srcs"] + h["lands"]
    outs = pl.pallas_call(
        body, name=name, out_shape=tuple(pltpu.HBM(t.shape, t.dtype) for t in thru),
        in_specs=[HBM_SPEC] * (2 * n) + [SEM_SPEC] * (2 * nc) + [ANY] * len(after),
        out_specs=tuple([HBM_SPEC] * (2 * n)),
        input_output_aliases={i: i for i in range(2 * n)},
        compiler_params=pltpu.CompilerParams(has_side_effects=pltpu.SideEffectType.DATAFLOW_SIDE_EFFECTING),
    )(*thru, *h["sems"], *after)
    return list(outs[:n]), list(outs[n:])


def _assemble(land, own, chip, axis):
    return jnp.concatenate([jnp.where(chip == k, own, land[k]) for k in range(4)], axis=axis)


def _pack(arrs):
    flat = jnp.concatenate([a.reshape(-1).astype(F32) for a in arrs])
    tile = SUBLANES * LANES
    pad = (-flat.shape[0]) % tile
    return jnp.pad(flat, (0, pad)).reshape(-1, LANES)


def _unpack(buf, shapes):
    flat = buf.reshape(-1)
    out, off = [], 0
    for s in shapes:
        n = int(np.prod(s))
        out.append(flat[off:off + n].reshape(s))
        off += n
    return out


def _lower_bounds(hg_lower_bounds):
    p = jax.nn.softmax(hg_lower_bounds.astype(F32), axis=0)
    return jnp.cumsum(p, axis=0) - p[0]


def kernel(x, meta_tokens, hg_lower_bounds, w_in, w_conv, w_pool, pool_scale, hg_norm_g, w_o, ln1_g, ln1_b, w_up, w_ffn_conv, b_ffn_conv, w_down, ln2_g, ln2_b, loss_target, m_meta_tokens, m_hg_lower_bounds, m_w_in, m_w_conv, m_w_pool, m_pool_scale, m_hg_norm_g, m_w_o, m_ln1_g, m_ln1_b, m_w_up, m_w_ffn_conv, m_b_ffn_conv, m_w_down, m_ln2_g, m_ln2_b, v_meta_tokens, v_hg_lower_bounds, v_w_in, v_w_conv, v_w_pool, v_pool_scale, v_hg_norm_g, v_w_o, v_ln1_g, v_ln1_b, v_w_up, v_w_ffn_conv, v_b_ffn_conv, v_w_down, v_ln2_g, v_ln2_b):
    S = x.shape[1]
    R = S + ROW0
    Fq = w_down.shape[1]
    F = 4 * Fq
    F2 = 2 * F
    F2q = w_up.shape[2]
    assert x.shape == (1, S, D) and R % 384 == 0 and S % ROW0 == 0
    chip = 2 * lax.axis_index("x") + lax.axis_index("y")
    tm = 384
    tm_w = max(t for t in range(SUBLANES, 2113, SUBLANES) if R % t == 0)
    tb_ffn = 256

    small_shapes = [(N_META, D // 4), (DEPTH, CW // 4, 3), (DEPTH, F2q, 3)]
    wb_in, wb_o, wb_up, wb_down = (w.astype(MX) for w in (w_in, w_o, w_up, w_down))
    h_s = _send_start([_pack([meta_tokens, w_conv, w_ffn_conv])], scatter=False, name="gather_small_start")
    h_a = _send_start([wb_in[0]], scatter=False, dep=h_s["token"], name="gather_a_start")
    (own_small,), (l_small,) = _send_wait(h_s, scatter=False, after=[h_a["token"]], name="gather_small_wait")
    sm = [_unpack(jnp.where(chip == k, own_small, l_small[k]), small_shapes) for k in range(4)]
    meta_full = jnp.concatenate([sm[k][0] for k in range(4)], axis=1)
    X = jnp.concatenate([jnp.zeros((PADR, D), F32), meta_full, x[0]], axis=0)
    (own_in,), (l_in,) = _send_wait(h_a, scatter=False, after=[X, wb_o, wb_up, wb_down], name="gather_a_wait")
    h_b = _send_start([wb_o[0], wb_up[0], wb_down[0]], scatter=False, dep=l_in, name="gather_b_start")
    Win, Wo, Wup, Wdown = {}, {}, {}, {}
    Win[0] = _assemble(l_in, own_in, chip, 1)
    wconv_full = jnp.concatenate([sm[k][1] for k in range(4)], axis=1)
    wffn_full = jnp.concatenate([sm[k][2] for k in range(4)], axis=1)
    taps_c = jnp.pad(wconv_full.transpose(0, 2, 1), ((0, 0), (0, SUBLANES - 3), (0, 0)))
    taps_f = jnp.pad(wffn_full.transpose(0, 2, 1), ((0, 0), (0, SUBLANES - 3), (0, 0)))
    wbd = jnp.stack([jax.scipy.linalg.block_diag(*[w_pool[l, g] for g in range(4)]) for l in range(DEPTH)]).astype(MX)
    lbs, lbs_vjp = jax.vjp(_lower_bounds, hg_lower_bounds)
    mat, suf = _hg_consts()

    def fwd_mixer(l, X, dep=None):
        h = _mm(X, Win[l], tm=tm, dep=dep, name=f"mm_in_{l}")
        y, o, sp = _mixer_fwd(h, taps_c[l], wbd[l], pool_scale[l].reshape(1, PW), hg_norm_g[l].reshape(1, HD),
                              lbs[l].reshape(1, HW), mat, name=f"mixer_fwd_{l}")
        return h, y, o, sp

    def fwd_rest(l, X, h, y, o, sp, dep=None):
        x1, xh1, r1 = _mm_ln(y, Wo[l], X, ln1_g[l], ln1_b[l], tm=tm, dep=dep, name=f"mm_o_ln_{l}")
        up, u, a = _mm_ffn_fwd(x1, Wup[l], taps_f[l], b_ffn_conv[l].reshape(1, F2), tm=tm // 2,
                               name=f"mm_up_ffn_{l}")
        x2, xh2, r2 = _mm_ln(a, Wdown[l], x1, ln2_g[l], ln2_b[l], tm=tm, name=f"mm_down_ln_{l}")
        return (X, h, y, o, sp, x1, xh1, r1, up, u, a, xh2, r2), x2

    h, y, o, sp = fwd_mixer(0, X, dep=h_b["token"])
    (own_o, own_up, own_down), (l_o, l_up, l_down) = _send_wait(h_b, scatter=False, after=[y], name="gather_b_wait")
    Wo[0], Wup[0], Wdown[0] = (_assemble(l_o, own_o, chip, 0), _assemble(l_up, own_up, chip, 1),
                               _assemble(l_down, own_down, chip, 0))
    h_c = _send_start([wb_in[1], wb_o[1], wb_up[1], wb_down[1]], scatter=False, dep=l_o, name="gather_c_start")
    saved0, X1 = fwd_rest(0, X, h, y, o, sp, dep=h_c["token"])
    own_c, l_c = _send_wait(h_c, scatter=False, after=[X1], name="gather_c_wait")
    Win[1], Wo[1] = _assemble(l_c[0], own_c[0], chip, 1), _assemble(l_c[1], own_c[1], chip, 0)
    Wup[1], Wdown[1] = _assemble(l_c[2], own_c[2], chip, 1), _assemble(l_c[3], own_c[3], chip, 0)
    saved1, X2 = fwd_rest(1, X1, *fwd_mixer(1, X1))
    saved = [saved0, saved1]

    dxo = X2

    sc = {}

    def scatter(nm, l, g, dep=None):
        sc[nm, l] = _send_start([g], scatter=True, dep=dep, name=f"scatter_{nm}_{l}_start")
        return sc[nm, l]["token"]

    tok = None

    small_g = [None] * DEPTH
    for l in reversed(range(DEPTH)):
        X, h, y, o, sp, x1, xh1, r1, up, u, a, xh2, r2 = saved[l]
        if l == DEPTH - 1:
            dz2, gb2, da, sq = _ln_bwd_mm(dxo, xh2, r2, ln2_g[l], Wdown[l], tm=ROW0, tgt=loss_target[0],
                                          name=f"loss_ln2_bwd_da_{l}")
            loss = lax.psum(0.5 * jnp.sum(sq[0]) / D, ("x", "y", "c"))
        else:
            dz2, gb2, da = _ln_bwd_mm(dxo, xh2, r2, ln2_g[l], Wdown[l], tm=tm, dep=tok, name=f"ln2_bwd_da_{l}")
        tok = scatter("w_down", l, _wgrad(a, dz2, slabs_on_cols=False, tm=tm_w, tn=512, name=f"wgrad_down_{l}"))
        dup, facc = _ffn_bwd(da, up, u, taps_f[l], tb=tb_ffn, dep=tok, name=f"ffn_bwd_{l}")
        dx1 = _mm(dup, Wup[l], nt=True, res=dz2, tm=tm, name=f"mm_dx1_{l}")
        tok = scatter("w_up", l, _wgrad(x1, dup, slabs_on_cols=True, tm=tm_w, tn=F2q, name=f"wgrad_up_{l}"))
        dz1, gb1, dym = _ln_bwd_mm(dx1, xh1, r1, ln1_g[l], Wo[l], tm=tm, dep=tok, name=f"ln1_bwd_dym_{l}")
        tok = scatter("w_o", l, _wgrad(y, dz1, slabs_on_cols=False, tm=tm_w, tn=512, name=f"wgrad_o_{l}"))
        dh, dwc, dsc, dgn, dlb, dwbd = _mixer_bwd(
            h, dym, o, sp, taps_c[l], wbd[l], pool_scale[l].reshape(1, PW), hg_norm_g[l].reshape(1, HD),
            lbs[l].reshape(1, HW), mat, suf, dep=tok, name=f"mixer_bwd_{l}")
        tok = scatter("w_in", l, _wgrad(X, dh, slabs_on_cols=True, tm=tm_w, tn=DIN // 4, name=f"wgrad_in_{l}"))
        if l == 0:
            dx_head, dx_seq = _mm_dx_head(dh, Win[l], dz1, dep=tok, name=f"mm_dx_{l}")
        else:
            dxo = _mm(dh, Win[l], nt=True, res=dz1, tm=tm, dep=tok, name=f"mm_dx_{l}")
        small_g[l] = dict(
            lbs=dlb[0], w_conv=dwc[0:3].T, w_pool=jnp.stack([dwbd[64 * g:64 * g + 64, 64 * g:64 * g + 64] for g in range(4)]),
            pool_scale=dsc[0], hg_norm_g=dgn[0], ln1_g=gb1[0], ln1_b=gb1[1], w_ffn_conv=facc[0:3].T,
            b_ffn_conv=facc[3], ln2_g=gb2[0], ln2_b=gb2[1])
    grad_x = dx_seq[None]

    sg_names = ["lbs", "w_conv", "w_pool", "pool_scale", "hg_norm_g", "ln1_g", "ln1_b", "w_ffn_conv",
                "b_ffn_conv", "ln2_g", "ln2_b"]
    sg_list = [dx_head[PADR:ROW0]] + [jnp.stack([small_g[l][nm] for l in range(DEPTH)]) for nm in sg_names]
    sg_shapes = [a.shape for a in sg_list]
    packed = _pack(sg_list)
    h_g = _send_start([packed], scatter=False, to="all", name="gather_small_grads_start")

    big_w = dict(w_in=(w_in, m_w_in, v_w_in), w_o=(w_o, m_w_o, v_w_o), w_up=(w_up, m_w_up, v_w_up),
                 w_down=(w_down, m_w_down, v_w_down))
    groups = {1: ("w_down", "w_up", "w_o", "w_in"), 0: ("w_down", "w_up", "w_o")}
    swaps, tokc = {}, h_g["token"]
    for l in reversed(range(DEPTH)):
        part = []
        for nm in groups[l]:
            after = [dx_seq, tokc]
            (own,), (recv,) = _send_wait(sc[nm, l], scatter=True, after=after, name=f"scatter_{nm}_{l}_wait")
            part.append(_sum_own(recv, own, chip, name=f"sum_{nm}_{l}"))
        swaps[l] = _send_start(part, scatter=False, to="sibling", name=f"swap_cores_{l}_start")
        tokc = swaps[l]["token"]

    (own_small_g,), (l_small_g,) = _send_wait(h_g, scatter=False, after=[tokc], name="gather_small_grads_wait")
    dev = 2 * chip + lax.axis_index("c")
    gathered = jnp.where((jnp.arange(8) == dev)[:, None, None], own_small_g[None], l_small_g)
    total = _sum_slots(gathered[:, None], name="sum_small_grads")[0]
    tot = dict(zip(["meta_tokens"] + sg_names, _unpack(total, sg_shapes)))
    (g_hg,) = lbs_vjp(tot["lbs"])
    small_grads = dict(
        meta_tokens=lax.dynamic_slice_in_dim(tot["meta_tokens"], chip * (D // 4), D // 4, axis=1),
        hg_lower_bounds=g_hg,
        w_conv=lax.dynamic_slice_in_dim(tot["w_conv"], chip * (CW // 4), CW // 4, axis=1),
        w_pool=tot["w_pool"], pool_scale=tot["pool_scale"], hg_norm_g=tot["hg_norm_g"],
        ln1_g=tot["ln1_g"], ln1_b=tot["ln1_b"],
        w_ffn_conv=lax.dynamic_slice_in_dim(tot["w_ffn_conv"], chip * F2q, F2q, axis=1),
        b_ffn_conv=tot["b_ffn_conv"], ln2_g=tot["ln2_g"], ln2_b=tot["ln2_b"])
    small_w = dict(meta_tokens=(meta_tokens, m_meta_tokens, v_meta_tokens),
                   hg_lower_bounds=(hg_lower_bounds, m_hg_lower_bounds, v_hg_lower_bounds),
                   w_conv=(w_conv, m_w_conv, v_w_conv), w_pool=(w_pool, m_w_pool, v_w_pool),
                   pool_scale=(pool_scale, m_pool_scale, v_pool_scale), hg_norm_g=(hg_norm_g, m_hg_norm_g, v_hg_norm_g),
                   ln1_g=(ln1_g, m_ln1_g, v_ln1_g), ln1_b=(ln1_b, m_ln1_b, v_ln1_b),
                   w_ffn_conv=(w_ffn_conv, m_w_ffn_conv, v_w_ffn_conv), b_ffn_conv=(b_ffn_conv, m_b_ffn_conv, v_b_ffn_conv),
                   ln2_g=(ln2_g, m_ln2_g, v_ln2_g), ln2_b=(ln2_b, m_ln2_b, v_ln2_b))
    names_s = list(small_w)
    shapes_s = [small_w[nm][0].shape for nm in names_s]
    pk = [_pack([small_w[nm][j] for nm in names_s])[None] for j in range(3)]
    pg = _pack([small_grads[nm] for nm in names_s])[None]
    outs_s = _adamw(pk[0], pk[1], pk[2], pg[0], None, layer=0, prev=None, name="adamw_small")
    small = {nm: [] for nm in names_s}
    for j in range(4):
        for nm, val in zip(names_s, _unpack(outs_s[j][0], shapes_s)):
            small[nm].append(val)

    big = {nm: None for nm in big_w}
    after = [outs_s[0]]
    for l in reversed(range(DEPTH)):
        mine, theirs = _send_wait(swaps[l], scatter=False, after=after, name=f"swap_cores_{l}_wait")
        for k, nm in enumerate(groups[l]):
            w, m, v = big_w[nm]
            big[nm] = _adamw(w, m, v, mine[k], theirs[k][0], layer=l, prev=big[nm], name=f"adamw_{nm}_{l}")
        after = [big[nm][0] for nm in groups[l]]
    (own,), (recv,) = _send_wait(sc["w_in", 0], scatter=True, after=after, name="scatter_w_in_0_wait")
    part = _sum_own(recv, own, chip, name="sum_w_in_0")
    (sib,) = _exchange([part], flips=[(0, 0, 1)], n_slots=1, slot_of=_zero_slot, scatter=False, self_copy=False,
                       name="swap_cores_0_in")
    big["w_in"] = _adamw(*big_w["w_in"], part, sib[0], layer=0, prev=big["w_in"], name="adamw_w_in_0")

    order = ["meta_tokens", "hg_lower_bounds", "w_in", "w_conv", "w_pool", "pool_scale", "hg_norm_g", "w_o",
             "ln1_g", "ln1_b", "w_up", "w_ffn_conv", "b_ffn_conv", "w_down", "ln2_g", "ln2_b"]
    res = {nm: (big[nm] if nm in big else small[nm]) for nm in order}
    outs = [loss, grad_x]
    for j in range(4):
        outs += [res[nm][j] for nm in order]
    return tuple(outs)
```

```python
import numpy as np

import jax
import jax.numpy as jnp
from jax import lax
from jax.experimental import pallas as pl
from jax.experimental.pallas import tpu as pltpu

F32 = jnp.float32
BF = jnp.bfloat16
MX = jnp.bfloat16

D = 1024
CW = 256
HW = 512
HD = 128
NH = 4
PW = 256
DIN = 3072
N_META = 16
CH = 64
MB = 4
BM = MB * CH
ROW0 = 256
PADR = ROW0 - N_META
HALO = 16
FH = 8
LEVELS = (32, 16, 8, 4, 2, 1)
DEPTH = 2
ALPHA = (2 * DEPTH) ** 0.25
LN_EPS = 1e-5
RMS_EPS = 1e-6
F_FLOOR = 1e-30
SCALE = HD ** -0.5
ADAM_LR, ADAM_B1, ADAM_B2, ADAM_EPS, ADAM_WD, ADAM_STEP = 0.001, 0.9, 0.999, 1e-08, 0.01, 10

VMEM_V7X = 64 * 2 ** 20
LANES = 128
SUBLANES = 8

NN = (((1,), (0,)), ((), ()))
NT = (((1,), (1,)), ((), ()))
TN = (((0,), (0,)), ((), ()))
MESH = pl.DeviceIdType.MESH
ANY = pl.BlockSpec(memory_space=pl.ANY)


def _dot(a, b, dn):
    return lax.dot_general(a.astype(MX), b.astype(MX), dn, preferred_element_type=F32)


def _cp(sem, est_bytes):
    lim = int(min(VMEM_V7X - 6 * 2 ** 20, max(32 * 2 ** 20, est_bytes)))
    return pltpu.CompilerParams(dimension_semantics=sem, vmem_limit_bytes=lim)


def _nbytes(shape, dtype):
    return int(np.prod(shape)) * jnp.dtype(dtype).itemsize


def _row_tile(rows, row_bytes, budget):
    best = SUBLANES
    for t in range(SUBLANES, rows + 1, SUBLANES):
        if rows % t == 0 and t * row_bytes <= budget:
            best = t
    return best


def _after(dep, body, in_specs, args):
    if dep is None:
        return body, list(in_specs), list(args)

    def body_after(dep_ref, *refs):
        body(*refs)

    return body_after, [ANY] + list(in_specs), [dep] + list(args)


def _mm(a, w, *, nt=False, res=None, tm, out_dtype=F32, zero_inert=False, dep=None, name):
    R, K = a.shape
    N = w.shape[0] if nt else w.shape[1]
    dn = NT if nt else NN

    def body(*refs):
        if res is None:
            a_ref, w_ref, o_ref = refs
        else:
            a_ref, w_ref, r_ref, o_ref = refs
        acc = _dot(a_ref[...], w_ref[...], dn)
        if res is not None:
            acc = acc + ALPHA * r_ref[...]
        if zero_inert:
            acc = jnp.where(_row_ids(pl.program_id(0) * tm, tm) >= PADR, acc, 0.0)
        o_ref[...] = acc.astype(out_dtype)

    in_specs = [pl.BlockSpec((tm, K), lambda i: (i, 0)), pl.BlockSpec(w.shape, lambda i: (0, 0))]
    args = [a, w]
    est = 2 * _nbytes((tm, K), a.dtype) + 2 * _nbytes(w.shape, w.dtype) + 3 * _nbytes((tm, N), F32)
    if res is not None:
        in_specs.append(pl.BlockSpec((tm, N), lambda i: (i, 0)))
        args.append(res)
        est += 2 * _nbytes((tm, N), F32)
    body, in_specs, args = _after(dep, body, in_specs, args)
    return pl.pallas_call(
        body, name=name, grid=(R // tm,), in_specs=in_specs,
        out_specs=pl.BlockSpec((tm, N), lambda i: (i, 0)),
        out_shape=jax.ShapeDtypeStruct((R, N), out_dtype),
        compiler_params=_cp(("parallel",), est + 4 * 2 ** 20),
    )(*args)


def _wgrad(a, b, *, slabs_on_cols, tm, tn, name):
    R, Ka = a.shape
    Nb = b.shape[1]
    if slabs_on_cols:
        out_shape = (4, Ka, Nb // 4)
        assert tn == Nb // 4
        out_spec = pl.BlockSpec((None, Ka, tn), lambda j, i: (j, 0, 0))
    else:
        out_shape = (4, Ka // 4, Nb)
        out_spec = pl.BlockSpec((4, Ka // 4, tn), lambda j, i: (0, 0, j))

    def body(a_ref, b_ref, o_ref):
        @pl.when(pl.program_id(1) == 0)
        def _():
            o_ref[...] = jnp.zeros(o_ref.shape, F32)

        acc = _dot(a_ref[...], b_ref[...], TN)
        o_ref[...] += acc.reshape(o_ref.shape)

    in_specs = [pl.BlockSpec((tm, Ka), lambda j, i: (i, 0)), pl.BlockSpec((tm, tn), lambda j, i: (i, j))]
    est = 2 * _nbytes((tm, Ka), a.dtype) + 2 * _nbytes((tm, tn), b.dtype) + 4 * _nbytes((Ka, tn), F32) \
        + _nbytes((tm, Ka), F32)
    return pl.pallas_call(
        body, name=name, grid=(Nb // tn, R // tm), in_specs=in_specs, out_specs=out_spec,
        out_shape=jax.ShapeDtypeStruct(out_shape, F32),
        compiler_params=_cp(("parallel", "arbitrary"), est + 4 * 2 ** 20),
    )(a, b)


def _mm_ln(a, w, xres, g, b, *, tm, dep=None, name):
    R, K = a.shape

    def body(a_ref, w_ref, x_ref, g_ref, b_ref, xo_ref, xh_ref, r_ref, xm_ref):
        z = ALPHA * x_ref[...] + _dot(a_ref[...], w_ref[...], NN)
        mu = jnp.mean(z, axis=-1, keepdims=True)
        zc = z - mu
        var = jnp.mean(zc * zc, axis=-1, keepdims=True)
        r = lax.rsqrt(var + LN_EPS)
        xh = zc * r
        xh_ref[...] = xh
        r_ref[...] = r
        xo = xh * g_ref[...] + b_ref[...]
        xo_ref[...] = xo
        xm_ref[...] = xo.astype(xm_ref.dtype)

    row = lambda i: (i, 0)
    fix = lambda i: (0, 0)
    est = 2 * _nbytes((tm, K), a.dtype) + 2 * _nbytes(w.shape, w.dtype) + 12 * _nbytes((tm, D), F32)
    body, in_specs, args = _after(
        dep, body, [pl.BlockSpec((tm, K), row), pl.BlockSpec(w.shape, fix), pl.BlockSpec((tm, D), row),
                    pl.BlockSpec((1, D), fix), pl.BlockSpec((1, D), fix)],
        [a, w, xres, g.reshape(1, D), b.reshape(1, D)])
    return pl.pallas_call(
        body, name=name, grid=(R // tm,), in_specs=in_specs,
        out_specs=[pl.BlockSpec((tm, D), row), pl.BlockSpec((tm, D), row), pl.BlockSpec((tm, 1), row),
                   pl.BlockSpec((tm, D), row)],
        out_shape=[jax.ShapeDtypeStruct((R, D), F32), jax.ShapeDtypeStruct((R, D), F32),
                   jax.ShapeDtypeStruct((R, 1), F32), jax.ShapeDtypeStruct((R, D), MX)],
        compiler_params=_cp(("parallel",), est + 4 * 2 ** 20),
    )(*args)


def _resident(shape):
    return pl.BlockSpec(shape, lambda i: (0,) * len(shape), pipeline_mode=pl.Buffered(1))


def _ln_bwd_mm(dx, xh, r, g, w, *, tm, tgt=None, dep=None, name):
    R = dx.shape[0]
    N = w.shape[0]
    assert tgt is None or tm == ROW0

    def body(*refs):
        if tgt is None:
            dx_ref, xh_ref, r_ref, g_ref, w_ref, dz_ref, gb_ref, o_ref = refs
        else:
            dx_ref, t_ref, xh_ref, r_ref, g_ref, w_ref, dz_ref, gb_ref, o_ref, sq_ref = refs
        i = pl.program_id(0)

        @pl.when(i == 0)
        def _():
            gb_ref[...] = jnp.zeros(gb_ref.shape, F32)
            if tgt is not None:
                sq_ref[...] = jnp.zeros(sq_ref.shape, F32)

        if tgt is None:
            dxv = dx_ref[...]
        else:
            err = jnp.where(i > 0, dx_ref[...] - t_ref[...], 0.0)
            sq_ref[0:1, :] += jnp.sum(err * err, axis=0, keepdims=True)
            dxv = err / D
        xhv = xh_ref[...]
        dyh = dxv * g_ref[...]
        m1 = jnp.mean(dyh, axis=-1, keepdims=True)
        m2 = jnp.mean(dyh * xhv, axis=-1, keepdims=True)
        dz = r_ref[...] * (dyh - m1 - xhv * m2)
        dz_ref[...] = dz
        gb_ref[0:1, :] += jnp.sum(dxv * xhv, axis=0, keepdims=True)
        gb_ref[1:2, :] += jnp.sum(dxv, axis=0, keepdims=True)
        o_ref[...] = _dot(dz, w_ref[...], NT)

    row = lambda i: (i, 0)
    fix = lambda i: (0, 0)
    in_specs = [pl.BlockSpec((tm, D), row), pl.BlockSpec((tm, D), row), pl.BlockSpec((tm, 1), row),
                pl.BlockSpec((1, D), fix), _resident(w.shape)]
    args = [dx, xh, r, g.reshape(1, D), w]
    out_specs = [pl.BlockSpec((tm, D), row), pl.BlockSpec((SUBLANES, D), fix), pl.BlockSpec((tm, N), row)]
    out_shape = [jax.ShapeDtypeStruct((R, D), F32), jax.ShapeDtypeStruct((SUBLANES, D), F32),
                 jax.ShapeDtypeStruct((R, N), F32)]
    if tgt is not None:
        in_specs.insert(1, pl.BlockSpec((tm, D), lambda i: (jnp.maximum(i - 1, 0), 0)))
        args.insert(1, tgt)
        out_specs.append(pl.BlockSpec((SUBLANES, D), fix))
        out_shape.append(jax.ShapeDtypeStruct((SUBLANES, D), F32))
    body, in_specs, args = _after(dep, body, in_specs, args)
    est = _nbytes(w.shape, w.dtype) + 16 * _nbytes((tm, D), F32) + 4 * _nbytes((tm, N), F32)
    return pl.pallas_call(
        body, name=name, grid=(R // tm,), in_specs=in_specs, out_specs=out_specs, out_shape=out_shape,
        compiler_params=_cp(("arbitrary",), est + 4 * 2 ** 20),
    )(*args)


def _mm_dx_head(dh, w, res, *, dep=None, name):
    R, K = dh.shape
    tm = ROW0

    def body(a_ref, w_ref, r_ref, head_ref, seq_ref):
        i = pl.program_id(0)
        acc = _dot(a_ref[...], w_ref[...], NT) + ALPHA * r_ref[...]

        @pl.when(i == 0)
        def _():
            head_ref[...] = acc

        @pl.when(i > 0)
        def _():
            seq_ref[...] = acc

    row = lambda i: (i, 0)
    body, in_specs, args = _after(
        dep, body, [pl.BlockSpec((tm, K), row), _resident(w.shape), pl.BlockSpec((tm, D), row)], [dh, w, res])
    est = _nbytes(w.shape, w.dtype) + 2 * _nbytes((tm, K), dh.dtype) + 8 * _nbytes((tm, D), F32)
    return pl.pallas_call(
        body, name=name, grid=(R // tm,), in_specs=in_specs,
        out_specs=[pl.BlockSpec((tm, D), lambda i: (0, 0)), pl.BlockSpec((tm, D), lambda i: (jnp.maximum(i - 1, 0), 0))],
        out_shape=[jax.ShapeDtypeStruct((ROW0, D), F32), jax.ShapeDtypeStruct((R - ROW0, D), F32)],
        compiler_params=_cp(("arbitrary",), est + 4 * 2 ** 20),
    )(*args)


def _mm_ffn_fwd(x1, w, taps, bias, *, tm, name):
    R, K = x1.shape
    F2 = w.shape[1]
    F = F2 // 2

    def body(x_ref, w_ref, t_ref, b_ref, up_ref, u_ref, a_ref, carry_ref):
        i = pl.program_id(0)

        @pl.when(i == 0)
        def _():
            carry_ref[...] = jnp.zeros(carry_ref.shape, F32)

        acc = _dot(x_ref[...], w_ref[...], NN)
        acc = jnp.where(_row_ids(i * tm, tm) >= PADR, acc, 0.0)
        up_ref[...] = acc.astype(up_ref.dtype)
        ue = jnp.concatenate([carry_ref[...], acc], axis=0)
        carry_ref[...] = acc[tm - FH:tm]
        u = (t_ref[0:1, :] * pltpu.roll(ue, 2, 0)[FH:] + t_ref[1:2, :] * pltpu.roll(ue, 1, 0)[FH:]
             + t_ref[2:3, :] * acc + b_ref[...])
        u_ref[...] = u
        gate = u[:, :F]
        val = u[:, F:]
        a_ref[...] = (gate * jax.nn.sigmoid(gate) * val).astype(a_ref.dtype)

    row = lambda i: (i, 0)
    est = _nbytes(w.shape, w.dtype) + 2 * _nbytes((tm, K), x1.dtype) + 10 * _nbytes((tm + FH, F2), F32)
    return pl.pallas_call(
        body, name=name, grid=(R // tm,),
        in_specs=[pl.BlockSpec((tm, K), row), _resident(w.shape), _resident((SUBLANES, F2)), _resident((1, F2))],
        out_specs=[pl.BlockSpec((tm, F2), row), pl.BlockSpec((tm, F2), row), pl.BlockSpec((tm, F), row)],
        out_shape=[jax.ShapeDtypeStruct((R, F2), MX), jax.ShapeDtypeStruct((R, F2), F32),
                   jax.ShapeDtypeStruct((R, F), MX)],
        scratch_shapes=[pltpu.VMEM((FH, F2), F32)],
        compiler_params=_cp(("arbitrary",), est + 4 * 2 ** 20),
    )(x1, w, taps, bias)


def _row_ids(start, n):
    return start + lax.broadcasted_iota(jnp.int32, (n, 1), 0)


def _ffn_bwd(da, up, u, taps, *, tb, dep=None, name):
    R, F2 = up.shape
    F = F2 // 2
    nh = tb // FH
    nb = R // tb
    last = R // FH - 1
    m = tb + FH

    def body(da_ref, dan_ref, x_ref, u_ref, un_ref, w_ref, du_ref, acc_ref):
        i = pl.program_id(0)

        @pl.when(i == 0)
        def _():
            acc_ref[...] = jnp.zeros(acc_ref.shape, F32)

        inside = (i < nb - 1).astype(F32)
        for j in range(F // LANES):
            gsl = slice(j * LANES, (j + 1) * LANES)
            vsl = slice(F + j * LANES, F + (j + 1) * LANES)
            gate = jnp.concatenate([u_ref[:, gsl], un_ref[:, gsl]], axis=0)
            val = jnp.concatenate([u_ref[:, vsl], un_ref[:, vsl]], axis=0)
            dae = jnp.concatenate([da_ref[:, gsl], dan_ref[:, gsl] * inside], axis=0)
            sg = jax.nn.sigmoid(gate)
            gs = gate * sg
            for sl, du in ((gsl, dae * val * (sg * (1.0 + gate - gs))), (vsl, dae * gs)):
                du0 = du[0:tb]
                du1 = pltpu.roll(du, m - 1, 0)[0:tb]
                du2 = pltpu.roll(du, m - 2, 0)[0:tb]
                du_ref[:, sl] = (w_ref[2:3, sl] * du0 + w_ref[1:2, sl] * du1 + w_ref[0:1, sl] * du2).astype(du_ref.dtype)
                x = x_ref[:, sl].astype(F32)
                acc_ref[0:1, sl] += jnp.sum(du2 * x, axis=0, keepdims=True)
                acc_ref[1:2, sl] += jnp.sum(du1 * x, axis=0, keepdims=True)
                acc_ref[2:3, sl] += jnp.sum(du0 * x, axis=0, keepdims=True)
                acc_ref[3:4, sl] += jnp.sum(du0, axis=0, keepdims=True)

        @pl.when(i * tb < PADR)
        def _():
            row = _row_ids(i * tb, tb)
            du_ref[...] = jnp.where(row >= PADR, du_ref[...], jnp.zeros((), du_ref.dtype))

    nxt = lambda i: (jnp.minimum(i * nh + nh, last), 0)
    cur = lambda i: (i, 0)
    fix = lambda i: (0, 0)
    est = 12 * _nbytes((tb + 2 * FH, F2), F32)
    body, in_specs, args = _after(
        dep, body, [pl.BlockSpec((tb, F), cur), pl.BlockSpec((FH, F), nxt),
                    pl.BlockSpec((tb, F2), cur), pl.BlockSpec((tb, F2), cur), pl.BlockSpec((FH, F2), nxt),
                    pl.BlockSpec((SUBLANES, F2), fix)], [da, da, up, u, u, taps])
    return pl.pallas_call(
        body, name=name, grid=(R // tb,), in_specs=in_specs,
        out_specs=[pl.BlockSpec((tb, F2), cur), pl.BlockSpec((SUBLANES, F2), fix)],
        out_shape=[jax.ShapeDtypeStruct((R, F2), MX), jax.ShapeDtypeStruct((SUBLANES, F2), F32)],
        compiler_params=_cp(("arbitrary",), est),
    )(*args)


def _hg_consts():
    t = np.arange(CH)[:, None]
    j = np.arange(CH)[None, :]
    low = (j <= t).astype(np.float32)
    blocks = [low, (j > t).astype(np.float32)]
    for m in LEVELS:
        ref = (t // (2 * m)) * 2 * m + m - 1
        blocks.append(low - (j <= ref).astype(np.float32))
    mat = np.concatenate(blocks, axis=0)
    suf = (j >= t).astype(np.float32)
    return jnp.asarray(mat, BF), jnp.asarray(suf, BF)


def _split_dot(mat, x):
    hi = x.astype(BF)
    lo = (x - hi.astype(F32)).astype(BF)
    return (lax.dot_general(mat, hi, NN, preferred_element_type=F32)
            + lax.dot_general(mat, lo, NN, preferred_element_type=F32))


def _lane_select(a2, a4, a8, a16):
    lane = lax.broadcasted_iota(jnp.int32, (1, PW), 1)
    return jnp.where(lane < 64, a2, jnp.where(lane < 128, a4, jnp.where(lane < 192, a8, a16)))


def _pool_count(row):
    win = _lane_select(2.0, 4.0, 8.0, 16.0)
    t1 = jnp.maximum((row - PADR + 1).astype(F32), 1.0)
    return jnp.minimum(t1, win)


def _gates(fz, lb, valid):
    sig = jax.nn.sigmoid(fz)
    f = lb + (1.0 - lb) * sig
    lf = jnp.where(valid, jnp.log(jnp.maximum(f, F_FLOOR)), 0.0)
    kk = jnp.where(valid, (1.0 - lb) * (1.0 - sig), 0.0)
    return sig, f, lf, kk


def _level_masks():
    tt = lax.broadcasted_iota(jnp.int32, (CH, CH), 0)
    ss = lax.broadcasted_iota(jnp.int32, (CH, CH), 1)
    xr = tt ^ ss
    low = tt > ss
    return tt, ss, [(xr >= m) & (xr < 2 * m) & low for m in LEVELS]


def _intra(q, kk, ex, sl, tt, ss, masks):
    a = jnp.where(tt == ss, jnp.sum(q * kk, axis=-1, keepdims=True), 0.0)
    parts = []
    for l in range(len(LEVELS)):
        e = jnp.exp(-jnp.abs(ex[(2 + l) * CH:(3 + l) * CH, sl]))
        ql = (q * e).astype(MX)
        kl = (kk * e).astype(MX)
        a = a + jnp.where(masks[l], _dot(ql, kl, NT), 0.0)
        parts.append((e, ql, kl))
    return a, parts


def _mixer_fwd(h, taps, wbd, ps, gn, lb, mat, *, name):
    R = h.shape[0]
    nb = R // BM
    q4 = BM // HALO

    def body(h_ref, hp_ref, w_ref, wbd_ref, ps_ref, gn_ref, lb_ref, mat_ref, y_ref, o_ref, sp_ref, st_ref):
        i = pl.program_id(0)

        @pl.when(i == 0)
        def _():
            st_ref[...] = jnp.zeros(st_ref.shape, F32)

        row = _row_ids(i * BM, BM)
        valid_b = row >= PADR
        rowe = _row_ids(i * BM - HALO, BM + HALO)
        valide = rowe >= PADR
        se = jnp.concatenate([hp_ref[:, 256:512] * hp_ref[:, 512:768], h_ref[:, 256:512] * h_ref[:, 512:768]], axis=0)
        se = jnp.where(valide, se, 0.0)
        conv = (w_ref[0:1, :] * pltpu.roll(se, 2, 0) + w_ref[1:2, :] * pltpu.roll(se, 1, 0)
                + w_ref[2:3, :] * se)[HALO:]
        y_ref[:, 0:CW] = (h_ref[:, 0:256] * conv).astype(y_ref.dtype)
        ve = jnp.where(valide, jnp.concatenate([hp_ref[:, 2816:3072], h_ref[:, 2816:3072]], axis=0), 0.0)
        s2 = ve + pltpu.roll(ve, 1, 0)
        s4 = s2 + pltpu.roll(s2, 2, 0)
        s8 = s4 + pltpu.roll(s4, 4, 0)
        s16 = s8 + pltpu.roll(s8, 8, 0)
        dp = (_lane_select(s2, s4, s8, s16) / _pool_count(rowe) - ve)[HALO:]
        y_ref[:, CW + HW:D] = (_dot(dp, wbd_ref[...], NN) * ps_ref[...]).astype(y_ref.dtype)
        lbv = lb_ref[...]
        tt, ss, masks = _level_masks()
        for c in range(MB):
            rs = slice(c * CH, (c + 1) * CH)
            _, _, lf, kk = _gates(h_ref[rs, 1280:1792], lbv, valid_b[rs])
            q = h_ref[rs, 768:1280] * SCALE
            ii = h_ref[rs, 1792:2304]
            gz = h_ref[rs, 2304:2816]
            ex = _split_dot(mat_ref[...], lf)
            eg = jnp.exp(ex[0:CH])
            q0 = q * eg
            kr = kk * jnp.exp(ex[CH:2 * CH])
            for hd in range(NH):
                sl = slice(HD * hd, HD * (hd + 1))
                st = st_ref[hd]
                sp_ref[c, hd] = st
                a, _ = _intra(q[:, sl], kk[:, sl], ex, sl, tt, ss, masks)
                o = _dot(q0[:, sl], st, NT) + _dot(a, ii[:, sl], NN)
                st_ref[hd] = st * eg[CH - 1:CH, sl] + _dot(ii[:, sl], kr[:, sl], TN)
                o_ref[rs, sl] = o
                r = lax.rsqrt(jnp.mean(o * o, axis=-1, keepdims=True) + RMS_EPS)
                g_ = gz[:, sl]
                y_ref[rs, CW + HD * hd:CW + HD * (hd + 1)] = (
                    o * r * gn_ref[...] * (g_ * jax.nn.sigmoid(g_))).astype(y_ref.dtype)

    fix = lambda i: (0, 0)
    return pl.pallas_call(
        body, name=name, grid=(nb,),
        in_specs=[pl.BlockSpec((BM, DIN), lambda i: (i, 0)),
                  pl.BlockSpec((HALO, DIN), lambda i: (jnp.maximum(i * q4 - 1, 0), 0)),
                  pl.BlockSpec((SUBLANES, CW), fix), pl.BlockSpec((PW, PW), fix), pl.BlockSpec((1, PW), fix),
                  pl.BlockSpec((1, HD), fix), pl.BlockSpec((1, HW), fix), pl.BlockSpec(mat.shape, fix)],
        out_specs=[pl.BlockSpec((BM, D), lambda i: (i, 0)), pl.BlockSpec((BM, HW), lambda i: (i, 0)),
                   pl.BlockSpec((MB, NH, HD, HD), lambda i: (i, 0, 0, 0))],
        out_shape=[jax.ShapeDtypeStruct((R, D), MX), jax.ShapeDtypeStruct((R, HW), F32),
                   jax.ShapeDtypeStruct((R // CH, NH, HD, HD), F32)],
        scratch_shapes=[pltpu.VMEM((NH, HD, HD), F32)],
        compiler_params=_cp(("arbitrary",), 32 * 2 ** 20),
    )(h, h, taps, wbd, ps, gn, lb, mat)


def _mixer_bwd(h, dy, o, sp, taps, wbd, ps, gn, lb, mat, suf, *, dep=None, name):
    R = h.shape[0]
    nb = R // BM
    q4 = BM // HALO
    lasth = R // HALO - 1

    def body(h_ref, hp_ref, hn_ref, dy_ref, dyn_ref, o_ref, sp_ref, w_ref, wbd_ref, ps_ref, gn_ref, lb_ref,
             mat_ref, suf_ref, dh_ref, dw_ref, dsc_ref, dgn_ref, dlb_ref, dwbd_ref, dst_ref):
        i = pl.program_id(0)
        b = nb - 1 - i

        @pl.when(i == 0)
        def _():
            dst_ref[...] = jnp.zeros(dst_ref.shape, F32)
            dw_ref[...] = jnp.zeros(dw_ref.shape, F32)
            dsc_ref[...] = jnp.zeros(dsc_ref.shape, F32)
            dgn_ref[...] = jnp.zeros(dgn_ref.shape, F32)
            dlb_ref[...] = jnp.zeros(dlb_ref.shape, F32)
            dwbd_ref[...] = jnp.zeros(dwbd_ref.shape, F32)

        ne = BM + 2 * HALO
        nc = BM + HALO
        row = _row_ids(b * BM, BM)
        valid = row >= PADR
        rowe = _row_ids(b * BM - HALO, ne)
        valide = (rowe >= PADR) & (rowe < R)
        rown = rowe[HALO:]
        validn = rown < R

        def cat3(lo, hi):
            return jnp.concatenate([hp_ref[:, lo:hi], h_ref[:, lo:hi], hn_ref[:, lo:hi]], axis=0)

        def catd(lo, hi):
            return jnp.where(validn, jnp.concatenate([dy_ref[:, lo:hi], dyn_ref[:, lo:hi]], axis=0), 0.0)

        w0, w1, w2 = w_ref[0:1, :], w_ref[1:2, :], w_ref[2:3, :]
        cce = cat3(256, 512)
        cve = cat3(512, 768)
        se = jnp.where(valide, cce * cve, 0.0)
        sm2 = pltpu.roll(se, 2, 0)
        sm1 = pltpu.roll(se, 1, 0)
        conv = (w0 * sm2 + w1 * sm1 + w2 * se)[HALO:HALO + BM]
        cbn = jnp.concatenate([h_ref[:, 0:256], hn_ref[:, 0:256]], axis=0)
        dconv = catd(0, CW) * cbn
        ds = w2 * dconv + w1 * pltpu.roll(dconv, nc - 1, 0) + w0 * pltpu.roll(dconv, nc - 2, 0)
        ds = jnp.where(valid, ds[0:BM], 0.0)
        dcv = dconv[0:BM]
        dw_ref[0:1, :] += jnp.sum(dcv * sm2[HALO:HALO + BM], axis=0, keepdims=True)
        dw_ref[1:2, :] += jnp.sum(dcv * sm1[HALO:HALO + BM], axis=0, keepdims=True)
        dw_ref[2:3, :] += jnp.sum(dcv * se[HALO:HALO + BM], axis=0, keepdims=True)
        dh_ref[:, 0:256] = jnp.where(valid, dy_ref[:, 0:CW] * conv, 0.0).astype(dh_ref.dtype)
        dh_ref[:, 256:512] = (ds * h_ref[:, 512:768]).astype(dh_ref.dtype)
        dh_ref[:, 512:768] = (ds * h_ref[:, 256:512]).astype(dh_ref.dtype)

        ve = jnp.where(valide, cat3(2816, 3072), 0.0)
        s2 = ve + pltpu.roll(ve, 1, 0)
        s4 = s2 + pltpu.roll(s2, 2, 0)
        s8 = s4 + pltpu.roll(s4, 4, 0)
        s16 = s8 + pltpu.roll(s8, 8, 0)
        cnte = _pool_count(rowe)
        dp = (_lane_select(s2, s4, s8, s16) / cnte - ve)[HALO:HALO + BM]
        dyp = catd(CW + HW, D)
        pre = _dot(dp, wbd_ref[...], NN)
        dsc_ref[0:1, :] += jnp.sum(dyp[0:BM] * pre, axis=0, keepdims=True)
        dyps = dyp * ps_ref[...]
        dd = _dot(dyps, wbd_ref[...], NT)
        dwbd_ref[...] += _dot(dp, dyps[0:BM], TN)
        e = dd / cnte[HALO:]
        t2 = e + pltpu.roll(e, nc - 1, 0)
        t4 = t2 + pltpu.roll(t2, nc - 2, 0)
        t8 = t4 + pltpu.roll(t4, nc - 4, 0)
        t16 = t8 + pltpu.roll(t8, nc - 8, 0)
        dv = (_lane_select(t2, t4, t8, t16) - dd)[0:BM]
        dh_ref[:, 2816:3072] = jnp.where(valid, dv, 0.0).astype(dh_ref.dtype)

        lbv = lb_ref[...]
        tt, ss, masks = _level_masks()
        gnv = gn_ref[...]
        for c in reversed(range(MB)):
            rs = slice(c * CH, (c + 1) * CH)
            vc = valid[rs]
            sig, f, lf, kk = _gates(h_ref[rs, 1280:1792], lbv, vc)
            q = h_ref[rs, 768:1280] * SCALE
            ii = h_ref[rs, 1792:2304]
            gz = h_ref[rs, 2304:2816]
            ex = _split_dot(mat_ref[...], lf)
            eg = jnp.exp(ex[0:CH])
            egr = jnp.exp(ex[CH:2 * CH])
            q0 = q * eg
            kr = kk * egr
            dqs, dks, dis, dgzs, tails, dbs = [], [], [], [], [], []
            for hd in range(NH):
                sl = slice(HD * hd, HD * (hd + 1))
                ov = o_ref[rs, sl]
                r = lax.rsqrt(jnp.mean(ov * ov, axis=-1, keepdims=True) + RMS_EPS)
                oh = ov * r
                g_ = gz[:, sl]
                sg = jax.nn.sigmoid(g_)
                dyv = dy_ref[rs, CW + HD * hd:CW + HD * (hd + 1)]
                don = dyv * (g_ * sg)
                dgzs.append(dyv * (oh * gnv) * (sg * (1.0 + g_ * (1.0 - sg))))
                dgn_ref[0:1, :] += jnp.sum(don * oh, axis=0, keepdims=True)
                doh = don * gnv
                do = r * (doh - oh * jnp.mean(doh * oh, axis=-1, keepdims=True))
                st = sp_ref[c, hd]
                dst = dst_ref[hd]
                qh, kh, ih = q[:, sl], kk[:, sl], ii[:, sl]
                a, parts = _intra(qh, kh, ex, sl, tt, ss, masks)
                da = jnp.where(tt >= ss, _dot(do, ih, NT), 0.0)
                dis.append(_dot(a, do, TN) + _dot(kr[:, sl], dst, NT))
                q0h = q0[:, sl].astype(MX)
                krh = kr[:, sl].astype(MX)
                dq0 = _dot(do, st, NN)
                dkr = _dot(ih, dst, NN)
                dq = dq0 * eg[:, sl]
                dk = dkr * egr[:, sl]
                kdk = krh.astype(F32) * dkr
                db = q0h.astype(F32) * dq0 - kdk
                tails.append(jnp.sum(kdk, axis=0, keepdims=True)
                             + eg[CH - 1:CH, sl] * jnp.sum(dst * st, axis=0, keepdims=True))
                dga = jnp.sum(jnp.where(tt == ss, da, 0.0), axis=-1, keepdims=True)
                dq = dq + dga * kh
                dk = dk + dga * qh
                for l in range(len(LEVELS)):
                    e_l, ql, kl = parts[l]
                    dpl = jnp.where(masks[l], da, 0.0).astype(MX)
                    dql = _dot(dpl, kl, NN)
                    dkl = _dot(dpl, ql, TN)
                    dq = dq + dql * e_l
                    dk = dk + dkl * e_l
                    db = db + (ql.astype(F32) * dql - kl.astype(F32) * dkl)
                dst_ref[hd] = dst * eg[CH - 1:CH, sl] + _dot(do, q0h, TN)
                dqs.append(dq)
                dks.append(dk)
                dbs.append(db)
            dq = jnp.concatenate(dqs, axis=1)
            dk = jnp.concatenate(dks, axis=1)
            db = jnp.concatenate(dbs, axis=1)
            dlf = _split_dot(suf_ref[...], db) + jnp.concatenate(tails, axis=1)
            t = jnp.where(vc, dlf * jnp.where(f > F_FLOOR, 1.0 / f, 0.0) - dk, 0.0)
            dlb_ref[0:1, :] += jnp.sum(t * (1.0 - sig), axis=0, keepdims=True)
            dh_ref[rs, 768:1280] = jnp.where(vc, dq * SCALE, 0.0).astype(dh_ref.dtype)
            dh_ref[rs, 1280:1792] = (t * (1.0 - lbv) * (sig * (1.0 - sig))).astype(dh_ref.dtype)
            dh_ref[rs, 1792:2304] = jnp.where(vc, jnp.concatenate(dis, axis=1), 0.0).astype(dh_ref.dtype)
            dh_ref[rs, 2304:2816] = jnp.where(vc, jnp.concatenate(dgzs, axis=1), 0.0).astype(dh_ref.dtype)

    cur = lambda i: (nb - 1 - i, 0)
    prev = lambda i: (jnp.maximum((nb - 1 - i) * q4 - 1, 0), 0)
    nxt = lambda i: (jnp.minimum((nb - 1 - i) * q4 + q4, lasth), 0)
    fix = lambda i: (0, 0)
    body, in_specs, args = _after(
        dep, body,
        [pl.BlockSpec((BM, DIN), cur), pl.BlockSpec((HALO, DIN), prev), pl.BlockSpec((HALO, DIN), nxt),
         pl.BlockSpec((BM, D), cur), pl.BlockSpec((HALO, D), nxt), pl.BlockSpec((BM, HW), cur),
         pl.BlockSpec((MB, NH, HD, HD), lambda i: (nb - 1 - i, 0, 0, 0)),
         pl.BlockSpec((SUBLANES, CW), fix), pl.BlockSpec((PW, PW), fix), pl.BlockSpec((1, PW), fix),
         pl.BlockSpec((1, HD), fix), pl.BlockSpec((1, HW), fix), pl.BlockSpec(mat.shape, fix),
         pl.BlockSpec(suf.shape, fix)],
        [h, h, h, dy, dy, o, sp, taps, wbd, ps, gn, lb, mat, suf])
    return pl.pallas_call(
        body, name=name, grid=(nb,), in_specs=in_specs,
        out_specs=[pl.BlockSpec((BM, DIN), cur), pl.BlockSpec((SUBLANES, CW), fix), pl.BlockSpec((SUBLANES, PW), fix),
                   pl.BlockSpec((SUBLANES, HD), fix), pl.BlockSpec((SUBLANES, HW), fix), pl.BlockSpec((PW, PW), fix)],
        out_shape=[jax.ShapeDtypeStruct((R, DIN), MX), jax.ShapeDtypeStruct((SUBLANES, CW), F32),
                   jax.ShapeDtypeStruct((SUBLANES, PW), F32), jax.ShapeDtypeStruct((SUBLANES, HD), F32),
                   jax.ShapeDtypeStruct((SUBLANES, HW), F32), jax.ShapeDtypeStruct((PW, PW), F32)],
        scratch_shapes=[pltpu.VMEM((NH, HD, HD), F32)],
        compiler_params=_cp(("arbitrary",), 40 * 2 ** 20),
    )(*args)


def _sum_slots(recv, *, name):
    S, L, rows, cols = recv.shape
    tr = _row_tile(rows, S * cols * 4, 6 * 2 ** 20)

    def body(r_ref, o_ref):
        acc = r_ref[0]
        for s in range(1, S):
            acc = acc + r_ref[s]
        o_ref[...] = acc

    return pl.pallas_call(
        body, name=name, grid=(L, rows // tr),
        in_specs=[pl.BlockSpec((S, None, tr, cols), lambda l, i: (0, l, i, 0))],
        out_specs=pl.BlockSpec((None, tr, cols), lambda l, i: (l, i, 0)),
        out_shape=jax.ShapeDtypeStruct((L, rows, cols), F32),
        compiler_params=_cp(("parallel", "parallel"), 4 * S * tr * cols * 4),
    )(recv)


def _sum_own(recv, own, chip, *, name):
    S, rows, cols = recv.shape
    tr = _row_tile(rows, S * cols * 4, 6 * 2 ** 20)

    def body(me_ref, r_ref, o_ref, out_ref):
        me = me_ref[0]
        acc = None
        for s in range(S):
            t = jnp.where(me == s, o_ref[...], r_ref[s])
            acc = t if acc is None else acc + t
        out_ref[...] = acc

    grid_spec = pltpu.PrefetchScalarGridSpec(
        num_scalar_prefetch=1, grid=(rows // tr,),
        in_specs=[pl.BlockSpec((S, tr, cols), lambda i, me: (0, i, 0)),
                  pl.BlockSpec((None, tr, cols), lambda i, me: (me[0], i, 0))],
        out_specs=pl.BlockSpec((tr, cols), lambda i, me: (i, 0)))
    return pl.pallas_call(
        body, name=name, grid_spec=grid_spec, out_shape=jax.ShapeDtypeStruct((rows, cols), F32),
        compiler_params=_cp(("parallel",), 5 * S * tr * cols * 4),
    )(chip.reshape(1).astype(jnp.int32), recv, own)


def _adamw(w, m, v, ga, gb, *, layer, prev, name):
    L, rows, cols = w.shape
    tr = _row_tile(rows, cols * 4, 2 ** 20)
    two = gb is not None
    nin = 5 if two else 4

    def body(*refs):
        w_ref, m_ref, v_ref, a_ref = refs[:4]
        g_ref, d_ref, m2_ref, v2_ref = refs[-4:]
        g = a_ref[...] + refs[4][...] if two else a_ref[...]
        m2 = ADAM_B1 * m_ref[...] + (1.0 - ADAM_B1) * g
        v2 = ADAM_B2 * v_ref[...] + (1.0 - ADAM_B2) * (g * g)
        m_hat = m2 / (1.0 - ADAM_B1 ** ADAM_STEP)
        v_hat = v2 / (1.0 - ADAM_B2 ** ADAM_STEP)
        g_ref[...] = g
        d_ref[...] = -ADAM_LR * (m_hat / (jnp.sqrt(v_hat) + ADAM_EPS) + ADAM_WD * w_ref[...])
        m2_ref[...] = m2
        v2_ref[...] = v2

    spec = pl.BlockSpec((None, tr, cols), lambda i: (layer, i, 0))
    gspec = pl.BlockSpec((tr, cols), lambda i: (i, 0))
    args = [w, m, v, ga] + ([gb] if two else [])
    in_specs = [spec] * 3 + [gspec] * (nin - 3)
    aliases = {}
    if prev is not None:
        args += list(prev)
        in_specs += [ANY] * 4
        aliases = {nin + j: j for j in range(4)}
    sd = jax.ShapeDtypeStruct((L, rows, cols), F32)
    return pl.pallas_call(
        body, name=name, grid=(rows // tr,), in_specs=in_specs, out_specs=[spec] * 4,
        out_shape=[sd] * 4, input_output_aliases=aliases,
        compiler_params=_cp(("parallel",), 24 * tr * cols * 4),
    )(*args)


def _exchange(arrays, *, flips, n_slots, slot_of, scatter, self_copy, dep=None, name):
    n = len(arrays)
    nf = len(flips)
    deps = [] if dep is None else [dep]
    nd = len(deps)
    out_shapes = [jax.ShapeDtypeStruct(a.shape if scatter else (n_slots,) + a.shape, a.dtype) for a in arrays]

    def body(*refs):
        ins, outs = refs[:n], refs[n + nd:2 * n + nd]
        send_sems, recv_sems, loc_sems = refs[2 * n + nd:]
        x, y, c = lax.axis_index("x"), lax.axis_index("y"), lax.axis_index("c")
        me = slot_of(x, y, c)
        peers = [(1 - x if fx else x, 1 - y if fy else y, 1 - c if fc else c) for fx, fy, fc in flips]
        local, remote = [], []
        for a in range(n):
            if self_copy:
                lc = pltpu.make_async_copy(ins[a].at[me] if scatter else ins[a], outs[a].at[me], loc_sems.at[a])
                lc.start()
                local.append(lc)
            for k, p in enumerate(peers):
                src = ins[a].at[slot_of(*p)] if scatter else ins[a]
                cp = pltpu.make_async_remote_copy(
                    src_ref=src, dst_ref=outs[a].at[me], send_sem=send_sems.at[a, k], recv_sem=recv_sems.at[a, k],
                    device_id=p, device_id_type=MESH)
                cp.start()
                remote.append(cp)
        for a in range(n):
            for k, p in enumerate(peers):
                src = ins[a].at[slot_of(*p)] if scatter else ins[a]
                pltpu.make_async_remote_copy(
                    src_ref=src, dst_ref=outs[a].at[slot_of(*p)], send_sem=send_sems.at[a, k],
                    recv_sem=recv_sems.at[a, k], device_id=p, device_id_type=MESH).wait_recv()
        for cp in remote:
            cp.wait_send()
        for lc in local:
            lc.wait()

    return pl.pallas_call(
        body, name=name, in_specs=[ANY] * (n + nd), out_specs=[ANY] * n, out_shape=out_shapes,
        scratch_shapes=[pltpu.SemaphoreType.DMA((n, nf)), pltpu.SemaphoreType.DMA((n, nf)),
                        pltpu.SemaphoreType.DMA((n,))],
        compiler_params=pltpu.CompilerParams(has_side_effects=True),
    )(*arrays, *deps)


ALL_FLIPS = [(fx, fy, fc) for fx in (0, 1) for fy in (0, 1) for fc in (0, 1) if fx or fy or fc]


def _zero_slot(x, y, c):
    return 0


HBM_SPEC = pl.BlockSpec(memory_space=pltpu.HBM)
SEM_SPEC = pl.BlockSpec(memory_space=pltpu.SEMAPHORE)


N_SLOTS = dict(chips=4, sibling=1, all=8)


def _peers(to, x, y, c):
    if to == "sibling":
        return [(x, y, 1 - c)]
    if to == "chips":
        return [(1 - x, y, c), (x, 1 - y, c), (1 - x, 1 - y, c)]
    return [(1 - x if fx else x, 1 - y if fy else y, 1 - c if fc else c) for fx, fy, fc in ALL_FLIPS]


def _slot(to, x, y, c):
    return {"chips": 2 * x + y, "sibling": 0, "all": 4 * x + 2 * y + c}[to]


def _send_start(srcs, *, scatter, to="chips", dep=None, name):
    n = len(srcs)
    npeer = N_SLOTS[to] - (to != "sibling")
    nc = n * npeer
    srcs = [pltpu.with_memory_space_constraint(s, pltpu.HBM) for s in srcs]
    land_shapes = [s.shape if scatter else (N_SLOTS[to],) + s.shape for s in srcs]
    lands = [pltpu.with_memory_space_constraint(lax.empty(sh, s.dtype), pltpu.HBM) for sh, s in zip(land_shapes, srcs)]

    deps = [] if dep is None else [dep]
    nd = len(deps)

    def body(*refs):
        ins, lnd = refs[:n], refs[n:2 * n]
        send_sems, recv_sems = refs[2 * n + nd:2 * n + nd + nc], refs[2 * n + nd + nc:2 * n + nd + 2 * nc]
        token = refs[-1]
        x, y, c = lax.axis_index("x"), lax.axis_index("y"), lax.axis_index("c")
        me = _slot(to, x, y, c)
        for a in range(n):
            for k, p in enumerate(_peers(to, x, y, c)):
                src = ins[a].at[_slot(to, *p)] if scatter else ins[a]
                j = a * npeer + k
                pltpu.make_async_remote_copy(
                    src_ref=src, dst_ref=lnd[a].at[me], send_sem=send_sems[j], recv_sem=recv_sems[j],
                    device_id=p, device_id_type=MESH).start()
        token[...] = jnp.zeros(token.shape, token.dtype)

    sem = pltpu.SemaphoreType.DMA(())
    outs = pl.pallas_call(
        body, name=name,
        out_shape=(*[sem] * (2 * nc), *[pltpu.HBM(s.shape, s.dtype) for s in srcs],
                   *[pltpu.HBM(sh, s.dtype) for sh, s in zip(land_shapes, srcs)],
                   jax.ShapeDtypeStruct((SUBLANES, LANES), F32)),
        in_specs=[HBM_SPEC] * (2 * n) + [ANY] * nd,
        out_specs=(*[SEM_SPEC] * (2 * nc), *[HBM_SPEC] * (2 * n), pl.BlockSpec(memory_space=pltpu.VMEM)),
        input_output_aliases={i: 2 * nc + i for i in range(2 * n)},
        compiler_params=pltpu.CompilerParams(has_side_effects=pltpu.SideEffectType.DATAFLOW_SIDE_EFFECTING),
    )(*srcs, *lands, *deps)
    return dict(sems=list(outs[:2 * nc]), srcs=list(outs[2 * nc:2 * nc + n]),
                lands=list(outs[2 * nc + n:2 * nc + 2 * n]), token=outs[-1], to=to)


def _send_wait(h, *, scatter, after, name):
    n = len(h["srcs"])
    to = h["to"]
    nc = len(h["sems"]) // 2

    def body(*refs):
        ins, lnd = refs[:n], refs[n:2 * n]
        send_sems, recv_sems = refs[2 * n:2 * n + nc], refs[2 * n + nc:2 * n + 2 * nc]
        x, y, c = lax.axis_index("x"), lax.axis_index("y"), lax.axis_index("c")
        for a in range(n):
            for k, p in enumerate(_peers(to, x, y, c)):
                slot = _slot(to, *p)
                j = a * (nc // n) + k
                cp = pltpu.make_async_remote_copy(
                    src_ref=ins[a].at[slot] if scatter else ins[a], dst_ref=lnd[a].at[slot],
                    send_sem=send_sems[j], recv_sem=recv_sems[j], device_id=p, device_id_type=MESH)
                cp.wait_send()
                cp.wait_recv()

    thru = h["srcs"] + h["lands"]
    outs = pl.pallas_call(
        body, name=name, out_shape=tuple(pltpu.HBM(t.shape, t.dtype) for t in thru),
        in_specs=[HBM_SPEC] * (2 * n) + [SEM_SPEC] * (2 * nc) + [ANY] * len(after),
        out_specs=tuple([HBM_SPEC] * (2 * n)),
        input_output_aliases={i: i for i in range(2 * n)},
        compiler_params=pltpu.CompilerParams(has_side_effects=pltpu.SideEffectType.DATAFLOW_SIDE_EFFECTING),
    )(*thru, *h["sems"], *after)
    return list(outs[:n]), list(outs[n:])


def _assemble(land, own, chip, axis):
    return jnp.concatenate([jnp.where(chip == k, own, land[k]) for k in range(4)], axis=axis)


def _pack(arrs):
    flat = jnp.concatenate([a.reshape(-1).astype(F32) for a in arrs])
    tile = SUBLANES * LANES
    pad = (-flat.shape[0]) % tile
    return jnp.pad(flat, (0, pad)).reshape(-1, LANES)


def _unpack(buf, shapes):
    flat = buf.reshape(-1)
    out, off = [], 0
    for s in shapes:
        n = int(np.prod(s))
        out.append(flat[off:off + n].reshape(s))
        off += n
    return out


def _lower_bounds(hg_lower_bounds):
    p = jax.nn.softmax(hg_lower_bounds.astype(F32), axis=0)
    return jnp.cumsum(p, axis=0) - p[0]


def kernel(x, meta_tokens, hg_lower_bounds, w_in, w_conv, w_pool, pool_scale, hg_norm_g, w_o, ln1_g, ln1_b, w_up, w_ffn_conv, b_ffn_conv, w_down, ln2_g, ln2_b, loss_target, m_meta_tokens, m_hg_lower_bounds, m_w_in, m_w_conv, m_w_pool, m_pool_scale, m_hg_norm_g, m_w_o, m_ln1_g, m_ln1_b, m_w_up, m_w_ffn_conv, m_b_ffn_conv, m_w_down, m_ln2_g, m_ln2_b, v_meta_tokens, v_hg_lower_bounds, v_w_in, v_w_conv, v_w_pool, v_pool_scale, v_hg_norm_g, v_w_o, v_ln1_g, v_ln1_b, v_w_up, v_w_ffn_conv, v_b_ffn_conv, v_w_down, v_ln2_g, v_ln2_b):
    S = x.shape[1]
    R = S + ROW0
    Fq = w_down.shape[1]
    F = 4 * Fq
    F2 = 2 * F
    F2q = w_up.shape[2]
    assert x.shape == (1, S, D) and R % 384 == 0 and S % ROW0 == 0
    chip = 2 * lax.axis_index("x") + lax.axis_index("y")
    tm = 384
    tm_w = max(t for t in range(SUBLANES, 2113, SUBLANES) if R % t == 0)
    tb_ffn = 256

    small_shapes = [(N_META, D // 4), (DEPTH, CW // 4, 3), (DEPTH, F2q, 3)]
    wb_in, wb_o, wb_up, wb_down = (w.astype(MX) for w in (w_in, w_o, w_up, w_down))
    h_s = _send_start([_pack([meta_tokens, w_conv, w_ffn_conv])], scatter=False, name="gather_small_start")
    h_a = _send_start([wb_in[0]], scatter=False, dep=h_s["token"], name="gather_a_start")
    (own_small,), (l_small,) = _send_wait(h_s, scatter=False, after=[h_a["token"]], name="gather_small_wait")
    sm = [_unpack(jnp.where(chip == k, own_small, l_small[k]), small_shapes) for k in range(4)]
    meta_full = jnp.concatenate([sm[k][0] for k in range(4)], axis=1)
    X = jnp.concatenate([jnp.zeros((PADR, D), F32), meta_full, x[0]], axis=0)
    (own_in,), (l_in,) = _send_wait(h_a, scatter=False, after=[X, wb_o, wb_up, wb_down], name="gather_a_wait")
    h_b = _send_start([wb_o[0], wb_up[0], wb_down[0]], scatter=False, dep=l_in, name="gather_b_start")
    Win, Wo, Wup, Wdown = {}, {}, {}, {}
    Win[0] = _assemble(l_in, own_in, chip, 1)
    wconv_full = jnp.concatenate([sm[k][1] for k in range(4)], axis=1)
    wffn_full = jnp.concatenate([sm[k][2] for k in range(4)], axis=1)
    taps_c = jnp.pad(wconv_full.transpose(0, 2, 1), ((0, 0), (0, SUBLANES - 3), (0, 0)))
    taps_f = jnp.pad(wffn_full.transpose(0, 2, 1), ((0, 0), (0, SUBLANES - 3), (0, 0)))
    wbd = jnp.stack([jax.scipy.linalg.block_diag(*[w_pool[l, g] for g in range(4)]) for l in range(DEPTH)]).astype(MX)
    lbs, lbs_vjp = jax.vjp(_lower_bounds, hg_lower_bounds)
    mat, suf = _hg_consts()

    def fwd_mixer(l, X, dep=None):
        h = _mm(X, Win[l], tm=tm, dep=dep, name=f"mm_in_{l}")
        y, o, sp = _mixer_fwd(h, taps_c[l], wbd[l], pool_scale[l].reshape(1, PW), hg_norm_g[l].reshape(1, HD),
                              lbs[l].reshape(1, HW), mat, name=f"mixer_fwd_{l}")
        return h, y, o, sp

    def fwd_rest(l, X, h, y, o, sp, dep=None):
        x1, xh1, r1, x1m = _mm_ln(y, Wo[l], X, ln1_g[l], ln1_b[l], tm=tm, dep=dep, name=f"mm_o_ln_{l}")
        up, u, a = _mm_ffn_fwd(x1m, Wup[l], taps_f[l], b_ffn_conv[l].reshape(1, F2), tm=tm // 2,
                               name=f"mm_up_ffn_{l}")
        x2, xh2, r2, _ = _mm_ln(a, Wdown[l], x1, ln2_g[l], ln2_b[l], tm=tm, name=f"mm_down_ln_{l}")
        return (X, h, y, o, sp, x1m, xh1, r1, up, u, a, xh2, r2), x2

    h, y, o, sp = fwd_mixer(0, X, dep=h_b["token"])
    (own_o, own_up, own_down), (l_o, l_up, l_down) = _send_wait(h_b, scatter=False, after=[y], name="gather_b_wait")
    Wo[0], Wup[0], Wdown[0] = (_assemble(l_o, own_o, chip, 0), _assemble(l_up, own_up, chip, 1),
                               _assemble(l_down, own_down, chip, 0))
    h_c = _send_start([wb_in[1], wb_o[1], wb_up[1], wb_down[1]], scatter=False, dep=l_o, name="gather_c_start")
    saved0, X1 = fwd_rest(0, X, h, y, o, sp, dep=h_c["token"])
    own_c, l_c = _send_wait(h_c, scatter=False, after=[X1], name="gather_c_wait")
    Win[1], Wo[1] = _assemble(l_c[0], own_c[0], chip, 1), _assemble(l_c[1], own_c[1], chip, 0)
    Wup[1], Wdown[1] = _assemble(l_c[2], own_c[2], chip, 1), _assemble(l_c[3], own_c[3], chip, 0)
    saved1, X2 = fwd_rest(1, X1, *fwd_mixer(1, X1))
    saved = [saved0, saved1]

    dxo = X2

    sc = {}

    def scatter(nm, l, g, dep=None):
        sc[nm, l] = _send_start([g], scatter=True, dep=dep, name=f"scatter_{nm}_{l}_start")
        return sc[nm, l]["token"]

    tok = None

    small_g = [None] * DEPTH
    for l in reversed(range(DEPTH)):
        X, h, y, o, sp, x1m, xh1, r1, up, u, a, xh2, r2 = saved[l]
        if l == DEPTH - 1:
            dz2, gb2, da, sq = _ln_bwd_mm(dxo, xh2, r2, ln2_g[l], Wdown[l], tm=ROW0, tgt=loss_target[0],
                                          name=f"loss_ln2_bwd_da_{l}")
            loss = lax.psum(0.5 * jnp.sum(sq[0]) / D, ("x", "y", "c"))
        else:
            dz2, gb2, da = _ln_bwd_mm(dxo, xh2, r2, ln2_g[l], Wdown[l], tm=tm, dep=tok, name=f"ln2_bwd_da_{l}")
        tok = scatter("w_down", l, _wgrad(a, dz2, slabs_on_cols=False, tm=tm_w, tn=512, name=f"wgrad_down_{l}"))
        dup, facc = _ffn_bwd(da, up, u, taps_f[l], tb=tb_ffn, dep=tok, name=f"ffn_bwd_{l}")
        dx1 = _mm(dup, Wup[l], nt=True, res=dz2, tm=tm, name=f"mm_dx1_{l}")
        tok = scatter("w_up", l, _wgrad(x1m, dup, slabs_on_cols=True, tm=tm_w, tn=F2q, name=f"wgrad_up_{l}"))
        dz1, gb1, dym = _ln_bwd_mm(dx1, xh1, r1, ln1_g[l], Wo[l], tm=tm, dep=tok, name=f"ln1_bwd_dym_{l}")
        tok = scatter("w_o", l, _wgrad(y, dz1, slabs_on_cols=False, tm=tm_w, tn=512, name=f"wgrad_o_{l}"))
        dh, dwc, dsc, dgn, dlb, dwbd = _mixer_bwd(
            h, dym, o, sp, taps_c[l], wbd[l], pool_scale[l].reshape(1, PW), hg_norm_g[l].reshape(1, HD),
            lbs[l].reshape(1, HW), mat, suf, dep=tok, name=f"mixer_bwd_{l}")
        tok = scatter("w_in", l, _wgrad(X, dh, slabs_on_cols=True, tm=tm_w, tn=DIN // 4, name=f"wgrad_in_{l}"))
        if l == 0:
            dx_head, dx_seq = _mm_dx_head(dh, Win[l], dz1, dep=tok, name=f"mm_dx_{l}")
        else:
            dxo = _mm(dh, Win[l], nt=True, res=dz1, tm=tm, dep=tok, name=f"mm_dx_{l}")
        small_g[l] = dict(
            lbs=dlb[0], w_conv=dwc[0:3].T, w_pool=jnp.stack([dwbd[64 * g:64 * g + 64, 64 * g:64 * g + 64] for g in range(4)]),
            pool_scale=dsc[0], hg_norm_g=dgn[0], ln1_g=gb1[0], ln1_b=gb1[1], w_ffn_conv=facc[0:3].T,
            b_ffn_conv=facc[3], ln2_g=gb2[0], ln2_b=gb2[1])
    grad_x = dx_seq[None]

    sg_names = ["lbs", "w_conv", "w_pool", "pool_scale", "hg_norm_g", "ln1_g", "ln1_b", "w_ffn_conv",
                "b_ffn_conv", "ln2_g", "ln2_b"]
    sg_list = [dx_head[PADR:ROW0]] + [jnp.stack([small_g[l][nm] for l in range(DEPTH)]) for nm in sg_names]
    sg_shapes = [a.shape for a in sg_list]
    packed = _pack(sg_list)
    h_g = _send_start([packed], scatter=False, to="all", name="gather_small_grads_start")

    big_w = dict(w_in=(w_in, m_w_in, v_w_in), w_o=(w_o, m_w_o, v_w_o), w_up=(w_up, m_w_up, v_w_up),
                 w_down=(w_down, m_w_down, v_w_down))
    groups = {1: ("w_down", "w_up", "w_o", "w_in"), 0: ("w_down", "w_up", "w_o")}
    swaps, tokc = {}, h_g["token"]
    for l in reversed(range(DEPTH)):
        part = []
        for nm in groups[l]:
            after = [dx_seq, tokc]
            (own,), (recv,) = _send_wait(sc[nm, l], scatter=True, after=after, name=f"scatter_{nm}_{l}_wait")
            part.append(_sum_own(recv, own, chip, name=f"sum_{nm}_{l}"))
        swaps[l] = _send_start(part, scatter=False, to="sibling", name=f"swap_cores_{l}_start")
        tokc = swaps[l]["token"]

    (own_small_g,), (l_small_g,) = _send_wait(h_g, scatter=False, after=[tokc], name="gather_small_grads_wait")
    dev = 2 * chip + lax.axis_index("c")
    gathered = jnp.where((jnp.arange(8) == dev)[:, None, None], own_small_g[None], l_small_g)
    total = _sum_slots(gathered[:, None], name="sum_small_grads")[0]
    tot = dict(zip(["meta_tokens"] + sg_names, _unpack(total, sg_shapes)))
    (g_hg,) = lbs_vjp(tot["lbs"])
    small_grads = dict(
        meta_tokens=lax.dynamic_slice_in_dim(tot["meta_tokens"], chip * (D // 4), D // 4, axis=1),
        hg_lower_bounds=g_hg,
        w_conv=lax.dynamic_slice_in_dim(tot["w_conv"], chip * (CW // 4), CW // 4, axis=1),
        w_pool=tot["w_pool"], pool_scale=tot["pool_scale"], hg_norm_g=tot["hg_norm_g"],
        ln1_g=tot["ln1_g"], ln1_b=tot["ln1_b"],
        w_ffn_conv=lax.dynamic_slice_in_dim(tot["w_ffn_conv"], chip * F2q, F2q, axis=1),
        b_ffn_conv=tot["b_ffn_conv"], ln2_g=tot["ln2_g"], ln2_b=tot["ln2_b"])
    small_w = dict(meta_tokens=(meta_tokens, m_meta_tokens, v_meta_tokens),
                   hg_lower_bounds=(hg_lower_bounds, m_hg_lower_bounds, v_hg_lower_bounds),
                   w_conv=(w_conv, m_w_conv, v_w_conv), w_pool=(w_pool, m_w_pool, v_w_pool),
                   pool_scale=(pool_scale, m_pool_scale, v_pool_scale), hg_norm_g=(hg_norm_g, m_hg_norm_g, v_hg_norm_g),
                   ln1_g=(ln1_g, m_ln1_g, v_ln1_g), ln1_b=(ln1_b, m_ln1_b, v_ln1_b),
                   w_ffn_conv=(w_ffn_conv, m_w_ffn_conv, v_w_ffn_conv), b_ffn_conv=(b_ffn_conv, m_b_ffn_conv, v_b_ffn_conv),
                   ln2_g=(ln2_g, m_ln2_g, v_ln2_g), ln2_b=(ln2_b, m_ln2_b, v_ln2_b))
    names_s = list(small_w)
    shapes_s = [small_w[nm][0].shape for nm in names_s]
    pk = [_pack([small_w[nm][j] for nm in names_s])[None] for j in range(3)]
    pg = _pack([small_grads[nm] for nm in names_s])[None]
    outs_s = _adamw(pk[0], pk[1], pk[2], pg[0], None, layer=0, prev=None, name="adamw_small")
    small = {nm: [] for nm in names_s}
    for j in range(4):
        for nm, val in zip(names_s, _unpack(outs_s[j][0], shapes_s)):
            small[nm].append(val)

    big = {nm: None for nm in big_w}
    after = [outs_s[0]]
    for l in reversed(range(DEPTH)):
        mine, theirs = _send_wait(swaps[l], scatter=False, after=after, name=f"swap_cores_{l}_wait")
        for k, nm in enumerate(groups[l]):
            w, m, v = big_w[nm]
            big[nm] = _adamw(w, m, v, mine[k], theirs[k][0], layer=l, prev=big[nm], name=f"adamw_{nm}_{l}")
        after = [big[nm][0] for nm in groups[l]]
    (own,), (recv,) = _send_wait(sc["w_in", 0], scatter=True, after=after, name="scatter_w_in_0_wait")
    part = _sum_own(recv, own, chip, name="sum_w_in_0")
    (sib,) = _exchange([part], flips=[(0, 0, 1)], n_slots=1, slot_of=_zero_slot, scatter=False, self_copy=False,
                       name="swap_cores_0_in")
    big["w_in"] = _adamw(*big_w["w_in"], part, sib[0], layer=0, prev=big["w_in"], name="adamw_w_in_0")

    order = ["meta_tokens", "hg_lower_bounds", "w_in", "w_conv", "w_pool", "pool_scale", "hg_norm_g", "w_o",
             "ln1_g", "ln1_b", "w_up", "w_ffn_conv", "b_ffn_conv", "w_down", "ln2_g", "ln2_b"]
    res = {nm: (big[nm] if nm in big else small[nm]) for nm in order}
    outs = [loss, grad_x]
    for j in range(4):
        outs += [res[nm][j] for nm in order]
    return tuple(outs)
```
